```python
import jax, jax.numpy as jnp
from jax import lax
import numpy as np

D_MODEL = 2048
BATCH = 8
SEQ = 4096
DEPTH = 2

CHUNK = 64
D_RNN = 2048
LRU_BLOCKS = 8
LRU_BLOCK_W = D_RNN // LRU_BLOCKS
LRU_C = 8.0
CONV_A_WIDTH = 4
D_CONV = 2048
CONV_B_WIDTH = 3
D_FF = 4 * D_MODEL
EPS = 1e-6

SPLIT_SIZES = (D_RNN, D_RNN, D_CONV, D_CONV, D_CONV, D_MODEL, D_MODEL)
SPLIT_POINTS = tuple(int(v) for v in np.cumsum(SPLIT_SIZES)[:-1])
N_IN = int(sum(SPLIT_SIZES))

kernel_name = "hybrid_rglru_shortconv_gated_trunk"


def _rmsnorm(x, g):
    xf = x.astype(jnp.float32)
    y = xf * lax.rsqrt(jnp.mean(xf * xf, axis=-1, keepdims=True) + EPS)
    return (y * g.astype(jnp.float32)).astype(x.dtype)


def _causal_dwconv(x, w):
    k, c = w.shape
    return lax.conv_general_dilated(
        x, w[:, None, :].astype(x.dtype), window_strides=(1,), padding=[(k - 1, 0)],
        dimension_numbers=("NWC", "WIO", "NWC"), feature_group_count=c)


def _lru_combine(left, right):
    a_l, b_l = left
    a_r, b_r = right
    return a_l * a_r, a_r * b_l + b_r


def _rg_lru(x, wr, br, wi, bi, lam):
    bsz, s, c = x.shape
    xb = x.reshape(bsz, s, LRU_BLOCKS, LRU_BLOCK_W)
    r = jax.nn.sigmoid(jnp.einsum("bsnh,nhk->bsnk", xb, wr) + br).reshape(bsz, s, c)
    i = jax.nn.sigmoid(jnp.einsum("bsnh,nhk->bsnk", xb, wi) + bi).reshape(bsz, s, c)
    log_a = -LRU_C * r.astype(jnp.float32) * jax.nn.softplus(-lam.astype(jnp.float32))
    a = jnp.exp(log_a)
    mult = jnp.sqrt(-jnp.expm1(2.0 * log_a))
    b = mult * (i * x).astype(jnp.float32)
    _, h = lax.associative_scan(_lru_combine, (a, b), axis=1)
    return h.astype(x.dtype)


def _layer(x, g1, w_in, b_in, conv_a_w, conv_a_b, lru_wr, lru_br, lru_wi, lru_bi, lru_lam,
           conv_b_w, w_pa, w_pb, w_o, g2, w_mlp1, w_mlp2):
    h = _rmsnorm(x, g1)
    z = jnp.einsum("bsd,dn->bsn", h, w_in) + b_in
    xa, ya, cb, cc, cx, ga, gb = jnp.split(z, SPLIT_POINTS, axis=-1)
    xa = _causal_dwconv(xa, conv_a_w) + conv_a_b
    xa = _rg_lru(xa, lru_wr, lru_br, lru_wi, lru_bi, lru_lam)
    out_a = jnp.einsum("bsc,cd->bsd", xa * jax.nn.gelu(ya), w_pa)
    out_b = jnp.einsum("bsc,cd->bsd", cb * _causal_dwconv(cc * cx, conv_b_w), w_pb)
    merged = jax.nn.sigmoid(ga) * out_a + jax.nn.sigmoid(gb) * out_b
    x = x + jnp.einsum("bsd,de->bse", merged, w_o)
    h2 = _rmsnorm(x, g2)
    u = jnp.square(jax.nn.relu(jnp.einsum("bsd,df->bsf", h2, w_mlp1)))
    return x + jnp.einsum("bsf,fd->bsd", u, w_mlp2)


def _fwd_setup_inputs(seed: int = 0) -> dict:
    key = jax.random.key(seed)
    ks = jax.random.split(key, 24)
    f32 = jnp.float32
    L = DEPTH

    def nrm(k, shape, scale):
        return jax.random.normal(k, shape, f32) * scale

    u = jax.random.uniform(ks[10], (L, D_RNN), f32, minval=0.9, maxval=0.999)
    p = u ** (1.0 / LRU_C)
    lru_lam = jnp.log(p) - jnp.log1p(-p)
    return {
        "x": nrm(ks[0], (BATCH, SEQ, D_MODEL), 1.0),
        "norm1_g": 1.0 + nrm(ks[1], (L, D_MODEL), 0.02),
        "w_in": nrm(ks[2], (L, D_MODEL, N_IN), D_MODEL ** -0.5),
        "b_in": nrm(ks[3], (L, N_IN), 0.02),
        "conv_a_w": nrm(ks[4], (L, CONV_A_WIDTH, D_RNN), CONV_A_WIDTH ** -0.5),
        "conv_a_b": nrm(ks[5], (L, D_RNN), 0.02),
        "lru_wr": nrm(ks[6], (L, LRU_BLOCKS, LRU_BLOCK_W, LRU_BLOCK_W), LRU_BLOCK_W ** -0.5),
        "lru_br": nrm(ks[7], (L, LRU_BLOCKS, LRU_BLOCK_W), 0.02),
        "lru_wi": nrm(ks[8], (L, LRU_BLOCKS, LRU_BLOCK_W, LRU_BLOCK_W), LRU_BLOCK_W ** -0.5),
        "lru_bi": nrm(ks[9], (L, LRU_BLOCKS, LRU_BLOCK_W), 0.02),
        "lru_lam": lru_lam,
        "conv_b_w": nrm(ks[11], (L, CONV_B_WIDTH, D_CONV), CONV_B_WIDTH ** -0.5),
        "w_pa": nrm(ks[12], (L, D_RNN, D_MODEL), D_RNN ** -0.5),
        "w_pb": nrm(ks[13], (L, D_CONV, D_MODEL), D_CONV ** -0.5),
        "w_o": nrm(ks[14], (L, D_MODEL, D_MODEL), D_MODEL ** -0.5),
        "norm2_g": 1.0 + nrm(ks[15], (L, D_MODEL), 0.02),
        "w_mlp1": nrm(ks[16], (L, D_MODEL, D_FF), D_MODEL ** -0.5),
        "w_mlp2": nrm(ks[17], (L, D_FF, D_MODEL), D_FF ** -0.5),
        "final_g": 1.0 + nrm(ks[18], (D_MODEL,), 0.02),
    }


def _fwd_reference(x, norm1_g, w_in, b_in, conv_a_w, conv_a_b, lru_wr, lru_br, lru_wi, lru_bi,
              lru_lam, conv_b_w, w_pa, w_pb, w_o, norm2_g, w_mlp1, w_mlp2, final_g):
    for l in range(DEPTH):
        x = _layer(x, norm1_g[l], w_in[l], b_in[l], conv_a_w[l], conv_a_b[l], lru_wr[l], lru_br[l],
                   lru_wi[l], lru_bi[l], lru_lam[l], conv_b_w[l], w_pa[l], w_pb[l], w_o[l],
                   norm2_g[l], w_mlp1[l], w_mlp2[l])
    return _rmsnorm(x, final_g)


import jax as _jax
import jax.numpy as _jnp

TWIN_FORMAT = 'train_step'
FWD_PARAMS = ['x', 'norm1_g', 'w_in', 'b_in', 'conv_a_w', 'conv_a_b', 'lru_wr', 'lru_br', 'lru_wi', 'lru_bi', 'lru_lam', 'conv_b_w', 'w_pa', 'w_pb', 'w_o', 'norm2_g', 'w_mlp1', 'w_mlp2', 'final_g']
TWIN_WEIGHTS = ['norm1_g', 'w_in', 'b_in', 'conv_a_w', 'conv_a_b', 'lru_wr', 'lru_br', 'lru_wi', 'lru_bi', 'lru_lam', 'conv_b_w', 'w_pa', 'w_pb', 'w_o', 'norm2_g', 'w_mlp1', 'w_mlp2', 'final_g']
TWIN_DIFF_INPUT = 'x'
TWIN_INPUTS = ['x', 'norm1_g', 'w_in', 'b_in', 'conv_a_w', 'conv_a_b', 'lru_wr', 'lru_br', 'lru_wi', 'lru_bi', 'lru_lam', 'conv_b_w', 'w_pa', 'w_pb', 'w_o', 'norm2_g', 'w_mlp1', 'w_mlp2', 'final_g', 'loss_target', 'm_norm1_g', 'm_w_in', 'm_b_in', 'm_conv_a_w', 'm_conv_a_b', 'm_lru_wr', 'm_lru_br', 'm_lru_wi', 'm_lru_bi', 'm_lru_lam', 'm_conv_b_w', 'm_w_pa', 'm_w_pb', 'm_w_o', 'm_norm2_g', 'm_w_mlp1', 'm_w_mlp2', 'm_final_g', 'v_norm1_g', 'v_w_in', 'v_b_in', 'v_conv_a_w', 'v_conv_a_b', 'v_lru_wr', 'v_lru_br', 'v_lru_wi', 'v_lru_bi', 'v_lru_lam', 'v_conv_b_w', 'v_w_pa', 'v_w_pb', 'v_w_o', 'v_norm2_g', 'v_w_mlp1', 'v_w_mlp2', 'v_final_g']
TWIN_OUTPUTS = ['loss', 'grad_x', 'grad_norm1_g', 'grad_w_in', 'grad_b_in', 'grad_conv_a_w', 'grad_conv_a_b', 'grad_lru_wr', 'grad_lru_br', 'grad_lru_wi', 'grad_lru_bi', 'grad_lru_lam', 'grad_conv_b_w', 'grad_w_pa', 'grad_w_pb', 'grad_w_o', 'grad_norm2_g', 'grad_w_mlp1', 'grad_w_mlp2', 'grad_final_g', 'delta_norm1_g', 'delta_w_in', 'delta_b_in', 'delta_conv_a_w', 'delta_conv_a_b', 'delta_lru_wr', 'delta_lru_br', 'delta_lru_wi', 'delta_lru_bi', 'delta_lru_lam', 'delta_conv_b_w', 'delta_w_pa', 'delta_w_pb', 'delta_w_o', 'delta_norm2_g', 'delta_w_mlp1', 'delta_w_mlp2', 'delta_final_g', 'new_m_norm1_g', 'new_m_w_in', 'new_m_b_in', 'new_m_conv_a_w', 'new_m_conv_a_b', 'new_m_lru_wr', 'new_m_lru_br', 'new_m_lru_wi', 'new_m_lru_bi', 'new_m_lru_lam', 'new_m_conv_b_w', 'new_m_w_pa', 'new_m_w_pb', 'new_m_w_o', 'new_m_norm2_g', 'new_m_w_mlp1', 'new_m_w_mlp2', 'new_m_final_g', 'new_v_norm1_g', 'new_v_w_in', 'new_v_b_in', 'new_v_conv_a_w', 'new_v_conv_a_b', 'new_v_lru_wr', 'new_v_lru_br', 'new_v_lru_wi', 'new_v_lru_bi', 'new_v_lru_lam', 'new_v_conv_b_w', 'new_v_w_pa', 'new_v_w_pb', 'new_v_w_o', 'new_v_norm2_g', 'new_v_w_mlp1', 'new_v_w_mlp2', 'new_v_final_g']
TWIN_LEAF_KINDS = {'loss': 'loss', 'grad_x': 'grad_x', 'grad_norm1_g': 'grad_w', 'grad_w_in': 'grad_w', 'grad_b_in': 'grad_w', 'grad_conv_a_w': 'grad_w', 'grad_conv_a_b': 'grad_w', 'grad_lru_wr': 'grad_w', 'grad_lru_br': 'grad_w', 'grad_lru_wi': 'grad_w', 'grad_lru_bi': 'grad_w', 'grad_lru_lam': 'grad_w', 'grad_conv_b_w': 'grad_w', 'grad_w_pa': 'grad_w', 'grad_w_pb': 'grad_w', 'grad_w_o': 'grad_w', 'grad_norm2_g': 'grad_w', 'grad_w_mlp1': 'grad_w', 'grad_w_mlp2': 'grad_w', 'grad_final_g': 'grad_w', 'delta_norm1_g': 'delta_w', 'delta_w_in': 'delta_w', 'delta_b_in': 'delta_w', 'delta_conv_a_w': 'delta_w', 'delta_conv_a_b': 'delta_w', 'delta_lru_wr': 'delta_w', 'delta_lru_br': 'delta_w', 'delta_lru_wi': 'delta_w', 'delta_lru_bi': 'delta_w', 'delta_lru_lam': 'delta_w', 'delta_conv_b_w': 'delta_w', 'delta_w_pa': 'delta_w', 'delta_w_pb': 'delta_w', 'delta_w_o': 'delta_w', 'delta_norm2_g': 'delta_w', 'delta_w_mlp1': 'delta_w', 'delta_w_mlp2': 'delta_w', 'delta_final_g': 'delta_w', 'new_m_norm1_g': 'new_m', 'new_m_w_in': 'new_m', 'new_m_b_in': 'new_m', 'new_m_conv_a_w': 'new_m', 'new_m_conv_a_b': 'new_m', 'new_m_lru_wr': 'new_m', 'new_m_lru_br': 'new_m', 'new_m_lru_wi': 'new_m', 'new_m_lru_bi': 'new_m', 'new_m_lru_lam': 'new_m', 'new_m_conv_b_w': 'new_m', 'new_m_w_pa': 'new_m', 'new_m_w_pb': 'new_m', 'new_m_w_o': 'new_m', 'new_m_norm2_g': 'new_m', 'new_m_w_mlp1': 'new_m', 'new_m_w_mlp2': 'new_m', 'new_m_final_g': 'new_m', 'new_v_norm1_g': 'new_v', 'new_v_w_in': 'new_v', 'new_v_b_in': 'new_v', 'new_v_conv_a_w': 'new_v', 'new_v_conv_a_b': 'new_v', 'new_v_lru_wr': 'new_v', 'new_v_lru_br': 'new_v', 'new_v_lru_wi': 'new_v', 'new_v_lru_bi': 'new_v', 'new_v_lru_lam': 'new_v', 'new_v_conv_b_w': 'new_v', 'new_v_w_pa': 'new_v', 'new_v_w_pb': 'new_v', 'new_v_w_o': 'new_v', 'new_v_norm2_g': 'new_v', 'new_v_w_mlp1': 'new_v', 'new_v_w_mlp2': 'new_v', 'new_v_final_g': 'new_v'}


def _forward(args):
    return _fwd_reference(*[args[k] for k in FWD_PARAMS])


def _output_shape():
    def fwd():
        inp = _fwd_setup_inputs(0)
        return _fwd_reference(*[inp[k] for k in FWD_PARAMS])
    out = _jax.eval_shape(fwd)
    return out.shape, out.dtype

N_MICROBATCH = 1
ADAM_LR = 0.001
ADAM_B1 = 0.9
ADAM_B2 = 0.999
ADAM_EPS = 1e-08
ADAM_WD = 0.01
ADAM_STEP = 10
PER_EXAMPLE_BATCH_AXIS = {'x': 0, 'loss_target': 0}
SHARED_INPUTS = []
_WEIGHT_DTYPES = {'norm1_g': _jnp.float32, 'w_in': _jnp.float32, 'b_in': _jnp.float32, 'conv_a_w': _jnp.float32, 'conv_a_b': _jnp.float32, 'lru_wr': _jnp.float32, 'lru_br': _jnp.float32, 'lru_wi': _jnp.float32, 'lru_bi': _jnp.float32, 'lru_lam': _jnp.float32, 'conv_b_w': _jnp.float32, 'w_pa': _jnp.float32, 'w_pb': _jnp.float32, 'w_o': _jnp.float32, 'norm2_g': _jnp.float32, 'w_mlp1': _jnp.float32, 'w_mlp2': _jnp.float32, 'final_g': _jnp.float32}
MOMENT_SCALE = {'norm1_g': 8.608775e-02, 'w_in': 3.206626e-02, 'b_in': 8.157619e-02, 'conv_a_w': 3.200361e-02, 'conv_a_b': 2.064142e-01, 'lru_wr': 4.940636e-03, 'lru_br': 6.430960e-03, 'lru_wi': 8.973042e-03, 'lru_bi': 1.164262e-02, 'lru_lam': 1.513426e-02, 'conv_b_w': 4.182509e-02, 'w_pa': 3.314134e-02, 'w_pb': 4.167518e-02, 'w_o': 5.161679e-02, 'norm2_g': 7.405516e-02, 'w_mlp1': 3.672900e-02, 'w_mlp2': 7.751052e-02, 'final_g': 1.627604e+01}


def _to_microbatches(a, axis):
    t = _jnp.moveaxis(a, axis, 0)
    t = t.reshape((N_MICROBATCH, t.shape[0] // N_MICROBATCH) + t.shape[1:])
    return _jnp.moveaxis(t, 1, axis + 1)


def setup_inputs(seed: int = 0) -> dict:
    inp = _fwd_setup_inputs(seed)
    key = _jax.random.fold_in(_jax.random.key(seed), 7919)
    shape, _ = _output_shape()
    out = dict(inp)
    out["loss_target"] = _jax.random.normal(_jax.random.fold_in(key, 0), shape, _jnp.float32)
    for i, name in enumerate(TWIN_WEIGHTS):
        w = inp[name].astype(_jnp.float32)
        if MOMENT_SCALE is None:
            s = _jnp.sqrt(_jnp.mean(_jnp.square(w)) + 1e-30)
        else:
            s = MOMENT_SCALE[name]
        km, kv = _jax.random.split(_jax.random.fold_in(key, i + 1))
        out[name] = w
        out["m_" + name] = s * _jax.random.normal(km, w.shape, _jnp.float32)
        out["v_" + name] = (s * s) * _jax.random.uniform(kv, w.shape, _jnp.float32, 0.5, 1.5)
    if N_MICROBATCH > 1:
        for name, axis in PER_EXAMPLE_BATCH_AXIS.items():
            out[name] = _to_microbatches(out[name], axis)
    return {'x': out['x'], 'norm1_g': out['norm1_g'], 'w_in': out['w_in'], 'b_in': out['b_in'], 'conv_a_w': out['conv_a_w'], 'conv_a_b': out['conv_a_b'], 'lru_wr': out['lru_wr'], 'lru_br': out['lru_br'], 'lru_wi': out['lru_wi'], 'lru_bi': out['lru_bi'], 'lru_lam': out['lru_lam'], 'conv_b_w': out['conv_b_w'], 'w_pa': out['w_pa'], 'w_pb': out['w_pb'], 'w_o': out['w_o'], 'norm2_g': out['norm2_g'], 'w_mlp1': out['w_mlp1'], 'w_mlp2': out['w_mlp2'], 'final_g': out['final_g'], 'loss_target': out['loss_target'], 'm_norm1_g': out['m_norm1_g'], 'm_w_in': out['m_w_in'], 'm_b_in': out['m_b_in'], 'm_conv_a_w': out['m_conv_a_w'], 'm_conv_a_b': out['m_conv_a_b'], 'm_lru_wr': out['m_lru_wr'], 'm_lru_br': out['m_lru_br'], 'm_lru_wi': out['m_lru_wi'], 'm_lru_bi': out['m_lru_bi'], 'm_lru_lam': out['m_lru_lam'], 'm_conv_b_w': out['m_conv_b_w'], 'm_w_pa': out['m_w_pa'], 'm_w_pb': out['m_w_pb'], 'm_w_o': out['m_w_o'], 'm_norm2_g': out['m_norm2_g'], 'm_w_mlp1': out['m_w_mlp1'], 'm_w_mlp2': out['m_w_mlp2'], 'm_final_g': out['m_final_g'], 'v_norm1_g': out['v_norm1_g'], 'v_w_in': out['v_w_in'], 'v_b_in': out['v_b_in'], 'v_conv_a_w': out['v_conv_a_w'], 'v_conv_a_b': out['v_conv_a_b'], 'v_lru_wr': out['v_lru_wr'], 'v_lru_br': out['v_lru_br'], 'v_lru_wi': out['v_lru_wi'], 'v_lru_bi': out['v_lru_bi'], 'v_lru_lam': out['v_lru_lam'], 'v_conv_b_w': out['v_conv_b_w'], 'v_w_pa': out['v_w_pa'], 'v_w_pb': out['v_w_pb'], 'v_w_o': out['v_w_o'], 'v_norm2_g': out['v_norm2_g'], 'v_w_mlp1': out['v_w_mlp1'], 'v_w_mlp2': out['v_w_mlp2'], 'v_final_g': out['v_final_g']}


def _loss(weights, diff, rest, loss_target):
    with _jax.named_scope("forward"):
        args = {**rest, TWIN_DIFF_INPUT: diff, **{k: w.astype(_WEIGHT_DTYPES[k]) for k, w in weights.items()}}
        y = _forward(args)
    with _jax.named_scope("loss_head"):
        err = _jnp.square(y.astype(_jnp.float32) - loss_target)
        return 0.5 * _jnp.sum(_jnp.mean(err, axis=-1)) if err.ndim else 0.5 * err


def _adamw(w, g, m, v):
    m = ADAM_B1 * m + (1.0 - ADAM_B1) * g
    v = ADAM_B2 * v + (1.0 - ADAM_B2) * _jnp.square(g)
    m_hat = m / (1.0 - ADAM_B1 ** ADAM_STEP)
    v_hat = v / (1.0 - ADAM_B2 ** ADAM_STEP)
    delta = -ADAM_LR * (m_hat / (_jnp.sqrt(v_hat) + ADAM_EPS) + ADAM_WD * w)
    return delta, m, v


def reference(x, norm1_g, w_in, b_in, conv_a_w, conv_a_b, lru_wr, lru_br, lru_wi, lru_bi, lru_lam, conv_b_w, w_pa, w_pb, w_o, norm2_g, w_mlp1, w_mlp2, final_g, loss_target, m_norm1_g, m_w_in, m_b_in, m_conv_a_w, m_conv_a_b, m_lru_wr, m_lru_br, m_lru_wi, m_lru_bi, m_lru_lam, m_conv_b_w, m_w_pa, m_w_pb, m_w_o, m_norm2_g, m_w_mlp1, m_w_mlp2, m_final_g, v_norm1_g, v_w_in, v_b_in, v_conv_a_w, v_conv_a_b, v_lru_wr, v_lru_br, v_lru_wi, v_lru_bi, v_lru_lam, v_conv_b_w, v_w_pa, v_w_pb, v_w_o, v_norm2_g, v_w_mlp1, v_w_mlp2, v_final_g):
    given = dict(x=x, norm1_g=norm1_g, w_in=w_in, b_in=b_in, conv_a_w=conv_a_w, conv_a_b=conv_a_b, lru_wr=lru_wr, lru_br=lru_br, lru_wi=lru_wi, lru_bi=lru_bi, lru_lam=lru_lam, conv_b_w=conv_b_w, w_pa=w_pa, w_pb=w_pb, w_o=w_o, norm2_g=norm2_g, w_mlp1=w_mlp1, w_mlp2=w_mlp2, final_g=final_g, loss_target=loss_target, m_norm1_g=m_norm1_g, m_w_in=m_w_in, m_b_in=m_b_in, m_conv_a_w=m_conv_a_w, m_conv_a_b=m_conv_a_b, m_lru_wr=m_lru_wr, m_lru_br=m_lru_br, m_lru_wi=m_lru_wi, m_lru_bi=m_lru_bi, m_lru_lam=m_lru_lam, m_conv_b_w=m_conv_b_w, m_w_pa=m_w_pa, m_w_pb=m_w_pb, m_w_o=m_w_o, m_norm2_g=m_norm2_g, m_w_mlp1=m_w_mlp1, m_w_mlp2=m_w_mlp2, m_final_g=m_final_g, v_norm1_g=v_norm1_g, v_w_in=v_w_in, v_b_in=v_b_in, v_conv_a_w=v_conv_a_w, v_conv_a_b=v_conv_a_b, v_lru_wr=v_lru_wr, v_lru_br=v_lru_br, v_lru_wi=v_lru_wi, v_lru_bi=v_lru_bi, v_lru_lam=v_lru_lam, v_conv_b_w=v_conv_b_w, v_w_pa=v_w_pa, v_w_pb=v_w_pb, v_w_o=v_w_o, v_norm2_g=v_norm2_g, v_w_mlp1=v_w_mlp1, v_w_mlp2=v_w_mlp2, v_final_g=v_final_g)
    weights = {n: given[n] for n in TWIN_WEIGHTS}
    shared = {n: given[n] for n in SHARED_INPUTS}
    per_example = {n: given[n] for n in ['x']}
    grad_fn = _jax.value_and_grad(_loss, argnums=(0, 1))

    def one_microbatch(ex, loss_target):
        ex = dict(ex)
        diff = ex.pop(TWIN_DIFF_INPUT)
        return grad_fn(weights, diff, {**shared, **ex}, loss_target)

    if N_MICROBATCH == 1:
        loss, (grad_w, grad_x) = one_microbatch(per_example, given["loss_target"])
    else:
        def body(carry, xs):
            loss_sum, grad_sum = carry
            l_k, (gw_k, gx_k) = one_microbatch(xs[0], xs[1])
            with _jax.named_scope("update"):
                return (loss_sum + l_k, _jax.tree.map(_jnp.add, grad_sum, gw_k)), gx_k

        init = (_jnp.zeros((), _jnp.float32), _jax.tree.map(_jnp.zeros_like, weights))
        (loss, grad_w), grad_x = _jax.lax.scan(body, init, (per_example, given["loss_target"]))
    with _jax.named_scope("update"):
        delta_w, new_m, new_v = {}, {}, {}
        for n in TWIN_WEIGHTS:
            delta_w[n], new_m[n], new_v[n] = _adamw(weights[n], grad_w[n], given["m_" + n], given["v_" + n])
    return (loss, grad_x, *[grad_w[n] for n in TWIN_WEIGHTS], *[delta_w[n] for n in TWIN_WEIGHTS],
            *[new_m[n] for n in TWIN_WEIGHTS], *[new_v[n] for n in TWIN_WEIGHTS])
```

```python
import functools

import jax
import jax.numpy as jnp
from jax import lax
from jax.experimental import pallas as pl
from jax.experimental.pallas import tpu as pltpu

F32 = jnp.float32
BF16 = jnp.bfloat16
MESH = pl.DeviceIdType.MESH

EPS = 1e-6
LRU_C = 8.0
ADAM_LR = 0.001
ADAM_B1 = 0.9
ADAM_B2 = 0.999
ADAM_EPS = 1e-08
ADAM_WD = 0.01
ADAM_STEP = 10

N_CHIPS = 4
N_DEV = 8
HALO = 8
VMEM_LIMIT = 56 * 1024 * 1024
MM_TILES = (1024, 1024, 1024)
MM_TILES_FUSED = (512, 1024, 1024)
SEQ_CHUNK = 256
ROW_TILE = 256

R_CAB, R_BR, R_BI, R_LAM, R_CAW, R_CBW, R_ROWS = 0, 1, 2, 3, 4, 8, 16


def _cparams(sem):
    return pltpu.CompilerParams(dimension_semantics=sem, vmem_limit_bytes=VMEM_LIMIT)


def _div_tile(n, pref, unit=128):
    if n <= pref:
        return n
    t = (pref // unit) * unit
    while n % t:
        t -= unit
    return t


def _sigmoid(v):
    return 1.0 / (1.0 + jnp.exp(-v))


def _gelu_and_grad(y):
    k = 0.7978845608028654
    c = 0.044715
    y2 = y * y
    t = jnp.tanh(k * (y + c * y2 * y))
    g = 0.5 * y * (1.0 + t)
    gp = 0.5 * (1.0 + t) + 0.5 * y * (1.0 - t * t) * (k * (1.0 + 3.0 * c * y2))
    return g, gp


def _softplus_neg(lam):
    e = jnp.exp(-jnp.abs(lam))
    w = 1.0 + e
    l1p = jnp.where(w == 1.0, e, jnp.log(w) * e / jnp.where(w == 1.0, 1.0, w - 1.0))
    return jnp.maximum(-lam, 0.0) + l1p


def _mm(name, mode, a, b, M, N, K, out_dtypes, epilogue=None, extras=(), la=None, lb=None, tiles=None):
    tiles = MM_TILES if tiles is None else tiles
    tm, tn, tk = _div_tile(M, tiles[0]), _div_tile(N, tiles[1]), _div_tile(K, tiles[2])
    assert M % tm == 0 and N % tn == 0 and K % tk == 0, (name, M, N, K)
    nk = K // tk

    def spec(lead, shape, imap):
        if lead is None:
            return pl.BlockSpec(shape, imap)
        return pl.BlockSpec((None,) + shape, lambda i, j, k: (lead,) + imap(i, j, k))

    if mode == "nn":
        a_spec = spec(la, (tm, tk), lambda i, j, k: (i, k))
        b_spec = spec(lb, (tk, tn), lambda i, j, k: (k, j))
        dn = (((1,), (0,)), ((), ()))
    elif mode == "nt":
        a_spec = spec(la, (tm, tk), lambda i, j, k: (i, k))
        b_spec = spec(lb, (tn, tk), lambda i, j, k: (j, k))
        dn = (((1,), (1,)), ((), ()))
    else:
        a_spec = spec(la, (tk, tm), lambda i, j, k: (k, i))
        b_spec = spec(lb, (tk, tn), lambda i, j, k: (k, j))
        dn = (((0,), (0,)), ((), ()))

    ex_arrays, ex_specs = [], []
    for arr, kind, off in extras:
        ex_arrays.append(arr)
        if kind == "bias":
            ex_specs.append(pl.BlockSpec((1, tn), lambda i, j, k: (0, j)))
        else:
            assert off % tn == 0
            ex_specs.append(pl.BlockSpec((tm, tn), lambda i, j, k, o=off // tn: (i, j + o)))
    n_ex, n_out = len(ex_arrays), len(out_dtypes)

    def body(*refs):
        a_ref, b_ref = refs[0], refs[1]
        ex = refs[2:2 + n_ex]
        outs = refs[2 + n_ex:2 + n_ex + n_out]
        acc = refs[-1]
        k = pl.program_id(2)

        @pl.when(k == 0)
        def _():
            acc[...] = jnp.zeros_like(acc)

        acc[...] += lax.dot_general(a_ref[...], b_ref[...], dn, preferred_element_type=F32)

        @pl.when(k == nk - 1)
        def _():
            r = acc[...]
            vals = (r,) if epilogue is None else epilogue(r, *[e[...] for e in ex])
            for o, v in zip(outs, vals):
                o[...] = v.astype(o.dtype)

    res = pl.pallas_call(
        body,
        grid=(M // tm, N // tn, nk),
        in_specs=[a_spec, b_spec, *ex_specs],
        out_specs=[pl.BlockSpec((tm, tn), lambda i, j, k: (i, j)) for _ in range(n_out)],
        out_shape=[jax.ShapeDtypeStruct((M, N), d) for d in out_dtypes],
        scratch_shapes=[pltpu.VMEM((tm, tn), F32)],
        compiler_params=_cparams(("parallel", "parallel", "arbitrary")),
        name=name,
    )(a, b, *ex_arrays)
    return res[0] if n_out == 1 else res


def _rms_fwd(name, x, g_row):
    T, D = x.shape
    tm = min(ROW_TILE, T)

    def body(x_ref, g_ref, h_ref):
        xv = x_ref[...]
        r = lax.rsqrt(jnp.mean(xv * xv, axis=-1, keepdims=True) + EPS)
        h_ref[...] = (xv * r * g_ref[...]).astype(BF16)

    return pl.pallas_call(
        body,
        grid=(T // tm,),
        in_specs=[pl.BlockSpec((tm, D), lambda i: (i, 0)), pl.BlockSpec((1, D), lambda i: (0, 0))],
        out_specs=pl.BlockSpec((tm, D), lambda i: (i, 0)),
        out_shape=jax.ShapeDtypeStruct((T, D), BF16),
        compiler_params=_cparams(("parallel",)),
        name=name,
    )(x, g_row)


def _rms_bwd(name, x, g_row, dh, dres):
    T, D = x.shape
    tm = min(ROW_TILE, T)

    def body(x_ref, g_ref, dh_ref, dres_ref, dx_ref, dxb_ref, dg_ref):
        xv, dhv = x_ref[...], dh_ref[...]
        r = lax.rsqrt(jnp.mean(xv * xv, axis=-1, keepdims=True) + EPS)
        gd = g_ref[...] * dhv
        c = jnp.mean(xv * gd, axis=-1, keepdims=True)
        dx = r * gd - xv * (r * r * r) * c + dres_ref[...]
        dx_ref[...] = dx
        dxb_ref[...] = dx.astype(BF16)

        @pl.when(pl.program_id(0) == 0)
        def _():
            dg_ref[...] = jnp.zeros_like(dg_ref)

        dg_ref[...] += jnp.sum(dhv * xv * r, axis=0, keepdims=True)

    row = pl.BlockSpec((tm, D), lambda i: (i, 0))
    vec = pl.BlockSpec((1, D), lambda i: (0, 0))
    return pl.pallas_call(
        body,
        grid=(T // tm,),
        in_specs=[row, vec, row, row],
        out_specs=[row, row, vec],
        out_shape=[jax.ShapeDtypeStruct((T, D), F32), jax.ShapeDtypeStruct((T, D), BF16),
                   jax.ShapeDtypeStruct((1, D), F32)],
        compiler_params=_cparams(("arbitrary",)),
        name=name,
    )(x, g_row, dh, dres)


def _loss_head(name, x, g_row, tgt):
    T, D = x.shape
    tm = min(ROW_TILE, T)

    def body(x_ref, g_ref, t_ref, dx_ref, dxb_ref, dg_ref, loss_ref):
        xv, g = x_ref[...], g_ref[...]
        r = lax.rsqrt(jnp.mean(xv * xv, axis=-1, keepdims=True) + EPS)
        xh = xv * r
        e = xh * g - t_ref[...]
        lpart = 0.5 * jnp.sum(jnp.mean(e * e, axis=-1, keepdims=True))
        dy = e * (1.0 / D)
        gd = g * dy
        c = jnp.mean(xv * gd, axis=-1, keepdims=True)
        dx = r * gd - xv * (r * r * r) * c
        dx_ref[...] = dx
        dxb_ref[...] = dx.astype(BF16)

        @pl.when(pl.program_id(0) == 0)
        def _():
            dg_ref[...] = jnp.zeros_like(dg_ref)
            loss_ref[...] = jnp.zeros_like(loss_ref)

        dg_ref[...] += jnp.sum(dy * xh, axis=0, keepdims=True)
        loss_ref[...] += jnp.full(loss_ref.shape, lpart, F32)

    row = pl.BlockSpec((tm, D), lambda i: (i, 0))
    vec = pl.BlockSpec((1, D), lambda i: (0, 0))
    return pl.pallas_call(
        body,
        grid=(T // tm,),
        in_specs=[row, vec, row],
        out_specs=[row, row, vec, pl.BlockSpec((8, 128), lambda i: (0, 0))],
        out_shape=[jax.ShapeDtypeStruct((T, D), F32), jax.ShapeDtypeStruct((T, D), BF16),
                   jax.ShapeDtypeStruct((1, D), F32), jax.ShapeDtypeStruct((8, 128), F32)],
        compiler_params=_cparams(("arbitrary",)),
        name=name,
    )(x, g_row, tgt)


def _colsum(name, a):
    T, N = a.shape
    tm, tn = min(512, T), _div_tile(N, 2048)

    def body(a_ref, o_ref):
        @pl.when(pl.program_id(1) == 0)
        def _():
            o_ref[...] = jnp.zeros_like(o_ref)

        o_ref[...] += jnp.sum(a_ref[...].astype(F32), axis=0, keepdims=True)

    return pl.pallas_call(
        body,
        grid=(N // tn, T // tm),
        in_specs=[pl.BlockSpec((tm, tn), lambda j, i: (i, j))],
        out_specs=pl.BlockSpec((1, tn), lambda j, i: (0, j)),
        out_shape=jax.ShapeDtypeStruct((1, N), F32),
        compiler_params=_cparams(("parallel", "arbitrary")),
        name=name,
    )(a)


def _tile_scan(a, b, row, reverse):
    for s in (1, 2, 4):
        if reverse:
            a_s, b_s, m = pltpu.roll(a, 8 - s, 0), pltpu.roll(b, 8 - s, 0), row < 8 - s
        else:
            a_s, b_s, m = pltpu.roll(a, s, 0), pltpu.roll(b, s, 0), row >= s
        b = jnp.where(m, a * b_s + b, b)
        a = jnp.where(m, a * a_s, a)
    return a, b


def _chunk_scan(a_s, b_s, out_ref, carry, n_tiles, width, reverse):
    row = lax.broadcasted_iota(jnp.int32, (8, width), 0)
    edge = 0 if reverse else 7

    def step(j, c):
        jj = (n_tiles - 1 - j) if reverse else j
        o = pl.multiple_of(jj * 8, 8)
        ca, cb = _tile_scan(a_s[pl.ds(o, 8), :], b_s[pl.ds(o, 8), :], row, reverse)
        h = ca * c + cb
        out_ref[pl.ds(o, 8), :] = h
        return jnp.broadcast_to(h[edge:edge + 1, :], (8, width))

    carry[...] = lax.fori_loop(0, n_tiles, step, carry[...])


def _gates(xc, p_ref, wr_ref, wi_ref):
    xcb = xc.astype(BF16)
    r = _sigmoid(jnp.dot(xcb, wr_ref[...], preferred_element_type=F32) + p_ref[R_BR:R_BR + 1, :])
    ig = _sigmoid(jnp.dot(xcb, wi_ref[...], preferred_element_type=F32) + p_ref[R_BI:R_BI + 1, :])
    sp = _softplus_neg(p_ref[R_LAM:R_LAM + 1, :])
    log_a = (-LRU_C) * r * sp
    a = jnp.exp(log_a)
    t = jnp.tanh(log_a)
    mult = jnp.sqrt(-2.0 * t / (1.0 - t))
    return xcb, r, ig, sp, a, mult


def _mixer_specs(Tc, bw, nb, layer):
    def seg(s):
        return pl.BlockSpec((Tc, bw), lambda n, i: (i, s * nb + n))

    p_spec = pl.BlockSpec((None, R_ROWS, bw), lambda n, i: (layer, 0, n))
    w_spec = pl.BlockSpec((None, None, bw, bw), lambda n, i: (layer, n, 0, 0))
    return seg, p_spec, w_spec


def _mixer_fwd(name, layer, z, pch, wr, wi):
    T, D = z.shape[0], z.shape[1] // 7
    bw, nb = wr.shape[-1], wr.shape[1]
    Tc = min(SEQ_CHUNK, T)
    nT = T // Tc

    def body(xa_ref, ya_ref, cb_ref, cc_ref, cx_ref, p_ref, wr_ref, wi_ref,
             pa_ref, pb_ref, xc_ref, hl_ref, vb_ref, xa_buf, u_buf, a_s, b_s, carry):
        @pl.when(pl.program_id(1) == 0)
        def _():
            xa_buf[0:HALO, :] = jnp.zeros((HALO, bw), F32)
            u_buf[0:HALO, :] = jnp.zeros((HALO, bw), F32)
            carry[...] = jnp.zeros_like(carry)

        xa_buf[HALO:HALO + Tc, :] = xa_ref[...]
        xc = p_ref[R_CAB:R_CAB + 1, :]
        for k in range(4):
            xc = xc + p_ref[R_CAW + k:R_CAW + k + 1, :] * xa_buf[HALO - 3 + k:HALO - 3 + k + Tc, :]
        xc_ref[...] = xc
        _, _, ig, _, a, mult = _gates(xc, p_ref, wr_ref, wi_ref)
        a_s[...] = a
        b_s[...] = mult * (ig * xc)
        _chunk_scan(a_s, b_s, hl_ref, carry, Tc // 8, bw, False)
        g, _ = _gelu_and_grad(ya_ref[...])
        pa_ref[...] = (hl_ref[...] * g).astype(BF16)

        u_buf[HALO:HALO + Tc, :] = cc_ref[...] * cx_ref[...]
        vb = jnp.zeros((Tc, bw), F32)
        for k in range(3):
            vb = vb + p_ref[R_CBW + k:R_CBW + k + 1, :] * u_buf[HALO - 2 + k:HALO - 2 + k + Tc, :]
        vb_ref[...] = vb
        pb_ref[...] = (cb_ref[...] * vb).astype(BF16)
        xa_buf[0:HALO, :] = xa_buf[Tc:Tc + HALO, :]
        u_buf[0:HALO, :] = u_buf[Tc:Tc + HALO, :]

    seg, p_spec, w_spec = _mixer_specs(Tc, bw, nb, layer)
    out = pl.BlockSpec((Tc, bw), lambda n, i: (i, n))
    return pl.pallas_call(
        body,
        grid=(nb, nT),
        in_specs=[seg(0), seg(1), seg(2), seg(3), seg(4), p_spec, w_spec, w_spec],
        out_specs=[out] * 5,
        out_shape=[jax.ShapeDtypeStruct((T, D), BF16), jax.ShapeDtypeStruct((T, D), BF16),
                   jax.ShapeDtypeStruct((T, D), F32), jax.ShapeDtypeStruct((T, D), F32),
                   jax.ShapeDtypeStruct((T, D), F32)],
        scratch_shapes=[pltpu.VMEM((Tc + HALO, bw), F32), pltpu.VMEM((Tc + HALO, bw), F32),
                        pltpu.VMEM((Tc, bw), F32), pltpu.VMEM((Tc, bw), F32), pltpu.VMEM((8, bw), F32)],
        compiler_params=_cparams(("parallel", "arbitrary")),
        name=name,
    )(z, z, z, z, z, pch, wr, wi)


def _mixer_bwd(name, layer, z, xc, hl, vb, dpa, dpb, pch, wr, wi):
    T, D = z.shape[0], z.shape[1] // 7
    bw, nb = wr.shape[-1], wr.shape[1]
    Tc = min(SEQ_CHUNK, T)
    nT = T // Tc
    tpc = Tc // 8

    def body(xa_ref, ya_ref, cb_ref, cc_ref, cx_ref, xc_ref, hl_ref, hp_ref, vb_ref, dpa_ref, dpb_ref,
             p_ref, wr_ref, wi_ref,
             dxa_ref, dya_ref, dcb_ref, dcc_ref, dcx_ref, dwr_ref, dwi_ref, sm_ref,
             h_buf, a_buf, dxc_buf, dvb_buf, a_s, d_s, lam_s, carry):
        i = pl.program_id(1)

        @pl.when(i == 0)
        def _():
            a_buf[Tc:Tc + HALO, :] = jnp.zeros((HALO, bw), F32)
            dxc_buf[Tc:Tc + HALO, :] = jnp.zeros((HALO, bw), F32)
            dvb_buf[Tc:Tc + HALO, :] = jnp.zeros((HALO, bw), F32)
            carry[...] = jnp.zeros_like(carry)
            dwr_ref[...] = jnp.zeros_like(dwr_ref)
            dwi_ref[...] = jnp.zeros_like(dwi_ref)
            sm_ref[...] = jnp.zeros_like(sm_ref)

        xcv = xc_ref[...]
        xcb, r, ig, sp, a, mult = _gates(xcv, p_ref, wr_ref, wi_ref)
        g, gp = _gelu_and_grad(ya_ref[...])
        hlv, dpav = hl_ref[...], dpa_ref[...]
        dya_ref[...] = (dpav * hlv * gp).astype(BF16)

        a_buf[0:Tc, :] = a
        a_s[...] = a_buf[1:Tc + 1, :]
        d_s[...] = dpav * g
        _chunk_scan(a_s, d_s, lam_s, carry, tpc, bw, True)
        lamv = lam_s[...]

        h_buf[HALO:HALO + Tc, :] = hlv
        h_buf[0:HALO, :] = jnp.where(i == nT - 1, 0.0, hp_ref[...])
        da = lamv * h_buf[HALO - 1:HALO - 1 + Tc, :]
        dmult = lamv * (ig * xcv)
        dbx = lamv * mult
        dig = dbx * xcv
        dxc = dbx * ig
        dlog_a = da * a - dmult * (a * a) / mult
        dpr = (dlog_a * ((-LRU_C) * sp)) * r * (1.0 - r)
        dpi = dig * ig * (1.0 - ig)
        dprb, dpib = dpr.astype(BF16), dpi.astype(BF16)
        nt = (((1,), (1,)), ((), ()))
        tn = (((0,), (0,)), ((), ()))
        dxc = dxc + lax.dot_general(dprb, wr_ref[...], nt, preferred_element_type=F32)
        dxc = dxc + lax.dot_general(dpib, wi_ref[...], nt, preferred_element_type=F32)
        dwr_ref[...] += lax.dot_general(xcb, dprb, tn, preferred_element_type=F32)
        dwi_ref[...] += lax.dot_general(xcb, dpib, tn, preferred_element_type=F32)

        def rowsum(v):
            return jnp.sum(v, axis=0, keepdims=True)

        sm_ref[R_CAB:R_CAB + 1, :] += rowsum(dxc)
        sm_ref[R_BR:R_BR + 1, :] += rowsum(dpr)
        sm_ref[R_BI:R_BI + 1, :] += rowsum(dpi)
        sm_ref[R_LAM:R_LAM + 1, :] += rowsum(dlog_a * ((-LRU_C) * r))

        dxc_buf[0:Tc, :] = dxc
        xav = xa_ref[...]
        dxa = jnp.zeros((Tc, bw), F32)
        for k in range(4):
            sh = dxc_buf[3 - k:3 - k + Tc, :]
            dxa = dxa + p_ref[R_CAW + k:R_CAW + k + 1, :] * sh
            sm_ref[R_CAW + k:R_CAW + k + 1, :] += rowsum(xav * sh)
        dxa_ref[...] = dxa.astype(BF16)

        dpbv, cbv, ccv, cxv = dpb_ref[...], cb_ref[...], cc_ref[...], cx_ref[...]
        dcb_ref[...] = (dpbv * vb_ref[...]).astype(BF16)
        dvb_buf[0:Tc, :] = dpbv * cbv
        u = ccv * cxv
        du = jnp.zeros((Tc, bw), F32)
        for k in range(3):
            sh = dvb_buf[2 - k:2 - k + Tc, :]
            du = du + p_ref[R_CBW + k:R_CBW + k + 1, :] * sh
            sm_ref[R_CBW + k:R_CBW + k + 1, :] += rowsum(u * sh)
        dcc_ref[...] = (du * cxv).astype(BF16)
        dcx_ref[...] = (du * ccv).astype(BF16)

        a_buf[Tc:Tc + HALO, :] = a_buf[0:HALO, :]
        dxc_buf[Tc:Tc + HALO, :] = dxc_buf[0:HALO, :]
        dvb_buf[Tc:Tc + HALO, :] = dvb_buf[0:HALO, :]

        @pl.when(i == nT - 1)
        def _():
            sm_ref[R_LAM:R_LAM + 1, :] = sm_ref[R_LAM:R_LAM + 1, :] * (-_sigmoid(-p_ref[R_LAM:R_LAM + 1, :]))

    def seg(s):
        return pl.BlockSpec((Tc, bw), lambda n, i: (nT - 1 - i, s * nb + n))

    blk = pl.BlockSpec((Tc, bw), lambda n, i: (nT - 1 - i, n))
    halo = pl.BlockSpec((8, bw), lambda n, i: (jnp.maximum((nT - 1 - i) * tpc - 1, 0), n))
    p_spec = pl.BlockSpec((None, R_ROWS, bw), lambda n, i: (layer, 0, n))
    w_spec = pl.BlockSpec((None, None, bw, bw), lambda n, i: (layer, n, 0, 0))
    dw_spec = pl.BlockSpec((None, bw, bw), lambda n, i: (n, 0, 0))
    act = jax.ShapeDtypeStruct((T, D), BF16)
    return pl.pallas_call(
        body,
        grid=(nb, nT),
        in_specs=[seg(0), seg(1), seg(2), seg(3), seg(4), blk, blk, halo, blk, blk, blk, p_spec, w_spec, w_spec],
        out_specs=[blk] * 5 + [dw_spec, dw_spec, pl.BlockSpec((R_ROWS, bw), lambda n, i: (0, n))],
        out_shape=[act] * 5 + [jax.ShapeDtypeStruct((nb, bw, bw), F32), jax.ShapeDtypeStruct((nb, bw, bw), F32),
                               jax.ShapeDtypeStruct((R_ROWS, D), F32)],
        scratch_shapes=[pltpu.VMEM((Tc + HALO, bw), F32)] * 4 + [pltpu.VMEM((Tc, bw), F32)] * 3
        + [pltpu.VMEM((8, bw), F32)],
        compiler_params=_cparams(("parallel", "arbitrary")),
        name=name,
    )(z, z, z, z, z, xc, hl, hl, vb, dpa, dpb, pch, wr, wi)


def _local_fwd_bwd(x, tgt, W):
    T, D = x.shape
    depth = W["w_in"].shape[0]
    FF = W["w1"].shape[2]
    g1, g2 = W["g1"], W["g2"]
    saved = []
    xs = x
    for l in range(depth):
        h = _rms_fwd(f"rms1_fwd_{l}", xs, g1[l][None])
        z = _mm(f"in_proj_{l}", "nn", h, W["w_in"], T, 7 * D, D, [F32], lb=l,
                epilogue=lambda acc, b: (acc + b,), extras=[(W["b_in"][l][None], "bias", 0)])
        pa, pb, xc, hl, vb = _mixer_fwd(f"mixer_fwd_{l}", l, z, W["pch"], W["wr"], W["wi"])
        oa = _mm(f"proj_a_{l}", "nn", pa, W["w_pa"], T, D, D, [F32], lb=l)

        def merge(acc, oav, ga, gb):
            return acc, _sigmoid(ga) * oav + _sigmoid(gb) * acc

        ob, mg = _mm(f"proj_b_merge_{l}", "nn", pb, W["w_pb"], T, D, D, [F32, BF16], lb=l, epilogue=merge,
                     tiles=MM_TILES_FUSED,
                     extras=[(oa, "tile", 0), (z, "tile", 5 * D), (z, "tile", 6 * D)])
        x1 = _mm(f"out_proj_{l}", "nn", mg, W["w_o"], T, D, D, [F32], lb=l,
                 epilogue=lambda acc, res: (res + acc,), extras=[(xs, "tile", 0)])
        h2 = _rms_fwd(f"rms2_fwd_{l}", x1, g2[l][None])

        def relu2(acc):
            pr = jnp.maximum(acc, 0.0)
            return pr * pr, pr

        u, pr = _mm(f"mlp1_{l}", "nn", h2, W["w1"], T, FF, D, [BF16, BF16], lb=l, epilogue=relu2)
        x2 = _mm(f"mlp2_{l}", "nn", u, W["w2"], T, D, FF, [F32], lb=l,
                 epilogue=lambda acc, res: (res + acc,), extras=[(x1, "tile", 0)])
        saved.append(dict(x0=xs, h=h, z=z, pa=pa, pb=pb, xc=xc, hl=hl, vb=vb, oa=oa, ob=ob, mg=mg, x1=x1,
                          h2=h2, u=u, pr=pr))
        xs = x2

    dx, dxb, dgf, loss_blk = _loss_head("loss_head", xs, W["gf"][None], tgt)
    grads = [None] * depth
    for l in reversed(range(depth)):
        s = saved[l]
        dp = _mm(f"mlp2_dx_{l}", "nt", dxb, W["w2"], T, FF, D, [BF16], lb=l,
                 epilogue=lambda acc, prv: (2.0 * prv.astype(F32) * acc,), extras=[(s["pr"], "tile", 0)])
        dw2 = _mm(f"mlp2_dw_{l}", "tn", s["u"], dxb, FF, D, T, [BF16])
        dh2 = _mm(f"mlp1_dx_{l}", "nt", dp, W["w1"], T, D, FF, [F32], lb=l)
        dw1 = _mm(f"mlp1_dw_{l}", "tn", s["h2"], dp, D, FF, T, [BF16])
        dx1, dx1b, dg2 = _rms_bwd(f"rms2_bwd_{l}", s["x1"], g2[l][None], dh2, dx)

        def unmerge(acc, ga, gb, oav, obv):
            sa, sb = _sigmoid(ga), _sigmoid(gb)
            return acc * sa, acc * sb, acc * oav * sa * (1.0 - sa), acc * obv * sb * (1.0 - sb)

        doa, dob, dga, dgb = _mm(f"out_proj_dx_{l}", "nt", dx1b, W["w_o"], T, D, D, [BF16] * 4, lb=l,
                                 tiles=MM_TILES_FUSED,
                                 epilogue=unmerge,
                                 extras=[(s["z"], "tile", 5 * D), (s["z"], "tile", 6 * D),
                                         (s["oa"], "tile", 0), (s["ob"], "tile", 0)])
        dwo = _mm(f"out_proj_dw_{l}", "tn", s["mg"], dx1b, D, D, T, [BF16])
        dpa = _mm(f"proj_a_dx_{l}", "nt", doa, W["w_pa"], T, D, D, [F32], lb=l)
        dwpa = _mm(f"proj_a_dw_{l}", "tn", s["pa"], doa, D, D, T, [BF16])
        dpb = _mm(f"proj_b_dx_{l}", "nt", dob, W["w_pb"], T, D, D, [F32], lb=l)
        dwpb = _mm(f"proj_b_dw_{l}", "tn", s["pb"], dob, D, D, T, [BF16])
        dxa, dya, dcb, dcc, dcx, dwr, dwi, sm = _mixer_bwd(
            f"mixer_bwd_{l}", l, s["z"], s["xc"], s["hl"], s["vb"], dpa, dpb, W["pch"], W["wr"], W["wi"])
        dz = jnp.concatenate([dxa, dya, dcb, dcc, dcx, dga, dgb], axis=1)
        dbin = _colsum(f"bias_grad_{l}", dz)
        dh = _mm(f"in_proj_dx_{l}", "nt", dz, W["w_in"], T, D, 7 * D, [F32], lb=l)
        dwin = _mm(f"in_proj_dw_{l}", "tn", s["h"], dz, D, 7 * D, T, [BF16])
        dx, dxb, dg1 = _rms_bwd(f"rms1_bwd_{l}", s["x0"], g1[l][None], dh, dx1)
        grads[l] = dict(w_in=dwin, w_pa=dwpa, w_pb=dwpb, w_o=dwo, w1=dw1, w2=dw2, wr=dwr, wi=dwi,
                        sm=sm, b_in=dbin, g1=dg1, g2=dg2)
    return loss_blk[0, 0], dx, grads, dgf


ANY = pl.BlockSpec(memory_space=pl.ANY)


def _place():
    x, y, c = lax.axis_index("x"), lax.axis_index("y"), lax.axis_index("c")
    peers = [(1 - x, y, c), (x, 1 - y, c), (1 - x, 1 - y, c)]
    chips = [2 * (1 - x) + y, 2 * x + (1 - y), 2 * (1 - x) + (1 - y)]
    return (x, y, c), 2 * x + y, peers, chips


def _window(ref, dim, q, size):
    idx = [slice(None)] * len(ref.shape)
    idx[dim] = pl.ds(q * size, size)
    return ref.at[tuple(idx)]


def _gather_weights(shards, dims, small):
    n = len(shards)
    sizes = [s.shape[d] for s, d in zip(shards, dims)]
    full = [jax.ShapeDtypeStruct(s.shape[:d] + (s.shape[d] * N_CHIPS,) + s.shape[d + 1:], s.dtype)
            for s, d in zip(shards, dims)]
    full.append(jax.ShapeDtypeStruct((N_CHIPS,) + small.shape, small.dtype))

    def body(*refs):
        ins, outs = refs[:n + 1], refs[n + 1:2 * n + 2]
        send_sems, recv_sems, local_sems = refs[2 * n + 2:]
        _, k, peers, chips = _place()

        def dst(w, q):
            return outs[w].at[q] if w == n else _window(outs[w], dims[w], q, sizes[w])

        local = [pltpu.make_async_copy(ins[w], dst(w, k), local_sems.at[w]) for w in range(n + 1)]
        for cp in local:
            cp.start()
        sends = []
        for p, peer in enumerate(peers):
            for w in range(n + 1):
                s = p * (n + 1) + w
                sends.append(pltpu.make_async_remote_copy(
                    src_ref=ins[w], dst_ref=dst(w, k), send_sem=send_sems.at[s], recv_sem=recv_sems.at[s],
                    device_id=peer, device_id_type=MESH))
        for cp in sends:
            cp.start()
        for p, peer in enumerate(peers):
            for w in range(n + 1):
                s = p * (n + 1) + w
                pltpu.make_async_remote_copy(
                    src_ref=ins[w], dst_ref=dst(w, chips[p]), send_sem=send_sems.at[s], recv_sem=recv_sems.at[s],
                    device_id=peer, device_id_type=MESH).wait_recv()
        for cp in sends:
            cp.wait_send()
        for cp in local:
            cp.wait()

    return pl.pallas_call(
        body,
        in_specs=[ANY] * (n + 1),
        out_specs=[ANY] * (n + 1),
        out_shape=full,
        scratch_shapes=[pltpu.SemaphoreType.DMA((3 * (n + 1),)), pltpu.SemaphoreType.DMA((3 * (n + 1),)),
                        pltpu.SemaphoreType.DMA((n + 1,))],
        name="gather_weights",
    )(*shards, small)


def _scatter_grads(grads, dims):
    n, depth = len(grads), len(grads[0])
    sizes = [g[0].shape[d] // N_CHIPS for g, d in zip(grads, dims)]
    land = []
    for g, d, sz in zip(grads, dims, sizes):
        shp = g[0].shape
        land.append(jax.ShapeDtypeStruct((N_CHIPS, depth) + shp[:d] + (sz,) + shp[d + 1:], g[0].dtype))

    def body(*refs):
        ins, outs = refs[:n * depth], refs[n * depth:n * depth + n]
        send_sems, recv_sems, local_sems = refs[n * depth + n:]
        _, k, peers, chips = _place()

        def src(w, l, q):
            return _window(ins[w * depth + l], dims[w], q, sizes[w])

        local = [pltpu.make_async_copy(src(w, l, k), outs[w].at[3, l], local_sems.at[w * depth + l])
                 for w in range(n) for l in range(depth)]
        for cp in local:
            cp.start()
        sends = []
        for p, peer in enumerate(peers):
            for w in range(n):
                for l in range(depth):
                    s = (p * n + w) * depth + l
                    sends.append(pltpu.make_async_remote_copy(
                        src_ref=src(w, l, chips[p]), dst_ref=outs[w].at[p, l], send_sem=send_sems.at[s],
                        recv_sem=recv_sems.at[s], device_id=peer, device_id_type=MESH))
        for cp in sends:
            cp.start()
        for cp in sends:
            cp.wait_recv()
        for cp in sends:
            cp.wait_send()
        for cp in local:
            cp.wait()

    flat = [g for gl in grads for g in gl]
    return pl.pallas_call(
        body,
        in_specs=[ANY] * (n * depth),
        out_specs=[ANY] * n,
        out_shape=land,
        scratch_shapes=[pltpu.SemaphoreType.DMA((3 * n * depth,)), pltpu.SemaphoreType.DMA((3 * n * depth,)),
                        pltpu.SemaphoreType.DMA((n * depth,))],
        name="scatter_grads",
    )(*flat)


def _sum_slots(name, land):
    _, R, C = land.shape
    tr, tc = _div_tile(R, 512, 8), _div_tile(C, 1024)

    def body(a_ref, b_ref, c_ref, d_ref, o_ref):
        o_ref[...] = ((d_ref[...].astype(F32) + a_ref[...].astype(F32)) + b_ref[...].astype(F32)) \
            + c_ref[...].astype(F32)

    def slot(q):
        return pl.BlockSpec((None, tr, tc), lambda i, j: (q, i, j))

    return pl.pallas_call(
        body,
        grid=(R // tr, C // tc),
        in_specs=[slot(0), slot(1), slot(2), slot(3)],
        out_specs=pl.BlockSpec((tr, tc), lambda i, j: (i, j)),
        out_shape=jax.ShapeDtypeStruct((R, C), F32),
        compiler_params=_cparams(("parallel", "parallel")),
        name=name,
    )(land, land, land, land)


def _swap_with_sibling(parts):
    n = len(parts)

    def body(*refs):
        ins, outs = refs[:n], refs[n:2 * n]
        send_sems, recv_sems = refs[2 * n:]
        (x, y, c), _, _, _ = _place()
        copies = [pltpu.make_async_remote_copy(
            src_ref=ins[w], dst_ref=outs[w], send_sem=send_sems.at[w], recv_sem=recv_sems.at[w],
            device_id=(x, y, 1 - c), device_id_type=MESH) for w in range(n)]
        for cp in copies:
            cp.start()
        for cp in copies:
            cp.wait()

    return pl.pallas_call(
        body,
        in_specs=[ANY] * n,
        out_specs=[ANY] * n,
        out_shape=[jax.ShapeDtypeStruct(p.shape, p.dtype) for p in parts],
        scratch_shapes=[pltpu.SemaphoreType.DMA((n,)), pltpu.SemaphoreType.DMA((n,))],
        name="swap_with_sibling",
    )(*parts)


def _allreduce_small(pack):
    R, C = pack.shape

    def gather_body(in_ref, slots_ref, send_sems, recv_sems, local_sem):
        x, y, c = lax.axis_index("x"), lax.axis_index("y"), lax.axis_index("c")
        me = 4 * x + 2 * y + c
        flips = [(dx, dy, dc) for dx in (0, 1) for dy in (0, 1) for dc in (0, 1)][1:]

        def flip(v, d):
            return 1 - v if d else v

        local = pltpu.make_async_copy(in_ref, slots_ref.at[me], local_sem)
        local.start()
        sends = []
        for j, (dx, dy, dc) in enumerate(flips):
            px, py, pc = flip(x, dx), flip(y, dy), flip(c, dc)
            sends.append((pltpu.make_async_remote_copy(
                src_ref=in_ref, dst_ref=slots_ref.at[me], send_sem=send_sems.at[j], recv_sem=recv_sems.at[j],
                device_id=(px, py, pc), device_id_type=MESH), 4 * px + 2 * py + pc, j))
        for cp, _, _ in sends:
            cp.start()
        for cp, peer_id, j in sends:
            pltpu.make_async_remote_copy(
                src_ref=in_ref, dst_ref=slots_ref.at[peer_id], send_sem=send_sems.at[j], recv_sem=recv_sems.at[j],
                device_id=(x, y, c), device_id_type=MESH).wait_recv()
        for cp, _, _ in sends:
            cp.wait_send()
        local.wait()

    slots = pl.pallas_call(
        gather_body,
        in_specs=[ANY],
        out_specs=ANY,
        out_shape=jax.ShapeDtypeStruct((N_DEV, R, C), pack.dtype),
        scratch_shapes=[pltpu.SemaphoreType.DMA((N_DEV - 1,)), pltpu.SemaphoreType.DMA((N_DEV - 1,)),
                        pltpu.SemaphoreType.DMA],
        name="allgather_small",
    )(pack)

    def sum_body(s_ref, o_ref):
        acc = s_ref[0]
        for d in range(1, N_DEV):
            acc = acc + s_ref[d]
        o_ref[...] = acc

    return pl.pallas_call(
        sum_body,
        out_shape=jax.ShapeDtypeStruct((R, C), pack.dtype),
        name="sum_small",
    )(slots)


def _adamw_math(w, g, m, v):
    m2 = ADAM_B1 * m + (1.0 - ADAM_B1) * g
    v2 = ADAM_B2 * v + (1.0 - ADAM_B2) * (g * g)
    m_hat = m2 / (1.0 - ADAM_B1 ** ADAM_STEP)
    v_hat = v2 / (1.0 - ADAM_B2 ** ADAM_STEP)
    delta = -ADAM_LR * (m_hat / (jnp.sqrt(v_hat) + ADAM_EPS) + ADAM_WD * w)
    return delta, m2, v2


def _adamw_big(name, w, m, v, s_mine, s_sib):
    shape = w.shape
    C = shape[-1]
    R = w.size // C
    tr, tc = _div_tile(R, 256, 8), _div_tile(C, 1024)

    def body(w_ref, m_ref, v_ref, a_ref, b_ref, g_ref, d_ref, nm_ref, nv_ref):
        g = a_ref[...] + b_ref[...]
        delta, m2, v2 = _adamw_math(w_ref[...], g, m_ref[...], v_ref[...])
        g_ref[...], d_ref[...], nm_ref[...], nv_ref[...] = g, delta, m2, v2

    blk = pl.BlockSpec((tr, tc), lambda i, j: (i, j))
    outs = pl.pallas_call(
        body,
        grid=(R // tr, C // tc),
        in_specs=[blk] * 5,
        out_specs=[blk] * 4,
        out_shape=[jax.ShapeDtypeStruct((R, C), F32)] * 4,
        compiler_params=_cparams(("parallel", "parallel")),
        name=name,
    )(w.reshape(R, C), m.reshape(R, C), v.reshape(R, C), s_mine.reshape(R, C), s_sib.reshape(R, C))
    return [o.reshape(shape) for o in outs]


def _adamw_small(name, w, g, m, v):
    shape = w.shape
    two_d = (w.size // shape[-1], shape[-1])

    def body(w_ref, g_ref, m_ref, v_ref, d_ref, nm_ref, nv_ref):
        d_ref[...], nm_ref[...], nv_ref[...] = _adamw_math(w_ref[...], g_ref[...], m_ref[...], v_ref[...])

    outs = pl.pallas_call(
        body,
        out_shape=[jax.ShapeDtypeStruct(two_d, F32)] * 3,
        name=name,
    )(w.reshape(two_d), g.reshape(two_d), m.reshape(two_d), v.reshape(two_d))
    return [o.reshape(shape) for o in outs]


SMALL_ROWS = 40
S_BIN, S_G1, S_G2 = 16, 24, 32
BIG = ("w_in", "w_pa", "w_pb", "w_o", "w_mlp1", "w_mlp2", "lru_wr", "lru_wi")
BIG_DIM = dict(w_in=2, w_pa=1, w_pb=1, w_o=1, w_mlp1=2, w_mlp2=1, lru_wr=2, lru_wi=2)
WEIGHTS = ("norm1_g", "w_in", "b_in", "conv_a_w", "conv_a_b", "lru_wr", "lru_br", "lru_wi", "lru_bi", "lru_lam",
           "conv_b_w", "w_pa", "w_pb", "w_o", "norm2_g", "w_mlp1", "w_mlp2", "final_g")


def _rows_at(a, r0, total):
    pad = [(0, 0)] * a.ndim
    pad[-2] = (r0, total - r0 - a.shape[-2])
    return jnp.pad(a, pad)


def kernel(x, norm1_g, w_in, b_in, conv_a_w, conv_a_b, lru_wr, lru_br, lru_wi, lru_bi, lru_lam, conv_b_w, w_pa, w_pb, w_o, norm2_g, w_mlp1, w_mlp2, final_g, loss_target, m_norm1_g, m_w_in, m_b_in, m_conv_a_w, m_conv_a_b, m_lru_wr, m_lru_br, m_lru_wi, m_lru_bi, m_lru_lam, m_conv_b_w, m_w_pa, m_w_pb, m_w_o, m_norm2_g, m_w_mlp1, m_w_mlp2, m_final_g, v_norm1_g, v_w_in, v_b_in, v_conv_a_w, v_conv_a_b, v_lru_wr, v_lru_br, v_lru_wi, v_lru_bi, v_lru_lam, v_conv_b_w, v_w_pa, v_w_pb, v_w_o, v_norm2_g, v_w_mlp1, v_w_mlp2, v_final_g):
    wts = dict(norm1_g=norm1_g, w_in=w_in, b_in=b_in, conv_a_w=conv_a_w, conv_a_b=conv_a_b, lru_wr=lru_wr,
               lru_br=lru_br, lru_wi=lru_wi, lru_bi=lru_bi, lru_lam=lru_lam, conv_b_w=conv_b_w, w_pa=w_pa,
               w_pb=w_pb, w_o=w_o, norm2_g=norm2_g, w_mlp1=w_mlp1, w_mlp2=w_mlp2, final_g=final_g)
    mom = dict(norm1_g=m_norm1_g, w_in=m_w_in, b_in=m_b_in, conv_a_w=m_conv_a_w, conv_a_b=m_conv_a_b,
               lru_wr=m_lru_wr, lru_br=m_lru_br, lru_wi=m_lru_wi, lru_bi=m_lru_bi, lru_lam=m_lru_lam,
               conv_b_w=m_conv_b_w, w_pa=m_w_pa, w_pb=m_w_pb, w_o=m_w_o, norm2_g=m_norm2_g, w_mlp1=m_w_mlp1,
               w_mlp2=m_w_mlp2, final_g=m_final_g)
    vel = dict(norm1_g=v_norm1_g, w_in=v_w_in, b_in=v_b_in, conv_a_w=v_conv_a_w, conv_a_b=v_conv_a_b,
               lru_wr=v_lru_wr, lru_br=v_lru_br, lru_wi=v_lru_wi, lru_bi=v_lru_bi, lru_lam=v_lru_lam,
               conv_b_w=v_conv_b_w, w_pa=v_w_pa, w_pb=v_w_pb, w_o=v_w_o, norm2_g=v_norm2_g, w_mlp1=v_w_mlp1,
               w_mlp2=v_w_mlp2, final_g=v_final_g)
    depth, D = norm1_g.shape
    nb, bw = lru_wr.shape[1], lru_wr.shape[3]
    chip = 2 * lax.axis_index("x") + lax.axis_index("y")

    small_parts = [conv_a_w.reshape(-1), conv_b_w.reshape(-1), lru_br.reshape(-1), lru_bi.reshape(-1)]
    small_len = sum(p.shape[0] for p in small_parts)
    small_rows = -(-small_len // 1024) * 8
    small = jnp.concatenate(small_parts + [jnp.zeros((small_rows * 128 - small_len,), F32)]).reshape(small_rows, 128)
    gathered = _gather_weights([wts[n].astype(BF16) for n in BIG], [BIG_DIM[n] for n in BIG], small)
    full = dict(zip(BIG, gathered[:-1]))
    flat = gathered[-1].reshape(N_CHIPS, small_rows * 128)
    off = 0
    small_full = []
    for part, shard in zip(small_parts, (conv_a_w, conv_b_w, lru_br, lru_bi)):
        piece = flat[:, off:off + part.shape[0]].reshape((N_CHIPS,) + shard.shape)
        small_full.append(jnp.moveaxis(piece, 0, -2).reshape(shard.shape[:-1] + (N_CHIPS * shard.shape[-1],)))
        off += part.shape[0]
    caw_f, cbw_f, br_f, bi_f = small_full
    pch = (_rows_at(conv_a_b[:, None, :], R_CAB, R_ROWS) + _rows_at(br_f.reshape(depth, 1, D), R_BR, R_ROWS)
           + _rows_at(bi_f.reshape(depth, 1, D), R_BI, R_ROWS) + _rows_at(lru_lam[:, None, :], R_LAM, R_ROWS)
           + _rows_at(caw_f, R_CAW, R_ROWS) + _rows_at(cbw_f, R_CBW, R_ROWS))
    W = dict(w_in=full["w_in"], b_in=b_in, pch=pch, wr=full["lru_wr"], wi=full["lru_wi"], w_pa=full["w_pa"],
             w_pb=full["w_pb"], w_o=full["w_o"], w1=full["w_mlp1"], w2=full["w_mlp2"], g1=norm1_g, g2=norm2_g,
             gf=final_g)

    loss_local, dx, grads, dgf = _local_fwd_bwd(x[0], loss_target[0], W)
    loss = lax.psum(loss_local, ("x", "y", "c"))

    key = dict(w_in="w_in", w_pa="w_pa", w_pb="w_pb", w_o="w_o", w_mlp1="w1", w_mlp2="w2", lru_wr="wr", lru_wi="wi")
    per_layer = [[grads[l][key[n]].astype(BF16) for l in range(depth)] for n in BIG]
    land = _scatter_grads(per_layer, [BIG_DIM[n] - 1 for n in BIG])
    chip_sums = []
    for n, ld in zip(BIG, land):
        cols = ld.shape[-1]
        chip_sums.append(_sum_slots(f"sum_slots_{n}", ld.reshape(N_CHIPS, -1, cols)))
    sib_sums = _swap_with_sibling(chip_sums)
    out_g, out_d, out_m, out_v = {}, {}, {}, {}
    for n, mine, sib in zip(BIG, chip_sums, sib_sums):
        out_g[n], out_d[n], out_m[n], out_v[n] = _adamw_big(f"adamw_{n}", wts[n], mom[n], vel[n], mine, sib)

    rows = []
    for l in range(depth):
        g = grads[l]
        rows.append(_rows_at(g["sm"], 0, SMALL_ROWS) + _rows_at(g["b_in"].reshape(7, D), S_BIN, SMALL_ROWS)
                    + _rows_at(g["g1"], S_G1, SMALL_ROWS) + _rows_at(g["g2"], S_G2, SMALL_ROWS))
    rows.append(_rows_at(dgf, 0, 8))
    tot = _allreduce_small(jnp.concatenate(rows, axis=0))
    per = tot[:depth * SMALL_ROWS].reshape(depth, SMALL_ROWS, D)

    def cols_of_chip(a, axis):
        size = a.shape[axis] // N_CHIPS
        return lax.dynamic_slice_in_dim(a, chip * size, size, axis=axis)

    small_g = dict(
        norm1_g=per[:, S_G1], b_in=per[:, S_BIN:S_BIN + 7].reshape(depth, 7 * D),
        conv_a_w=cols_of_chip(per[:, R_CAW:R_CAW + 4], 2), conv_a_b=per[:, R_CAB],
        lru_br=cols_of_chip(per[:, R_BR].reshape(depth, nb, bw), 2),
        lru_bi=cols_of_chip(per[:, R_BI].reshape(depth, nb, bw), 2), lru_lam=per[:, R_LAM],
        conv_b_w=cols_of_chip(per[:, R_CBW:R_CBW + 3], 2), norm2_g=per[:, S_G2],
        final_g=tot[depth * SMALL_ROWS])
    for n, g in small_g.items():
        out_g[n] = g
        out_d[n], out_m[n], out_v[n] = _adamw_small(f"adamw_{n}", wts[n], g, mom[n], vel[n])

    return (loss, dx[None], *[out_g[n] for n in WEIGHTS], *[out_d[n] for n in WEIGHTS],
            *[out_m[n] for n in WEIGHTS], *[out_v[n] for n in WEIGHTS])
```

```python
import functools

import jax
import jax.numpy as jnp
from jax import lax
from jax.experimental import pallas as pl
from jax.experimental.pallas import tpu as pltpu

F32 = jnp.float32
BF16 = jnp.bfloat16
MESH = pl.DeviceIdType.MESH

EPS = 1e-6
LRU_C = 8.0
ADAM_LR = 0.001
ADAM_B1 = 0.9
ADAM_B2 = 0.999
ADAM_EPS = 1e-08
ADAM_WD = 0.01
ADAM_STEP = 10

N_CHIPS = 4
N_DEV = 8
HALO = 8
VMEM_LIMIT = 56 * 1024 * 1024
MM_TILES = (1024, 1024, 1024)
MM_TILES_FUSED = (512, 1024, 1024)
SEQ_CHUNK = 256
ROW_TILE = 256

R_CAB, R_BR, R_BI, R_LAM, R_CAW, R_CBW, R_ROWS = 0, 1, 2, 3, 4, 8, 16


def _cparams(sem):
    return pltpu.CompilerParams(dimension_semantics=sem, vmem_limit_bytes=VMEM_LIMIT)


def _div_tile(n, pref, unit=128):
    if n <= pref:
        return n
    t = (pref // unit) * unit
    while n % t:
        t -= unit
    return t


def _sigmoid(v):
    return 1.0 / (1.0 + jnp.exp(-v))


def _gelu_and_grad(y):
    k = 0.7978845608028654
    c = 0.044715
    y2 = y * y
    t = jnp.tanh(k * (y + c * y2 * y))
    g = 0.5 * y * (1.0 + t)
    gp = 0.5 * (1.0 + t) + 0.5 * y * (1.0 - t * t) * (k * (1.0 + 3.0 * c * y2))
    return g, gp


def _softplus_neg(lam):
    e = jnp.exp(-jnp.abs(lam))
    w = 1.0 + e
    l1p = jnp.where(w == 1.0, e, jnp.log(w) * e / jnp.where(w == 1.0, 1.0, w - 1.0))
    return jnp.maximum(-lam, 0.0) + l1p


def _mm(name, mode, a, b, M, N, K, out_dtypes, epilogue=None, extras=(), la=None, lb=None, tiles=None):
    tiles = MM_TILES if tiles is None else tiles
    tm, tn, tk = _div_tile(M, tiles[0]), _div_tile(N, tiles[1]), _div_tile(K, tiles[2])
    assert M % tm == 0 and N % tn == 0 and K % tk == 0, (name, M, N, K)
    nk = K // tk

    def spec(lead, shape, imap):
        if lead is None:
            return pl.BlockSpec(shape, imap)
        return pl.BlockSpec((None,) + shape, lambda i, j, k: (lead,) + imap(i, j, k))

    if mode == "nn":
        a_spec = spec(la, (tm, tk), lambda i, j, k: (i, k))
        b_spec = spec(lb, (tk, tn), lambda i, j, k: (k, j))
        dn = (((1,), (0,)), ((), ()))
    elif mode == "nt":
        a_spec = spec(la, (tm, tk), lambda i, j, k: (i, k))
        b_spec = spec(lb, (tn, tk), lambda i, j, k: (j, k))
        dn = (((1,), (1,)), ((), ()))
    else:
        a_spec = spec(la, (tk, tm), lambda i, j, k: (k, i))
        b_spec = spec(lb, (tk, tn), lambda i, j, k: (k, j))
        dn = (((0,), (0,)), ((), ()))

    ex_arrays, ex_specs = [], []
    for arr, kind, off in extras:
        ex_arrays.append(arr)
        if kind == "bias":
            ex_specs.append(pl.BlockSpec((1, tn), lambda i, j, k: (0, j)))
        else:
            assert off % tn == 0
            ex_specs.append(pl.BlockSpec((tm, tn), lambda i, j, k, o=off // tn: (i, j + o)))
    n_ex, n_out = len(ex_arrays), len(out_dtypes)

    def body(*refs):
        a_ref, b_ref = refs[0], refs[1]
        ex = refs[2:2 + n_ex]
        outs = refs[2 + n_ex:2 + n_ex + n_out]
        acc = refs[-1]
        k = pl.program_id(2)

        @pl.when(k == 0)
        def _():
            acc[...] = jnp.zeros_like(acc)

        acc[...] += lax.dot_general(a_ref[...], b_ref[...], dn, preferred_element_type=F32)

        @pl.when(k == nk - 1)
        def _():
            r = acc[...]
            vals = (r,) if epilogue is None else epilogue(r, *[e[...] for e in ex])
            for o, v in zip(outs, vals):
                o[...] = v.astype(o.dtype)

    res = pl.pallas_call(
        body,
        grid=(M // tm, N // tn, nk),
        in_specs=[a_spec, b_spec, *ex_specs],
        out_specs=[pl.BlockSpec((tm, tn), lambda i, j, k: (i, j)) for _ in range(n_out)],
        out_shape=[jax.ShapeDtypeStruct((M, N), d) for d in out_dtypes],
        scratch_shapes=[pltpu.VMEM((tm, tn), F32)],
        compiler_params=_cparams(("parallel", "parallel", "arbitrary")),
        name=name,
    )(a, b, *ex_arrays)
    return res[0] if n_out == 1 else res


def _rms_fwd(name, x, g_row):
    T, D = x.shape
    tm = min(ROW_TILE, T)

    def body(x_ref, g_ref, h_ref):
        xv = x_ref[...]
        r = lax.rsqrt(jnp.mean(xv * xv, axis=-1, keepdims=True) + EPS)
        h_ref[...] = (xv * r * g_ref[...]).astype(BF16)

    return pl.pallas_call(
        body,
        grid=(T // tm,),
        in_specs=[pl.BlockSpec((tm, D), lambda i: (i, 0)), pl.BlockSpec((1, D), lambda i: (0, 0))],
        out_specs=pl.BlockSpec((tm, D), lambda i: (i, 0)),
        out_shape=jax.ShapeDtypeStruct((T, D), BF16),
        compiler_params=_cparams(("parallel",)),
        name=name,
    )(x, g_row)


def _rms_bwd(name, x, g_row, dh, dres):
    T, D = x.shape
    tm = min(ROW_TILE, T)

    def body(x_ref, g_ref, dh_ref, dres_ref, dx_ref, dxb_ref, dg_ref):
        xv, dhv = x_ref[...], dh_ref[...]
        r = lax.rsqrt(jnp.mean(xv * xv, axis=-1, keepdims=True) + EPS)
        gd = g_ref[...] * dhv
        c = jnp.mean(xv * gd, axis=-1, keepdims=True)
        dx = r * gd - xv * (r * r * r) * c + dres_ref[...]
        dx_ref[...] = dx
        dxb_ref[...] = dx.astype(BF16)

        @pl.when(pl.program_id(0) == 0)
        def _():
            dg_ref[...] = jnp.zeros_like(dg_ref)

        dg_ref[...] += jnp.sum(dhv * xv * r, axis=0, keepdims=True)

    row = pl.BlockSpec((tm, D), lambda i: (i, 0))
    vec = pl.BlockSpec((1, D), lambda i: (0, 0))
    return pl.pallas_call(
        body,
        grid=(T // tm,),
        in_specs=[row, vec, row, row],
        out_specs=[row, row, vec],
        out_shape=[jax.ShapeDtypeStruct((T, D), F32), jax.ShapeDtypeStruct((T, D), BF16),
                   jax.ShapeDtypeStruct((1, D), F32)],
        compiler_params=_cparams(("arbitrary",)),
        name=name,
    )(x, g_row, dh, dres)


def _loss_head(name, x, g_row, tgt):
    T, D = x.shape
    tm = min(ROW_TILE, T)

    def body(x_ref, g_ref, t_ref, dx_ref, dxb_ref, dg_ref, loss_ref):
        xv, g = x_ref[...], g_ref[...]
        r = lax.rsqrt(jnp.mean(xv * xv, axis=-1, keepdims=True) + EPS)
        xh = xv * r
        e = xh * g - t_ref[...]
        lpart = 0.5 * jnp.sum(jnp.mean(e * e, axis=-1, keepdims=True))
        dy = e * (1.0 / D)
        gd = g * dy
        c = jnp.mean(xv * gd, axis=-1, keepdims=True)
        dx = r * gd - xv * (r * r * r) * c
        dx_ref[...] = dx
        dxb_ref[...] = dx.astype(BF16)

        @pl.when(pl.program_id(0) == 0)
        def _():
            dg_ref[...] = jnp.zeros_like(dg_ref)
            loss_ref[...] = jnp.zeros_like(loss_ref)

        dg_ref[...] += jnp.sum(dy * xh, axis=0, keepdims=True)
        loss_ref[...] += jnp.full(loss_ref.shape, lpart, F32)

    row = pl.BlockSpec((tm, D), lambda i: (i, 0))
    vec = pl.BlockSpec((1, D), lambda i: (0, 0))
    return pl.pallas_call(
        body,
        grid=(T // tm,),
        in_specs=[row, vec, row],
        out_specs=[row, row, vec, pl.BlockSpec((8, 128), lambda i: (0, 0))],
        out_shape=[jax.ShapeDtypeStruct((T, D), F32), jax.ShapeDtypeStruct((T, D), BF16),
                   jax.ShapeDtypeStruct((1, D), F32), jax.ShapeDtypeStruct((8, 128), F32)],
        compiler_params=_cparams(("arbitrary",)),
        name=name,
    )(x, g_row, tgt)


def _colsum(name, a):
    T, N = a.shape
    tm, tn = min(512, T), _div_tile(N, 2048)

    def body(a_ref, o_ref):
        @pl.when(pl.program_id(1) == 0)
        def _():
            o_ref[...] = jnp.zeros_like(o_ref)

        o_ref[...] += jnp.sum(a_ref[...].astype(F32), axis=0, keepdims=True)

    return pl.pallas_call(
        body,
        grid=(N // tn, T // tm),
        in_specs=[pl.BlockSpec((tm, tn), lambda j, i: (i, j))],
        out_specs=pl.BlockSpec((1, tn), lambda j, i: (0, j)),
        out_shape=jax.ShapeDtypeStruct((1, N), F32),
        compiler_params=_cparams(("parallel", "arbitrary")),
        name=name,
    )(a)


def _tile_scan(a, b, row, reverse):
    for s in (1, 2, 4):
        if reverse:
            a_s, b_s, m = pltpu.roll(a, 8 - s, 0), pltpu.roll(b, 8 - s, 0), row < 8 - s
        else:
            a_s, b_s, m = pltpu.roll(a, s, 0), pltpu.roll(b, s, 0), row >= s
        b = jnp.where(m, a * b_s + b, b)
        a = jnp.where(m, a * a_s, a)
    return a, b


def _chunk_scan(a_s, b_s, out_ref, carry, n_tiles, width, reverse):
    row = lax.broadcasted_iota(jnp.int32, (8, width), 0)
    edge = 0 if reverse else 7

    def step(j, c):
        jj = (n_tiles - 1 - j) if reverse else j
        o = pl.multiple_of(jj * 8, 8)
        ca, cb = _tile_scan(a_s[pl.ds(o, 8), :], b_s[pl.ds(o, 8), :], row, reverse)
        h = ca * c + cb
        out_ref[pl.ds(o, 8), :] = h
        return jnp.broadcast_to(h[edge:edge + 1, :], (8, width))

    carry[...] = lax.fori_loop(0, n_tiles, step, carry[...])


def _gates(xc, p_ref, wr_ref, wi_ref):
    xcb = xc.astype(BF16)
    r = _sigmoid(jnp.dot(xcb, wr_ref[...], preferred_element_type=F32) + p_ref[R_BR:R_BR + 1, :])
    ig = _sigmoid(jnp.dot(xcb, wi_ref[...], preferred_element_type=F32) + p_ref[R_BI:R_BI + 1, :])
    sp = _softplus_neg(p_ref[R_LAM:R_LAM + 1, :])
    log_a = (-LRU_C) * r * sp
    a = jnp.exp(log_a)
    t = jnp.tanh(log_a)
    mult = jnp.sqrt(-2.0 * t / (1.0 - t))
    return xcb, r, ig, sp, a, mult


def _mixer_specs(Tc, bw, nb, layer):
    def seg(s):
        return pl.BlockSpec((Tc, bw), lambda n, i: (i, s * nb + n))

    p_spec = pl.BlockSpec((None, R_ROWS, bw), lambda n, i: (layer, 0, n))
    w_spec = pl.BlockSpec((None, None, bw, bw), lambda n, i: (layer, n, 0, 0))
    return seg, p_spec, w_spec


def _mixer_fwd(name, layer, z, pch, wr, wi):
    T, D = z.shape[0], z.shape[1] // 7
    bw, nb = wr.shape[-1], wr.shape[1]
    Tc = min(SEQ_CHUNK, T)
    nT = T // Tc

    def body(xa_ref, ya_ref, cb_ref, cc_ref, cx_ref, p_ref, wr_ref, wi_ref,
             pa_ref, pb_ref, xc_ref, hl_ref, vb_ref, xa_buf, u_buf, a_s, b_s, carry):
        @pl.when(pl.program_id(1) == 0)
        def _():
            xa_buf[0:HALO, :] = jnp.zeros((HALO, bw), F32)
            u_buf[0:HALO, :] = jnp.zeros((HALO, bw), F32)
            carry[...] = jnp.zeros_like(carry)

        xa_buf[HALO:HALO + Tc, :] = xa_ref[...]
        xc = p_ref[R_CAB:R_CAB + 1, :]
        for k in range(4):
            xc = xc + p_ref[R_CAW + k:R_CAW + k + 1, :] * xa_buf[HALO - 3 + k:HALO - 3 + k + Tc, :]
        xc_ref[...] = xc
        _, _, ig, _, a, mult = _gates(xc, p_ref, wr_ref, wi_ref)
        a_s[...] = a
        b_s[...] = mult * (ig * xc)
        _chunk_scan(a_s, b_s, hl_ref, carry, Tc // 8, bw, False)
        g, _ = _gelu_and_grad(ya_ref[...])
        pa_ref[...] = (hl_ref[...] * g).astype(BF16)

        u_buf[HALO:HALO + Tc, :] = cc_ref[...] * cx_ref[...]
        vb = jnp.zeros((Tc, bw), F32)
        for k in range(3):
            vb = vb + p_ref[R_CBW + k:R_CBW + k + 1, :] * u_buf[HALO - 2 + k:HALO - 2 + k + Tc, :]
        vb_ref[...] = vb
        pb_ref[...] = (cb_ref[...] * vb).astype(BF16)
        xa_buf[0:HALO, :] = xa_buf[Tc:Tc + HALO, :]
        u_buf[0:HALO, :] = u_buf[Tc:Tc + HALO, :]

    seg, p_spec, w_spec = _mixer_specs(Tc, bw, nb, layer)
    out = pl.BlockSpec((Tc, bw), lambda n, i: (i, n))
    return pl.pallas_call(
        body,
        grid=(nb, nT),
        in_specs=[seg(0), seg(1), seg(2), seg(3), seg(4), p_spec, w_spec, w_spec],
        out_specs=[out] * 5,
        out_shape=[jax.ShapeDtypeStruct((T, D), BF16), jax.ShapeDtypeStruct((T, D), BF16),
                   jax.ShapeDtypeStruct((T, D), F32), jax.ShapeDtypeStruct((T, D), F32),
                   jax.ShapeDtypeStruct((T, D), F32)],
        scratch_shapes=[pltpu.VMEM((Tc + HALO, bw), F32), pltpu.VMEM((Tc + HALO, bw), F32),
                        pltpu.VMEM((Tc, bw), F32), pltpu.VMEM((Tc, bw), F32), pltpu.VMEM((8, bw), F32)],
        compiler_params=_cparams(("parallel", "arbitrary")),
        name=name,
    )(z, z, z, z, z, pch, wr, wi)


def _mixer_bwd(name, layer, z, xc, hl, vb, dpa, dpb, pch, wr, wi):
    T, D = z.shape[0], z.shape[1] // 7
    bw, nb = wr.shape[-1], wr.shape[1]
    Tc = min(SEQ_CHUNK, T)
    nT = T // Tc
    tpc = Tc // 8

    def body(xa_ref, ya_ref, cb_ref, cc_ref, cx_ref, xc_ref, hl_ref, hp_ref, vb_ref, dpa_ref, dpb_ref,
             p_ref, wr_ref, wi_ref,
             dxa_ref, dya_ref, dcb_ref, dcc_ref, dcx_ref, dwr_ref, dwi_ref, sm_ref,
             h_buf, a_buf, dxc_buf, dvb_buf, a_s, d_s, lam_s, carry):
        i = pl.program_id(1)

        @pl.when(i == 0)
        def _():
            a_buf[Tc:Tc + HALO, :] = jnp.zeros((HALO, bw), F32)
            dxc_buf[Tc:Tc + HALO, :] = jnp.zeros((HALO, bw), F32)
            dvb_buf[Tc:Tc + HALO, :] = jnp.zeros((HALO, bw), F32)
            carry[...] = jnp.zeros_like(carry)
            dwr_ref[...] = jnp.zeros_like(dwr_ref)
            dwi_ref[...] = jnp.zeros_like(dwi_ref)
            sm_ref[...] = jnp.zeros_like(sm_ref)

        xcv = xc_ref[...]
        xcb, r, ig, sp, a, mult = _gates(xcv, p_ref, wr_ref, wi_ref)
        g, gp = _gelu_and_grad(ya_ref[...])
        hlv, dpav = hl_ref[...], dpa_ref[...]
        dya_ref[...] = (dpav * hlv * gp).astype(BF16)

        a_buf[0:Tc, :] = a
        a_s[...] = a_buf[1:Tc + 1, :]
        d_s[...] = dpav * g
        _chunk_scan(a_s, d_s, lam_s, carry, tpc, bw, True)
        lamv = lam_s[...]

        h_buf[HALO:HALO + Tc, :] = hlv
        h_buf[0:HALO, :] = jnp.where(i == nT - 1, 0.0, hp_ref[...])
        da = lamv * h_buf[HALO - 1:HALO - 1 + Tc, :]
        dmult = lamv * (ig * xcv)
        dbx = lamv * mult
        dig = dbx * xcv
        dxc = dbx * ig
        dlog_a = da * a - dmult * (a * a) / mult
        dpr = (dlog_a * ((-LRU_C) * sp)) * r * (1.0 - r)
        dpi = dig * ig * (1.0 - ig)
        dprb, dpib = dpr.astype(BF16), dpi.astype(BF16)
        nt = (((1,), (1,)), ((), ()))
        tn = (((0,), (0,)), ((), ()))
        dxc = dxc + lax.dot_general(dprb, wr_ref[...], nt, preferred_element_type=F32)
        dxc = dxc + lax.dot_general(dpib, wi_ref[...], nt, preferred_element_type=F32)
        dwr_ref[...] += lax.dot_general(xcb, dprb, tn, preferred_element_type=F32)
        dwi_ref[...] += lax.dot_general(xcb, dpib, tn, preferred_element_type=F32)

        def rowsum(v):
            return jnp.sum(v, axis=0, keepdims=True)

        sm_ref[R_CAB:R_CAB + 1, :] += rowsum(dxc)
        sm_ref[R_BR:R_BR + 1, :] += rowsum(dpr)
        sm_ref[R_BI:R_BI + 1, :] += rowsum(dpi)
        sm_ref[R_LAM:R_LAM + 1, :] += rowsum(dlog_a * ((-LRU_C) * r))

        dxc_buf[0:Tc, :] = dxc
        xav = xa_ref[...]
        dxa = jnp.zeros((Tc, bw), F32)
        for k in range(4):
            sh = dxc_buf[3 - k:3 - k + Tc, :]
            dxa = dxa + p_ref[R_CAW + k:R_CAW + k + 1, :] * sh
            sm_ref[R_CAW + k:R_CAW + k + 1, :] += rowsum(xav * sh)
        dxa_ref[...] = dxa.astype(BF16)

        dpbv, cbv, ccv, cxv = dpb_ref[...], cb_ref[...], cc_ref[...], cx_ref[...]
        dcb_ref[...] = (dpbv * vb_ref[...]).astype(BF16)
        dvb_buf[0:Tc, :] = dpbv * cbv
        u = ccv * cxv
        du = jnp.zeros((Tc, bw), F32)
        for k in range(3):
            sh = dvb_buf[2 - k:2 - k + Tc, :]
            du = du + p_ref[R_CBW + k:R_CBW + k + 1, :] * sh
            sm_ref[R_CBW + k:R_CBW + k + 1, :] += rowsum(u * sh)
        dcc_ref[...] = (du * cxv).astype(BF16)
        dcx_ref[...] = (du * ccv).astype(BF16)

        a_buf[Tc:Tc + HALO, :] = a_buf[0:HALO, :]
        dxc_buf[Tc:Tc + HALO, :] = dxc_buf[0:HALO, :]
        dvb_buf[Tc:Tc + HALO, :] = dvb_buf[0:HALO, :]

        @pl.when(i == nT - 1)
        def _():
            sm_ref[R_LAM:R_LAM + 1, :] = sm_ref[R_LAM:R_LAM + 1, :] * (-_sigmoid(-p_ref[R_LAM:R_LAM + 1, :]))

    def seg(s):
        return pl.BlockSpec((Tc, bw), lambda n, i: (nT - 1 - i, s * nb + n))

    blk = pl.BlockSpec((Tc, bw), lambda n, i: (nT - 1 - i, n))
    halo = pl.BlockSpec((8, bw), lambda n, i: (jnp.maximum((nT - 1 - i) * tpc - 1, 0), n))
    p_spec = pl.BlockSpec((None, R_ROWS, bw), lambda n, i: (layer, 0, n))
    w_spec = pl.BlockSpec((None, None, bw, bw), lambda n, i: (layer, n, 0, 0))
    dw_spec = pl.BlockSpec((None, bw, bw), lambda n, i: (n, 0, 0))
    act = jax.ShapeDtypeStruct((T, D), BF16)
    return pl.pallas_call(
        body,
        grid=(nb, nT),
        in_specs=[seg(0), seg(1), seg(2), seg(3), seg(4), blk, blk, halo, blk, blk, blk, p_spec, w_spec, w_spec],
        out_specs=[blk] * 5 + [dw_spec, dw_spec, pl.BlockSpec((R_ROWS, bw), lambda n, i: (0, n))],
        out_shape=[act] * 5 + [jax.ShapeDtypeStruct((nb, bw, bw), F32), jax.ShapeDtypeStruct((nb, bw, bw), F32),
                               jax.ShapeDtypeStruct((R_ROWS, D), F32)],
        scratch_shapes=[pltpu.VMEM((Tc + HALO, bw), F32)] * 4 + [pltpu.VMEM((Tc, bw), F32)] * 3
        + [pltpu.VMEM((8, bw), F32)],
        compiler_params=_cparams(("parallel", "arbitrary")),
        name=name,
    )(z, z, z, z, z, xc, hl, hl, vb, dpa, dpb, pch, wr, wi)


def _local_fwd_bwd(x, tgt, W):
    T, D = x.shape
    depth = len(W["w_in"])
    FF = W["w1"][0].shape[1]
    g1, g2 = W["g1"], W["g2"]
    saved = []
    xs = x
    for l in range(depth):
        h = _rms_fwd(f"rms1_fwd_{l}", xs, g1[l][None])
        z = _mm(f"in_proj_{l}", "nn", h, W["w_in"][l], T, 7 * D, D, [F32],
                epilogue=lambda acc, b: (acc + b,), extras=[(W["b_in"][l][None], "bias", 0)])
        pa, pb, xc, hl, vb = _mixer_fwd(f"mixer_fwd_{l}", l, z, W["pch"], W["wr"], W["wi"])
        oa = _mm(f"proj_a_{l}", "nn", pa, W["w_pa"][l], T, D, D, [F32])

        def merge(acc, oav, ga, gb):
            return acc, _sigmoid(ga) * oav + _sigmoid(gb) * acc

        ob, mg = _mm(f"proj_b_merge_{l}", "nn", pb, W["w_pb"][l], T, D, D, [F32, BF16], epilogue=merge,
                     tiles=MM_TILES_FUSED,
                     extras=[(oa, "tile", 0), (z, "tile", 5 * D), (z, "tile", 6 * D)])
        x1 = _mm(f"out_proj_{l}", "nn", mg, W["w_o"][l], T, D, D, [F32],
                 epilogue=lambda acc, res: (res + acc,), extras=[(xs, "tile", 0)])
        h2 = _rms_fwd(f"rms2_fwd_{l}", x1, g2[l][None])

        def relu2(acc):
            pr = jnp.maximum(acc, 0.0)
            return pr * pr, pr

        u, pr = _mm(f"mlp1_{l}", "nn", h2, W["w1"][l], T, FF, D, [BF16, BF16], epilogue=relu2)
        x2 = _mm(f"mlp2_{l}", "nn", u, W["w2"][l], T, D, FF, [F32],
                 epilogue=lambda acc, res: (res + acc,), extras=[(x1, "tile", 0)])
        saved.append(dict(x0=xs, h=h, z=z, pa=pa, pb=pb, xc=xc, hl=hl, vb=vb, oa=oa, ob=ob, mg=mg, x1=x1,
                          h2=h2, u=u, pr=pr))
        xs = x2

    dx, dxb, dgf, loss_blk = _loss_head("loss_head", xs, W["gf"][None], tgt)
    grads = [None] * depth
    for l in reversed(range(depth)):
        s = saved[l]
        dp = _mm(f"mlp2_dx_{l}", "nt", dxb, W["w2"][l], T, FF, D, [BF16],
                 epilogue=lambda acc, prv: (2.0 * prv.astype(F32) * acc,), extras=[(s["pr"], "tile", 0)])
        dw2 = _mm(f"mlp2_dw_{l}", "tn", s["u"], dxb, FF, D, T, [BF16])
        dh2 = _mm(f"mlp1_dx_{l}", "nt", dp, W["w1"][l], T, D, FF, [F32])
        dw1 = _mm(f"mlp1_dw_{l}", "tn", s["h2"], dp, D, FF, T, [BF16])
        dx1, dx1b, dg2 = _rms_bwd(f"rms2_bwd_{l}", s["x1"], g2[l][None], dh2, dx)

        def unmerge(acc, ga, gb, oav, obv):
            sa, sb = _sigmoid(ga), _sigmoid(gb)
            return acc * sa, acc * sb, acc * oav * sa * (1.0 - sa), acc * obv * sb * (1.0 - sb)

        doa, dob, dga, dgb = _mm(f"out_proj_dx_{l}", "nt", dx1b, W["w_o"][l], T, D, D, [BF16] * 4,
                                 tiles=MM_TILES_FUSED,
                                 epilogue=unmerge,
                                 extras=[(s["z"], "tile", 5 * D), (s["z"], "tile", 6 * D),
                                         (s["oa"], "tile", 0), (s["ob"], "tile", 0)])
        dwo = _mm(f"out_proj_dw_{l}", "tn", s["mg"], dx1b, D, D, T, [BF16])
        dpa = _mm(f"proj_a_dx_{l}", "nt", doa, W["w_pa"][l], T, D, D, [F32])
        dwpa = _mm(f"proj_a_dw_{l}", "tn", s["pa"], doa, D, D, T, [BF16])
        dpb = _mm(f"proj_b_dx_{l}", "nt", dob, W["w_pb"][l], T, D, D, [F32])
        dwpb = _mm(f"proj_b_dw_{l}", "tn", s["pb"], dob, D, D, T, [BF16])
        dxa, dya, dcb, dcc, dcx, dwr, dwi, sm = _mixer_bwd(
            f"mixer_bwd_{l}", l, s["z"], s["xc"], s["hl"], s["vb"], dpa, dpb, W["pch"], W["wr"], W["wi"])
        dz = jnp.concatenate([dxa, dya, dcb, dcc, dcx, dga, dgb], axis=1)
        dbin = _colsum(f"bias_grad_{l}", dz)
        dh = _mm(f"in_proj_dx_{l}", "nt", dz, W["w_in"][l], T, D, 7 * D, [F32])
        dwin = _mm(f"in_proj_dw_{l}", "tn", s["h"], dz, D, 7 * D, T, [BF16])
        dx, dxb, dg1 = _rms_bwd(f"rms1_bwd_{l}", s["x0"], g1[l][None], dh, dx1)
        grads[l] = dict(w_in=dwin, w_pa=dwpa, w_pb=dwpb, w_o=dwo, w1=dw1, w2=dw2, wr=dwr, wi=dwi,
                        sm=sm, b_in=dbin, g1=dg1, g2=dg2)
    return loss_blk[0, 0], dx, grads, dgf


ANY = pl.BlockSpec(memory_space=pl.ANY)


def _place():
    x, y, c = lax.axis_index("x"), lax.axis_index("y"), lax.axis_index("c")
    peers = [(1 - x, y, c), (x, 1 - y, c), (1 - x, 1 - y, c)]
    chips = [2 * (1 - x) + y, 2 * x + (1 - y), 2 * (1 - x) + (1 - y)]
    return (x, y, c), 2 * x + y, peers, chips


def _window(ref, dim, q, size):
    idx = [slice(None)] * len(ref.shape)
    idx[dim] = pl.ds(q * size, size)
    return ref.at[tuple(idx)]


def _gather_weights(shards, dims, small):
    n = len(shards)
    sizes = [s.shape[d] for s, d in zip(shards, dims)]
    full = [jax.ShapeDtypeStruct(s.shape[:d] + (s.shape[d] * N_CHIPS,) + s.shape[d + 1:], s.dtype)
            for s, d in zip(shards, dims)]
    full.append(jax.ShapeDtypeStruct((N_CHIPS,) + small.shape, small.dtype))

    def body(*refs):
        ins, outs = refs[:n + 1], refs[n + 1:2 * n + 2]
        send_sems, recv_sems, local_sems = refs[2 * n + 2:]
        _, k, peers, chips = _place()

        def dst(w, q):
            return outs[w].at[q] if w == n else _window(outs[w], dims[w], q, sizes[w])

        local = [pltpu.make_async_copy(ins[w], dst(w, k), local_sems.at[w]) for w in range(n + 1)]
        for cp in local:
            cp.start()
        sends = []
        for p, peer in enumerate(peers):
            for w in range(n + 1):
                s = p * (n + 1) + w
                sends.append(pltpu.make_async_remote_copy(
                    src_ref=ins[w], dst_ref=dst(w, k), send_sem=send_sems.at[s], recv_sem=recv_sems.at[s],
                    device_id=peer, device_id_type=MESH))
        for cp in sends:
            cp.start()
        for p, peer in enumerate(peers):
            for w in range(n + 1):
                s = p * (n + 1) + w
                pltpu.make_async_remote_copy(
                    src_ref=ins[w], dst_ref=dst(w, chips[p]), send_sem=send_sems.at[s], recv_sem=recv_sems.at[s],
                    device_id=peer, device_id_type=MESH).wait_recv()
        for cp in sends:
            cp.wait_send()
        for cp in local:
            cp.wait()

    return pl.pallas_call(
        body,
        in_specs=[ANY] * (n + 1),
        out_specs=[ANY] * (n + 1),
        out_shape=full,
        scratch_shapes=[pltpu.SemaphoreType.DMA((3 * (n + 1),)), pltpu.SemaphoreType.DMA((3 * (n + 1),)),
                        pltpu.SemaphoreType.DMA((n + 1,))],
        name="gather_weights",
    )(*shards, small)


def _scatter_grads(grads, dims):
    n, depth = len(grads), len(grads[0])
    sizes = [g[0].shape[d] // N_CHIPS for g, d in zip(grads, dims)]
    land = []
    for g, d, sz in zip(grads, dims, sizes):
        shp = g[0].shape
        land.append(jax.ShapeDtypeStruct((N_CHIPS, depth) + shp[:d] + (sz,) + shp[d + 1:], g[0].dtype))

    def body(*refs):
        ins, outs = refs[:n * depth], refs[n * depth:n * depth + n]
        send_sems, recv_sems, local_sems = refs[n * depth + n:]
        _, k, peers, chips = _place()

        def src(w, l, q):
            return _window(ins[w * depth + l], dims[w], q, sizes[w])

        local = [pltpu.make_async_copy(src(w, l, k), outs[w].at[3, l], local_sems.at[w * depth + l])
                 for w in range(n) for l in range(depth)]
        for cp in local:
            cp.start()
        sends = []
        for p, peer in enumerate(peers):
            for w in range(n):
                for l in range(depth):
                    s = (p * n + w) * depth + l
                    sends.append(pltpu.make_async_remote_copy(
                        src_ref=src(w, l, chips[p]), dst_ref=outs[w].at[p, l], send_sem=send_sems.at[s],
                        recv_sem=recv_sems.at[s], device_id=peer, device_id_type=MESH))
        for cp in sends:
            cp.start()
        for cp in sends:
            cp.wait_recv()
        for cp in sends:
            cp.wait_send()
        for cp in local:
            cp.wait()

    flat = [g for gl in grads for g in gl]
    return pl.pallas_call(
        body,
        in_specs=[ANY] * (n * depth),
        out_specs=[ANY] * n,
        out_shape=land,
        scratch_shapes=[pltpu.SemaphoreType.DMA((3 * n * depth,)), pltpu.SemaphoreType.DMA((3 * n * depth,)),
                        pltpu.SemaphoreType.DMA((n * depth,))],
        name="scatter_grads",
    )(*flat)


def _sum_slots(name, land):
    _, R, C = land.shape
    tr, tc = _div_tile(R, 512, 8), _div_tile(C, 1024)

    def body(a_ref, b_ref, c_ref, d_ref, o_ref):
        o_ref[...] = ((d_ref[...].astype(F32) + a_ref[...].astype(F32)) + b_ref[...].astype(F32)) \
            + c_ref[...].astype(F32)

    def slot(q):
        return pl.BlockSpec((None, tr, tc), lambda i, j: (q, i, j))

    return pl.pallas_call(
        body,
        grid=(R // tr, C // tc),
        in_specs=[slot(0), slot(1), slot(2), slot(3)],
        out_specs=pl.BlockSpec((tr, tc), lambda i, j: (i, j)),
        out_shape=jax.ShapeDtypeStruct((R, C), F32),
        compiler_params=_cparams(("parallel", "parallel")),
        name=name,
    )(land, land, land, land)


def _swap_with_sibling(parts):
    n = len(parts)

    def body(*refs):
        ins, outs = refs[:n], refs[n:2 * n]
        send_sems, recv_sems = refs[2 * n:]
        (x, y, c), _, _, _ = _place()
        copies = [pltpu.make_async_remote_copy(
            src_ref=ins[w], dst_ref=outs[w], send_sem=send_sems.at[w], recv_sem=recv_sems.at[w],
            device_id=(x, y, 1 - c), device_id_type=MESH) for w in range(n)]
        for cp in copies:
            cp.start()
        for cp in copies:
            cp.wait()

    return pl.pallas_call(
        body,
        in_specs=[ANY] * n,
        out_specs=[ANY] * n,
        out_shape=[jax.ShapeDtypeStruct(p.shape, p.dtype) for p in parts],
        scratch_shapes=[pltpu.SemaphoreType.DMA((n,)), pltpu.SemaphoreType.DMA((n,))],
        name="swap_with_sibling",
    )(*parts)


def _comm_call(name, ins, out_shapes, plan, n_local, n_remote, aliases=None):
    n_in, n_out = len(ins), len(out_shapes)

    def body(*refs):
        in_refs, out_refs = refs[:n_in], refs[n_in:n_in + n_out]
        sems = refs[n_in + n_out:]
        local, remote = plan(in_refs, out_refs)
        assert len(local) == n_local and len(remote) == n_remote, (name, len(local), len(remote))
        lcs = [pltpu.make_async_copy(s, d, sems[2].at[i]) for i, (s, d) in enumerate(local)]
        for cp in lcs:
            cp.start()
        sends = [pltpu.make_async_remote_copy(src_ref=s, dst_ref=d, send_sem=sems[0].at[i], recv_sem=sems[1].at[i],
                                              device_id=peer, device_id_type=MESH)
                 for i, (s, d, peer, _) in enumerate(remote)]
        for cp in sends:
            cp.start()
        for i, (s, _, peer, landing) in enumerate(remote):
            pltpu.make_async_remote_copy(src_ref=s, dst_ref=landing, send_sem=sems[0].at[i], recv_sem=sems[1].at[i],
                                         device_id=peer, device_id_type=MESH).wait_recv()
        for cp in sends:
            cp.wait_send()
        for cp in lcs:
            cp.wait()

    scratch = [pltpu.SemaphoreType.DMA((n_remote,)), pltpu.SemaphoreType.DMA((n_remote,))]
    if n_local:
        scratch.append(pltpu.SemaphoreType.DMA((n_local,)))
    return pl.pallas_call(
        body,
        in_specs=[ANY] * n_in,
        out_specs=[ANY] * n_out,
        out_shape=out_shapes,
        scratch_shapes=scratch,
        input_output_aliases=aliases or {},
        name=name,
    )(*ins)


def _gather_halves(name, shards, dims):
    n = len(shards)
    sizes = [s.shape[d] for s, d in zip(shards, dims)]
    full = [jax.ShapeDtypeStruct(s.shape[:d] + (s.shape[d] * N_CHIPS,) + s.shape[d + 1:], s.dtype)
            for s, d in zip(shards, dims)]

    def plan(ins, outs):
        (_, _, c), k, peers, chips = _place()
        local, remote = [], []
        for w in range(n):
            d, h = dims[w], sizes[w] // 2
            local.append((ins[w], _window(outs[w], d, k, sizes[w])))
            for p in range(3):
                remote.append((_window(ins[w], d, c, h), _window(outs[w], d, 2 * k + c, h), peers[p],
                               _window(outs[w], d, 2 * chips[p] + c, h)))
        return local, remote

    return _comm_call(name, shards, full, plan, n, 3 * n)


def _forward_halves(name, full, dims):
    n = len(full)
    sizes = [f.shape[d] // N_CHIPS for f, d in zip(full, dims)]

    def plan(ins, outs):
        (x, y, c), _, _, chips = _place()
        remote = []
        for w in range(n):
            d, h = dims[w], sizes[w] // 2
            for p in range(3):
                mine = _window(outs[w], d, 2 * chips[p] + c, h)
                remote.append((mine, mine, (x, y, 1 - c), _window(outs[w], d, 2 * chips[p] + 1 - c, h)))
        return [], remote

    return _comm_call(name, full, [jax.ShapeDtypeStruct(f.shape, f.dtype) for f in full], plan, 0, 3 * n,
                      aliases={w: w for w in range(n)})


def _half_shape(shape, dim):
    return shape[:dim] + (shape[dim] // (2 * N_CHIPS),) + shape[dim + 1:]


def _swap_halves(name, grads, dims):
    n = len(grads)
    outs_shape = [jax.ShapeDtypeStruct((N_CHIPS,) + _half_shape(g.shape, d), g.dtype) for g, d in zip(grads, dims)]

    def plan(ins, outs):
        (x, y, c), _, _, _ = _place()
        remote = []
        for w in range(n):
            d, h = dims[w], grads[w].shape[dims[w]] // (2 * N_CHIPS)
            for q in range(N_CHIPS):
                remote.append((_window(ins[w], d, 2 * q + 1 - c, h), outs[w].at[q], (x, y, 1 - c), outs[w].at[q]))
        return [], remote

    return _comm_call(name, grads, outs_shape, plan, 0, N_CHIPS * n)


def _add_halves(name, g, got, dim, core):
    R, C = g.shape
    if dim == 1:
        r, cc = R, C // (2 * N_CHIPS)
    else:
        r, cc = R // (2 * N_CHIPS), C
    tr, tc = _div_tile(r, 512, 16), _div_tile(cc, 1024)
    nr, nc = r // tr, cc // tc

    def g_map(q, i, j, core_ref):
        w = 2 * q + core_ref[0]
        return (i, w * nc + j) if dim == 1 else (w * nr + i, j)

    def body(core_ref, g_ref, got_ref, o_ref):
        o_ref[...] = (g_ref[...].astype(F32) + got_ref[...].astype(F32)).astype(o_ref.dtype)

    slab = pl.BlockSpec((None, tr, tc), lambda q, i, j, core_ref: (q, i, j))
    return pl.pallas_call(
        body,
        grid_spec=pltpu.PrefetchScalarGridSpec(
            num_scalar_prefetch=1, grid=(N_CHIPS, nr, nc),
            in_specs=[pl.BlockSpec((tr, tc), g_map), slab], out_specs=slab),
        out_shape=jax.ShapeDtypeStruct((N_CHIPS, r, cc), g.dtype),
        compiler_params=_cparams(("parallel", "parallel", "parallel")),
        name=name,
    )(core, g, got)


def _scatter_halves(name, sums):
    n = len(sums)

    def plan(ins, outs):
        _, k, peers, chips = _place()
        local, remote = [], []
        for w in range(n):
            local.append((ins[w].at[k], outs[w].at[3]))
            for p in range(3):
                remote.append((ins[w].at[chips[p]], outs[w].at[p], peers[p], outs[w].at[p]))
        return local, remote

    return _comm_call(name, sums, [jax.ShapeDtypeStruct(s.shape, s.dtype) for s in sums], plan, n, 3 * n)


def _join_halves(name, parts, layers, dims, shapes):
    names = list(shapes)
    outs_shape = [jax.ShapeDtypeStruct(shapes[t], F32) for t in names]

    def plan(ins, outs):
        (x, y, c), _, _, _ = _place()
        local, remote = [], []
        for i, (t, l) in enumerate(layers):
            o = outs[names.index(t)]
            d, h = dims[t], shapes[t][dims[t]] // 2

            def win(half):
                idx = [slice(None)] * len(o.shape)
                idx[0] = l
                idx[d] = pl.ds(half * h, h)
                return o.at[tuple(idx)]

            local.append((ins[i], win(c)))
            remote.append((ins[i], win(c), (x, y, 1 - c), win(1 - c)))
        return local, remote

    return _comm_call(name, parts, outs_shape, plan, len(parts), len(parts))


def _allreduce_small(pack):
    R, C = pack.shape

    def gather_body(in_ref, slots_ref, send_sems, recv_sems, local_sem):
        x, y, c = lax.axis_index("x"), lax.axis_index("y"), lax.axis_index("c")
        me = 4 * x + 2 * y + c
        flips = [(dx, dy, dc) for dx in (0, 1) for dy in (0, 1) for dc in (0, 1)][1:]

        def flip(v, d):
            return 1 - v if d else v

        local = pltpu.make_async_copy(in_ref, slots_ref.at[me], local_sem)
        local.start()
        sends = []
        for j, (dx, dy, dc) in enumerate(flips):
            px, py, pc = flip(x, dx), flip(y, dy), flip(c, dc)
            sends.append((pltpu.make_async_remote_copy(
                src_ref=in_ref, dst_ref=slots_ref.at[me], send_sem=send_sems.at[j], recv_sem=recv_sems.at[j],
                device_id=(px, py, pc), device_id_type=MESH), 4 * px + 2 * py + pc, j))
        for cp, _, _ in sends:
            cp.start()
        for cp, peer_id, j in sends:
            pltpu.make_async_remote_copy(
                src_ref=in_ref, dst_ref=slots_ref.at[peer_id], send_sem=send_sems.at[j], recv_sem=recv_sems.at[j],
                device_id=(x, y, c), device_id_type=MESH).wait_recv()
        for cp, _, _ in sends:
            cp.wait_send()
        local.wait()

    slots = pl.pallas_call(
        gather_body,
        in_specs=[ANY],
        out_specs=ANY,
        out_shape=jax.ShapeDtypeStruct((N_DEV, R, C), pack.dtype),
        scratch_shapes=[pltpu.SemaphoreType.DMA((N_DEV - 1,)), pltpu.SemaphoreType.DMA((N_DEV - 1,)),
                        pltpu.SemaphoreType.DMA],
        name="allgather_small",
    )(pack)

    def sum_body(s_ref, o_ref):
        acc = s_ref[0]
        for d in range(1, N_DEV):
            acc = acc + s_ref[d]
        o_ref[...] = acc

    return pl.pallas_call(
        sum_body,
        out_shape=jax.ShapeDtypeStruct((R, C), pack.dtype),
        name="sum_small",
    )(slots)


def _adamw_math(w, g, m, v):
    m2 = ADAM_B1 * m + (1.0 - ADAM_B1) * g
    v2 = ADAM_B2 * v + (1.0 - ADAM_B2) * (g * g)
    m_hat = m2 / (1.0 - ADAM_B1 ** ADAM_STEP)
    v_hat = v2 / (1.0 - ADAM_B2 ** ADAM_STEP)
    delta = -ADAM_LR * (m_hat / (jnp.sqrt(v_hat) + ADAM_EPS) + ADAM_WD * w)
    return delta, m2, v2


def _adamw_big(name, w, m, v, g_parts):
    shape = w.shape
    C = shape[-1]
    R = w.size // C
    tr, tc = _div_tile(R, 256, 8), _div_tile(C, 1024)
    n_g = len(g_parts)

    def body(*refs):
        w_ref, m_ref, v_ref = refs[:3]
        g_ref, d_ref, nm_ref, nv_ref = refs[3 + n_g:]
        g = refs[3][...]
        for extra in refs[4:3 + n_g]:
            g = g + extra[...]
        delta, m2, v2 = _adamw_math(w_ref[...], g, m_ref[...], v_ref[...])
        g_ref[...], d_ref[...], nm_ref[...], nv_ref[...] = g, delta, m2, v2

    blk = pl.BlockSpec((tr, tc), lambda i, j: (i, j))
    outs = pl.pallas_call(
        body,
        grid=(R // tr, C // tc),
        in_specs=[blk] * (3 + n_g),
        out_specs=[blk] * 4,
        out_shape=[jax.ShapeDtypeStruct((R, C), F32)] * 4,
        compiler_params=_cparams(("parallel", "parallel")),
        name=name,
    )(w.reshape(R, C), m.reshape(R, C), v.reshape(R, C), *[g.reshape(R, C) for g in g_parts])
    return [o.reshape(shape) for o in outs]


def _adamw_small(name, w, g, m, v):
    shape = w.shape
    two_d = (w.size // shape[-1], shape[-1])

    def body(w_ref, g_ref, m_ref, v_ref, d_ref, nm_ref, nv_ref):
        d_ref[...], nm_ref[...], nv_ref[...] = _adamw_math(w_ref[...], g_ref[...], m_ref[...], v_ref[...])

    outs = pl.pallas_call(
        body,
        out_shape=[jax.ShapeDtypeStruct(two_d, F32)] * 3,
        name=name,
    )(w.reshape(two_d), g.reshape(two_d), m.reshape(two_d), v.reshape(two_d))
    return [o.reshape(shape) for o in outs]


SMALL_ROWS = 40
S_BIN, S_G1, S_G2 = 16, 24, 32
MATS = ("w_in", "w_pa", "w_pb", "w_o", "w_mlp1", "w_mlp2")
LRU = ("lru_wr", "lru_wi")
BIG_DIM = dict(w_in=2, w_pa=1, w_pb=1, w_o=1, w_mlp1=2, w_mlp2=1, lru_wr=2, lru_wi=2)
WEIGHTS = ("norm1_g", "w_in", "b_in", "conv_a_w", "conv_a_b", "lru_wr", "lru_br", "lru_wi", "lru_bi", "lru_lam",
           "conv_b_w", "w_pa", "w_pb", "w_o", "norm2_g", "w_mlp1", "w_mlp2", "final_g")


def _rows_at(a, r0, total):
    pad = [(0, 0)] * a.ndim
    pad[-2] = (r0, total - r0 - a.shape[-2])
    return jnp.pad(a, pad)


def kernel(x, norm1_g, w_in, b_in, conv_a_w, conv_a_b, lru_wr, lru_br, lru_wi, lru_bi, lru_lam, conv_b_w, w_pa, w_pb, w_o, norm2_g, w_mlp1, w_mlp2, final_g, loss_target, m_norm1_g, m_w_in, m_b_in, m_conv_a_w, m_conv_a_b, m_lru_wr, m_lru_br, m_lru_wi, m_lru_bi, m_lru_lam, m_conv_b_w, m_w_pa, m_w_pb, m_w_o, m_norm2_g, m_w_mlp1, m_w_mlp2, m_final_g, v_norm1_g, v_w_in, v_b_in, v_conv_a_w, v_conv_a_b, v_lru_wr, v_lru_br, v_lru_wi, v_lru_bi, v_lru_lam, v_conv_b_w, v_w_pa, v_w_pb, v_w_o, v_norm2_g, v_w_mlp1, v_w_mlp2, v_final_g):
    wts = dict(norm1_g=norm1_g, w_in=w_in, b_in=b_in, conv_a_w=conv_a_w, conv_a_b=conv_a_b, lru_wr=lru_wr,
               lru_br=lru_br, lru_wi=lru_wi, lru_bi=lru_bi, lru_lam=lru_lam, conv_b_w=conv_b_w, w_pa=w_pa,
               w_pb=w_pb, w_o=w_o, norm2_g=norm2_g, w_mlp1=w_mlp1, w_mlp2=w_mlp2, final_g=final_g)
    mom = dict(norm1_g=m_norm1_g, w_in=m_w_in, b_in=m_b_in, conv_a_w=m_conv_a_w, conv_a_b=m_conv_a_b,
               lru_wr=m_lru_wr, lru_br=m_lru_br, lru_wi=m_lru_wi, lru_bi=m_lru_bi, lru_lam=m_lru_lam,
               conv_b_w=m_conv_b_w, w_pa=m_w_pa, w_pb=m_w_pb, w_o=m_w_o, norm2_g=m_norm2_g, w_mlp1=m_w_mlp1,
               w_mlp2=m_w_mlp2, final_g=m_final_g)
    vel = dict(norm1_g=v_norm1_g, w_in=v_w_in, b_in=v_b_in, conv_a_w=v_conv_a_w, conv_a_b=v_conv_a_b,
               lru_wr=v_lru_wr, lru_br=v_lru_br, lru_wi=v_lru_wi, lru_bi=v_lru_bi, lru_lam=v_lru_lam,
               conv_b_w=v_conv_b_w, w_pa=v_w_pa, w_pb=v_w_pb, w_o=v_w_o, norm2_g=v_norm2_g, w_mlp1=v_w_mlp1,
               w_mlp2=v_w_mlp2, final_g=v_final_g)
    depth, D = norm1_g.shape
    nb, bw = lru_wr.shape[1], lru_wr.shape[3]
    chip = 2 * lax.axis_index("x") + lax.axis_index("y")

    small_parts = [conv_a_w.reshape(-1), conv_b_w.reshape(-1), lru_br.reshape(-1), lru_bi.reshape(-1)]
    small_len = sum(p.shape[0] for p in small_parts)
    small_rows = -(-small_len // 1024) * 8
    small = jnp.concatenate(small_parts + [jnp.zeros((small_rows * 128 - small_len,), F32)]).reshape(small_rows, 128)
    gathered = _gather_weights([wts[n].astype(BF16) for n in LRU], [BIG_DIM[n] for n in LRU], small)
    full = dict(zip(LRU, gathered[:-1]))
    items = [(n, l) for l in range(depth) for n in MATS]
    mat_dims = [BIG_DIM[n] - 1 for n, _ in items]
    halves = _gather_halves("gather_halves", [wts[n][l].astype(BF16) for n, l in items], mat_dims)
    mats = _forward_halves("forward_halves", halves, mat_dims)
    for n in MATS:
        full[n] = [mats[items.index((n, l))] for l in range(depth)]
    flat = gathered[-1].reshape(N_CHIPS, small_rows * 128)
    off = 0
    small_full = []
    for part, shard in zip(small_parts, (conv_a_w, conv_b_w, lru_br, lru_bi)):
        piece = flat[:, off:off + part.shape[0]].reshape((N_CHIPS,) + shard.shape)
        small_full.append(jnp.moveaxis(piece, 0, -2).reshape(shard.shape[:-1] + (N_CHIPS * shard.shape[-1],)))
        off += part.shape[0]
    caw_f, cbw_f, br_f, bi_f = small_full
    pch = (_rows_at(conv_a_b[:, None, :], R_CAB, R_ROWS) + _rows_at(br_f.reshape(depth, 1, D), R_BR, R_ROWS)
           + _rows_at(bi_f.reshape(depth, 1, D), R_BI, R_ROWS) + _rows_at(lru_lam[:, None, :], R_LAM, R_ROWS)
           + _rows_at(caw_f, R_CAW, R_ROWS) + _rows_at(cbw_f, R_CBW, R_ROWS))
    W = dict(w_in=full["w_in"], b_in=b_in, pch=pch, wr=full["lru_wr"], wi=full["lru_wi"], w_pa=full["w_pa"],
             w_pb=full["w_pb"], w_o=full["w_o"], w1=full["w_mlp1"], w2=full["w_mlp2"], g1=norm1_g, g2=norm2_g,
             gf=final_g)

    loss_local, dx, grads, dgf = _local_fwd_bwd(x[0], loss_target[0], W)
    loss = lax.psum(loss_local, ("x", "y", "c"))

    key = dict(w_in="w_in", w_pa="w_pa", w_pb="w_pb", w_o="w_o", w_mlp1="w1", w_mlp2="w2", lru_wr="wr", lru_wi="wi")
    out_g, out_d, out_m, out_v = {}, {}, {}, {}
    per_layer = [[grads[l][key[n]].astype(BF16) for l in range(depth)] for n in LRU]
    land = _scatter_grads(per_layer, [BIG_DIM[n] - 1 for n in LRU])
    chip_sums = [_sum_slots(f"sum_slots_{n}", ld.reshape(N_CHIPS, -1, ld.shape[-1])) for n, ld in zip(LRU, land)]
    sib_sums = _swap_with_sibling(chip_sums)
    for n, mine, sib in zip(LRU, chip_sums, sib_sums):
        out_g[n], out_d[n], out_m[n], out_v[n] = _adamw_big(f"adamw_{n}", wts[n], mom[n], vel[n], [mine, sib])
    core = lax.axis_index("c").astype(jnp.int32).reshape(1)
    mat_grads = [grads[l][key[n]] for n, l in items]
    got = _swap_halves("swap_halves", mat_grads, mat_dims)
    sums = [_add_halves(f"add_halves_{n}_{l}", g, r, d, core)
            for (n, l), g, r, d in zip(items, mat_grads, got, mat_dims)]
    landed = _scatter_halves("scatter_halves", sums)
    reduced = [_sum_slots(f"sum_slots_{n}_{l}", ld) for (n, l), ld in zip(items, landed)]
    joined = _join_halves("join_halves", reduced, items, {n: BIG_DIM[n] for n in MATS},
                          {n: wts[n].shape for n in MATS})
    for n, g in zip(MATS, joined):
        out_g[n], out_d[n], out_m[n], out_v[n] = _adamw_big(f"adamw_{n}", wts[n], mom[n], vel[n], [g])

    rows = []
    for l in range(depth):
        g = grads[l]
        rows.append(_rows_at(g["sm"], 0, SMALL_ROWS) + _rows_at(g["b_in"].reshape(7, D), S_BIN, SMALL_ROWS)
                    + _rows_at(g["g1"], S_G1, SMALL_ROWS) + _rows_at(g["g2"], S_G2, SMALL_ROWS))
    rows.append(_rows_at(dgf, 0, 8))
    tot = _allreduce_small(jnp.concatenate(rows, axis=0))
    per = tot[:depth * SMALL_ROWS].reshape(depth, SMALL_ROWS, D)

    def cols_of_chip(a, axis):
        size = a.shape[axis] // N_CHIPS
        return lax.dynamic_slice_in_dim(a, chip * size, size, axis=axis)

    small_g = dict(
        norm1_g=per[:, S_G1], b_in=per[:, S_BIN:S_BIN + 7].reshape(depth, 7 * D),
        conv_a_w=cols_of_chip(per[:, R_CAW:R_CAW + 4], 2), conv_a_b=per[:, R_CAB],
        lru_br=cols_of_chip(per[:, R_BR].reshape(depth, nb, bw), 2),
        lru_bi=cols_of_chip(per[:, R_BI].reshape(depth, nb, bw), 2), lru_lam=per[:, R_LAM],
        conv_b_w=cols_of_chip(per[:, R_CBW:R_CBW + 3], 2), norm2_g=per[:, S_G2],
        final_g=tot[depth * SMALL_ROWS])
    for n, g in small_g.items():
        out_g[n] = g
        out_d[n], out_m[n], out_v[n] = _adamw_small(f"adamw_{n}", wts[n], g, mom[n], vel[n])

    return (loss, dx[None], *[out_g[n] for n in WEIGHTS], *[out_d[n] for n in WEIGHTS],
            *[out_m[n] for n in WEIGHTS], *[out_v[n] for n in WEIGHTS])
```

```python
import functools

import jax
import jax.numpy as jnp
from jax import lax
from jax.experimental import pallas as pl
from jax.experimental.pallas import tpu as pltpu

F32 = jnp.float32
BF16 = jnp.bfloat16
MESH = pl.DeviceIdType.MESH

EPS = 1e-6
LRU_C = 8.0
ADAM_LR = 0.001
ADAM_B1 = 0.9
ADAM_B2 = 0.999
ADAM_EPS = 1e-08
ADAM_WD = 0.01
ADAM_STEP = 10

N_CHIPS = 4
N_DEV = 8
HALO = 8
VMEM_LIMIT = 56 * 1024 * 1024
MM_TILES = (1024, 1024, 1024)
MM_TILES_FUSED = (512, 1024, 1024)
SEQ_CHUNK = 256
ROW_TILE = 256

R_CAB, R_BR, R_BI, R_LAM, R_CAW, R_CBW, R_ROWS = 0, 1, 2, 3, 4, 8, 16


def _cparams(sem):
    return pltpu.CompilerParams(dimension_semantics=sem, vmem_limit_bytes=VMEM_LIMIT)


def _div_tile(n, pref, unit=128):
    if n <= pref:
        return n
    t = (pref // unit) * unit
    while n % t:
        t -= unit
    return t


def _sigmoid(v):
    return 1.0 / (1.0 + jnp.exp(-v))


def _gelu_and_grad(y):
    k = 0.7978845608028654
    c = 0.044715
    y2 = y * y
    t = jnp.tanh(k * (y + c * y2 * y))
    g = 0.5 * y * (1.0 + t)
    gp = 0.5 * (1.0 + t) + 0.5 * y * (1.0 - t * t) * (k * (1.0 + 3.0 * c * y2))
    return g, gp


def _softplus_neg(lam):
    e = jnp.exp(-jnp.abs(lam))
    w = 1.0 + e
    l1p = jnp.where(w == 1.0, e, jnp.log(w) * e / jnp.where(w == 1.0, 1.0, w - 1.0))
    return jnp.maximum(-lam, 0.0) + l1p


def _mm(name, mode, a, b, M, N, K, out_dtypes, epilogue=None, extras=(), la=None, lb=None, tiles=None):
    tiles = MM_TILES if tiles is None else tiles
    tm, tn, tk = _div_tile(M, tiles[0]), _div_tile(N, tiles[1]), _div_tile(K, tiles[2])
    assert M % tm == 0 and N % tn == 0 and K % tk == 0, (name, M, N, K)
    nk = K // tk

    def spec(lead, shape, imap):
        if lead is None:
            return pl.BlockSpec(shape, imap)
        return pl.BlockSpec((None,) + shape, lambda i, j, k: (lead,) + imap(i, j, k))

    if mode == "nn":
        a_spec = spec(la, (tm, tk), lambda i, j, k: (i, k))
        b_spec = spec(lb, (tk, tn), lambda i, j, k: (k, j))
        dn = (((1,), (0,)), ((), ()))
    elif mode == "nt":
        a_spec = spec(la, (tm, tk), lambda i, j, k: (i, k))
        b_spec = spec(lb, (tn, tk), lambda i, j, k: (j, k))
        dn = (((1,), (1,)), ((), ()))
    else:
        a_spec = spec(la, (tk, tm), lambda i, j, k: (k, i))
        b_spec = spec(lb, (tk, tn), lambda i, j, k: (k, j))
        dn = (((0,), (0,)), ((), ()))

    ex_arrays, ex_specs = [], []
    for arr, kind, off in extras:
        ex_arrays.append(arr)
        if kind == "bias":
            ex_specs.append(pl.BlockSpec((1, tn), lambda i, j, k: (0, j)))
        else:
            assert off % tn == 0
            ex_specs.append(pl.BlockSpec((tm, tn), lambda i, j, k, o=off // tn: (i, j + o)))
    n_ex, n_out = len(ex_arrays), len(out_dtypes)

    def body(*refs):
        a_ref, b_ref = refs[0], refs[1]
        ex = refs[2:2 + n_ex]
        outs = refs[2 + n_ex:2 + n_ex + n_out]
        acc = refs[-1]
        k = pl.program_id(2)

        @pl.when(k == 0)
        def _():
            acc[...] = jnp.zeros_like(acc)

        acc[...] += lax.dot_general(a_ref[...], b_ref[...], dn, preferred_element_type=F32)

        @pl.when(k == nk - 1)
        def _():
            r = acc[...]
            vals = (r,) if epilogue is None else epilogue(r, *[e[...] for e in ex])
            for o, v in zip(outs, vals):
                o[...] = v.astype(o.dtype)

    res = pl.pallas_call(
        body,
        grid=(M // tm, N // tn, nk),
        in_specs=[a_spec, b_spec, *ex_specs],
        out_specs=[pl.BlockSpec((tm, tn), lambda i, j, k: (i, j)) for _ in range(n_out)],
        out_shape=[jax.ShapeDtypeStruct((M, N), d) for d in out_dtypes],
        scratch_shapes=[pltpu.VMEM((tm, tn), F32)],
        compiler_params=_cparams(("parallel", "parallel", "arbitrary")),
        name=name,
    )(a, b, *ex_arrays)
    return res[0] if n_out == 1 else res


def _rms_fwd(name, x, g_row):
    T, D = x.shape
    tm = min(ROW_TILE, T)

    def body(x_ref, g_ref, h_ref):
        xv = x_ref[...]
        r = lax.rsqrt(jnp.mean(xv * xv, axis=-1, keepdims=True) + EPS)
        h_ref[...] = (xv * r * g_ref[...]).astype(BF16)

    return pl.pallas_call(
        body,
        grid=(T // tm,),
        in_specs=[pl.BlockSpec((tm, D), lambda i: (i, 0)), pl.BlockSpec((1, D), lambda i: (0, 0))],
        out_specs=pl.BlockSpec((tm, D), lambda i: (i, 0)),
        out_shape=jax.ShapeDtypeStruct((T, D), BF16),
        compiler_params=_cparams(("parallel",)),
        name=name,
    )(x, g_row)


def _rms_bwd(name, x, g_row, dh, dres):
    T, D = x.shape
    tm = min(ROW_TILE, T)

    def body(x_ref, g_ref, dh_ref, dres_ref, dx_ref, dxb_ref, dg_ref):
        xv, dhv = x_ref[...], dh_ref[...]
        r = lax.rsqrt(jnp.mean(xv * xv, axis=-1, keepdims=True) + EPS)
        gd = g_ref[...] * dhv
        c = jnp.mean(xv * gd, axis=-1, keepdims=True)
        dx = r * gd - xv * (r * r * r) * c + dres_ref[...]
        dx_ref[...] = dx
        dxb_ref[...] = dx.astype(BF16)

        @pl.when(pl.program_id(0) == 0)
        def _():
            dg_ref[...] = jnp.zeros_like(dg_ref)

        dg_ref[...] += jnp.sum(dhv * xv * r, axis=0, keepdims=True)

    row = pl.BlockSpec((tm, D), lambda i: (i, 0))
    vec = pl.BlockSpec((1, D), lambda i: (0, 0))
    return pl.pallas_call(
        body,
        grid=(T // tm,),
        in_specs=[row, vec, row, row],
        out_specs=[row, row, vec],
        out_shape=[jax.ShapeDtypeStruct((T, D), F32), jax.ShapeDtypeStruct((T, D), BF16),
                   jax.ShapeDtypeStruct((1, D), F32)],
        compiler_params=_cparams(("arbitrary",)),
        name=name,
    )(x, g_row, dh, dres)


def _loss_head(name, x, g_row, tgt):
    T, D = x.shape
    tm = min(ROW_TILE, T)

    def body(x_ref, g_ref, t_ref, dx_ref, dxb_ref, dg_ref, loss_ref):
        xv, g = x_ref[...], g_ref[...]
        r = lax.rsqrt(jnp.mean(xv * xv, axis=-1, keepdims=True) + EPS)
        xh = xv * r
        e = xh * g - t_ref[...]
        lpart = 0.5 * jnp.sum(jnp.mean(e * e, axis=-1, keepdims=True))
        dy = e * (1.0 / D)
        gd = g * dy
        c = jnp.mean(xv * gd, axis=-1, keepdims=True)
        dx = r * gd - xv * (r * r * r) * c
        dx_ref[...] = dx
        dxb_ref[...] = dx.astype(BF16)

        @pl.when(pl.program_id(0) == 0)
        def _():
            dg_ref[...] = jnp.zeros_like(dg_ref)
            loss_ref[...] = jnp.zeros_like(loss_ref)

        dg_ref[...] += jnp.sum(dy * xh, axis=0, keepdims=True)
        loss_ref[...] += jnp.full(loss_ref.shape, lpart, F32)

    row = pl.BlockSpec((tm, D), lambda i: (i, 0))
    vec = pl.BlockSpec((1, D), lambda i: (0, 0))
    return pl.pallas_call(
        body,
        grid=(T // tm,),
        in_specs=[row, vec, row],
        out_specs=[row, row, vec, pl.BlockSpec((8, 128), lambda i: (0, 0))],
        out_shape=[jax.ShapeDtypeStruct((T, D), F32), jax.ShapeDtypeStruct((T, D), BF16),
                   jax.ShapeDtypeStruct((1, D), F32), jax.ShapeDtypeStruct((8, 128), F32)],
        compiler_params=_cparams(("arbitrary",)),
        name=name,
    )(x, g_row, tgt)


def _colsum(name, a):
    T, N = a.shape
    tm, tn = min(512, T), _div_tile(N, 2048)

    def body(a_ref, o_ref):
        @pl.when(pl.program_id(1) == 0)
        def _():
            o_ref[...] = jnp.zeros_like(o_ref)

        o_ref[...] += jnp.sum(a_ref[...].astype(F32), axis=0, keepdims=True)

    return pl.pallas_call(
        body,
        grid=(N // tn, T // tm),
        in_specs=[pl.BlockSpec((tm, tn), lambda j, i: (i, j))],
        out_specs=pl.BlockSpec((1, tn), lambda j, i: (0, j)),
        out_shape=jax.ShapeDtypeStruct((1, N), F32),
        compiler_params=_cparams(("parallel", "arbitrary")),
        name=name,
    )(a)


def _tile_scan(a, b, row, reverse):
    for s in (1, 2, 4):
        if reverse:
            a_s, b_s, m = pltpu.roll(a, 8 - s, 0), pltpu.roll(b, 8 - s, 0), row < 8 - s
        else:
            a_s, b_s, m = pltpu.roll(a, s, 0), pltpu.roll(b, s, 0), row >= s
        b = jnp.where(m, a * b_s + b, b)
        a = jnp.where(m, a * a_s, a)
    return a, b


def _chunk_scan(a_s, b_s, out_ref, carry, n_tiles, width, reverse):
    row = lax.broadcasted_iota(jnp.int32, (8, width), 0)
    edge = 0 if reverse else 7

    def step(j, c):
        jj = (n_tiles - 1 - j) if reverse else j
        o = pl.multiple_of(jj * 8, 8)
        ca, cb = _tile_scan(a_s[pl.ds(o, 8), :], b_s[pl.ds(o, 8), :], row, reverse)
        h = ca * c + cb
        out_ref[pl.ds(o, 8), :] = h
        return jnp.broadcast_to(h[edge:edge + 1, :], (8, width))

    carry[...] = lax.fori_loop(0, n_tiles, step, carry[...])


def _gates(xc, p_ref, wr_ref, wi_ref):
    xcb = xc.astype(BF16)
    r = _sigmoid(jnp.dot(xcb, wr_ref[...], preferred_element_type=F32) + p_ref[R_BR:R_BR + 1, :])
    ig = _sigmoid(jnp.dot(xcb, wi_ref[...], preferred_element_type=F32) + p_ref[R_BI:R_BI + 1, :])
    sp = _softplus_neg(p_ref[R_LAM:R_LAM + 1, :])
    log_a = (-LRU_C) * r * sp
    a = jnp.exp(log_a)
    t = jnp.tanh(log_a)
    mult = jnp.sqrt(-2.0 * t / (1.0 - t))
    return xcb, r, ig, sp, a, mult


def _mixer_specs(Tc, bw, nb, layer):
    def seg(s):
        return pl.BlockSpec((Tc, bw), lambda n, i: (i, s * nb + n))

    p_spec = pl.BlockSpec((None, R_ROWS, bw), lambda n, i: (layer, 0, n))
    w_spec = pl.BlockSpec((None, None, bw, bw), lambda n, i: (layer, n, 0, 0))
    return seg, p_spec, w_spec


def _mixer_fwd(name, layer, z, pch, wr, wi):
    T, D = z.shape[0], z.shape[1] // 7
    bw, nb = wr.shape[-1], wr.shape[1]
    Tc = min(SEQ_CHUNK, T)
    nT = T // Tc

    def body(xa_ref, ya_ref, cb_ref, cc_ref, cx_ref, p_ref, wr_ref, wi_ref,
             pa_ref, pb_ref, xc_ref, hl_ref, vb_ref, xa_buf, u_buf, a_s, b_s, carry):
        @pl.when(pl.program_id(1) == 0)
        def _():
            xa_buf[0:HALO, :] = jnp.zeros((HALO, bw), F32)
            u_buf[0:HALO, :] = jnp.zeros((HALO, bw), F32)
            carry[...] = jnp.zeros_like(carry)

        xa_buf[HALO:HALO + Tc, :] = xa_ref[...]
        xc = p_ref[R_CAB:R_CAB + 1, :]
        for k in range(4):
            xc = xc + p_ref[R_CAW + k:R_CAW + k + 1, :] * xa_buf[HALO - 3 + k:HALO - 3 + k + Tc, :]
        xc_ref[...] = xc
        _, _, ig, _, a, mult = _gates(xc, p_ref, wr_ref, wi_ref)
        a_s[...] = a
        b_s[...] = mult * (ig * xc)
        _chunk_scan(a_s, b_s, hl_ref, carry, Tc // 8, bw, False)
        g, _ = _gelu_and_grad(ya_ref[...])
        pa_ref[...] = (hl_ref[...] * g).astype(BF16)

        u_buf[HALO:HALO + Tc, :] = cc_ref[...] * cx_ref[...]
        vb = jnp.zeros((Tc, bw), F32)
        for k in range(3):
            vb = vb + p_ref[R_CBW + k:R_CBW + k + 1, :] * u_buf[HALO - 2 + k:HALO - 2 + k + Tc, :]
        vb_ref[...] = vb
        pb_ref[...] = (cb_ref[...] * vb).astype(BF16)
        xa_buf[0:HALO, :] = xa_buf[Tc:Tc + HALO, :]
        u_buf[0:HALO, :] = u_buf[Tc:Tc + HALO, :]

    seg, p_spec, w_spec = _mixer_specs(Tc, bw, nb, layer)
    out = pl.BlockSpec((Tc, bw), lambda n, i: (i, n))
    return pl.pallas_call(
        body,
        grid=(nb, nT),
        in_specs=[seg(0), seg(1), seg(2), seg(3), seg(4), p_spec, w_spec, w_spec],
        out_specs=[out] * 5,
        out_shape=[jax.ShapeDtypeStruct((T, D), BF16), jax.ShapeDtypeStruct((T, D), BF16),
                   jax.ShapeDtypeStruct((T, D), F32), jax.ShapeDtypeStruct((T, D), F32),
                   jax.ShapeDtypeStruct((T, D), F32)],
        scratch_shapes=[pltpu.VMEM((Tc + HALO, bw), F32), pltpu.VMEM((Tc + HALO, bw), F32),
                        pltpu.VMEM((Tc, bw), F32), pltpu.VMEM((Tc, bw), F32), pltpu.VMEM((8, bw), F32)],
        compiler_params=_cparams(("parallel", "arbitrary")),
        name=name,
    )(z, z, z, z, z, pch, wr, wi)


def _mixer_bwd(name, layer, z, xc, hl, vb, dpa, dpb, pch, wr, wi):
    T, D = z.shape[0], z.shape[1] // 7
    bw, nb = wr.shape[-1], wr.shape[1]
    Tc = min(SEQ_CHUNK, T)
    nT = T // Tc
    tpc = Tc // 8

    def body(xa_ref, ya_ref, cb_ref, cc_ref, cx_ref, xc_ref, hl_ref, hp_ref, vb_ref, dpa_ref, dpb_ref,
             p_ref, wr_ref, wi_ref,
             dxa_ref, dya_ref, dcb_ref, dcc_ref, dcx_ref, dwr_ref, dwi_ref, sm_ref,
             h_buf, a_buf, dxc_buf, dvb_buf, a_s, d_s, lam_s, carry):
        i = pl.program_id(1)

        @pl.when(i == 0)
        def _():
            a_buf[Tc:Tc + HALO, :] = jnp.zeros((HALO, bw), F32)
            dxc_buf[Tc:Tc + HALO, :] = jnp.zeros((HALO, bw), F32)
            dvb_buf[Tc:Tc + HALO, :] = jnp.zeros((HALO, bw), F32)
            carry[...] = jnp.zeros_like(carry)
            dwr_ref[...] = jnp.zeros_like(dwr_ref)
            dwi_ref[...] = jnp.zeros_like(dwi_ref)
            sm_ref[...] = jnp.zeros_like(sm_ref)

        xcv = xc_ref[...]
        xcb, r, ig, sp, a, mult = _gates(xcv, p_ref, wr_ref, wi_ref)
        g, gp = _gelu_and_grad(ya_ref[...])
        hlv, dpav = hl_ref[...], dpa_ref[...]
        dya_ref[...] = (dpav * hlv * gp).astype(BF16)

        a_buf[0:Tc, :] = a
        a_s[...] = a_buf[1:Tc + 1, :]
        d_s[...] = dpav * g
        _chunk_scan(a_s, d_s, lam_s, carry, tpc, bw, True)
        lamv = lam_s[...]

        h_buf[HALO:HALO + Tc, :] = hlv
        h_buf[0:HALO, :] = jnp.where(i == nT - 1, 0.0, hp_ref[...])
        da = lamv * h_buf[HALO - 1:HALO - 1 + Tc, :]
        dmult = lamv * (ig * xcv)
        dbx = lamv * mult
        dig = dbx * xcv
        dxc = dbx * ig
        dlog_a = da * a - dmult * (a * a) / mult
        dpr = (dlog_a * ((-LRU_C) * sp)) * r * (1.0 - r)
        dpi = dig * ig * (1.0 - ig)
        dprb, dpib = dpr.astype(BF16), dpi.astype(BF16)
        nt = (((1,), (1,)), ((), ()))
        tn = (((0,), (0,)), ((), ()))
        dxc = dxc + lax.dot_general(dprb, wr_ref[...], nt, preferred_element_type=F32)
        dxc = dxc + lax.dot_general(dpib, wi_ref[...], nt, preferred_element_type=F32)
        dwr_ref[...] += lax.dot_general(xcb, dprb, tn, preferred_element_type=F32)
        dwi_ref[...] += lax.dot_general(xcb, dpib, tn, preferred_element_type=F32)

        def rowsum(v):
            return jnp.sum(v, axis=0, keepdims=True)

        sm_ref[R_CAB:R_CAB + 1, :] += rowsum(dxc)
        sm_ref[R_BR:R_BR + 1, :] += rowsum(dpr)
        sm_ref[R_BI:R_BI + 1, :] += rowsum(dpi)
        sm_ref[R_LAM:R_LAM + 1, :] += rowsum(dlog_a * ((-LRU_C) * r))

        dxc_buf[0:Tc, :] = dxc
        xav = xa_ref[...]
        dxa = jnp.zeros((Tc, bw), F32)
        for k in range(4):
            sh = dxc_buf[3 - k:3 - k + Tc, :]
            dxa = dxa + p_ref[R_CAW + k:R_CAW + k + 1, :] * sh
            sm_ref[R_CAW + k:R_CAW + k + 1, :] += rowsum(xav * sh)
        dxa_ref[...] = dxa.astype(BF16)

        dpbv, cbv, ccv, cxv = dpb_ref[...], cb_ref[...], cc_ref[...], cx_ref[...]
        dcb_ref[...] = (dpbv * vb_ref[...]).astype(BF16)
        dvb_buf[0:Tc, :] = dpbv * cbv
        u = ccv * cxv
        du = jnp.zeros((Tc, bw), F32)
        for k in range(3):
            sh = dvb_buf[2 - k:2 - k + Tc, :]
            du = du + p_ref[R_CBW + k:R_CBW + k + 1, :] * sh
            sm_ref[R_CBW + k:R_CBW + k + 1, :] += rowsum(u * sh)
        dcc_ref[...] = (du * cxv).astype(BF16)
        dcx_ref[...] = (du * ccv).astype(BF16)

        a_buf[Tc:Tc + HALO, :] = a_buf[0:HALO, :]
        dxc_buf[Tc:Tc + HALO, :] = dxc_buf[0:HALO, :]
        dvb_buf[Tc:Tc + HALO, :] = dvb_buf[0:HALO, :]

        @pl.when(i == nT - 1)
        def _():
            sm_ref[R_LAM:R_LAM + 1, :] = sm_ref[R_LAM:R_LAM + 1, :] * (-_sigmoid(-p_ref[R_LAM:R_LAM + 1, :]))

    def seg(s):
        return pl.BlockSpec((Tc, bw), lambda n, i: (nT - 1 - i, s * nb + n))

    blk = pl.BlockSpec((Tc, bw), lambda n, i: (nT - 1 - i, n))
    halo = pl.BlockSpec((8, bw), lambda n, i: (jnp.maximum((nT - 1 - i) * tpc - 1, 0), n))
    p_spec = pl.BlockSpec((None, R_ROWS, bw), lambda n, i: (layer, 0, n))
    w_spec = pl.BlockSpec((None, None, bw, bw), lambda n, i: (layer, n, 0, 0))
    dw_spec = pl.BlockSpec((None, bw, bw), lambda n, i: (n, 0, 0))
    act = jax.ShapeDtypeStruct((T, D), BF16)
    return pl.pallas_call(
        body,
        grid=(nb, nT),
        in_specs=[seg(0), seg(1), seg(2), seg(3), seg(4), blk, blk, halo, blk, blk, blk, p_spec, w_spec, w_spec],
        out_specs=[blk] * 5 + [dw_spec, dw_spec, pl.BlockSpec((R_ROWS, bw), lambda n, i: (0, n))],
        out_shape=[act] * 5 + [jax.ShapeDtypeStruct((nb, bw, bw), F32), jax.ShapeDtypeStruct((nb, bw, bw), F32),
                               jax.ShapeDtypeStruct((R_ROWS, D), F32)],
        scratch_shapes=[pltpu.VMEM((Tc + HALO, bw), F32)] * 4 + [pltpu.VMEM((Tc, bw), F32)] * 3
        + [pltpu.VMEM((8, bw), F32)],
        compiler_params=_cparams(("parallel", "arbitrary")),
        name=name,
    )(z, z, z, z, z, xc, hl, hl, vb, dpa, dpb, pch, wr, wi)


def _local_fwd_bwd(x, tgt, W):
    T, D = x.shape
    depth = len(W["w_in"])
    FF = W["w1"][0].shape[1]
    g1, g2 = W["g1"], W["g2"]
    saved = []
    xs = x
    for l in range(depth):
        h = _rms_fwd(f"rms1_fwd_{l}", xs, g1[l][None])
        z = _mm(f"in_proj_{l}", "nn", h, W["w_in"][l], T, 7 * D, D, [F32],
                epilogue=lambda acc, b: (acc + b,), extras=[(W["b_in"][l][None], "bias", 0)])
        pa, pb, xc, hl, vb = _mixer_fwd(f"mixer_fwd_{l}", l, z, W["pch"], W["wr"], W["wi"])
        oa = _mm(f"proj_a_{l}", "nn", pa, W["w_pa"][l], T, D, D, [F32])

        def merge(acc, oav, ga, gb):
            return acc, _sigmoid(ga) * oav + _sigmoid(gb) * acc

        ob, mg = _mm(f"proj_b_merge_{l}", "nn", pb, W["w_pb"][l], T, D, D, [F32, BF16], epilogue=merge,
                     tiles=MM_TILES_FUSED,
                     extras=[(oa, "tile", 0), (z, "tile", 5 * D), (z, "tile", 6 * D)])
        x1 = _mm(f"out_proj_{l}", "nn", mg, W["w_o"][l], T, D, D, [F32],
                 epilogue=lambda acc, res: (res + acc,), extras=[(xs, "tile", 0)])
        h2 = _rms_fwd(f"rms2_fwd_{l}", x1, g2[l][None])

        def relu2(acc):
            pr = jnp.maximum(acc, 0.0)
            return pr * pr, pr

        u, pr = _mm(f"mlp1_{l}", "nn", h2, W["w1"][l], T, FF, D, [BF16, BF16], epilogue=relu2)
        x2 = _mm(f"mlp2_{l}", "nn", u, W["w2"][l], T, D, FF, [F32],
                 epilogue=lambda acc, res: (res + acc,), extras=[(x1, "tile", 0)])
        saved.append(dict(x0=xs, h=h, z=z, pa=pa, pb=pb, xc=xc, hl=hl, vb=vb, oa=oa, ob=ob, mg=mg, x1=x1,
                          h2=h2, u=u, pr=pr))
        xs = x2

    dx, dxb, dgf, loss_blk = _loss_head("loss_head", xs, W["gf"][None], tgt)
    grads = [None] * depth
    for l in reversed(range(depth)):
        s = saved[l]
        dp = _mm(f"mlp2_dx_{l}", "nt", dxb, W["w2"][l], T, FF, D, [BF16],
                 epilogue=lambda acc, prv: (2.0 * prv.astype(F32) * acc,), extras=[(s["pr"], "tile", 0)])
        dw2 = _mm(f"mlp2_dw_{l}", "tn", s["u"], dxb, FF, D, T, [BF16])
        dh2 = _mm(f"mlp1_dx_{l}", "nt", dp, W["w1"][l], T, D, FF, [F32])
        dw1 = _mm(f"mlp1_dw_{l}", "tn", s["h2"], dp, D, FF, T, [BF16])
        dx1, dx1b, dg2 = _rms_bwd(f"rms2_bwd_{l}", s["x1"], g2[l][None], dh2, dx)

        def unmerge(acc, ga, gb, oav, obv):
            sa, sb = _sigmoid(ga), _sigmoid(gb)
            return acc * sa, acc * sb, acc * oav * sa * (1.0 - sa), acc * obv * sb * (1.0 - sb)

        doa, dob, dga, dgb = _mm(f"out_proj_dx_{l}", "nt", dx1b, W["w_o"][l], T, D, D, [BF16] * 4,
                                 tiles=MM_TILES_FUSED,
                                 epilogue=unmerge,
                                 extras=[(s["z"], "tile", 5 * D), (s["z"], "tile", 6 * D),
                                         (s["oa"], "tile", 0), (s["ob"], "tile", 0)])
        dwo = _mm(f"out_proj_dw_{l}", "tn", s["mg"], dx1b, D, D, T, [BF16])
        dpa = _mm(f"proj_a_dx_{l}", "nt", doa, W["w_pa"][l], T, D, D, [F32])
        dwpa = _mm(f"proj_a_dw_{l}", "tn", s["pa"], doa, D, D, T, [BF16])
        dpb = _mm(f"proj_b_dx_{l}", "nt", dob, W["w_pb"][l], T, D, D, [F32])
        dwpb = _mm(f"proj_b_dw_{l}", "tn", s["pb"], dob, D, D, T, [BF16])
        dxa, dya, dcb, dcc, dcx, dwr, dwi, sm = _mixer_bwd(
            f"mixer_bwd_{l}", l, s["z"], s["xc"], s["hl"], s["vb"], dpa, dpb, W["pch"], W["wr"], W["wi"])
        dz = jnp.concatenate([dxa, dya, dcb, dcc, dcx, dga, dgb], axis=1)
        dbin = _colsum(f"bias_grad_{l}", dz)
        dh = _mm(f"in_proj_dx_{l}", "nt", dz, W["w_in"][l], T, D, 7 * D, [F32])
        dwin = _mm(f"in_proj_dw_{l}", "tn", s["h"], dz, D, 7 * D, T, [BF16])
        dx, dxb, dg1 = _rms_bwd(f"rms1_bwd_{l}", s["x0"], g1[l][None], dh, dx1)
        grads[l] = dict(w_in=dwin, w_pa=dwpa, w_pb=dwpb, w_o=dwo, w1=dw1, w2=dw2, wr=dwr, wi=dwi,
                        sm=sm, b_in=dbin, g1=dg1, g2=dg2)
    return loss_blk[0, 0], dx, grads, dgf


ANY = pl.BlockSpec(memory_space=pl.ANY)


def _place():
    x, y, c = lax.axis_index("x"), lax.axis_index("y"), lax.axis_index("c")
    peers = [(1 - x, y, c), (x, 1 - y, c), (1 - x, 1 - y, c)]
    chips = [2 * (1 - x) + y, 2 * x + (1 - y), 2 * (1 - x) + (1 - y)]
    return (x, y, c), 2 * x + y, peers, chips


def _window(ref, dim, q, size):
    idx = [slice(None)] * len(ref.shape)
    idx[dim] = pl.ds(q * size, size)
    return ref.at[tuple(idx)]


def _gather_weights(shards, dims, small):
    n = len(shards)
    sizes = [s.shape[d] for s, d in zip(shards, dims)]
    full = [jax.ShapeDtypeStruct(s.shape[:d] + (s.shape[d] * N_CHIPS,) + s.shape[d + 1:], s.dtype)
            for s, d in zip(shards, dims)]
    full.append(jax.ShapeDtypeStruct((N_CHIPS,) + small.shape, small.dtype))

    def body(*refs):
        ins, outs = refs[:n + 1], refs[n + 1:2 * n + 2]
        send_sems, recv_sems, local_sems = refs[2 * n + 2:]
        _, k, peers, chips = _place()

        def dst(w, q):
            return outs[w].at[q] if w == n else _window(outs[w], dims[w], q, sizes[w])

        local = [pltpu.make_async_copy(ins[w], dst(w, k), local_sems.at[w]) for w in range(n + 1)]
        for cp in local:
            cp.start()
        sends = []
        for p, peer in enumerate(peers):
            for w in range(n + 1):
                s = p * (n + 1) + w
                sends.append(pltpu.make_async_remote_copy(
                    src_ref=ins[w], dst_ref=dst(w, k), send_sem=send_sems.at[s], recv_sem=recv_sems.at[s],
                    device_id=peer, device_id_type=MESH))
        for cp in sends:
            cp.start()
        for p, peer in enumerate(peers):
            for w in range(n + 1):
                s = p * (n + 1) + w
                pltpu.make_async_remote_copy(
                    src_ref=ins[w], dst_ref=dst(w, chips[p]), send_sem=send_sems.at[s], recv_sem=recv_sems.at[s],
                    device_id=peer, device_id_type=MESH).wait_recv()
        for cp in sends:
            cp.wait_send()
        for cp in local:
            cp.wait()

    return pl.pallas_call(
        body,
        in_specs=[ANY] * (n + 1),
        out_specs=[ANY] * (n + 1),
        out_shape=full,
        scratch_shapes=[pltpu.SemaphoreType.DMA((3 * (n + 1),)), pltpu.SemaphoreType.DMA((3 * (n + 1),)),
                        pltpu.SemaphoreType.DMA((n + 1,))],
        name="gather_weights",
    )(*shards, small)


def _scatter_grads(grads, dims):
    n, depth = len(grads), len(grads[0])
    sizes = [g[0].shape[d] // N_CHIPS for g, d in zip(grads, dims)]
    land = []
    for g, d, sz in zip(grads, dims, sizes):
        shp = g[0].shape
        land.append(jax.ShapeDtypeStruct((N_CHIPS, depth) + shp[:d] + (sz,) + shp[d + 1:], g[0].dtype))

    def body(*refs):
        ins, outs = refs[:n * depth], refs[n * depth:n * depth + n]
        send_sems, recv_sems, local_sems = refs[n * depth + n:]
        _, k, peers, chips = _place()

        def src(w, l, q):
            return _window(ins[w * depth + l], dims[w], q, sizes[w])

        local = [pltpu.make_async_copy(src(w, l, k), outs[w].at[3, l], local_sems.at[w * depth + l])
                 for w in range(n) for l in range(depth)]
        for cp in local:
            cp.start()
        sends = []
        for p, peer in enumerate(peers):
            for w in range(n):
                for l in range(depth):
                    s = (p * n + w) * depth + l
                    sends.append(pltpu.make_async_remote_copy(
                        src_ref=src(w, l, chips[p]), dst_ref=outs[w].at[p, l], send_sem=send_sems.at[s],
                        recv_sem=recv_sems.at[s], device_id=peer, device_id_type=MESH))
        for cp in sends:
            cp.start()
        for cp in sends:
            cp.wait_recv()
        for cp in sends:
            cp.wait_send()
        for cp in local:
            cp.wait()

    flat = [g for gl in grads for g in gl]
    return pl.pallas_call(
        body,
        in_specs=[ANY] * (n * depth),
        out_specs=[ANY] * n,
        out_shape=land,
        scratch_shapes=[pltpu.SemaphoreType.DMA((3 * n * depth,)), pltpu.SemaphoreType.DMA((3 * n * depth,)),
                        pltpu.SemaphoreType.DMA((n * depth,))],
        name="scatter_grads",
    )(*flat)


def _sum_slots(name, land):
    _, R, C = land.shape
    tr, tc = _div_tile(R, 512, 8), _div_tile(C, 1024)

    def body(a_ref, b_ref, c_ref, d_ref, o_ref):
        o_ref[...] = ((d_ref[...].astype(F32) + a_ref[...].astype(F32)) + b_ref[...].astype(F32)) \
            + c_ref[...].astype(F32)

    def slot(q):
        return pl.BlockSpec((None, tr, tc), lambda i, j: (q, i, j))

    return pl.pallas_call(
        body,
        grid=(R // tr, C // tc),
        in_specs=[slot(0), slot(1), slot(2), slot(3)],
        out_specs=pl.BlockSpec((tr, tc), lambda i, j: (i, j)),
        out_shape=jax.ShapeDtypeStruct((R, C), F32),
        compiler_params=_cparams(("parallel", "parallel")),
        name=name,
    )(land, land, land, land)


def _swap_with_sibling(parts):
    n = len(parts)

    def body(*refs):
        ins, outs = refs[:n], refs[n:2 * n]
        send_sems, recv_sems = refs[2 * n:]
        (x, y, c), _, _, _ = _place()
        copies = [pltpu.make_async_remote_copy(
            src_ref=ins[w], dst_ref=outs[w], send_sem=send_sems.at[w], recv_sem=recv_sems.at[w],
            device_id=(x, y, 1 - c), device_id_type=MESH) for w in range(n)]
        for cp in copies:
            cp.start()
        for cp in copies:
            cp.wait()

    return pl.pallas_call(
        body,
        in_specs=[ANY] * n,
        out_specs=[ANY] * n,
        out_shape=[jax.ShapeDtypeStruct(p.shape, p.dtype) for p in parts],
        scratch_shapes=[pltpu.SemaphoreType.DMA((n,)), pltpu.SemaphoreType.DMA((n,))],
        name="swap_with_sibling",
    )(*parts)


def _comm_call(name, ins, out_shapes, plan, n_local, n_remote, aliases=None):
    n_in, n_out = len(ins), len(out_shapes)

    def body(*refs):
        in_refs, out_refs = refs[:n_in], refs[n_in:n_in + n_out]
        sems = refs[n_in + n_out:]
        local, remote = plan(in_refs, out_refs)
        assert len(local) == n_local and len(remote) == n_remote, (name, len(local), len(remote))
        lcs = [pltpu.make_async_copy(s, d, sems[2].at[i]) for i, (s, d) in enumerate(local)]
        for cp in lcs:
            cp.start()
        sends = [pltpu.make_async_remote_copy(src_ref=s, dst_ref=d, send_sem=sems[0].at[i], recv_sem=sems[1].at[i],
                                              device_id=peer, device_id_type=MESH)
                 for i, (s, d, peer, _) in enumerate(remote)]
        for cp in sends:
            cp.start()
        for i, (s, _, peer, landing) in enumerate(remote):
            pltpu.make_async_remote_copy(src_ref=s, dst_ref=landing, send_sem=sems[0].at[i], recv_sem=sems[1].at[i],
                                         device_id=peer, device_id_type=MESH).wait_recv()
        for cp in sends:
            cp.wait_send()
        for cp in lcs:
            cp.wait()

    scratch = [pltpu.SemaphoreType.DMA((n_remote,)), pltpu.SemaphoreType.DMA((n_remote,))]
    if n_local:
        scratch.append(pltpu.SemaphoreType.DMA((n_local,)))
    return pl.pallas_call(
        body,
        in_specs=[ANY] * n_in,
        out_specs=[ANY] * n_out,
        out_shape=out_shapes,
        scratch_shapes=scratch,
        input_output_aliases=aliases or {},
        name=name,
    )(*ins)


def _place_shard(name, w, layer, dim, chip):
    _, a, b = w.shape
    full = (a * N_CHIPS, b) if dim == 0 else (a, b * N_CHIPS)
    tr, tc = _div_tile(a, 512, 16), _div_tile(b, 2048)
    nr, nc = a // tr, b // tc

    def out_map(i, j, chip_ref):
        return (chip_ref[0] * nr + i, j) if dim == 0 else (i, chip_ref[0] * nc + j)

    def body(chip_ref, w_ref, o_ref):
        o_ref[...] = w_ref[...].astype(o_ref.dtype)

    return pl.pallas_call(
        body,
        grid_spec=pltpu.PrefetchScalarGridSpec(
            num_scalar_prefetch=1, grid=(nr, nc),
            in_specs=[pl.BlockSpec((None, tr, tc), lambda i, j, chip_ref: (layer, i, j))],
            out_specs=pl.BlockSpec((tr, tc), out_map)),
        out_shape=jax.ShapeDtypeStruct(full, BF16),
        compiler_params=_cparams(("parallel", "parallel")),
        name=name,
    )(chip, w)


def _gather_halves(name, full, dims):
    n = len(full)
    sizes = [f.shape[d] // N_CHIPS for f, d in zip(full, dims)]

    def plan(ins, outs):
        (_, _, c), k, peers, chips = _place()
        remote = []
        for w in range(n):
            d, h = dims[w], sizes[w] // 2
            mine = _window(outs[w], d, 2 * k + c, h)
            for p in range(3):
                remote.append((mine, mine, peers[p], _window(outs[w], d, 2 * chips[p] + c, h)))
        return [], remote

    return _comm_call(name, full, [jax.ShapeDtypeStruct(f.shape, f.dtype) for f in full], plan, 0, 3 * n,
                      aliases={w: w for w in range(n)})


def _forward_halves(name, full, dims):
    n = len(full)
    sizes = [f.shape[d] // N_CHIPS for f, d in zip(full, dims)]

    def plan(ins, outs):
        (x, y, c), _, _, chips = _place()
        remote = []
        for w in range(n):
            d, h = dims[w], sizes[w] // 2
            for p in range(3):
                mine = _window(outs[w], d, 2 * chips[p] + c, h)
                remote.append((mine, mine, (x, y, 1 - c), _window(outs[w], d, 2 * chips[p] + 1 - c, h)))
        return [], remote

    return _comm_call(name, full, [jax.ShapeDtypeStruct(f.shape, f.dtype) for f in full], plan, 0, 3 * n,
                      aliases={w: w for w in range(n)})


def _half_shape(shape, dim):
    return shape[:dim] + (shape[dim] // (2 * N_CHIPS),) + shape[dim + 1:]


def _swap_halves(name, grads, dims):
    n = len(grads)
    outs_shape = [jax.ShapeDtypeStruct((N_CHIPS,) + _half_shape(g.shape, d), g.dtype) for g, d in zip(grads, dims)]

    def plan(ins, outs):
        (x, y, c), _, _, _ = _place()
        remote = []
        for w in range(n):
            d, h = dims[w], grads[w].shape[dims[w]] // (2 * N_CHIPS)
            for q in range(N_CHIPS):
                remote.append((_window(ins[w], d, 2 * q + 1 - c, h), outs[w].at[q], (x, y, 1 - c), outs[w].at[q]))
        return [], remote

    return _comm_call(name, grads, outs_shape, plan, 0, N_CHIPS * n)


def _add_halves(name, g, got, dim, core):
    R, C = g.shape
    if dim == 1:
        r, cc = R, C // (2 * N_CHIPS)
    else:
        r, cc = R // (2 * N_CHIPS), C
    tr, tc = _div_tile(r, 512, 16), _div_tile(cc, 1024)
    nr, nc = r // tr, cc // tc

    def g_map(q, i, j, core_ref):
        w = 2 * q + core_ref[0]
        return (i, w * nc + j) if dim == 1 else (w * nr + i, j)

    def body(core_ref, g_ref, got_ref, o_ref):
        o_ref[...] = (g_ref[...].astype(F32) + got_ref[...].astype(F32)).astype(o_ref.dtype)

    slab = pl.BlockSpec((None, tr, tc), lambda q, i, j, core_ref: (q, i, j))
    return pl.pallas_call(
        body,
        grid_spec=pltpu.PrefetchScalarGridSpec(
            num_scalar_prefetch=1, grid=(N_CHIPS, nr, nc),
            in_specs=[pl.BlockSpec((tr, tc), g_map), slab], out_specs=slab),
        out_shape=jax.ShapeDtypeStruct((N_CHIPS, r, cc), g.dtype),
        compiler_params=_cparams(("parallel", "parallel", "parallel")),
        name=name,
    )(core, g, got)


def _scatter_halves(name, sums):
    n = len(sums)

    def plan(ins, outs):
        _, _, peers, chips = _place()
        remote = []
        for w in range(n):
            for p in range(3):
                remote.append((ins[w].at[chips[p]], outs[w].at[p], peers[p], outs[w].at[p]))
        return [], remote

    return _comm_call(name, sums, [jax.ShapeDtypeStruct((3,) + s.shape[1:], s.dtype) for s in sums], plan, 0, 3 * n)


def _reduce_into(name, sums, land, acc, layer, dim, shape, where):
    _, r, cc = sums.shape
    tr, tc = _div_tile(r, 512, 16), _div_tile(cc, 1024)
    nr, nc = r // tr, cc // tc

    def out_map(i, j, s):
        return (layer, s[1] * nr + i, j) if dim == 1 else (layer, i, s[1] * nc + j)

    def body(*refs):
        own, a_ref, b_ref, c_ref, o_ref = refs[1], refs[2], refs[3], refs[4], refs[-1]
        o_ref[...] = ((own[...].astype(F32) + a_ref[...].astype(F32)) + b_ref[...].astype(F32)) \
            + c_ref[...].astype(F32)

    def slot(p):
        return pl.BlockSpec((None, tr, tc), lambda i, j, s: (p, i, j))

    in_specs = [pl.BlockSpec((None, tr, tc), lambda i, j, s: (s[0], i, j)), slot(0), slot(1), slot(2)]
    args = [where, sums, land, land, land]
    if acc is not None:
        in_specs.append(ANY)
        args.append(acc)
    return pl.pallas_call(
        body,
        grid_spec=pltpu.PrefetchScalarGridSpec(
            num_scalar_prefetch=1, grid=(nr, nc), in_specs=in_specs,
            out_specs=pl.BlockSpec((None, tr, tc), out_map)),
        out_shape=jax.ShapeDtypeStruct(shape, F32),
        input_output_aliases={5: 0} if acc is not None else {},
        compiler_params=_cparams(("parallel", "parallel")),
        name=name,
    )(*args)


def _join_halves(name, grads, dims):
    n = len(grads)

    def plan(ins, outs):
        (x, y, c), _, _, _ = _place()
        remote = []
        for w in range(n):
            d, h = dims[w], grads[w].shape[dims[w]] // 2
            mine = _window(outs[w], d, c, h)
            remote.append((mine, mine, (x, y, 1 - c), _window(outs[w], d, 1 - c, h)))
        return [], remote

    return _comm_call(name, grads, [jax.ShapeDtypeStruct(g.shape, g.dtype) for g in grads], plan, 0, n,
                      aliases={w: w for w in range(n)})


def _allreduce_small(pack):
    R, C = pack.shape

    def gather_body(in_ref, slots_ref, send_sems, recv_sems, local_sem):
        x, y, c = lax.axis_index("x"), lax.axis_index("y"), lax.axis_index("c")
        me = 4 * x + 2 * y + c
        flips = [(dx, dy, dc) for dx in (0, 1) for dy in (0, 1) for dc in (0, 1)][1:]

        def flip(v, d):
            return 1 - v if d else v

        local = pltpu.make_async_copy(in_ref, slots_ref.at[me], local_sem)
        local.start()
        sends = []
        for j, (dx, dy, dc) in enumerate(flips):
            px, py, pc = flip(x, dx), flip(y, dy), flip(c, dc)
            sends.append((pltpu.make_async_remote_copy(
                src_ref=in_ref, dst_ref=slots_ref.at[me], send_sem=send_sems.at[j], recv_sem=recv_sems.at[j],
                device_id=(px, py, pc), device_id_type=MESH), 4 * px + 2 * py + pc, j))
        for cp, _, _ in sends:
            cp.start()
        for cp, peer_id, j in sends:
            pltpu.make_async_remote_copy(
                src_ref=in_ref, dst_ref=slots_ref.at[peer_id], send_sem=send_sems.at[j], recv_sem=recv_sems.at[j],
                device_id=(x, y, c), device_id_type=MESH).wait_recv()
        for cp, _, _ in sends:
            cp.wait_send()
        local.wait()

    slots = pl.pallas_call(
        gather_body,
        in_specs=[ANY],
        out_specs=ANY,
        out_shape=jax.ShapeDtypeStruct((N_DEV, R, C), pack.dtype),
        scratch_shapes=[pltpu.SemaphoreType.DMA((N_DEV - 1,)), pltpu.SemaphoreType.DMA((N_DEV - 1,)),
                        pltpu.SemaphoreType.DMA],
        name="allgather_small",
    )(pack)

    def sum_body(s_ref, o_ref):
        acc = s_ref[0]
        for d in range(1, N_DEV):
            acc = acc + s_ref[d]
        o_ref[...] = acc

    return pl.pallas_call(
        sum_body,
        out_shape=jax.ShapeDtypeStruct((R, C), pack.dtype),
        name="sum_small",
    )(slots)


def _adamw_math(w, g, m, v):
    m2 = ADAM_B1 * m + (1.0 - ADAM_B1) * g
    v2 = ADAM_B2 * v + (1.0 - ADAM_B2) * (g * g)
    m_hat = m2 / (1.0 - ADAM_B1 ** ADAM_STEP)
    v_hat = v2 / (1.0 - ADAM_B2 ** ADAM_STEP)
    delta = -ADAM_LR * (m_hat / (jnp.sqrt(v_hat) + ADAM_EPS) + ADAM_WD * w)
    return delta, m2, v2


def _adamw_big(name, w, m, v, g_parts):
    shape = w.shape
    C = shape[-1]
    R = w.size // C
    tr, tc = _div_tile(R, 256, 8), _div_tile(C, 1024)
    n_g = len(g_parts)

    def body(*refs):
        w_ref, m_ref, v_ref = refs[:3]
        g_ref, d_ref, nm_ref, nv_ref = refs[3 + n_g:]
        g = refs[3][...]
        for extra in refs[4:3 + n_g]:
            g = g + extra[...]
        delta, m2, v2 = _adamw_math(w_ref[...], g, m_ref[...], v_ref[...])
        g_ref[...], d_ref[...], nm_ref[...], nv_ref[...] = g, delta, m2, v2

    blk = pl.BlockSpec((tr, tc), lambda i, j: (i, j))
    outs = pl.pallas_call(
        body,
        grid=(R // tr, C // tc),
        in_specs=[blk] * (3 + n_g),
        out_specs=[blk] * 4,
        out_shape=[jax.ShapeDtypeStruct((R, C), F32)] * 4,
        compiler_params=_cparams(("parallel", "parallel")),
        name=name,
    )(w.reshape(R, C), m.reshape(R, C), v.reshape(R, C), *[g.reshape(R, C) for g in g_parts])
    return [o.reshape(shape) for o in outs]


def _adamw_small(name, w, g, m, v):
    shape = w.shape
    two_d = (w.size // shape[-1], shape[-1])

    def body(w_ref, g_ref, m_ref, v_ref, d_ref, nm_ref, nv_ref):
        d_ref[...], nm_ref[...], nv_ref[...] = _adamw_math(w_ref[...], g_ref[...], m_ref[...], v_ref[...])

    outs = pl.pallas_call(
        body,
        out_shape=[jax.ShapeDtypeStruct(two_d, F32)] * 3,
        name=name,
    )(w.reshape(two_d), g.reshape(two_d), m.reshape(two_d), v.reshape(two_d))
    return [o.reshape(shape) for o in outs]


SMALL_ROWS = 40
S_BIN, S_G1, S_G2 = 16, 24, 32
MATS = ("w_in", "w_pa", "w_pb", "w_o", "w_mlp1", "w_mlp2")
LRU = ("lru_wr", "lru_wi")
BIG_DIM = dict(w_in=2, w_pa=1, w_pb=1, w_o=1, w_mlp1=2, w_mlp2=1, lru_wr=2, lru_wi=2)
WEIGHTS = ("norm1_g", "w_in", "b_in", "conv_a_w", "conv_a_b", "lru_wr", "lru_br", "lru_wi", "lru_bi", "lru_lam",
           "conv_b_w", "w_pa", "w_pb", "w_o", "norm2_g", "w_mlp1", "w_mlp2", "final_g")


def _rows_at(a, r0, total):
    pad = [(0, 0)] * a.ndim
    pad[-2] = (r0, total - r0 - a.shape[-2])
    return jnp.pad(a, pad)


def kernel(x, norm1_g, w_in, b_in, conv_a_w, conv_a_b, lru_wr, lru_br, lru_wi, lru_bi, lru_lam, conv_b_w, w_pa, w_pb, w_o, norm2_g, w_mlp1, w_mlp2, final_g, loss_target, m_norm1_g, m_w_in, m_b_in, m_conv_a_w, m_conv_a_b, m_lru_wr, m_lru_br, m_lru_wi, m_lru_bi, m_lru_lam, m_conv_b_w, m_w_pa, m_w_pb, m_w_o, m_norm2_g, m_w_mlp1, m_w_mlp2, m_final_g, v_norm1_g, v_w_in, v_b_in, v_conv_a_w, v_conv_a_b, v_lru_wr, v_lru_br, v_lru_wi, v_lru_bi, v_lru_lam, v_conv_b_w, v_w_pa, v_w_pb, v_w_o, v_norm2_g, v_w_mlp1, v_w_mlp2, v_final_g):
    wts = dict(norm1_g=norm1_g, w_in=w_in, b_in=b_in, conv_a_w=conv_a_w, conv_a_b=conv_a_b, lru_wr=lru_wr,
               lru_br=lru_br, lru_wi=lru_wi, lru_bi=lru_bi, lru_lam=lru_lam, conv_b_w=conv_b_w, w_pa=w_pa,
               w_pb=w_pb, w_o=w_o, norm2_g=norm2_g, w_mlp1=w_mlp1, w_mlp2=w_mlp2, final_g=final_g)
    mom = dict(norm1_g=m_norm1_g, w_in=m_w_in, b_in=m_b_in, conv_a_w=m_conv_a_w, conv_a_b=m_conv_a_b,
               lru_wr=m_lru_wr, lru_br=m_lru_br, lru_wi=m_lru_wi, lru_bi=m_lru_bi, lru_lam=m_lru_lam,
               conv_b_w=m_conv_b_w, w_pa=m_w_pa, w_pb=m_w_pb, w_o=m_w_o, norm2_g=m_norm2_g, w_mlp1=m_w_mlp1,
               w_mlp2=m_w_mlp2, final_g=m_final_g)
    vel = dict(norm1_g=v_norm1_g, w_in=v_w_in, b_in=v_b_in, conv_a_w=v_conv_a_w, conv_a_b=v_conv_a_b,
               lru_wr=v_lru_wr, lru_br=v_lru_br, lru_wi=v_lru_wi, lru_bi=v_lru_bi, lru_lam=v_lru_lam,
               conv_b_w=v_conv_b_w, w_pa=v_w_pa, w_pb=v_w_pb, w_o=v_w_o, norm2_g=v_norm2_g, w_mlp1=v_w_mlp1,
               w_mlp2=v_w_mlp2, final_g=v_final_g)
    depth, D = norm1_g.shape
    nb, bw = lru_wr.shape[1], lru_wr.shape[3]
    chip = 2 * lax.axis_index("x") + lax.axis_index("y")

    small_parts = [conv_a_w.reshape(-1), conv_b_w.reshape(-1), lru_br.reshape(-1), lru_bi.reshape(-1)]
    small_len = sum(p.shape[0] for p in small_parts)
    small_rows = -(-small_len // 1024) * 8
    small = jnp.concatenate(small_parts + [jnp.zeros((small_rows * 128 - small_len,), F32)]).reshape(small_rows, 128)
    gathered = _gather_weights([wts[n].astype(BF16) for n in LRU], [BIG_DIM[n] for n in LRU], small)
    full = dict(zip(LRU, gathered[:-1]))
    items = [(n, l) for l in range(depth) for n in MATS]
    mat_dims = [BIG_DIM[n] - 1 for n, _ in items]
    where = jnp.stack([chip, lax.axis_index("c")]).astype(jnp.int32)
    placed = [_place_shard(f"place_{n}_{l}", wts[n], l, BIG_DIM[n] - 1, where) for n, l in items]
    halves = _gather_halves("gather_halves", placed, mat_dims)
    mats = _forward_halves("forward_halves", halves, mat_dims)
    for n in MATS:
        full[n] = [mats[items.index((n, l))] for l in range(depth)]
    flat = gathered[-1].reshape(N_CHIPS, small_rows * 128)
    off = 0
    small_full = []
    for part, shard in zip(small_parts, (conv_a_w, conv_b_w, lru_br, lru_bi)):
        piece = flat[:, off:off + part.shape[0]].reshape((N_CHIPS,) + shard.shape)
        small_full.append(jnp.moveaxis(piece, 0, -2).reshape(shard.shape[:-1] + (N_CHIPS * shard.shape[-1],)))
        off += part.shape[0]
    caw_f, cbw_f, br_f, bi_f = small_full
    pch = (_rows_at(conv_a_b[:, None, :], R_CAB, R_ROWS) + _rows_at(br_f.reshape(depth, 1, D), R_BR, R_ROWS)
           + _rows_at(bi_f.reshape(depth, 1, D), R_BI, R_ROWS) + _rows_at(lru_lam[:, None, :], R_LAM, R_ROWS)
           + _rows_at(caw_f, R_CAW, R_ROWS) + _rows_at(cbw_f, R_CBW, R_ROWS))
    W = dict(w_in=full["w_in"], b_in=b_in, pch=pch, wr=full["lru_wr"], wi=full["lru_wi"], w_pa=full["w_pa"],
             w_pb=full["w_pb"], w_o=full["w_o"], w1=full["w_mlp1"], w2=full["w_mlp2"], g1=norm1_g, g2=norm2_g,
             gf=final_g)

    loss_local, dx, grads, dgf = _local_fwd_bwd(x[0], loss_target[0], W)
    loss = lax.psum(loss_local, ("x", "y", "c"))

    key = dict(w_in="w_in", w_pa="w_pa", w_pb="w_pb", w_o="w_o", w_mlp1="w1", w_mlp2="w2", lru_wr="wr", lru_wi="wi")
    out_g, out_d, out_m, out_v = {}, {}, {}, {}
    per_layer = [[grads[l][key[n]].astype(BF16) for l in range(depth)] for n in LRU]
    land = _scatter_grads(per_layer, [BIG_DIM[n] - 1 for n in LRU])
    chip_sums = [_sum_slots(f"sum_slots_{n}", ld.reshape(N_CHIPS, -1, ld.shape[-1])) for n, ld in zip(LRU, land)]
    sib_sums = _swap_with_sibling(chip_sums)
    for n, mine, sib in zip(LRU, chip_sums, sib_sums):
        out_g[n], out_d[n], out_m[n], out_v[n] = _adamw_big(f"adamw_{n}", wts[n], mom[n], vel[n], [mine, sib])
    core = lax.axis_index("c").astype(jnp.int32).reshape(1)
    mat_grads = [grads[l][key[n]] for n, l in items]
    got = _swap_halves("swap_halves", mat_grads, mat_dims)
    sums = [_add_halves(f"add_halves_{n}_{l}", g, r, d, core)
            for (n, l), g, r, d in zip(items, mat_grads, got, mat_dims)]
    landed = _scatter_halves("scatter_halves", sums)
    acc = {n: None for n in MATS}
    for (n, l), s, ld in zip(items, sums, landed):
        acc[n] = _reduce_into(f"reduce_{n}_{l}", s, ld, acc[n], l, BIG_DIM[n], wts[n].shape, where)
    joined = _join_halves("join_halves", [acc[n] for n in MATS], [BIG_DIM[n] for n in MATS])
    for n, g in zip(MATS, joined):
        out_g[n], out_d[n], out_m[n], out_v[n] = _adamw_big(f"adamw_{n}", wts[n], mom[n], vel[n], [g])

    rows = []
    for l in range(depth):
        g = grads[l]
        rows.append(_rows_at(g["sm"], 0, SMALL_ROWS) + _rows_at(g["b_in"].reshape(7, D), S_BIN, SMALL_ROWS)
                    + _rows_at(g["g1"], S_G1, SMALL_ROWS) + _rows_at(g["g2"], S_G2, SMALL_ROWS))
    rows.append(_rows_at(dgf, 0, 8))
    tot = _allreduce_small(jnp.concatenate(rows, axis=0))
    per = tot[:depth * SMALL_ROWS].reshape(depth, SMALL_ROWS, D)

    def cols_of_chip(a, axis):
        size = a.shape[axis] // N_CHIPS
        return lax.dynamic_slice_in_dim(a, chip * size, size, axis=axis)

    small_g = dict(
        norm1_g=per[:, S_G1], b_in=per[:, S_BIN:S_BIN + 7].reshape(depth, 7 * D),
        conv_a_w=cols_of_chip(per[:, R_CAW:R_CAW + 4], 2), conv_a_b=per[:, R_CAB],
        lru_br=cols_of_chip(per[:, R_BR].reshape(depth, nb, bw), 2),
        lru_bi=cols_of_chip(per[:, R_BI].reshape(depth, nb, bw), 2), lru_lam=per[:, R_LAM],
        conv_b_w=cols_of_chip(per[:, R_CBW:R_CBW + 3], 2), norm2_g=per[:, S_G2],
        final_g=tot[depth * SMALL_ROWS])
    for n, g in small_g.items():
        out_g[n] = g
        out_d[n], out_m[n], out_v[n] = _adamw_small(f"adamw_{n}", wts[n], g, mom[n], vel[n])

    return (loss, dx[None], *[out_g[n] for n in WEIGHTS], *[out_d[n] for n in WEIGHTS],
            *[out_m[n] for n in WEIGHTS], *[out_v[n] for n in WEIGHTS])
```

```python
import functools

import jax
import jax.numpy as jnp
from jax import lax
from jax.experimental import pallas as pl
from jax.experimental.pallas import tpu as pltpu

F32 = jnp.float32
BF16 = jnp.bfloat16
MESH = pl.DeviceIdType.MESH

EPS = 1e-6
LRU_C = 8.0
ADAM_LR = 0.001
ADAM_B1 = 0.9
ADAM_B2 = 0.999
ADAM_EPS = 1e-08
ADAM_WD = 0.01
ADAM_STEP = 10

N_CHIPS = 4
N_DEV = 8
HALO = 8
VMEM_LIMIT = 56 * 1024 * 1024
MM_TILES = (1024, 1024, 1024)
MM_TILES_FUSED = (512, 1024, 1024)
SEQ_CHUNK = 256
ROW_TILE = 256

R_CAB, R_BR, R_BI, R_LAM, R_CAW, R_CBW, R_ROWS = 0, 1, 2, 3, 4, 8, 16


def _cparams(sem):
    return pltpu.CompilerParams(dimension_semantics=sem, vmem_limit_bytes=VMEM_LIMIT)


def _div_tile(n, pref, unit=128):
    if n <= pref:
        return n
    t = (pref // unit) * unit
    while n % t:
        t -= unit
    return t


def _sigmoid(v):
    return 1.0 / (1.0 + jnp.exp(-v))


def _gelu_and_grad(y):
    k = 0.7978845608028654
    c = 0.044715
    y2 = y * y
    t = jnp.tanh(k * (y + c * y2 * y))
    g = 0.5 * y * (1.0 + t)
    gp = 0.5 * (1.0 + t) + 0.5 * y * (1.0 - t * t) * (k * (1.0 + 3.0 * c * y2))
    return g, gp


def _softplus_neg(lam):
    e = jnp.exp(-jnp.abs(lam))
    w = 1.0 + e
    l1p = jnp.where(w == 1.0, e, jnp.log(w) * e / jnp.where(w == 1.0, 1.0, w - 1.0))
    return jnp.maximum(-lam, 0.0) + l1p


def _mm(name, mode, a, b, M, N, K, out_dtypes, epilogue=None, extras=(), la=None, lb=None, tiles=None,
        carry=None):
    tiles = MM_TILES if tiles is None else tiles
    tm, tn, tk = _div_tile(M, tiles[0]), _div_tile(N, tiles[1]), _div_tile(K, tiles[2])
    assert M % tm == 0 and N % tn == 0 and K % tk == 0, (name, M, N, K)
    nk = K // tk

    def spec(lead, shape, imap):
        if lead is None:
            return pl.BlockSpec(shape, imap)
        return pl.BlockSpec((None,) + shape, lambda i, j, k: (lead,) + imap(i, j, k))

    if mode == "nn":
        a_spec = spec(la, (tm, tk), lambda i, j, k: (i, k))
        b_spec = spec(lb, (tk, tn), lambda i, j, k: (k, j))
        dn = (((1,), (0,)), ((), ()))
    elif mode == "nt":
        a_spec = spec(la, (tm, tk), lambda i, j, k: (i, k))
        b_spec = spec(lb, (tn, tk), lambda i, j, k: (j, k))
        dn = (((1,), (1,)), ((), ()))
    else:
        a_spec = spec(la, (tk, tm), lambda i, j, k: (k, i))
        b_spec = spec(lb, (tk, tn), lambda i, j, k: (k, j))
        dn = (((0,), (0,)), ((), ()))

    ex_arrays, ex_specs = [], []
    for arr, kind, off in extras:
        ex_arrays.append(arr)
        if kind == "bias":
            ex_specs.append(pl.BlockSpec((1, tn), lambda i, j, k: (0, j)))
        else:
            assert off % tn == 0
            ex_specs.append(pl.BlockSpec((tm, tn), lambda i, j, k, o=off // tn: (i, j + o)))
    n_ex, n_out = len(ex_arrays), len(out_dtypes)
    n_cin = len(carry.ins) if carry else 0
    n_cout = len(carry.out_shapes) if carry else 0
    n_in = 2 + n_ex + n_cin
    gi, gj = M // tm, N // tn

    def body(*refs):
        a_ref, b_ref = refs[0], refs[1]
        ex = refs[2:2 + n_ex]
        outs = refs[n_in:n_in + n_out]
        acc = refs[n_in + n_out + n_cout]
        i, j, k = pl.program_id(0), pl.program_id(1), pl.program_id(2)
        if carry:
            c_in, c_out = refs[2 + n_ex:n_in], refs[n_in + n_out:n_in + n_out + n_cout]
            sems = refs[n_in + n_out + n_cout + 1:]

            @pl.when((i == 0) & (j == 0) & (k == 0))
            def _():
                carry.start(c_in, c_out, sems)

        @pl.when(k == 0)
        def _():
            acc[...] = jnp.zeros_like(acc)

        acc[...] += lax.dot_general(a_ref[...], b_ref[...], dn, preferred_element_type=F32)

        @pl.when(k == nk - 1)
        def _():
            r = acc[...]
            vals = (r,) if epilogue is None else epilogue(r, *[e[...] for e in ex])
            for o, v in zip(outs, vals):
                o[...] = v.astype(o.dtype)

        if carry:
            @pl.when((i == gi - 1) & (j == gj - 1) & (k == nk - 1))
            def _():
                carry.finish(c_in, c_out, sems)

    res = pl.pallas_call(
        body,
        grid=(gi, gj, nk),
        in_specs=[a_spec, b_spec, *ex_specs] + [ANY] * n_cin,
        out_specs=[pl.BlockSpec((tm, tn), lambda i, j, k: (i, j)) for _ in range(n_out)] + [ANY] * n_cout,
        out_shape=[jax.ShapeDtypeStruct((M, N), d) for d in out_dtypes] + (carry.out_shapes if carry else []),
        scratch_shapes=[pltpu.VMEM((tm, tn), F32)] + (carry.scratch() if carry else []),
        input_output_aliases={2 + n_ex + ci: n_out + co for ci, co in carry.aliases.items()} if carry else {},
        compiler_params=_cparams(("arbitrary",) * 3 if carry else ("parallel", "parallel", "arbitrary")),
        name=name,
    )(a, b, *ex_arrays, *(carry.ins if carry else []))
    main = res[0] if n_out == 1 else res[:n_out]
    return (main, res[n_out:]) if carry else main


def _rms_fwd(name, x, g_row):
    T, D = x.shape
    tm = min(ROW_TILE, T)

    def body(x_ref, g_ref, h_ref):
        xv = x_ref[...]
        r = lax.rsqrt(jnp.mean(xv * xv, axis=-1, keepdims=True) + EPS)
        h_ref[...] = (xv * r * g_ref[...]).astype(BF16)

    return pl.pallas_call(
        body,
        grid=(T // tm,),
        in_specs=[pl.BlockSpec((tm, D), lambda i: (i, 0)), pl.BlockSpec((1, D), lambda i: (0, 0))],
        out_specs=pl.BlockSpec((tm, D), lambda i: (i, 0)),
        out_shape=jax.ShapeDtypeStruct((T, D), BF16),
        compiler_params=_cparams(("parallel",)),
        name=name,
    )(x, g_row)


def _rms_bwd(name, x, g_row, dh, dres):
    T, D = x.shape
    tm = min(ROW_TILE, T)

    def body(x_ref, g_ref, dh_ref, dres_ref, dx_ref, dxb_ref, dg_ref):
        xv, dhv = x_ref[...], dh_ref[...]
        r = lax.rsqrt(jnp.mean(xv * xv, axis=-1, keepdims=True) + EPS)
        gd = g_ref[...] * dhv
        c = jnp.mean(xv * gd, axis=-1, keepdims=True)
        dx = r * gd - xv * (r * r * r) * c + dres_ref[...]
        dx_ref[...] = dx
        dxb_ref[...] = dx.astype(BF16)

        @pl.when(pl.program_id(0) == 0)
        def _():
            dg_ref[...] = jnp.zeros_like(dg_ref)

        dg_ref[...] += jnp.sum(dhv * xv * r, axis=0, keepdims=True)

    row = pl.BlockSpec((tm, D), lambda i: (i, 0))
    vec = pl.BlockSpec((1, D), lambda i: (0, 0))
    return pl.pallas_call(
        body,
        grid=(T // tm,),
        in_specs=[row, vec, row, row],
        out_specs=[row, row, vec],
        out_shape=[jax.ShapeDtypeStruct((T, D), F32), jax.ShapeDtypeStruct((T, D), BF16),
                   jax.ShapeDtypeStruct((1, D), F32)],
        compiler_params=_cparams(("arbitrary",)),
        name=name,
    )(x, g_row, dh, dres)


def _loss_head(name, x, g_row, tgt):
    T, D = x.shape
    tm = min(ROW_TILE, T)

    def body(x_ref, g_ref, t_ref, dx_ref, dxb_ref, dg_ref, loss_ref):
        xv, g = x_ref[...], g_ref[...]
        r = lax.rsqrt(jnp.mean(xv * xv, axis=-1, keepdims=True) + EPS)
        xh = xv * r
        e = xh * g - t_ref[...]
        lpart = 0.5 * jnp.sum(jnp.mean(e * e, axis=-1, keepdims=True))
        dy = e * (1.0 / D)
        gd = g * dy
        c = jnp.mean(xv * gd, axis=-1, keepdims=True)
        dx = r * gd - xv * (r * r * r) * c
        dx_ref[...] = dx
        dxb_ref[...] = dx.astype(BF16)

        @pl.when(pl.program_id(0) == 0)
        def _():
            dg_ref[...] = jnp.zeros_like(dg_ref)
            loss_ref[...] = jnp.zeros_like(loss_ref)

        dg_ref[...] += jnp.sum(dy * xh, axis=0, keepdims=True)
        loss_ref[...] += jnp.full(loss_ref.shape, lpart, F32)

    row = pl.BlockSpec((tm, D), lambda i: (i, 0))
    vec = pl.BlockSpec((1, D), lambda i: (0, 0))
    return pl.pallas_call(
        body,
        grid=(T // tm,),
        in_specs=[row, vec, row],
        out_specs=[row, row, vec, pl.BlockSpec((8, 128), lambda i: (0, 0))],
        out_shape=[jax.ShapeDtypeStruct((T, D), F32), jax.ShapeDtypeStruct((T, D), BF16),
                   jax.ShapeDtypeStruct((1, D), F32), jax.ShapeDtypeStruct((8, 128), F32)],
        compiler_params=_cparams(("arbitrary",)),
        name=name,
    )(x, g_row, tgt)


def _colsum(name, a):
    T, N = a.shape
    tm, tn = min(512, T), _div_tile(N, 2048)

    def body(a_ref, o_ref):
        @pl.when(pl.program_id(1) == 0)
        def _():
            o_ref[...] = jnp.zeros_like(o_ref)

        o_ref[...] += jnp.sum(a_ref[...].astype(F32), axis=0, keepdims=True)

    return pl.pallas_call(
        body,
        grid=(N // tn, T // tm),
        in_specs=[pl.BlockSpec((tm, tn), lambda j, i: (i, j))],
        out_specs=pl.BlockSpec((1, tn), lambda j, i: (0, j)),
        out_shape=jax.ShapeDtypeStruct((1, N), F32),
        compiler_params=_cparams(("parallel", "arbitrary")),
        name=name,
    )(a)


def _tile_scan(a, b, row, reverse):
    for s in (1, 2, 4):
        if reverse:
            a_s, b_s, m = pltpu.roll(a, 8 - s, 0), pltpu.roll(b, 8 - s, 0), row < 8 - s
        else:
            a_s, b_s, m = pltpu.roll(a, s, 0), pltpu.roll(b, s, 0), row >= s
        b = jnp.where(m, a * b_s + b, b)
        a = jnp.where(m, a * a_s, a)
    return a, b


def _chunk_scan(a_s, b_s, out_ref, carry, n_tiles, width, reverse):
    row = lax.broadcasted_iota(jnp.int32, (8, width), 0)
    edge = 0 if reverse else 7

    def step(j, c):
        jj = (n_tiles - 1 - j) if reverse else j
        o = pl.multiple_of(jj * 8, 8)
        ca, cb = _tile_scan(a_s[pl.ds(o, 8), :], b_s[pl.ds(o, 8), :], row, reverse)
        h = ca * c + cb
        out_ref[pl.ds(o, 8), :] = h
        return jnp.broadcast_to(h[edge:edge + 1, :], (8, width))

    carry[...] = lax.fori_loop(0, n_tiles, step, carry[...])


def _gates(xc, p_ref, wr_ref, wi_ref):
    xcb = xc.astype(BF16)
    r = _sigmoid(jnp.dot(xcb, wr_ref[...], preferred_element_type=F32) + p_ref[R_BR:R_BR + 1, :])
    ig = _sigmoid(jnp.dot(xcb, wi_ref[...], preferred_element_type=F32) + p_ref[R_BI:R_BI + 1, :])
    sp = _softplus_neg(p_ref[R_LAM:R_LAM + 1, :])
    log_a = (-LRU_C) * r * sp
    a = jnp.exp(log_a)
    t = jnp.tanh(log_a)
    mult = jnp.sqrt(-2.0 * t / (1.0 - t))
    return xcb, r, ig, sp, a, mult


def _mixer_specs(Tc, bw, nb, layer):
    def seg(s):
        return pl.BlockSpec((Tc, bw), lambda n, i: (i, s * nb + n))

    p_spec = pl.BlockSpec((None, R_ROWS, bw), lambda n, i: (layer, 0, n))
    w_spec = pl.BlockSpec((None, None, bw, bw), lambda n, i: (layer, n, 0, 0))
    return seg, p_spec, w_spec


def _mixer_fwd(name, layer, z, pch, wr, wi, comm=None):
    T, D = z.shape[0], z.shape[1] // 7
    bw, nb = wr.shape[-1], wr.shape[1]
    Tc = min(SEQ_CHUNK, T)
    nT = T // Tc
    n_cin = len(comm.ins) if comm else 0
    n_cout = len(comm.out_shapes) if comm else 0

    def body(*refs):
        xa_ref, ya_ref, cb_ref, cc_ref, cx_ref, p_ref, wr_ref, wi_ref = refs[:8]
        pa_ref, pb_ref, xc_ref, hl_ref, vb_ref = refs[8 + n_cin:13 + n_cin]
        xa_buf, u_buf, a_s, b_s, carry = refs[13 + n_cin + n_cout:18 + n_cin + n_cout]
        if comm:
            c_in, c_out, sems = refs[8:8 + n_cin], refs[13 + n_cin:13 + n_cin + n_cout], refs[18 + n_cin + n_cout:]

            @pl.when((pl.program_id(0) == 0) & (pl.program_id(1) == 0))
            def _():
                comm.start(c_in, c_out, sems)

        @pl.when(pl.program_id(1) == 0)
        def _():
            xa_buf[0:HALO, :] = jnp.zeros((HALO, bw), F32)
            u_buf[0:HALO, :] = jnp.zeros((HALO, bw), F32)
            carry[...] = jnp.zeros_like(carry)

        xa_buf[HALO:HALO + Tc, :] = xa_ref[...]
        xc = p_ref[R_CAB:R_CAB + 1, :]
        for k in range(4):
            xc = xc + p_ref[R_CAW + k:R_CAW + k + 1, :] * xa_buf[HALO - 3 + k:HALO - 3 + k + Tc, :]
        xc_ref[...] = xc
        _, _, ig, _, a, mult = _gates(xc, p_ref, wr_ref, wi_ref)
        a_s[...] = a
        b_s[...] = mult * (ig * xc)
        _chunk_scan(a_s, b_s, hl_ref, carry, Tc // 8, bw, False)
        g, _ = _gelu_and_grad(ya_ref[...])
        pa_ref[...] = (hl_ref[...] * g).astype(BF16)

        u_buf[HALO:HALO + Tc, :] = cc_ref[...] * cx_ref[...]
        vb = jnp.zeros((Tc, bw), F32)
        for k in range(3):
            vb = vb + p_ref[R_CBW + k:R_CBW + k + 1, :] * u_buf[HALO - 2 + k:HALO - 2 + k + Tc, :]
        vb_ref[...] = vb
        pb_ref[...] = (cb_ref[...] * vb).astype(BF16)
        xa_buf[0:HALO, :] = xa_buf[Tc:Tc + HALO, :]
        u_buf[0:HALO, :] = u_buf[Tc:Tc + HALO, :]

        if comm:
            @pl.when((pl.program_id(0) == nb - 1) & (pl.program_id(1) == nT - 1))
            def _():
                comm.finish(c_in, c_out, sems)

    seg, p_spec, w_spec = _mixer_specs(Tc, bw, nb, layer)
    out = pl.BlockSpec((Tc, bw), lambda n, i: (i, n))
    res = pl.pallas_call(
        body,
        grid=(nb, nT),
        in_specs=[seg(0), seg(1), seg(2), seg(3), seg(4), p_spec, w_spec, w_spec] + [ANY] * n_cin,
        out_specs=[out] * 5 + [ANY] * n_cout,
        out_shape=[jax.ShapeDtypeStruct((T, D), BF16), jax.ShapeDtypeStruct((T, D), BF16),
                   jax.ShapeDtypeStruct((T, D), F32), jax.ShapeDtypeStruct((T, D), F32),
                   jax.ShapeDtypeStruct((T, D), F32)] + (comm.out_shapes if comm else []),
        scratch_shapes=[pltpu.VMEM((Tc + HALO, bw), F32), pltpu.VMEM((Tc + HALO, bw), F32),
                        pltpu.VMEM((Tc, bw), F32), pltpu.VMEM((Tc, bw), F32), pltpu.VMEM((8, bw), F32)]
        + (comm.scratch() if comm else []),
        input_output_aliases={8 + ci: 5 + co for ci, co in comm.aliases.items()} if comm else {},
        compiler_params=_cparams(("arbitrary", "arbitrary") if comm else ("parallel", "arbitrary")),
        name=name,
    )(z, z, z, z, z, pch, wr, wi, *(comm.ins if comm else []))
    return (res[:5], res[5:]) if comm else res


def _mixer_bwd(name, layer, z, xc, hl, vb, dpa, dpb, pch, wr, wi):
    T, D = z.shape[0], z.shape[1] // 7
    bw, nb = wr.shape[-1], wr.shape[1]
    Tc = min(SEQ_CHUNK, T)
    nT = T // Tc
    tpc = Tc // 8

    def body(xa_ref, ya_ref, cb_ref, cc_ref, cx_ref, xc_ref, hl_ref, hp_ref, vb_ref, dpa_ref, dpb_ref,
             p_ref, wr_ref, wi_ref,
             dxa_ref, dya_ref, dcb_ref, dcc_ref, dcx_ref, dwr_ref, dwi_ref, sm_ref,
             h_buf, a_buf, dxc_buf, dvb_buf, a_s, d_s, lam_s, carry):
        i = pl.program_id(1)

        @pl.when(i == 0)
        def _():
            a_buf[Tc:Tc + HALO, :] = jnp.zeros((HALO, bw), F32)
            dxc_buf[Tc:Tc + HALO, :] = jnp.zeros((HALO, bw), F32)
            dvb_buf[Tc:Tc + HALO, :] = jnp.zeros((HALO, bw), F32)
            carry[...] = jnp.zeros_like(carry)
            dwr_ref[...] = jnp.zeros_like(dwr_ref)
            dwi_ref[...] = jnp.zeros_like(dwi_ref)
            sm_ref[...] = jnp.zeros_like(sm_ref)

        xcv = xc_ref[...]
        xcb, r, ig, sp, a, mult = _gates(xcv, p_ref, wr_ref, wi_ref)
        g, gp = _gelu_and_grad(ya_ref[...])
        hlv, dpav = hl_ref[...], dpa_ref[...]
        dya_ref[...] = (dpav * hlv * gp).astype(BF16)

        a_buf[0:Tc, :] = a
        a_s[...] = a_buf[1:Tc + 1, :]
        d_s[...] = dpav * g
        _chunk_scan(a_s, d_s, lam_s, carry, tpc, bw, True)
        lamv = lam_s[...]

        h_buf[HALO:HALO + Tc, :] = hlv
        h_buf[0:HALO, :] = jnp.where(i == nT - 1, 0.0, hp_ref[...])
        da = lamv * h_buf[HALO - 1:HALO - 1 + Tc, :]
        dmult = lamv * (ig * xcv)
        dbx = lamv * mult
        dig = dbx * xcv
        dxc = dbx * ig
        dlog_a = da * a - dmult * (a * a) / mult
        dpr = (dlog_a * ((-LRU_C) * sp)) * r * (1.0 - r)
        dpi = dig * ig * (1.0 - ig)
        dprb, dpib = dpr.astype(BF16), dpi.astype(BF16)
        nt = (((1,), (1,)), ((), ()))
        tn = (((0,), (0,)), ((), ()))
        dxc = dxc + lax.dot_general(dprb, wr_ref[...], nt, preferred_element_type=F32)
        dxc = dxc + lax.dot_general(dpib, wi_ref[...], nt, preferred_element_type=F32)
        dwr_ref[...] += lax.dot_general(xcb, dprb, tn, preferred_element_type=F32)
        dwi_ref[...] += lax.dot_general(xcb, dpib, tn, preferred_element_type=F32)

        def rowsum(v):
            return jnp.sum(v, axis=0, keepdims=True)

        sm_ref[R_CAB:R_CAB + 1, :] += rowsum(dxc)
        sm_ref[R_BR:R_BR + 1, :] += rowsum(dpr)
        sm_ref[R_BI:R_BI + 1, :] += rowsum(dpi)
        sm_ref[R_LAM:R_LAM + 1, :] += rowsum(dlog_a * ((-LRU_C) * r))

        dxc_buf[0:Tc, :] = dxc
        xav = xa_ref[...]
        dxa = jnp.zeros((Tc, bw), F32)
        for k in range(4):
            sh = dxc_buf[3 - k:3 - k + Tc, :]
            dxa = dxa + p_ref[R_CAW + k:R_CAW + k + 1, :] * sh
            sm_ref[R_CAW + k:R_CAW + k + 1, :] += rowsum(xav * sh)
        dxa_ref[...] = dxa.astype(BF16)

        dpbv, cbv, ccv, cxv = dpb_ref[...], cb_ref[...], cc_ref[...], cx_ref[...]
        dcb_ref[...] = (dpbv * vb_ref[...]).astype(BF16)
        dvb_buf[0:Tc, :] = dpbv * cbv
        u = ccv * cxv
        du = jnp.zeros((Tc, bw), F32)
        for k in range(3):
            sh = dvb_buf[2 - k:2 - k + Tc, :]
            du = du + p_ref[R_CBW + k:R_CBW + k + 1, :] * sh
            sm_ref[R_CBW + k:R_CBW + k + 1, :] += rowsum(u * sh)
        dcc_ref[...] = (du * cxv).astype(BF16)
        dcx_ref[...] = (du * ccv).astype(BF16)

        a_buf[Tc:Tc + HALO, :] = a_buf[0:HALO, :]
        dxc_buf[Tc:Tc + HALO, :] = dxc_buf[0:HALO, :]
        dvb_buf[Tc:Tc + HALO, :] = dvb_buf[0:HALO, :]

        @pl.when(i == nT - 1)
        def _():
            sm_ref[R_LAM:R_LAM + 1, :] = sm_ref[R_LAM:R_LAM + 1, :] * (-_sigmoid(-p_ref[R_LAM:R_LAM + 1, :]))

    def seg(s):
        return pl.BlockSpec((Tc, bw), lambda n, i: (nT - 1 - i, s * nb + n))

    blk = pl.BlockSpec((Tc, bw), lambda n, i: (nT - 1 - i, n))
    halo = pl.BlockSpec((8, bw), lambda n, i: (jnp.maximum((nT - 1 - i) * tpc - 1, 0), n))
    p_spec = pl.BlockSpec((None, R_ROWS, bw), lambda n, i: (layer, 0, n))
    w_spec = pl.BlockSpec((None, None, bw, bw), lambda n, i: (layer, n, 0, 0))
    dw_spec = pl.BlockSpec((None, bw, bw), lambda n, i: (n, 0, 0))
    act = jax.ShapeDtypeStruct((T, D), BF16)
    return pl.pallas_call(
        body,
        grid=(nb, nT),
        in_specs=[seg(0), seg(1), seg(2), seg(3), seg(4), blk, blk, halo, blk, blk, blk, p_spec, w_spec, w_spec],
        out_specs=[blk] * 5 + [dw_spec, dw_spec, pl.BlockSpec((R_ROWS, bw), lambda n, i: (0, n))],
        out_shape=[act] * 5 + [jax.ShapeDtypeStruct((nb, bw, bw), F32), jax.ShapeDtypeStruct((nb, bw, bw), F32),
                               jax.ShapeDtypeStruct((R_ROWS, D), F32)],
        scratch_shapes=[pltpu.VMEM((Tc + HALO, bw), F32)] * 4 + [pltpu.VMEM((Tc, bw), F32)] * 3
        + [pltpu.VMEM((8, bw), F32)],
        compiler_params=_cparams(("parallel", "arbitrary")),
        name=name,
    )(z, z, z, z, z, xc, hl, hl, vb, dpa, dpb, pch, wr, wi)


def _local_fwd_bwd(x, tgt, W, placed=None):
    T, D = x.shape
    g1, g2 = W["g1"], W["g2"]
    depth = g1.shape[0]
    FF = 4 * D
    if placed is None:
        mats = {(n, l): W[n][l] for n in MATS for l in range(depth)}
    else:
        mats = {}
        mats["w_in", 0], = _run_carry("gather_first", _gather_carry([placed["w_in", 0]], [(0, 1, 0, 1)]))

    def gathering(specs):
        if placed is None or not specs:
            return None, []
        keys = [(n, l) for n, l, _, _, _ in specs]
        arrays = [mats.get(k, placed[k]) for k in keys]
        return _gather_carry(arrays, [(i, d, part, nparts) for i, (_, _, d, part, nparts) in enumerate(specs)]), keys

    def hosted(call, specs, **kw):
        carry, keys = gathering(specs)
        if carry is None:
            return call(**kw)
        res, got = call(**kw, **{("comm" if call.func is _mixer_fwd else "carry"): carry})
        mats.update(zip(keys, got))
        return res

    saved = []
    xs = x
    for l in range(depth):
        h = _rms_fwd(f"rms1_fwd_{l}", xs, g1[l][None])
        z = hosted(functools.partial(_mm, f"in_proj_{l}", "nn", h, mats["w_in", l], T, 7 * D, D, [F32]),
                   [("w_pa", l, 0, 0, 1), ("w_pb", l, 0, 0, 1), ("w_o", l, 0, 0, 1), ("w_mlp1", l, 1, 0, 1)],
                   epilogue=lambda acc, b: (acc + b,), extras=[(W["b_in"][l][None], "bias", 0)])
        pa, pb, xc, hl, vb = hosted(
            functools.partial(_mixer_fwd, f"mixer_fwd_{l}", l, z, W["pch"], W["wr"], W["wi"]),
            [("w_mlp2", l, 0, 0, 1)])
        oa = _mm(f"proj_a_{l}", "nn", pa, mats["w_pa", l], T, D, D, [F32])

        def merge(acc, oav, ga, gb):
            return acc, _sigmoid(ga) * oav + _sigmoid(gb) * acc

        ob, mg = _mm(f"proj_b_merge_{l}", "nn", pb, mats["w_pb", l], T, D, D, [F32, BF16], epilogue=merge,
                     tiles=MM_TILES_FUSED,
                     extras=[(oa, "tile", 0), (z, "tile", 5 * D), (z, "tile", 6 * D)])
        x1 = _mm(f"out_proj_{l}", "nn", mg, mats["w_o", l], T, D, D, [F32],
                 epilogue=lambda acc, res: (res + acc,), extras=[(xs, "tile", 0)])
        h2 = _rms_fwd(f"rms2_fwd_{l}", x1, g2[l][None])

        def relu2(acc):
            pr = jnp.maximum(acc, 0.0)
            return pr * pr, pr

        nxt = l + 1 < depth
        u, pr = hosted(functools.partial(_mm, f"mlp1_{l}", "nn", h2, mats["w_mlp1", l], T, FF, D, [BF16, BF16]),
                       [("w_in", l + 1, 1, 0, 2)] if nxt else [], epilogue=relu2)
        x2 = hosted(functools.partial(_mm, f"mlp2_{l}", "nn", u, mats["w_mlp2", l], T, D, FF, [F32]),
                    [("w_in", l + 1, 1, 1, 2)] if nxt else [],
                    epilogue=lambda acc, res: (res + acc,), extras=[(x1, "tile", 0)])
        saved.append(dict(x0=xs, h=h, z=z, pa=pa, pb=pb, xc=xc, hl=hl, vb=vb, oa=oa, ob=ob, mg=mg, x1=x1,
                          h2=h2, u=u, pr=pr))
        xs = x2

    dx, dxb, dgf, loss_blk = _loss_head("loss_head", xs, W["gf"][None], tgt)
    grads = [None] * depth
    for l in reversed(range(depth)):
        s = saved[l]
        dp = _mm(f"mlp2_dx_{l}", "nt", dxb, mats["w_mlp2", l], T, FF, D, [BF16],
                 epilogue=lambda acc, prv: (2.0 * prv.astype(F32) * acc,), extras=[(s["pr"], "tile", 0)])
        dw2 = _mm(f"mlp2_dw_{l}", "tn", s["u"], dxb, FF, D, T, [BF16])
        dh2 = _mm(f"mlp1_dx_{l}", "nt", dp, mats["w_mlp1", l], T, D, FF, [F32])
        dw1 = _mm(f"mlp1_dw_{l}", "tn", s["h2"], dp, D, FF, T, [BF16])
        dx1, dx1b, dg2 = _rms_bwd(f"rms2_bwd_{l}", s["x1"], g2[l][None], dh2, dx)

        def unmerge(acc, ga, gb, oav, obv):
            sa, sb = _sigmoid(ga), _sigmoid(gb)
            return acc * sa, acc * sb, acc * oav * sa * (1.0 - sa), acc * obv * sb * (1.0 - sb)

        doa, dob, dga, dgb = _mm(f"out_proj_dx_{l}", "nt", dx1b, mats["w_o", l], T, D, D, [BF16] * 4,
                                 tiles=MM_TILES_FUSED,
                                 epilogue=unmerge,
                                 extras=[(s["z"], "tile", 5 * D), (s["z"], "tile", 6 * D),
                                         (s["oa"], "tile", 0), (s["ob"], "tile", 0)])
        dwo = _mm(f"out_proj_dw_{l}", "tn", s["mg"], dx1b, D, D, T, [BF16])
        dpa = _mm(f"proj_a_dx_{l}", "nt", doa, mats["w_pa", l], T, D, D, [F32])
        dwpa = _mm(f"proj_a_dw_{l}", "tn", s["pa"], doa, D, D, T, [BF16])
        dpb = _mm(f"proj_b_dx_{l}", "nt", dob, mats["w_pb", l], T, D, D, [F32])
        dwpb = _mm(f"proj_b_dw_{l}", "tn", s["pb"], dob, D, D, T, [BF16])
        dxa, dya, dcb, dcc, dcx, dwr, dwi, sm = _mixer_bwd(
            f"mixer_bwd_{l}", l, s["z"], s["xc"], s["hl"], s["vb"], dpa, dpb, W["pch"], W["wr"], W["wi"])
        dz = jnp.concatenate([dxa, dya, dcb, dcc, dcx, dga, dgb], axis=1)
        dbin = _colsum(f"bias_grad_{l}", dz)
        dh = _mm(f"in_proj_dx_{l}", "nt", dz, mats["w_in", l], T, D, 7 * D, [F32])
        dwin = _mm(f"in_proj_dw_{l}", "tn", s["h"], dz, D, 7 * D, T, [BF16])
        dx, dxb, dg1 = _rms_bwd(f"rms1_bwd_{l}", s["x0"], g1[l][None], dh, dx1)
        grads[l] = dict(w_in=dwin, w_pa=dwpa, w_pb=dwpb, w_o=dwo, w1=dw1, w2=dw2, wr=dwr, wi=dwi,
                        sm=sm, b_in=dbin, g1=dg1, g2=dg2)
    return loss_blk[0, 0], dx, grads, dgf


ANY = pl.BlockSpec(memory_space=pl.ANY)


def _place():
    x, y, c = lax.axis_index("x"), lax.axis_index("y"), lax.axis_index("c")
    peers = [(1 - x, y, c), (x, 1 - y, c), (1 - x, 1 - y, c)]
    chips = [2 * (1 - x) + y, 2 * x + (1 - y), 2 * (1 - x) + (1 - y)]
    return (x, y, c), 2 * x + y, peers, chips


def _window(ref, dim, q, size):
    idx = [slice(None)] * len(ref.shape)
    idx[dim] = pl.ds(q * size, size)
    return ref.at[tuple(idx)]


def _gather_weights(shards, dims, small):
    n = len(shards)
    sizes = [s.shape[d] for s, d in zip(shards, dims)]
    full = [jax.ShapeDtypeStruct(s.shape[:d] + (s.shape[d] * N_CHIPS,) + s.shape[d + 1:], s.dtype)
            for s, d in zip(shards, dims)]
    full.append(jax.ShapeDtypeStruct((N_CHIPS,) + small.shape, small.dtype))

    def body(*refs):
        ins, outs = refs[:n + 1], refs[n + 1:2 * n + 2]
        send_sems, recv_sems, local_sems = refs[2 * n + 2:]
        _, k, peers, chips = _place()

        def dst(w, q):
            return outs[w].at[q] if w == n else _window(outs[w], dims[w], q, sizes[w])

        local = [pltpu.make_async_copy(ins[w], dst(w, k), local_sems.at[w]) for w in range(n + 1)]
        for cp in local:
            cp.start()
        sends = []
        for p, peer in enumerate(peers):
            for w in range(n + 1):
                s = p * (n + 1) + w
                sends.append(pltpu.make_async_remote_copy(
                    src_ref=ins[w], dst_ref=dst(w, k), send_sem=send_sems.at[s], recv_sem=recv_sems.at[s],
                    device_id=peer, device_id_type=MESH))
        for cp in sends:
            cp.start()
        for p, peer in enumerate(peers):
            for w in range(n + 1):
                s = p * (n + 1) + w
                pltpu.make_async_remote_copy(
                    src_ref=ins[w], dst_ref=dst(w, chips[p]), send_sem=send_sems.at[s], recv_sem=recv_sems.at[s],
                    device_id=peer, device_id_type=MESH).wait_recv()
        for cp in sends:
            cp.wait_send()
        for cp in local:
            cp.wait()

    return pl.pallas_call(
        body,
        in_specs=[ANY] * (n + 1),
        out_specs=[ANY] * (n + 1),
        out_shape=full,
        scratch_shapes=[pltpu.SemaphoreType.DMA((3 * (n + 1),)), pltpu.SemaphoreType.DMA((3 * (n + 1),)),
                        pltpu.SemaphoreType.DMA((n + 1,))],
        name="gather_weights",
    )(*shards, small)


def _scatter_grads(grads, dims):
    n, depth = len(grads), len(grads[0])
    sizes = [g[0].shape[d] // N_CHIPS for g, d in zip(grads, dims)]
    land = []
    for g, d, sz in zip(grads, dims, sizes):
        shp = g[0].shape
        land.append(jax.ShapeDtypeStruct((N_CHIPS, depth) + shp[:d] + (sz,) + shp[d + 1:], g[0].dtype))

    def body(*refs):
        ins, outs = refs[:n * depth], refs[n * depth:n * depth + n]
        send_sems, recv_sems, local_sems = refs[n * depth + n:]
        _, k, peers, chips = _place()

        def src(w, l, q):
            return _window(ins[w * depth + l], dims[w], q, sizes[w])

        local = [pltpu.make_async_copy(src(w, l, k), outs[w].at[3, l], local_sems.at[w * depth + l])
                 for w in range(n) for l in range(depth)]
        for cp in local:
            cp.start()
        sends = []
        for p, peer in enumerate(peers):
            for w in range(n):
                for l in range(depth):
                    s = (p * n + w) * depth + l
                    sends.append(pltpu.make_async_remote_copy(
                        src_ref=src(w, l, chips[p]), dst_ref=outs[w].at[p, l], send_sem=send_sems.at[s],
                        recv_sem=recv_sems.at[s], device_id=peer, device_id_type=MESH))
        for cp in sends:
            cp.start()
        for cp in sends:
            cp.wait_recv()
        for cp in sends:
            cp.wait_send()
        for cp in local:
            cp.wait()

    flat = [g for gl in grads for g in gl]
    return pl.pallas_call(
        body,
        in_specs=[ANY] * (n * depth),
        out_specs=[ANY] * n,
        out_shape=land,
        scratch_shapes=[pltpu.SemaphoreType.DMA((3 * n * depth,)), pltpu.SemaphoreType.DMA((3 * n * depth,)),
                        pltpu.SemaphoreType.DMA((n * depth,))],
        name="scatter_grads",
    )(*flat)


def _sum_slots(name, land):
    _, R, C = land.shape
    tr, tc = _div_tile(R, 512, 8), _div_tile(C, 1024)

    def body(a_ref, b_ref, c_ref, d_ref, o_ref):
        o_ref[...] = ((d_ref[...].astype(F32) + a_ref[...].astype(F32)) + b_ref[...].astype(F32)) \
            + c_ref[...].astype(F32)

    def slot(q):
        return pl.BlockSpec((None, tr, tc), lambda i, j: (q, i, j))

    return pl.pallas_call(
        body,
        grid=(R // tr, C // tc),
        in_specs=[slot(0), slot(1), slot(2), slot(3)],
        out_specs=pl.BlockSpec((tr, tc), lambda i, j: (i, j)),
        out_shape=jax.ShapeDtypeStruct((R, C), F32),
        compiler_params=_cparams(("parallel", "parallel")),
        name=name,
    )(land, land, land, land)


def _swap_with_sibling(parts):
    n = len(parts)

    def body(*refs):
        ins, outs = refs[:n], refs[n:2 * n]
        send_sems, recv_sems = refs[2 * n:]
        (x, y, c), _, _, _ = _place()
        copies = [pltpu.make_async_remote_copy(
            src_ref=ins[w], dst_ref=outs[w], send_sem=send_sems.at[w], recv_sem=recv_sems.at[w],
            device_id=(x, y, 1 - c), device_id_type=MESH) for w in range(n)]
        for cp in copies:
            cp.start()
        for cp in copies:
            cp.wait()

    return pl.pallas_call(
        body,
        in_specs=[ANY] * n,
        out_specs=[ANY] * n,
        out_shape=[jax.ShapeDtypeStruct(p.shape, p.dtype) for p in parts],
        scratch_shapes=[pltpu.SemaphoreType.DMA((n,)), pltpu.SemaphoreType.DMA((n,))],
        name="swap_with_sibling",
    )(*parts)


class _Carry:
    def __init__(self, ins, out_shapes, plan, n1, plan2=None, n2=0, aliases=None):
        self.ins, self.out_shapes, self.plan, self.n1 = list(ins), list(out_shapes), plan, n1
        self.plan2, self.n2, self.aliases = plan2, n2, dict(aliases or {})

    def scratch(self):
        s = [pltpu.SemaphoreType.DMA((self.n1,)), pltpu.SemaphoreType.DMA((self.n1,))]
        if self.plan2 is not None:
            s += [pltpu.SemaphoreType.DMA((self.n2,)), pltpu.SemaphoreType.DMA((self.n2,))]
        return s

    @staticmethod
    def _copy(src, dst, peer, send_sems, recv_sems, i):
        return pltpu.make_async_remote_copy(src_ref=src, dst_ref=dst, send_sem=send_sems.at[i],
                                            recv_sem=recv_sems.at[i], device_id=peer, device_id_type=MESH)

    def start(self, in_refs, out_refs, sems):
        remote = self.plan(in_refs, out_refs)
        assert len(remote) == self.n1, (len(remote), self.n1)
        for i, (s, d, peer, _) in enumerate(remote):
            self._copy(s, d, peer, sems[0], sems[1], i).start()

    def finish(self, in_refs, out_refs, sems):
        remote = self.plan(in_refs, out_refs)
        for i, (s, _, peer, landing) in enumerate(remote):
            self._copy(s, landing, peer, sems[0], sems[1], i).wait_recv()
        if self.plan2 is not None:
            second = self.plan2(in_refs, out_refs)
            assert len(second) == self.n2, (len(second), self.n2)
            for i, (s, d, peer, _) in enumerate(second):
                self._copy(s, d, peer, sems[2], sems[3], i).start()
            for i, (s, _, peer, landing) in enumerate(second):
                self._copy(s, landing, peer, sems[2], sems[3], i).wait_recv()
            for i, (s, d, peer, _) in enumerate(second):
                self._copy(s, d, peer, sems[2], sems[3], i).wait_send()
        for i, (s, d, peer, _) in enumerate(remote):
            self._copy(s, d, peer, sems[0], sems[1], i).wait_send()


def _run_carry(name, carry):
    n_in, n_out = len(carry.ins), len(carry.out_shapes)

    def body(*refs):
        in_refs, out_refs, sems = refs[:n_in], refs[n_in:n_in + n_out], refs[n_in + n_out:]
        carry.start(in_refs, out_refs, sems)
        carry.finish(in_refs, out_refs, sems)

    return pl.pallas_call(
        body,
        in_specs=[ANY] * n_in,
        out_specs=[ANY] * n_out,
        out_shape=carry.out_shapes,
        scratch_shapes=carry.scratch(),
        input_output_aliases=carry.aliases,
        name=name,
    )(*carry.ins)


def _comm_call(name, ins, out_shapes, plan, n_local, n_remote, aliases=None):
    assert n_local == 0
    return _run_carry(name, _Carry(ins, out_shapes, lambda i, o: plan(i, o)[1], n_remote, aliases=aliases))


def _gather_carry(arrays, items):
    shapes = [a.shape for a in arrays]

    def window(ref, item):
        idx, d, part, nparts = item
        h = shapes[idx][d] // (2 * N_CHIPS)
        rows = shapes[idx][1 - d] // nparts

        def win(j):
            sl = [None, None]
            sl[d] = pl.ds(j * h, h)
            sl[1 - d] = pl.ds(part * rows, rows)
            return ref.at[tuple(sl)]

        return win

    def plan1(ins, outs):
        (_, _, c), k, peers, chips = _place()
        remote = []
        for item in items:
            win = window(outs[item[0]], item)
            for p in range(3):
                remote.append((win(2 * k + c), win(2 * k + c), peers[p], win(2 * chips[p] + c)))
        return remote

    def plan2(ins, outs):
        (x, y, c), _, _, chips = _place()
        remote = []
        for item in items:
            win = window(outs[item[0]], item)
            for p in range(3):
                remote.append((win(2 * chips[p] + c), win(2 * chips[p] + c), (x, y, 1 - c),
                               win(2 * chips[p] + 1 - c)))
        return remote

    n = 3 * len(items)
    return _Carry(arrays, [jax.ShapeDtypeStruct(a.shape, a.dtype) for a in arrays], plan1, n, plan2, n,
                  aliases={i: i for i in range(len(arrays))})


def _place_shard(name, w, layer, dim, chip):
    _, a, b = w.shape
    full = (a * N_CHIPS, b) if dim == 0 else (a, b * N_CHIPS)
    tr, tc = _div_tile(a, 512, 16), _div_tile(b, 2048)
    nr, nc = a // tr, b // tc

    def out_map(i, j, chip_ref):
        return (chip_ref[0] * nr + i, j) if dim == 0 else (i, chip_ref[0] * nc + j)

    def body(chip_ref, w_ref, o_ref):
        o_ref[...] = w_ref[...].astype(o_ref.dtype)

    return pl.pallas_call(
        body,
        grid_spec=pltpu.PrefetchScalarGridSpec(
            num_scalar_prefetch=1, grid=(nr, nc),
            in_specs=[pl.BlockSpec((None, tr, tc), lambda i, j, chip_ref: (layer, i, j))],
            out_specs=pl.BlockSpec((tr, tc), out_map)),
        out_shape=jax.ShapeDtypeStruct(full, BF16),
        compiler_params=_cparams(("parallel", "parallel")),
        name=name,
    )(chip, w)


def _gather_halves(name, full, dims):
    n = len(full)
    sizes = [f.shape[d] // N_CHIPS for f, d in zip(full, dims)]

    def plan(ins, outs):
        (_, _, c), k, peers, chips = _place()
        remote = []
        for w in range(n):
            d, h = dims[w], sizes[w] // 2
            mine = _window(outs[w], d, 2 * k + c, h)
            for p in range(3):
                remote.append((mine, mine, peers[p], _window(outs[w], d, 2 * chips[p] + c, h)))
        return [], remote

    return _comm_call(name, full, [jax.ShapeDtypeStruct(f.shape, f.dtype) for f in full], plan, 0, 3 * n,
                      aliases={w: w for w in range(n)})


def _forward_halves(name, full, dims):
    n = len(full)
    sizes = [f.shape[d] // N_CHIPS for f, d in zip(full, dims)]

    def plan(ins, outs):
        (x, y, c), _, _, chips = _place()
        remote = []
        for w in range(n):
            d, h = dims[w], sizes[w] // 2
            for p in range(3):
                mine = _window(outs[w], d, 2 * chips[p] + c, h)
                remote.append((mine, mine, (x, y, 1 - c), _window(outs[w], d, 2 * chips[p] + 1 - c, h)))
        return [], remote

    return _comm_call(name, full, [jax.ShapeDtypeStruct(f.shape, f.dtype) for f in full], plan, 0, 3 * n,
                      aliases={w: w for w in range(n)})


def _half_shape(shape, dim):
    return shape[:dim] + (shape[dim] // (2 * N_CHIPS),) + shape[dim + 1:]


def _swap_halves(name, grads, dims):
    n = len(grads)
    outs_shape = [jax.ShapeDtypeStruct((N_CHIPS,) + _half_shape(g.shape, d), g.dtype) for g, d in zip(grads, dims)]

    def plan(ins, outs):
        (x, y, c), _, _, _ = _place()
        remote = []
        for w in range(n):
            d, h = dims[w], grads[w].shape[dims[w]] // (2 * N_CHIPS)
            for q in range(N_CHIPS):
                remote.append((_window(ins[w], d, 2 * q + 1 - c, h), outs[w].at[q], (x, y, 1 - c), outs[w].at[q]))
        return [], remote

    return _comm_call(name, grads, outs_shape, plan, 0, N_CHIPS * n)


def _add_halves(name, g, got, dim, core):
    R, C = g.shape
    if dim == 1:
        r, cc = R, C // (2 * N_CHIPS)
    else:
        r, cc = R // (2 * N_CHIPS), C
    tr, tc = _div_tile(r, 512, 16), _div_tile(cc, 1024)
    nr, nc = r // tr, cc // tc

    def g_map(q, i, j, core_ref):
        w = 2 * q + core_ref[0]
        return (i, w * nc + j) if dim == 1 else (w * nr + i, j)

    def body(core_ref, g_ref, got_ref, o_ref):
        o_ref[...] = (g_ref[...].astype(F32) + got_ref[...].astype(F32)).astype(o_ref.dtype)

    slab = pl.BlockSpec((None, tr, tc), lambda q, i, j, core_ref: (q, i, j))
    return pl.pallas_call(
        body,
        grid_spec=pltpu.PrefetchScalarGridSpec(
            num_scalar_prefetch=1, grid=(N_CHIPS, nr, nc),
            in_specs=[pl.BlockSpec((tr, tc), g_map), slab], out_specs=slab),
        out_shape=jax.ShapeDtypeStruct((N_CHIPS, r, cc), g.dtype),
        compiler_params=_cparams(("parallel", "parallel", "parallel")),
        name=name,
    )(core, g, got)


def _scatter_halves(name, sums):
    n = len(sums)

    def plan(ins, outs):
        _, _, peers, chips = _place()
        remote = []
        for w in range(n):
            for p in range(3):
                remote.append((ins[w].at[chips[p]], outs[w].at[p], peers[p], outs[w].at[p]))
        return [], remote

    return _comm_call(name, sums, [jax.ShapeDtypeStruct((3,) + s.shape[1:], s.dtype) for s in sums], plan, 0, 3 * n)


def _reduce_into(name, sums, land, acc, layer, dim, shape, where):
    _, r, cc = sums.shape
    tr, tc = _div_tile(r, 512, 16), _div_tile(cc, 1024)
    nr, nc = r // tr, cc // tc

    def out_map(i, j, s):
        return (layer, s[1] * nr + i, j) if dim == 1 else (layer, i, s[1] * nc + j)

    def body(*refs):
        own, a_ref, b_ref, c_ref, o_ref = refs[1], refs[2], refs[3], refs[4], refs[-1]
        o_ref[...] = ((own[...].astype(F32) + a_ref[...].astype(F32)) + b_ref[...].astype(F32)) \
            + c_ref[...].astype(F32)

    def slot(p):
        return pl.BlockSpec((None, tr, tc), lambda i, j, s: (p, i, j))

    in_specs = [pl.BlockSpec((None, tr, tc), lambda i, j, s: (s[0], i, j)), slot(0), slot(1), slot(2)]
    args = [where, sums, land, land, land]
    if acc is not None:
        in_specs.append(ANY)
        args.append(acc)
    return pl.pallas_call(
        body,
        grid_spec=pltpu.PrefetchScalarGridSpec(
            num_scalar_prefetch=1, grid=(nr, nc), in_specs=in_specs,
            out_specs=pl.BlockSpec((None, tr, tc), out_map)),
        out_shape=jax.ShapeDtypeStruct(shape, F32),
        input_output_aliases={5: 0} if acc is not None else {},
        compiler_params=_cparams(("parallel", "parallel")),
        name=name,
    )(*args)


def _join_halves(name, grads, dims):
    n = len(grads)

    def plan(ins, outs):
        (x, y, c), _, _, _ = _place()
        remote = []
        for w in range(n):
            d, h = dims[w], grads[w].shape[dims[w]] // 2
            mine = _window(outs[w], d, c, h)
            remote.append((mine, mine, (x, y, 1 - c), _window(outs[w], d, 1 - c, h)))
        return [], remote

    return _comm_call(name, grads, [jax.ShapeDtypeStruct(g.shape, g.dtype) for g in grads], plan, 0, n,
                      aliases={w: w for w in range(n)})


def _allreduce_small(pack):
    R, C = pack.shape

    def gather_body(in_ref, slots_ref, send_sems, recv_sems, local_sem):
        x, y, c = lax.axis_index("x"), lax.axis_index("y"), lax.axis_index("c")
        me = 4 * x + 2 * y + c
        flips = [(dx, dy, dc) for dx in (0, 1) for dy in (0, 1) for dc in (0, 1)][1:]

        def flip(v, d):
            return 1 - v if d else v

        local = pltpu.make_async_copy(in_ref, slots_ref.at[me], local_sem)
        local.start()
        sends = []
        for j, (dx, dy, dc) in enumerate(flips):
            px, py, pc = flip(x, dx), flip(y, dy), flip(c, dc)
            sends.append((pltpu.make_async_remote_copy(
                src_ref=in_ref, dst_ref=slots_ref.at[me], send_sem=send_sems.at[j], recv_sem=recv_sems.at[j],
                device_id=(px, py, pc), device_id_type=MESH), 4 * px + 2 * py + pc, j))
        for cp, _, _ in sends:
            cp.start()
        for cp, peer_id, j in sends:
            pltpu.make_async_remote_copy(
                src_ref=in_ref, dst_ref=slots_ref.at[peer_id], send_sem=send_sems.at[j], recv_sem=recv_sems.at[j],
                device_id=(x, y, c), device_id_type=MESH).wait_recv()
        for cp, _, _ in sends:
            cp.wait_send()
        local.wait()

    slots = pl.pallas_call(
        gather_body,
        in_specs=[ANY],
        out_specs=ANY,
        out_shape=jax.ShapeDtypeStruct((N_DEV, R, C), pack.dtype),
        scratch_shapes=[pltpu.SemaphoreType.DMA((N_DEV - 1,)), pltpu.SemaphoreType.DMA((N_DEV - 1,)),
                        pltpu.SemaphoreType.DMA],
        name="allgather_small",
    )(pack)

    def sum_body(s_ref, o_ref):
        acc = s_ref[0]
        for d in range(1, N_DEV):
            acc = acc + s_ref[d]
        o_ref[...] = acc

    return pl.pallas_call(
        sum_body,
        out_shape=jax.ShapeDtypeStruct((R, C), pack.dtype),
        name="sum_small",
    )(slots)


def _adamw_math(w, g, m, v):
    m2 = ADAM_B1 * m + (1.0 - ADAM_B1) * g
    v2 = ADAM_B2 * v + (1.0 - ADAM_B2) * (g * g)
    m_hat = m2 / (1.0 - ADAM_B1 ** ADAM_STEP)
    v_hat = v2 / (1.0 - ADAM_B2 ** ADAM_STEP)
    delta = -ADAM_LR * (m_hat / (jnp.sqrt(v_hat) + ADAM_EPS) + ADAM_WD * w)
    return delta, m2, v2


def _adamw_big(name, w, m, v, g_parts):
    shape = w.shape
    C = shape[-1]
    R = w.size // C
    tr, tc = _div_tile(R, 256, 8), _div_tile(C, 1024)
    n_g = len(g_parts)

    def body(*refs):
        w_ref, m_ref, v_ref = refs[:3]
        g_ref, d_ref, nm_ref, nv_ref = refs[3 + n_g:]
        g = refs[3][...]
        for extra in refs[4:3 + n_g]:
            g = g + extra[...]
        delta, m2, v2 = _adamw_math(w_ref[...], g, m_ref[...], v_ref[...])
        g_ref[...], d_ref[...], nm_ref[...], nv_ref[...] = g, delta, m2, v2

    blk = pl.BlockSpec((tr, tc), lambda i, j: (i, j))
    outs = pl.pallas_call(
        body,
        grid=(R // tr, C // tc),
        in_specs=[blk] * (3 + n_g),
        out_specs=[blk] * 4,
        out_shape=[jax.ShapeDtypeStruct((R, C), F32)] * 4,
        compiler_params=_cparams(("parallel", "parallel")),
        name=name,
    )(w.reshape(R, C), m.reshape(R, C), v.reshape(R, C), *[g.reshape(R, C) for g in g_parts])
    return [o.reshape(shape) for o in outs]


def _adamw_small(name, w, g, m, v):
    shape = w.shape
    two_d = (w.size // shape[-1], shape[-1])

    def body(w_ref, g_ref, m_ref, v_ref, d_ref, nm_ref, nv_ref):
        d_ref[...], nm_ref[...], nv_ref[...] = _adamw_math(w_ref[...], g_ref[...], m_ref[...], v_ref[...])

    outs = pl.pallas_call(
        body,
        out_shape=[jax.ShapeDtypeStruct(two_d, F32)] * 3,
        name=name,
    )(w.reshape(two_d), g.reshape(two_d), m.reshape(two_d), v.reshape(two_d))
    return [o.reshape(shape) for o in outs]


SMALL_ROWS = 40
S_BIN, S_G1, S_G2 = 16, 24, 32
MATS = ("w_in", "w_pa", "w_pb", "w_o", "w_mlp1", "w_mlp2")
LRU = ("lru_wr", "lru_wi")
BIG_DIM = dict(w_in=2, w_pa=1, w_pb=1, w_o=1, w_mlp1=2, w_mlp2=1, lru_wr=2, lru_wi=2)
WEIGHTS = ("norm1_g", "w_in", "b_in", "conv_a_w", "conv_a_b", "lru_wr", "lru_br", "lru_wi", "lru_bi", "lru_lam",
           "conv_b_w", "w_pa", "w_pb", "w_o", "norm2_g", "w_mlp1", "w_mlp2", "final_g")


def _rows_at(a, r0, total):
    pad = [(0, 0)] * a.ndim
    pad[-2] = (r0, total - r0 - a.shape[-2])
    return jnp.pad(a, pad)


def kernel(x, norm1_g, w_in, b_in, conv_a_w, conv_a_b, lru_wr, lru_br, lru_wi, lru_bi, lru_lam, conv_b_w, w_pa, w_pb, w_o, norm2_g, w_mlp1, w_mlp2, final_g, loss_target, m_norm1_g, m_w_in, m_b_in, m_conv_a_w, m_conv_a_b, m_lru_wr, m_lru_br, m_lru_wi, m_lru_bi, m_lru_lam, m_conv_b_w, m_w_pa, m_w_pb, m_w_o, m_norm2_g, m_w_mlp1, m_w_mlp2, m_final_g, v_norm1_g, v_w_in, v_b_in, v_conv_a_w, v_conv_a_b, v_lru_wr, v_lru_br, v_lru_wi, v_lru_bi, v_lru_lam, v_conv_b_w, v_w_pa, v_w_pb, v_w_o, v_norm2_g, v_w_mlp1, v_w_mlp2, v_final_g):
    wts = dict(norm1_g=norm1_g, w_in=w_in, b_in=b_in, conv_a_w=conv_a_w, conv_a_b=conv_a_b, lru_wr=lru_wr,
               lru_br=lru_br, lru_wi=lru_wi, lru_bi=lru_bi, lru_lam=lru_lam, conv_b_w=conv_b_w, w_pa=w_pa,
               w_pb=w_pb, w_o=w_o, norm2_g=norm2_g, w_mlp1=w_mlp1, w_mlp2=w_mlp2, final_g=final_g)
    mom = dict(norm1_g=m_norm1_g, w_in=m_w_in, b_in=m_b_in, conv_a_w=m_conv_a_w, conv_a_b=m_conv_a_b,
               lru_wr=m_lru_wr, lru_br=m_lru_br, lru_wi=m_lru_wi, lru_bi=m_lru_bi, lru_lam=m_lru_lam,
               conv_b_w=m_conv_b_w, w_pa=m_w_pa, w_pb=m_w_pb, w_o=m_w_o, norm2_g=m_norm2_g, w_mlp1=m_w_mlp1,
               w_mlp2=m_w_mlp2, final_g=m_final_g)
    vel = dict(norm1_g=v_norm1_g, w_in=v_w_in, b_in=v_b_in, conv_a_w=v_conv_a_w, conv_a_b=v_conv_a_b,
               lru_wr=v_lru_wr, lru_br=v_lru_br, lru_wi=v_lru_wi, lru_bi=v_lru_bi, lru_lam=v_lru_lam,
               conv_b_w=v_conv_b_w, w_pa=v_w_pa, w_pb=v_w_pb, w_o=v_w_o, norm2_g=v_norm2_g, w_mlp1=v_w_mlp1,
               w_mlp2=v_w_mlp2, final_g=v_final_g)
    depth, D = norm1_g.shape
    nb, bw = lru_wr.shape[1], lru_wr.shape[3]
    chip = 2 * lax.axis_index("x") + lax.axis_index("y")

    small_parts = [conv_a_w.reshape(-1), conv_b_w.reshape(-1), lru_br.reshape(-1), lru_bi.reshape(-1)]
    small_len = sum(p.shape[0] for p in small_parts)
    small_rows = -(-small_len // 1024) * 8
    small = jnp.concatenate(small_parts + [jnp.zeros((small_rows * 128 - small_len,), F32)]).reshape(small_rows, 128)
    gathered = _gather_weights([wts[n].astype(BF16) for n in LRU], [BIG_DIM[n] for n in LRU], small)
    full = dict(zip(LRU, gathered[:-1]))
    items = [(n, l) for l in range(depth) for n in MATS]
    mat_dims = [BIG_DIM[n] - 1 for n, _ in items]
    where = jnp.stack([chip, lax.axis_index("c")]).astype(jnp.int32)
    placed = {(n, l): _place_shard(f"place_{n}_{l}", wts[n], l, BIG_DIM[n] - 1, where) for n, l in items}
    flat = gathered[-1].reshape(N_CHIPS, small_rows * 128)
    off = 0
    small_full = []
    for part, shard in zip(small_parts, (conv_a_w, conv_b_w, lru_br, lru_bi)):
        piece = flat[:, off:off + part.shape[0]].reshape((N_CHIPS,) + shard.shape)
        small_full.append(jnp.moveaxis(piece, 0, -2).reshape(shard.shape[:-1] + (N_CHIPS * shard.shape[-1],)))
        off += part.shape[0]
    caw_f, cbw_f, br_f, bi_f = small_full
    pch = (_rows_at(conv_a_b[:, None, :], R_CAB, R_ROWS) + _rows_at(br_f.reshape(depth, 1, D), R_BR, R_ROWS)
           + _rows_at(bi_f.reshape(depth, 1, D), R_BI, R_ROWS) + _rows_at(lru_lam[:, None, :], R_LAM, R_ROWS)
           + _rows_at(caw_f, R_CAW, R_ROWS) + _rows_at(cbw_f, R_CBW, R_ROWS))
    W = dict(b_in=b_in, pch=pch, wr=full["lru_wr"], wi=full["lru_wi"], g1=norm1_g, g2=norm2_g, gf=final_g)

    loss_local, dx, grads, dgf = _local_fwd_bwd(x[0], loss_target[0], W, placed)
    loss = lax.psum(loss_local, ("x", "y", "c"))

    key = dict(w_in="w_in", w_pa="w_pa", w_pb="w_pb", w_o="w_o", w_mlp1="w1", w_mlp2="w2", lru_wr="wr", lru_wi="wi")
    out_g, out_d, out_m, out_v = {}, {}, {}, {}
    per_layer = [[grads[l][key[n]].astype(BF16) for l in range(depth)] for n in LRU]
    land = _scatter_grads(per_layer, [BIG_DIM[n] - 1 for n in LRU])
    chip_sums = [_sum_slots(f"sum_slots_{n}", ld.reshape(N_CHIPS, -1, ld.shape[-1])) for n, ld in zip(LRU, land)]
    sib_sums = _swap_with_sibling(chip_sums)
    for n, mine, sib in zip(LRU, chip_sums, sib_sums):
        out_g[n], out_d[n], out_m[n], out_v[n] = _adamw_big(f"adamw_{n}", wts[n], mom[n], vel[n], [mine, sib])
    core = lax.axis_index("c").astype(jnp.int32).reshape(1)
    mat_grads = [grads[l][key[n]] for n, l in items]
    got = _swap_halves("swap_halves", mat_grads, mat_dims)
    sums = [_add_halves(f"add_halves_{n}_{l}", g, r, d, core)
            for (n, l), g, r, d in zip(items, mat_grads, got, mat_dims)]
    landed = _scatter_halves("scatter_halves", sums)
    acc = {n: None for n in MATS}
    for (n, l), s, ld in zip(items, sums, landed):
        acc[n] = _reduce_into(f"reduce_{n}_{l}", s, ld, acc[n], l, BIG_DIM[n], wts[n].shape, where)
    joined = _join_halves("join_halves", [acc[n] for n in MATS], [BIG_DIM[n] for n in MATS])
    for n, g in zip(MATS, joined):
        out_g[n], out_d[n], out_m[n], out_v[n] = _adamw_big(f"adamw_{n}", wts[n], mom[n], vel[n], [g])

    rows = []
    for l in range(depth):
        g = grads[l]
        rows.append(_rows_at(g["sm"], 0, SMALL_ROWS) + _rows_at(g["b_in"].reshape(7, D), S_BIN, SMALL_ROWS)
                    + _rows_at(g["g1"], S_G1, SMALL_ROWS) + _rows_at(g["g2"], S_G2, SMALL_ROWS))
    rows.append(_rows_at(dgf, 0, 8))
    tot = _allreduce_small(jnp.concatenate(rows, axis=0))
    per = tot[:depth * SMALL_ROWS].reshape(depth, SMALL_ROWS, D)

    def cols_of_chip(a, axis):
        size = a.shape[axis] // N_CHIPS
        return lax.dynamic_slice_in_dim(a, chip * size, size, axis=axis)

    small_g = dict(
        norm1_g=per[:, S_G1], b_in=per[:, S_BIN:S_BIN + 7].reshape(depth, 7 * D),
        conv_a_w=cols_of_chip(per[:, R_CAW:R_CAW + 4], 2), conv_a_b=per[:, R_CAB],
        lru_br=cols_of_chip(per[:, R_BR].reshape(depth, nb, bw), 2),
        lru_bi=cols_of_chip(per[:, R_BI].reshape(depth, nb, bw), 2), lru_lam=per[:, R_LAM],
        conv_b_w=cols_of_chip(per[:, R_CBW:R_CBW + 3], 2), norm2_g=per[:, S_G2],
        final_g=tot[depth * SMALL_ROWS])
    for n, g in small_g.items():
        out_g[n] = g
        out_d[n], out_m[n], out_v[n] = _adamw_small(f"adamw_{n}", wts[n], g, mom[n], vel[n])

    return (loss, dx[None], *[out_g[n] for n in WEIGHTS], *[out_d[n] for n in WEIGHTS],
            *[out_m[n] for n in WEIGHTS], *[out_v[n] for n in WEIGHTS])
```

```python
import functools

import jax
import jax.numpy as jnp
from jax import lax
from jax.experimental import pallas as pl
from jax.experimental.pallas import tpu as pltpu

F32 = jnp.float32
BF16 = jnp.bfloat16
MESH = pl.DeviceIdType.MESH

EPS = 1e-6
LRU_C = 8.0
ADAM_LR = 0.001
ADAM_B1 = 0.9
ADAM_B2 = 0.999
ADAM_EPS = 1e-08
ADAM_WD = 0.01
ADAM_STEP = 10

N_CHIPS = 4
N_DEV = 8
HALO = 8
VMEM_LIMIT = 56 * 1024 * 1024
MM_TILES = (1024, 1024, 1024)
MM_TILES_FUSED = (512, 1024, 1024)
SEQ_CHUNK = 256
ROW_TILE = 256

R_CAB, R_BR, R_BI, R_LAM, R_CAW, R_CBW, R_ROWS = 0, 1, 2, 3, 4, 8, 16


def _cparams(sem):
    return pltpu.CompilerParams(dimension_semantics=sem, vmem_limit_bytes=VMEM_LIMIT)


def _div_tile(n, pref, unit=128):
    if n <= pref:
        return n
    t = (pref // unit) * unit
    while n % t:
        t -= unit
    return t


def _sigmoid(v):
    return 1.0 / (1.0 + jnp.exp(-v))


def _gelu_and_grad(y):
    k = 0.7978845608028654
    c = 0.044715
    y2 = y * y
    t = jnp.tanh(k * (y + c * y2 * y))
    g = 0.5 * y * (1.0 + t)
    gp = 0.5 * (1.0 + t) + 0.5 * y * (1.0 - t * t) * (k * (1.0 + 3.0 * c * y2))
    return g, gp


def _softplus_neg(lam):
    e = jnp.exp(-jnp.abs(lam))
    w = 1.0 + e
    l1p = jnp.where(w == 1.0, e, jnp.log(w) * e / jnp.where(w == 1.0, 1.0, w - 1.0))
    return jnp.maximum(-lam, 0.0) + l1p


def _mm(name, mode, a, b, M, N, K, out_dtypes, epilogue=None, extras=(), la=None, lb=None, tiles=None,
        carry=None):
    tiles = MM_TILES if tiles is None else tiles
    tm, tn, tk = _div_tile(M, tiles[0]), _div_tile(N, tiles[1]), _div_tile(K, tiles[2])
    assert M % tm == 0 and N % tn == 0 and K % tk == 0, (name, M, N, K)
    nk = K // tk

    def spec(lead, shape, imap):
        if lead is None:
            return pl.BlockSpec(shape, imap)
        return pl.BlockSpec((None,) + shape, lambda i, j, k: (lead,) + imap(i, j, k))

    if mode == "nn":
        a_spec = spec(la, (tm, tk), lambda i, j, k: (i, k))
        b_spec = spec(lb, (tk, tn), lambda i, j, k: (k, j))
        dn = (((1,), (0,)), ((), ()))
    elif mode == "nt":
        a_spec = spec(la, (tm, tk), lambda i, j, k: (i, k))
        b_spec = spec(lb, (tn, tk), lambda i, j, k: (j, k))
        dn = (((1,), (1,)), ((), ()))
    else:
        a_spec = spec(la, (tk, tm), lambda i, j, k: (k, i))
        b_spec = spec(lb, (tk, tn), lambda i, j, k: (k, j))
        dn = (((0,), (0,)), ((), ()))

    ex_arrays, ex_specs = [], []
    for arr, kind, off in extras:
        ex_arrays.append(arr)
        if kind == "bias":
            ex_specs.append(pl.BlockSpec((1, tn), lambda i, j, k: (0, j)))
        else:
            assert off % tn == 0
            ex_specs.append(pl.BlockSpec((tm, tn), lambda i, j, k, o=off // tn: (i, j + o)))
    n_ex, n_out = len(ex_arrays), len(out_dtypes)
    n_cin = len(carry.ins) if carry else 0
    n_cout = len(carry.out_shapes) if carry else 0
    n_in = 2 + n_ex + n_cin
    gi, gj = M // tm, N // tn

    def body(*refs):
        a_ref, b_ref = refs[0], refs[1]
        ex = refs[2:2 + n_ex]
        outs = refs[n_in:n_in + n_out]
        acc = refs[n_in + n_out + n_cout]
        i, j, k = pl.program_id(0), pl.program_id(1), pl.program_id(2)
        if carry:
            c_in, c_out = refs[2 + n_ex:n_in], refs[n_in + n_out:n_in + n_out + n_cout]
            sems = refs[n_in + n_out + n_cout + 1:]

            @pl.when((i == 0) & (j == 0) & (k == 0))
            def _():
                carry.start(c_in, c_out, sems)

        @pl.when(k == 0)
        def _():
            acc[...] = jnp.zeros_like(acc)

        acc[...] += lax.dot_general(a_ref[...], b_ref[...], dn, preferred_element_type=F32)

        @pl.when(k == nk - 1)
        def _():
            r = acc[...]
            vals = (r,) if epilogue is None else epilogue(r, *[e[...] for e in ex])
            for o, v in zip(outs, vals):
                o[...] = v.astype(o.dtype)

        if carry:
            @pl.when((i == gi - 1) & (j == gj - 1) & (k == nk - 1))
            def _():
                carry.finish(c_in, c_out, sems)

    res = pl.pallas_call(
        body,
        grid=(gi, gj, nk),
        in_specs=[a_spec, b_spec, *ex_specs] + [ANY] * n_cin,
        out_specs=[pl.BlockSpec((tm, tn), lambda i, j, k: (i, j)) for _ in range(n_out)] + [ANY] * n_cout,
        out_shape=[jax.ShapeDtypeStruct((M, N), d) for d in out_dtypes] + (carry.out_shapes if carry else []),
        scratch_shapes=[pltpu.VMEM((tm, tn), F32)] + (carry.scratch() if carry else []),
        input_output_aliases={2 + n_ex + ci: n_out + co for ci, co in carry.aliases.items()} if carry else {},
        compiler_params=_cparams(("arbitrary",) * 3 if carry else ("parallel", "parallel", "arbitrary")),
        name=name,
    )(a, b, *ex_arrays, *(carry.ins if carry else []))
    main = res[0] if n_out == 1 else res[:n_out]
    return (main, res[n_out:]) if carry else main


def _rms_fwd(name, x, g_row):
    T, D = x.shape
    tm = min(ROW_TILE, T)

    def body(x_ref, g_ref, h_ref):
        xv = x_ref[...]
        r = lax.rsqrt(jnp.mean(xv * xv, axis=-1, keepdims=True) + EPS)
        h_ref[...] = (xv * r * g_ref[...]).astype(BF16)

    return pl.pallas_call(
        body,
        grid=(T // tm,),
        in_specs=[pl.BlockSpec((tm, D), lambda i: (i, 0)), pl.BlockSpec((1, D), lambda i: (0, 0))],
        out_specs=pl.BlockSpec((tm, D), lambda i: (i, 0)),
        out_shape=jax.ShapeDtypeStruct((T, D), BF16),
        compiler_params=_cparams(("parallel",)),
        name=name,
    )(x, g_row)


def _rms_bwd(name, x, g_row, dh, dres):
    T, D = x.shape
    tm = min(ROW_TILE, T)

    def body(x_ref, g_ref, dh_ref, dres_ref, dx_ref, dxb_ref, dg_ref):
        xv, dhv = x_ref[...], dh_ref[...]
        r = lax.rsqrt(jnp.mean(xv * xv, axis=-1, keepdims=True) + EPS)
        gd = g_ref[...] * dhv
        c = jnp.mean(xv * gd, axis=-1, keepdims=True)
        dx = r * gd - xv * (r * r * r) * c + dres_ref[...]
        dx_ref[...] = dx
        dxb_ref[...] = dx.astype(BF16)

        @pl.when(pl.program_id(0) == 0)
        def _():
            dg_ref[...] = jnp.zeros_like(dg_ref)

        dg_ref[...] += jnp.sum(dhv * xv * r, axis=0, keepdims=True)

    row = pl.BlockSpec((tm, D), lambda i: (i, 0))
    vec = pl.BlockSpec((1, D), lambda i: (0, 0))
    return pl.pallas_call(
        body,
        grid=(T // tm,),
        in_specs=[row, vec, row, row],
        out_specs=[row, row, vec],
        out_shape=[jax.ShapeDtypeStruct((T, D), F32), jax.ShapeDtypeStruct((T, D), BF16),
                   jax.ShapeDtypeStruct((1, D), F32)],
        compiler_params=_cparams(("arbitrary",)),
        name=name,
    )(x, g_row, dh, dres)


def _loss_head(name, x, g_row, tgt):
    T, D = x.shape
    tm = min(ROW_TILE, T)

    def body(x_ref, g_ref, t_ref, dx_ref, dxb_ref, dg_ref, loss_ref):
        xv, g = x_ref[...], g_ref[...]
        r = lax.rsqrt(jnp.mean(xv * xv, axis=-1, keepdims=True) + EPS)
        xh = xv * r
        e = xh * g - t_ref[...]
        lpart = 0.5 * jnp.sum(jnp.mean(e * e, axis=-1, keepdims=True))
        dy = e * (1.0 / D)
        gd = g * dy
        c = jnp.mean(xv * gd, axis=-1, keepdims=True)
        dx = r * gd - xv * (r * r * r) * c
        dx_ref[...] = dx
        dxb_ref[...] = dx.astype(BF16)

        @pl.when(pl.program_id(0) == 0)
        def _():
            dg_ref[...] = jnp.zeros_like(dg_ref)
            loss_ref[...] = jnp.zeros_like(loss_ref)

        dg_ref[...] += jnp.sum(dy * xh, axis=0, keepdims=True)
        loss_ref[...] += jnp.full(loss_ref.shape, lpart, F32)

    row = pl.BlockSpec((tm, D), lambda i: (i, 0))
    vec = pl.BlockSpec((1, D), lambda i: (0, 0))
    return pl.pallas_call(
        body,
        grid=(T // tm,),
        in_specs=[row, vec, row],
        out_specs=[row, row, vec, pl.BlockSpec((8, 128), lambda i: (0, 0))],
        out_shape=[jax.ShapeDtypeStruct((T, D), F32), jax.ShapeDtypeStruct((T, D), BF16),
                   jax.ShapeDtypeStruct((1, D), F32), jax.ShapeDtypeStruct((8, 128), F32)],
        compiler_params=_cparams(("arbitrary",)),
        name=name,
    )(x, g_row, tgt)


def _colsum(name, a):
    T, N = a.shape
    tm, tn = min(512, T), _div_tile(N, 2048)

    def body(a_ref, o_ref):
        @pl.when(pl.program_id(1) == 0)
        def _():
            o_ref[...] = jnp.zeros_like(o_ref)

        o_ref[...] += jnp.sum(a_ref[...].astype(F32), axis=0, keepdims=True)

    return pl.pallas_call(
        body,
        grid=(N // tn, T // tm),
        in_specs=[pl.BlockSpec((tm, tn), lambda j, i: (i, j))],
        out_specs=pl.BlockSpec((1, tn), lambda j, i: (0, j)),
        out_shape=jax.ShapeDtypeStruct((1, N), F32),
        compiler_params=_cparams(("parallel", "arbitrary")),
        name=name,
    )(a)


def _tile_scan(a, b, row, reverse):
    for s in (1, 2, 4):
        if reverse:
            a_s, b_s, m = pltpu.roll(a, 8 - s, 0), pltpu.roll(b, 8 - s, 0), row < 8 - s
        else:
            a_s, b_s, m = pltpu.roll(a, s, 0), pltpu.roll(b, s, 0), row >= s
        b = jnp.where(m, a * b_s + b, b)
        a = jnp.where(m, a * a_s, a)
    return a, b


def _chunk_scan(a_s, b_s, out_ref, carry, n_tiles, width, reverse):
    row = lax.broadcasted_iota(jnp.int32, (8, width), 0)
    edge = 0 if reverse else 7

    def step(j, c):
        jj = (n_tiles - 1 - j) if reverse else j
        o = pl.multiple_of(jj * 8, 8)
        ca, cb = _tile_scan(a_s[pl.ds(o, 8), :], b_s[pl.ds(o, 8), :], row, reverse)
        h = ca * c + cb
        out_ref[pl.ds(o, 8), :] = h
        return jnp.broadcast_to(h[edge:edge + 1, :], (8, width))

    carry[...] = lax.fori_loop(0, n_tiles, step, carry[...])


def _gates(xc, p_ref, wr_ref, wi_ref):
    xcb = xc.astype(BF16)
    r = _sigmoid(jnp.dot(xcb, wr_ref[...], preferred_element_type=F32) + p_ref[R_BR:R_BR + 1, :])
    ig = _sigmoid(jnp.dot(xcb, wi_ref[...], preferred_element_type=F32) + p_ref[R_BI:R_BI + 1, :])
    sp = _softplus_neg(p_ref[R_LAM:R_LAM + 1, :])
    log_a = (-LRU_C) * r * sp
    a = jnp.exp(log_a)
    t = jnp.tanh(log_a)
    mult = jnp.sqrt(-2.0 * t / (1.0 - t))
    return xcb, r, ig, sp, a, mult


def _mixer_specs(Tc, bw, nb, layer):
    def seg(s):
        return pl.BlockSpec((Tc, bw), lambda n, i: (i, s * nb + n))

    p_spec = pl.BlockSpec((None, R_ROWS, bw), lambda n, i: (layer, 0, n))
    w_spec = pl.BlockSpec((None, None, bw, bw), lambda n, i: (layer, n, 0, 0))
    return seg, p_spec, w_spec


def _mixer_fwd(name, layer, z, pch, wr, wi, comm=None):
    T, D = z.shape[0], z.shape[1] // 7
    bw, nb = wr.shape[-1], wr.shape[1]
    Tc = min(SEQ_CHUNK, T)
    nT = T // Tc
    n_cin = len(comm.ins) if comm else 0
    n_cout = len(comm.out_shapes) if comm else 0

    def body(*refs):
        xa_ref, ya_ref, cb_ref, cc_ref, cx_ref, p_ref, wr_ref, wi_ref = refs[:8]
        pa_ref, pb_ref, xc_ref, hl_ref, vb_ref = refs[8 + n_cin:13 + n_cin]
        xa_buf, u_buf, a_s, b_s, carry = refs[13 + n_cin + n_cout:18 + n_cin + n_cout]
        if comm:
            c_in, c_out, sems = refs[8:8 + n_cin], refs[13 + n_cin:13 + n_cin + n_cout], refs[18 + n_cin + n_cout:]

            @pl.when((pl.program_id(0) == 0) & (pl.program_id(1) == 0))
            def _():
                comm.start(c_in, c_out, sems)

        @pl.when(pl.program_id(1) == 0)
        def _():
            xa_buf[0:HALO, :] = jnp.zeros((HALO, bw), F32)
            u_buf[0:HALO, :] = jnp.zeros((HALO, bw), F32)
            carry[...] = jnp.zeros_like(carry)

        xa_buf[HALO:HALO + Tc, :] = xa_ref[...]
        xc = p_ref[R_CAB:R_CAB + 1, :]
        for k in range(4):
            xc = xc + p_ref[R_CAW + k:R_CAW + k + 1, :] * xa_buf[HALO - 3 + k:HALO - 3 + k + Tc, :]
        xc_ref[...] = xc
        _, _, ig, _, a, mult = _gates(xc, p_ref, wr_ref, wi_ref)
        a_s[...] = a
        b_s[...] = mult * (ig * xc)
        _chunk_scan(a_s, b_s, hl_ref, carry, Tc // 8, bw, False)
        g, _ = _gelu_and_grad(ya_ref[...])
        pa_ref[...] = (hl_ref[...] * g).astype(BF16)

        u_buf[HALO:HALO + Tc, :] = cc_ref[...] * cx_ref[...]
        vb = jnp.zeros((Tc, bw), F32)
        for k in range(3):
            vb = vb + p_ref[R_CBW + k:R_CBW + k + 1, :] * u_buf[HALO - 2 + k:HALO - 2 + k + Tc, :]
        vb_ref[...] = vb
        pb_ref[...] = (cb_ref[...] * vb).astype(BF16)
        xa_buf[0:HALO, :] = xa_buf[Tc:Tc + HALO, :]
        u_buf[0:HALO, :] = u_buf[Tc:Tc + HALO, :]

        if comm:
            @pl.when((pl.program_id(0) == nb - 1) & (pl.program_id(1) == nT - 1))
            def _():
                comm.finish(c_in, c_out, sems)

    seg, p_spec, w_spec = _mixer_specs(Tc, bw, nb, layer)
    out = pl.BlockSpec((Tc, bw), lambda n, i: (i, n))
    res = pl.pallas_call(
        body,
        grid=(nb, nT),
        in_specs=[seg(0), seg(1), seg(2), seg(3), seg(4), p_spec, w_spec, w_spec] + [ANY] * n_cin,
        out_specs=[out] * 5 + [ANY] * n_cout,
        out_shape=[jax.ShapeDtypeStruct((T, D), BF16), jax.ShapeDtypeStruct((T, D), BF16),
                   jax.ShapeDtypeStruct((T, D), F32), jax.ShapeDtypeStruct((T, D), F32),
                   jax.ShapeDtypeStruct((T, D), F32)] + (comm.out_shapes if comm else []),
        scratch_shapes=[pltpu.VMEM((Tc + HALO, bw), F32), pltpu.VMEM((Tc + HALO, bw), F32),
                        pltpu.VMEM((Tc, bw), F32), pltpu.VMEM((Tc, bw), F32), pltpu.VMEM((8, bw), F32)]
        + (comm.scratch() if comm else []),
        input_output_aliases={8 + ci: 5 + co for ci, co in comm.aliases.items()} if comm else {},
        compiler_params=_cparams(("arbitrary", "arbitrary") if comm else ("parallel", "arbitrary")),
        name=name,
    )(z, z, z, z, z, pch, wr, wi, *(comm.ins if comm else []))
    return (res[:5], res[5:]) if comm else res


def _mixer_bwd(name, layer, z, xc, hl, vb, dpa, dpb, pch, wr, wi, comm=None):
    T, D = z.shape[0], z.shape[1] // 7
    bw, nb = wr.shape[-1], wr.shape[1]
    Tc = min(SEQ_CHUNK, T)
    nT = T // Tc
    tpc = Tc // 8
    n_cin = len(comm.ins) if comm else 0
    n_cout = len(comm.out_shapes) if comm else 0

    def body(*refs):
        (xa_ref, ya_ref, cb_ref, cc_ref, cx_ref, xc_ref, hl_ref, hp_ref, vb_ref, dpa_ref, dpb_ref,
         p_ref, wr_ref, wi_ref) = refs[:14]
        dxa_ref, dya_ref, dcb_ref, dcc_ref, dcx_ref, dwr_ref, dwi_ref, sm_ref = refs[14 + n_cin:22 + n_cin]
        h_buf, a_buf, dxc_buf, dvb_buf, a_s, d_s, lam_s, carry = refs[22 + n_cin + n_cout:30 + n_cin + n_cout]
        i = pl.program_id(1)
        if comm:
            c_in, c_out = refs[14:14 + n_cin], refs[22 + n_cin:22 + n_cin + n_cout]
            sems = refs[30 + n_cin + n_cout:]

            @pl.when((pl.program_id(0) == 0) & (i == 0))
            def _():
                comm.start(c_in, c_out, sems)

        @pl.when(i == 0)
        def _():
            a_buf[Tc:Tc + HALO, :] = jnp.zeros((HALO, bw), F32)
            dxc_buf[Tc:Tc + HALO, :] = jnp.zeros((HALO, bw), F32)
            dvb_buf[Tc:Tc + HALO, :] = jnp.zeros((HALO, bw), F32)
            carry[...] = jnp.zeros_like(carry)
            dwr_ref[...] = jnp.zeros_like(dwr_ref)
            dwi_ref[...] = jnp.zeros_like(dwi_ref)
            sm_ref[...] = jnp.zeros_like(sm_ref)

        xcv = xc_ref[...]
        xcb, r, ig, sp, a, mult = _gates(xcv, p_ref, wr_ref, wi_ref)
        g, gp = _gelu_and_grad(ya_ref[...])
        hlv, dpav = hl_ref[...], dpa_ref[...]
        dya_ref[...] = (dpav * hlv * gp).astype(BF16)

        a_buf[0:Tc, :] = a
        a_s[...] = a_buf[1:Tc + 1, :]
        d_s[...] = dpav * g
        _chunk_scan(a_s, d_s, lam_s, carry, tpc, bw, True)
        lamv = lam_s[...]

        h_buf[HALO:HALO + Tc, :] = hlv
        h_buf[0:HALO, :] = jnp.where(i == nT - 1, 0.0, hp_ref[...])
        da = lamv * h_buf[HALO - 1:HALO - 1 + Tc, :]
        dmult = lamv * (ig * xcv)
        dbx = lamv * mult
        dig = dbx * xcv
        dxc = dbx * ig
        dlog_a = da * a - dmult * (a * a) / mult
        dpr = (dlog_a * ((-LRU_C) * sp)) * r * (1.0 - r)
        dpi = dig * ig * (1.0 - ig)
        dprb, dpib = dpr.astype(BF16), dpi.astype(BF16)
        nt = (((1,), (1,)), ((), ()))
        tn = (((0,), (0,)), ((), ()))
        dxc = dxc + lax.dot_general(dprb, wr_ref[...], nt, preferred_element_type=F32)
        dxc = dxc + lax.dot_general(dpib, wi_ref[...], nt, preferred_element_type=F32)
        dwr_ref[...] += lax.dot_general(xcb, dprb, tn, preferred_element_type=F32)
        dwi_ref[...] += lax.dot_general(xcb, dpib, tn, preferred_element_type=F32)

        def rowsum(v):
            return jnp.sum(v, axis=0, keepdims=True)

        sm_ref[R_CAB:R_CAB + 1, :] += rowsum(dxc)
        sm_ref[R_BR:R_BR + 1, :] += rowsum(dpr)
        sm_ref[R_BI:R_BI + 1, :] += rowsum(dpi)
        sm_ref[R_LAM:R_LAM + 1, :] += rowsum(dlog_a * ((-LRU_C) * r))

        dxc_buf[0:Tc, :] = dxc
        xav = xa_ref[...]
        dxa = jnp.zeros((Tc, bw), F32)
        for k in range(4):
            sh = dxc_buf[3 - k:3 - k + Tc, :]
            dxa = dxa + p_ref[R_CAW + k:R_CAW + k + 1, :] * sh
            sm_ref[R_CAW + k:R_CAW + k + 1, :] += rowsum(xav * sh)
        dxa_ref[...] = dxa.astype(BF16)

        dpbv, cbv, ccv, cxv = dpb_ref[...], cb_ref[...], cc_ref[...], cx_ref[...]
        dcb_ref[...] = (dpbv * vb_ref[...]).astype(BF16)
        dvb_buf[0:Tc, :] = dpbv * cbv
        u = ccv * cxv
        du = jnp.zeros((Tc, bw), F32)
        for k in range(3):
            sh = dvb_buf[2 - k:2 - k + Tc, :]
            du = du + p_ref[R_CBW + k:R_CBW + k + 1, :] * sh
            sm_ref[R_CBW + k:R_CBW + k + 1, :] += rowsum(u * sh)
        dcc_ref[...] = (du * cxv).astype(BF16)
        dcx_ref[...] = (du * ccv).astype(BF16)

        a_buf[Tc:Tc + HALO, :] = a_buf[0:HALO, :]
        dxc_buf[Tc:Tc + HALO, :] = dxc_buf[0:HALO, :]
        dvb_buf[Tc:Tc + HALO, :] = dvb_buf[0:HALO, :]

        @pl.when(i == nT - 1)
        def _():
            sm_ref[R_LAM:R_LAM + 1, :] = sm_ref[R_LAM:R_LAM + 1, :] * (-_sigmoid(-p_ref[R_LAM:R_LAM + 1, :]))

        if comm:
            @pl.when((pl.program_id(0) == nb - 1) & (i == nT - 1))
            def _():
                comm.finish(c_in, c_out, sems)

    def seg(s):
        return pl.BlockSpec((Tc, bw), lambda n, i: (nT - 1 - i, s * nb + n))

    blk = pl.BlockSpec((Tc, bw), lambda n, i: (nT - 1 - i, n))
    halo = pl.BlockSpec((8, bw), lambda n, i: (jnp.maximum((nT - 1 - i) * tpc - 1, 0), n))
    p_spec = pl.BlockSpec((None, R_ROWS, bw), lambda n, i: (layer, 0, n))
    w_spec = pl.BlockSpec((None, None, bw, bw), lambda n, i: (layer, n, 0, 0))
    dw_spec = pl.BlockSpec((None, bw, bw), lambda n, i: (n, 0, 0))
    act = jax.ShapeDtypeStruct((T, D), BF16)
    res = pl.pallas_call(
        body,
        grid=(nb, nT),
        in_specs=[seg(0), seg(1), seg(2), seg(3), seg(4), blk, blk, halo, blk, blk, blk, p_spec, w_spec, w_spec]
        + [ANY] * n_cin,
        out_specs=[blk] * 5 + [dw_spec, dw_spec, pl.BlockSpec((R_ROWS, bw), lambda n, i: (0, n))] + [ANY] * n_cout,
        out_shape=[act] * 5 + [jax.ShapeDtypeStruct((nb, bw, bw), F32), jax.ShapeDtypeStruct((nb, bw, bw), F32),
                               jax.ShapeDtypeStruct((R_ROWS, D), F32)] + (comm.out_shapes if comm else []),
        scratch_shapes=[pltpu.VMEM((Tc + HALO, bw), F32)] * 4 + [pltpu.VMEM((Tc, bw), F32)] * 3
        + [pltpu.VMEM((8, bw), F32)] + (comm.scratch() if comm else []),
        input_output_aliases={14 + ci: 8 + co for ci, co in comm.aliases.items()} if comm else {},
        compiler_params=_cparams(("arbitrary", "arbitrary") if comm else ("parallel", "arbitrary")),
        name=name,
    )(z, z, z, z, z, xc, hl, hl, vb, dpa, dpb, pch, wr, wi, *(comm.ins if comm else []))
    return (res[:8], res[8:]) if comm else res


def _local_fwd_bwd(x, tgt, W, placed=None):
    T, D = x.shape
    g1, g2 = W["g1"], W["g2"]
    depth = g1.shape[0]
    FF = 4 * D
    if placed is None:
        mats = {(n, l): W[n][l] for n in MATS for l in range(depth)}
    else:
        mats = {}
        mats["w_in", 0], = _run_carry("gather_first", _gather_carry([placed["w_in", 0]], [(0, 1, 0, 1)]))

    def gathering(specs):
        if placed is None or not specs:
            return None, []
        keys = [(n, l) for n, l, _, _, _ in specs]
        arrays = [mats.get(k, placed[k]) for k in keys]
        return _gather_carry(arrays, [(i, d, part, nparts) for i, (_, _, d, part, nparts) in enumerate(specs)]), keys

    def hosted(call, specs, **kw):
        carry, keys = gathering(specs)
        if carry is None:
            return call(**kw)
        res, got = call(**kw, **{("comm" if call.func is _mixer_fwd else "carry"): carry})
        mats.update(zip(keys, got))
        return res

    saved = []
    xs = x
    for l in range(depth):
        h = _rms_fwd(f"rms1_fwd_{l}", xs, g1[l][None])
        z = hosted(functools.partial(_mm, f"in_proj_{l}", "nn", h, mats["w_in", l], T, 7 * D, D, [F32]),
                   [("w_pa", l, 0, 0, 1), ("w_pb", l, 0, 0, 1), ("w_o", l, 0, 0, 1), ("w_mlp1", l, 1, 0, 1)],
                   epilogue=lambda acc, b: (acc + b,), extras=[(W["b_in"][l][None], "bias", 0)])
        pa, pb, xc, hl, vb = hosted(
            functools.partial(_mixer_fwd, f"mixer_fwd_{l}", l, z, W["pch"], W["wr"], W["wi"]),
            [("w_mlp2", l, 0, 0, 1)])
        oa = _mm(f"proj_a_{l}", "nn", pa, mats["w_pa", l], T, D, D, [F32])

        def merge(acc, oav, ga, gb):
            return acc, _sigmoid(ga) * oav + _sigmoid(gb) * acc

        ob, mg = _mm(f"proj_b_merge_{l}", "nn", pb, mats["w_pb", l], T, D, D, [F32, BF16], epilogue=merge,
                     tiles=MM_TILES_FUSED,
                     extras=[(oa, "tile", 0), (z, "tile", 5 * D), (z, "tile", 6 * D)])
        x1 = _mm(f"out_proj_{l}", "nn", mg, mats["w_o", l], T, D, D, [F32],
                 epilogue=lambda acc, res: (res + acc,), extras=[(xs, "tile", 0)])
        h2 = _rms_fwd(f"rms2_fwd_{l}", x1, g2[l][None])

        def relu2(acc):
            pr = jnp.maximum(acc, 0.0)
            return pr * pr, pr

        nxt = l + 1 < depth
        u, pr = hosted(functools.partial(_mm, f"mlp1_{l}", "nn", h2, mats["w_mlp1", l], T, FF, D, [BF16, BF16]),
                       [("w_in", l + 1, 1, 0, 2)] if nxt else [], epilogue=relu2)
        x2 = hosted(functools.partial(_mm, f"mlp2_{l}", "nn", u, mats["w_mlp2", l], T, D, FF, [F32]),
                    [("w_in", l + 1, 1, 1, 2)] if nxt else [],
                    epilogue=lambda acc, res: (res + acc,), extras=[(x1, "tile", 0)])
        saved.append(dict(x0=xs, h=h, z=z, pa=pa, pb=pb, xc=xc, hl=hl, vb=vb, oa=oa, ob=ob, mg=mg, x1=x1,
                          h2=h2, u=u, pr=pr))
        xs = x2

    dx, dxb, dgf, loss_blk = _loss_head("loss_head", xs, W["gf"][None], tgt)

    gmat, got, sums, landed = {}, {}, {}, {}

    def reducing(call, swaps=(), scatters=(), mixer=False, **kw):
        swaps, scatters = [(n, l) for n in swaps], [(n, l) for n in scatters]
        if placed is None:
            return call(**kw)
        cs = _swap_carry([gmat[k] for k in swaps], [BIG_DIM[k[0]] - 1 for k in swaps]) if swaps else None
        cc = _scatter_carry([sums[k] for k in scatters]) if scatters else None
        res, moved = call(**kw, **{("comm" if mixer else "carry"): _merge_carries([cs, cc])})
        got.update(zip(swaps, moved[:len(swaps)]))
        landed.update(zip(scatters, moved[len(swaps):]))
        return res

    def add(names):
        if placed is not None:
            for n in names:
                sums[n, l] = _add_halves(f"add_halves_{n}_{l}", gmat[n, l], got[n, l], BIG_DIM[n] - 1, W["core"])

    grads = [None] * depth
    for l in reversed(range(depth)):
        s = saved[l]
        dp = _mm(f"mlp2_dx_{l}", "nt", dxb, mats["w_mlp2", l], T, FF, D, [BF16],
                 epilogue=lambda acc, prv: (2.0 * prv.astype(F32) * acc,), extras=[(s["pr"], "tile", 0)])
        dw2 = gmat["w_mlp2", l] = _mm(f"mlp2_dw_{l}", "tn", s["u"], dxb, FF, D, T, [BF16])
        dh2 = reducing(functools.partial(_mm, f"mlp1_dx_{l}", "nt", dp, mats["w_mlp1", l], T, D, FF, [F32]),
                       swaps=["w_mlp2"])
        add(["w_mlp2"])
        dw1 = gmat["w_mlp1", l] = reducing(functools.partial(_mm, f"mlp1_dw_{l}", "tn", s["h2"], dp, D, FF, T, [BF16]),
                                           scatters=["w_mlp2"])
        dx1, dx1b, dg2 = _rms_bwd(f"rms2_bwd_{l}", s["x1"], g2[l][None], dh2, dx)

        def unmerge(acc, ga, gb, oav, obv):
            sa, sb = _sigmoid(ga), _sigmoid(gb)
            return acc * sa, acc * sb, acc * oav * sa * (1.0 - sa), acc * obv * sb * (1.0 - sb)

        doa, dob, dga, dgb = reducing(
            functools.partial(_mm, f"out_proj_dx_{l}", "nt", dx1b, mats["w_o", l], T, D, D, [BF16] * 4),
            swaps=["w_mlp1"], tiles=MM_TILES_FUSED, epilogue=unmerge,
            extras=[(s["z"], "tile", 5 * D), (s["z"], "tile", 6 * D), (s["oa"], "tile", 0), (s["ob"], "tile", 0)])
        add(["w_mlp1"])
        dwo = gmat["w_o", l] = _mm(f"out_proj_dw_{l}", "tn", s["mg"], dx1b, D, D, T, [BF16])
        dpa = _mm(f"proj_a_dx_{l}", "nt", doa, mats["w_pa", l], T, D, D, [F32])
        dwpa = gmat["w_pa", l] = _mm(f"proj_a_dw_{l}", "tn", s["pa"], doa, D, D, T, [BF16])
        dpb = _mm(f"proj_b_dx_{l}", "nt", dob, mats["w_pb", l], T, D, D, [F32])
        dwpb = gmat["w_pb", l] = _mm(f"proj_b_dw_{l}", "tn", s["pb"], dob, D, D, T, [BF16])
        dxa, dya, dcb, dcc, dcx, dwr, dwi, sm = reducing(
            functools.partial(_mixer_bwd, f"mixer_bwd_{l}", l, s["z"], s["xc"], s["hl"], s["vb"], dpa, dpb,
                              W["pch"], W["wr"], W["wi"]),
            swaps=["w_o", "w_pa", "w_pb"], scatters=["w_mlp1"], mixer=True)
        add(["w_o", "w_pa", "w_pb"])
        dz = jnp.concatenate([dxa, dya, dcb, dcc, dcx, dga, dgb], axis=1)
        dbin = _colsum(f"bias_grad_{l}", dz)
        dwin = gmat["w_in", l] = reducing(
            functools.partial(_mm, f"in_proj_dw_{l}", "tn", s["h"], dz, D, 7 * D, T, [BF16]),
            scatters=["w_o", "w_pa", "w_pb"])
        if placed is not None:
            got["w_in", l], = _run_carry(f"swap_w_in_{l}", _swap_carry([dwin], [BIG_DIM["w_in"] - 1]))
        add(["w_in"])
        dh = reducing(functools.partial(_mm, f"in_proj_dx_{l}", "nt", dz, mats["w_in", l], T, D, 7 * D, [F32]),
                      scatters=["w_in"])
        dx, dxb, dg1 = _rms_bwd(f"rms1_bwd_{l}", s["x0"], g1[l][None], dh, dx1)
        grads[l] = dict(w_in=dwin, w_pa=dwpa, w_pb=dwpb, w_o=dwo, w_mlp1=dw1, w_mlp2=dw2, wr=dwr, wi=dwi,
                        sm=sm, b_in=dbin, g1=dg1, g2=dg2)
    return loss_blk[0, 0], dx, grads, dgf, sums, landed


ANY = pl.BlockSpec(memory_space=pl.ANY)


def _place():
    x, y, c = lax.axis_index("x"), lax.axis_index("y"), lax.axis_index("c")
    peers = [(1 - x, y, c), (x, 1 - y, c), (1 - x, 1 - y, c)]
    chips = [2 * (1 - x) + y, 2 * x + (1 - y), 2 * (1 - x) + (1 - y)]
    return (x, y, c), 2 * x + y, peers, chips


def _window(ref, dim, q, size):
    idx = [slice(None)] * len(ref.shape)
    idx[dim] = pl.ds(q * size, size)
    return ref.at[tuple(idx)]


def _gather_weights(shards, dims, small):
    n = len(shards)
    sizes = [s.shape[d] for s, d in zip(shards, dims)]
    full = [jax.ShapeDtypeStruct(s.shape[:d] + (s.shape[d] * N_CHIPS,) + s.shape[d + 1:], s.dtype)
            for s, d in zip(shards, dims)]
    full.append(jax.ShapeDtypeStruct((N_CHIPS,) + small.shape, small.dtype))

    def body(*refs):
        ins, outs = refs[:n + 1], refs[n + 1:2 * n + 2]
        send_sems, recv_sems, local_sems = refs[2 * n + 2:]
        _, k, peers, chips = _place()

        def dst(w, q):
            return outs[w].at[q] if w == n else _window(outs[w], dims[w], q, sizes[w])

        local = [pltpu.make_async_copy(ins[w], dst(w, k), local_sems.at[w]) for w in range(n + 1)]
        for cp in local:
            cp.start()
        sends = []
        for p, peer in enumerate(peers):
            for w in range(n + 1):
                s = p * (n + 1) + w
                sends.append(pltpu.make_async_remote_copy(
                    src_ref=ins[w], dst_ref=dst(w, k), send_sem=send_sems.at[s], recv_sem=recv_sems.at[s],
                    device_id=peer, device_id_type=MESH))
        for cp in sends:
            cp.start()
        for p, peer in enumerate(peers):
            for w in range(n + 1):
                s = p * (n + 1) + w
                pltpu.make_async_remote_copy(
                    src_ref=ins[w], dst_ref=dst(w, chips[p]), send_sem=send_sems.at[s], recv_sem=recv_sems.at[s],
                    device_id=peer, device_id_type=MESH).wait_recv()
        for cp in sends:
            cp.wait_send()
        for cp in local:
            cp.wait()

    return pl.pallas_call(
        body,
        in_specs=[ANY] * (n + 1),
        out_specs=[ANY] * (n + 1),
        out_shape=full,
        scratch_shapes=[pltpu.SemaphoreType.DMA((3 * (n + 1),)), pltpu.SemaphoreType.DMA((3 * (n + 1),)),
                        pltpu.SemaphoreType.DMA((n + 1,))],
        name="gather_weights",
    )(*shards, small)


def _scatter_grads(grads, dims):
    n, depth = len(grads), len(grads[0])
    sizes = [g[0].shape[d] // N_CHIPS for g, d in zip(grads, dims)]
    land = []
    for g, d, sz in zip(grads, dims, sizes):
        shp = g[0].shape
        land.append(jax.ShapeDtypeStruct((N_CHIPS, depth) + shp[:d] + (sz,) + shp[d + 1:], g[0].dtype))

    def body(*refs):
        ins, outs = refs[:n * depth], refs[n * depth:n * depth + n]
        send_sems, recv_sems, local_sems = refs[n * depth + n:]
        _, k, peers, chips = _place()

        def src(w, l, q):
            return _window(ins[w * depth + l], dims[w], q, sizes[w])

        local = [pltpu.make_async_copy(src(w, l, k), outs[w].at[3, l], local_sems.at[w * depth + l])
                 for w in range(n) for l in range(depth)]
        for cp in local:
            cp.start()
        sends = []
        for p, peer in enumerate(peers):
            for w in range(n):
                for l in range(depth):
                    s = (p * n + w) * depth + l
                    sends.append(pltpu.make_async_remote_copy(
                        src_ref=src(w, l, chips[p]), dst_ref=outs[w].at[p, l], send_sem=send_sems.at[s],
                        recv_sem=recv_sems.at[s], device_id=peer, device_id_type=MESH))
        for cp in sends:
            cp.start()
        for cp in sends:
            cp.wait_recv()
        for cp in sends:
            cp.wait_send()
        for cp in local:
            cp.wait()

    flat = [g for gl in grads for g in gl]
    return pl.pallas_call(
        body,
        in_specs=[ANY] * (n * depth),
        out_specs=[ANY] * n,
        out_shape=land,
        scratch_shapes=[pltpu.SemaphoreType.DMA((3 * n * depth,)), pltpu.SemaphoreType.DMA((3 * n * depth,)),
                        pltpu.SemaphoreType.DMA((n * depth,))],
        name="scatter_grads",
    )(*flat)


def _sum_slots(name, land):
    _, R, C = land.shape
    tr, tc = _div_tile(R, 512, 8), _div_tile(C, 1024)

    def body(a_ref, b_ref, c_ref, d_ref, o_ref):
        o_ref[...] = ((d_ref[...].astype(F32) + a_ref[...].astype(F32)) + b_ref[...].astype(F32)) \
            + c_ref[...].astype(F32)

    def slot(q):
        return pl.BlockSpec((None, tr, tc), lambda i, j: (q, i, j))

    return pl.pallas_call(
        body,
        grid=(R // tr, C // tc),
        in_specs=[slot(0), slot(1), slot(2), slot(3)],
        out_specs=pl.BlockSpec((tr, tc), lambda i, j: (i, j)),
        out_shape=jax.ShapeDtypeStruct((R, C), F32),
        compiler_params=_cparams(("parallel", "parallel")),
        name=name,
    )(land, land, land, land)


def _swap_with_sibling(parts):
    n = len(parts)

    def body(*refs):
        ins, outs = refs[:n], refs[n:2 * n]
        send_sems, recv_sems = refs[2 * n:]
        (x, y, c), _, _, _ = _place()
        copies = [pltpu.make_async_remote_copy(
            src_ref=ins[w], dst_ref=outs[w], send_sem=send_sems.at[w], recv_sem=recv_sems.at[w],
            device_id=(x, y, 1 - c), device_id_type=MESH) for w in range(n)]
        for cp in copies:
            cp.start()
        for cp in copies:
            cp.wait()

    return pl.pallas_call(
        body,
        in_specs=[ANY] * n,
        out_specs=[ANY] * n,
        out_shape=[jax.ShapeDtypeStruct(p.shape, p.dtype) for p in parts],
        scratch_shapes=[pltpu.SemaphoreType.DMA((n,)), pltpu.SemaphoreType.DMA((n,))],
        name="swap_with_sibling",
    )(*parts)


class _Carry:
    def __init__(self, ins, out_shapes, plan, n1, plan2=None, n2=0, aliases=None):
        self.ins, self.out_shapes, self.plan, self.n1 = list(ins), list(out_shapes), plan, n1
        self.plan2, self.n2, self.aliases = plan2, n2, dict(aliases or {})

    def scratch(self):
        s = [pltpu.SemaphoreType.DMA((self.n1,)), pltpu.SemaphoreType.DMA((self.n1,))]
        if self.plan2 is not None:
            s += [pltpu.SemaphoreType.DMA((self.n2,)), pltpu.SemaphoreType.DMA((self.n2,))]
        return s

    @staticmethod
    def _copy(src, dst, peer, send_sems, recv_sems, i):
        return pltpu.make_async_remote_copy(src_ref=src, dst_ref=dst, send_sem=send_sems.at[i],
                                            recv_sem=recv_sems.at[i], device_id=peer, device_id_type=MESH)

    def start(self, in_refs, out_refs, sems):
        remote = self.plan(in_refs, out_refs)
        assert len(remote) == self.n1, (len(remote), self.n1)
        for i, (s, d, peer, _) in enumerate(remote):
            self._copy(s, d, peer, sems[0], sems[1], i).start()

    def finish(self, in_refs, out_refs, sems):
        remote = self.plan(in_refs, out_refs)
        for i, (s, _, peer, landing) in enumerate(remote):
            self._copy(s, landing, peer, sems[0], sems[1], i).wait_recv()
        if self.plan2 is not None:
            second = self.plan2(in_refs, out_refs)
            assert len(second) == self.n2, (len(second), self.n2)
            for i, (s, d, peer, _) in enumerate(second):
                self._copy(s, d, peer, sems[2], sems[3], i).start()
            for i, (s, _, peer, landing) in enumerate(second):
                self._copy(s, landing, peer, sems[2], sems[3], i).wait_recv()
            for i, (s, d, peer, _) in enumerate(second):
                self._copy(s, d, peer, sems[2], sems[3], i).wait_send()
        for i, (s, d, peer, _) in enumerate(remote):
            self._copy(s, d, peer, sems[0], sems[1], i).wait_send()


def _run_carry(name, carry):
    n_in, n_out = len(carry.ins), len(carry.out_shapes)

    def body(*refs):
        in_refs, out_refs, sems = refs[:n_in], refs[n_in:n_in + n_out], refs[n_in + n_out:]
        carry.start(in_refs, out_refs, sems)
        carry.finish(in_refs, out_refs, sems)

    return pl.pallas_call(
        body,
        in_specs=[ANY] * n_in,
        out_specs=[ANY] * n_out,
        out_shape=carry.out_shapes,
        scratch_shapes=carry.scratch(),
        input_output_aliases=carry.aliases,
        name=name,
    )(*carry.ins)


def _comm_call(name, ins, out_shapes, plan, n_local, n_remote, aliases=None):
    assert n_local == 0
    return _run_carry(name, _Carry(ins, out_shapes, lambda i, o: plan(i, o)[1], n_remote, aliases=aliases))


def _gather_carry(arrays, items):
    shapes = [a.shape for a in arrays]

    def window(ref, item):
        idx, d, part, nparts = item
        h = shapes[idx][d] // (2 * N_CHIPS)
        rows = shapes[idx][1 - d] // nparts

        def win(j):
            sl = [None, None]
            sl[d] = pl.ds(j * h, h)
            sl[1 - d] = pl.ds(part * rows, rows)
            return ref.at[tuple(sl)]

        return win

    def plan1(ins, outs):
        (_, _, c), k, peers, chips = _place()
        remote = []
        for item in items:
            win = window(outs[item[0]], item)
            for p in range(3):
                remote.append((win(2 * k + c), win(2 * k + c), peers[p], win(2 * chips[p] + c)))
        return remote

    def plan2(ins, outs):
        (x, y, c), _, _, chips = _place()
        remote = []
        for item in items:
            win = window(outs[item[0]], item)
            for p in range(3):
                remote.append((win(2 * chips[p] + c), win(2 * chips[p] + c), (x, y, 1 - c),
                               win(2 * chips[p] + 1 - c)))
        return remote

    n = 3 * len(items)
    return _Carry(arrays, [jax.ShapeDtypeStruct(a.shape, a.dtype) for a in arrays], plan1, n, plan2, n,
                  aliases={i: i for i in range(len(arrays))})


def _place_shard(name, w, layer, dim, chip):
    _, a, b = w.shape
    full = (a * N_CHIPS, b) if dim == 0 else (a, b * N_CHIPS)
    tr, tc = _div_tile(a, 512, 16), _div_tile(b, 2048)
    nr, nc = a // tr, b // tc

    def out_map(i, j, chip_ref):
        return (chip_ref[0] * nr + i, j) if dim == 0 else (i, chip_ref[0] * nc + j)

    def body(chip_ref, w_ref, o_ref):
        o_ref[...] = w_ref[...].astype(o_ref.dtype)

    return pl.pallas_call(
        body,
        grid_spec=pltpu.PrefetchScalarGridSpec(
            num_scalar_prefetch=1, grid=(nr, nc),
            in_specs=[pl.BlockSpec((None, tr, tc), lambda i, j, chip_ref: (layer, i, j))],
            out_specs=pl.BlockSpec((tr, tc), out_map)),
        out_shape=jax.ShapeDtypeStruct(full, BF16),
        compiler_params=_cparams(("parallel", "parallel")),
        name=name,
    )(chip, w)


def _half_shape(shape, dim):
    return shape[:dim] + (shape[dim] // (2 * N_CHIPS),) + shape[dim + 1:]


def _swap_carry(grads, dims):
    shapes = [jax.ShapeDtypeStruct((N_CHIPS,) + _half_shape(g.shape, d), g.dtype) for g, d in zip(grads, dims)]

    def plan(ins, outs):
        (x, y, c), _, _, _ = _place()
        remote = []
        for w, d in enumerate(dims):
            h = grads[w].shape[d] // (2 * N_CHIPS)
            for q in range(N_CHIPS):
                remote.append((_window(ins[w], d, 2 * q + 1 - c, h), outs[w].at[q], (x, y, 1 - c), outs[w].at[q]))
        return remote

    return _Carry(grads, shapes, plan, N_CHIPS * len(grads))


def _scatter_carry(sums):
    def plan(ins, outs):
        _, _, peers, chips = _place()
        remote = []
        for w in range(len(sums)):
            for p in range(3):
                remote.append((ins[w].at[chips[p]], outs[w].at[p], peers[p], outs[w].at[p]))
        return remote

    return _Carry(sums, [jax.ShapeDtypeStruct((3,) + s.shape[1:], s.dtype) for s in sums], plan, 3 * len(sums))


def _merge_carries(carries):
    carries = [c for c in carries if c is not None]
    if len(carries) <= 1:
        return carries[0] if carries else None
    ins = [a for c in carries for a in c.ins]
    outs = [s for c in carries for s in c.out_shapes]

    def plan(in_refs, out_refs):
        remote, i0, o0 = [], 0, 0
        for c in carries:
            remote += c.plan(in_refs[i0:i0 + len(c.ins)], out_refs[o0:o0 + len(c.out_shapes)])
            i0, o0 = i0 + len(c.ins), o0 + len(c.out_shapes)
        return remote

    assert all(c.plan2 is None and not c.aliases for c in carries)
    return _Carry(ins, outs, plan, sum(c.n1 for c in carries))


def _add_halves(name, g, got, dim, core):
    R, C = g.shape
    if dim == 1:
        r, cc = R, C // (2 * N_CHIPS)
    else:
        r, cc = R // (2 * N_CHIPS), C
    tr, tc = _div_tile(r, 512, 16), _div_tile(cc, 1024)
    nr, nc = r // tr, cc // tc

    def g_map(q, i, j, core_ref):
        w = 2 * q + core_ref[0]
        return (i, w * nc + j) if dim == 1 else (w * nr + i, j)

    def body(core_ref, g_ref, got_ref, o_ref):
        o_ref[...] = (g_ref[...].astype(F32) + got_ref[...].astype(F32)).astype(o_ref.dtype)

    slab = pl.BlockSpec((None, tr, tc), lambda q, i, j, core_ref: (q, i, j))
    return pl.pallas_call(
        body,
        grid_spec=pltpu.PrefetchScalarGridSpec(
            num_scalar_prefetch=1, grid=(N_CHIPS, nr, nc),
            in_specs=[pl.BlockSpec((tr, tc), g_map), slab], out_specs=slab),
        out_shape=jax.ShapeDtypeStruct((N_CHIPS, r, cc), g.dtype),
        compiler_params=_cparams(("parallel", "parallel", "parallel")),
        name=name,
    )(core, g, got)


def _reduce_into(name, sums, land, acc, layer, dim, shape, where):
    _, r, cc = sums.shape
    tr, tc = _div_tile(r, 512, 16), _div_tile(cc, 1024)
    nr, nc = r // tr, cc // tc

    def out_map(i, j, s):
        return (layer, s[1] * nr + i, j) if dim == 1 else (layer, i, s[1] * nc + j)

    def body(*refs):
        own, a_ref, b_ref, c_ref, o_ref = refs[1], refs[2], refs[3], refs[4], refs[-1]
        o_ref[...] = ((own[...].astype(F32) + a_ref[...].astype(F32)) + b_ref[...].astype(F32)) \
            + c_ref[...].astype(F32)

    def slot(p):
        return pl.BlockSpec((None, tr, tc), lambda i, j, s: (p, i, j))

    in_specs = [pl.BlockSpec((None, tr, tc), lambda i, j, s: (s[0], i, j)), slot(0), slot(1), slot(2)]
    args = [where, sums, land, land, land]
    if acc is not None:
        in_specs.append(ANY)
        args.append(acc)
    return pl.pallas_call(
        body,
        grid_spec=pltpu.PrefetchScalarGridSpec(
            num_scalar_prefetch=1, grid=(nr, nc), in_specs=in_specs,
            out_specs=pl.BlockSpec((None, tr, tc), out_map)),
        out_shape=jax.ShapeDtypeStruct(shape, F32),
        input_output_aliases={5: 0} if acc is not None else {},
        compiler_params=_cparams(("parallel", "parallel")),
        name=name,
    )(*args)


def _join_halves(name, grads, dims):
    n = len(grads)

    def plan(ins, outs):
        (x, y, c), _, _, _ = _place()
        remote = []
        for w in range(n):
            d, h = dims[w], grads[w].shape[dims[w]] // 2
            mine = _window(outs[w], d, c, h)
            remote.append((mine, mine, (x, y, 1 - c), _window(outs[w], d, 1 - c, h)))
        return [], remote

    return _comm_call(name, grads, [jax.ShapeDtypeStruct(g.shape, g.dtype) for g in grads], plan, 0, n,
                      aliases={w: w for w in range(n)})


def _allreduce_small(pack):
    R, C = pack.shape

    def gather_body(in_ref, slots_ref, send_sems, recv_sems, local_sem):
        x, y, c = lax.axis_index("x"), lax.axis_index("y"), lax.axis_index("c")
        me = 4 * x + 2 * y + c
        flips = [(dx, dy, dc) for dx in (0, 1) for dy in (0, 1) for dc in (0, 1)][1:]

        def flip(v, d):
            return 1 - v if d else v

        local = pltpu.make_async_copy(in_ref, slots_ref.at[me], local_sem)
        local.start()
        sends = []
        for j, (dx, dy, dc) in enumerate(flips):
            px, py, pc = flip(x, dx), flip(y, dy), flip(c, dc)
            sends.append((pltpu.make_async_remote_copy(
                src_ref=in_ref, dst_ref=slots_ref.at[me], send_sem=send_sems.at[j], recv_sem=recv_sems.at[j],
                device_id=(px, py, pc), device_id_type=MESH), 4 * px + 2 * py + pc, j))
        for cp, _, _ in sends:
            cp.start()
        for cp, peer_id, j in sends:
            pltpu.make_async_remote_copy(
                src_ref=in_ref, dst_ref=slots_ref.at[peer_id], send_sem=send_sems.at[j], recv_sem=recv_sems.at[j],
                device_id=(x, y, c), device_id_type=MESH).wait_recv()
        for cp, _, _ in sends:
            cp.wait_send()
        local.wait()

    slots = pl.pallas_call(
        gather_body,
        in_specs=[ANY],
        out_specs=ANY,
        out_shape=jax.ShapeDtypeStruct((N_DEV, R, C), pack.dtype),
        scratch_shapes=[pltpu.SemaphoreType.DMA((N_DEV - 1,)), pltpu.SemaphoreType.DMA((N_DEV - 1,)),
                        pltpu.SemaphoreType.DMA],
        name="allgather_small",
    )(pack)

    def sum_body(s_ref, o_ref):
        acc = s_ref[0]
        for d in range(1, N_DEV):
            acc = acc + s_ref[d]
        o_ref[...] = acc

    return pl.pallas_call(
        sum_body,
        out_shape=jax.ShapeDtypeStruct((R, C), pack.dtype),
        name="sum_small",
    )(slots)


def _adamw_math(w, g, m, v):
    m2 = ADAM_B1 * m + (1.0 - ADAM_B1) * g
    v2 = ADAM_B2 * v + (1.0 - ADAM_B2) * (g * g)
    m_hat = m2 / (1.0 - ADAM_B1 ** ADAM_STEP)
    v_hat = v2 / (1.0 - ADAM_B2 ** ADAM_STEP)
    delta = -ADAM_LR * (m_hat / (jnp.sqrt(v_hat) + ADAM_EPS) + ADAM_WD * w)
    return delta, m2, v2


def _adamw_big(name, w, m, v, g_parts):
    shape = w.shape
    C = shape[-1]
    R = w.size // C
    tr, tc = _div_tile(R, 256, 8), _div_tile(C, 1024)
    n_g = len(g_parts)

    def body(*refs):
        w_ref, m_ref, v_ref = refs[:3]
        g_ref, d_ref, nm_ref, nv_ref = refs[3 + n_g:]
        g = refs[3][...]
        for extra in refs[4:3 + n_g]:
            g = g + extra[...]
        delta, m2, v2 = _adamw_math(w_ref[...], g, m_ref[...], v_ref[...])
        g_ref[...], d_ref[...], nm_ref[...], nv_ref[...] = g, delta, m2, v2

    blk = pl.BlockSpec((tr, tc), lambda i, j: (i, j))
    outs = pl.pallas_call(
        body,
        grid=(R // tr, C // tc),
        in_specs=[blk] * (3 + n_g),
        out_specs=[blk] * 4,
        out_shape=[jax.ShapeDtypeStruct((R, C), F32)] * 4,
        compiler_params=_cparams(("parallel", "parallel")),
        name=name,
    )(w.reshape(R, C), m.reshape(R, C), v.reshape(R, C), *[g.reshape(R, C) for g in g_parts])
    return [o.reshape(shape) for o in outs]


def _adamw_small(name, w, g, m, v):
    shape = w.shape
    two_d = (w.size // shape[-1], shape[-1])

    def body(w_ref, g_ref, m_ref, v_ref, d_ref, nm_ref, nv_ref):
        d_ref[...], nm_ref[...], nv_ref[...] = _adamw_math(w_ref[...], g_ref[...], m_ref[...], v_ref[...])

    outs = pl.pallas_call(
        body,
        out_shape=[jax.ShapeDtypeStruct(two_d, F32)] * 3,
        name=name,
    )(w.reshape(two_d), g.reshape(two_d), m.reshape(two_d), v.reshape(two_d))
    return [o.reshape(shape) for o in outs]


SMALL_ROWS = 40
S_BIN, S_G1, S_G2 = 16, 24, 32
MATS = ("w_in", "w_pa", "w_pb", "w_o", "w_mlp1", "w_mlp2")
LRU = ("lru_wr", "lru_wi")
BIG_DIM = dict(w_in=2, w_pa=1, w_pb=1, w_o=1, w_mlp1=2, w_mlp2=1, lru_wr=2, lru_wi=2)
WEIGHTS = ("norm1_g", "w_in", "b_in", "conv_a_w", "conv_a_b", "lru_wr", "lru_br", "lru_wi", "lru_bi", "lru_lam",
           "conv_b_w", "w_pa", "w_pb", "w_o", "norm2_g", "w_mlp1", "w_mlp2", "final_g")


def _rows_at(a, r0, total):
    pad = [(0, 0)] * a.ndim
    pad[-2] = (r0, total - r0 - a.shape[-2])
    return jnp.pad(a, pad)


def kernel(x, norm1_g, w_in, b_in, conv_a_w, conv_a_b, lru_wr, lru_br, lru_wi, lru_bi, lru_lam, conv_b_w, w_pa, w_pb, w_o, norm2_g, w_mlp1, w_mlp2, final_g, loss_target, m_norm1_g, m_w_in, m_b_in, m_conv_a_w, m_conv_a_b, m_lru_wr, m_lru_br, m_lru_wi, m_lru_bi, m_lru_lam, m_conv_b_w, m_w_pa, m_w_pb, m_w_o, m_norm2_g, m_w_mlp1, m_w_mlp2, m_final_g, v_norm1_g, v_w_in, v_b_in, v_conv_a_w, v_conv_a_b, v_lru_wr, v_lru_br, v_lru_wi, v_lru_bi, v_lru_lam, v_conv_b_w, v_w_pa, v_w_pb, v_w_o, v_norm2_g, v_w_mlp1, v_w_mlp2, v_final_g):
    wts = dict(norm1_g=norm1_g, w_in=w_in, b_in=b_in, conv_a_w=conv_a_w, conv_a_b=conv_a_b, lru_wr=lru_wr,
               lru_br=lru_br, lru_wi=lru_wi, lru_bi=lru_bi, lru_lam=lru_lam, conv_b_w=conv_b_w, w_pa=w_pa,
               w_pb=w_pb, w_o=w_o, norm2_g=norm2_g, w_mlp1=w_mlp1, w_mlp2=w_mlp2, final_g=final_g)
    mom = dict(norm1_g=m_norm1_g, w_in=m_w_in, b_in=m_b_in, conv_a_w=m_conv_a_w, conv_a_b=m_conv_a_b,
               lru_wr=m_lru_wr, lru_br=m_lru_br, lru_wi=m_lru_wi, lru_bi=m_lru_bi, lru_lam=m_lru_lam,
               conv_b_w=m_conv_b_w, w_pa=m_w_pa, w_pb=m_w_pb, w_o=m_w_o, norm2_g=m_norm2_g, w_mlp1=m_w_mlp1,
               w_mlp2=m_w_mlp2, final_g=m_final_g)
    vel = dict(norm1_g=v_norm1_g, w_in=v_w_in, b_in=v_b_in, conv_a_w=v_conv_a_w, conv_a_b=v_conv_a_b,
               lru_wr=v_lru_wr, lru_br=v_lru_br, lru_wi=v_lru_wi, lru_bi=v_lru_bi, lru_lam=v_lru_lam,
               conv_b_w=v_conv_b_w, w_pa=v_w_pa, w_pb=v_w_pb, w_o=v_w_o, norm2_g=v_norm2_g, w_mlp1=v_w_mlp1,
               w_mlp2=v_w_mlp2, final_g=v_final_g)
    depth, D = norm1_g.shape
    nb, bw = lru_wr.shape[1], lru_wr.shape[3]
    chip = 2 * lax.axis_index("x") + lax.axis_index("y")

    small_parts = [conv_a_w.reshape(-1), conv_b_w.reshape(-1), lru_br.reshape(-1), lru_bi.reshape(-1)]
    small_len = sum(p.shape[0] for p in small_parts)
    small_rows = -(-small_len // 1024) * 8
    small = jnp.concatenate(small_parts + [jnp.zeros((small_rows * 128 - small_len,), F32)]).reshape(small_rows, 128)
    gathered = _gather_weights([wts[n].astype(BF16) for n in LRU], [BIG_DIM[n] for n in LRU], small)
    full = dict(zip(LRU, gathered[:-1]))
    items = [(n, l) for l in range(depth) for n in MATS]
    mat_dims = [BIG_DIM[n] - 1 for n, _ in items]
    where = jnp.stack([chip, lax.axis_index("c")]).astype(jnp.int32)
    placed = {(n, l): _place_shard(f"place_{n}_{l}", wts[n], l, BIG_DIM[n] - 1, where) for n, l in items}
    flat = gathered[-1].reshape(N_CHIPS, small_rows * 128)
    off = 0
    small_full = []
    for part, shard in zip(small_parts, (conv_a_w, conv_b_w, lru_br, lru_bi)):
        piece = flat[:, off:off + part.shape[0]].reshape((N_CHIPS,) + shard.shape)
        small_full.append(jnp.moveaxis(piece, 0, -2).reshape(shard.shape[:-1] + (N_CHIPS * shard.shape[-1],)))
        off += part.shape[0]
    caw_f, cbw_f, br_f, bi_f = small_full
    pch = (_rows_at(conv_a_b[:, None, :], R_CAB, R_ROWS) + _rows_at(br_f.reshape(depth, 1, D), R_BR, R_ROWS)
           + _rows_at(bi_f.reshape(depth, 1, D), R_BI, R_ROWS) + _rows_at(lru_lam[:, None, :], R_LAM, R_ROWS)
           + _rows_at(caw_f, R_CAW, R_ROWS) + _rows_at(cbw_f, R_CBW, R_ROWS))
    W = dict(b_in=b_in, pch=pch, wr=full["lru_wr"], wi=full["lru_wi"], g1=norm1_g, g2=norm2_g, gf=final_g,
             core=lax.axis_index("c").astype(jnp.int32).reshape(1))

    loss_local, dx, grads, dgf, sums, landed = _local_fwd_bwd(x[0], loss_target[0], W, placed)
    loss = lax.psum(loss_local, ("x", "y", "c"))

    key = dict(w_in="w_in", w_pa="w_pa", w_pb="w_pb", w_o="w_o", w_mlp1="w1", w_mlp2="w2", lru_wr="wr", lru_wi="wi")
    out_g, out_d, out_m, out_v = {}, {}, {}, {}
    per_layer = [[grads[l][key[n]].astype(BF16) for l in range(depth)] for n in LRU]
    land = _scatter_grads(per_layer, [BIG_DIM[n] - 1 for n in LRU])
    chip_sums = [_sum_slots(f"sum_slots_{n}", ld.reshape(N_CHIPS, -1, ld.shape[-1])) for n, ld in zip(LRU, land)]
    sib_sums = _swap_with_sibling(chip_sums)
    for n, mine, sib in zip(LRU, chip_sums, sib_sums):
        out_g[n], out_d[n], out_m[n], out_v[n] = _adamw_big(f"adamw_{n}", wts[n], mom[n], vel[n], [mine, sib])
    acc = {n: None for n in MATS}
    for n, l in reversed(items):
        acc[n] = _reduce_into(f"reduce_{n}_{l}", sums[n, l], landed[n, l], acc[n], l, BIG_DIM[n], wts[n].shape, where)
    joined = _join_halves("join_halves", [acc[n] for n in MATS], [BIG_DIM[n] for n in MATS])
    for n, g in zip(MATS, joined):
        out_g[n], out_d[n], out_m[n], out_v[n] = _adamw_big(f"adamw_{n}", wts[n], mom[n], vel[n], [g])

    rows = []
    for l in range(depth):
        g = grads[l]
        rows.append(_rows_at(g["sm"], 0, SMALL_ROWS) + _rows_at(g["b_in"].reshape(7, D), S_BIN, SMALL_ROWS)
                    + _rows_at(g["g1"], S_G1, SMALL_ROWS) + _rows_at(g["g2"], S_G2, SMALL_ROWS))
    rows.append(_rows_at(dgf, 0, 8))
    tot = _allreduce_small(jnp.concatenate(rows, axis=0))
    per = tot[:depth * SMALL_ROWS].reshape(depth, SMALL_ROWS, D)

    def cols_of_chip(a, axis):
        size = a.shape[axis] // N_CHIPS
        return lax.dynamic_slice_in_dim(a, chip * size, size, axis=axis)

    small_g = dict(
        norm1_g=per[:, S_G1], b_in=per[:, S_BIN:S_BIN + 7].reshape(depth, 7 * D),
        conv_a_w=cols_of_chip(per[:, R_CAW:R_CAW + 4], 2), conv_a_b=per[:, R_CAB],
        lru_br=cols_of_chip(per[:, R_BR].reshape(depth, nb, bw), 2),
        lru_bi=cols_of_chip(per[:, R_BI].reshape(depth, nb, bw), 2), lru_lam=per[:, R_LAM],
        conv_b_w=cols_of_chip(per[:, R_CBW:R_CBW + 3], 2), norm2_g=per[:, S_G2],
        final_g=tot[depth * SMALL_ROWS])
    for n, g in small_g.items():
        out_g[n] = g
        out_d[n], out_m[n], out_v[n] = _adamw_small(f"adamw_{n}", wts[n], g, mom[n], vel[n])

    return (loss, dx[None], *[out_g[n] for n in WEIGHTS], *[out_d[n] for n in WEIGHTS],
            *[out_m[n] for n in WEIGHTS], *[out_v[n] for n in WEIGHTS])
```

```python
import functools

import jax
import jax.numpy as jnp
from jax import lax
from jax.experimental import pallas as pl
from jax.experimental.pallas import tpu as pltpu

F32 = jnp.float32
BF16 = jnp.bfloat16
MESH = pl.DeviceIdType.MESH

EPS = 1e-6
LRU_C = 8.0
ADAM_LR = 0.001
ADAM_B1 = 0.9
ADAM_B2 = 0.999
ADAM_EPS = 1e-08
ADAM_WD = 0.01
ADAM_STEP = 10

N_CHIPS = 4
N_DEV = 8
HALO = 8
VMEM_LIMIT = 56 * 1024 * 1024
MM_TILES = (1024, 1024, 2048)
MM_TILES_FUSED = (512, 1024, 2048)
SEQ_CHUNK = 256
ROW_TILE = 256

R_CAB, R_BR, R_BI, R_LAM, R_CAW, R_CBW, R_ROWS = 0, 1, 2, 3, 4, 8, 16


def _cparams(sem):
    return pltpu.CompilerParams(dimension_semantics=sem, vmem_limit_bytes=VMEM_LIMIT)


def _div_tile(n, pref, unit=128):
    if n <= pref:
        return n
    t = (pref // unit) * unit
    while n % t:
        t -= unit
    return t


def _sigmoid(v):
    return 1.0 / (1.0 + jnp.exp(-v))


def _gelu_and_grad(y):
    k = 0.7978845608028654
    c = 0.044715
    y2 = y * y
    t = jnp.tanh(k * (y + c * y2 * y))
    g = 0.5 * y * (1.0 + t)
    gp = 0.5 * (1.0 + t) + 0.5 * y * (1.0 - t * t) * (k * (1.0 + 3.0 * c * y2))
    return g, gp


def _softplus_neg(lam):
    e = jnp.exp(-jnp.abs(lam))
    w = 1.0 + e
    l1p = jnp.where(w == 1.0, e, jnp.log(w) * e / jnp.where(w == 1.0, 1.0, w - 1.0))
    return jnp.maximum(-lam, 0.0) + l1p


def _mm(name, mode, a, b, M, N, K, out_dtypes, epilogue=None, extras=(), la=None, lb=None, tiles=None,
        carry=None):
    tiles = MM_TILES if tiles is None else tiles
    tm, tn, tk = _div_tile(M, tiles[0]), _div_tile(N, tiles[1]), _div_tile(K, tiles[2])
    assert M % tm == 0 and N % tn == 0 and K % tk == 0, (name, M, N, K)
    nk = K // tk

    def spec(lead, shape, imap):
        if lead is None:
            return pl.BlockSpec(shape, imap)
        return pl.BlockSpec((None,) + shape, lambda i, j, k: (lead,) + imap(i, j, k))

    if mode == "nn":
        a_spec = spec(la, (tm, tk), lambda i, j, k: (i, k))
        b_spec = spec(lb, (tk, tn), lambda i, j, k: (k, j))
        dn = (((1,), (0,)), ((), ()))
    elif mode == "nt":
        a_spec = spec(la, (tm, tk), lambda i, j, k: (i, k))
        b_spec = spec(lb, (tn, tk), lambda i, j, k: (j, k))
        dn = (((1,), (1,)), ((), ()))
    else:
        a_spec = spec(la, (tk, tm), lambda i, j, k: (k, i))
        b_spec = spec(lb, (tk, tn), lambda i, j, k: (k, j))
        dn = (((0,), (0,)), ((), ()))

    ex_arrays, ex_specs = [], []
    for arr, kind, off in extras:
        ex_arrays.append(arr)
        if kind == "bias":
            ex_specs.append(pl.BlockSpec((1, tn), lambda i, j, k: (0, j)))
        else:
            assert off % tn == 0
            ex_specs.append(pl.BlockSpec((tm, tn), lambda i, j, k, o=off // tn: (i, j + o)))
    n_ex, n_out = len(ex_arrays), len(out_dtypes)
    n_cin = len(carry.ins) if carry else 0
    n_cout = len(carry.out_shapes) if carry else 0
    n_in = 2 + n_ex + n_cin
    gi, gj = M // tm, N // tn

    def body(*refs):
        a_ref, b_ref = refs[0], refs[1]
        ex = refs[2:2 + n_ex]
        outs = refs[n_in:n_in + n_out]
        acc = refs[n_in + n_out + n_cout]
        i, j, k = pl.program_id(0), pl.program_id(1), pl.program_id(2)
        if carry:
            c_in, c_out = refs[2 + n_ex:n_in], refs[n_in + n_out:n_in + n_out + n_cout]
            sems = refs[n_in + n_out + n_cout + 1:]

            @pl.when((i == 0) & (j == 0) & (k == 0))
            def _():
                carry.start(c_in, c_out, sems)

        def product():
            return lax.dot_general(a_ref[...], b_ref[...], dn, preferred_element_type=F32)

        def finish(r):
            vals = (r,) if epilogue is None else epilogue(r, *[e[...] for e in ex])
            for o, v in zip(outs, vals):
                o[...] = v.astype(o.dtype)

        if nk == 1:
            finish(product())
        else:
            @pl.when(k == 0)
            def _():
                acc[...] = product()

            @pl.when((k > 0) & (k < nk - 1))
            def _():
                acc[...] += product()

            @pl.when(k == nk - 1)
            def _():
                finish(acc[...] + product())

        if carry:
            @pl.when((i == gi - 1) & (j == gj - 1) & (k == nk - 1))
            def _():
                carry.finish(c_in, c_out, sems)

    res = pl.pallas_call(
        body,
        grid=(gi, gj, nk),
        in_specs=[a_spec, b_spec, *ex_specs] + [ANY] * n_cin,
        out_specs=[pl.BlockSpec((tm, tn), lambda i, j, k: (i, j)) for _ in range(n_out)] + [ANY] * n_cout,
        out_shape=[jax.ShapeDtypeStruct((M, N), d) for d in out_dtypes] + (carry.out_shapes if carry else []),
        scratch_shapes=[pltpu.VMEM((tm, tn) if nk > 1 else (8, 128), F32)] + (carry.scratch() if carry else []),
        input_output_aliases={2 + n_ex + ci: n_out + co for ci, co in carry.aliases.items()} if carry else {},
        compiler_params=_cparams(("arbitrary",) * 3 if carry else ("parallel", "parallel", "arbitrary")),
        name=name,
    )(a, b, *ex_arrays, *(carry.ins if carry else []))
    main = res[0] if n_out == 1 else res[:n_out]
    return (main, res[n_out:]) if carry else main


def _rms_fwd(name, x, g_row):
    T, D = x.shape
    tm = min(ROW_TILE, T)

    def body(x_ref, g_ref, h_ref):
        xv = x_ref[...]
        r = lax.rsqrt(jnp.mean(xv * xv, axis=-1, keepdims=True) + EPS)
        h_ref[...] = (xv * r * g_ref[...]).astype(BF16)

    return pl.pallas_call(
        body,
        grid=(T // tm,),
        in_specs=[pl.BlockSpec((tm, D), lambda i: (i, 0)), pl.BlockSpec((1, D), lambda i: (0, 0))],
        out_specs=pl.BlockSpec((tm, D), lambda i: (i, 0)),
        out_shape=jax.ShapeDtypeStruct((T, D), BF16),
        compiler_params=_cparams(("parallel",)),
        name=name,
    )(x, g_row)


def _rms_bwd(name, x, g_row, dh, dres):
    T, D = x.shape
    tm = min(ROW_TILE, T)

    def body(x_ref, g_ref, dh_ref, dres_ref, dx_ref, dxb_ref, dg_ref):
        xv, dhv = x_ref[...], dh_ref[...]
        r = lax.rsqrt(jnp.mean(xv * xv, axis=-1, keepdims=True) + EPS)
        gd = g_ref[...] * dhv
        c = jnp.mean(xv * gd, axis=-1, keepdims=True)
        dx = r * gd - xv * (r * r * r) * c + dres_ref[...]
        dx_ref[...] = dx
        dxb_ref[...] = dx.astype(BF16)

        @pl.when(pl.program_id(0) == 0)
        def _():
            dg_ref[...] = jnp.zeros_like(dg_ref)

        dg_ref[...] += jnp.sum(dhv * xv * r, axis=0, keepdims=True)

    row = pl.BlockSpec((tm, D), lambda i: (i, 0))
    vec = pl.BlockSpec((1, D), lambda i: (0, 0))
    return pl.pallas_call(
        body,
        grid=(T // tm,),
        in_specs=[row, vec, row, row],
        out_specs=[row, row, vec],
        out_shape=[jax.ShapeDtypeStruct((T, D), F32), jax.ShapeDtypeStruct((T, D), BF16),
                   jax.ShapeDtypeStruct((1, D), F32)],
        compiler_params=_cparams(("arbitrary",)),
        name=name,
    )(x, g_row, dh, dres)


def _loss_head(name, x, g_row, tgt):
    T, D = x.shape
    tm = min(ROW_TILE, T)

    def body(x_ref, g_ref, t_ref, dx_ref, dxb_ref, dg_ref, loss_ref):
        xv, g = x_ref[...], g_ref[...]
        r = lax.rsqrt(jnp.mean(xv * xv, axis=-1, keepdims=True) + EPS)
        xh = xv * r
        e = xh * g - t_ref[...]
        lpart = 0.5 * jnp.sum(jnp.mean(e * e, axis=-1, keepdims=True))
        dy = e * (1.0 / D)
        gd = g * dy
        c = jnp.mean(xv * gd, axis=-1, keepdims=True)
        dx = r * gd - xv * (r * r * r) * c
        dx_ref[...] = dx
        dxb_ref[...] = dx.astype(BF16)

        @pl.when(pl.program_id(0) == 0)
        def _():
            dg_ref[...] = jnp.zeros_like(dg_ref)
            loss_ref[...] = jnp.zeros_like(loss_ref)

        dg_ref[...] += jnp.sum(dy * xh, axis=0, keepdims=True)
        loss_ref[...] += jnp.full(loss_ref.shape, lpart, F32)

    row = pl.BlockSpec((tm, D), lambda i: (i, 0))
    vec = pl.BlockSpec((1, D), lambda i: (0, 0))
    return pl.pallas_call(
        body,
        grid=(T // tm,),
        in_specs=[row, vec, row],
        out_specs=[row, row, vec, pl.BlockSpec((8, 128), lambda i: (0, 0))],
        out_shape=[jax.ShapeDtypeStruct((T, D), F32), jax.ShapeDtypeStruct((T, D), BF16),
                   jax.ShapeDtypeStruct((1, D), F32), jax.ShapeDtypeStruct((8, 128), F32)],
        compiler_params=_cparams(("arbitrary",)),
        name=name,
    )(x, g_row, tgt)


def _colsum(name, a):
    T, N = a.shape
    tm, tn = min(512, T), _div_tile(N, 2048)

    def body(a_ref, o_ref):
        @pl.when(pl.program_id(1) == 0)
        def _():
            o_ref[...] = jnp.zeros_like(o_ref)

        o_ref[...] += jnp.sum(a_ref[...].astype(F32), axis=0, keepdims=True)

    return pl.pallas_call(
        body,
        grid=(N // tn, T // tm),
        in_specs=[pl.BlockSpec((tm, tn), lambda j, i: (i, j))],
        out_specs=pl.BlockSpec((1, tn), lambda j, i: (0, j)),
        out_shape=jax.ShapeDtypeStruct((1, N), F32),
        compiler_params=_cparams(("parallel", "arbitrary")),
        name=name,
    )(a)


def _tile_scan(a, b, row, reverse):
    for s in (1, 2, 4):
        if reverse:
            a_s, b_s, m = pltpu.roll(a, 8 - s, 0), pltpu.roll(b, 8 - s, 0), row < 8 - s
        else:
            a_s, b_s, m = pltpu.roll(a, s, 0), pltpu.roll(b, s, 0), row >= s
        b = jnp.where(m, a * b_s + b, b)
        a = jnp.where(m, a * a_s, a)
    return a, b


def _chunk_scan(a_s, b_s, out_ref, carry, n_tiles, width, reverse):
    row = lax.broadcasted_iota(jnp.int32, (8, width), 0)
    edge = 0 if reverse else 7

    def step(j, c):
        jj = (n_tiles - 1 - j) if reverse else j
        o = pl.multiple_of(jj * 8, 8)
        ca, cb = _tile_scan(a_s[pl.ds(o, 8), :], b_s[pl.ds(o, 8), :], row, reverse)
        h = ca * c + cb
        out_ref[pl.ds(o, 8), :] = h
        return jnp.broadcast_to(h[edge:edge + 1, :], (8, width))

    carry[...] = lax.fori_loop(0, n_tiles, step, carry[...])


def _gates(xc, p_ref, wr_ref, wi_ref):
    xcb = xc.astype(BF16)
    r = _sigmoid(jnp.dot(xcb, wr_ref[...], preferred_element_type=F32) + p_ref[R_BR:R_BR + 1, :])
    ig = _sigmoid(jnp.dot(xcb, wi_ref[...], preferred_element_type=F32) + p_ref[R_BI:R_BI + 1, :])
    sp = _softplus_neg(p_ref[R_LAM:R_LAM + 1, :])
    log_a = (-LRU_C) * r * sp
    a = jnp.exp(log_a)
    t = jnp.tanh(log_a)
    mult = jnp.sqrt(-2.0 * t / (1.0 - t))
    return xcb, r, ig, sp, a, mult


def _mixer_specs(Tc, bw, nb, layer):
    def seg(s):
        return pl.BlockSpec((Tc, bw), lambda n, i: (i, s * nb + n))

    p_spec = pl.BlockSpec((None, R_ROWS, bw), lambda n, i: (layer, 0, n))
    w_spec = pl.BlockSpec((None, None, bw, bw), lambda n, i: (layer, n, 0, 0))
    return seg, p_spec, w_spec


def _mixer_fwd(name, layer, z, pch, wr, wi, comm=None):
    T, D = z.shape[0], z.shape[1] // 7
    bw, nb = wr.shape[-1], wr.shape[1]
    Tc = min(SEQ_CHUNK, T)
    nT = T // Tc
    n_cin = len(comm.ins) if comm else 0
    n_cout = len(comm.out_shapes) if comm else 0

    def body(*refs):
        xa_ref, ya_ref, cb_ref, cc_ref, cx_ref, p_ref, wr_ref, wi_ref = refs[:8]
        pa_ref, pb_ref, xc_ref, hl_ref, vb_ref = refs[8 + n_cin:13 + n_cin]
        xa_buf, u_buf, a_s, b_s, carry = refs[13 + n_cin + n_cout:18 + n_cin + n_cout]
        if comm:
            c_in, c_out, sems = refs[8:8 + n_cin], refs[13 + n_cin:13 + n_cin + n_cout], refs[18 + n_cin + n_cout:]

            @pl.when((pl.program_id(0) == 0) & (pl.program_id(1) == 0))
            def _():
                comm.start(c_in, c_out, sems)

        @pl.when(pl.program_id(1) == 0)
        def _():
            xa_buf[0:HALO, :] = jnp.zeros((HALO, bw), F32)
            u_buf[0:HALO, :] = jnp.zeros((HALO, bw), F32)
            carry[...] = jnp.zeros_like(carry)

        xa_buf[HALO:HALO + Tc, :] = xa_ref[...]
        xc = p_ref[R_CAB:R_CAB + 1, :]
        for k in range(4):
            xc = xc + p_ref[R_CAW + k:R_CAW + k + 1, :] * xa_buf[HALO - 3 + k:HALO - 3 + k + Tc, :]
        xc_ref[...] = xc
        _, _, ig, _, a, mult = _gates(xc, p_ref, wr_ref, wi_ref)
        a_s[...] = a
        b_s[...] = mult * (ig * xc)
        _chunk_scan(a_s, b_s, hl_ref, carry, Tc // 8, bw, False)
        g, _ = _gelu_and_grad(ya_ref[...])
        pa_ref[...] = (hl_ref[...] * g).astype(BF16)

        u_buf[HALO:HALO + Tc, :] = cc_ref[...] * cx_ref[...]
        vb = jnp.zeros((Tc, bw), F32)
        for k in range(3):
            vb = vb + p_ref[R_CBW + k:R_CBW + k + 1, :] * u_buf[HALO - 2 + k:HALO - 2 + k + Tc, :]
        vb_ref[...] = vb
        pb_ref[...] = (cb_ref[...] * vb).astype(BF16)
        xa_buf[0:HALO, :] = xa_buf[Tc:Tc + HALO, :]
        u_buf[0:HALO, :] = u_buf[Tc:Tc + HALO, :]

        if comm:
            @pl.when((pl.program_id(0) == nb - 1) & (pl.program_id(1) == nT - 1))
            def _():
                comm.finish(c_in, c_out, sems)

    seg, p_spec, w_spec = _mixer_specs(Tc, bw, nb, layer)
    out = pl.BlockSpec((Tc, bw), lambda n, i: (i, n))
    res = pl.pallas_call(
        body,
        grid=(nb, nT),
        in_specs=[seg(0), seg(1), seg(2), seg(3), seg(4), p_spec, w_spec, w_spec] + [ANY] * n_cin,
        out_specs=[out] * 5 + [ANY] * n_cout,
        out_shape=[jax.ShapeDtypeStruct((T, D), BF16), jax.ShapeDtypeStruct((T, D), BF16),
                   jax.ShapeDtypeStruct((T, D), F32), jax.ShapeDtypeStruct((T, D), F32),
                   jax.ShapeDtypeStruct((T, D), F32)] + (comm.out_shapes if comm else []),
        scratch_shapes=[pltpu.VMEM((Tc + HALO, bw), F32), pltpu.VMEM((Tc + HALO, bw), F32),
                        pltpu.VMEM((Tc, bw), F32), pltpu.VMEM((Tc, bw), F32), pltpu.VMEM((8, bw), F32)]
        + (comm.scratch() if comm else []),
        input_output_aliases={8 + ci: 5 + co for ci, co in comm.aliases.items()} if comm else {},
        compiler_params=_cparams(("arbitrary", "arbitrary") if comm else ("parallel", "arbitrary")),
        name=name,
    )(z, z, z, z, z, pch, wr, wi, *(comm.ins if comm else []))
    return (res[:5], res[5:]) if comm else res


def _mixer_bwd(name, layer, z, xc, hl, vb, dpa, dpb, pch, wr, wi, comm=None):
    T, D = z.shape[0], z.shape[1] // 7
    bw, nb = wr.shape[-1], wr.shape[1]
    Tc = min(SEQ_CHUNK, T)
    nT = T // Tc
    tpc = Tc // 8
    n_cin = len(comm.ins) if comm else 0
    n_cout = len(comm.out_shapes) if comm else 0

    def body(*refs):
        (xa_ref, ya_ref, cb_ref, cc_ref, cx_ref, xc_ref, hl_ref, hp_ref, vb_ref, dpa_ref, dpb_ref,
         p_ref, wr_ref, wi_ref) = refs[:14]
        dxa_ref, dya_ref, dcb_ref, dcc_ref, dcx_ref, dwr_ref, dwi_ref, sm_ref = refs[14 + n_cin:22 + n_cin]
        h_buf, a_buf, dxc_buf, dvb_buf, a_s, d_s, lam_s, carry = refs[22 + n_cin + n_cout:30 + n_cin + n_cout]
        i = pl.program_id(1)
        if comm:
            c_in, c_out = refs[14:14 + n_cin], refs[22 + n_cin:22 + n_cin + n_cout]
            sems = refs[30 + n_cin + n_cout:]

            @pl.when((pl.program_id(0) == 0) & (i == 0))
            def _():
                comm.start(c_in, c_out, sems)

        @pl.when(i == 0)
        def _():
            a_buf[Tc:Tc + HALO, :] = jnp.zeros((HALO, bw), F32)
            dxc_buf[Tc:Tc + HALO, :] = jnp.zeros((HALO, bw), F32)
            dvb_buf[Tc:Tc + HALO, :] = jnp.zeros((HALO, bw), F32)
            carry[...] = jnp.zeros_like(carry)
            dwr_ref[...] = jnp.zeros_like(dwr_ref)
            dwi_ref[...] = jnp.zeros_like(dwi_ref)
            sm_ref[...] = jnp.zeros_like(sm_ref)

        xcv = xc_ref[...]
        xcb, r, ig, sp, a, mult = _gates(xcv, p_ref, wr_ref, wi_ref)
        g, gp = _gelu_and_grad(ya_ref[...])
        hlv, dpav = hl_ref[...], dpa_ref[...]
        dya_ref[...] = (dpav * hlv * gp).astype(BF16)

        a_buf[0:Tc, :] = a
        a_s[...] = a_buf[1:Tc + 1, :]
        d_s[...] = dpav * g
        _chunk_scan(a_s, d_s, lam_s, carry, tpc, bw, True)
        lamv = lam_s[...]

        h_buf[HALO:HALO + Tc, :] = hlv
        h_buf[0:HALO, :] = jnp.where(i == nT - 1, 0.0, hp_ref[...])
        da = lamv * h_buf[HALO - 1:HALO - 1 + Tc, :]
        dmult = lamv * (ig * xcv)
        dbx = lamv * mult
        dig = dbx * xcv
        dxc = dbx * ig
        dlog_a = da * a - dmult * (a * a) / mult
        dpr = (dlog_a * ((-LRU_C) * sp)) * r * (1.0 - r)
        dpi = dig * ig * (1.0 - ig)
        dprb, dpib = dpr.astype(BF16), dpi.astype(BF16)
        nt = (((1,), (1,)), ((), ()))
        tn = (((0,), (0,)), ((), ()))
        dxc = dxc + lax.dot_general(dprb, wr_ref[...], nt, preferred_element_type=F32)
        dxc = dxc + lax.dot_general(dpib, wi_ref[...], nt, preferred_element_type=F32)
        dwr_ref[...] += lax.dot_general(xcb, dprb, tn, preferred_element_type=F32)
        dwi_ref[...] += lax.dot_general(xcb, dpib, tn, preferred_element_type=F32)

        def rowsum(v):
            return jnp.sum(v, axis=0, keepdims=True)

        sm_ref[R_CAB:R_CAB + 1, :] += rowsum(dxc)
        sm_ref[R_BR:R_BR + 1, :] += rowsum(dpr)
        sm_ref[R_BI:R_BI + 1, :] += rowsum(dpi)
        sm_ref[R_LAM:R_LAM + 1, :] += rowsum(dlog_a * ((-LRU_C) * r))

        dxc_buf[0:Tc, :] = dxc
        xav = xa_ref[...]
        dxa = jnp.zeros((Tc, bw), F32)
        for k in range(4):
            sh = dxc_buf[3 - k:3 - k + Tc, :]
            dxa = dxa + p_ref[R_CAW + k:R_CAW + k + 1, :] * sh
            sm_ref[R_CAW + k:R_CAW + k + 1, :] += rowsum(xav * sh)
        dxa_ref[...] = dxa.astype(BF16)

        dpbv, cbv, ccv, cxv = dpb_ref[...], cb_ref[...], cc_ref[...], cx_ref[...]
        dcb_ref[...] = (dpbv * vb_ref[...]).astype(BF16)
        dvb_buf[0:Tc, :] = dpbv * cbv
        u = ccv * cxv
        du = jnp.zeros((Tc, bw), F32)
        for k in range(3):
            sh = dvb_buf[2 - k:2 - k + Tc, :]
            du = du + p_ref[R_CBW + k:R_CBW + k + 1, :] * sh
            sm_ref[R_CBW + k:R_CBW + k + 1, :] += rowsum(u * sh)
        dcc_ref[...] = (du * cxv).astype(BF16)
        dcx_ref[...] = (du * ccv).astype(BF16)

        a_buf[Tc:Tc + HALO, :] = a_buf[0:HALO, :]
        dxc_buf[Tc:Tc + HALO, :] = dxc_buf[0:HALO, :]
        dvb_buf[Tc:Tc + HALO, :] = dvb_buf[0:HALO, :]

        @pl.when(i == nT - 1)
        def _():
            sm_ref[R_LAM:R_LAM + 1, :] = sm_ref[R_LAM:R_LAM + 1, :] * (-_sigmoid(-p_ref[R_LAM:R_LAM + 1, :]))

        if comm:
            @pl.when((pl.program_id(0) == nb - 1) & (i == nT - 1))
            def _():
                comm.finish(c_in, c_out, sems)

    def seg(s):
        return pl.BlockSpec((Tc, bw), lambda n, i: (nT - 1 - i, s * nb + n))

    blk = pl.BlockSpec((Tc, bw), lambda n, i: (nT - 1 - i, n))
    halo = pl.BlockSpec((8, bw), lambda n, i: (jnp.maximum((nT - 1 - i) * tpc - 1, 0), n))
    p_spec = pl.BlockSpec((None, R_ROWS, bw), lambda n, i: (layer, 0, n))
    w_spec = pl.BlockSpec((None, None, bw, bw), lambda n, i: (layer, n, 0, 0))
    dw_spec = pl.BlockSpec((None, bw, bw), lambda n, i: (n, 0, 0))
    act = jax.ShapeDtypeStruct((T, D), BF16)
    res = pl.pallas_call(
        body,
        grid=(nb, nT),
        in_specs=[seg(0), seg(1), seg(2), seg(3), seg(4), blk, blk, halo, blk, blk, blk, p_spec, w_spec, w_spec]
        + [ANY] * n_cin,
        out_specs=[blk] * 5 + [dw_spec, dw_spec, pl.BlockSpec((R_ROWS, bw), lambda n, i: (0, n))] + [ANY] * n_cout,
        out_shape=[act] * 5 + [jax.ShapeDtypeStruct((nb, bw, bw), F32), jax.ShapeDtypeStruct((nb, bw, bw), F32),
                               jax.ShapeDtypeStruct((R_ROWS, D), F32)] + (comm.out_shapes if comm else []),
        scratch_shapes=[pltpu.VMEM((Tc + HALO, bw), F32)] * 4 + [pltpu.VMEM((Tc, bw), F32)] * 3
        + [pltpu.VMEM((8, bw), F32)] + (comm.scratch() if comm else []),
        input_output_aliases={14 + ci: 8 + co for ci, co in comm.aliases.items()} if comm else {},
        compiler_params=_cparams(("arbitrary", "arbitrary") if comm else ("parallel", "arbitrary")),
        name=name,
    )(z, z, z, z, z, xc, hl, hl, vb, dpa, dpb, pch, wr, wi, *(comm.ins if comm else []))
    return (res[:8], res[8:]) if comm else res


def _local_fwd_bwd(x, tgt, W, placed=None):
    T, D = x.shape
    g1, g2 = W["g1"], W["g2"]
    depth = g1.shape[0]
    FF = 4 * D
    if placed is None:
        mats = {(n, l): W[n][l] for n in MATS for l in range(depth)}
    else:
        mats = {}
        mats["w_in", 0], = _run_carry("gather_first", _gather_carry([placed["w_in", 0]], [(0, 1, 0, 1)]))

    def gathering(specs):
        if placed is None or not specs:
            return None, []
        keys = [(n, l) for n, l, _, _, _ in specs]
        arrays = [mats.get(k, placed[k]) for k in keys]
        return _gather_carry(arrays, [(i, d, part, nparts) for i, (_, _, d, part, nparts) in enumerate(specs)]), keys

    def hosted(call, specs, **kw):
        carry, keys = gathering(specs)
        if carry is None:
            return call(**kw)
        res, got = call(**kw, **{("comm" if call.func is _mixer_fwd else "carry"): carry})
        mats.update(zip(keys, got))
        return res

    saved = []
    xs = x
    for l in range(depth):
        h = _rms_fwd(f"rms1_fwd_{l}", xs, g1[l][None])
        z = hosted(functools.partial(_mm, f"in_proj_{l}", "nn", h, mats["w_in", l], T, 7 * D, D, [F32]),
                   [("w_pa", l, 0, 0, 1), ("w_pb", l, 0, 0, 1), ("w_o", l, 0, 0, 1), ("w_mlp1", l, 1, 0, 1)],
                   epilogue=lambda acc, b: (acc + b,), extras=[(W["b_in"][l][None], "bias", 0)])
        pa, pb, xc, hl, vb = hosted(
            functools.partial(_mixer_fwd, f"mixer_fwd_{l}", l, z, W["pch"], W["wr"], W["wi"]),
            [("w_mlp2", l, 0, 0, 1)])
        oa = _mm(f"proj_a_{l}", "nn", pa, mats["w_pa", l], T, D, D, [F32])

        def merge(acc, oav, ga, gb):
            return acc, _sigmoid(ga) * oav + _sigmoid(gb) * acc

        ob, mg = _mm(f"proj_b_merge_{l}", "nn", pb, mats["w_pb", l], T, D, D, [F32, BF16], epilogue=merge,
                     tiles=MM_TILES_FUSED,
                     extras=[(oa, "tile", 0), (z, "tile", 5 * D), (z, "tile", 6 * D)])
        x1 = _mm(f"out_proj_{l}", "nn", mg, mats["w_o", l], T, D, D, [F32],
                 epilogue=lambda acc, res: (res + acc,), extras=[(xs, "tile", 0)])
        h2 = _rms_fwd(f"rms2_fwd_{l}", x1, g2[l][None])

        def relu2(acc):
            pr = jnp.maximum(acc, 0.0)
            return pr * pr, pr

        nxt = l + 1 < depth
        u, pr = hosted(functools.partial(_mm, f"mlp1_{l}", "nn", h2, mats["w_mlp1", l], T, FF, D, [BF16, BF16]),
                       [("w_in", l + 1, 1, 0, 2)] if nxt else [], epilogue=relu2)
        x2 = hosted(functools.partial(_mm, f"mlp2_{l}", "nn", u, mats["w_mlp2", l], T, D, FF, [F32]),
                    [("w_in", l + 1, 1, 1, 2)] if nxt else [],
                    epilogue=lambda acc, res: (res + acc,), extras=[(x1, "tile", 0)])
        saved.append(dict(x0=xs, h=h, z=z, pa=pa, pb=pb, xc=xc, hl=hl, vb=vb, oa=oa, ob=ob, mg=mg, x1=x1,
                          h2=h2, u=u, pr=pr))
        xs = x2

    dx, dxb, dgf, loss_blk = _loss_head("loss_head", xs, W["gf"][None], tgt)

    gmat, got, sums, landed = {}, {}, {}, {}

    def reducing(call, swaps=(), scatters=(), mixer=False, **kw):
        swaps, scatters = [(n, l) for n in swaps], [(n, l) for n in scatters]
        if placed is None:
            return call(**kw)
        cs = _swap_carry([gmat[k] for k in swaps], [BIG_DIM[k[0]] - 1 for k in swaps]) if swaps else None
        cc = _scatter_carry([sums[k] for k in scatters]) if scatters else None
        res, moved = call(**kw, **{("comm" if mixer else "carry"): _merge_carries([cs, cc])})
        got.update(zip(swaps, moved[:len(swaps)]))
        landed.update(zip(scatters, moved[len(swaps):]))
        return res

    def add(names):
        if placed is not None:
            for n in names:
                sums[n, l] = _add_halves(f"add_halves_{n}_{l}", gmat[n, l], got[n, l], BIG_DIM[n] - 1, W["core"])

    grads = [None] * depth
    for l in reversed(range(depth)):
        s = saved[l]
        dp = _mm(f"mlp2_dx_{l}", "nt", dxb, mats["w_mlp2", l], T, FF, D, [BF16],
                 epilogue=lambda acc, prv: (2.0 * prv.astype(F32) * acc,), extras=[(s["pr"], "tile", 0)])
        dw2 = gmat["w_mlp2", l] = _mm(f"mlp2_dw_{l}", "tn", s["u"], dxb, FF, D, T, [BF16])
        dh2 = reducing(functools.partial(_mm, f"mlp1_dx_{l}", "nt", dp, mats["w_mlp1", l], T, D, FF, [F32]),
                       swaps=["w_mlp2"])
        add(["w_mlp2"])
        dw1 = gmat["w_mlp1", l] = reducing(functools.partial(_mm, f"mlp1_dw_{l}", "tn", s["h2"], dp, D, FF, T, [BF16]),
                                           scatters=["w_mlp2"])
        dx1, dx1b, dg2 = _rms_bwd(f"rms2_bwd_{l}", s["x1"], g2[l][None], dh2, dx)

        def unmerge(acc, ga, gb, oav, obv):
            sa, sb = _sigmoid(ga), _sigmoid(gb)
            return acc * sa, acc * sb, acc * oav * sa * (1.0 - sa), acc * obv * sb * (1.0 - sb)

        doa, dob, dga, dgb = reducing(
            functools.partial(_mm, f"out_proj_dx_{l}", "nt", dx1b, mats["w_o", l], T, D, D, [BF16] * 4),
            swaps=["w_mlp1"], tiles=MM_TILES_FUSED, epilogue=unmerge,
            extras=[(s["z"], "tile", 5 * D), (s["z"], "tile", 6 * D), (s["oa"], "tile", 0), (s["ob"], "tile", 0)])
        add(["w_mlp1"])
        dwo = gmat["w_o", l] = _mm(f"out_proj_dw_{l}", "tn", s["mg"], dx1b, D, D, T, [BF16])
        dpa = _mm(f"proj_a_dx_{l}", "nt", doa, mats["w_pa", l], T, D, D, [F32])
        dwpa = gmat["w_pa", l] = _mm(f"proj_a_dw_{l}", "tn", s["pa"], doa, D, D, T, [BF16])
        dpb = _mm(f"proj_b_dx_{l}", "nt", dob, mats["w_pb", l], T, D, D, [F32])
        dwpb = gmat["w_pb", l] = _mm(f"proj_b_dw_{l}", "tn", s["pb"], dob, D, D, T, [BF16])
        dxa, dya, dcb, dcc, dcx, dwr, dwi, sm = reducing(
            functools.partial(_mixer_bwd, f"mixer_bwd_{l}", l, s["z"], s["xc"], s["hl"], s["vb"], dpa, dpb,
                              W["pch"], W["wr"], W["wi"]),
            swaps=["w_o", "w_pa", "w_pb"], scatters=["w_mlp1"], mixer=True)
        add(["w_o", "w_pa", "w_pb"])
        dz = jnp.concatenate([dxa, dya, dcb, dcc, dcx, dga, dgb], axis=1)
        dbin = _colsum(f"bias_grad_{l}", dz)
        dwin = gmat["w_in", l] = reducing(
            functools.partial(_mm, f"in_proj_dw_{l}", "tn", s["h"], dz, D, 7 * D, T, [BF16]),
            scatters=["w_o", "w_pa", "w_pb"])
        if placed is not None:
            got["w_in", l], = _run_carry(f"swap_w_in_{l}", _swap_carry([dwin], [BIG_DIM["w_in"] - 1]))
        add(["w_in"])
        dh = reducing(functools.partial(_mm, f"in_proj_dx_{l}", "nt", dz, mats["w_in", l], T, D, 7 * D, [F32]),
                      scatters=["w_in"])
        dx, dxb, dg1 = _rms_bwd(f"rms1_bwd_{l}", s["x0"], g1[l][None], dh, dx1)
        grads[l] = dict(w_in=dwin, w_pa=dwpa, w_pb=dwpb, w_o=dwo, w_mlp1=dw1, w_mlp2=dw2, wr=dwr, wi=dwi,
                        sm=sm, b_in=dbin, g1=dg1, g2=dg2)
    return loss_blk[0, 0], dx, grads, dgf, sums, landed


ANY = pl.BlockSpec(memory_space=pl.ANY)


def _place():
    x, y, c = lax.axis_index("x"), lax.axis_index("y"), lax.axis_index("c")
    peers = [(1 - x, y, c), (x, 1 - y, c), (1 - x, 1 - y, c)]
    chips = [2 * (1 - x) + y, 2 * x + (1 - y), 2 * (1 - x) + (1 - y)]
    return (x, y, c), 2 * x + y, peers, chips


def _window(ref, dim, q, size):
    idx = [slice(None)] * len(ref.shape)
    idx[dim] = pl.ds(q * size, size)
    return ref.at[tuple(idx)]


def _gather_weights(shards, dims, small):
    n = len(shards)
    sizes = [s.shape[d] for s, d in zip(shards, dims)]
    full = [jax.ShapeDtypeStruct(s.shape[:d] + (s.shape[d] * N_CHIPS,) + s.shape[d + 1:], s.dtype)
            for s, d in zip(shards, dims)]
    full.append(jax.ShapeDtypeStruct((N_CHIPS,) + small.shape, small.dtype))

    def body(*refs):
        ins, outs = refs[:n + 1], refs[n + 1:2 * n + 2]
        send_sems, recv_sems, local_sems = refs[2 * n + 2:]
        _, k, peers, chips = _place()

        def dst(w, q):
            return outs[w].at[q] if w == n else _window(outs[w], dims[w], q, sizes[w])

        local = [pltpu.make_async_copy(ins[w], dst(w, k), local_sems.at[w]) for w in range(n + 1)]
        for cp in local:
            cp.start()
        sends = []
        for p, peer in enumerate(peers):
            for w in range(n + 1):
                s = p * (n + 1) + w
                sends.append(pltpu.make_async_remote_copy(
                    src_ref=ins[w], dst_ref=dst(w, k), send_sem=send_sems.at[s], recv_sem=recv_sems.at[s],
                    device_id=peer, device_id_type=MESH))
        for cp in sends:
            cp.start()
        for p, peer in enumerate(peers):
            for w in range(n + 1):
                s = p * (n + 1) + w
                pltpu.make_async_remote_copy(
                    src_ref=ins[w], dst_ref=dst(w, chips[p]), send_sem=send_sems.at[s], recv_sem=recv_sems.at[s],
                    device_id=peer, device_id_type=MESH).wait_recv()
        for cp in sends:
            cp.wait_send()
        for cp in local:
            cp.wait()

    return pl.pallas_call(
        body,
        in_specs=[ANY] * (n + 1),
        out_specs=[ANY] * (n + 1),
        out_shape=full,
        scratch_shapes=[pltpu.SemaphoreType.DMA((3 * (n + 1),)), pltpu.SemaphoreType.DMA((3 * (n + 1),)),
                        pltpu.SemaphoreType.DMA((n + 1,))],
        name="gather_weights",
    )(*shards, small)


def _scatter_grads(grads, dims):
    n, depth = len(grads), len(grads[0])
    sizes = [g[0].shape[d] // N_CHIPS for g, d in zip(grads, dims)]
    land = []
    for g, d, sz in zip(grads, dims, sizes):
        shp = g[0].shape
        land.append(jax.ShapeDtypeStruct((N_CHIPS, depth) + shp[:d] + (sz,) + shp[d + 1:], g[0].dtype))

    def body(*refs):
        ins, outs = refs[:n * depth], refs[n * depth:n * depth + n]
        send_sems, recv_sems, local_sems = refs[n * depth + n:]
        _, k, peers, chips = _place()

        def src(w, l, q):
            return _window(ins[w * depth + l], dims[w], q, sizes[w])

        local = [pltpu.make_async_copy(src(w, l, k), outs[w].at[3, l], local_sems.at[w * depth + l])
                 for w in range(n) for l in range(depth)]
        for cp in local:
            cp.start()
        sends = []
        for p, peer in enumerate(peers):
            for w in range(n):
                for l in range(depth):
                    s = (p * n + w) * depth + l
                    sends.append(pltpu.make_async_remote_copy(
                        src_ref=src(w, l, chips[p]), dst_ref=outs[w].at[p, l], send_sem=send_sems.at[s],
                        recv_sem=recv_sems.at[s], device_id=peer, device_id_type=MESH))
        for cp in sends:
            cp.start()
        for cp in sends:
            cp.wait_recv()
        for cp in sends:
            cp.wait_send()
        for cp in local:
            cp.wait()

    flat = [g for gl in grads for g in gl]
    return pl.pallas_call(
        body,
        in_specs=[ANY] * (n * depth),
        out_specs=[ANY] * n,
        out_shape=land,
        scratch_shapes=[pltpu.SemaphoreType.DMA((3 * n * depth,)), pltpu.SemaphoreType.DMA((3 * n * depth,)),
                        pltpu.SemaphoreType.DMA((n * depth,))],
        name="scatter_grads",
    )(*flat)


def _sum_slots(name, land):
    _, R, C = land.shape
    tr, tc = _div_tile(R, 512, 8), _div_tile(C, 1024)

    def body(a_ref, b_ref, c_ref, d_ref, o_ref):
        o_ref[...] = ((d_ref[...].astype(F32) + a_ref[...].astype(F32)) + b_ref[...].astype(F32)) \
            + c_ref[...].astype(F32)

    def slot(q):
        return pl.BlockSpec((None, tr, tc), lambda i, j: (q, i, j))

    return pl.pallas_call(
        body,
        grid=(R // tr, C // tc),
        in_specs=[slot(0), slot(1), slot(2), slot(3)],
        out_specs=pl.BlockSpec((tr, tc), lambda i, j: (i, j)),
        out_shape=jax.ShapeDtypeStruct((R, C), F32),
        compiler_params=_cparams(("parallel", "parallel")),
        name=name,
    )(land, land, land, land)


def _swap_with_sibling(parts):
    n = len(parts)

    def body(*refs):
        ins, outs = refs[:n], refs[n:2 * n]
        send_sems, recv_sems = refs[2 * n:]
        (x, y, c), _, _, _ = _place()
        copies = [pltpu.make_async_remote_copy(
            src_ref=ins[w], dst_ref=outs[w], send_sem=send_sems.at[w], recv_sem=recv_sems.at[w],
            device_id=(x, y, 1 - c), device_id_type=MESH) for w in range(n)]
        for cp in copies:
            cp.start()
        for cp in copies:
            cp.wait()

    return pl.pallas_call(
        body,
        in_specs=[ANY] * n,
        out_specs=[ANY] * n,
        out_shape=[jax.ShapeDtypeStruct(p.shape, p.dtype) for p in parts],
        scratch_shapes=[pltpu.SemaphoreType.DMA((n,)), pltpu.SemaphoreType.DMA((n,))],
        name="swap_with_sibling",
    )(*parts)


class _Carry:
    def __init__(self, ins, out_shapes, plan, n1, plan2=None, n2=0, aliases=None):
        self.ins, self.out_shapes, self.plan, self.n1 = list(ins), list(out_shapes), plan, n1
        self.plan2, self.n2, self.aliases = plan2, n2, dict(aliases or {})

    def scratch(self):
        s = [pltpu.SemaphoreType.DMA((self.n1,)), pltpu.SemaphoreType.DMA((self.n1,))]
        if self.plan2 is not None:
            s += [pltpu.SemaphoreType.DMA((self.n2,)), pltpu.SemaphoreType.DMA((self.n2,))]
        return s

    @staticmethod
    def _copy(src, dst, peer, send_sems, recv_sems, i):
        return pltpu.make_async_remote_copy(src_ref=src, dst_ref=dst, send_sem=send_sems.at[i],
                                            recv_sem=recv_sems.at[i], device_id=peer, device_id_type=MESH)

    def start(self, in_refs, out_refs, sems):
        remote = self.plan(in_refs, out_refs)
        assert len(remote) == self.n1, (len(remote), self.n1)
        for i, (s, d, peer, _) in enumerate(remote):
            self._copy(s, d, peer, sems[0], sems[1], i).start()

    def finish(self, in_refs, out_refs, sems):
        remote = self.plan(in_refs, out_refs)
        for i, (s, _, peer, landing) in enumerate(remote):
            self._copy(s, landing, peer, sems[0], sems[1], i).wait_recv()
        if self.plan2 is not None:
            second = self.plan2(in_refs, out_refs)
            assert len(second) == self.n2, (len(second), self.n2)
            for i, (s, d, peer, _) in enumerate(second):
                self._copy(s, d, peer, sems[2], sems[3], i).start()
            for i, (s, _, peer, landing) in enumerate(second):
                self._copy(s, landing, peer, sems[2], sems[3], i).wait_recv()
            for i, (s, d, peer, _) in enumerate(second):
                self._copy(s, d, peer, sems[2], sems[3], i).wait_send()
        for i, (s, d, peer, _) in enumerate(remote):
            self._copy(s, d, peer, sems[0], sems[1], i).wait_send()


def _run_carry(name, carry):
    n_in, n_out = len(carry.ins), len(carry.out_shapes)

    def body(*refs):
        in_refs, out_refs, sems = refs[:n_in], refs[n_in:n_in + n_out], refs[n_in + n_out:]
        carry.start(in_refs, out_refs, sems)
        carry.finish(in_refs, out_refs, sems)

    return pl.pallas_call(
        body,
        in_specs=[ANY] * n_in,
        out_specs=[ANY] * n_out,
        out_shape=carry.out_shapes,
        scratch_shapes=carry.scratch(),
        input_output_aliases=carry.aliases,
        name=name,
    )(*carry.ins)


def _comm_call(name, ins, out_shapes, plan, n_local, n_remote, aliases=None):
    assert n_local == 0
    return _run_carry(name, _Carry(ins, out_shapes, lambda i, o: plan(i, o)[1], n_remote, aliases=aliases))


def _gather_carry(arrays, items):
    shapes = [a.shape for a in arrays]

    def window(ref, item):
        idx, d, part, nparts = item
        h = shapes[idx][d] // (2 * N_CHIPS)
        rows = shapes[idx][1 - d] // nparts

        def win(j):
            sl = [None, None]
            sl[d] = pl.ds(j * h, h)
            sl[1 - d] = pl.ds(part * rows, rows)
            return ref.at[tuple(sl)]

        return win

    def plan1(ins, outs):
        (_, _, c), k, peers, chips = _place()
        remote = []
        for item in items:
            win = window(outs[item[0]], item)
            for p in range(3):
                remote.append((win(2 * k + c), win(2 * k + c), peers[p], win(2 * chips[p] + c)))
        return remote

    def plan2(ins, outs):
        (x, y, c), _, _, chips = _place()
        remote = []
        for item in items:
            win = window(outs[item[0]], item)
            for p in range(3):
                remote.append((win(2 * chips[p] + c), win(2 * chips[p] + c), (x, y, 1 - c),
                               win(2 * chips[p] + 1 - c)))
        return remote

    n = 3 * len(items)
    return _Carry(arrays, [jax.ShapeDtypeStruct(a.shape, a.dtype) for a in arrays], plan1, n, plan2, n,
                  aliases={i: i for i in range(len(arrays))})


def _place_shard(name, w, layer, dim, chip):
    _, a, b = w.shape
    full = (a * N_CHIPS, b) if dim == 0 else (a, b * N_CHIPS)
    tr, tc = _div_tile(a, 512, 16), _div_tile(b, 2048)
    nr, nc = a // tr, b // tc

    def out_map(i, j, chip_ref):
        return (chip_ref[0] * nr + i, j) if dim == 0 else (i, chip_ref[0] * nc + j)

    def body(chip_ref, w_ref, o_ref):
        o_ref[...] = w_ref[...].astype(o_ref.dtype)

    return pl.pallas_call(
        body,
        grid_spec=pltpu.PrefetchScalarGridSpec(
            num_scalar_prefetch=1, grid=(nr, nc),
            in_specs=[pl.BlockSpec((None, tr, tc), lambda i, j, chip_ref: (layer, i, j))],
            out_specs=pl.BlockSpec((tr, tc), out_map)),
        out_shape=jax.ShapeDtypeStruct(full, BF16),
        compiler_params=_cparams(("parallel", "parallel")),
        name=name,
    )(chip, w)


def _half_shape(shape, dim):
    return shape[:dim] + (shape[dim] // (2 * N_CHIPS),) + shape[dim + 1:]


def _swap_carry(grads, dims):
    shapes = [jax.ShapeDtypeStruct((N_CHIPS,) + _half_shape(g.shape, d), g.dtype) for g, d in zip(grads, dims)]

    def plan(ins, outs):
        (x, y, c), _, _, _ = _place()
        remote = []
        for w, d in enumerate(dims):
            h = grads[w].shape[d] // (2 * N_CHIPS)
            for q in range(N_CHIPS):
                remote.append((_window(ins[w], d, 2 * q + 1 - c, h), outs[w].at[q], (x, y, 1 - c), outs[w].at[q]))
        return remote

    return _Carry(grads, shapes, plan, N_CHIPS * len(grads))


def _scatter_carry(sums):
    def plan(ins, outs):
        _, _, peers, chips = _place()
        remote = []
        for w in range(len(sums)):
            for p in range(3):
                remote.append((ins[w].at[chips[p]], outs[w].at[p], peers[p], outs[w].at[p]))
        return remote

    return _Carry(sums, [jax.ShapeDtypeStruct((3,) + s.shape[1:], s.dtype) for s in sums], plan, 3 * len(sums))


def _merge_carries(carries):
    carries = [c for c in carries if c is not None]
    if len(carries) <= 1:
        return carries[0] if carries else None
    ins = [a for c in carries for a in c.ins]
    outs = [s for c in carries for s in c.out_shapes]

    def plan(in_refs, out_refs):
        remote, i0, o0 = [], 0, 0
        for c in carries:
            remote += c.plan(in_refs[i0:i0 + len(c.ins)], out_refs[o0:o0 + len(c.out_shapes)])
            i0, o0 = i0 + len(c.ins), o0 + len(c.out_shapes)
        return remote

    assert all(c.plan2 is None and not c.aliases for c in carries)
    return _Carry(ins, outs, plan, sum(c.n1 for c in carries))


def _add_halves(name, g, got, dim, core):
    R, C = g.shape
    if dim == 1:
        r, cc = R, C // (2 * N_CHIPS)
    else:
        r, cc = R // (2 * N_CHIPS), C
    tr, tc = _div_tile(r, 512, 16), _div_tile(cc, 1024)
    nr, nc = r // tr, cc // tc

    def g_map(q, i, j, core_ref):
        w = 2 * q + core_ref[0]
        return (i, w * nc + j) if dim == 1 else (w * nr + i, j)

    def body(core_ref, g_ref, got_ref, o_ref):
        o_ref[...] = (g_ref[...].astype(F32) + got_ref[...].astype(F32)).astype(o_ref.dtype)

    slab = pl.BlockSpec((None, tr, tc), lambda q, i, j, core_ref: (q, i, j))
    return pl.pallas_call(
        body,
        grid_spec=pltpu.PrefetchScalarGridSpec(
            num_scalar_prefetch=1, grid=(N_CHIPS, nr, nc),
            in_specs=[pl.BlockSpec((tr, tc), g_map), slab], out_specs=slab),
        out_shape=jax.ShapeDtypeStruct((N_CHIPS, r, cc), g.dtype),
        compiler_params=_cparams(("parallel", "parallel", "parallel")),
        name=name,
    )(core, g, got)


def _reduce_into(name, sums, land, acc, layer, dim, shape, where):
    _, r, cc = sums.shape
    tr, tc = _div_tile(r, 512, 16), _div_tile(cc, 1024)
    nr, nc = r // tr, cc // tc

    def out_map(i, j, s):
        return (layer, s[1] * nr + i, j) if dim == 1 else (layer, i, s[1] * nc + j)

    def body(*refs):
        own, a_ref, b_ref, c_ref, o_ref = refs[1], refs[2], refs[3], refs[4], refs[-1]
        o_ref[...] = ((own[...].astype(F32) + a_ref[...].astype(F32)) + b_ref[...].astype(F32)) \
            + c_ref[...].astype(F32)

    def slot(p):
        return pl.BlockSpec((None, tr, tc), lambda i, j, s: (p, i, j))

    in_specs = [pl.BlockSpec((None, tr, tc), lambda i, j, s: (s[0], i, j)), slot(0), slot(1), slot(2)]
    args = [where, sums, land, land, land]
    if acc is not None:
        in_specs.append(ANY)
        args.append(acc)
    return pl.pallas_call(
        body,
        grid_spec=pltpu.PrefetchScalarGridSpec(
            num_scalar_prefetch=1, grid=(nr, nc), in_specs=in_specs,
            out_specs=pl.BlockSpec((None, tr, tc), out_map)),
        out_shape=jax.ShapeDtypeStruct(shape, F32),
        input_output_aliases={5: 0} if acc is not None else {},
        compiler_params=_cparams(("parallel", "parallel")),
        name=name,
    )(*args)


def _join_halves(name, grads, dims):
    n = len(grads)

    def plan(ins, outs):
        (x, y, c), _, _, _ = _place()
        remote = []
        for w in range(n):
            d, h = dims[w], grads[w].shape[dims[w]] // 2
            mine = _window(outs[w], d, c, h)
            remote.append((mine, mine, (x, y, 1 - c), _window(outs[w], d, 1 - c, h)))
        return [], remote

    return _comm_call(name, grads, [jax.ShapeDtypeStruct(g.shape, g.dtype) for g in grads], plan, 0, n,
                      aliases={w: w for w in range(n)})


def _allreduce_small(pack):
    R, C = pack.shape

    def gather_body(in_ref, slots_ref, send_sems, recv_sems, local_sem):
        x, y, c = lax.axis_index("x"), lax.axis_index("y"), lax.axis_index("c")
        me = 4 * x + 2 * y + c
        flips = [(dx, dy, dc) for dx in (0, 1) for dy in (0, 1) for dc in (0, 1)][1:]

        def flip(v, d):
            return 1 - v if d else v

        local = pltpu.make_async_copy(in_ref, slots_ref.at[me], local_sem)
        local.start()
        sends = []
        for j, (dx, dy, dc) in enumerate(flips):
            px, py, pc = flip(x, dx), flip(y, dy), flip(c, dc)
            sends.append((pltpu.make_async_remote_copy(
                src_ref=in_ref, dst_ref=slots_ref.at[me], send_sem=send_sems.at[j], recv_sem=recv_sems.at[j],
                device_id=(px, py, pc), device_id_type=MESH), 4 * px + 2 * py + pc, j))
        for cp, _, _ in sends:
            cp.start()
        for cp, peer_id, j in sends:
            pltpu.make_async_remote_copy(
                src_ref=in_ref, dst_ref=slots_ref.at[peer_id], send_sem=send_sems.at[j], recv_sem=recv_sems.at[j],
                device_id=(x, y, c), device_id_type=MESH).wait_recv()
        for cp, _, _ in sends:
            cp.wait_send()
        local.wait()

    slots = pl.pallas_call(
        gather_body,
        in_specs=[ANY],
        out_specs=ANY,
        out_shape=jax.ShapeDtypeStruct((N_DEV, R, C), pack.dtype),
        scratch_shapes=[pltpu.SemaphoreType.DMA((N_DEV - 1,)), pltpu.SemaphoreType.DMA((N_DEV - 1,)),
                        pltpu.SemaphoreType.DMA],
        name="allgather_small",
    )(pack)

    def sum_body(s_ref, o_ref):
        acc = s_ref[0]
        for d in range(1, N_DEV):
            acc = acc + s_ref[d]
        o_ref[...] = acc

    return pl.pallas_call(
        sum_body,
        out_shape=jax.ShapeDtypeStruct((R, C), pack.dtype),
        name="sum_small",
    )(slots)


def _adamw_math(w, g, m, v):
    m2 = ADAM_B1 * m + (1.0 - ADAM_B1) * g
    v2 = ADAM_B2 * v + (1.0 - ADAM_B2) * (g * g)
    m_hat = m2 / (1.0 - ADAM_B1 ** ADAM_STEP)
    v_hat = v2 / (1.0 - ADAM_B2 ** ADAM_STEP)
    delta = -ADAM_LR * (m_hat / (jnp.sqrt(v_hat) + ADAM_EPS) + ADAM_WD * w)
    return delta, m2, v2


def _adamw_big(name, w, m, v, g_parts):
    shape = w.shape
    C = shape[-1]
    R = w.size // C
    tr, tc = _div_tile(R, 256, 8), _div_tile(C, 1024)
    n_g = len(g_parts)

    def body(*refs):
        w_ref, m_ref, v_ref = refs[:3]
        g_ref, d_ref, nm_ref, nv_ref = refs[3 + n_g:]
        g = refs[3][...]
        for extra in refs[4:3 + n_g]:
            g = g + extra[...]
        delta, m2, v2 = _adamw_math(w_ref[...], g, m_ref[...], v_ref[...])
        g_ref[...], d_ref[...], nm_ref[...], nv_ref[...] = g, delta, m2, v2

    blk = pl.BlockSpec((tr, tc), lambda i, j: (i, j))
    outs = pl.pallas_call(
        body,
        grid=(R // tr, C // tc),
        in_specs=[blk] * (3 + n_g),
        out_specs=[blk] * 4,
        out_shape=[jax.ShapeDtypeStruct((R, C), F32)] * 4,
        compiler_params=_cparams(("parallel", "parallel")),
        name=name,
    )(w.reshape(R, C), m.reshape(R, C), v.reshape(R, C), *[g.reshape(R, C) for g in g_parts])
    return [o.reshape(shape) for o in outs]


def _adamw_small(name, w, g, m, v):
    shape = w.shape
    two_d = (w.size // shape[-1], shape[-1])

    def body(w_ref, g_ref, m_ref, v_ref, d_ref, nm_ref, nv_ref):
        d_ref[...], nm_ref[...], nv_ref[...] = _adamw_math(w_ref[...], g_ref[...], m_ref[...], v_ref[...])

    outs = pl.pallas_call(
        body,
        out_shape=[jax.ShapeDtypeStruct(two_d, F32)] * 3,
        name=name,
    )(w.reshape(two_d), g.reshape(two_d), m.reshape(two_d), v.reshape(two_d))
    return [o.reshape(shape) for o in outs]


SMALL_ROWS = 40
S_BIN, S_G1, S_G2 = 16, 24, 32
MATS = ("w_in", "w_pa", "w_pb", "w_o", "w_mlp1", "w_mlp2")
LRU = ("lru_wr", "lru_wi")
BIG_DIM = dict(w_in=2, w_pa=1, w_pb=1, w_o=1, w_mlp1=2, w_mlp2=1, lru_wr=2, lru_wi=2)
WEIGHTS = ("norm1_g", "w_in", "b_in", "conv_a_w", "conv_a_b", "lru_wr", "lru_br", "lru_wi", "lru_bi", "lru_lam",
           "conv_b_w", "w_pa", "w_pb", "w_o", "norm2_g", "w_mlp1", "w_mlp2", "final_g")


def _rows_at(a, r0, total):
    pad = [(0, 0)] * a.ndim
    pad[-2] = (r0, total - r0 - a.shape[-2])
    return jnp.pad(a, pad)


def kernel(x, norm1_g, w_in, b_in, conv_a_w, conv_a_b, lru_wr, lru_br, lru_wi, lru_bi, lru_lam, conv_b_w, w_pa, w_pb, w_o, norm2_g, w_mlp1, w_mlp2, final_g, loss_target, m_norm1_g, m_w_in, m_b_in, m_conv_a_w, m_conv_a_b, m_lru_wr, m_lru_br, m_lru_wi, m_lru_bi, m_lru_lam, m_conv_b_w, m_w_pa, m_w_pb, m_w_o, m_norm2_g, m_w_mlp1, m_w_mlp2, m_final_g, v_norm1_g, v_w_in, v_b_in, v_conv_a_w, v_conv_a_b, v_lru_wr, v_lru_br, v_lru_wi, v_lru_bi, v_lru_lam, v_conv_b_w, v_w_pa, v_w_pb, v_w_o, v_norm2_g, v_w_mlp1, v_w_mlp2, v_final_g):
    wts = dict(norm1_g=norm1_g, w_in=w_in, b_in=b_in, conv_a_w=conv_a_w, conv_a_b=conv_a_b, lru_wr=lru_wr,
               lru_br=lru_br, lru_wi=lru_wi, lru_bi=lru_bi, lru_lam=lru_lam, conv_b_w=conv_b_w, w_pa=w_pa,
               w_pb=w_pb, w_o=w_o, norm2_g=norm2_g, w_mlp1=w_mlp1, w_mlp2=w_mlp2, final_g=final_g)
    mom = dict(norm1_g=m_norm1_g, w_in=m_w_in, b_in=m_b_in, conv_a_w=m_conv_a_w, conv_a_b=m_conv_a_b,
               lru_wr=m_lru_wr, lru_br=m_lru_br, lru_wi=m_lru_wi, lru_bi=m_lru_bi, lru_lam=m_lru_lam,
               conv_b_w=m_conv_b_w, w_pa=m_w_pa, w_pb=m_w_pb, w_o=m_w_o, norm2_g=m_norm2_g, w_mlp1=m_w_mlp1,
               w_mlp2=m_w_mlp2, final_g=m_final_g)
    vel = dict(norm1_g=v_norm1_g, w_in=v_w_in, b_in=v_b_in, conv_a_w=v_conv_a_w, conv_a_b=v_conv_a_b,
               lru_wr=v_lru_wr, lru_br=v_lru_br, lru_wi=v_lru_wi, lru_bi=v_lru_bi, lru_lam=v_lru_lam,
               conv_b_w=v_conv_b_w, w_pa=v_w_pa, w_pb=v_w_pb, w_o=v_w_o, norm2_g=v_norm2_g, w_mlp1=v_w_mlp1,
               w_mlp2=v_w_mlp2, final_g=v_final_g)
    depth, D = norm1_g.shape
    nb, bw = lru_wr.shape[1], lru_wr.shape[3]
    chip = 2 * lax.axis_index("x") + lax.axis_index("y")

    small_parts = [conv_a_w.reshape(-1), conv_b_w.reshape(-1), lru_br.reshape(-1), lru_bi.reshape(-1)]
    small_len = sum(p.shape[0] for p in small_parts)
    small_rows = -(-small_len // 1024) * 8
    small = jnp.concatenate(small_parts + [jnp.zeros((small_rows * 128 - small_len,), F32)]).reshape(small_rows, 128)
    gathered = _gather_weights([wts[n].astype(BF16) for n in LRU], [BIG_DIM[n] for n in LRU], small)
    full = dict(zip(LRU, gathered[:-1]))
    items = [(n, l) for l in range(depth) for n in MATS]
    mat_dims = [BIG_DIM[n] - 1 for n, _ in items]
    where = jnp.stack([chip, lax.axis_index("c")]).astype(jnp.int32)
    placed = {(n, l): _place_shard(f"place_{n}_{l}", wts[n], l, BIG_DIM[n] - 1, where) for n, l in items}
    flat = gathered[-1].reshape(N_CHIPS, small_rows * 128)
    off = 0
    small_full = []
    for part, shard in zip(small_parts, (conv_a_w, conv_b_w, lru_br, lru_bi)):
        piece = flat[:, off:off + part.shape[0]].reshape((N_CHIPS,) + shard.shape)
        small_full.append(jnp.moveaxis(piece, 0, -2).reshape(shard.shape[:-1] + (N_CHIPS * shard.shape[-1],)))
        off += part.shape[0]
    caw_f, cbw_f, br_f, bi_f = small_full
    pch = (_rows_at(conv_a_b[:, None, :], R_CAB, R_ROWS) + _rows_at(br_f.reshape(depth, 1, D), R_BR, R_ROWS)
           + _rows_at(bi_f.reshape(depth, 1, D), R_BI, R_ROWS) + _rows_at(lru_lam[:, None, :], R_LAM, R_ROWS)
           + _rows_at(caw_f, R_CAW, R_ROWS) + _rows_at(cbw_f, R_CBW, R_ROWS))
    W = dict(b_in=b_in, pch=pch, wr=full["lru_wr"], wi=full["lru_wi"], g1=norm1_g, g2=norm2_g, gf=final_g,
             core=lax.axis_index("c").astype(jnp.int32).reshape(1))

    loss_local, dx, grads, dgf, sums, landed = _local_fwd_bwd(x[0], loss_target[0], W, placed)
    loss = lax.psum(loss_local, ("x", "y", "c"))

    key = dict(w_in="w_in", w_pa="w_pa", w_pb="w_pb", w_o="w_o", w_mlp1="w1", w_mlp2="w2", lru_wr="wr", lru_wi="wi")
    out_g, out_d, out_m, out_v = {}, {}, {}, {}
    per_layer = [[grads[l][key[n]].astype(BF16) for l in range(depth)] for n in LRU]
    land = _scatter_grads(per_layer, [BIG_DIM[n] - 1 for n in LRU])
    chip_sums = [_sum_slots(f"sum_slots_{n}", ld.reshape(N_CHIPS, -1, ld.shape[-1])) for n, ld in zip(LRU, land)]
    sib_sums = _swap_with_sibling(chip_sums)
    for n, mine, sib in zip(LRU, chip_sums, sib_sums):
        out_g[n], out_d[n], out_m[n], out_v[n] = _adamw_big(f"adamw_{n}", wts[n], mom[n], vel[n], [mine, sib])
    acc = {n: None for n in MATS}
    for n, l in reversed(items):
        acc[n] = _reduce_into(f"reduce_{n}_{l}", sums[n, l], landed[n, l], acc[n], l, BIG_DIM[n], wts[n].shape, where)
    joined = _join_halves("join_halves", [acc[n] for n in MATS], [BIG_DIM[n] for n in MATS])
    for n, g in zip(MATS, joined):
        out_g[n], out_d[n], out_m[n], out_v[n] = _adamw_big(f"adamw_{n}", wts[n], mom[n], vel[n], [g])

    rows = []
    for l in range(depth):
        g = grads[l]
        rows.append(_rows_at(g["sm"], 0, SMALL_ROWS) + _rows_at(g["b_in"].reshape(7, D), S_BIN, SMALL_ROWS)
                    + _rows_at(g["g1"], S_G1, SMALL_ROWS) + _rows_at(g["g2"], S_G2, SMALL_ROWS))
    rows.append(_rows_at(dgf, 0, 8))
    tot = _allreduce_small(jnp.concatenate(rows, axis=0))
    per = tot[:depth * SMALL_ROWS].reshape(depth, SMALL_ROWS, D)

    def cols_of_chip(a, axis):
        size = a.shape[axis] // N_CHIPS
        return lax.dynamic_slice_in_dim(a, chip * size, size, axis=axis)

    small_g = dict(
        norm1_g=per[:, S_G1], b_in=per[:, S_BIN:S_BIN + 7].reshape(depth, 7 * D),
        conv_a_w=cols_of_chip(per[:, R_CAW:R_CAW + 4], 2), conv_a_b=per[:, R_CAB],
        lru_br=cols_of_chip(per[:, R_BR].reshape(depth, nb, bw), 2),
        lru_bi=cols_of_chip(per[:, R_BI].reshape(depth, nb, bw), 2), lru_lam=per[:, R_LAM],
        conv_b_w=cols_of_chip(per[:, R_CBW:R_CBW + 3], 2), norm2_g=per[:, S_G2],
        final_g=tot[depth * SMALL_ROWS])
    for n, g in small_g.items():
        out_g[n] = g
        out_d[n], out_m[n], out_v[n] = _adamw_small(f"adamw_{n}", wts[n], g, mom[n], vel[n])

    return (loss, dx[None], *[out_g[n] for n in WEIGHTS], *[out_d[n] for n in WEIGHTS],
            *[out_m[n] for n in WEIGHTS], *[out_v[n] for n in WEIGHTS])
```

```python
import functools

import jax
import jax.numpy as jnp
from jax import lax
from jax.experimental import pallas as pl
from jax.experimental.pallas import tpu as pltpu

F32 = jnp.float32
BF16 = jnp.bfloat16
MESH = pl.DeviceIdType.MESH

EPS = 1e-6
LRU_C = 8.0
ADAM_LR = 0.001
ADAM_B1 = 0.9
ADAM_B2 = 0.999
ADAM_EPS = 1e-08
ADAM_WD = 0.01
ADAM_STEP = 10

N_CHIPS = 4
N_DEV = 8
HALO = 8
VMEM_LIMIT = 56 * 1024 * 1024
MM_TILES = (1024, 1024, 2048)
MM_TILES_FUSED = (512, 1024, 2048)
SEQ_CHUNK = 256
ROW_TILE = 256

R_CAB, R_BR, R_BI, R_LAM, R_CAW, R_CBW, R_ROWS = 0, 1, 2, 3, 4, 8, 16


def _cparams(sem):
    return pltpu.CompilerParams(dimension_semantics=sem, vmem_limit_bytes=VMEM_LIMIT)


def _div_tile(n, pref, unit=128):
    if n <= pref:
        return n
    t = (pref // unit) * unit
    while n % t:
        t -= unit
    return t


def _sigmoid(v):
    return 1.0 / (1.0 + jnp.exp(-v))


def _gelu_and_grad(y):
    k = 0.7978845608028654
    c = 0.044715
    y2 = y * y
    t = jnp.tanh(k * (y + c * y2 * y))
    g = 0.5 * y * (1.0 + t)
    gp = 0.5 * (1.0 + t) + 0.5 * y * (1.0 - t * t) * (k * (1.0 + 3.0 * c * y2))
    return g, gp


def _softplus_neg(lam):
    e = jnp.exp(-jnp.abs(lam))
    w = 1.0 + e
    l1p = jnp.where(w == 1.0, e, jnp.log(w) * e / jnp.where(w == 1.0, 1.0, w - 1.0))
    return jnp.maximum(-lam, 0.0) + l1p


def _mm(name, mode, a, b, M, N, K, out_dtypes, epilogue=None, extras=(), la=None, lb=None, tiles=None,
        carry=None):
    tiles = MM_TILES if tiles is None else tiles
    tm, tn, tk = _div_tile(M, tiles[0]), _div_tile(N, tiles[1]), _div_tile(K, tiles[2])
    assert M % tm == 0 and N % tn == 0 and K % tk == 0, (name, M, N, K)
    nk = K // tk

    def spec(lead, shape, imap):
        if lead is None:
            return pl.BlockSpec(shape, imap)
        return pl.BlockSpec((None,) + shape, lambda i, j, k: (lead,) + imap(i, j, k))

    if mode == "nn":
        a_spec = spec(la, (tm, tk), lambda i, j, k: (i, k))
        b_spec = spec(lb, (tk, tn), lambda i, j, k: (k, j))
        dn = (((1,), (0,)), ((), ()))
    elif mode == "nt":
        a_spec = spec(la, (tm, tk), lambda i, j, k: (i, k))
        b_spec = spec(lb, (tn, tk), lambda i, j, k: (j, k))
        dn = (((1,), (1,)), ((), ()))
    else:
        a_spec = spec(la, (tk, tm), lambda i, j, k: (k, i))
        b_spec = spec(lb, (tk, tn), lambda i, j, k: (k, j))
        dn = (((0,), (0,)), ((), ()))

    ex_arrays, ex_specs = [], []
    for arr, kind, off in extras:
        ex_arrays.append(arr)
        if kind == "bias":
            ex_specs.append(pl.BlockSpec((1, tn), lambda i, j, k: (0, j)))
        else:
            assert off % tn == 0
            ex_specs.append(pl.BlockSpec((tm, tn), lambda i, j, k, o=off // tn: (i, j + o)))
    n_ex, n_out = len(ex_arrays), len(out_dtypes)
    n_cin = len(carry.ins) if carry else 0
    n_cout = len(carry.out_shapes) if carry else 0
    n_in = 2 + n_ex + n_cin
    gi, gj = M // tm, N // tn

    def body(*refs):
        a_ref, b_ref = refs[0], refs[1]
        ex = refs[2:2 + n_ex]
        outs = refs[n_in:n_in + n_out]
        acc = refs[n_in + n_out + n_cout]
        i, j, k = pl.program_id(0), pl.program_id(1), pl.program_id(2)
        if carry:
            c_in, c_out = refs[2 + n_ex:n_in], refs[n_in + n_out:n_in + n_out + n_cout]
            sems = refs[n_in + n_out + n_cout + 1:]

            @pl.when((i == 0) & (j == 0) & (k == 0))
            def _():
                carry.start(c_in, c_out, sems)

        def product():
            return lax.dot_general(a_ref[...], b_ref[...], dn, preferred_element_type=F32)

        def finish(r):
            vals = (r,) if epilogue is None else epilogue(r, *[e[...] for e in ex])
            for o, v in zip(outs, vals):
                o[...] = v.astype(o.dtype)

        if nk == 1:
            finish(product())
        else:
            @pl.when(k == 0)
            def _():
                acc[...] = product()

            @pl.when((k > 0) & (k < nk - 1))
            def _():
                acc[...] += product()

            @pl.when(k == nk - 1)
            def _():
                finish(acc[...] + product())

        if carry:
            @pl.when((i == gi - 1) & (j == gj - 1) & (k == nk - 1))
            def _():
                carry.finish(c_in, c_out, sems)

    res = pl.pallas_call(
        body,
        grid=(gi, gj, nk),
        in_specs=[a_spec, b_spec, *ex_specs] + [ANY] * n_cin,
        out_specs=[pl.BlockSpec((tm, tn), lambda i, j, k: (i, j)) for _ in range(n_out)] + [ANY] * n_cout,
        out_shape=[jax.ShapeDtypeStruct((M, N), d) for d in out_dtypes] + (carry.out_shapes if carry else []),
        scratch_shapes=[pltpu.VMEM((tm, tn) if nk > 1 else (8, 128), F32)] + (carry.scratch() if carry else []),
        input_output_aliases={2 + n_ex + ci: n_out + co for ci, co in carry.aliases.items()} if carry else {},
        compiler_params=_cparams(("arbitrary",) * 3 if carry else ("parallel", "parallel", "arbitrary")),
        name=name,
    )(a, b, *ex_arrays, *(carry.ins if carry else []))
    main = res[0] if n_out == 1 else res[:n_out]
    return (main, res[n_out:]) if carry else main


def _rms_fwd(name, x, g_row):
    T, D = x.shape
    tm = min(ROW_TILE, T)

    def body(x_ref, g_ref, h_ref):
        xv = x_ref[...]
        r = lax.rsqrt(jnp.mean(xv * xv, axis=-1, keepdims=True) + EPS)
        h_ref[...] = (xv * r * g_ref[...]).astype(BF16)

    return pl.pallas_call(
        body,
        grid=(T // tm,),
        in_specs=[pl.BlockSpec((tm, D), lambda i: (i, 0)), pl.BlockSpec((1, D), lambda i: (0, 0))],
        out_specs=pl.BlockSpec((tm, D), lambda i: (i, 0)),
        out_shape=jax.ShapeDtypeStruct((T, D), BF16),
        compiler_params=_cparams(("parallel",)),
        name=name,
    )(x, g_row)


def _rms_bwd(name, x, g_row, dh, dres):
    T, D = x.shape
    tm = min(ROW_TILE, T)

    def body(x_ref, g_ref, dh_ref, dres_ref, dx_ref, dxb_ref, dg_ref):
        xv, dhv = x_ref[...], dh_ref[...]
        r = lax.rsqrt(jnp.mean(xv * xv, axis=-1, keepdims=True) + EPS)
        gd = g_ref[...] * dhv
        c = jnp.mean(xv * gd, axis=-1, keepdims=True)
        dx = r * gd - xv * (r * r * r) * c + dres_ref[...]
        dx_ref[...] = dx
        dxb_ref[...] = dx.astype(BF16)

        @pl.when(pl.program_id(0) == 0)
        def _():
            dg_ref[...] = jnp.zeros_like(dg_ref)

        dg_ref[...] += jnp.sum(dhv * xv * r, axis=0, keepdims=True)

    row = pl.BlockSpec((tm, D), lambda i: (i, 0))
    vec = pl.BlockSpec((1, D), lambda i: (0, 0))
    return pl.pallas_call(
        body,
        grid=(T // tm,),
        in_specs=[row, vec, row, row],
        out_specs=[row, row, vec],
        out_shape=[jax.ShapeDtypeStruct((T, D), F32), jax.ShapeDtypeStruct((T, D), BF16),
                   jax.ShapeDtypeStruct((1, D), F32)],
        compiler_params=_cparams(("arbitrary",)),
        name=name,
    )(x, g_row, dh, dres)


def _loss_head(name, x, g_row, tgt):
    T, D = x.shape
    tm = min(ROW_TILE, T)

    def body(x_ref, g_ref, t_ref, dx_ref, dxb_ref, dg_ref, loss_ref):
        xv, g = x_ref[...], g_ref[...]
        r = lax.rsqrt(jnp.mean(xv * xv, axis=-1, keepdims=True) + EPS)
        xh = xv * r
        e = xh * g - t_ref[...]
        lpart = 0.5 * jnp.sum(jnp.mean(e * e, axis=-1, keepdims=True))
        dy = e * (1.0 / D)
        gd = g * dy
        c = jnp.mean(xv * gd, axis=-1, keepdims=True)
        dx = r * gd - xv * (r * r * r) * c
        dx_ref[...] = dx
        dxb_ref[...] = dx.astype(BF16)

        @pl.when(pl.program_id(0) == 0)
        def _():
            dg_ref[...] = jnp.zeros_like(dg_ref)
            loss_ref[...] = jnp.zeros_like(loss_ref)

        dg_ref[...] += jnp.sum(dy * xh, axis=0, keepdims=True)
        loss_ref[...] += jnp.full(loss_ref.shape, lpart, F32)

    row = pl.BlockSpec((tm, D), lambda i: (i, 0))
    vec = pl.BlockSpec((1, D), lambda i: (0, 0))
    return pl.pallas_call(
        body,
        grid=(T // tm,),
        in_specs=[row, vec, row],
        out_specs=[row, row, vec, pl.BlockSpec((8, 128), lambda i: (0, 0))],
        out_shape=[jax.ShapeDtypeStruct((T, D), F32), jax.ShapeDtypeStruct((T, D), BF16),
                   jax.ShapeDtypeStruct((1, D), F32), jax.ShapeDtypeStruct((8, 128), F32)],
        compiler_params=_cparams(("arbitrary",)),
        name=name,
    )(x, g_row, tgt)


def _colsum(name, a, carry=None):
    T, N = a.shape
    tm, tn = min(512, T), _div_tile(N, 2048)
    gj, gi = N // tn, T // tm
    n_cin = len(carry.ins) if carry else 0
    n_cout = len(carry.out_shapes) if carry else 0

    def body(*refs):
        a_ref, o_ref = refs[0], refs[1 + n_cin]
        j, i = pl.program_id(0), pl.program_id(1)
        if carry:
            c_in, c_out, sems = refs[1:1 + n_cin], refs[2 + n_cin:2 + n_cin + n_cout], refs[2 + n_cin + n_cout:]

            @pl.when((j == 0) & (i == 0))
            def _():
                carry.start(c_in, c_out, sems)

        @pl.when(i == 0)
        def _():
            o_ref[...] = jnp.zeros_like(o_ref)

        o_ref[...] += jnp.sum(a_ref[...].astype(F32), axis=0, keepdims=True)

        if carry:
            @pl.when((j == gj - 1) & (i == gi - 1))
            def _():
                carry.finish(c_in, c_out, sems)

    res = pl.pallas_call(
        body,
        grid=(gj, gi),
        in_specs=[pl.BlockSpec((tm, tn), lambda j, i: (i, j))] + [ANY] * n_cin,
        out_specs=[pl.BlockSpec((1, tn), lambda j, i: (0, j))] + [ANY] * n_cout,
        out_shape=[jax.ShapeDtypeStruct((1, N), F32)] + (carry.out_shapes if carry else []),
        scratch_shapes=carry.scratch() if carry else [],
        compiler_params=_cparams(("arbitrary", "arbitrary") if carry else ("parallel", "arbitrary")),
        name=name,
    )(a, *(carry.ins if carry else []))
    return (res[0], res[1:]) if carry else res[0]


def _tile_scan(a, b, row, reverse):
    for s in (1, 2, 4):
        if reverse:
            a_s, b_s, m = pltpu.roll(a, 8 - s, 0), pltpu.roll(b, 8 - s, 0), row < 8 - s
        else:
            a_s, b_s, m = pltpu.roll(a, s, 0), pltpu.roll(b, s, 0), row >= s
        b = jnp.where(m, a * b_s + b, b)
        a = jnp.where(m, a * a_s, a)
    return a, b


def _chunk_scan(a_s, b_s, out_ref, carry, n_tiles, width, reverse):
    row = lax.broadcasted_iota(jnp.int32, (8, width), 0)
    edge = 0 if reverse else 7

    def step(j, c):
        jj = (n_tiles - 1 - j) if reverse else j
        o = pl.multiple_of(jj * 8, 8)
        ca, cb = _tile_scan(a_s[pl.ds(o, 8), :], b_s[pl.ds(o, 8), :], row, reverse)
        h = ca * c + cb
        out_ref[pl.ds(o, 8), :] = h
        return jnp.broadcast_to(h[edge:edge + 1, :], (8, width))

    carry[...] = lax.fori_loop(0, n_tiles, step, carry[...])


def _gates(xc, p_ref, wr_ref, wi_ref):
    xcb = xc.astype(BF16)
    r = _sigmoid(jnp.dot(xcb, wr_ref[...], preferred_element_type=F32) + p_ref[R_BR:R_BR + 1, :])
    ig = _sigmoid(jnp.dot(xcb, wi_ref[...], preferred_element_type=F32) + p_ref[R_BI:R_BI + 1, :])
    sp = _softplus_neg(p_ref[R_LAM:R_LAM + 1, :])
    log_a = (-LRU_C) * r * sp
    a = jnp.exp(log_a)
    t = jnp.tanh(log_a)
    mult = jnp.sqrt(-2.0 * t / (1.0 - t))
    return xcb, r, ig, sp, a, mult


def _mixer_specs(Tc, bw, nb, layer):
    def seg(s):
        return pl.BlockSpec((Tc, bw), lambda n, i: (i, s * nb + n))

    p_spec = pl.BlockSpec((None, R_ROWS, bw), lambda n, i: (layer, 0, n))
    w_spec = pl.BlockSpec((None, None, bw, bw), lambda n, i: (layer, n, 0, 0))
    return seg, p_spec, w_spec


def _mixer_fwd(name, layer, z, pch, wr, wi, comm=None):
    T, D = z.shape[0], z.shape[1] // 7
    bw, nb = wr.shape[-1], wr.shape[1]
    Tc = min(SEQ_CHUNK, T)
    nT = T // Tc
    n_cin = len(comm.ins) if comm else 0
    n_cout = len(comm.out_shapes) if comm else 0

    def body(*refs):
        xa_ref, ya_ref, cb_ref, cc_ref, cx_ref, p_ref, wr_ref, wi_ref = refs[:8]
        pa_ref, pb_ref, xc_ref, hl_ref, vb_ref = refs[8 + n_cin:13 + n_cin]
        xa_buf, u_buf, a_s, b_s, carry = refs[13 + n_cin + n_cout:18 + n_cin + n_cout]
        if comm:
            c_in, c_out, sems = refs[8:8 + n_cin], refs[13 + n_cin:13 + n_cin + n_cout], refs[18 + n_cin + n_cout:]

            @pl.when((pl.program_id(0) == 0) & (pl.program_id(1) == 0))
            def _():
                comm.start(c_in, c_out, sems)

        @pl.when(pl.program_id(1) == 0)
        def _():
            xa_buf[0:HALO, :] = jnp.zeros((HALO, bw), F32)
            u_buf[0:HALO, :] = jnp.zeros((HALO, bw), F32)
            carry[...] = jnp.zeros_like(carry)

        xa_buf[HALO:HALO + Tc, :] = xa_ref[...]
        xc = p_ref[R_CAB:R_CAB + 1, :]
        for k in range(4):
            xc = xc + p_ref[R_CAW + k:R_CAW + k + 1, :] * xa_buf[HALO - 3 + k:HALO - 3 + k + Tc, :]
        xc_ref[...] = xc
        _, _, ig, _, a, mult = _gates(xc, p_ref, wr_ref, wi_ref)
        a_s[...] = a
        b_s[...] = mult * (ig * xc)
        _chunk_scan(a_s, b_s, hl_ref, carry, Tc // 8, bw, False)
        g, _ = _gelu_and_grad(ya_ref[...])
        pa_ref[...] = (hl_ref[...] * g).astype(BF16)

        u_buf[HALO:HALO + Tc, :] = cc_ref[...] * cx_ref[...]
        vb = jnp.zeros((Tc, bw), F32)
        for k in range(3):
            vb = vb + p_ref[R_CBW + k:R_CBW + k + 1, :] * u_buf[HALO - 2 + k:HALO - 2 + k + Tc, :]
        vb_ref[...] = vb
        pb_ref[...] = (cb_ref[...] * vb).astype(BF16)
        xa_buf[0:HALO, :] = xa_buf[Tc:Tc + HALO, :]
        u_buf[0:HALO, :] = u_buf[Tc:Tc + HALO, :]

        if comm:
            @pl.when((pl.program_id(0) == nb - 1) & (pl.program_id(1) == nT - 1))
            def _():
                comm.finish(c_in, c_out, sems)

    seg, p_spec, w_spec = _mixer_specs(Tc, bw, nb, layer)
    out = pl.BlockSpec((Tc, bw), lambda n, i: (i, n))
    res = pl.pallas_call(
        body,
        grid=(nb, nT),
        in_specs=[seg(0), seg(1), seg(2), seg(3), seg(4), p_spec, w_spec, w_spec] + [ANY] * n_cin,
        out_specs=[out] * 5 + [ANY] * n_cout,
        out_shape=[jax.ShapeDtypeStruct((T, D), BF16), jax.ShapeDtypeStruct((T, D), BF16),
                   jax.ShapeDtypeStruct((T, D), F32), jax.ShapeDtypeStruct((T, D), F32),
                   jax.ShapeDtypeStruct((T, D), F32)] + (comm.out_shapes if comm else []),
        scratch_shapes=[pltpu.VMEM((Tc + HALO, bw), F32), pltpu.VMEM((Tc + HALO, bw), F32),
                        pltpu.VMEM((Tc, bw), F32), pltpu.VMEM((Tc, bw), F32), pltpu.VMEM((8, bw), F32)]
        + (comm.scratch() if comm else []),
        input_output_aliases={8 + ci: 5 + co for ci, co in comm.aliases.items()} if comm else {},
        compiler_params=_cparams(("arbitrary", "arbitrary") if comm else ("parallel", "arbitrary")),
        name=name,
    )(z, z, z, z, z, pch, wr, wi, *(comm.ins if comm else []))
    return (res[:5], res[5:]) if comm else res


def _mixer_bwd(name, layer, z, xc, hl, vb, dpa, dpb, pch, wr, wi, comm=None):
    T, D = z.shape[0], z.shape[1] // 7
    bw, nb = wr.shape[-1], wr.shape[1]
    Tc = min(SEQ_CHUNK, T)
    nT = T // Tc
    tpc = Tc // 8
    n_cin = len(comm.ins) if comm else 0
    n_cout = len(comm.out_shapes) if comm else 0

    def body(*refs):
        (xa_ref, ya_ref, cb_ref, cc_ref, cx_ref, xc_ref, hl_ref, hp_ref, vb_ref, dpa_ref, dpb_ref,
         p_ref, wr_ref, wi_ref) = refs[:14]
        dxa_ref, dya_ref, dcb_ref, dcc_ref, dcx_ref, dwr_ref, dwi_ref, sm_ref = refs[14 + n_cin:22 + n_cin]
        h_buf, a_buf, dxc_buf, dvb_buf, a_s, d_s, lam_s, carry = refs[22 + n_cin + n_cout:30 + n_cin + n_cout]
        i = pl.program_id(1)
        if comm:
            c_in, c_out = refs[14:14 + n_cin], refs[22 + n_cin:22 + n_cin + n_cout]
            sems = refs[30 + n_cin + n_cout:]

            @pl.when((pl.program_id(0) == 0) & (i == 0))
            def _():
                comm.start(c_in, c_out, sems)

        @pl.when(i == 0)
        def _():
            a_buf[Tc:Tc + HALO, :] = jnp.zeros((HALO, bw), F32)
            dxc_buf[Tc:Tc + HALO, :] = jnp.zeros((HALO, bw), F32)
            dvb_buf[Tc:Tc + HALO, :] = jnp.zeros((HALO, bw), F32)
            carry[...] = jnp.zeros_like(carry)
            dwr_ref[...] = jnp.zeros_like(dwr_ref)
            dwi_ref[...] = jnp.zeros_like(dwi_ref)
            sm_ref[...] = jnp.zeros_like(sm_ref)

        xcv = xc_ref[...]
        xcb, r, ig, sp, a, mult = _gates(xcv, p_ref, wr_ref, wi_ref)
        g, gp = _gelu_and_grad(ya_ref[...])
        hlv, dpav = hl_ref[...], dpa_ref[...]
        dya_ref[...] = (dpav * hlv * gp).astype(BF16)

        a_buf[0:Tc, :] = a
        a_s[...] = a_buf[1:Tc + 1, :]
        d_s[...] = dpav * g
        _chunk_scan(a_s, d_s, lam_s, carry, tpc, bw, True)
        lamv = lam_s[...]

        h_buf[HALO:HALO + Tc, :] = hlv
        h_buf[0:HALO, :] = jnp.where(i == nT - 1, 0.0, hp_ref[...])
        da = lamv * h_buf[HALO - 1:HALO - 1 + Tc, :]
        dmult = lamv * (ig * xcv)
        dbx = lamv * mult
        dig = dbx * xcv
        dxc = dbx * ig
        dlog_a = da * a - dmult * (a * a) / mult
        dpr = (dlog_a * ((-LRU_C) * sp)) * r * (1.0 - r)
        dpi = dig * ig * (1.0 - ig)
        dprb, dpib = dpr.astype(BF16), dpi.astype(BF16)
        nt = (((1,), (1,)), ((), ()))
        tn = (((0,), (0,)), ((), ()))
        dxc = dxc + lax.dot_general(dprb, wr_ref[...], nt, preferred_element_type=F32)
        dxc = dxc + lax.dot_general(dpib, wi_ref[...], nt, preferred_element_type=F32)
        dwr_ref[...] += lax.dot_general(xcb, dprb, tn, preferred_element_type=F32)
        dwi_ref[...] += lax.dot_general(xcb, dpib, tn, preferred_element_type=F32)

        def rowsum(v):
            return jnp.sum(v, axis=0, keepdims=True)

        sm_ref[R_CAB:R_CAB + 1, :] += rowsum(dxc)
        sm_ref[R_BR:R_BR + 1, :] += rowsum(dpr)
        sm_ref[R_BI:R_BI + 1, :] += rowsum(dpi)
        sm_ref[R_LAM:R_LAM + 1, :] += rowsum(dlog_a * ((-LRU_C) * r))

        dxc_buf[0:Tc, :] = dxc
        xav = xa_ref[...]
        dxa = jnp.zeros((Tc, bw), F32)
        for k in range(4):
            sh = dxc_buf[3 - k:3 - k + Tc, :]
            dxa = dxa + p_ref[R_CAW + k:R_CAW + k + 1, :] * sh
            sm_ref[R_CAW + k:R_CAW + k + 1, :] += rowsum(xav * sh)
        dxa_ref[...] = dxa.astype(BF16)

        dpbv, cbv, ccv, cxv = dpb_ref[...], cb_ref[...], cc_ref[...], cx_ref[...]
        dcb_ref[...] = (dpbv * vb_ref[...]).astype(BF16)
        dvb_buf[0:Tc, :] = dpbv * cbv
        u = ccv * cxv
        du = jnp.zeros((Tc, bw), F32)
        for k in range(3):
            sh = dvb_buf[2 - k:2 - k + Tc, :]
            du = du + p_ref[R_CBW + k:R_CBW + k + 1, :] * sh
            sm_ref[R_CBW + k:R_CBW + k + 1, :] += rowsum(u * sh)
        dcc_ref[...] = (du * cxv).astype(BF16)
        dcx_ref[...] = (du * ccv).astype(BF16)

        a_buf[Tc:Tc + HALO, :] = a_buf[0:HALO, :]
        dxc_buf[Tc:Tc + HALO, :] = dxc_buf[0:HALO, :]
        dvb_buf[Tc:Tc + HALO, :] = dvb_buf[0:HALO, :]

        @pl.when(i == nT - 1)
        def _():
            sm_ref[R_LAM:R_LAM + 1, :] = sm_ref[R_LAM:R_LAM + 1, :] * (-_sigmoid(-p_ref[R_LAM:R_LAM + 1, :]))

        if comm:
            @pl.when((pl.program_id(0) == nb - 1) & (i == nT - 1))
            def _():
                comm.finish(c_in, c_out, sems)

    def seg(s):
        return pl.BlockSpec((Tc, bw), lambda n, i: (nT - 1 - i, s * nb + n))

    blk = pl.BlockSpec((Tc, bw), lambda n, i: (nT - 1 - i, n))
    halo = pl.BlockSpec((8, bw), lambda n, i: (jnp.maximum((nT - 1 - i) * tpc - 1, 0), n))
    p_spec = pl.BlockSpec((None, R_ROWS, bw), lambda n, i: (layer, 0, n))
    w_spec = pl.BlockSpec((None, None, bw, bw), lambda n, i: (layer, n, 0, 0))
    dw_spec = pl.BlockSpec((None, bw, bw), lambda n, i: (n, 0, 0))
    act = jax.ShapeDtypeStruct((T, D), BF16)
    res = pl.pallas_call(
        body,
        grid=(nb, nT),
        in_specs=[seg(0), seg(1), seg(2), seg(3), seg(4), blk, blk, halo, blk, blk, blk, p_spec, w_spec, w_spec]
        + [ANY] * n_cin,
        out_specs=[blk] * 5 + [dw_spec, dw_spec, pl.BlockSpec((R_ROWS, bw), lambda n, i: (0, n))] + [ANY] * n_cout,
        out_shape=[act] * 5 + [jax.ShapeDtypeStruct((nb, bw, bw), F32), jax.ShapeDtypeStruct((nb, bw, bw), F32),
                               jax.ShapeDtypeStruct((R_ROWS, D), F32)] + (comm.out_shapes if comm else []),
        scratch_shapes=[pltpu.VMEM((Tc + HALO, bw), F32)] * 4 + [pltpu.VMEM((Tc, bw), F32)] * 3
        + [pltpu.VMEM((8, bw), F32)] + (comm.scratch() if comm else []),
        input_output_aliases={14 + ci: 8 + co for ci, co in comm.aliases.items()} if comm else {},
        compiler_params=_cparams(("arbitrary", "arbitrary") if comm else ("parallel", "arbitrary")),
        name=name,
    )(z, z, z, z, z, xc, hl, hl, vb, dpa, dpb, pch, wr, wi, *(comm.ins if comm else []))
    return (res[:8], res[8:]) if comm else res


def _local_fwd_bwd(x, tgt, W, placed=None):
    T, D = x.shape
    g1, g2 = W["g1"], W["g2"]
    depth = g1.shape[0]
    FF = 4 * D
    if placed is None:
        mats = {(n, l): W[n][l] for n in MATS for l in range(depth)}
    else:
        mats = {}
        mats["w_in", 0], = _run_carry("gather_first", _gather_carry([placed["w_in", 0]], [(0, 1, 0, 1)]))

    def gathering(specs):
        if placed is None or not specs:
            return None, []
        keys = [(n, l) for n, l, _, _, _ in specs]
        arrays = [mats.get(k, placed[k]) for k in keys]
        return _gather_carry(arrays, [(i, d, part, nparts) for i, (_, _, d, part, nparts) in enumerate(specs)]), keys

    def hosted(call, specs, **kw):
        carry, keys = gathering(specs)
        if carry is None:
            return call(**kw)
        res, got = call(**kw, **{("comm" if call.func is _mixer_fwd else "carry"): carry})
        mats.update(zip(keys, got))
        return res

    saved = []
    xs = x
    for l in range(depth):
        h = _rms_fwd(f"rms1_fwd_{l}", xs, g1[l][None])
        nxt = l + 1 < depth
        projs = [("w_pa", l, 0, 0, 1), ("w_pb", l, 0, 0, 1), ("w_o", l, 0, 0, 1)]
        z = hosted(functools.partial(_mm, f"in_proj_{l}", "nn", h, mats["w_in", l], T, 7 * D, D, [F32]),
                   projs + ([("w_mlp1", l, 1, 0, 1)] if nxt else []),
                   epilogue=lambda acc, b: (acc + b,), extras=[(W["b_in"][l][None], "bias", 0)])
        pa, pb, xc, hl, vb = hosted(
            functools.partial(_mixer_fwd, f"mixer_fwd_{l}", l, z, W["pch"], W["wr"], W["wi"]),
            [("w_mlp2", l, 0, 0, 1)] if nxt else [("w_mlp1", l, 1, 0, 1)])
        oa = _mm(f"proj_a_{l}", "nn", pa, mats["w_pa", l], T, D, D, [F32])

        def merge(acc, oav, ga, gb):
            return acc, _sigmoid(ga) * oav + _sigmoid(gb) * acc

        ob, mg = _mm(f"proj_b_merge_{l}", "nn", pb, mats["w_pb", l], T, D, D, [F32, BF16], epilogue=merge,
                     tiles=MM_TILES_FUSED,
                     extras=[(oa, "tile", 0), (z, "tile", 5 * D), (z, "tile", 6 * D)])
        x1 = _mm(f"out_proj_{l}", "nn", mg, mats["w_o", l], T, D, D, [F32],
                 epilogue=lambda acc, res: (res + acc,), extras=[(xs, "tile", 0)])
        h2 = _rms_fwd(f"rms2_fwd_{l}", x1, g2[l][None])

        def relu2(acc):
            pr = jnp.maximum(acc, 0.0)
            return pr * pr, pr

        u, pr = hosted(functools.partial(_mm, f"mlp1_{l}", "nn", h2, mats["w_mlp1", l], T, FF, D, [BF16, BF16]),
                       [("w_in", l + 1, 1, 0, 2)] if nxt else [("w_mlp2", l, 0, 0, 1)], epilogue=relu2)
        x2 = hosted(functools.partial(_mm, f"mlp2_{l}", "nn", u, mats["w_mlp2", l], T, D, FF, [F32]),
                    [("w_in", l + 1, 1, 1, 2)] if nxt else [],
                    epilogue=lambda acc, res: (res + acc,), extras=[(x1, "tile", 0)])
        saved.append(dict(x0=xs, h=h, z=z, pa=pa, pb=pb, xc=xc, hl=hl, vb=vb, oa=oa, ob=ob, mg=mg, x1=x1,
                          h2=h2, u=u, pr=pr))
        xs = x2

    dx, dxb, dgf, loss_blk = _loss_head("loss_head", xs, W["gf"][None], tgt)

    gmat, got, sums, landed = {}, {}, {}, {}

    def reducing(call, swaps=(), scatters=(), mixer=False, **kw):
        swaps, scatters = [(n, l) for n in swaps], [(n, l) for n in scatters]
        if placed is None:
            return call(**kw)
        cs = _swap_carry([gmat[k] for k in swaps], [BIG_DIM[k[0]] - 1 for k in swaps]) if swaps else None
        cc = _scatter_carry([sums[k] for k in scatters]) if scatters else None
        res, moved = call(**kw, **{("comm" if mixer else "carry"): _merge_carries([cs, cc])})
        got.update(zip(swaps, moved[:len(swaps)]))
        landed.update(zip(scatters, moved[len(swaps):]))
        return res

    def add(names):
        if placed is not None:
            for n in names:
                sums[n, l] = _add_halves(f"add_halves_{n}_{l}", gmat[n, l], got[n, l], BIG_DIM[n] - 1, W["core"])

    grads = [None] * depth
    for l in reversed(range(depth)):
        s = saved[l]
        dp = _mm(f"mlp2_dx_{l}", "nt", dxb, mats["w_mlp2", l], T, FF, D, [BF16],
                 epilogue=lambda acc, prv: (2.0 * prv.astype(F32) * acc,), extras=[(s["pr"], "tile", 0)])
        dw2 = gmat["w_mlp2", l] = _mm(f"mlp2_dw_{l}", "tn", s["u"], dxb, FF, D, T, [BF16])
        dh2 = reducing(functools.partial(_mm, f"mlp1_dx_{l}", "nt", dp, mats["w_mlp1", l], T, D, FF, [F32]),
                       swaps=["w_mlp2"])
        add(["w_mlp2"])
        dw1 = gmat["w_mlp1", l] = reducing(functools.partial(_mm, f"mlp1_dw_{l}", "tn", s["h2"], dp, D, FF, T, [BF16]),
                                           scatters=["w_mlp2"])
        dx1, dx1b, dg2 = _rms_bwd(f"rms2_bwd_{l}", s["x1"], g2[l][None], dh2, dx)

        def unmerge(acc, ga, gb, oav, obv):
            sa, sb = _sigmoid(ga), _sigmoid(gb)
            return acc * sa, acc * sb, acc * oav * sa * (1.0 - sa), acc * obv * sb * (1.0 - sb)

        doa, dob, dga, dgb = reducing(
            functools.partial(_mm, f"out_proj_dx_{l}", "nt", dx1b, mats["w_o", l], T, D, D, [BF16] * 4),
            swaps=["w_mlp1"], tiles=MM_TILES_FUSED, epilogue=unmerge,
            extras=[(s["z"], "tile", 5 * D), (s["z"], "tile", 6 * D), (s["oa"], "tile", 0), (s["ob"], "tile", 0)])
        add(["w_mlp1"])
        dwo = gmat["w_o", l] = _mm(f"out_proj_dw_{l}", "tn", s["mg"], dx1b, D, D, T, [BF16])
        dpa = _mm(f"proj_a_dx_{l}", "nt", doa, mats["w_pa", l], T, D, D, [F32])
        dwpa = gmat["w_pa", l] = _mm(f"proj_a_dw_{l}", "tn", s["pa"], doa, D, D, T, [BF16])
        dpb = _mm(f"proj_b_dx_{l}", "nt", dob, mats["w_pb", l], T, D, D, [F32])
        dwpb = gmat["w_pb", l] = _mm(f"proj_b_dw_{l}", "tn", s["pb"], dob, D, D, T, [BF16])
        dxa, dya, dcb, dcc, dcx, dwr, dwi, sm = reducing(
            functools.partial(_mixer_bwd, f"mixer_bwd_{l}", l, s["z"], s["xc"], s["hl"], s["vb"], dpa, dpb,
                              W["pch"], W["wr"], W["wi"]),
            swaps=["w_o", "w_pa", "w_pb"], scatters=["w_mlp1"], mixer=True)
        add(["w_o", "w_pa", "w_pb"])
        dz = jnp.concatenate([dxa, dya, dcb, dcc, dcx, dga, dgb], axis=1)
        dwin = gmat["w_in", l] = reducing(
            functools.partial(_mm, f"in_proj_dw_{l}", "tn", s["h"], dz, D, 7 * D, T, [BF16]),
            scatters=["w_o", "w_pa", "w_pb"])
        dbin = reducing(functools.partial(_colsum, f"bias_grad_{l}", dz), swaps=["w_in"])
        add(["w_in"])
        dh = reducing(functools.partial(_mm, f"in_proj_dx_{l}", "nt", dz, mats["w_in", l], T, D, 7 * D, [F32]),
                      scatters=["w_in"])
        dx, dxb, dg1 = _rms_bwd(f"rms1_bwd_{l}", s["x0"], g1[l][None], dh, dx1)
        grads[l] = dict(w_in=dwin, w_pa=dwpa, w_pb=dwpb, w_o=dwo, w_mlp1=dw1, w_mlp2=dw2, wr=dwr, wi=dwi,
                        sm=sm, b_in=dbin, g1=dg1, g2=dg2)
    return loss_blk[0, 0], dx, grads, dgf, sums, landed


ANY = pl.BlockSpec(memory_space=pl.ANY)


def _place():
    x, y, c = lax.axis_index("x"), lax.axis_index("y"), lax.axis_index("c")
    peers = [(1 - x, y, c), (x, 1 - y, c), (1 - x, 1 - y, c)]
    chips = [2 * (1 - x) + y, 2 * x + (1 - y), 2 * (1 - x) + (1 - y)]
    return (x, y, c), 2 * x + y, peers, chips


def _window(ref, dim, q, size):
    idx = [slice(None)] * len(ref.shape)
    idx[dim] = pl.ds(q * size, size)
    return ref.at[tuple(idx)]


def _gather_weights(shards, dims, small):
    n = len(shards)
    sizes = [s.shape[d] for s, d in zip(shards, dims)]
    full = [jax.ShapeDtypeStruct(s.shape[:d] + (s.shape[d] * N_CHIPS,) + s.shape[d + 1:], s.dtype)
            for s, d in zip(shards, dims)]
    full.append(jax.ShapeDtypeStruct((N_CHIPS,) + small.shape, small.dtype))

    def body(*refs):
        ins, outs = refs[:n + 1], refs[n + 1:2 * n + 2]
        send_sems, recv_sems, local_sems = refs[2 * n + 2:]
        _, k, peers, chips = _place()

        def dst(w, q):
            return outs[w].at[q] if w == n else _window(outs[w], dims[w], q, sizes[w])

        local = [pltpu.make_async_copy(ins[w], dst(w, k), local_sems.at[w]) for w in range(n + 1)]
        for cp in local:
            cp.start()
        sends = []
        for p, peer in enumerate(peers):
            for w in range(n + 1):
                s = p * (n + 1) + w
                sends.append(pltpu.make_async_remote_copy(
                    src_ref=ins[w], dst_ref=dst(w, k), send_sem=send_sems.at[s], recv_sem=recv_sems.at[s],
                    device_id=peer, device_id_type=MESH))
        for cp in sends:
            cp.start()
        for p, peer in enumerate(peers):
            for w in range(n + 1):
                s = p * (n + 1) + w
                pltpu.make_async_remote_copy(
                    src_ref=ins[w], dst_ref=dst(w, chips[p]), send_sem=send_sems.at[s], recv_sem=recv_sems.at[s],
                    device_id=peer, device_id_type=MESH).wait_recv()
        for cp in sends:
            cp.wait_send()
        for cp in local:
            cp.wait()

    return pl.pallas_call(
        body,
        in_specs=[ANY] * (n + 1),
        out_specs=[ANY] * (n + 1),
        out_shape=full,
        scratch_shapes=[pltpu.SemaphoreType.DMA((3 * (n + 1),)), pltpu.SemaphoreType.DMA((3 * (n + 1),)),
                        pltpu.SemaphoreType.DMA((n + 1,))],
        name="gather_weights",
    )(*shards, small)


def _scatter_grads(grads, dims):
    n, depth = len(grads), len(grads[0])
    sizes = [g[0].shape[d] // N_CHIPS for g, d in zip(grads, dims)]
    land = []
    for g, d, sz in zip(grads, dims, sizes):
        shp = g[0].shape
        land.append(jax.ShapeDtypeStruct((N_CHIPS, depth) + shp[:d] + (sz,) + shp[d + 1:], g[0].dtype))

    def body(*refs):
        ins, outs = refs[:n * depth], refs[n * depth:n * depth + n]
        send_sems, recv_sems, local_sems = refs[n * depth + n:]
        _, k, peers, chips = _place()

        def src(w, l, q):
            return _window(ins[w * depth + l], dims[w], q, sizes[w])

        local = [pltpu.make_async_copy(src(w, l, k), outs[w].at[3, l], local_sems.at[w * depth + l])
                 for w in range(n) for l in range(depth)]
        for cp in local:
            cp.start()
        sends = []
        for p, peer in enumerate(peers):
            for w in range(n):
                for l in range(depth):
                    s = (p * n + w) * depth + l
                    sends.append(pltpu.make_async_remote_copy(
                        src_ref=src(w, l, chips[p]), dst_ref=outs[w].at[p, l], send_sem=send_sems.at[s],
                        recv_sem=recv_sems.at[s], device_id=peer, device_id_type=MESH))
        for cp in sends:
            cp.start()
        for cp in sends:
            cp.wait_recv()
        for cp in sends:
            cp.wait_send()
        for cp in local:
            cp.wait()

    flat = [g for gl in grads for g in gl]
    return pl.pallas_call(
        body,
        in_specs=[ANY] * (n * depth),
        out_specs=[ANY] * n,
        out_shape=land,
        scratch_shapes=[pltpu.SemaphoreType.DMA((3 * n * depth,)), pltpu.SemaphoreType.DMA((3 * n * depth,)),
                        pltpu.SemaphoreType.DMA((n * depth,))],
        name="scatter_grads",
    )(*flat)


def _sum_slots(name, land):
    _, R, C = land.shape
    tr, tc = _div_tile(R, 512, 8), _div_tile(C, 1024)

    def body(a_ref, b_ref, c_ref, d_ref, o_ref):
        o_ref[...] = ((d_ref[...].astype(F32) + a_ref[...].astype(F32)) + b_ref[...].astype(F32)) \
            + c_ref[...].astype(F32)

    def slot(q):
        return pl.BlockSpec((None, tr, tc), lambda i, j: (q, i, j))

    return pl.pallas_call(
        body,
        grid=(R // tr, C // tc),
        in_specs=[slot(0), slot(1), slot(2), slot(3)],
        out_specs=pl.BlockSpec((tr, tc), lambda i, j: (i, j)),
        out_shape=jax.ShapeDtypeStruct((R, C), F32),
        compiler_params=_cparams(("parallel", "parallel")),
        name=name,
    )(land, land, land, land)


def _swap_with_sibling(parts):
    n = len(parts)

    def body(*refs):
        ins, outs = refs[:n], refs[n:2 * n]
        send_sems, recv_sems = refs[2 * n:]
        (x, y, c), _, _, _ = _place()
        copies = [pltpu.make_async_remote_copy(
            src_ref=ins[w], dst_ref=outs[w], send_sem=send_sems.at[w], recv_sem=recv_sems.at[w],
            device_id=(x, y, 1 - c), device_id_type=MESH) for w in range(n)]
        for cp in copies:
            cp.start()
        for cp in copies:
            cp.wait()

    return pl.pallas_call(
        body,
        in_specs=[ANY] * n,
        out_specs=[ANY] * n,
        out_shape=[jax.ShapeDtypeStruct(p.shape, p.dtype) for p in parts],
        scratch_shapes=[pltpu.SemaphoreType.DMA((n,)), pltpu.SemaphoreType.DMA((n,))],
        name="swap_with_sibling",
    )(*parts)


class _Carry:
    def __init__(self, ins, out_shapes, plan, n1, plan2=None, n2=0, aliases=None):
        self.ins, self.out_shapes, self.plan, self.n1 = list(ins), list(out_shapes), plan, n1
        self.plan2, self.n2, self.aliases = plan2, n2, dict(aliases or {})

    def scratch(self):
        s = [pltpu.SemaphoreType.DMA((self.n1,)), pltpu.SemaphoreType.DMA((self.n1,))]
        if self.plan2 is not None:
            s += [pltpu.SemaphoreType.DMA((self.n2,)), pltpu.SemaphoreType.DMA((self.n2,))]
        return s

    @staticmethod
    def _copy(src, dst, peer, send_sems, recv_sems, i):
        return pltpu.make_async_remote_copy(src_ref=src, dst_ref=dst, send_sem=send_sems.at[i],
                                            recv_sem=recv_sems.at[i], device_id=peer, device_id_type=MESH)

    def start(self, in_refs, out_refs, sems):
        remote = self.plan(in_refs, out_refs)
        assert len(remote) == self.n1, (len(remote), self.n1)
        for i, (s, d, peer, _) in enumerate(remote):
            self._copy(s, d, peer, sems[0], sems[1], i).start()

    def finish(self, in_refs, out_refs, sems):
        remote = self.plan(in_refs, out_refs)
        for i, (s, _, peer, landing) in enumerate(remote):
            self._copy(s, landing, peer, sems[0], sems[1], i).wait_recv()
        if self.plan2 is not None:
            second = self.plan2(in_refs, out_refs)
            assert len(second) == self.n2, (len(second), self.n2)
            for i, (s, d, peer, _) in enumerate(second):
                self._copy(s, d, peer, sems[2], sems[3], i).start()
            for i, (s, _, peer, landing) in enumerate(second):
                self._copy(s, landing, peer, sems[2], sems[3], i).wait_recv()
            for i, (s, d, peer, _) in enumerate(second):
                self._copy(s, d, peer, sems[2], sems[3], i).wait_send()
        for i, (s, d, peer, _) in enumerate(remote):
            self._copy(s, d, peer, sems[0], sems[1], i).wait_send()


def _run_carry(name, carry):
    n_in, n_out = len(carry.ins), len(carry.out_shapes)

    def body(*refs):
        in_refs, out_refs, sems = refs[:n_in], refs[n_in:n_in + n_out], refs[n_in + n_out:]
        carry.start(in_refs, out_refs, sems)
        carry.finish(in_refs, out_refs, sems)

    return pl.pallas_call(
        body,
        in_specs=[ANY] * n_in,
        out_specs=[ANY] * n_out,
        out_shape=carry.out_shapes,
        scratch_shapes=carry.scratch(),
        input_output_aliases=carry.aliases,
        name=name,
    )(*carry.ins)


def _comm_call(name, ins, out_shapes, plan, n_local, n_remote, aliases=None):
    assert n_local == 0
    return _run_carry(name, _Carry(ins, out_shapes, lambda i, o: plan(i, o)[1], n_remote, aliases=aliases))


def _gather_carry(arrays, items):
    shapes = [a.shape for a in arrays]

    def window(ref, item):
        idx, d, part, nparts = item
        h = shapes[idx][d] // (2 * N_CHIPS)
        rows = shapes[idx][1 - d] // nparts

        def win(j):
            sl = [None, None]
            sl[d] = pl.ds(j * h, h)
            sl[1 - d] = pl.ds(part * rows, rows)
            return ref.at[tuple(sl)]

        return win

    def plan1(ins, outs):
        (_, _, c), k, peers, chips = _place()
        remote = []
        for item in items:
            win = window(outs[item[0]], item)
            for p in range(3):
                remote.append((win(2 * k + c), win(2 * k + c), peers[p], win(2 * chips[p] + c)))
        return remote

    def plan2(ins, outs):
        (x, y, c), _, _, chips = _place()
        remote = []
        for item in items:
            win = window(outs[item[0]], item)
            for p in range(3):
                remote.append((win(2 * chips[p] + c), win(2 * chips[p] + c), (x, y, 1 - c),
                               win(2 * chips[p] + 1 - c)))
        return remote

    n = 3 * len(items)
    return _Carry(arrays, [jax.ShapeDtypeStruct(a.shape, a.dtype) for a in arrays], plan1, n, plan2, n,
                  aliases={i: i for i in range(len(arrays))})


def _place_shard(name, w, layer, dim, chip):
    _, a, b = w.shape
    full = (a * N_CHIPS, b) if dim == 0 else (a, b * N_CHIPS)
    tr, tc = _div_tile(a, 512, 16), _div_tile(b, 2048)
    nr, nc = a // tr, b // tc

    def out_map(i, j, chip_ref):
        return (chip_ref[0] * nr + i, j) if dim == 0 else (i, chip_ref[0] * nc + j)

    def body(chip_ref, w_ref, o_ref):
        o_ref[...] = w_ref[...].astype(o_ref.dtype)

    return pl.pallas_call(
        body,
        grid_spec=pltpu.PrefetchScalarGridSpec(
            num_scalar_prefetch=1, grid=(nr, nc),
            in_specs=[pl.BlockSpec((None, tr, tc), lambda i, j, chip_ref: (layer, i, j))],
            out_specs=pl.BlockSpec((tr, tc), out_map)),
        out_shape=jax.ShapeDtypeStruct(full, BF16),
        compiler_params=_cparams(("parallel", "parallel")),
        name=name,
    )(chip, w)


def _half_shape(shape, dim):
    return shape[:dim] + (shape[dim] // (2 * N_CHIPS),) + shape[dim + 1:]


def _swap_carry(grads, dims):
    shapes = [jax.ShapeDtypeStruct((N_CHIPS,) + _half_shape(g.shape, d), g.dtype) for g, d in zip(grads, dims)]

    def plan(ins, outs):
        (x, y, c), _, _, _ = _place()
        remote = []
        for w, d in enumerate(dims):
            h = grads[w].shape[d] // (2 * N_CHIPS)
            for q in range(N_CHIPS):
                remote.append((_window(ins[w], d, 2 * q + 1 - c, h), outs[w].at[q], (x, y, 1 - c), outs[w].at[q]))
        return remote

    return _Carry(grads, shapes, plan, N_CHIPS * len(grads))


def _scatter_carry(sums):
    def plan(ins, outs):
        _, _, peers, chips = _place()
        remote = []
        for w in range(len(sums)):
            for p in range(3):
                remote.append((ins[w].at[chips[p]], outs[w].at[p], peers[p], outs[w].at[p]))
        return remote

    return _Carry(sums, [jax.ShapeDtypeStruct((3,) + s.shape[1:], s.dtype) for s in sums], plan, 3 * len(sums))


def _merge_carries(carries):
    carries = [c for c in carries if c is not None]
    if len(carries) <= 1:
        return carries[0] if carries else None
    ins = [a for c in carries for a in c.ins]
    outs = [s for c in carries for s in c.out_shapes]

    def plan(in_refs, out_refs):
        remote, i0, o0 = [], 0, 0
        for c in carries:
            remote += c.plan(in_refs[i0:i0 + len(c.ins)], out_refs[o0:o0 + len(c.out_shapes)])
            i0, o0 = i0 + len(c.ins), o0 + len(c.out_shapes)
        return remote

    assert all(c.plan2 is None and not c.aliases for c in carries)
    return _Carry(ins, outs, plan, sum(c.n1 for c in carries))


def _add_halves(name, g, got, dim, core):
    R, C = g.shape
    if dim == 1:
        r, cc = R, C // (2 * N_CHIPS)
    else:
        r, cc = R // (2 * N_CHIPS), C
    tr, tc = _div_tile(r, 512, 16), _div_tile(cc, 1024)
    nr, nc = r // tr, cc // tc

    def g_map(q, i, j, core_ref):
        w = 2 * q + core_ref[0]
        return (i, w * nc + j) if dim == 1 else (w * nr + i, j)

    def body(core_ref, g_ref, got_ref, o_ref):
        o_ref[...] = (g_ref[...].astype(F32) + got_ref[...].astype(F32)).astype(o_ref.dtype)

    slab = pl.BlockSpec((None, tr, tc), lambda q, i, j, core_ref: (q, i, j))
    return pl.pallas_call(
        body,
        grid_spec=pltpu.PrefetchScalarGridSpec(
            num_scalar_prefetch=1, grid=(N_CHIPS, nr, nc),
            in_specs=[pl.BlockSpec((tr, tc), g_map), slab], out_specs=slab),
        out_shape=jax.ShapeDtypeStruct((N_CHIPS, r, cc), g.dtype),
        compiler_params=_cparams(("parallel", "parallel", "parallel")),
        name=name,
    )(core, g, got)


def _reduce_into(name, sums, land, acc, layer, dim, shape, where):
    _, r, cc = sums.shape
    tr, tc = _div_tile(r, 512, 16), _div_tile(cc, 1024)
    nr, nc = r // tr, cc // tc

    def out_map(i, j, s):
        return (layer, s[1] * nr + i, j) if dim == 1 else (layer, i, s[1] * nc + j)

    def body(*refs):
        own, a_ref, b_ref, c_ref, o_ref = refs[1], refs[2], refs[3], refs[4], refs[-1]
        o_ref[...] = ((own[...].astype(F32) + a_ref[...].astype(F32)) + b_ref[...].astype(F32)) \
            + c_ref[...].astype(F32)

    def slot(p):
        return pl.BlockSpec((None, tr, tc), lambda i, j, s: (p, i, j))

    in_specs = [pl.BlockSpec((None, tr, tc), lambda i, j, s: (s[0], i, j)), slot(0), slot(1), slot(2)]
    args = [where, sums, land, land, land]
    if acc is not None:
        in_specs.append(ANY)
        args.append(acc)
    return pl.pallas_call(
        body,
        grid_spec=pltpu.PrefetchScalarGridSpec(
            num_scalar_prefetch=1, grid=(nr, nc), in_specs=in_specs,
            out_specs=pl.BlockSpec((None, tr, tc), out_map)),
        out_shape=jax.ShapeDtypeStruct(shape, F32),
        input_output_aliases={5: 0} if acc is not None else {},
        compiler_params=_cparams(("parallel", "parallel")),
        name=name,
    )(*args)


def _join_halves(name, grads, dims):
    n = len(grads)

    def plan(ins, outs):
        (x, y, c), _, _, _ = _place()
        remote = []
        for w in range(n):
            d, h = dims[w], grads[w].shape[dims[w]] // 2
            mine = _window(outs[w], d, c, h)
            remote.append((mine, mine, (x, y, 1 - c), _window(outs[w], d, 1 - c, h)))
        return [], remote

    return _comm_call(name, grads, [jax.ShapeDtypeStruct(g.shape, g.dtype) for g in grads], plan, 0, n,
                      aliases={w: w for w in range(n)})


def _allreduce_small(pack):
    R, C = pack.shape

    def gather_body(in_ref, slots_ref, send_sems, recv_sems, local_sem):
        x, y, c = lax.axis_index("x"), lax.axis_index("y"), lax.axis_index("c")
        me = 4 * x + 2 * y + c
        flips = [(dx, dy, dc) for dx in (0, 1) for dy in (0, 1) for dc in (0, 1)][1:]

        def flip(v, d):
            return 1 - v if d else v

        local = pltpu.make_async_copy(in_ref, slots_ref.at[me], local_sem)
        local.start()
        sends = []
        for j, (dx, dy, dc) in enumerate(flips):
            px, py, pc = flip(x, dx), flip(y, dy), flip(c, dc)
            sends.append((pltpu.make_async_remote_copy(
                src_ref=in_ref, dst_ref=slots_ref.at[me], send_sem=send_sems.at[j], recv_sem=recv_sems.at[j],
                device_id=(px, py, pc), device_id_type=MESH), 4 * px + 2 * py + pc, j))
        for cp, _, _ in sends:
            cp.start()
        for cp, peer_id, j in sends:
            pltpu.make_async_remote_copy(
                src_ref=in_ref, dst_ref=slots_ref.at[peer_id], send_sem=send_sems.at[j], recv_sem=recv_sems.at[j],
                device_id=(x, y, c), device_id_type=MESH).wait_recv()
        for cp, _, _ in sends:
            cp.wait_send()
        local.wait()

    slots = pl.pallas_call(
        gather_body,
        in_specs=[ANY],
        out_specs=ANY,
        out_shape=jax.ShapeDtypeStruct((N_DEV, R, C), pack.dtype),
        scratch_shapes=[pltpu.SemaphoreType.DMA((N_DEV - 1,)), pltpu.SemaphoreType.DMA((N_DEV - 1,)),
                        pltpu.SemaphoreType.DMA],
        name="allgather_small",
    )(pack)

    def sum_body(s_ref, o_ref):
        acc = s_ref[0]
        for d in range(1, N_DEV):
            acc = acc + s_ref[d]
        o_ref[...] = acc

    return pl.pallas_call(
        sum_body,
        out_shape=jax.ShapeDtypeStruct((R, C), pack.dtype),
        name="sum_small",
    )(slots)


def _adamw_math(w, g, m, v):
    m2 = ADAM_B1 * m + (1.0 - ADAM_B1) * g
    v2 = ADAM_B2 * v + (1.0 - ADAM_B2) * (g * g)
    m_hat = m2 / (1.0 - ADAM_B1 ** ADAM_STEP)
    v_hat = v2 / (1.0 - ADAM_B2 ** ADAM_STEP)
    delta = -ADAM_LR * (m_hat / (jnp.sqrt(v_hat) + ADAM_EPS) + ADAM_WD * w)
    return delta, m2, v2


def _adamw_big(name, w, m, v, g_parts):
    shape = w.shape
    C = shape[-1]
    R = w.size // C
    tr, tc = _div_tile(R, 256, 8), _div_tile(C, 1024)
    n_g = len(g_parts)

    def body(*refs):
        w_ref, m_ref, v_ref = refs[:3]
        g_ref, d_ref, nm_ref, nv_ref = refs[3 + n_g:]
        g = refs[3][...]
        for extra in refs[4:3 + n_g]:
            g = g + extra[...]
        delta, m2, v2 = _adamw_math(w_ref[...], g, m_ref[...], v_ref[...])
        g_ref[...], d_ref[...], nm_ref[...], nv_ref[...] = g, delta, m2, v2

    blk = pl.BlockSpec((tr, tc), lambda i, j: (i, j))
    outs = pl.pallas_call(
        body,
        grid=(R // tr, C // tc),
        in_specs=[blk] * (3 + n_g),
        out_specs=[blk] * 4,
        out_shape=[jax.ShapeDtypeStruct((R, C), F32)] * 4,
        compiler_params=_cparams(("parallel", "parallel")),
        name=name,
    )(w.reshape(R, C), m.reshape(R, C), v.reshape(R, C), *[g.reshape(R, C) for g in g_parts])
    return [o.reshape(shape) for o in outs]


def _adamw_small(name, w, g, m, v):
    shape = w.shape
    two_d = (w.size // shape[-1], shape[-1])

    def body(w_ref, g_ref, m_ref, v_ref, d_ref, nm_ref, nv_ref):
        d_ref[...], nm_ref[...], nv_ref[...] = _adamw_math(w_ref[...], g_ref[...], m_ref[...], v_ref[...])

    outs = pl.pallas_call(
        body,
        out_shape=[jax.ShapeDtypeStruct(two_d, F32)] * 3,
        name=name,
    )(w.reshape(two_d), g.reshape(two_d), m.reshape(two_d), v.reshape(two_d))
    return [o.reshape(shape) for o in outs]


SMALL_ROWS = 40
S_BIN, S_G1, S_G2 = 16, 24, 32
MATS = ("w_in", "w_pa", "w_pb", "w_o", "w_mlp1", "w_mlp2")
LRU = ("lru_wr", "lru_wi")
BIG_DIM = dict(w_in=2, w_pa=1, w_pb=1, w_o=1, w_mlp1=2, w_mlp2=1, lru_wr=2, lru_wi=2)
WEIGHTS = ("norm1_g", "w_in", "b_in", "conv_a_w", "conv_a_b", "lru_wr", "lru_br", "lru_wi", "lru_bi", "lru_lam",
           "conv_b_w", "w_pa", "w_pb", "w_o", "norm2_g", "w_mlp1", "w_mlp2", "final_g")


def _rows_at(a, r0, total):
    pad = [(0, 0)] * a.ndim
    pad[-2] = (r0, total - r0 - a.shape[-2])
    return jnp.pad(a, pad)


def kernel(x, norm1_g, w_in, b_in, conv_a_w, conv_a_b, lru_wr, lru_br, lru_wi, lru_bi, lru_lam, conv_b_w, w_pa, w_pb, w_o, norm2_g, w_mlp1, w_mlp2, final_g, loss_target, m_norm1_g, m_w_in, m_b_in, m_conv_a_w, m_conv_a_b, m_lru_wr, m_lru_br, m_lru_wi, m_lru_bi, m_lru_lam, m_conv_b_w, m_w_pa, m_w_pb, m_w_o, m_norm2_g, m_w_mlp1, m_w_mlp2, m_final_g, v_norm1_g, v_w_in, v_b_in, v_conv_a_w, v_conv_a_b, v_lru_wr, v_lru_br, v_lru_wi, v_lru_bi, v_lru_lam, v_conv_b_w, v_w_pa, v_w_pb, v_w_o, v_norm2_g, v_w_mlp1, v_w_mlp2, v_final_g):
    wts = dict(norm1_g=norm1_g, w_in=w_in, b_in=b_in, conv_a_w=conv_a_w, conv_a_b=conv_a_b, lru_wr=lru_wr,
               lru_br=lru_br, lru_wi=lru_wi, lru_bi=lru_bi, lru_lam=lru_lam, conv_b_w=conv_b_w, w_pa=w_pa,
               w_pb=w_pb, w_o=w_o, norm2_g=norm2_g, w_mlp1=w_mlp1, w_mlp2=w_mlp2, final_g=final_g)
    mom = dict(norm1_g=m_norm1_g, w_in=m_w_in, b_in=m_b_in, conv_a_w=m_conv_a_w, conv_a_b=m_conv_a_b,
               lru_wr=m_lru_wr, lru_br=m_lru_br, lru_wi=m_lru_wi, lru_bi=m_lru_bi, lru_lam=m_lru_lam,
               conv_b_w=m_conv_b_w, w_pa=m_w_pa, w_pb=m_w_pb, w_o=m_w_o, norm2_g=m_norm2_g, w_mlp1=m_w_mlp1,
               w_mlp2=m_w_mlp2, final_g=m_final_g)
    vel = dict(norm1_g=v_norm1_g, w_in=v_w_in, b_in=v_b_in, conv_a_w=v_conv_a_w, conv_a_b=v_conv_a_b,
               lru_wr=v_lru_wr, lru_br=v_lru_br, lru_wi=v_lru_wi, lru_bi=v_lru_bi, lru_lam=v_lru_lam,
               conv_b_w=v_conv_b_w, w_pa=v_w_pa, w_pb=v_w_pb, w_o=v_w_o, norm2_g=v_norm2_g, w_mlp1=v_w_mlp1,
               w_mlp2=v_w_mlp2, final_g=v_final_g)
    depth, D = norm1_g.shape
    nb, bw = lru_wr.shape[1], lru_wr.shape[3]
    chip = 2 * lax.axis_index("x") + lax.axis_index("y")

    small_parts = [conv_a_w.reshape(-1), conv_b_w.reshape(-1), lru_br.reshape(-1), lru_bi.reshape(-1)]
    small_len = sum(p.shape[0] for p in small_parts)
    small_rows = -(-small_len // 1024) * 8
    small = jnp.concatenate(small_parts + [jnp.zeros((small_rows * 128 - small_len,), F32)]).reshape(small_rows, 128)
    gathered = _gather_weights([wts[n].astype(BF16) for n in LRU], [BIG_DIM[n] for n in LRU], small)
    full = dict(zip(LRU, gathered[:-1]))
    items = [(n, l) for l in range(depth) for n in MATS]
    mat_dims = [BIG_DIM[n] - 1 for n, _ in items]
    where = jnp.stack([chip, lax.axis_index("c")]).astype(jnp.int32)
    placed = {(n, l): _place_shard(f"place_{n}_{l}", wts[n], l, BIG_DIM[n] - 1, where) for n, l in items}
    flat = gathered[-1].reshape(N_CHIPS, small_rows * 128)
    off = 0
    small_full = []
    for part, shard in zip(small_parts, (conv_a_w, conv_b_w, lru_br, lru_bi)):
        piece = flat[:, off:off + part.shape[0]].reshape((N_CHIPS,) + shard.shape)
        small_full.append(jnp.moveaxis(piece, 0, -2).reshape(shard.shape[:-1] + (N_CHIPS * shard.shape[-1],)))
        off += part.shape[0]
    caw_f, cbw_f, br_f, bi_f = small_full
    pch = (_rows_at(conv_a_b[:, None, :], R_CAB, R_ROWS) + _rows_at(br_f.reshape(depth, 1, D), R_BR, R_ROWS)
           + _rows_at(bi_f.reshape(depth, 1, D), R_BI, R_ROWS) + _rows_at(lru_lam[:, None, :], R_LAM, R_ROWS)
           + _rows_at(caw_f, R_CAW, R_ROWS) + _rows_at(cbw_f, R_CBW, R_ROWS))
    W = dict(b_in=b_in, pch=pch, wr=full["lru_wr"], wi=full["lru_wi"], g1=norm1_g, g2=norm2_g, gf=final_g,
             core=lax.axis_index("c").astype(jnp.int32).reshape(1))

    loss_local, dx, grads, dgf, sums, landed = _local_fwd_bwd(x[0], loss_target[0], W, placed)
    loss = lax.psum(loss_local, ("x", "y", "c"))

    key = dict(w_in="w_in", w_pa="w_pa", w_pb="w_pb", w_o="w_o", w_mlp1="w1", w_mlp2="w2", lru_wr="wr", lru_wi="wi")
    out_g, out_d, out_m, out_v = {}, {}, {}, {}
    per_layer = [[grads[l][key[n]].astype(BF16) for l in range(depth)] for n in LRU]
    land = _scatter_grads(per_layer, [BIG_DIM[n] - 1 for n in LRU])
    chip_sums = [_sum_slots(f"sum_slots_{n}", ld.reshape(N_CHIPS, -1, ld.shape[-1])) for n, ld in zip(LRU, land)]
    sib_sums = _swap_with_sibling(chip_sums)
    for n, mine, sib in zip(LRU, chip_sums, sib_sums):
        out_g[n], out_d[n], out_m[n], out_v[n] = _adamw_big(f"adamw_{n}", wts[n], mom[n], vel[n], [mine, sib])
    acc = {n: None for n in MATS}
    for n, l in reversed(items):
        acc[n] = _reduce_into(f"reduce_{n}_{l}", sums[n, l], landed[n, l], acc[n], l, BIG_DIM[n], wts[n].shape, where)
    joined = _join_halves("join_halves", [acc[n] for n in MATS], [BIG_DIM[n] for n in MATS])
    for n, g in zip(MATS, joined):
        out_g[n], out_d[n], out_m[n], out_v[n] = _adamw_big(f"adamw_{n}", wts[n], mom[n], vel[n], [g])

    rows = []
    for l in range(depth):
        g = grads[l]
        rows.append(_rows_at(g["sm"], 0, SMALL_ROWS) + _rows_at(g["b_in"].reshape(7, D), S_BIN, SMALL_ROWS)
                    + _rows_at(g["g1"], S_G1, SMALL_ROWS) + _rows_at(g["g2"], S_G2, SMALL_ROWS))
    rows.append(_rows_at(dgf, 0, 8))
    tot = _allreduce_small(jnp.concatenate(rows, axis=0))
    per = tot[:depth * SMALL_ROWS].reshape(depth, SMALL_ROWS, D)

    def cols_of_chip(a, axis):
        size = a.shape[axis] // N_CHIPS
        return lax.dynamic_slice_in_dim(a, chip * size, size, axis=axis)

    small_g = dict(
        norm1_g=per[:, S_G1], b_in=per[:, S_BIN:S_BIN + 7].reshape(depth, 7 * D),
        conv_a_w=cols_of_chip(per[:, R_CAW:R_CAW + 4], 2), conv_a_b=per[:, R_CAB],
        lru_br=cols_of_chip(per[:, R_BR].reshape(depth, nb, bw), 2),
        lru_bi=cols_of_chip(per[:, R_BI].reshape(depth, nb, bw), 2), lru_lam=per[:, R_LAM],
        conv_b_w=cols_of_chip(per[:, R_CBW:R_CBW + 3], 2), norm2_g=per[:, S_G2],
        final_g=tot[depth * SMALL_ROWS])
    for n, g in small_g.items():
        out_g[n] = g
        out_d[n], out_m[n], out_v[n] = _adamw_small(f"adamw_{n}", wts[n], g, mom[n], vel[n])

    return (loss, dx[None], *[out_g[n] for n in WEIGHTS], *[out_d[n] for n in WEIGHTS],
            *[out_m[n] for n in WEIGHTS], *[out_v[n] for n in WEIGHTS])
```

```python
import functools

import jax
import jax.numpy as jnp
from jax import lax
from jax.experimental import pallas as pl
from jax.experimental.pallas import tpu as pltpu

F32 = jnp.float32
BF16 = jnp.bfloat16
MESH = pl.DeviceIdType.MESH

EPS = 1e-6
LRU_C = 8.0
ADAM_LR = 0.001
ADAM_B1 = 0.9
ADAM_B2 = 0.999
ADAM_EPS = 1e-08
ADAM_WD = 0.01
ADAM_STEP = 10

N_CHIPS = 4
N_DEV = 8
HALO = 8
VMEM_LIMIT = 56 * 1024 * 1024
MM_TILES = (1024, 1024, 2048)
MM_TILES_FUSED = (512, 1024, 2048)
SEQ_CHUNK = 256
MIXER_ROW_BLOCK = 32
ROW_TILE = 256

R_CAB, R_BR, R_BI, R_LAM, R_CAW, R_CBW, R_ROWS = 0, 1, 2, 3, 4, 8, 16


def _cparams(sem):
    return pltpu.CompilerParams(dimension_semantics=sem, vmem_limit_bytes=VMEM_LIMIT)


def _div_tile(n, pref, unit=128):
    if n <= pref:
        return n
    t = (pref // unit) * unit
    while n % t:
        t -= unit
    return t


def _sigmoid(v):
    return 1.0 / (1.0 + jnp.exp(-v))


def _gelu_and_grad(y):
    k = 0.7978845608028654
    c = 0.044715
    y2 = y * y
    t = jnp.tanh(k * (y + c * y2 * y))
    g = 0.5 * y * (1.0 + t)
    gp = 0.5 * (1.0 + t) + 0.5 * y * (1.0 - t * t) * (k * (1.0 + 3.0 * c * y2))
    return g, gp


def _softplus_neg(lam):
    e = jnp.exp(-jnp.abs(lam))
    w = 1.0 + e
    l1p = jnp.where(w == 1.0, e, jnp.log(w) * e / jnp.where(w == 1.0, 1.0, w - 1.0))
    return jnp.maximum(-lam, 0.0) + l1p


def _mm(name, mode, a, b, M, N, K, out_dtypes, epilogue=None, extras=(), la=None, lb=None, tiles=None,
        carry=None):
    tiles = MM_TILES if tiles is None else tiles
    tm, tn, tk = _div_tile(M, tiles[0]), _div_tile(N, tiles[1]), _div_tile(K, tiles[2])
    assert M % tm == 0 and N % tn == 0 and K % tk == 0, (name, M, N, K)
    nk = K // tk

    def spec(lead, shape, imap):
        if lead is None:
            return pl.BlockSpec(shape, imap)
        return pl.BlockSpec((None,) + shape, lambda i, j, k: (lead,) + imap(i, j, k))

    if mode == "nn":
        a_spec = spec(la, (tm, tk), lambda i, j, k: (i, k))
        b_spec = spec(lb, (tk, tn), lambda i, j, k: (k, j))
        dn = (((1,), (0,)), ((), ()))
    elif mode == "nt":
        a_spec = spec(la, (tm, tk), lambda i, j, k: (i, k))
        b_spec = spec(lb, (tn, tk), lambda i, j, k: (j, k))
        dn = (((1,), (1,)), ((), ()))
    else:
        a_spec = spec(la, (tk, tm), lambda i, j, k: (k, i))
        b_spec = spec(lb, (tk, tn), lambda i, j, k: (k, j))
        dn = (((0,), (0,)), ((), ()))

    ex_arrays, ex_specs = [], []
    for arr, kind, off in extras:
        ex_arrays.append(arr)
        if kind == "bias":
            ex_specs.append(pl.BlockSpec((1, tn), lambda i, j, k: (0, j)))
        else:
            assert off % tn == 0
            ex_specs.append(pl.BlockSpec((tm, tn), lambda i, j, k, o=off // tn: (i, j + o)))
    n_ex, n_out = len(ex_arrays), len(out_dtypes)
    n_cin = len(carry.ins) if carry else 0
    n_cout = len(carry.out_shapes) if carry else 0
    n_in = 2 + n_ex + n_cin
    gi, gj = M // tm, N // tn

    def body(*refs):
        a_ref, b_ref = refs[0], refs[1]
        ex = refs[2:2 + n_ex]
        outs = refs[n_in:n_in + n_out]
        acc = refs[n_in + n_out + n_cout]
        i, j, k = pl.program_id(0), pl.program_id(1), pl.program_id(2)
        if carry:
            c_in, c_out = refs[2 + n_ex:n_in], refs[n_in + n_out:n_in + n_out + n_cout]
            sems = refs[n_in + n_out + n_cout + 1:]

            @pl.when((i == 0) & (j == 0) & (k == 0))
            def _():
                carry.start(c_in, c_out, sems)

        def product():
            return lax.dot_general(a_ref[...], b_ref[...], dn, preferred_element_type=F32)

        def finish(r):
            vals = (r,) if epilogue is None else epilogue(r, *[e[...] for e in ex])
            for o, v in zip(outs, vals):
                o[...] = v.astype(o.dtype)

        if nk == 1:
            finish(product())
        else:
            @pl.when(k == 0)
            def _():
                acc[...] = product()

            @pl.when((k > 0) & (k < nk - 1))
            def _():
                acc[...] += product()

            @pl.when(k == nk - 1)
            def _():
                finish(acc[...] + product())

        if carry:
            @pl.when((i == gi - 1) & (j == gj - 1) & (k == nk - 1))
            def _():
                carry.finish(c_in, c_out, sems)

    res = pl.pallas_call(
        body,
        grid=(gi, gj, nk),
        in_specs=[a_spec, b_spec, *ex_specs] + [ANY] * n_cin,
        out_specs=[pl.BlockSpec((tm, tn), lambda i, j, k: (i, j)) for _ in range(n_out)] + [ANY] * n_cout,
        out_shape=[jax.ShapeDtypeStruct((M, N), d) for d in out_dtypes] + (carry.out_shapes if carry else []),
        scratch_shapes=[pltpu.VMEM((tm, tn) if nk > 1 else (8, 128), F32)] + (carry.scratch() if carry else []),
        input_output_aliases={2 + n_ex + ci: n_out + co for ci, co in carry.aliases.items()} if carry else {},
        compiler_params=_cparams(("arbitrary",) * 3 if carry else ("parallel", "parallel", "arbitrary")),
        name=name,
    )(a, b, *ex_arrays, *(carry.ins if carry else []))
    main = res[0] if n_out == 1 else res[:n_out]
    return (main, res[n_out:]) if carry else main


def _rms_fwd(name, x, g_row):
    T, D = x.shape
    tm = min(ROW_TILE, T)

    def body(x_ref, g_ref, h_ref):
        xv = x_ref[...]
        r = lax.rsqrt(jnp.mean(xv * xv, axis=-1, keepdims=True) + EPS)
        h_ref[...] = (xv * r * g_ref[...]).astype(BF16)

    return pl.pallas_call(
        body,
        grid=(T // tm,),
        in_specs=[pl.BlockSpec((tm, D), lambda i: (i, 0)), pl.BlockSpec((1, D), lambda i: (0, 0))],
        out_specs=pl.BlockSpec((tm, D), lambda i: (i, 0)),
        out_shape=jax.ShapeDtypeStruct((T, D), BF16),
        compiler_params=_cparams(("parallel",)),
        name=name,
    )(x, g_row)


def _rms_bwd(name, x, g_row, dh, dres):
    T, D = x.shape
    tm = min(ROW_TILE, T)

    def body(x_ref, g_ref, dh_ref, dres_ref, dx_ref, dxb_ref, dg_ref):
        xv, dhv = x_ref[...], dh_ref[...]
        r = lax.rsqrt(jnp.mean(xv * xv, axis=-1, keepdims=True) + EPS)
        gd = g_ref[...] * dhv
        c = jnp.mean(xv * gd, axis=-1, keepdims=True)
        dx = r * gd - xv * (r * r * r) * c + dres_ref[...]
        dx_ref[...] = dx
        dxb_ref[...] = dx.astype(BF16)

        @pl.when(pl.program_id(0) == 0)
        def _():
            dg_ref[...] = jnp.zeros_like(dg_ref)

        dg_ref[...] += jnp.sum(dhv * xv * r, axis=0, keepdims=True)

    row = pl.BlockSpec((tm, D), lambda i: (i, 0))
    vec = pl.BlockSpec((1, D), lambda i: (0, 0))
    return pl.pallas_call(
        body,
        grid=(T // tm,),
        in_specs=[row, vec, row, row],
        out_specs=[row, row, vec],
        out_shape=[jax.ShapeDtypeStruct((T, D), F32), jax.ShapeDtypeStruct((T, D), BF16),
                   jax.ShapeDtypeStruct((1, D), F32)],
        compiler_params=_cparams(("arbitrary",)),
        name=name,
    )(x, g_row, dh, dres)


def _loss_head(name, x, g_row, tgt):
    T, D = x.shape
    tm = min(ROW_TILE, T)

    def body(x_ref, g_ref, t_ref, dx_ref, dxb_ref, dg_ref, loss_ref):
        xv, g = x_ref[...], g_ref[...]
        r = lax.rsqrt(jnp.mean(xv * xv, axis=-1, keepdims=True) + EPS)
        xh = xv * r
        e = xh * g - t_ref[...]
        lpart = 0.5 * jnp.sum(jnp.mean(e * e, axis=-1, keepdims=True))
        dy = e * (1.0 / D)
        gd = g * dy
        c = jnp.mean(xv * gd, axis=-1, keepdims=True)
        dx = r * gd - xv * (r * r * r) * c
        dx_ref[...] = dx
        dxb_ref[...] = dx.astype(BF16)

        @pl.when(pl.program_id(0) == 0)
        def _():
            dg_ref[...] = jnp.zeros_like(dg_ref)
            loss_ref[...] = jnp.zeros_like(loss_ref)

        dg_ref[...] += jnp.sum(dy * xh, axis=0, keepdims=True)
        loss_ref[...] += jnp.full(loss_ref.shape, lpart, F32)

    row = pl.BlockSpec((tm, D), lambda i: (i, 0))
    vec = pl.BlockSpec((1, D), lambda i: (0, 0))
    return pl.pallas_call(
        body,
        grid=(T // tm,),
        in_specs=[row, vec, row],
        out_specs=[row, row, vec, pl.BlockSpec((8, 128), lambda i: (0, 0))],
        out_shape=[jax.ShapeDtypeStruct((T, D), F32), jax.ShapeDtypeStruct((T, D), BF16),
                   jax.ShapeDtypeStruct((1, D), F32), jax.ShapeDtypeStruct((8, 128), F32)],
        compiler_params=_cparams(("arbitrary",)),
        name=name,
    )(x, g_row, tgt)


def _colsum(name, a, carry=None):
    T, N = a.shape
    tm, tn = min(512, T), _div_tile(N, 2048)
    gj, gi = N // tn, T // tm
    n_cin = len(carry.ins) if carry else 0
    n_cout = len(carry.out_shapes) if carry else 0

    def body(*refs):
        a_ref, o_ref = refs[0], refs[1 + n_cin]
        j, i = pl.program_id(0), pl.program_id(1)
        if carry:
            c_in, c_out, sems = refs[1:1 + n_cin], refs[2 + n_cin:2 + n_cin + n_cout], refs[2 + n_cin + n_cout:]

            @pl.when((j == 0) & (i == 0))
            def _():
                carry.start(c_in, c_out, sems)

        @pl.when(i == 0)
        def _():
            o_ref[...] = jnp.zeros_like(o_ref)

        o_ref[...] += jnp.sum(a_ref[...].astype(F32), axis=0, keepdims=True)

        if carry:
            @pl.when((j == gj - 1) & (i == gi - 1))
            def _():
                carry.finish(c_in, c_out, sems)

    res = pl.pallas_call(
        body,
        grid=(gj, gi),
        in_specs=[pl.BlockSpec((tm, tn), lambda j, i: (i, j))] + [ANY] * n_cin,
        out_specs=[pl.BlockSpec((1, tn), lambda j, i: (0, j))] + [ANY] * n_cout,
        out_shape=[jax.ShapeDtypeStruct((1, N), F32)] + (carry.out_shapes if carry else []),
        scratch_shapes=carry.scratch() if carry else [],
        compiler_params=_cparams(("arbitrary", "arbitrary") if carry else ("parallel", "arbitrary")),
        name=name,
    )(a, *(carry.ins if carry else []))
    return (res[0], res[1:]) if carry else res[0]


def _tile_scan(a, b, row, reverse):
    for s in (1, 2, 4):
        if reverse:
            a_s, b_s, m = pltpu.roll(a, 8 - s, 0), pltpu.roll(b, 8 - s, 0), row < 8 - s
        else:
            a_s, b_s, m = pltpu.roll(a, s, 0), pltpu.roll(b, s, 0), row >= s
        b = jnp.where(m, a * b_s + b, b)
        a = jnp.where(m, a * a_s, a)
    return a, b


def _chunk_scan(a_s, b_s, out_ref, carry, n_tiles, width, reverse):
    row = lax.broadcasted_iota(jnp.int32, (8, width), 0)
    edge = 0 if reverse else 7

    group = 4 if n_tiles % 4 == 0 else 1

    def step(j, c):
        jj = (n_tiles // group - 1 - j) if reverse else j
        base = pl.multiple_of(jj * (8 * group), 8 * group)
        order = range(group - 1, -1, -1) if reverse else range(group)
        parts = {t: _tile_scan(a_s[pl.ds(base + 8 * t, 8), :], b_s[pl.ds(base + 8 * t, 8), :], row, reverse)
                 for t in order}
        for t in order:
            h = parts[t][0] * c + parts[t][1]
            out_ref[pl.ds(base + 8 * t, 8), :] = h
            c = jnp.broadcast_to(h[edge:edge + 1, :], (8, width))
        return c

    carry[...] = lax.fori_loop(0, n_tiles // group, step, carry[...])


def _gates(xc, p_ref, wr_ref, wi_ref):
    xcb = xc.astype(BF16)
    r = _sigmoid(jnp.dot(xcb, wr_ref[...], preferred_element_type=F32) + p_ref[R_BR:R_BR + 1, :])
    ig = _sigmoid(jnp.dot(xcb, wi_ref[...], preferred_element_type=F32) + p_ref[R_BI:R_BI + 1, :])
    sp = _softplus_neg(p_ref[R_LAM:R_LAM + 1, :])
    log_a = (-LRU_C) * r * sp
    a = jnp.exp(log_a)
    t = jnp.tanh(log_a)
    mult = jnp.sqrt(-2.0 * t / (1.0 - t))
    return xcb, r, ig, sp, a, mult


def _mixer_specs(Tc, bw, nb, layer):
    def seg(s):
        return pl.BlockSpec((Tc, bw), lambda n, i: (i, s * nb + n))

    p_spec = pl.BlockSpec((None, R_ROWS, bw), lambda n, i: (layer, 0, n))
    w_spec = pl.BlockSpec((None, None, bw, bw), lambda n, i: (layer, n, 0, 0))
    return seg, p_spec, w_spec


def _mixer_fwd(name, layer, z, pch, wr, wi, comm=None):
    T, D = z.shape[0], z.shape[1] // 7
    bw, nb = wr.shape[-1], wr.shape[1]
    Tc = min(SEQ_CHUNK, T)
    nT = T // Tc
    n_cin = len(comm.ins) if comm else 0
    n_cout = len(comm.out_shapes) if comm else 0

    def body(*refs):
        xa_ref, ya_ref, cb_ref, cc_ref, cx_ref, p_ref, wr_ref, wi_ref = refs[:8]
        pa_ref, pb_ref, xc_ref, hl_ref, vb_ref = refs[8 + n_cin:13 + n_cin]
        xa_buf, u_buf, a_s, b_s, carry = refs[13 + n_cin + n_cout:18 + n_cin + n_cout]
        if comm:
            c_in, c_out, sems = refs[8:8 + n_cin], refs[13 + n_cin:13 + n_cin + n_cout], refs[18 + n_cin + n_cout:]

            @pl.when((pl.program_id(0) == 0) & (pl.program_id(1) == 0))
            def _():
                comm.start(c_in, c_out, sems)

        @pl.when(pl.program_id(1) == 0)
        def _():
            xa_buf[0:HALO, :] = jnp.zeros((HALO, bw), F32)
            u_buf[0:HALO, :] = jnp.zeros((HALO, bw), F32)
            carry[...] = jnp.zeros_like(carry)

        xa_buf[HALO:HALO + Tc, :] = xa_ref[...]
        xc = p_ref[R_CAB:R_CAB + 1, :]
        for k in range(4):
            xc = xc + p_ref[R_CAW + k:R_CAW + k + 1, :] * xa_buf[HALO - 3 + k:HALO - 3 + k + Tc, :]
        xc_ref[...] = xc
        _, _, ig, _, a, mult = _gates(xc, p_ref, wr_ref, wi_ref)
        a_s[...] = a
        b_s[...] = mult * (ig * xc)
        _chunk_scan(a_s, b_s, hl_ref, carry, Tc // 8, bw, False)
        g, _ = _gelu_and_grad(ya_ref[...])
        pa_ref[...] = (hl_ref[...] * g).astype(BF16)

        u_buf[HALO:HALO + Tc, :] = cc_ref[...] * cx_ref[...]
        vb = jnp.zeros((Tc, bw), F32)
        for k in range(3):
            vb = vb + p_ref[R_CBW + k:R_CBW + k + 1, :] * u_buf[HALO - 2 + k:HALO - 2 + k + Tc, :]
        vb_ref[...] = vb
        pb_ref[...] = (cb_ref[...] * vb).astype(BF16)
        xa_buf[0:HALO, :] = xa_buf[Tc:Tc + HALO, :]
        u_buf[0:HALO, :] = u_buf[Tc:Tc + HALO, :]

        if comm:
            @pl.when((pl.program_id(0) == nb - 1) & (pl.program_id(1) == nT - 1))
            def _():
                comm.finish(c_in, c_out, sems)

    seg, p_spec, w_spec = _mixer_specs(Tc, bw, nb, layer)
    out = pl.BlockSpec((Tc, bw), lambda n, i: (i, n))
    res = pl.pallas_call(
        body,
        grid=(nb, nT),
        in_specs=[seg(0), seg(1), seg(2), seg(3), seg(4), p_spec, w_spec, w_spec] + [ANY] * n_cin,
        out_specs=[out] * 5 + [ANY] * n_cout,
        out_shape=[jax.ShapeDtypeStruct((T, D), BF16), jax.ShapeDtypeStruct((T, D), BF16),
                   jax.ShapeDtypeStruct((T, D), F32), jax.ShapeDtypeStruct((T, D), F32),
                   jax.ShapeDtypeStruct((T, D), F32)] + (comm.out_shapes if comm else []),
        scratch_shapes=[pltpu.VMEM((Tc + HALO, bw), F32), pltpu.VMEM((Tc + HALO, bw), F32),
                        pltpu.VMEM((Tc, bw), F32), pltpu.VMEM((Tc, bw), F32), pltpu.VMEM((8, bw), F32)]
        + (comm.scratch() if comm else []),
        input_output_aliases={8 + ci: 5 + co for ci, co in comm.aliases.items()} if comm else {},
        compiler_params=_cparams(("arbitrary", "arbitrary") if comm else ("parallel", "arbitrary")),
        name=name,
    )(z, z, z, z, z, pch, wr, wi, *(comm.ins if comm else []))
    return (res[:5], res[5:]) if comm else res


def _mixer_bwd(name, layer, z, xc, hl, vb, dpa, dpb, pch, wr, wi, comm=None):
    T, D = z.shape[0], z.shape[1] // 7
    bw, nb = wr.shape[-1], wr.shape[1]
    Tc = min(SEQ_CHUNK, T)
    nT = T // Tc
    tpc = Tc // 8
    rb = min(MIXER_ROW_BLOCK, Tc)
    n_cin = len(comm.ins) if comm else 0
    n_cout = len(comm.out_shapes) if comm else 0

    def body(*refs):
        (xa_ref, ya_ref, cb_ref, cc_ref, cx_ref, xc_ref, hl_ref, hp_ref, vb_ref, dpa_ref, dpb_ref,
         p_ref, wr_ref, wi_ref) = refs[:14]
        dxa_ref, dya_ref, dcb_ref, dcc_ref, dcx_ref, dwr_ref, dwi_ref, sm_ref = refs[14 + n_cin:22 + n_cin]
        (h_buf, a_buf, dxc_buf, dvb_buf, a_s, d_s, lam_s, r_s, i_s, m_s, dpr_s, dpi_s, sm8,
         carry) = refs[22 + n_cin + n_cout:36 + n_cin + n_cout]
        i = pl.program_id(1)
        if comm:
            c_in, c_out = refs[14:14 + n_cin], refs[22 + n_cin:22 + n_cin + n_cout]
            sems = refs[36 + n_cin + n_cout:]

            @pl.when((pl.program_id(0) == 0) & (i == 0))
            def _():
                comm.start(c_in, c_out, sems)

        @pl.when(i == 0)
        def _():
            a_buf[Tc:Tc + HALO, :] = jnp.zeros((HALO, bw), F32)
            dxc_buf[Tc:Tc + HALO, :] = jnp.zeros((HALO, bw), F32)
            dvb_buf[Tc:Tc + HALO, :] = jnp.zeros((HALO, bw), F32)
            carry[...] = jnp.zeros_like(carry)
            dwr_ref[...] = jnp.zeros_like(dwr_ref)
            dwi_ref[...] = jnp.zeros_like(dwi_ref)
            sm8[...] = jnp.zeros_like(sm8)

        blocks = [(g * rb, slice(g * rb, (g + 1) * rb)) for g in range(Tc // rb)]

        def prow(k):
            return p_ref[k:k + 1, :]

        def part8(v):
            return jnp.sum(v.reshape(rb // 8, 8, bw), axis=0)

        sums8 = {}

        def tally(k, v):
            sums8[k] = sums8[k] + part8(v) if k in sums8 else part8(v)

        sp = _softplus_neg(prow(R_LAM))
        xcb = xc_ref[...].astype(BF16)
        r_s[...] = jnp.dot(xcb, wr_ref[...], preferred_element_type=F32)
        i_s[...] = jnp.dot(xcb, wi_ref[...], preferred_element_type=F32)
        for _, rs in blocks:
            r = _sigmoid(r_s[rs, :] + prow(R_BR))
            ig = _sigmoid(i_s[rs, :] + prow(R_BI))
            log_a = (-LRU_C) * r * sp
            t = jnp.tanh(log_a)
            r_s[rs, :] = r
            i_s[rs, :] = ig
            m_s[rs, :] = jnp.sqrt(-2.0 * t / (1.0 - t))
            a_buf[rs, :] = jnp.exp(log_a)
            g, gp = _gelu_and_grad(ya_ref[rs, :])
            dpav = dpa_ref[rs, :]
            d_s[rs, :] = dpav * g
            dya_ref[rs, :] = (dpav * hl_ref[rs, :] * gp).astype(BF16)
            dpbv = dpb_ref[rs, :]
            dcb_ref[rs, :] = (dpbv * vb_ref[rs, :]).astype(BF16)
            dvb_buf[rs, :] = dpbv * cb_ref[rs, :]

        a_s[...] = a_buf[1:Tc + 1, :]
        _chunk_scan(a_s, d_s, lam_s, carry, tpc, bw, True)
        h_buf[HALO:HALO + Tc, :] = hl_ref[...]
        h_buf[0:HALO, :] = jnp.where(i == nT - 1, 0.0, hp_ref[...])

        for r0, rs in blocks:
            lamv, xcv, r, ig, a = lam_s[rs, :], xc_ref[rs, :], r_s[rs, :], i_s[rs, :], a_buf[rs, :]
            mult = m_s[rs, :]
            da = lamv * h_buf[HALO - 1 + r0:HALO - 1 + r0 + rb, :]
            dmult = lamv * (ig * xcv)
            dbx = lamv * mult
            dig = dbx * xcv
            dxc_buf[rs, :] = dbx * ig
            dlog_a = da * a - dmult * (a * a) / mult
            dpr = (dlog_a * ((-LRU_C) * sp)) * r * (1.0 - r)
            dpi = dig * ig * (1.0 - ig)
            dpr_s[rs, :] = dpr.astype(BF16)
            dpi_s[rs, :] = dpi.astype(BF16)
            tally(R_BR, dpr)
            tally(R_BI, dpi)
            tally(R_LAM, dlog_a * ((-LRU_C) * r))

        dprb, dpib = dpr_s[...], dpi_s[...]
        nt = (((1,), (1,)), ((), ()))
        tn = (((0,), (0,)), ((), ()))
        dxc_buf[0:Tc, :] += (lax.dot_general(dprb, wr_ref[...], nt, preferred_element_type=F32)
                             + lax.dot_general(dpib, wi_ref[...], nt, preferred_element_type=F32))
        dwr_ref[...] += lax.dot_general(xcb, dprb, tn, preferred_element_type=F32)
        dwi_ref[...] += lax.dot_general(xcb, dpib, tn, preferred_element_type=F32)

        for r0, rs in blocks:
            xav = xa_ref[rs, :]
            tally(R_CAB, dxc_buf[rs, :])
            dxa = jnp.zeros((rb, bw), F32)
            for k in range(4):
                sh = dxc_buf[3 - k + r0:3 - k + r0 + rb, :]
                dxa = dxa + prow(R_CAW + k) * sh
                tally(R_CAW + k, xav * sh)
            dxa_ref[rs, :] = dxa.astype(BF16)
            ccv, cxv = cc_ref[rs, :], cx_ref[rs, :]
            u = ccv * cxv
            du = jnp.zeros((rb, bw), F32)
            for k in range(3):
                sh = dvb_buf[2 - k + r0:2 - k + r0 + rb, :]
                du = du + prow(R_CBW + k) * sh
                tally(R_CBW + k, u * sh)
            dcc_ref[rs, :] = (du * cxv).astype(BF16)
            dcx_ref[rs, :] = (du * ccv).astype(BF16)

        for k, v in sums8.items():
            sm8[8 * k:8 * k + 8, :] += v
        a_buf[Tc:Tc + HALO, :] = a_buf[0:HALO, :]
        dxc_buf[Tc:Tc + HALO, :] = dxc_buf[0:HALO, :]
        dvb_buf[Tc:Tc + HALO, :] = dvb_buf[0:HALO, :]

        @pl.when(i == nT - 1)
        def _():
            sm_ref[...] = jnp.sum(sm8[...].reshape(R_ROWS, 8, bw), axis=1)
            sm_ref[R_LAM:R_LAM + 1, :] = sm_ref[R_LAM:R_LAM + 1, :] * (-_sigmoid(-prow(R_LAM)))

        if comm:
            @pl.when((pl.program_id(0) == nb - 1) & (i == nT - 1))
            def _():
                comm.finish(c_in, c_out, sems)

    def seg(s):
        return pl.BlockSpec((Tc, bw), lambda n, i: (nT - 1 - i, s * nb + n))

    blk = pl.BlockSpec((Tc, bw), lambda n, i: (nT - 1 - i, n))
    halo = pl.BlockSpec((8, bw), lambda n, i: (jnp.maximum((nT - 1 - i) * tpc - 1, 0), n))
    p_spec = pl.BlockSpec((None, R_ROWS, bw), lambda n, i: (layer, 0, n))
    w_spec = pl.BlockSpec((None, None, bw, bw), lambda n, i: (layer, n, 0, 0))
    dw_spec = pl.BlockSpec((None, bw, bw), lambda n, i: (n, 0, 0))
    act = jax.ShapeDtypeStruct((T, D), BF16)
    res = pl.pallas_call(
        body,
        grid=(nb, nT),
        in_specs=[seg(0), seg(1), seg(2), seg(3), seg(4), blk, blk, halo, blk, blk, blk, p_spec, w_spec, w_spec]
        + [ANY] * n_cin,
        out_specs=[blk] * 5 + [dw_spec, dw_spec, pl.BlockSpec((R_ROWS, bw), lambda n, i: (0, n))] + [ANY] * n_cout,
        out_shape=[act] * 5 + [jax.ShapeDtypeStruct((nb, bw, bw), F32), jax.ShapeDtypeStruct((nb, bw, bw), F32),
                               jax.ShapeDtypeStruct((R_ROWS, D), F32)] + (comm.out_shapes if comm else []),
        scratch_shapes=[pltpu.VMEM((Tc + HALO, bw), F32)] * 4 + [pltpu.VMEM((Tc, bw), F32)] * 6
        + [pltpu.VMEM((Tc, bw), BF16)] * 2 + [pltpu.VMEM((8 * R_ROWS, bw), F32), pltpu.VMEM((8, bw), F32)]
        + (comm.scratch() if comm else []),
        input_output_aliases={14 + ci: 8 + co for ci, co in comm.aliases.items()} if comm else {},
        compiler_params=_cparams(("arbitrary", "arbitrary") if comm else ("parallel", "arbitrary")),
        name=name,
    )(z, z, z, z, z, xc, hl, hl, vb, dpa, dpb, pch, wr, wi, *(comm.ins if comm else []))
    return (res[:8], res[8:]) if comm else res


def _local_fwd_bwd(x, tgt, W, placed=None):
    T, D = x.shape
    g1, g2 = W["g1"], W["g2"]
    depth = g1.shape[0]
    FF = 4 * D
    if placed is None:
        mats = {(n, l): W[n][l] for n in MATS for l in range(depth)}
    else:
        mats = {}
        mats["w_in", 0], = _run_carry("gather_first", _gather_carry([placed["w_in", 0]], [(0, 1, 0, 1)]))

    def gathering(specs):
        if placed is None or not specs:
            return None, []
        keys = [(n, l) for n, l, _, _, _ in specs]
        arrays = [mats.get(k, placed[k]) for k in keys]
        return _gather_carry(arrays, [(i, d, part, nparts) for i, (_, _, d, part, nparts) in enumerate(specs)]), keys

    def hosted(call, specs, **kw):
        carry, keys = gathering(specs)
        if carry is None:
            return call(**kw)
        res, got = call(**kw, **{("comm" if call.func is _mixer_fwd else "carry"): carry})
        mats.update(zip(keys, got))
        return res

    saved = []
    xs = x
    for l in range(depth):
        h = _rms_fwd(f"rms1_fwd_{l}", xs, g1[l][None])
        nxt = l + 1 < depth
        projs = [("w_pa", l, 0, 0, 1), ("w_pb", l, 0, 0, 1), ("w_o", l, 0, 0, 1)]
        z = hosted(functools.partial(_mm, f"in_proj_{l}", "nn", h, mats["w_in", l], T, 7 * D, D, [F32]),
                   projs + ([("w_mlp1", l, 1, 0, 1)] if nxt else []),
                   epilogue=lambda acc, b: (acc + b,), extras=[(W["b_in"][l][None], "bias", 0)])
        pa, pb, xc, hl, vb = hosted(
            functools.partial(_mixer_fwd, f"mixer_fwd_{l}", l, z, W["pch"], W["wr"], W["wi"]),
            [("w_mlp2", l, 0, 0, 1)] if nxt else [("w_mlp1", l, 1, 0, 1)])
        oa = _mm(f"proj_a_{l}", "nn", pa, mats["w_pa", l], T, D, D, [F32])

        def merge(acc, oav, ga, gb):
            return acc, _sigmoid(ga) * oav + _sigmoid(gb) * acc

        ob, mg = _mm(f"proj_b_merge_{l}", "nn", pb, mats["w_pb", l], T, D, D, [F32, BF16], epilogue=merge,
                     tiles=MM_TILES_FUSED,
                     extras=[(oa, "tile", 0), (z, "tile", 5 * D), (z, "tile", 6 * D)])
        x1 = _mm(f"out_proj_{l}", "nn", mg, mats["w_o", l], T, D, D, [F32],
                 epilogue=lambda acc, res: (res + acc,), extras=[(xs, "tile", 0)])
        h2 = _rms_fwd(f"rms2_fwd_{l}", x1, g2[l][None])

        def relu2(acc):
            pr = jnp.maximum(acc, 0.0)
            return pr * pr, pr

        u, pr = hosted(functools.partial(_mm, f"mlp1_{l}", "nn", h2, mats["w_mlp1", l], T, FF, D, [BF16, BF16]),
                       [("w_in", l + 1, 1, 0, 2)] if nxt else [("w_mlp2", l, 0, 0, 1)], epilogue=relu2)
        x2 = hosted(functools.partial(_mm, f"mlp2_{l}", "nn", u, mats["w_mlp2", l], T, D, FF, [F32]),
                    [("w_in", l + 1, 1, 1, 2)] if nxt else [],
                    epilogue=lambda acc, res: (res + acc,), extras=[(x1, "tile", 0)])
        saved.append(dict(x0=xs, h=h, z=z, pa=pa, pb=pb, xc=xc, hl=hl, vb=vb, oa=oa, ob=ob, mg=mg, x1=x1,
                          h2=h2, u=u, pr=pr))
        xs = x2

    dx, dxb, dgf, loss_blk = _loss_head("loss_head", xs, W["gf"][None], tgt)

    gmat, got, sums, landed = {}, {}, {}, {}

    def reducing(call, swaps=(), scatters=(), mixer=False, **kw):
        swaps, scatters = [(n, l) for n in swaps], [(n, l) for n in scatters]
        if placed is None:
            return call(**kw)
        cs = _swap_carry([gmat[k] for k in swaps], [BIG_DIM[k[0]] - 1 for k in swaps]) if swaps else None
        cc = _scatter_carry([sums[k] for k in scatters]) if scatters else None
        res, moved = call(**kw, **{("comm" if mixer else "carry"): _merge_carries([cs, cc])})
        got.update(zip(swaps, moved[:len(swaps)]))
        landed.update(zip(scatters, moved[len(swaps):]))
        return res

    def add(names):
        if placed is not None:
            for n in names:
                sums[n, l] = _add_halves(f"add_halves_{n}_{l}", gmat[n, l], got[n, l], BIG_DIM[n] - 1, W["core"])

    grads = [None] * depth
    for l in reversed(range(depth)):
        s = saved[l]
        dp = _mm(f"mlp2_dx_{l}", "nt", dxb, mats["w_mlp2", l], T, FF, D, [BF16],
                 epilogue=lambda acc, prv: (2.0 * prv.astype(F32) * acc,), extras=[(s["pr"], "tile", 0)])
        dw2 = gmat["w_mlp2", l] = _mm(f"mlp2_dw_{l}", "tn", s["u"], dxb, FF, D, T, [BF16])
        dh2 = reducing(functools.partial(_mm, f"mlp1_dx_{l}", "nt", dp, mats["w_mlp1", l], T, D, FF, [F32]),
                       swaps=["w_mlp2"])
        add(["w_mlp2"])
        dw1 = gmat["w_mlp1", l] = reducing(functools.partial(_mm, f"mlp1_dw_{l}", "tn", s["h2"], dp, D, FF, T, [BF16]),
                                           scatters=["w_mlp2"])
        dx1, dx1b, dg2 = _rms_bwd(f"rms2_bwd_{l}", s["x1"], g2[l][None], dh2, dx)

        def unmerge(acc, ga, gb, oav, obv):
            sa, sb = _sigmoid(ga), _sigmoid(gb)
            return acc * sa, acc * sb, acc * oav * sa * (1.0 - sa), acc * obv * sb * (1.0 - sb)

        doa, dob, dga, dgb = reducing(
            functools.partial(_mm, f"out_proj_dx_{l}", "nt", dx1b, mats["w_o", l], T, D, D, [BF16] * 4),
            swaps=["w_mlp1"], tiles=MM_TILES_FUSED, epilogue=unmerge,
            extras=[(s["z"], "tile", 5 * D), (s["z"], "tile", 6 * D), (s["oa"], "tile", 0), (s["ob"], "tile", 0)])
        add(["w_mlp1"])
        dwo = gmat["w_o", l] = _mm(f"out_proj_dw_{l}", "tn", s["mg"], dx1b, D, D, T, [BF16])
        dpa = _mm(f"proj_a_dx_{l}", "nt", doa, mats["w_pa", l], T, D, D, [F32])
        dwpa = gmat["w_pa", l] = _mm(f"proj_a_dw_{l}", "tn", s["pa"], doa, D, D, T, [BF16])
        dpb = _mm(f"proj_b_dx_{l}", "nt", dob, mats["w_pb", l], T, D, D, [F32])
        dwpb = gmat["w_pb", l] = _mm(f"proj_b_dw_{l}", "tn", s["pb"], dob, D, D, T, [BF16])
        dxa, dya, dcb, dcc, dcx, dwr, dwi, sm = reducing(
            functools.partial(_mixer_bwd, f"mixer_bwd_{l}", l, s["z"], s["xc"], s["hl"], s["vb"], dpa, dpb,
                              W["pch"], W["wr"], W["wi"]),
            swaps=["w_o", "w_pa", "w_pb"], scatters=["w_mlp1"], mixer=True)
        add(["w_o", "w_pa", "w_pb"])
        dz = jnp.concatenate([dxa, dya, dcb, dcc, dcx, dga, dgb], axis=1)
        dwin = gmat["w_in", l] = reducing(
            functools.partial(_mm, f"in_proj_dw_{l}", "tn", s["h"], dz, D, 7 * D, T, [BF16]),
            scatters=["w_o", "w_pa", "w_pb"])
        dbin = reducing(functools.partial(_colsum, f"bias_grad_{l}", dz), swaps=["w_in"])
        add(["w_in"])
        dh = reducing(functools.partial(_mm, f"in_proj_dx_{l}", "nt", dz, mats["w_in", l], T, D, 7 * D, [F32]),
                      scatters=["w_in"])
        dx, dxb, dg1 = _rms_bwd(f"rms1_bwd_{l}", s["x0"], g1[l][None], dh, dx1)
        grads[l] = dict(w_in=dwin, w_pa=dwpa, w_pb=dwpb, w_o=dwo, w_mlp1=dw1, w_mlp2=dw2, wr=dwr, wi=dwi,
                        sm=sm, b_in=dbin, g1=dg1, g2=dg2)
    return loss_blk[0, 0], dx, grads, dgf, sums, landed


ANY = pl.BlockSpec(memory_space=pl.ANY)


def _place():
    x, y, c = lax.axis_index("x"), lax.axis_index("y"), lax.axis_index("c")
    peers = [(1 - x, y, c), (x, 1 - y, c), (1 - x, 1 - y, c)]
    chips = [2 * (1 - x) + y, 2 * x + (1 - y), 2 * (1 - x) + (1 - y)]
    return (x, y, c), 2 * x + y, peers, chips


def _window(ref, dim, q, size):
    idx = [slice(None)] * len(ref.shape)
    idx[dim] = pl.ds(q * size, size)
    return ref.at[tuple(idx)]


def _gather_weights(shards, dims, small):
    n = len(shards)
    sizes = [s.shape[d] for s, d in zip(shards, dims)]
    full = [jax.ShapeDtypeStruct(s.shape[:d] + (s.shape[d] * N_CHIPS,) + s.shape[d + 1:], s.dtype)
            for s, d in zip(shards, dims)]
    full.append(jax.ShapeDtypeStruct((N_CHIPS,) + small.shape, small.dtype))

    def body(*refs):
        ins, outs = refs[:n + 1], refs[n + 1:2 * n + 2]
        send_sems, recv_sems, local_sems = refs[2 * n + 2:]
        _, k, peers, chips = _place()

        def dst(w, q):
            return outs[w].at[q] if w == n else _window(outs[w], dims[w], q, sizes[w])

        local = [pltpu.make_async_copy(ins[w], dst(w, k), local_sems.at[w]) for w in range(n + 1)]
        for cp in local:
            cp.start()
        sends = []
        for p, peer in enumerate(peers):
            for w in range(n + 1):
                s = p * (n + 1) + w
                sends.append(pltpu.make_async_remote_copy(
                    src_ref=ins[w], dst_ref=dst(w, k), send_sem=send_sems.at[s], recv_sem=recv_sems.at[s],
                    device_id=peer, device_id_type=MESH))
        for cp in sends:
            cp.start()
        for p, peer in enumerate(peers):
            for w in range(n + 1):
                s = p * (n + 1) + w
                pltpu.make_async_remote_copy(
                    src_ref=ins[w], dst_ref=dst(w, chips[p]), send_sem=send_sems.at[s], recv_sem=recv_sems.at[s],
                    device_id=peer, device_id_type=MESH).wait_recv()
        for cp in sends:
            cp.wait_send()
        for cp in local:
            cp.wait()

    return pl.pallas_call(
        body,
        in_specs=[ANY] * (n + 1),
        out_specs=[ANY] * (n + 1),
        out_shape=full,
        scratch_shapes=[pltpu.SemaphoreType.DMA((3 * (n + 1),)), pltpu.SemaphoreType.DMA((3 * (n + 1),)),
                        pltpu.SemaphoreType.DMA((n + 1,))],
        name="gather_weights",
    )(*shards, small)


def _scatter_grads(grads, dims):
    n, depth = len(grads), len(grads[0])
    sizes = [g[0].shape[d] // N_CHIPS for g, d in zip(grads, dims)]
    land = []
    for g, d, sz in zip(grads, dims, sizes):
        shp = g[0].shape
        land.append(jax.ShapeDtypeStruct((N_CHIPS, depth) + shp[:d] + (sz,) + shp[d + 1:], g[0].dtype))

    def body(*refs):
        ins, outs = refs[:n * depth], refs[n * depth:n * depth + n]
        send_sems, recv_sems, local_sems = refs[n * depth + n:]
        _, k, peers, chips = _place()

        def src(w, l, q):
            return _window(ins[w * depth + l], dims[w], q, sizes[w])

        local = [pltpu.make_async_copy(src(w, l, k), outs[w].at[3, l], local_sems.at[w * depth + l])
                 for w in range(n) for l in range(depth)]
        for cp in local:
            cp.start()
        sends = []
        for p, peer in enumerate(peers):
            for w in range(n):
                for l in range(depth):
                    s = (p * n + w) * depth + l
                    sends.append(pltpu.make_async_remote_copy(
                        src_ref=src(w, l, chips[p]), dst_ref=outs[w].at[p, l], send_sem=send_sems.at[s],
                        recv_sem=recv_sems.at[s], device_id=peer, device_id_type=MESH))
        for cp in sends:
            cp.start()
        for cp in sends:
            cp.wait_recv()
        for cp in sends:
            cp.wait_send()
        for cp in local:
            cp.wait()

    flat = [g for gl in grads for g in gl]
    return pl.pallas_call(
        body,
        in_specs=[ANY] * (n * depth),
        out_specs=[ANY] * n,
        out_shape=land,
        scratch_shapes=[pltpu.SemaphoreType.DMA((3 * n * depth,)), pltpu.SemaphoreType.DMA((3 * n * depth,)),
                        pltpu.SemaphoreType.DMA((n * depth,))],
        name="scatter_grads",
    )(*flat)


def _sum_slots(name, land):
    _, R, C = land.shape
    tr, tc = _div_tile(R, 512, 8), _div_tile(C, 1024)

    def body(a_ref, b_ref, c_ref, d_ref, o_ref):
        o_ref[...] = ((d_ref[...].astype(F32) + a_ref[...].astype(F32)) + b_ref[...].astype(F32)) \
            + c_ref[...].astype(F32)

    def slot(q):
        return pl.BlockSpec((None, tr, tc), lambda i, j: (q, i, j))

    return pl.pallas_call(
        body,
        grid=(R // tr, C // tc),
        in_specs=[slot(0), slot(1), slot(2), slot(3)],
        out_specs=pl.BlockSpec((tr, tc), lambda i, j: (i, j)),
        out_shape=jax.ShapeDtypeStruct((R, C), F32),
        compiler_params=_cparams(("parallel", "parallel")),
        name=name,
    )(land, land, land, land)


def _swap_with_sibling(parts):
    n = len(parts)

    def body(*refs):
        ins, outs = refs[:n], refs[n:2 * n]
        send_sems, recv_sems = refs[2 * n:]
        (x, y, c), _, _, _ = _place()
        copies = [pltpu.make_async_remote_copy(
            src_ref=ins[w], dst_ref=outs[w], send_sem=send_sems.at[w], recv_sem=recv_sems.at[w],
            device_id=(x, y, 1 - c), device_id_type=MESH) for w in range(n)]
        for cp in copies:
            cp.start()
        for cp in copies:
            cp.wait()

    return pl.pallas_call(
        body,
        in_specs=[ANY] * n,
        out_specs=[ANY] * n,
        out_shape=[jax.ShapeDtypeStruct(p.shape, p.dtype) for p in parts],
        scratch_shapes=[pltpu.SemaphoreType.DMA((n,)), pltpu.SemaphoreType.DMA((n,))],
        name="swap_with_sibling",
    )(*parts)


class _Carry:
    def __init__(self, ins, out_shapes, plan, n1, plan2=None, n2=0, aliases=None):
        self.ins, self.out_shapes, self.plan, self.n1 = list(ins), list(out_shapes), plan, n1
        self.plan2, self.n2, self.aliases = plan2, n2, dict(aliases or {})

    def scratch(self):
        s = [pltpu.SemaphoreType.DMA((self.n1,)), pltpu.SemaphoreType.DMA((self.n1,))]
        if self.plan2 is not None:
            s += [pltpu.SemaphoreType.DMA((self.n2,)), pltpu.SemaphoreType.DMA((self.n2,))]
        return s

    @staticmethod
    def _copy(src, dst, peer, send_sems, recv_sems, i):
        return pltpu.make_async_remote_copy(src_ref=src, dst_ref=dst, send_sem=send_sems.at[i],
                                            recv_sem=recv_sems.at[i], device_id=peer, device_id_type=MESH)

    def start(self, in_refs, out_refs, sems):
        remote = self.plan(in_refs, out_refs)
        assert len(remote) == self.n1, (len(remote), self.n1)
        for i, (s, d, peer, _) in enumerate(remote):
            self._copy(s, d, peer, sems[0], sems[1], i).start()

    def finish(self, in_refs, out_refs, sems):
        remote = self.plan(in_refs, out_refs)
        for i, (s, _, peer, landing) in enumerate(remote):
            self._copy(s, landing, peer, sems[0], sems[1], i).wait_recv()
        if self.plan2 is not None:
            second = self.plan2(in_refs, out_refs)
            assert len(second) == self.n2, (len(second), self.n2)
            for i, (s, d, peer, _) in enumerate(second):
                self._copy(s, d, peer, sems[2], sems[3], i).start()
            for i, (s, _, peer, landing) in enumerate(second):
                self._copy(s, landing, peer, sems[2], sems[3], i).wait_recv()
            for i, (s, d, peer, _) in enumerate(second):
                self._copy(s, d, peer, sems[2], sems[3], i).wait_send()
        for i, (s, d, peer, _) in enumerate(remote):
            self._copy(s, d, peer, sems[0], sems[1], i).wait_send()


def _run_carry(name, carry):
    n_in, n_out = len(carry.ins), len(carry.out_shapes)

    def body(*refs):
        in_refs, out_refs, sems = refs[:n_in], refs[n_in:n_in + n_out], refs[n_in + n_out:]
        carry.start(in_refs, out_refs, sems)
        carry.finish(in_refs, out_refs, sems)

    return pl.pallas_call(
        body,
        in_specs=[ANY] * n_in,
        out_specs=[ANY] * n_out,
        out_shape=carry.out_shapes,
        scratch_shapes=carry.scratch(),
        input_output_aliases=carry.aliases,
        name=name,
    )(*carry.ins)


def _comm_call(name, ins, out_shapes, plan, n_local, n_remote, aliases=None):
    assert n_local == 0
    return _run_carry(name, _Carry(ins, out_shapes, lambda i, o: plan(i, o)[1], n_remote, aliases=aliases))


def _gather_carry(arrays, items):
    shapes = [a.shape for a in arrays]

    def window(ref, item):
        idx, d, part, nparts = item
        h = shapes[idx][d] // (2 * N_CHIPS)
        rows = shapes[idx][1 - d] // nparts

        def win(j):
            sl = [None, None]
            sl[d] = pl.ds(j * h, h)
            sl[1 - d] = pl.ds(part * rows, rows)
            return ref.at[tuple(sl)]

        return win

    def plan1(ins, outs):
        (_, _, c), k, peers, chips = _place()
        remote = []
        for item in items:
            win = window(outs[item[0]], item)
            for p in range(3):
                remote.append((win(2 * k + c), win(2 * k + c), peers[p], win(2 * chips[p] + c)))
        return remote

    def plan2(ins, outs):
        (x, y, c), _, _, chips = _place()
        remote = []
        for item in items:
            win = window(outs[item[0]], item)
            for p in range(3):
                remote.append((win(2 * chips[p] + c), win(2 * chips[p] + c), (x, y, 1 - c),
                               win(2 * chips[p] + 1 - c)))
        return remote

    n = 3 * len(items)
    return _Carry(arrays, [jax.ShapeDtypeStruct(a.shape, a.dtype) for a in arrays], plan1, n, plan2, n,
                  aliases={i: i for i in range(len(arrays))})


def _place_shard(name, w, layer, dim, chip):
    _, a, b = w.shape
    full = (a * N_CHIPS, b) if dim == 0 else (a, b * N_CHIPS)
    tr, tc = _div_tile(a, 512, 16), _div_tile(b, 2048)
    nr, nc = a // tr, b // tc

    def out_map(i, j, chip_ref):
        return (chip_ref[0] * nr + i, j) if dim == 0 else (i, chip_ref[0] * nc + j)

    def body(chip_ref, w_ref, o_ref):
        o_ref[...] = w_ref[...].astype(o_ref.dtype)

    return pl.pallas_call(
        body,
        grid_spec=pltpu.PrefetchScalarGridSpec(
            num_scalar_prefetch=1, grid=(nr, nc),
            in_specs=[pl.BlockSpec((None, tr, tc), lambda i, j, chip_ref: (layer, i, j))],
            out_specs=pl.BlockSpec((tr, tc), out_map)),
        out_shape=jax.ShapeDtypeStruct(full, BF16),
        compiler_params=_cparams(("parallel", "parallel")),
        name=name,
    )(chip, w)


def _half_shape(shape, dim):
    return shape[:dim] + (shape[dim] // (2 * N_CHIPS),) + shape[dim + 1:]


def _swap_carry(grads, dims):
    shapes = [jax.ShapeDtypeStruct((N_CHIPS,) + _half_shape(g.shape, d), g.dtype) for g, d in zip(grads, dims)]

    def plan(ins, outs):
        (x, y, c), _, _, _ = _place()
        remote = []
        for w, d in enumerate(dims):
            h = grads[w].shape[d] // (2 * N_CHIPS)
            for q in range(N_CHIPS):
                remote.append((_window(ins[w], d, 2 * q + 1 - c, h), outs[w].at[q], (x, y, 1 - c), outs[w].at[q]))
        return remote

    return _Carry(grads, shapes, plan, N_CHIPS * len(grads))


def _scatter_carry(sums):
    def plan(ins, outs):
        _, _, peers, chips = _place()
        remote = []
        for w in range(len(sums)):
            for p in range(3):
                remote.append((ins[w].at[chips[p]], outs[w].at[p], peers[p], outs[w].at[p]))
        return remote

    return _Carry(sums, [jax.ShapeDtypeStruct((3,) + s.shape[1:], s.dtype) for s in sums], plan, 3 * len(sums))


def _merge_carries(carries):
    carries = [c for c in carries if c is not None]
    if len(carries) <= 1:
        return carries[0] if carries else None
    ins = [a for c in carries for a in c.ins]
    outs = [s for c in carries for s in c.out_shapes]

    def plan(in_refs, out_refs):
        remote, i0, o0 = [], 0, 0
        for c in carries:
            remote += c.plan(in_refs[i0:i0 + len(c.ins)], out_refs[o0:o0 + len(c.out_shapes)])
            i0, o0 = i0 + len(c.ins), o0 + len(c.out_shapes)
        return remote

    assert all(c.plan2 is None and not c.aliases for c in carries)
    return _Carry(ins, outs, plan, sum(c.n1 for c in carries))


def _add_halves(name, g, got, dim, core):
    R, C = g.shape
    if dim == 1:
        r, cc = R, C // (2 * N_CHIPS)
    else:
        r, cc = R // (2 * N_CHIPS), C
    tr, tc = _div_tile(r, 512, 16), _div_tile(cc, 1024)
    nr, nc = r // tr, cc // tc

    def g_map(q, i, j, core_ref):
        w = 2 * q + core_ref[0]
        return (i, w * nc + j) if dim == 1 else (w * nr + i, j)

    def body(core_ref, g_ref, got_ref, o_ref):
        o_ref[...] = (g_ref[...].astype(F32) + got_ref[...].astype(F32)).astype(o_ref.dtype)

    slab = pl.BlockSpec((None, tr, tc), lambda q, i, j, core_ref: (q, i, j))
    return pl.pallas_call(
        body,
        grid_spec=pltpu.PrefetchScalarGridSpec(
            num_scalar_prefetch=1, grid=(N_CHIPS, nr, nc),
            in_specs=[pl.BlockSpec((tr, tc), g_map), slab], out_specs=slab),
        out_shape=jax.ShapeDtypeStruct((N_CHIPS, r, cc), g.dtype),
        compiler_params=_cparams(("parallel", "parallel", "parallel")),
        name=name,
    )(core, g, got)


def _reduce_into(name, sums, land, acc, layer, dim, shape, where):
    _, r, cc = sums.shape
    tr, tc = _div_tile(r, 512, 16), _div_tile(cc, 1024)
    nr, nc = r // tr, cc // tc

    def out_map(i, j, s):
        return (layer, s[1] * nr + i, j) if dim == 1 else (layer, i, s[1] * nc + j)

    def body(*refs):
        own, a_ref, b_ref, c_ref, o_ref = refs[1], refs[2], refs[3], refs[4], refs[-1]
        o_ref[...] = ((own[...].astype(F32) + a_ref[...].astype(F32)) + b_ref[...].astype(F32)) \
            + c_ref[...].astype(F32)

    def slot(p):
        return pl.BlockSpec((None, tr, tc), lambda i, j, s: (p, i, j))

    in_specs = [pl.BlockSpec((None, tr, tc), lambda i, j, s: (s[0], i, j)), slot(0), slot(1), slot(2)]
    args = [where, sums, land, land, land]
    if acc is not None:
        in_specs.append(ANY)
        args.append(acc)
    return pl.pallas_call(
        body,
        grid_spec=pltpu.PrefetchScalarGridSpec(
            num_scalar_prefetch=1, grid=(nr, nc), in_specs=in_specs,
            out_specs=pl.BlockSpec((None, tr, tc), out_map)),
        out_shape=jax.ShapeDtypeStruct(shape, F32),
        input_output_aliases={5: 0} if acc is not None else {},
        compiler_params=_cparams(("parallel", "parallel")),
        name=name,
    )(*args)


def _join_halves(name, grads, dims):
    n = len(grads)

    def plan(ins, outs):
        (x, y, c), _, _, _ = _place()
        remote = []
        for w in range(n):
            d, h = dims[w], grads[w].shape[dims[w]] // 2
            mine = _window(outs[w], d, c, h)
            remote.append((mine, mine, (x, y, 1 - c), _window(outs[w], d, 1 - c, h)))
        return [], remote

    return _comm_call(name, grads, [jax.ShapeDtypeStruct(g.shape, g.dtype) for g in grads], plan, 0, n,
                      aliases={w: w for w in range(n)})


def _allreduce_small(pack):
    R, C = pack.shape

    def gather_body(in_ref, slots_ref, send_sems, recv_sems, local_sem):
        x, y, c = lax.axis_index("x"), lax.axis_index("y"), lax.axis_index("c")
        me = 4 * x + 2 * y + c
        flips = [(dx, dy, dc) for dx in (0, 1) for dy in (0, 1) for dc in (0, 1)][1:]

        def flip(v, d):
            return 1 - v if d else v

        local = pltpu.make_async_copy(in_ref, slots_ref.at[me], local_sem)
        local.start()
        sends = []
        for j, (dx, dy, dc) in enumerate(flips):
            px, py, pc = flip(x, dx), flip(y, dy), flip(c, dc)
            sends.append((pltpu.make_async_remote_copy(
                src_ref=in_ref, dst_ref=slots_ref.at[me], send_sem=send_sems.at[j], recv_sem=recv_sems.at[j],
                device_id=(px, py, pc), device_id_type=MESH), 4 * px + 2 * py + pc, j))
        for cp, _, _ in sends:
            cp.start()
        for cp, peer_id, j in sends:
            pltpu.make_async_remote_copy(
                src_ref=in_ref, dst_ref=slots_ref.at[peer_id], send_sem=send_sems.at[j], recv_sem=recv_sems.at[j],
                device_id=(x, y, c), device_id_type=MESH).wait_recv()
        for cp, _, _ in sends:
            cp.wait_send()
        local.wait()

    slots = pl.pallas_call(
        gather_body,
        in_specs=[ANY],
        out_specs=ANY,
        out_shape=jax.ShapeDtypeStruct((N_DEV, R, C), pack.dtype),
        scratch_shapes=[pltpu.SemaphoreType.DMA((N_DEV - 1,)), pltpu.SemaphoreType.DMA((N_DEV - 1,)),
                        pltpu.SemaphoreType.DMA],
        name="allgather_small",
    )(pack)

    def sum_body(s_ref, o_ref):
        acc = s_ref[0]
        for d in range(1, N_DEV):
            acc = acc + s_ref[d]
        o_ref[...] = acc

    return pl.pallas_call(
        sum_body,
        out_shape=jax.ShapeDtypeStruct((R, C), pack.dtype),
        name="sum_small",
    )(slots)


def _adamw_math(w, g, m, v):
    m2 = ADAM_B1 * m + (1.0 - ADAM_B1) * g
    v2 = ADAM_B2 * v + (1.0 - ADAM_B2) * (g * g)
    m_hat = m2 / (1.0 - ADAM_B1 ** ADAM_STEP)
    v_hat = v2 / (1.0 - ADAM_B2 ** ADAM_STEP)
    delta = -ADAM_LR * (m_hat / (jnp.sqrt(v_hat) + ADAM_EPS) + ADAM_WD * w)
    return delta, m2, v2


def _adamw_big(name, w, m, v, g_parts):
    shape = w.shape
    C = shape[-1]
    R = w.size // C
    tr, tc = _div_tile(R, 256, 8), _div_tile(C, 1024)
    n_g = len(g_parts)

    def body(*refs):
        w_ref, m_ref, v_ref = refs[:3]
        g_ref, d_ref, nm_ref, nv_ref = refs[3 + n_g:]
        g = refs[3][...]
        for extra in refs[4:3 + n_g]:
            g = g + extra[...]
        delta, m2, v2 = _adamw_math(w_ref[...], g, m_ref[...], v_ref[...])
        g_ref[...], d_ref[...], nm_ref[...], nv_ref[...] = g, delta, m2, v2

    blk = pl.BlockSpec((tr, tc), lambda i, j: (i, j))
    outs = pl.pallas_call(
        body,
        grid=(R // tr, C // tc),
        in_specs=[blk] * (3 + n_g),
        out_specs=[blk] * 4,
        out_shape=[jax.ShapeDtypeStruct((R, C), F32)] * 4,
        compiler_params=_cparams(("parallel", "parallel")),
        name=name,
    )(w.reshape(R, C), m.reshape(R, C), v.reshape(R, C), *[g.reshape(R, C) for g in g_parts])
    return [o.reshape(shape) for o in outs]


def _adamw_small(name, w, g, m, v):
    shape = w.shape
    two_d = (w.size // shape[-1], shape[-1])

    def body(w_ref, g_ref, m_ref, v_ref, d_ref, nm_ref, nv_ref):
        d_ref[...], nm_ref[...], nv_ref[...] = _adamw_math(w_ref[...], g_ref[...], m_ref[...], v_ref[...])

    outs = pl.pallas_call(
        body,
        out_shape=[jax.ShapeDtypeStruct(two_d, F32)] * 3,
        name=name,
    )(w.reshape(two_d), g.reshape(two_d), m.reshape(two_d), v.reshape(two_d))
    return [o.reshape(shape) for o in outs]


SMALL_ROWS = 40
S_BIN, S_G1, S_G2 = 16, 24, 32
MATS = ("w_in", "w_pa", "w_pb", "w_o", "w_mlp1", "w_mlp2")
LRU = ("lru_wr", "lru_wi")
BIG_DIM = dict(w_in=2, w_pa=1, w_pb=1, w_o=1, w_mlp1=2, w_mlp2=1, lru_wr=2, lru_wi=2)
WEIGHTS = ("norm1_g", "w_in", "b_in", "conv_a_w", "conv_a_b", "lru_wr", "lru_br", "lru_wi", "lru_bi", "lru_lam",
           "conv_b_w", "w_pa", "w_pb", "w_o", "norm2_g", "w_mlp1", "w_mlp2", "final_g")


def _rows_at(a, r0, total):
    pad = [(0, 0)] * a.ndim
    pad[-2] = (r0, total - r0 - a.shape[-2])
    return jnp.pad(a, pad)


def kernel(x, norm1_g, w_in, b_in, conv_a_w, conv_a_b, lru_wr, lru_br, lru_wi, lru_bi, lru_lam, conv_b_w, w_pa, w_pb, w_o, norm2_g, w_mlp1, w_mlp2, final_g, loss_target, m_norm1_g, m_w_in, m_b_in, m_conv_a_w, m_conv_a_b, m_lru_wr, m_lru_br, m_lru_wi, m_lru_bi, m_lru_lam, m_conv_b_w, m_w_pa, m_w_pb, m_w_o, m_norm2_g, m_w_mlp1, m_w_mlp2, m_final_g, v_norm1_g, v_w_in, v_b_in, v_conv_a_w, v_conv_a_b, v_lru_wr, v_lru_br, v_lru_wi, v_lru_bi, v_lru_lam, v_conv_b_w, v_w_pa, v_w_pb, v_w_o, v_norm2_g, v_w_mlp1, v_w_mlp2, v_final_g):
    wts = dict(norm1_g=norm1_g, w_in=w_in, b_in=b_in, conv_a_w=conv_a_w, conv_a_b=conv_a_b, lru_wr=lru_wr,
               lru_br=lru_br, lru_wi=lru_wi, lru_bi=lru_bi, lru_lam=lru_lam, conv_b_w=conv_b_w, w_pa=w_pa,
               w_pb=w_pb, w_o=w_o, norm2_g=norm2_g, w_mlp1=w_mlp1, w_mlp2=w_mlp2, final_g=final_g)
    mom = dict(norm1_g=m_norm1_g, w_in=m_w_in, b_in=m_b_in, conv_a_w=m_conv_a_w, conv_a_b=m_conv_a_b,
               lru_wr=m_lru_wr, lru_br=m_lru_br, lru_wi=m_lru_wi, lru_bi=m_lru_bi, lru_lam=m_lru_lam,
               conv_b_w=m_conv_b_w, w_pa=m_w_pa, w_pb=m_w_pb, w_o=m_w_o, norm2_g=m_norm2_g, w_mlp1=m_w_mlp1,
               w_mlp2=m_w_mlp2, final_g=m_final_g)
    vel = dict(norm1_g=v_norm1_g, w_in=v_w_in, b_in=v_b_in, conv_a_w=v_conv_a_w, conv_a_b=v_conv_a_b,
               lru_wr=v_lru_wr, lru_br=v_lru_br, lru_wi=v_lru_wi, lru_bi=v_lru_bi, lru_lam=v_lru_lam,
               conv_b_w=v_conv_b_w, w_pa=v_w_pa, w_pb=v_w_pb, w_o=v_w_o, norm2_g=v_norm2_g, w_mlp1=v_w_mlp1,
               w_mlp2=v_w_mlp2, final_g=v_final_g)
    depth, D = norm1_g.shape
    nb, bw = lru_wr.shape[1], lru_wr.shape[3]
    chip = 2 * lax.axis_index("x") + lax.axis_index("y")

    small_parts = [conv_a_w.reshape(-1), conv_b_w.reshape(-1), lru_br.reshape(-1), lru_bi.reshape(-1)]
    small_len = sum(p.shape[0] for p in small_parts)
    small_rows = -(-small_len // 1024) * 8
    small = jnp.concatenate(small_parts + [jnp.zeros((small_rows * 128 - small_len,), F32)]).reshape(small_rows, 128)
    gathered = _gather_weights([wts[n].astype(BF16) for n in LRU], [BIG_DIM[n] for n in LRU], small)
    full = dict(zip(LRU, gathered[:-1]))
    items = [(n, l) for l in range(depth) for n in MATS]
    mat_dims = [BIG_DIM[n] - 1 for n, _ in items]
    where = jnp.stack([chip, lax.axis_index("c")]).astype(jnp.int32)
    placed = {(n, l): _place_shard(f"place_{n}_{l}", wts[n], l, BIG_DIM[n] - 1, where) for n, l in items}
    flat = gathered[-1].reshape(N_CHIPS, small_rows * 128)
    off = 0
    small_full = []
    for part, shard in zip(small_parts, (conv_a_w, conv_b_w, lru_br, lru_bi)):
        piece = flat[:, off:off + part.shape[0]].reshape((N_CHIPS,) + shard.shape)
        small_full.append(jnp.moveaxis(piece, 0, -2).reshape(shard.shape[:-1] + (N_CHIPS * shard.shape[-1],)))
        off += part.shape[0]
    caw_f, cbw_f, br_f, bi_f = small_full
    pch = (_rows_at(conv_a_b[:, None, :], R_CAB, R_ROWS) + _rows_at(br_f.reshape(depth, 1, D), R_BR, R_ROWS)
           + _rows_at(bi_f.reshape(depth, 1, D), R_BI, R_ROWS) + _rows_at(lru_lam[:, None, :], R_LAM, R_ROWS)
           + _rows_at(caw_f, R_CAW, R_ROWS) + _rows_at(cbw_f, R_CBW, R_ROWS))
    W = dict(b_in=b_in, pch=pch, wr=full["lru_wr"], wi=full["lru_wi"], g1=norm1_g, g2=norm2_g, gf=final_g,
             core=lax.axis_index("c").astype(jnp.int32).reshape(1))

    loss_local, dx, grads, dgf, sums, landed = _local_fwd_bwd(x[0], loss_target[0], W, placed)
    loss = lax.psum(loss_local, ("x", "y", "c"))

    key = dict(w_in="w_in", w_pa="w_pa", w_pb="w_pb", w_o="w_o", w_mlp1="w1", w_mlp2="w2", lru_wr="wr", lru_wi="wi")
    out_g, out_d, out_m, out_v = {}, {}, {}, {}
    per_layer = [[grads[l][key[n]].astype(BF16) for l in range(depth)] for n in LRU]
    land = _scatter_grads(per_layer, [BIG_DIM[n] - 1 for n in LRU])
    chip_sums = [_sum_slots(f"sum_slots_{n}", ld.reshape(N_CHIPS, -1, ld.shape[-1])) for n, ld in zip(LRU, land)]
    sib_sums = _swap_with_sibling(chip_sums)
    for n, mine, sib in zip(LRU, chip_sums, sib_sums):
        out_g[n], out_d[n], out_m[n], out_v[n] = _adamw_big(f"adamw_{n}", wts[n], mom[n], vel[n], [mine, sib])
    acc = {n: None for n in MATS}
    for n, l in reversed(items):
        acc[n] = _reduce_into(f"reduce_{n}_{l}", sums[n, l], landed[n, l], acc[n], l, BIG_DIM[n], wts[n].shape, where)
    joined = _join_halves("join_halves", [acc[n] for n in MATS], [BIG_DIM[n] for n in MATS])
    for n, g in zip(MATS, joined):
        out_g[n], out_d[n], out_m[n], out_v[n] = _adamw_big(f"adamw_{n}", wts[n], mom[n], vel[n], [g])

    rows = []
    for l in range(depth):
        g = grads[l]
        rows.append(_rows_at(g["sm"], 0, SMALL_ROWS) + _rows_at(g["b_in"].reshape(7, D), S_BIN, SMALL_ROWS)
                    + _rows_at(g["g1"], S_G1, SMALL_ROWS) + _rows_at(g["g2"], S_G2, SMALL_ROWS))
    rows.append(_rows_at(dgf, 0, 8))
    tot = _allreduce_small(jnp.concatenate(rows, axis=0))
    per = tot[:depth * SMALL_ROWS].reshape(depth, SMALL_ROWS, D)

    def cols_of_chip(a, axis):
        size = a.shape[axis] // N_CHIPS
        return lax.dynamic_slice_in_dim(a, chip * size, size, axis=axis)

    small_g = dict(
        norm1_g=per[:, S_G1], b_in=per[:, S_BIN:S_BIN + 7].reshape(depth, 7 * D),
        conv_a_w=cols_of_chip(per[:, R_CAW:R_CAW + 4], 2), conv_a_b=per[:, R_CAB],
        lru_br=cols_of_chip(per[:, R_BR].reshape(depth, nb, bw), 2),
        lru_bi=cols_of_chip(per[:, R_BI].reshape(depth, nb, bw), 2), lru_lam=per[:, R_LAM],
        conv_b_w=cols_of_chip(per[:, R_CBW:R_CBW + 3], 2), norm2_g=per[:, S_G2],
        final_g=tot[depth * SMALL_ROWS])
    for n, g in small_g.items():
        out_g[n] = g
        out_d[n], out_m[n], out_v[n] = _adamw_small(f"adamw_{n}", wts[n], g, mom[n], vel[n])

    return (loss, dx[None], *[out_g[n] for n in WEIGHTS], *[out_d[n] for n in WEIGHTS],
            *[out_m[n] for n in WEIGHTS], *[out_v[n] for n in WEIGHTS])
```

```python
import functools

import jax
import jax.numpy as jnp
from jax import lax
from jax.experimental import pallas as pl
from jax.experimental.pallas import tpu as pltpu

F32 = jnp.float32
BF16 = jnp.bfloat16
MESH = pl.DeviceIdType.MESH

EPS = 1e-6
LRU_C = 8.0
ADAM_LR = 0.001
ADAM_B1 = 0.9
ADAM_B2 = 0.999
ADAM_EPS = 1e-08
ADAM_WD = 0.01
ADAM_STEP = 10

N_CHIPS = 4
N_DEV = 8
HALO = 8
VMEM_LIMIT = 56 * 1024 * 1024
MM_TILES = (1024, 1024, 2048)
MM_TILES_FUSED = (512, 1024, 2048)
SEQ_CHUNK = 256
MIXER_ROW_BLOCK = 32
ROW_TILE = 256

R_CAB, R_BR, R_BI, R_LAM, R_CAW, R_CBW, R_ROWS = 0, 1, 2, 3, 4, 8, 16


def _cparams(sem):
    return pltpu.CompilerParams(dimension_semantics=sem, vmem_limit_bytes=VMEM_LIMIT)


def _div_tile(n, pref, unit=128):
    if n <= pref:
        return n
    t = (pref // unit) * unit
    while n % t:
        t -= unit
    return t


def _sigmoid(v):
    return 1.0 / (1.0 + jnp.exp(-v))


def _gelu_and_grad(y):
    k = 0.7978845608028654
    c = 0.044715
    y2 = y * y
    t = jnp.tanh(k * (y + c * y2 * y))
    g = 0.5 * y * (1.0 + t)
    gp = 0.5 * (1.0 + t) + 0.5 * y * (1.0 - t * t) * (k * (1.0 + 3.0 * c * y2))
    return g, gp


def _softplus_neg(lam):
    e = jnp.exp(-jnp.abs(lam))
    w = 1.0 + e
    l1p = jnp.where(w == 1.0, e, jnp.log(w) * e / jnp.where(w == 1.0, 1.0, w - 1.0))
    return jnp.maximum(-lam, 0.0) + l1p


def _mm(name, mode, a, b, M, N, K, out_dtypes, epilogue=None, extras=(), la=None, lb=None, tiles=None,
        carry=None):
    tiles = MM_TILES if tiles is None else tiles
    tm, tn, tk = _div_tile(M, tiles[0]), _div_tile(N, tiles[1]), _div_tile(K, tiles[2])
    assert M % tm == 0 and N % tn == 0 and K % tk == 0, (name, M, N, K)
    nk = K // tk

    def spec(lead, shape, imap):
        if lead is None:
            return pl.BlockSpec(shape, imap)
        return pl.BlockSpec((None,) + shape, lambda i, j, k: (lead,) + imap(i, j, k))

    if mode == "nn":
        a_spec = spec(la, (tm, tk), lambda i, j, k: (i, k))
        b_spec = spec(lb, (tk, tn), lambda i, j, k: (k, j))
        dn = (((1,), (0,)), ((), ()))
    elif mode == "nt":
        a_spec = spec(la, (tm, tk), lambda i, j, k: (i, k))
        b_spec = spec(lb, (tn, tk), lambda i, j, k: (j, k))
        dn = (((1,), (1,)), ((), ()))
    else:
        a_spec = spec(la, (tk, tm), lambda i, j, k: (k, i))
        b_spec = spec(lb, (tk, tn), lambda i, j, k: (k, j))
        dn = (((0,), (0,)), ((), ()))

    ex_arrays, ex_specs = [], []
    for arr, kind, off in extras:
        ex_arrays.append(arr)
        if kind == "bias":
            ex_specs.append(pl.BlockSpec((1, tn), lambda i, j, k: (0, j)))
        else:
            assert off % tn == 0
            ex_specs.append(pl.BlockSpec((tm, tn), lambda i, j, k, o=off // tn: (i, j + o)))
    n_ex, n_out = len(ex_arrays), len(out_dtypes)
    n_cin = len(carry.ins) if carry else 0
    n_cout = len(carry.out_shapes) if carry else 0
    n_in = 2 + n_ex + n_cin
    gi, gj = M // tm, N // tn

    def body(*refs):
        a_ref, b_ref = refs[0], refs[1]
        ex = refs[2:2 + n_ex]
        outs = refs[n_in:n_in + n_out]
        acc = refs[n_in + n_out + n_cout]
        i, j, k = pl.program_id(0), pl.program_id(1), pl.program_id(2)
        if carry:
            c_in, c_out = refs[2 + n_ex:n_in], refs[n_in + n_out:n_in + n_out + n_cout]
            sems = refs[n_in + n_out + n_cout + 1:]
            step = (i * gj + j) * nk + k
            carry.ride(step, gi * gj * nk, c_in, c_out, sems, True)

        def product():
            return lax.dot_general(a_ref[...], b_ref[...], dn, preferred_element_type=F32)

        def finish(r):
            vals = (r,) if epilogue is None else epilogue(r, *[e[...] for e in ex])
            for o, v in zip(outs, vals):
                o[...] = v.astype(o.dtype)

        if nk == 1:
            finish(product())
        else:
            @pl.when(k == 0)
            def _():
                acc[...] = product()

            @pl.when((k > 0) & (k < nk - 1))
            def _():
                acc[...] += product()

            @pl.when(k == nk - 1)
            def _():
                finish(acc[...] + product())

        if carry:
            carry.ride(step, gi * gj * nk, c_in, c_out, sems, False)

    res = pl.pallas_call(
        body,
        grid=(gi, gj, nk),
        in_specs=[a_spec, b_spec, *ex_specs] + [ANY] * n_cin,
        out_specs=[pl.BlockSpec((tm, tn), lambda i, j, k: (i, j)) for _ in range(n_out)] + [ANY] * n_cout,
        out_shape=[jax.ShapeDtypeStruct((M, N), d) for d in out_dtypes] + (carry.out_shapes if carry else []),
        scratch_shapes=[pltpu.VMEM((tm, tn) if nk > 1 else (8, 128), F32)] + (carry.scratch() if carry else []),
        input_output_aliases={2 + n_ex + ci: n_out + co for ci, co in carry.aliases.items()} if carry else {},
        compiler_params=_cparams(("arbitrary",) * 3 if carry else ("parallel", "parallel", "arbitrary")),
        name=name,
    )(a, b, *ex_arrays, *(carry.ins if carry else []))
    main = res[0] if n_out == 1 else res[:n_out]
    return (main, res[n_out:]) if carry else main


def _rms_fwd(name, x, g_row):
    T, D = x.shape
    tm = min(ROW_TILE, T)

    def body(x_ref, g_ref, h_ref):
        xv = x_ref[...]
        r = lax.rsqrt(jnp.mean(xv * xv, axis=-1, keepdims=True) + EPS)
        h_ref[...] = (xv * r * g_ref[...]).astype(BF16)

    return pl.pallas_call(
        body,
        grid=(T // tm,),
        in_specs=[pl.BlockSpec((tm, D), lambda i: (i, 0)), pl.BlockSpec((1, D), lambda i: (0, 0))],
        out_specs=pl.BlockSpec((tm, D), lambda i: (i, 0)),
        out_shape=jax.ShapeDtypeStruct((T, D), BF16),
        compiler_params=_cparams(("parallel",)),
        name=name,
    )(x, g_row)


def _rms_bwd(name, x, g_row, dh, dres):
    T, D = x.shape
    tm = min(ROW_TILE, T)

    def body(x_ref, g_ref, dh_ref, dres_ref, dx_ref, dxb_ref, dg_ref):
        xv, dhv = x_ref[...], dh_ref[...]
        r = lax.rsqrt(jnp.mean(xv * xv, axis=-1, keepdims=True) + EPS)
        gd = g_ref[...] * dhv
        c = jnp.mean(xv * gd, axis=-1, keepdims=True)
        dx = r * gd - xv * (r * r * r) * c + dres_ref[...]
        dx_ref[...] = dx
        dxb_ref[...] = dx.astype(BF16)

        @pl.when(pl.program_id(0) == 0)
        def _():
            dg_ref[...] = jnp.zeros_like(dg_ref)

        dg_ref[...] += jnp.sum(dhv * xv * r, axis=0, keepdims=True)

    row = pl.BlockSpec((tm, D), lambda i: (i, 0))
    vec = pl.BlockSpec((1, D), lambda i: (0, 0))
    return pl.pallas_call(
        body,
        grid=(T // tm,),
        in_specs=[row, vec, row, row],
        out_specs=[row, row, vec],
        out_shape=[jax.ShapeDtypeStruct((T, D), F32), jax.ShapeDtypeStruct((T, D), BF16),
                   jax.ShapeDtypeStruct((1, D), F32)],
        compiler_params=_cparams(("arbitrary",)),
        name=name,
    )(x, g_row, dh, dres)


def _loss_head(name, x, g_row, tgt):
    T, D = x.shape
    tm = min(ROW_TILE, T)

    def body(x_ref, g_ref, t_ref, dx_ref, dxb_ref, dg_ref, loss_ref):
        xv, g = x_ref[...], g_ref[...]
        r = lax.rsqrt(jnp.mean(xv * xv, axis=-1, keepdims=True) + EPS)
        xh = xv * r
        e = xh * g - t_ref[...]
        lpart = 0.5 * jnp.sum(jnp.mean(e * e, axis=-1, keepdims=True))
        dy = e * (1.0 / D)
        gd = g * dy
        c = jnp.mean(xv * gd, axis=-1, keepdims=True)
        dx = r * gd - xv * (r * r * r) * c
        dx_ref[...] = dx
        dxb_ref[...] = dx.astype(BF16)

        @pl.when(pl.program_id(0) == 0)
        def _():
            dg_ref[...] = jnp.zeros_like(dg_ref)
            loss_ref[...] = jnp.zeros_like(loss_ref)

        dg_ref[...] += jnp.sum(dy * xh, axis=0, keepdims=True)
        loss_ref[...] += jnp.full(loss_ref.shape, lpart, F32)

    row = pl.BlockSpec((tm, D), lambda i: (i, 0))
    vec = pl.BlockSpec((1, D), lambda i: (0, 0))
    return pl.pallas_call(
        body,
        grid=(T // tm,),
        in_specs=[row, vec, row],
        out_specs=[row, row, vec, pl.BlockSpec((8, 128), lambda i: (0, 0))],
        out_shape=[jax.ShapeDtypeStruct((T, D), F32), jax.ShapeDtypeStruct((T, D), BF16),
                   jax.ShapeDtypeStruct((1, D), F32), jax.ShapeDtypeStruct((8, 128), F32)],
        compiler_params=_cparams(("arbitrary",)),
        name=name,
    )(x, g_row, tgt)


def _colsum(name, a, carry=None):
    T, N = a.shape
    tm, tn = min(512, T), _div_tile(N, 2048)
    gj, gi = N // tn, T // tm
    n_cin = len(carry.ins) if carry else 0
    n_cout = len(carry.out_shapes) if carry else 0

    def body(*refs):
        a_ref, o_ref = refs[0], refs[1 + n_cin]
        j, i = pl.program_id(0), pl.program_id(1)
        if carry:
            c_in, c_out, sems = refs[1:1 + n_cin], refs[2 + n_cin:2 + n_cin + n_cout], refs[2 + n_cin + n_cout:]
            carry.ride(j * gi + i, gj * gi, c_in, c_out, sems, True)

        @pl.when(i == 0)
        def _():
            o_ref[...] = jnp.zeros_like(o_ref)

        o_ref[...] += jnp.sum(a_ref[...].astype(F32), axis=0, keepdims=True)

        if carry:
            carry.ride(j * gi + i, gj * gi, c_in, c_out, sems, False)

    res = pl.pallas_call(
        body,
        grid=(gj, gi),
        in_specs=[pl.BlockSpec((tm, tn), lambda j, i: (i, j))] + [ANY] * n_cin,
        out_specs=[pl.BlockSpec((1, tn), lambda j, i: (0, j))] + [ANY] * n_cout,
        out_shape=[jax.ShapeDtypeStruct((1, N), F32)] + (carry.out_shapes if carry else []),
        scratch_shapes=carry.scratch() if carry else [],
        compiler_params=_cparams(("arbitrary", "arbitrary") if carry else ("parallel", "arbitrary")),
        name=name,
    )(a, *(carry.ins if carry else []))
    return (res[0], res[1:]) if carry else res[0]


def _tile_scan(a, b, row, reverse):
    for s in (1, 2, 4):
        if reverse:
            a_s, b_s, m = pltpu.roll(a, 8 - s, 0), pltpu.roll(b, 8 - s, 0), row < 8 - s
        else:
            a_s, b_s, m = pltpu.roll(a, s, 0), pltpu.roll(b, s, 0), row >= s
        b = jnp.where(m, a * b_s + b, b)
        a = jnp.where(m, a * a_s, a)
    return a, b


def _chunk_scan(a_s, b_s, out_ref, carry, n_tiles, width, reverse):
    row = lax.broadcasted_iota(jnp.int32, (8, width), 0)
    edge = 0 if reverse else 7

    group = 4 if n_tiles % 4 == 0 else 1

    def step(j, c):
        jj = (n_tiles // group - 1 - j) if reverse else j
        base = pl.multiple_of(jj * (8 * group), 8 * group)
        order = range(group - 1, -1, -1) if reverse else range(group)
        parts = {t: _tile_scan(a_s[pl.ds(base + 8 * t, 8), :], b_s[pl.ds(base + 8 * t, 8), :], row, reverse)
                 for t in order}
        for t in order:
            h = parts[t][0] * c + parts[t][1]
            out_ref[pl.ds(base + 8 * t, 8), :] = h
            c = jnp.broadcast_to(h[edge:edge + 1, :], (8, width))
        return c

    carry[...] = lax.fori_loop(0, n_tiles // group, step, carry[...])


def _gates(xc, p_ref, wr_ref, wi_ref):
    xcb = xc.astype(BF16)
    r = _sigmoid(jnp.dot(xcb, wr_ref[...], preferred_element_type=F32) + p_ref[R_BR:R_BR + 1, :])
    ig = _sigmoid(jnp.dot(xcb, wi_ref[...], preferred_element_type=F32) + p_ref[R_BI:R_BI + 1, :])
    sp = _softplus_neg(p_ref[R_LAM:R_LAM + 1, :])
    log_a = (-LRU_C) * r * sp
    a = jnp.exp(log_a)
    t = jnp.tanh(log_a)
    mult = jnp.sqrt(-2.0 * t / (1.0 - t))
    return xcb, r, ig, sp, a, mult


def _mixer_specs(Tc, bw, nb, layer):
    def seg(s):
        return pl.BlockSpec((Tc, bw), lambda n, i: (i, s * nb + n))

    p_spec = pl.BlockSpec((None, R_ROWS, bw), lambda n, i: (layer, 0, n))
    w_spec = pl.BlockSpec((None, None, bw, bw), lambda n, i: (layer, n, 0, 0))
    return seg, p_spec, w_spec


def _mixer_fwd(name, layer, z, pch, wr, wi, comm=None):
    T, D = z.shape[0], z.shape[1] // 7
    bw, nb = wr.shape[-1], wr.shape[1]
    Tc = min(SEQ_CHUNK, T)
    nT = T // Tc
    n_cin = len(comm.ins) if comm else 0
    n_cout = len(comm.out_shapes) if comm else 0

    def body(*refs):
        xa_ref, ya_ref, cb_ref, cc_ref, cx_ref, p_ref, wr_ref, wi_ref = refs[:8]
        pa_ref, pb_ref, xc_ref, hl_ref, vb_ref = refs[8 + n_cin:13 + n_cin]
        xa_buf, u_buf, a_s, b_s, carry = refs[13 + n_cin + n_cout:18 + n_cin + n_cout]
        if comm:
            c_in, c_out, sems = refs[8:8 + n_cin], refs[13 + n_cin:13 + n_cin + n_cout], refs[18 + n_cin + n_cout:]
            step = pl.program_id(0) * nT + pl.program_id(1)
            comm.ride(step, nb * nT, c_in, c_out, sems, True)

        @pl.when(pl.program_id(1) == 0)
        def _():
            xa_buf[0:HALO, :] = jnp.zeros((HALO, bw), F32)
            u_buf[0:HALO, :] = jnp.zeros((HALO, bw), F32)
            carry[...] = jnp.zeros_like(carry)

        xa_buf[HALO:HALO + Tc, :] = xa_ref[...]
        xc = p_ref[R_CAB:R_CAB + 1, :]
        for k in range(4):
            xc = xc + p_ref[R_CAW + k:R_CAW + k + 1, :] * xa_buf[HALO - 3 + k:HALO - 3 + k + Tc, :]
        xc_ref[...] = xc
        _, _, ig, _, a, mult = _gates(xc, p_ref, wr_ref, wi_ref)
        a_s[...] = a
        b_s[...] = mult * (ig * xc)
        _chunk_scan(a_s, b_s, hl_ref, carry, Tc // 8, bw, False)
        g, _ = _gelu_and_grad(ya_ref[...])
        pa_ref[...] = (hl_ref[...] * g).astype(BF16)

        u_buf[HALO:HALO + Tc, :] = cc_ref[...] * cx_ref[...]
        vb = jnp.zeros((Tc, bw), F32)
        for k in range(3):
            vb = vb + p_ref[R_CBW + k:R_CBW + k + 1, :] * u_buf[HALO - 2 + k:HALO - 2 + k + Tc, :]
        vb_ref[...] = vb
        pb_ref[...] = (cb_ref[...] * vb).astype(BF16)
        xa_buf[0:HALO, :] = xa_buf[Tc:Tc + HALO, :]
        u_buf[0:HALO, :] = u_buf[Tc:Tc + HALO, :]

        if comm:
            comm.ride(step, nb * nT, c_in, c_out, sems, False)

    seg, p_spec, w_spec = _mixer_specs(Tc, bw, nb, layer)
    out = pl.BlockSpec((Tc, bw), lambda n, i: (i, n))
    res = pl.pallas_call(
        body,
        grid=(nb, nT),
        in_specs=[seg(0), seg(1), seg(2), seg(3), seg(4), p_spec, w_spec, w_spec] + [ANY] * n_cin,
        out_specs=[out] * 5 + [ANY] * n_cout,
        out_shape=[jax.ShapeDtypeStruct((T, D), BF16), jax.ShapeDtypeStruct((T, D), BF16),
                   jax.ShapeDtypeStruct((T, D), F32), jax.ShapeDtypeStruct((T, D), F32),
                   jax.ShapeDtypeStruct((T, D), F32)] + (comm.out_shapes if comm else []),
        scratch_shapes=[pltpu.VMEM((Tc + HALO, bw), F32), pltpu.VMEM((Tc + HALO, bw), F32),
                        pltpu.VMEM((Tc, bw), F32), pltpu.VMEM((Tc, bw), F32), pltpu.VMEM((8, bw), F32)]
        + (comm.scratch() if comm else []),
        input_output_aliases={8 + ci: 5 + co for ci, co in comm.aliases.items()} if comm else {},
        compiler_params=_cparams(("arbitrary", "arbitrary") if comm else ("parallel", "arbitrary")),
        name=name,
    )(z, z, z, z, z, pch, wr, wi, *(comm.ins if comm else []))
    return (res[:5], res[5:]) if comm else res


def _mixer_bwd(name, layer, z, xc, hl, vb, dpa, dpb, pch, wr, wi, comm=None):
    T, D = z.shape[0], z.shape[1] // 7
    bw, nb = wr.shape[-1], wr.shape[1]
    Tc = min(SEQ_CHUNK, T)
    nT = T // Tc
    tpc = Tc // 8
    rb = min(MIXER_ROW_BLOCK, Tc)
    n_cin = len(comm.ins) if comm else 0
    n_cout = len(comm.out_shapes) if comm else 0

    def body(*refs):
        (xa_ref, ya_ref, cb_ref, cc_ref, cx_ref, xc_ref, hl_ref, hp_ref, vb_ref, dpa_ref, dpb_ref,
         p_ref, wr_ref, wi_ref) = refs[:14]
        dxa_ref, dya_ref, dcb_ref, dcc_ref, dcx_ref, dwr_ref, dwi_ref, sm_ref = refs[14 + n_cin:22 + n_cin]
        (h_buf, a_buf, dxc_buf, dvb_buf, a_s, d_s, lam_s, r_s, i_s, m_s, dpr_s, dpi_s, sm8,
         carry) = refs[22 + n_cin + n_cout:36 + n_cin + n_cout]
        i = pl.program_id(1)
        if comm:
            c_in, c_out = refs[14:14 + n_cin], refs[22 + n_cin:22 + n_cin + n_cout]
            sems = refs[36 + n_cin + n_cout:]
            step = pl.program_id(0) * nT + i
            comm.ride(step, nb * nT, c_in, c_out, sems, True)

        @pl.when(i == 0)
        def _():
            a_buf[Tc:Tc + HALO, :] = jnp.zeros((HALO, bw), F32)
            dxc_buf[Tc:Tc + HALO, :] = jnp.zeros((HALO, bw), F32)
            dvb_buf[Tc:Tc + HALO, :] = jnp.zeros((HALO, bw), F32)
            carry[...] = jnp.zeros_like(carry)
            dwr_ref[...] = jnp.zeros_like(dwr_ref)
            dwi_ref[...] = jnp.zeros_like(dwi_ref)
            sm8[...] = jnp.zeros_like(sm8)

        blocks = [(g * rb, slice(g * rb, (g + 1) * rb)) for g in range(Tc // rb)]

        def prow(k):
            return p_ref[k:k + 1, :]

        def part8(v):
            return jnp.sum(v.reshape(rb // 8, 8, bw), axis=0)

        sums8 = {}

        def tally(k, v):
            sums8[k] = sums8[k] + part8(v) if k in sums8 else part8(v)

        sp = _softplus_neg(prow(R_LAM))
        xcb = xc_ref[...].astype(BF16)
        r_s[...] = jnp.dot(xcb, wr_ref[...], preferred_element_type=F32)
        i_s[...] = jnp.dot(xcb, wi_ref[...], preferred_element_type=F32)
        for _, rs in blocks:
            r = _sigmoid(r_s[rs, :] + prow(R_BR))
            ig = _sigmoid(i_s[rs, :] + prow(R_BI))
            log_a = (-LRU_C) * r * sp
            t = jnp.tanh(log_a)
            r_s[rs, :] = r
            i_s[rs, :] = ig
            m_s[rs, :] = jnp.sqrt(-2.0 * t / (1.0 - t))
            a_buf[rs, :] = jnp.exp(log_a)
            g, gp = _gelu_and_grad(ya_ref[rs, :])
            dpav = dpa_ref[rs, :]
            d_s[rs, :] = dpav * g
            dya_ref[rs, :] = (dpav * hl_ref[rs, :] * gp).astype(BF16)
            dpbv = dpb_ref[rs, :]
            dcb_ref[rs, :] = (dpbv * vb_ref[rs, :]).astype(BF16)
            dvb_buf[rs, :] = dpbv * cb_ref[rs, :]

        a_s[...] = a_buf[1:Tc + 1, :]
        _chunk_scan(a_s, d_s, lam_s, carry, tpc, bw, True)
        h_buf[HALO:HALO + Tc, :] = hl_ref[...]
        h_buf[0:HALO, :] = jnp.where(i == nT - 1, 0.0, hp_ref[...])

        for r0, rs in blocks:
            lamv, xcv, r, ig, a = lam_s[rs, :], xc_ref[rs, :], r_s[rs, :], i_s[rs, :], a_buf[rs, :]
            mult = m_s[rs, :]
            da = lamv * h_buf[HALO - 1 + r0:HALO - 1 + r0 + rb, :]
            dmult = lamv * (ig * xcv)
            dbx = lamv * mult
            dig = dbx * xcv
            dxc_buf[rs, :] = dbx * ig
            dlog_a = da * a - dmult * (a * a) / mult
            dpr = (dlog_a * ((-LRU_C) * sp)) * r * (1.0 - r)
            dpi = dig * ig * (1.0 - ig)
            dpr_s[rs, :] = dpr.astype(BF16)
            dpi_s[rs, :] = dpi.astype(BF16)
            tally(R_BR, dpr)
            tally(R_BI, dpi)
            tally(R_LAM, dlog_a * ((-LRU_C) * r))

        dprb, dpib = dpr_s[...], dpi_s[...]
        nt = (((1,), (1,)), ((), ()))
        tn = (((0,), (0,)), ((), ()))
        dxc_buf[0:Tc, :] += (lax.dot_general(dprb, wr_ref[...], nt, preferred_element_type=F32)
                             + lax.dot_general(dpib, wi_ref[...], nt, preferred_element_type=F32))
        dwr_ref[...] += lax.dot_general(xcb, dprb, tn, preferred_element_type=F32)
        dwi_ref[...] += lax.dot_general(xcb, dpib, tn, preferred_element_type=F32)

        for r0, rs in blocks:
            xav = xa_ref[rs, :]
            tally(R_CAB, dxc_buf[rs, :])
            dxa = jnp.zeros((rb, bw), F32)
            for k in range(4):
                sh = dxc_buf[3 - k + r0:3 - k + r0 + rb, :]
                dxa = dxa + prow(R_CAW + k) * sh
                tally(R_CAW + k, xav * sh)
            dxa_ref[rs, :] = dxa.astype(BF16)
            ccv, cxv = cc_ref[rs, :], cx_ref[rs, :]
            u = ccv * cxv
            du = jnp.zeros((rb, bw), F32)
            for k in range(3):
                sh = dvb_buf[2 - k + r0:2 - k + r0 + rb, :]
                du = du + prow(R_CBW + k) * sh
                tally(R_CBW + k, u * sh)
            dcc_ref[rs, :] = (du * cxv).astype(BF16)
            dcx_ref[rs, :] = (du * ccv).astype(BF16)

        for k, v in sums8.items():
            sm8[8 * k:8 * k + 8, :] += v
        a_buf[Tc:Tc + HALO, :] = a_buf[0:HALO, :]
        dxc_buf[Tc:Tc + HALO, :] = dxc_buf[0:HALO, :]
        dvb_buf[Tc:Tc + HALO, :] = dvb_buf[0:HALO, :]

        @pl.when(i == nT - 1)
        def _():
            sm_ref[...] = jnp.sum(sm8[...].reshape(R_ROWS, 8, bw), axis=1)
            sm_ref[R_LAM:R_LAM + 1, :] = sm_ref[R_LAM:R_LAM + 1, :] * (-_sigmoid(-prow(R_LAM)))

        if comm:
            comm.ride(step, nb * nT, c_in, c_out, sems, False)

    def seg(s):
        return pl.BlockSpec((Tc, bw), lambda n, i: (nT - 1 - i, s * nb + n))

    blk = pl.BlockSpec((Tc, bw), lambda n, i: (nT - 1 - i, n))
    halo = pl.BlockSpec((8, bw), lambda n, i: (jnp.maximum((nT - 1 - i) * tpc - 1, 0), n))
    p_spec = pl.BlockSpec((None, R_ROWS, bw), lambda n, i: (layer, 0, n))
    w_spec = pl.BlockSpec((None, None, bw, bw), lambda n, i: (layer, n, 0, 0))
    dw_spec = pl.BlockSpec((None, bw, bw), lambda n, i: (n, 0, 0))
    act = jax.ShapeDtypeStruct((T, D), BF16)
    res = pl.pallas_call(
        body,
        grid=(nb, nT),
        in_specs=[seg(0), seg(1), seg(2), seg(3), seg(4), blk, blk, halo, blk, blk, blk, p_spec, w_spec, w_spec]
        + [ANY] * n_cin,
        out_specs=[blk] * 5 + [dw_spec, dw_spec, pl.BlockSpec((R_ROWS, bw), lambda n, i: (0, n))] + [ANY] * n_cout,
        out_shape=[act] * 5 + [jax.ShapeDtypeStruct((nb, bw, bw), F32), jax.ShapeDtypeStruct((nb, bw, bw), F32),
                               jax.ShapeDtypeStruct((R_ROWS, D), F32)] + (comm.out_shapes if comm else []),
        scratch_shapes=[pltpu.VMEM((Tc + HALO, bw), F32)] * 4 + [pltpu.VMEM((Tc, bw), F32)] * 6
        + [pltpu.VMEM((Tc, bw), BF16)] * 2 + [pltpu.VMEM((8 * R_ROWS, bw), F32), pltpu.VMEM((8, bw), F32)]
        + (comm.scratch() if comm else []),
        input_output_aliases={14 + ci: 8 + co for ci, co in comm.aliases.items()} if comm else {},
        compiler_params=_cparams(("arbitrary", "arbitrary") if comm else ("parallel", "arbitrary")),
        name=name,
    )(z, z, z, z, z, xc, hl, hl, vb, dpa, dpb, pch, wr, wi, *(comm.ins if comm else []))
    return (res[:8], res[8:]) if comm else res


def _local_fwd_bwd(x, tgt, W, placed=None):
    T, D = x.shape
    g1, g2 = W["g1"], W["g2"]
    depth = g1.shape[0]
    FF = 4 * D
    if placed is None:
        mats = {(n, l): W[n][l] for n in MATS for l in range(depth)}
    else:
        mats = {}
        mats["w_in", 0], = _run_carry("gather_first", _gather_carry([placed["w_in", 0]], [(0, 1, 0, 1)]))

    def gathering(specs):
        if placed is None or not specs:
            return None, []
        keys = [(n, l) for n, l, _, _, _ in specs]
        arrays = [mats.get(k, placed[k]) for k in keys]
        return _gather_carry(arrays, [(i, d, part, nparts) for i, (_, _, d, part, nparts) in enumerate(specs)]), keys

    def hosted(call, specs, **kw):
        carry, keys = gathering(specs)
        if carry is None:
            return call(**kw)
        res, got = call(**kw, **{("comm" if call.func is _mixer_fwd else "carry"): carry})
        mats.update(zip(keys, got))
        return res

    saved = []
    xs = x
    for l in range(depth):
        h = _rms_fwd(f"rms1_fwd_{l}", xs, g1[l][None])
        nxt = l + 1 < depth
        projs = [("w_pa", l, 0, 0, 1), ("w_pb", l, 0, 0, 1), ("w_o", l, 0, 0, 1)]
        z = hosted(functools.partial(_mm, f"in_proj_{l}", "nn", h, mats["w_in", l], T, 7 * D, D, [F32]),
                   projs + ([("w_mlp1", l, 1, 0, 1)] if nxt else []),
                   epilogue=lambda acc, b: (acc + b,), extras=[(W["b_in"][l][None], "bias", 0)])
        pa, pb, xc, hl, vb = hosted(
            functools.partial(_mixer_fwd, f"mixer_fwd_{l}", l, z, W["pch"], W["wr"], W["wi"]),
            [("w_mlp2", l, 0, 0, 1)] if nxt else [("w_mlp1", l, 1, 0, 1)])
        oa = _mm(f"proj_a_{l}", "nn", pa, mats["w_pa", l], T, D, D, [F32])

        def merge(acc, oav, ga, gb):
            return acc, _sigmoid(ga) * oav + _sigmoid(gb) * acc

        ob, mg = _mm(f"proj_b_merge_{l}", "nn", pb, mats["w_pb", l], T, D, D, [F32, BF16], epilogue=merge,
                     tiles=MM_TILES_FUSED,
                     extras=[(oa, "tile", 0), (z, "tile", 5 * D), (z, "tile", 6 * D)])
        x1 = _mm(f"out_proj_{l}", "nn", mg, mats["w_o", l], T, D, D, [F32],
                 epilogue=lambda acc, res: (res + acc,), extras=[(xs, "tile", 0)])
        h2 = _rms_fwd(f"rms2_fwd_{l}", x1, g2[l][None])

        def relu2(acc):
            pr = jnp.maximum(acc, 0.0)
            return pr * pr, pr

        u, pr = hosted(functools.partial(_mm, f"mlp1_{l}", "nn", h2, mats["w_mlp1", l], T, FF, D, [BF16, BF16]),
                       [("w_in", l + 1, 1, 0, 2)] if nxt else [("w_mlp2", l, 0, 0, 1)], epilogue=relu2)
        x2 = hosted(functools.partial(_mm, f"mlp2_{l}", "nn", u, mats["w_mlp2", l], T, D, FF, [F32]),
                    [("w_in", l + 1, 1, 1, 2)] if nxt else [],
                    epilogue=lambda acc, res: (res + acc,), extras=[(x1, "tile", 0)])
        saved.append(dict(x0=xs, h=h, z=z, pa=pa, pb=pb, xc=xc, hl=hl, vb=vb, oa=oa, ob=ob, mg=mg, x1=x1,
                          h2=h2, u=u, pr=pr))
        xs = x2

    dx, dxb, dgf, loss_blk = _loss_head("loss_head", xs, W["gf"][None], tgt)

    gmat, got, sums, landed = {}, {}, {}, {}

    def reducing(call, swaps=(), scatters=(), mixer=False, **kw):
        swaps, scatters = [(n, l) for n in swaps], [(n, l) for n in scatters]
        if placed is None:
            return call(**kw)
        cs = _swap_carry([gmat[k] for k in swaps], [BIG_DIM[k[0]] - 1 for k in swaps]) if swaps else None
        cc = _scatter_carry([sums[k] for k in scatters]) if scatters else None
        res, moved = call(**kw, **{("comm" if mixer else "carry"): _merge_carries([cs, cc])})
        got.update(zip(swaps, moved[:len(swaps)]))
        landed.update(zip(scatters, moved[len(swaps):]))
        return res

    def add(names):
        if placed is not None:
            for n in names:
                sums[n, l] = _add_halves(f"add_halves_{n}_{l}", gmat[n, l], got[n, l], BIG_DIM[n] - 1, W["core"])

    grads = [None] * depth
    for l in reversed(range(depth)):
        s = saved[l]
        dp = _mm(f"mlp2_dx_{l}", "nt", dxb, mats["w_mlp2", l], T, FF, D, [BF16],
                 epilogue=lambda acc, prv: (2.0 * prv.astype(F32) * acc,), extras=[(s["pr"], "tile", 0)])
        dw2 = gmat["w_mlp2", l] = _mm(f"mlp2_dw_{l}", "tn", s["u"], dxb, FF, D, T, [BF16])
        dh2 = reducing(functools.partial(_mm, f"mlp1_dx_{l}", "nt", dp, mats["w_mlp1", l], T, D, FF, [F32]),
                       swaps=["w_mlp2"])
        add(["w_mlp2"])
        dw1 = gmat["w_mlp1", l] = reducing(functools.partial(_mm, f"mlp1_dw_{l}", "tn", s["h2"], dp, D, FF, T, [BF16]),
                                           scatters=["w_mlp2"])
        dx1, dx1b, dg2 = _rms_bwd(f"rms2_bwd_{l}", s["x1"], g2[l][None], dh2, dx)

        def unmerge(acc, ga, gb, oav, obv):
            sa, sb = _sigmoid(ga), _sigmoid(gb)
            return acc * sa, acc * sb, acc * oav * sa * (1.0 - sa), acc * obv * sb * (1.0 - sb)

        doa, dob, dga, dgb = reducing(
            functools.partial(_mm, f"out_proj_dx_{l}", "nt", dx1b, mats["w_o", l], T, D, D, [BF16] * 4),
            swaps=["w_mlp1"], tiles=MM_TILES_FUSED, epilogue=unmerge,
            extras=[(s["z"], "tile", 5 * D), (s["z"], "tile", 6 * D), (s["oa"], "tile", 0), (s["ob"], "tile", 0)])
        add(["w_mlp1"])
        dwo = gmat["w_o", l] = _mm(f"out_proj_dw_{l}", "tn", s["mg"], dx1b, D, D, T, [BF16])
        dpa = _mm(f"proj_a_dx_{l}", "nt", doa, mats["w_pa", l], T, D, D, [F32])
        dwpa = gmat["w_pa", l] = _mm(f"proj_a_dw_{l}", "tn", s["pa"], doa, D, D, T, [BF16])
        dpb = _mm(f"proj_b_dx_{l}", "nt", dob, mats["w_pb", l], T, D, D, [F32])
        dwpb = gmat["w_pb", l] = _mm(f"proj_b_dw_{l}", "tn", s["pb"], dob, D, D, T, [BF16])
        dxa, dya, dcb, dcc, dcx, dwr, dwi, sm = reducing(
            functools.partial(_mixer_bwd, f"mixer_bwd_{l}", l, s["z"], s["xc"], s["hl"], s["vb"], dpa, dpb,
                              W["pch"], W["wr"], W["wi"]),
            swaps=["w_o", "w_pa", "w_pb"], scatters=["w_mlp1"], mixer=True)
        add(["w_o", "w_pa", "w_pb"])
        dz = jnp.concatenate([dxa, dya, dcb, dcc, dcx, dga, dgb], axis=1)
        dwin = gmat["w_in", l] = reducing(
            functools.partial(_mm, f"in_proj_dw_{l}", "tn", s["h"], dz, D, 7 * D, T, [BF16]),
            scatters=["w_o", "w_pa", "w_pb"])
        dbin = reducing(functools.partial(_colsum, f"bias_grad_{l}", dz), swaps=["w_in"])
        add(["w_in"])
        dh = reducing(functools.partial(_mm, f"in_proj_dx_{l}", "nt", dz, mats["w_in", l], T, D, 7 * D, [F32]),
                      scatters=["w_in"])
        dx, dxb, dg1 = _rms_bwd(f"rms1_bwd_{l}", s["x0"], g1[l][None], dh, dx1)
        grads[l] = dict(w_in=dwin, w_pa=dwpa, w_pb=dwpb, w_o=dwo, w_mlp1=dw1, w_mlp2=dw2, wr=dwr, wi=dwi,
                        sm=sm, b_in=dbin, g1=dg1, g2=dg2)
    return loss_blk[0, 0], dx, grads, dgf, sums, landed


ANY = pl.BlockSpec(memory_space=pl.ANY)


def _place():
    x, y, c = lax.axis_index("x"), lax.axis_index("y"), lax.axis_index("c")
    peers = [(1 - x, y, c), (x, 1 - y, c), (1 - x, 1 - y, c)]
    chips = [2 * (1 - x) + y, 2 * x + (1 - y), 2 * (1 - x) + (1 - y)]
    return (x, y, c), 2 * x + y, peers, chips


def _window(ref, dim, q, size):
    idx = [slice(None)] * len(ref.shape)
    idx[dim] = pl.ds(q * size, size)
    return ref.at[tuple(idx)]


def _gather_weights(shards, dims, small):
    n = len(shards)
    sizes = [s.shape[d] for s, d in zip(shards, dims)]
    full = [jax.ShapeDtypeStruct(s.shape[:d] + (s.shape[d] * N_CHIPS,) + s.shape[d + 1:], s.dtype)
            for s, d in zip(shards, dims)]
    full.append(jax.ShapeDtypeStruct((N_CHIPS,) + small.shape, small.dtype))

    def body(*refs):
        ins, outs = refs[:n + 1], refs[n + 1:2 * n + 2]
        send_sems, recv_sems, local_sems = refs[2 * n + 2:]
        _, k, peers, chips = _place()

        def dst(w, q):
            return outs[w].at[q] if w == n else _window(outs[w], dims[w], q, sizes[w])

        local = [pltpu.make_async_copy(ins[w], dst(w, k), local_sems.at[w]) for w in range(n + 1)]
        for cp in local:
            cp.start()
        sends = []
        for p, peer in enumerate(peers):
            for w in range(n + 1):
                s = p * (n + 1) + w
                sends.append(pltpu.make_async_remote_copy(
                    src_ref=ins[w], dst_ref=dst(w, k), send_sem=send_sems.at[s], recv_sem=recv_sems.at[s],
                    device_id=peer, device_id_type=MESH))
        for cp in sends:
            cp.start()
        for p, peer in enumerate(peers):
            for w in range(n + 1):
                s = p * (n + 1) + w
                pltpu.make_async_remote_copy(
                    src_ref=ins[w], dst_ref=dst(w, chips[p]), send_sem=send_sems.at[s], recv_sem=recv_sems.at[s],
                    device_id=peer, device_id_type=MESH).wait_recv()
        for cp in sends:
            cp.wait_send()
        for cp in local:
            cp.wait()

    return pl.pallas_call(
        body,
        in_specs=[ANY] * (n + 1),
        out_specs=[ANY] * (n + 1),
        out_shape=full,
        scratch_shapes=[pltpu.SemaphoreType.DMA((3 * (n + 1),)), pltpu.SemaphoreType.DMA((3 * (n + 1),)),
                        pltpu.SemaphoreType.DMA((n + 1,))],
        name="gather_weights",
    )(*shards, small)


def _scatter_grads(grads, dims):
    n, depth = len(grads), len(grads[0])
    sizes = [g[0].shape[d] // N_CHIPS for g, d in zip(grads, dims)]
    land = []
    for g, d, sz in zip(grads, dims, sizes):
        shp = g[0].shape
        land.append(jax.ShapeDtypeStruct((N_CHIPS, depth) + shp[:d] + (sz,) + shp[d + 1:], g[0].dtype))

    def body(*refs):
        ins, outs = refs[:n * depth], refs[n * depth:n * depth + n]
        send_sems, recv_sems, local_sems = refs[n * depth + n:]
        _, k, peers, chips = _place()

        def src(w, l, q):
            return _window(ins[w * depth + l], dims[w], q, sizes[w])

        local = [pltpu.make_async_copy(src(w, l, k), outs[w].at[3, l], local_sems.at[w * depth + l])
                 for w in range(n) for l in range(depth)]
        for cp in local:
            cp.start()
        sends = []
        for p, peer in enumerate(peers):
            for w in range(n):
                for l in range(depth):
                    s = (p * n + w) * depth + l
                    sends.append(pltpu.make_async_remote_copy(
                        src_ref=src(w, l, chips[p]), dst_ref=outs[w].at[p, l], send_sem=send_sems.at[s],
                        recv_sem=recv_sems.at[s], device_id=peer, device_id_type=MESH))
        for cp in sends:
            cp.start()
        for cp in sends:
            cp.wait_recv()
        for cp in sends:
            cp.wait_send()
        for cp in local:
            cp.wait()

    flat = [g for gl in grads for g in gl]
    return pl.pallas_call(
        body,
        in_specs=[ANY] * (n * depth),
        out_specs=[ANY] * n,
        out_shape=land,
        scratch_shapes=[pltpu.SemaphoreType.DMA((3 * n * depth,)), pltpu.SemaphoreType.DMA((3 * n * depth,)),
                        pltpu.SemaphoreType.DMA((n * depth,))],
        name="scatter_grads",
    )(*flat)


def _sum_slots(name, land):
    _, R, C = land.shape
    tr, tc = _div_tile(R, 512, 8), _div_tile(C, 1024)

    def body(a_ref, b_ref, c_ref, d_ref, o_ref):
        o_ref[...] = ((d_ref[...].astype(F32) + a_ref[...].astype(F32)) + b_ref[...].astype(F32)) \
            + c_ref[...].astype(F32)

    def slot(q):
        return pl.BlockSpec((None, tr, tc), lambda i, j: (q, i, j))

    return pl.pallas_call(
        body,
        grid=(R // tr, C // tc),
        in_specs=[slot(0), slot(1), slot(2), slot(3)],
        out_specs=pl.BlockSpec((tr, tc), lambda i, j: (i, j)),
        out_shape=jax.ShapeDtypeStruct((R, C), F32),
        compiler_params=_cparams(("parallel", "parallel")),
        name=name,
    )(land, land, land, land)


def _swap_with_sibling(parts):
    n = len(parts)

    def body(*refs):
        ins, outs = refs[:n], refs[n:2 * n]
        send_sems, recv_sems = refs[2 * n:]
        (x, y, c), _, _, _ = _place()
        copies = [pltpu.make_async_remote_copy(
            src_ref=ins[w], dst_ref=outs[w], send_sem=send_sems.at[w], recv_sem=recv_sems.at[w],
            device_id=(x, y, 1 - c), device_id_type=MESH) for w in range(n)]
        for cp in copies:
            cp.start()
        for cp in copies:
            cp.wait()

    return pl.pallas_call(
        body,
        in_specs=[ANY] * n,
        out_specs=[ANY] * n,
        out_shape=[jax.ShapeDtypeStruct(p.shape, p.dtype) for p in parts],
        scratch_shapes=[pltpu.SemaphoreType.DMA((n,)), pltpu.SemaphoreType.DMA((n,))],
        name="swap_with_sibling",
    )(*parts)


class _Carry:
    def __init__(self, ins, out_shapes, rounds, counts, aliases=None, marks=(0.6, 0.92)):
        self.ins, self.out_shapes, self.rounds, self.counts = list(ins), list(out_shapes), list(rounds), list(counts)
        self.aliases, self.marks = dict(aliases or {}), marks

    def scratch(self):
        return [pltpu.SemaphoreType.DMA((n,)) for n in self.counts for _ in range(2)]

    def _copies(self, r, in_refs, out_refs, sems, landing):
        remote = self.rounds[r](in_refs, out_refs)
        assert len(remote) == self.counts[r], (r, len(remote), self.counts[r])
        return [pltpu.make_async_remote_copy(src_ref=s, dst_ref=(land if landing else d), send_sem=sems[2 * r].at[i],
                                             recv_sem=sems[2 * r + 1].at[i], device_id=peer, device_id_type=MESH)
                for i, (s, d, peer, land) in enumerate(remote)]

    def begin(self, r, in_refs, out_refs, sems):
        if r > 0:
            for cp in self._copies(r - 1, in_refs, out_refs, sems, True):
                cp.wait_recv()
        for cp in self._copies(r, in_refs, out_refs, sems, False):
            cp.start()

    def end(self, in_refs, out_refs, sems):
        last = len(self.rounds) - 1
        for cp in self._copies(last, in_refs, out_refs, sems, True):
            cp.wait_recv()
        for r in range(last + 1):
            for cp in self._copies(r, in_refs, out_refs, sems, False):
                cp.wait_send()

    def ride(self, step, total, in_refs, out_refs, sems, first):
        if first:
            @pl.when(step == 0)
            def _():
                self.begin(0, in_refs, out_refs, sems)
            return
        for r in range(1, len(self.rounds)):
            @pl.when(step == min(total - 1, int(total * self.marks[r - 1])))
            def _(r=r):
                self.begin(r, in_refs, out_refs, sems)

        @pl.when(step == total - 1)
        def _():
            self.end(in_refs, out_refs, sems)


def _run_carry(name, carry):
    n_in, n_out = len(carry.ins), len(carry.out_shapes)

    def body(*refs):
        in_refs, out_refs, sems = refs[:n_in], refs[n_in:n_in + n_out], refs[n_in + n_out:]
        for r in range(len(carry.rounds)):
            carry.begin(r, in_refs, out_refs, sems)
        carry.end(in_refs, out_refs, sems)

    return pl.pallas_call(
        body,
        in_specs=[ANY] * n_in,
        out_specs=[ANY] * n_out,
        out_shape=carry.out_shapes,
        scratch_shapes=carry.scratch(),
        input_output_aliases=carry.aliases,
        name=name,
    )(*carry.ins)


def _comm_call(name, ins, out_shapes, plan, n_local, n_remote, aliases=None):
    assert n_local == 0
    return _run_carry(name, _Carry(ins, out_shapes, [lambda i, o: plan(i, o)[1]], [n_remote], aliases=aliases))


def _gather_carry(arrays, items):
    shapes = [a.shape for a in arrays]

    def ring():
        x, y, c = lax.axis_index("x"), lax.axis_index("y"), lax.axis_index("c")
        first = (x + (1 - c) * (1 - 2 * x), y + c * (1 - 2 * y), c)
        second = (x + c * (1 - 2 * x), y + (1 - c) * (1 - 2 * y), c)
        return c, 2 * x + y, first, second

    def chip(pos):
        return 2 * pos[0] + pos[1]

    def round0(ins, outs):
        c, k, first, second = ring()
        remote = []
        for item in items:
            win = window(outs[item[0]], item)
            remote.append((win(2 * k + c), win(2 * k + c), first, win(2 * chip(first) + c)))
            remote.append((win(2 * k + c), win(2 * k + c), second, win(2 * chip(second) + c)))
        return remote

    def round1(ins, outs):
        c, k, first, second = ring()
        remote = []
        for item in items:
            win = window(outs[item[0]], item)
            relayed = win(2 * chip(first) + c)
            remote.append((relayed, relayed, second, win(2 * (3 - k) + c)))
        return remote

    def window(ref, item):
        idx, d, part, nparts = item
        h = shapes[idx][d] // (2 * N_CHIPS)
        rows = shapes[idx][1 - d] // nparts

        def win(j):
            sl = [None, None]
            sl[d] = pl.ds(j * h, h)
            sl[1 - d] = pl.ds(part * rows, rows)
            return ref.at[tuple(sl)]

        return win

    def round2(ins, outs):
        (x, y, c), _, _, chips = _place()
        remote = []
        for item in items:
            win = window(outs[item[0]], item)
            for p in range(3):
                remote.append((win(2 * chips[p] + c), win(2 * chips[p] + c), (x, y, 1 - c),
                               win(2 * chips[p] + 1 - c)))
        return remote

    n = len(items)
    return _Carry(arrays, [jax.ShapeDtypeStruct(a.shape, a.dtype) for a in arrays], [round0, round1, round2],
                  [2 * n, n, 3 * n], aliases={i: i for i in range(len(arrays))})


def _place_shard(name, w, layer, dim, chip):
    _, a, b = w.shape
    full = (a * N_CHIPS, b) if dim == 0 else (a, b * N_CHIPS)
    tr, tc = _div_tile(a, 512, 16), _div_tile(b, 2048)
    nr, nc = a // tr, b // tc

    def out_map(i, j, chip_ref):
        return (chip_ref[0] * nr + i, j) if dim == 0 else (i, chip_ref[0] * nc + j)

    def body(chip_ref, w_ref, o_ref):
        o_ref[...] = w_ref[...].astype(o_ref.dtype)

    return pl.pallas_call(
        body,
        grid_spec=pltpu.PrefetchScalarGridSpec(
            num_scalar_prefetch=1, grid=(nr, nc),
            in_specs=[pl.BlockSpec((None, tr, tc), lambda i, j, chip_ref: (layer, i, j))],
            out_specs=pl.BlockSpec((tr, tc), out_map)),
        out_shape=jax.ShapeDtypeStruct(full, BF16),
        compiler_params=_cparams(("parallel", "parallel")),
        name=name,
    )(chip, w)


def _half_shape(shape, dim):
    return shape[:dim] + (shape[dim] // (2 * N_CHIPS),) + shape[dim + 1:]


def _swap_carry(grads, dims):
    shapes = [jax.ShapeDtypeStruct((N_CHIPS,) + _half_shape(g.shape, d), g.dtype) for g, d in zip(grads, dims)]

    def plan(ins, outs):
        (x, y, c), _, _, _ = _place()
        remote = []
        for w, d in enumerate(dims):
            h = grads[w].shape[d] // (2 * N_CHIPS)
            for q in range(N_CHIPS):
                remote.append((_window(ins[w], d, 2 * q + 1 - c, h), outs[w].at[q], (x, y, 1 - c), outs[w].at[q]))
        return remote

    return _Carry(grads, shapes, [plan], [N_CHIPS * len(grads)])


def _scatter_carry(sums):
    def plan(ins, outs):
        _, _, peers, chips = _place()
        remote = []
        for w in range(len(sums)):
            for p in range(3):
                remote.append((ins[w].at[chips[p]], outs[w].at[p], peers[p], outs[w].at[p]))
        return remote

    return _Carry(sums, [jax.ShapeDtypeStruct((3,) + s.shape[1:], s.dtype) for s in sums], [plan], [3 * len(sums)])


def _merge_carries(carries):
    carries = [c for c in carries if c is not None]
    if len(carries) <= 1:
        return carries[0] if carries else None
    ins = [a for c in carries for a in c.ins]
    outs = [s for c in carries for s in c.out_shapes]

    def plan(in_refs, out_refs):
        remote, i0, o0 = [], 0, 0
        for c in carries:
            remote += c.rounds[0](in_refs[i0:i0 + len(c.ins)], out_refs[o0:o0 + len(c.out_shapes)])
            i0, o0 = i0 + len(c.ins), o0 + len(c.out_shapes)
        return remote

    assert all(len(c.rounds) == 1 and not c.aliases for c in carries)
    return _Carry(ins, outs, [plan], [sum(c.counts[0] for c in carries)])


def _add_halves(name, g, got, dim, core):
    R, C = g.shape
    if dim == 1:
        r, cc = R, C // (2 * N_CHIPS)
    else:
        r, cc = R // (2 * N_CHIPS), C
    tr, tc = _div_tile(r, 512, 16), _div_tile(cc, 1024)
    nr, nc = r // tr, cc // tc

    def g_map(q, i, j, core_ref):
        w = 2 * q + core_ref[0]
        return (i, w * nc + j) if dim == 1 else (w * nr + i, j)

    def body(core_ref, g_ref, got_ref, o_ref):
        o_ref[...] = (g_ref[...].astype(F32) + got_ref[...].astype(F32)).astype(o_ref.dtype)

    slab = pl.BlockSpec((None, tr, tc), lambda q, i, j, core_ref: (q, i, j))
    return pl.pallas_call(
        body,
        grid_spec=pltpu.PrefetchScalarGridSpec(
            num_scalar_prefetch=1, grid=(N_CHIPS, nr, nc),
            in_specs=[pl.BlockSpec((tr, tc), g_map), slab], out_specs=slab),
        out_shape=jax.ShapeDtypeStruct((N_CHIPS, r, cc), g.dtype),
        compiler_params=_cparams(("parallel", "parallel", "parallel")),
        name=name,
    )(core, g, got)


def _reduce_into(name, sums, land, acc, layer, dim, shape, where):
    _, r, cc = sums.shape
    tr, tc = _div_tile(r, 512, 16), _div_tile(cc, 1024)
    nr, nc = r // tr, cc // tc

    def out_map(i, j, s):
        return (layer, s[1] * nr + i, j) if dim == 1 else (layer, i, s[1] * nc + j)

    def body(*refs):
        own, a_ref, b_ref, c_ref, o_ref = refs[1], refs[2], refs[3], refs[4], refs[-1]
        o_ref[...] = ((own[...].astype(F32) + a_ref[...].astype(F32)) + b_ref[...].astype(F32)) \
            + c_ref[...].astype(F32)

    def slot(p):
        return pl.BlockSpec((None, tr, tc), lambda i, j, s: (p, i, j))

    in_specs = [pl.BlockSpec((None, tr, tc), lambda i, j, s: (s[0], i, j)), slot(0), slot(1), slot(2)]
    args = [where, sums, land, land, land]
    if acc is not None:
        in_specs.append(ANY)
        args.append(acc)
    return pl.pallas_call(
        body,
        grid_spec=pltpu.PrefetchScalarGridSpec(
            num_scalar_prefetch=1, grid=(nr, nc), in_specs=in_specs,
            out_specs=pl.BlockSpec((None, tr, tc), out_map)),
        out_shape=jax.ShapeDtypeStruct(shape, F32),
        input_output_aliases={5: 0} if acc is not None else {},
        compiler_params=_cparams(("parallel", "parallel")),
        name=name,
    )(*args)


def _join_halves(name, grads, dims):
    n = len(grads)

    def plan(ins, outs):
        (x, y, c), _, _, _ = _place()
        remote = []
        for w in range(n):
            d, h = dims[w], grads[w].shape[dims[w]] // 2
            mine = _window(outs[w], d, c, h)
            remote.append((mine, mine, (x, y, 1 - c), _window(outs[w], d, 1 - c, h)))
        return [], remote

    return _comm_call(name, grads, [jax.ShapeDtypeStruct(g.shape, g.dtype) for g in grads], plan, 0, n,
                      aliases={w: w for w in range(n)})


def _allreduce_small(pack):
    R, C = pack.shape

    def gather_body(in_ref, slots_ref, send_sems, recv_sems, local_sem):
        x, y, c = lax.axis_index("x"), lax.axis_index("y"), lax.axis_index("c")
        me = 4 * x + 2 * y + c
        flips = [(dx, dy, dc) for dx in (0, 1) for dy in (0, 1) for dc in (0, 1)][1:]

        def flip(v, d):
            return 1 - v if d else v

        local = pltpu.make_async_copy(in_ref, slots_ref.at[me], local_sem)
        local.start()
        sends = []
        for j, (dx, dy, dc) in enumerate(flips):
            px, py, pc = flip(x, dx), flip(y, dy), flip(c, dc)
            sends.append((pltpu.make_async_remote_copy(
                src_ref=in_ref, dst_ref=slots_ref.at[me], send_sem=send_sems.at[j], recv_sem=recv_sems.at[j],
                device_id=(px, py, pc), device_id_type=MESH), 4 * px + 2 * py + pc, j))
        for cp, _, _ in sends:
            cp.start()
        for cp, peer_id, j in sends:
            pltpu.make_async_remote_copy(
                src_ref=in_ref, dst_ref=slots_ref.at[peer_id], send_sem=send_sems.at[j], recv_sem=recv_sems.at[j],
                device_id=(x, y, c), device_id_type=MESH).wait_recv()
        for cp, _, _ in sends:
            cp.wait_send()
        local.wait()

    slots = pl.pallas_call(
        gather_body,
        in_specs=[ANY],
        out_specs=ANY,
        out_shape=jax.ShapeDtypeStruct((N_DEV, R, C), pack.dtype),
        scratch_shapes=[pltpu.SemaphoreType.DMA((N_DEV - 1,)), pltpu.SemaphoreType.DMA((N_DEV - 1,)),
                        pltpu.SemaphoreType.DMA],
        name="allgather_small",
    )(pack)

    def sum_body(s_ref, o_ref):
        acc = s_ref[0]
        for d in range(1, N_DEV):
            acc = acc + s_ref[d]
        o_ref[...] = acc

    return pl.pallas_call(
        sum_body,
        out_shape=jax.ShapeDtypeStruct((R, C), pack.dtype),
        name="sum_small",
    )(slots)


def _adamw_math(w, g, m, v):
    m2 = ADAM_B1 * m + (1.0 - ADAM_B1) * g
    v2 = ADAM_B2 * v + (1.0 - ADAM_B2) * (g * g)
    m_hat = m2 / (1.0 - ADAM_B1 ** ADAM_STEP)
    v_hat = v2 / (1.0 - ADAM_B2 ** ADAM_STEP)
    delta = -ADAM_LR * (m_hat / (jnp.sqrt(v_hat) + ADAM_EPS) + ADAM_WD * w)
    return delta, m2, v2


def _adamw_big(name, w, m, v, g_parts):
    shape = w.shape
    C = shape[-1]
    R = w.size // C
    tr, tc = _div_tile(R, 256, 8), _div_tile(C, 1024)
    n_g = len(g_parts)

    def body(*refs):
        w_ref, m_ref, v_ref = refs[:3]
        g_ref, d_ref, nm_ref, nv_ref = refs[3 + n_g:]
        g = refs[3][...]
        for extra in refs[4:3 + n_g]:
            g = g + extra[...]
        delta, m2, v2 = _adamw_math(w_ref[...], g, m_ref[...], v_ref[...])
        g_ref[...], d_ref[...], nm_ref[...], nv_ref[...] = g, delta, m2, v2

    blk = pl.BlockSpec((tr, tc), lambda i, j: (i, j))
    outs = pl.pallas_call(
        body,
        grid=(R // tr, C // tc),
        in_specs=[blk] * (3 + n_g),
        out_specs=[blk] * 4,
        out_shape=[jax.ShapeDtypeStruct((R, C), F32)] * 4,
        compiler_params=_cparams(("parallel", "parallel")),
        name=name,
    )(w.reshape(R, C), m.reshape(R, C), v.reshape(R, C), *[g.reshape(R, C) for g in g_parts])
    return [o.reshape(shape) for o in outs]


def _adamw_small(name, w, g, m, v):
    shape = w.shape
    two_d = (w.size // shape[-1], shape[-1])

    def body(w_ref, g_ref, m_ref, v_ref, d_ref, nm_ref, nv_ref):
        d_ref[...], nm_ref[...], nv_ref[...] = _adamw_math(w_ref[...], g_ref[...], m_ref[...], v_ref[...])

    outs = pl.pallas_call(
        body,
        out_shape=[jax.ShapeDtypeStruct(two_d, F32)] * 3,
        name=name,
    )(w.reshape(two_d), g.reshape(two_d), m.reshape(two_d), v.reshape(two_d))
    return [o.reshape(shape) for o in outs]


SMALL_ROWS = 40
S_BIN, S_G1, S_G2 = 16, 24, 32
MATS = ("w_in", "w_pa", "w_pb", "w_o", "w_mlp1", "w_mlp2")
LRU = ("lru_wr", "lru_wi")
BIG_DIM = dict(w_in=2, w_pa=1, w_pb=1, w_o=1, w_mlp1=2, w_mlp2=1, lru_wr=2, lru_wi=2)
WEIGHTS = ("norm1_g", "w_in", "b_in", "conv_a_w", "conv_a_b", "lru_wr", "lru_br", "lru_wi", "lru_bi", "lru_lam",
           "conv_b_w", "w_pa", "w_pb", "w_o", "norm2_g", "w_mlp1", "w_mlp2", "final_g")


def _rows_at(a, r0, total):
    pad = [(0, 0)] * a.ndim
    pad[-2] = (r0, total - r0 - a.shape[-2])
    return jnp.pad(a, pad)


def kernel(x, norm1_g, w_in, b_in, conv_a_w, conv_a_b, lru_wr, lru_br, lru_wi, lru_bi, lru_lam, conv_b_w, w_pa, w_pb, w_o, norm2_g, w_mlp1, w_mlp2, final_g, loss_target, m_norm1_g, m_w_in, m_b_in, m_conv_a_w, m_conv_a_b, m_lru_wr, m_lru_br, m_lru_wi, m_lru_bi, m_lru_lam, m_conv_b_w, m_w_pa, m_w_pb, m_w_o, m_norm2_g, m_w_mlp1, m_w_mlp2, m_final_g, v_norm1_g, v_w_in, v_b_in, v_conv_a_w, v_conv_a_b, v_lru_wr, v_lru_br, v_lru_wi, v_lru_bi, v_lru_lam, v_conv_b_w, v_w_pa, v_w_pb, v_w_o, v_norm2_g, v_w_mlp1, v_w_mlp2, v_final_g):
    wts = dict(norm1_g=norm1_g, w_in=w_in, b_in=b_in, conv_a_w=conv_a_w, conv_a_b=conv_a_b, lru_wr=lru_wr,
               lru_br=lru_br, lru_wi=lru_wi, lru_bi=lru_bi, lru_lam=lru_lam, conv_b_w=conv_b_w, w_pa=w_pa,
               w_pb=w_pb, w_o=w_o, norm2_g=norm2_g, w_mlp1=w_mlp1, w_mlp2=w_mlp2, final_g=final_g)
    mom = dict(norm1_g=m_norm1_g, w_in=m_w_in, b_in=m_b_in, conv_a_w=m_conv_a_w, conv_a_b=m_conv_a_b,
               lru_wr=m_lru_wr, lru_br=m_lru_br, lru_wi=m_lru_wi, lru_bi=m_lru_bi, lru_lam=m_lru_lam,
               conv_b_w=m_conv_b_w, w_pa=m_w_pa, w_pb=m_w_pb, w_o=m_w_o, norm2_g=m_norm2_g, w_mlp1=m_w_mlp1,
               w_mlp2=m_w_mlp2, final_g=m_final_g)
    vel = dict(norm1_g=v_norm1_g, w_in=v_w_in, b_in=v_b_in, conv_a_w=v_conv_a_w, conv_a_b=v_conv_a_b,
               lru_wr=v_lru_wr, lru_br=v_lru_br, lru_wi=v_lru_wi, lru_bi=v_lru_bi, lru_lam=v_lru_lam,
               conv_b_w=v_conv_b_w, w_pa=v_w_pa, w_pb=v_w_pb, w_o=v_w_o, norm2_g=v_norm2_g, w_mlp1=v_w_mlp1,
               w_mlp2=v_w_mlp2, final_g=v_final_g)
    depth, D = norm1_g.shape
    nb, bw = lru_wr.shape[1], lru_wr.shape[3]
    chip = 2 * lax.axis_index("x") + lax.axis_index("y")

    small_parts = [conv_a_w.reshape(-1), conv_b_w.reshape(-1), lru_br.reshape(-1), lru_bi.reshape(-1)]
    small_len = sum(p.shape[0] for p in small_parts)
    small_rows = -(-small_len // 1024) * 8
    small = jnp.concatenate(small_parts + [jnp.zeros((small_rows * 128 - small_len,), F32)]).reshape(small_rows, 128)
    gathered = _gather_weights([wts[n].astype(BF16) for n in LRU], [BIG_DIM[n] for n in LRU], small)
    full = dict(zip(LRU, gathered[:-1]))
    items = [(n, l) for l in range(depth) for n in MATS]
    mat_dims = [BIG_DIM[n] - 1 for n, _ in items]
    where = jnp.stack([chip, lax.axis_index("c")]).astype(jnp.int32)
    placed = {(n, l): _place_shard(f"place_{n}_{l}", wts[n], l, BIG_DIM[n] - 1, where) for n, l in items}
    flat = gathered[-1].reshape(N_CHIPS, small_rows * 128)
    off = 0
    small_full = []
    for part, shard in zip(small_parts, (conv_a_w, conv_b_w, lru_br, lru_bi)):
        piece = flat[:, off:off + part.shape[0]].reshape((N_CHIPS,) + shard.shape)
        small_full.append(jnp.moveaxis(piece, 0, -2).reshape(shard.shape[:-1] + (N_CHIPS * shard.shape[-1],)))
        off += part.shape[0]
    caw_f, cbw_f, br_f, bi_f = small_full
    pch = (_rows_at(conv_a_b[:, None, :], R_CAB, R_ROWS) + _rows_at(br_f.reshape(depth, 1, D), R_BR, R_ROWS)
           + _rows_at(bi_f.reshape(depth, 1, D), R_BI, R_ROWS) + _rows_at(lru_lam[:, None, :], R_LAM, R_ROWS)
           + _rows_at(caw_f, R_CAW, R_ROWS) + _rows_at(cbw_f, R_CBW, R_ROWS))
    W = dict(b_in=b_in, pch=pch, wr=full["lru_wr"], wi=full["lru_wi"], g1=norm1_g, g2=norm2_g, gf=final_g,
             core=lax.axis_index("c").astype(jnp.int32).reshape(1))

    loss_local, dx, grads, dgf, sums, landed = _local_fwd_bwd(x[0], loss_target[0], W, placed)
    loss = lax.psum(loss_local, ("x", "y", "c"))

    key = dict(w_in="w_in", w_pa="w_pa", w_pb="w_pb", w_o="w_o", w_mlp1="w1", w_mlp2="w2", lru_wr="wr", lru_wi="wi")
    out_g, out_d, out_m, out_v = {}, {}, {}, {}
    per_layer = [[grads[l][key[n]].astype(BF16) for l in range(depth)] for n in LRU]
    land = _scatter_grads(per_layer, [BIG_DIM[n] - 1 for n in LRU])
    chip_sums = [_sum_slots(f"sum_slots_{n}", ld.reshape(N_CHIPS, -1, ld.shape[-1])) for n, ld in zip(LRU, land)]
    sib_sums = _swap_with_sibling(chip_sums)
    for n, mine, sib in zip(LRU, chip_sums, sib_sums):
        out_g[n], out_d[n], out_m[n], out_v[n] = _adamw_big(f"adamw_{n}", wts[n], mom[n], vel[n], [mine, sib])
    acc = {n: None for n in MATS}
    for n, l in reversed(items):
        acc[n] = _reduce_into(f"reduce_{n}_{l}", sums[n, l], landed[n, l], acc[n], l, BIG_DIM[n], wts[n].shape, where)
    joined = _join_halves("join_halves", [acc[n] for n in MATS], [BIG_DIM[n] for n in MATS])
    for n, g in zip(MATS, joined):
        out_g[n], out_d[n], out_m[n], out_v[n] = _adamw_big(f"adamw_{n}", wts[n], mom[n], vel[n], [g])

    rows = []
    for l in range(depth):
        g = grads[l]
        rows.append(_rows_at(g["sm"], 0, SMALL_ROWS) + _rows_at(g["b_in"].reshape(7, D), S_BIN, SMALL_ROWS)
                    + _rows_at(g["g1"], S_G1, SMALL_ROWS) + _rows_at(g["g2"], S_G2, SMALL_ROWS))
    rows.append(_rows_at(dgf, 0, 8))
    tot = _allreduce_small(jnp.concatenate(rows, axis=0))
    per = tot[:depth * SMALL_ROWS].reshape(depth, SMALL_ROWS, D)

    def cols_of_chip(a, axis):
        size = a.shape[axis] // N_CHIPS
        return lax.dynamic_slice_in_dim(a, chip * size, size, axis=axis)

    small_g = dict(
        norm1_g=per[:, S_G1], b_in=per[:, S_BIN:S_BIN + 7].reshape(depth, 7 * D),
        conv_a_w=cols_of_chip(per[:, R_CAW:R_CAW + 4], 2), conv_a_b=per[:, R_CAB],
        lru_br=cols_of_chip(per[:, R_BR].reshape(depth, nb, bw), 2),
        lru_bi=cols_of_chip(per[:, R_BI].reshape(depth, nb, bw), 2), lru_lam=per[:, R_LAM],
        conv_b_w=cols_of_chip(per[:, R_CBW:R_CBW + 3], 2), norm2_g=per[:, S_G2],
        final_g=tot[depth * SMALL_ROWS])
    for n, g in small_g.items():
        out_g[n] = g
        out_d[n], out_m[n], out_v[n] = _adamw_small(f"adamw_{n}", wts[n], g, mom[n], vel[n])

    return (loss, dx[None], *[out_g[n] for n in WEIGHTS], *[out_d[n] for n in WEIGHTS],
            *[out_m[n] for n in WEIGHTS], *[out_v[n] for n in WEIGHTS])
```

```python
import functools

import jax
import jax.numpy as jnp
from jax import lax
from jax.experimental import pallas as pl
from jax.experimental.pallas import tpu as pltpu

F32 = jnp.float32
BF16 = jnp.bfloat16
MESH = pl.DeviceIdType.MESH

EPS = 1e-6
LRU_C = 8.0
ADAM_LR = 0.001
ADAM_B1 = 0.9
ADAM_B2 = 0.999
ADAM_EPS = 1e-08
ADAM_WD = 0.01
ADAM_STEP = 10

N_CHIPS = 4
N_DEV = 8
HALO = 8
VMEM_LIMIT = 56 * 1024 * 1024
MM_TILES = (1024, 1024, 2048)
MM_TILES_FUSED = (512, 1024, 2048)
SEQ_CHUNK = 256
MIXER_ROW_BLOCK = 32
ROW_TILE = 256

R_CAB, R_BR, R_BI, R_LAM, R_CAW, R_CBW, R_ROWS = 0, 1, 2, 3, 4, 8, 16


def _cparams(sem):
    return pltpu.CompilerParams(dimension_semantics=sem, vmem_limit_bytes=VMEM_LIMIT)


def _div_tile(n, pref, unit=128):
    if n <= pref:
        return n
    t = (pref // unit) * unit
    while n % t:
        t -= unit
    return t


def _sigmoid(v):
    return 1.0 / (1.0 + jnp.exp(-v))


def _gelu_and_grad(y):
    k = 0.7978845608028654
    c = 0.044715
    y2 = y * y
    t = jnp.tanh(k * (y + c * y2 * y))
    g = 0.5 * y * (1.0 + t)
    gp = 0.5 * (1.0 + t) + 0.5 * y * (1.0 - t * t) * (k * (1.0 + 3.0 * c * y2))
    return g, gp


def _softplus_neg(lam):
    e = jnp.exp(-jnp.abs(lam))
    w = 1.0 + e
    l1p = jnp.where(w == 1.0, e, jnp.log(w) * e / jnp.where(w == 1.0, 1.0, w - 1.0))
    return jnp.maximum(-lam, 0.0) + l1p


def _mm(name, mode, a, b, M, N, K, out_dtypes, epilogue=None, extras=(), la=None, lb=None, tiles=None,
        carry=None):
    tiles = MM_TILES if tiles is None else tiles
    tm, tn, tk = _div_tile(M, tiles[0]), _div_tile(N, tiles[1]), _div_tile(K, tiles[2])
    assert M % tm == 0 and N % tn == 0 and K % tk == 0, (name, M, N, K)
    nk = K // tk

    def spec(lead, shape, imap):
        if lead is None:
            return pl.BlockSpec(shape, imap)
        return pl.BlockSpec((None,) + shape, lambda i, j, k: (lead,) + imap(i, j, k))

    if mode == "nn":
        a_spec = spec(la, (tm, tk), lambda i, j, k: (i, k))
        b_spec = spec(lb, (tk, tn), lambda i, j, k: (k, j))
        dn = (((1,), (0,)), ((), ()))
    elif mode == "nt":
        a_spec = spec(la, (tm, tk), lambda i, j, k: (i, k))
        b_spec = spec(lb, (tn, tk), lambda i, j, k: (j, k))
        dn = (((1,), (1,)), ((), ()))
    else:
        a_spec = spec(la, (tk, tm), lambda i, j, k: (k, i))
        b_spec = spec(lb, (tk, tn), lambda i, j, k: (k, j))
        dn = (((0,), (0,)), ((), ()))

    ex_arrays, ex_specs = [], []
    for arr, kind, off in extras:
        ex_arrays.append(arr)
        if kind == "bias":
            ex_specs.append(pl.BlockSpec((1, tn), lambda i, j, k: (0, j)))
        else:
            assert off % tn == 0
            ex_specs.append(pl.BlockSpec((tm, tn), lambda i, j, k, o=off // tn: (i, j + o)))
    n_ex, n_out = len(ex_arrays), len(out_dtypes)
    n_cin = len(carry.ins) if carry else 0
    n_cout = len(carry.out_shapes) if carry else 0
    n_in = 2 + n_ex + n_cin
    gi, gj = M // tm, N // tn

    def body(*refs):
        a_ref, b_ref = refs[0], refs[1]
        ex = refs[2:2 + n_ex]
        outs = refs[n_in:n_in + n_out]
        acc = refs[n_in + n_out + n_cout]
        i, j, k = pl.program_id(0), pl.program_id(1), pl.program_id(2)
        if carry:
            c_in, c_out = refs[2 + n_ex:n_in], refs[n_in + n_out:n_in + n_out + n_cout]
            sems = refs[n_in + n_out + n_cout + 1:]
            step = (i * gj + j) * nk + k
            carry.ride(step, gi * gj * nk, c_in, c_out, sems, True)

        def product():
            return lax.dot_general(a_ref[...], b_ref[...], dn, preferred_element_type=F32)

        def finish(r):
            vals = (r,) if epilogue is None else epilogue(r, *[e[...] for e in ex])
            for o, v in zip(outs, vals):
                o[...] = v.astype(o.dtype)

        if nk == 1:
            finish(product())
        else:
            @pl.when(k == 0)
            def _():
                acc[...] = product()

            @pl.when((k > 0) & (k < nk - 1))
            def _():
                acc[...] += product()

            @pl.when(k == nk - 1)
            def _():
                finish(acc[...] + product())

        if carry:
            carry.ride(step, gi * gj * nk, c_in, c_out, sems, False)

    res = pl.pallas_call(
        body,
        grid=(gi, gj, nk),
        in_specs=[a_spec, b_spec, *ex_specs] + [ANY] * n_cin,
        out_specs=[pl.BlockSpec((tm, tn), lambda i, j, k: (i, j)) for _ in range(n_out)] + [ANY] * n_cout,
        out_shape=[jax.ShapeDtypeStruct((M, N), d) for d in out_dtypes] + (carry.out_shapes if carry else []),
        scratch_shapes=[pltpu.VMEM((tm, tn) if nk > 1 else (8, 128), F32)] + (carry.scratch() if carry else []),
        input_output_aliases={2 + n_ex + ci: n_out + co for ci, co in carry.aliases.items()} if carry else {},
        compiler_params=_cparams(("arbitrary",) * 3 if carry else ("parallel", "parallel", "arbitrary")),
        name=name,
    )(a, b, *ex_arrays, *(carry.ins if carry else []))
    main = res[0] if n_out == 1 else res[:n_out]
    return (main, res[n_out:]) if carry else main


def _rms_fwd(name, x, g_row):
    T, D = x.shape
    tm = min(ROW_TILE, T)

    def body(x_ref, g_ref, h_ref):
        xv = x_ref[...]
        r = lax.rsqrt(jnp.mean(xv * xv, axis=-1, keepdims=True) + EPS)
        h_ref[...] = (xv * r * g_ref[...]).astype(BF16)

    return pl.pallas_call(
        body,
        grid=(T // tm,),
        in_specs=[pl.BlockSpec((tm, D), lambda i: (i, 0)), pl.BlockSpec((1, D), lambda i: (0, 0))],
        out_specs=pl.BlockSpec((tm, D), lambda i: (i, 0)),
        out_shape=jax.ShapeDtypeStruct((T, D), BF16),
        compiler_params=_cparams(("parallel",)),
        name=name,
    )(x, g_row)


def _rms_bwd(name, x, g_row, dh, dres):
    T, D = x.shape
    tm = min(ROW_TILE, T)

    def body(x_ref, g_ref, dh_ref, dres_ref, dx_ref, dxb_ref, dg_ref):
        xv, dhv = x_ref[...], dh_ref[...]
        r = lax.rsqrt(jnp.mean(xv * xv, axis=-1, keepdims=True) + EPS)
        gd = g_ref[...] * dhv
        c = jnp.mean(xv * gd, axis=-1, keepdims=True)
        dx = r * gd - xv * (r * r * r) * c + dres_ref[...]
        dx_ref[...] = dx
        dxb_ref[...] = dx.astype(BF16)

        @pl.when(pl.program_id(0) == 0)
        def _():
            dg_ref[...] = jnp.zeros_like(dg_ref)

        dg_ref[...] += jnp.sum(dhv * xv * r, axis=0, keepdims=True)

    row = pl.BlockSpec((tm, D), lambda i: (i, 0))
    vec = pl.BlockSpec((1, D), lambda i: (0, 0))
    return pl.pallas_call(
        body,
        grid=(T // tm,),
        in_specs=[row, vec, row, row],
        out_specs=[row, row, vec],
        out_shape=[jax.ShapeDtypeStruct((T, D), F32), jax.ShapeDtypeStruct((T, D), BF16),
                   jax.ShapeDtypeStruct((1, D), F32)],
        compiler_params=_cparams(("arbitrary",)),
        name=name,
    )(x, g_row, dh, dres)


def _loss_head(name, x, g_row, tgt):
    T, D = x.shape
    tm = min(ROW_TILE, T)

    def body(x_ref, g_ref, t_ref, dx_ref, dxb_ref, dg_ref, loss_ref):
        xv, g = x_ref[...], g_ref[...]
        r = lax.rsqrt(jnp.mean(xv * xv, axis=-1, keepdims=True) + EPS)
        xh = xv * r
        e = xh * g - t_ref[...]
        lpart = 0.5 * jnp.sum(jnp.mean(e * e, axis=-1, keepdims=True))
        dy = e * (1.0 / D)
        gd = g * dy
        c = jnp.mean(xv * gd, axis=-1, keepdims=True)
        dx = r * gd - xv * (r * r * r) * c
        dx_ref[...] = dx
        dxb_ref[...] = dx.astype(BF16)

        @pl.when(pl.program_id(0) == 0)
        def _():
            dg_ref[...] = jnp.zeros_like(dg_ref)
            loss_ref[...] = jnp.zeros_like(loss_ref)

        dg_ref[...] += jnp.sum(dy * xh, axis=0, keepdims=True)
        loss_ref[...] += jnp.full(loss_ref.shape, lpart, F32)

    row = pl.BlockSpec((tm, D), lambda i: (i, 0))
    vec = pl.BlockSpec((1, D), lambda i: (0, 0))
    return pl.pallas_call(
        body,
        grid=(T // tm,),
        in_specs=[row, vec, row],
        out_specs=[row, row, vec, pl.BlockSpec((8, 128), lambda i: (0, 0))],
        out_shape=[jax.ShapeDtypeStruct((T, D), F32), jax.ShapeDtypeStruct((T, D), BF16),
                   jax.ShapeDtypeStruct((1, D), F32), jax.ShapeDtypeStruct((8, 128), F32)],
        compiler_params=_cparams(("arbitrary",)),
        name=name,
    )(x, g_row, tgt)


def _colsum(name, a, carry=None):
    T, N = a.shape
    tm, tn = min(512, T), _div_tile(N, 2048)
    gj, gi = N // tn, T // tm
    n_cin = len(carry.ins) if carry else 0
    n_cout = len(carry.out_shapes) if carry else 0

    def body(*refs):
        a_ref, o_ref = refs[0], refs[1 + n_cin]
        j, i = pl.program_id(0), pl.program_id(1)
        if carry:
            c_in, c_out, sems = refs[1:1 + n_cin], refs[2 + n_cin:2 + n_cin + n_cout], refs[2 + n_cin + n_cout:]
            carry.ride(j * gi + i, gj * gi, c_in, c_out, sems, True)

        @pl.when(i == 0)
        def _():
            o_ref[...] = jnp.zeros_like(o_ref)

        o_ref[...] += jnp.sum(a_ref[...].astype(F32), axis=0, keepdims=True)

        if carry:
            carry.ride(j * gi + i, gj * gi, c_in, c_out, sems, False)

    res = pl.pallas_call(
        body,
        grid=(gj, gi),
        in_specs=[pl.BlockSpec((tm, tn), lambda j, i: (i, j))] + [ANY] * n_cin,
        out_specs=[pl.BlockSpec((1, tn), lambda j, i: (0, j))] + [ANY] * n_cout,
        out_shape=[jax.ShapeDtypeStruct((1, N), F32)] + (carry.out_shapes if carry else []),
        scratch_shapes=carry.scratch() if carry else [],
        compiler_params=_cparams(("arbitrary", "arbitrary") if carry else ("parallel", "arbitrary")),
        name=name,
    )(a, *(carry.ins if carry else []))
    return (res[0], res[1:]) if carry else res[0]


def _tile_scan(a, b, row, reverse):
    for s in (1, 2, 4):
        if reverse:
            a_s, b_s, m = pltpu.roll(a, 8 - s, 0), pltpu.roll(b, 8 - s, 0), row < 8 - s
        else:
            a_s, b_s, m = pltpu.roll(a, s, 0), pltpu.roll(b, s, 0), row >= s
        b = jnp.where(m, a * b_s + b, b)
        a = jnp.where(m, a * a_s, a)
    return a, b


def _chunk_scan(a_s, b_s, out_ref, carry, n_tiles, width, reverse):
    row = lax.broadcasted_iota(jnp.int32, (8, width), 0)
    edge = 0 if reverse else 7

    group = 4 if n_tiles % 4 == 0 else 1

    def step(j, c):
        jj = (n_tiles // group - 1 - j) if reverse else j
        base = pl.multiple_of(jj * (8 * group), 8 * group)
        order = range(group - 1, -1, -1) if reverse else range(group)
        parts = {t: _tile_scan(a_s[pl.ds(base + 8 * t, 8), :], b_s[pl.ds(base + 8 * t, 8), :], row, reverse)
                 for t in order}
        for t in order:
            h = parts[t][0] * c + parts[t][1]
            out_ref[pl.ds(base + 8 * t, 8), :] = h
            c = jnp.broadcast_to(h[edge:edge + 1, :], (8, width))
        return c

    carry[...] = lax.fori_loop(0, n_tiles // group, step, carry[...])


def _gates(xc, p_ref, wr_ref, wi_ref):
    xcb = xc.astype(BF16)
    r = _sigmoid(jnp.dot(xcb, wr_ref[...], preferred_element_type=F32) + p_ref[R_BR:R_BR + 1, :])
    ig = _sigmoid(jnp.dot(xcb, wi_ref[...], preferred_element_type=F32) + p_ref[R_BI:R_BI + 1, :])
    sp = _softplus_neg(p_ref[R_LAM:R_LAM + 1, :])
    log_a = (-LRU_C) * r * sp
    a = jnp.exp(log_a)
    t = jnp.tanh(log_a)
    mult = jnp.sqrt(-2.0 * t / (1.0 - t))
    return xcb, r, ig, sp, a, mult


def _mixer_specs(Tc, bw, nb, layer):
    def seg(s):
        return pl.BlockSpec((Tc, bw), lambda n, i: (i, s * nb + n))

    p_spec = pl.BlockSpec((None, R_ROWS, bw), lambda n, i: (layer, 0, n))
    w_spec = pl.BlockSpec((None, None, bw, bw), lambda n, i: (layer, n, 0, 0))
    return seg, p_spec, w_spec


def _mixer_fwd(name, layer, z, pch, wr, wi, comm=None):
    T, D = z.shape[0], z.shape[1] // 7
    bw, nb = wr.shape[-1], wr.shape[1]
    Tc = min(SEQ_CHUNK, T)
    nT = T // Tc
    n_cin = len(comm.ins) if comm else 0
    n_cout = len(comm.out_shapes) if comm else 0

    def body(*refs):
        xa_ref, ya_ref, cb_ref, cc_ref, cx_ref, p_ref, wr_ref, wi_ref = refs[:8]
        pa_ref, pb_ref, xc_ref, hl_ref, vb_ref = refs[8 + n_cin:13 + n_cin]
        xa_buf, u_buf, a_s, b_s, carry = refs[13 + n_cin + n_cout:18 + n_cin + n_cout]
        if comm:
            c_in, c_out, sems = refs[8:8 + n_cin], refs[13 + n_cin:13 + n_cin + n_cout], refs[18 + n_cin + n_cout:]
            step = pl.program_id(0) * nT + pl.program_id(1)
            comm.ride(step, nb * nT, c_in, c_out, sems, True)

        @pl.when(pl.program_id(1) == 0)
        def _():
            xa_buf[0:HALO, :] = jnp.zeros((HALO, bw), F32)
            u_buf[0:HALO, :] = jnp.zeros((HALO, bw), F32)
            carry[...] = jnp.zeros_like(carry)

        xa_buf[HALO:HALO + Tc, :] = xa_ref[...]
        xc = p_ref[R_CAB:R_CAB + 1, :]
        for k in range(4):
            xc = xc + p_ref[R_CAW + k:R_CAW + k + 1, :] * xa_buf[HALO - 3 + k:HALO - 3 + k + Tc, :]
        xc_ref[...] = xc
        _, _, ig, _, a, mult = _gates(xc, p_ref, wr_ref, wi_ref)
        a_s[...] = a
        b_s[...] = mult * (ig * xc)
        _chunk_scan(a_s, b_s, hl_ref, carry, Tc // 8, bw, False)
        g, _ = _gelu_and_grad(ya_ref[...])
        pa_ref[...] = (hl_ref[...] * g).astype(BF16)

        u_buf[HALO:HALO + Tc, :] = cc_ref[...] * cx_ref[...]
        vb = jnp.zeros((Tc, bw), F32)
        for k in range(3):
            vb = vb + p_ref[R_CBW + k:R_CBW + k + 1, :] * u_buf[HALO - 2 + k:HALO - 2 + k + Tc, :]
        vb_ref[...] = vb
        pb_ref[...] = (cb_ref[...] * vb).astype(BF16)
        xa_buf[0:HALO, :] = xa_buf[Tc:Tc + HALO, :]
        u_buf[0:HALO, :] = u_buf[Tc:Tc + HALO, :]

        if comm:
            comm.ride(step, nb * nT, c_in, c_out, sems, False)

    seg, p_spec, w_spec = _mixer_specs(Tc, bw, nb, layer)
    out = pl.BlockSpec((Tc, bw), lambda n, i: (i, n))
    res = pl.pallas_call(
        body,
        grid=(nb, nT),
        in_specs=[seg(0), seg(1), seg(2), seg(3), seg(4), p_spec, w_spec, w_spec] + [ANY] * n_cin,
        out_specs=[out] * 5 + [ANY] * n_cout,
        out_shape=[jax.ShapeDtypeStruct((T, D), BF16), jax.ShapeDtypeStruct((T, D), BF16),
                   jax.ShapeDtypeStruct((T, D), F32), jax.ShapeDtypeStruct((T, D), F32),
                   jax.ShapeDtypeStruct((T, D), F32)] + (comm.out_shapes if comm else []),
        scratch_shapes=[pltpu.VMEM((Tc + HALO, bw), F32), pltpu.VMEM((Tc + HALO, bw), F32),
                        pltpu.VMEM((Tc, bw), F32), pltpu.VMEM((Tc, bw), F32), pltpu.VMEM((8, bw), F32)]
        + (comm.scratch() if comm else []),
        input_output_aliases={8 + ci: 5 + co for ci, co in comm.aliases.items()} if comm else {},
        compiler_params=_cparams(("arbitrary", "arbitrary") if comm else ("parallel", "arbitrary")),
        name=name,
    )(z, z, z, z, z, pch, wr, wi, *(comm.ins if comm else []))
    return (res[:5], res[5:]) if comm else res


def _mixer_bwd(name, layer, z, xc, hl, vb, dpa, dpb, dga, dgb, pch, wr, wi, comm=None):
    T, D = z.shape[0], z.shape[1] // 7
    bw, nb = wr.shape[-1], wr.shape[1]
    Tc = min(SEQ_CHUNK, T)
    nT = T // Tc
    tpc = Tc // 8
    rb = min(MIXER_ROW_BLOCK, Tc)
    n_cin = len(comm.ins) if comm else 0
    n_cout = len(comm.out_shapes) if comm else 0
    total = nb * nT

    def body(*refs):
        (xa_ref, ya_ref, cb_ref, cc_ref, cx_ref, xc_ref, hl_ref, hp_ref, vb_ref, dpa_ref, dpb_ref, dga_ref, dgb_ref,
         p_ref, wr_ref, wi_ref) = refs[:16]
        dz_ref, dwr_ref, dwi_ref, sm_ref = refs[16 + n_cin:20 + n_cin]
        (h_buf, a_buf, dxc_buf, dvb_buf, a_s, d_s, lam_s, r_s, i_s, m_s, dpr_s, dpi_s, sm8,
         carry, stage, out_sems) = refs[20 + n_cin + n_cout:36 + n_cin + n_cout]
        i = pl.program_id(1)
        step = pl.program_id(0) * nT + i
        if comm:
            c_in, c_out = refs[16:16 + n_cin], refs[20 + n_cin:20 + n_cin + n_cout]
            sems = refs[36 + n_cin + n_cout:]
            comm.ride(step, total, c_in, c_out, sems, True)

        slot = step % 2

        def out_copies(sl):
            rows = pl.ds(pl.multiple_of((nT - 1 - i) * Tc, Tc), Tc)
            return [pltpu.make_async_copy(
                stage.at[sl, s], dz_ref.at[rows, pl.ds(pl.multiple_of((s * nb + pl.program_id(0)) * bw, bw), bw)],
                out_sems.at[sl, s]) for s in range(7)]

        @pl.when(step >= 2)
        def _():
            for cp in out_copies(slot):
                cp.wait()

        dxa_ref, dya_ref, dcb_ref, dcc_ref, dcx_ref = [stage.at[slot, s] for s in range(5)]
        stage[slot, 5, :, :] = dga_ref[...]
        stage[slot, 6, :, :] = dgb_ref[...]

        @pl.when(i == 0)
        def _():
            a_buf[Tc:Tc + HALO, :] = jnp.zeros((HALO, bw), F32)
            dxc_buf[Tc:Tc + HALO, :] = jnp.zeros((HALO, bw), F32)
            dvb_buf[Tc:Tc + HALO, :] = jnp.zeros((HALO, bw), F32)
            carry[...] = jnp.zeros_like(carry)
            dwr_ref[...] = jnp.zeros_like(dwr_ref)
            dwi_ref[...] = jnp.zeros_like(dwi_ref)
            sm8[...] = jnp.zeros_like(sm8)

        blocks = [(g * rb, slice(g * rb, (g + 1) * rb)) for g in range(Tc // rb)]

        def prow(k):
            return p_ref[k:k + 1, :]

        def part8(v):
            return jnp.sum(v.reshape(rb // 8, 8, bw), axis=0)

        sums8 = {}

        def tally(k, v):
            sums8[k] = sums8[k] + part8(v) if k in sums8 else part8(v)

        sp = _softplus_neg(prow(R_LAM))
        xcb = xc_ref[...].astype(BF16)
        r_s[...] = jnp.dot(xcb, wr_ref[...], preferred_element_type=F32)
        i_s[...] = jnp.dot(xcb, wi_ref[...], preferred_element_type=F32)
        for _, rs in blocks:
            r = _sigmoid(r_s[rs, :] + prow(R_BR))
            ig = _sigmoid(i_s[rs, :] + prow(R_BI))
            log_a = (-LRU_C) * r * sp
            t = jnp.tanh(log_a)
            r_s[rs, :] = r
            i_s[rs, :] = ig
            m_s[rs, :] = jnp.sqrt(-2.0 * t / (1.0 - t))
            a_buf[rs, :] = jnp.exp(log_a)
            g, gp = _gelu_and_grad(ya_ref[rs, :])
            dpav = dpa_ref[rs, :]
            d_s[rs, :] = dpav * g
            dya_ref[rs, :] = (dpav * hl_ref[rs, :] * gp).astype(BF16)
            dpbv = dpb_ref[rs, :]
            dcb_ref[rs, :] = (dpbv * vb_ref[rs, :]).astype(BF16)
            dvb_buf[rs, :] = dpbv * cb_ref[rs, :]

        a_s[...] = a_buf[1:Tc + 1, :]
        _chunk_scan(a_s, d_s, lam_s, carry, tpc, bw, True)
        h_buf[HALO:HALO + Tc, :] = hl_ref[...]
        h_buf[0:HALO, :] = jnp.where(i == nT - 1, 0.0, hp_ref[...])

        for r0, rs in blocks:
            lamv, xcv, r, ig, a = lam_s[rs, :], xc_ref[rs, :], r_s[rs, :], i_s[rs, :], a_buf[rs, :]
            mult = m_s[rs, :]
            da = lamv * h_buf[HALO - 1 + r0:HALO - 1 + r0 + rb, :]
            dmult = lamv * (ig * xcv)
            dbx = lamv * mult
            dig = dbx * xcv
            dxc_buf[rs, :] = dbx * ig
            dlog_a = da * a - dmult * (a * a) / mult
            dpr = (dlog_a * ((-LRU_C) * sp)) * r * (1.0 - r)
            dpi = dig * ig * (1.0 - ig)
            dpr_s[rs, :] = dpr.astype(BF16)
            dpi_s[rs, :] = dpi.astype(BF16)
            tally(R_BR, dpr)
            tally(R_BI, dpi)
            tally(R_LAM, dlog_a * ((-LRU_C) * r))

        dprb, dpib = dpr_s[...], dpi_s[...]
        nt = (((1,), (1,)), ((), ()))
        tn = (((0,), (0,)), ((), ()))
        dxc_buf[0:Tc, :] += (lax.dot_general(dprb, wr_ref[...], nt, preferred_element_type=F32)
                             + lax.dot_general(dpib, wi_ref[...], nt, preferred_element_type=F32))
        dwr_ref[...] += lax.dot_general(xcb, dprb, tn, preferred_element_type=F32)
        dwi_ref[...] += lax.dot_general(xcb, dpib, tn, preferred_element_type=F32)

        for r0, rs in blocks:
            xav = xa_ref[rs, :]
            tally(R_CAB, dxc_buf[rs, :])
            dxa = jnp.zeros((rb, bw), F32)
            for k in range(4):
                sh = dxc_buf[3 - k + r0:3 - k + r0 + rb, :]
                dxa = dxa + prow(R_CAW + k) * sh
                tally(R_CAW + k, xav * sh)
            dxa_ref[rs, :] = dxa.astype(BF16)
            ccv, cxv = cc_ref[rs, :], cx_ref[rs, :]
            u = ccv * cxv
            du = jnp.zeros((rb, bw), F32)
            for k in range(3):
                sh = dvb_buf[2 - k + r0:2 - k + r0 + rb, :]
                du = du + prow(R_CBW + k) * sh
                tally(R_CBW + k, u * sh)
            dcc_ref[rs, :] = (du * cxv).astype(BF16)
            dcx_ref[rs, :] = (du * ccv).astype(BF16)

        for k, v in sums8.items():
            sm8[8 * k:8 * k + 8, :] += v
        a_buf[Tc:Tc + HALO, :] = a_buf[0:HALO, :]
        dxc_buf[Tc:Tc + HALO, :] = dxc_buf[0:HALO, :]
        dvb_buf[Tc:Tc + HALO, :] = dvb_buf[0:HALO, :]

        @pl.when(i == nT - 1)
        def _():
            sm_ref[...] = jnp.sum(sm8[...].reshape(R_ROWS, 8, bw), axis=1)
            sm_ref[R_LAM:R_LAM + 1, :] = sm_ref[R_LAM:R_LAM + 1, :] * (-_sigmoid(-prow(R_LAM)))

        for cp in out_copies(slot):
            cp.start()

        @pl.when(step == total - 1)
        def _():
            if total > 1:
                for cp in out_copies(1 - slot):
                    cp.wait()
            for cp in out_copies(slot):
                cp.wait()

        if comm:
            comm.ride(step, total, c_in, c_out, sems, False)

    def seg(s):
        return pl.BlockSpec((Tc, bw), lambda n, i: (nT - 1 - i, s * nb + n))

    blk = pl.BlockSpec((Tc, bw), lambda n, i: (nT - 1 - i, n))
    halo = pl.BlockSpec((8, bw), lambda n, i: (jnp.maximum((nT - 1 - i) * tpc - 1, 0), n))
    p_spec = pl.BlockSpec((None, R_ROWS, bw), lambda n, i: (layer, 0, n))
    w_spec = pl.BlockSpec((None, None, bw, bw), lambda n, i: (layer, n, 0, 0))
    dw_spec = pl.BlockSpec((None, bw, bw), lambda n, i: (n, 0, 0))
    res = pl.pallas_call(
        body,
        grid=(nb, nT),
        in_specs=[seg(0), seg(1), seg(2), seg(3), seg(4), blk, blk, halo, blk, blk, blk, blk, blk,
                  p_spec, w_spec, w_spec] + [ANY] * n_cin,
        out_specs=[ANY, dw_spec, dw_spec, pl.BlockSpec((R_ROWS, bw), lambda n, i: (0, n))] + [ANY] * n_cout,
        out_shape=[jax.ShapeDtypeStruct((T, 7 * D), BF16), jax.ShapeDtypeStruct((nb, bw, bw), F32),
                   jax.ShapeDtypeStruct((nb, bw, bw), F32), jax.ShapeDtypeStruct((R_ROWS, D), F32)]
        + (comm.out_shapes if comm else []),
        scratch_shapes=[pltpu.VMEM((Tc + HALO, bw), F32)] * 4 + [pltpu.VMEM((Tc, bw), F32)] * 6
        + [pltpu.VMEM((Tc, bw), BF16)] * 2 + [pltpu.VMEM((8 * R_ROWS, bw), F32), pltpu.VMEM((8, bw), F32),
                                              pltpu.VMEM((2, 7, Tc, bw), BF16), pltpu.SemaphoreType.DMA((2, 7))]
        + (comm.scratch() if comm else []),
        input_output_aliases={16 + ci: 4 + co for ci, co in comm.aliases.items()} if comm else {},
        compiler_params=_cparams(("arbitrary", "arbitrary")),
        name=name,
    )(z, z, z, z, z, xc, hl, hl, vb, dpa, dpb, dga, dgb, pch, wr, wi, *(comm.ins if comm else []))
    return (res[:4], res[4:]) if comm else res


def _local_fwd_bwd(x, tgt, W, placed=None):
    T, D = x.shape
    g1, g2 = W["g1"], W["g2"]
    depth = g1.shape[0]
    FF = 4 * D
    if placed is None:
        mats = {(n, l): W[n][l] for n in MATS for l in range(depth)}
    else:
        mats = {}
        mats["w_in", 0], = _run_carry("gather_first", _gather_carry([placed["w_in", 0]], [(0, 1, 0, 1)]))

    def gathering(specs):
        if placed is None or not specs:
            return None, []
        keys = [(n, l) for n, l, _, _, _ in specs]
        arrays = [mats.get(k, placed[k]) for k in keys]
        return _gather_carry(arrays, [(i, d, part, nparts) for i, (_, _, d, part, nparts) in enumerate(specs)]), keys

    def hosted(call, specs, **kw):
        carry, keys = gathering(specs)
        if carry is None:
            return call(**kw)
        res, got = call(**kw, **{("comm" if call.func is _mixer_fwd else "carry"): carry})
        mats.update(zip(keys, got))
        return res

    saved = []
    xs = x
    for l in range(depth):
        h = _rms_fwd(f"rms1_fwd_{l}", xs, g1[l][None])
        nxt = l + 1 < depth
        projs = [("w_pa", l, 0, 0, 1), ("w_pb", l, 0, 0, 1), ("w_o", l, 0, 0, 1)]
        z = hosted(functools.partial(_mm, f"in_proj_{l}", "nn", h, mats["w_in", l], T, 7 * D, D, [F32]),
                   projs + ([("w_mlp1", l, 1, 0, 1)] if nxt else []),
                   epilogue=lambda acc, b: (acc + b,), extras=[(W["b_in"][l][None], "bias", 0)])
        pa, pb, xc, hl, vb = hosted(
            functools.partial(_mixer_fwd, f"mixer_fwd_{l}", l, z, W["pch"], W["wr"], W["wi"]),
            [("w_mlp2", l, 0, 0, 1)] if nxt else [("w_mlp1", l, 1, 0, 1)])
        oa = _mm(f"proj_a_{l}", "nn", pa, mats["w_pa", l], T, D, D, [F32])

        def merge(acc, oav, ga, gb):
            return acc, _sigmoid(ga) * oav + _sigmoid(gb) * acc

        ob, mg = _mm(f"proj_b_merge_{l}", "nn", pb, mats["w_pb", l], T, D, D, [F32, BF16], epilogue=merge,
                     tiles=MM_TILES_FUSED,
                     extras=[(oa, "tile", 0), (z, "tile", 5 * D), (z, "tile", 6 * D)])
        x1 = _mm(f"out_proj_{l}", "nn", mg, mats["w_o", l], T, D, D, [F32],
                 epilogue=lambda acc, res: (res + acc,), extras=[(xs, "tile", 0)])
        h2 = _rms_fwd(f"rms2_fwd_{l}", x1, g2[l][None])

        def relu2(acc):
            pr = jnp.maximum(acc, 0.0)
            return pr * pr, pr

        u, pr = hosted(functools.partial(_mm, f"mlp1_{l}", "nn", h2, mats["w_mlp1", l], T, FF, D, [BF16, BF16]),
                       [("w_in", l + 1, 1, 0, 2)] if nxt else [("w_mlp2", l, 0, 0, 1)], epilogue=relu2)
        x2 = hosted(functools.partial(_mm, f"mlp2_{l}", "nn", u, mats["w_mlp2", l], T, D, FF, [F32]),
                    [("w_in", l + 1, 1, 1, 2)] if nxt else [],
                    epilogue=lambda acc, res: (res + acc,), extras=[(x1, "tile", 0)])
        saved.append(dict(x0=xs, h=h, z=z, pa=pa, pb=pb, xc=xc, hl=hl, vb=vb, oa=oa, ob=ob, mg=mg, x1=x1,
                          h2=h2, u=u, pr=pr))
        xs = x2

    dx, dxb, dgf, loss_blk = _loss_head("loss_head", xs, W["gf"][None], tgt)

    gmat, got, sums, landed = {}, {}, {}, {}

    def reducing(call, swaps=(), scatters=(), mixer=False, **kw):
        swaps, scatters = [(n, l) for n in swaps], [(n, l) for n in scatters]
        if placed is None:
            return call(**kw)
        cs = _swap_carry([gmat[k] for k in swaps], [BIG_DIM[k[0]] - 1 for k in swaps]) if swaps else None
        cc = _scatter_carry([sums[k] for k in scatters]) if scatters else None
        res, moved = call(**kw, **{("comm" if mixer else "carry"): _merge_carries([cs, cc])})
        got.update(zip(swaps, moved[:len(swaps)]))
        landed.update(zip(scatters, moved[len(swaps):]))
        return res

    def add(names):
        if placed is not None:
            for n in names:
                sums[n, l] = _add_halves(f"add_halves_{n}_{l}", gmat[n, l], got[n, l], BIG_DIM[n] - 1, W["core"])

    grads = [None] * depth
    for l in reversed(range(depth)):
        s = saved[l]
        dp = _mm(f"mlp2_dx_{l}", "nt", dxb, mats["w_mlp2", l], T, FF, D, [BF16],
                 epilogue=lambda acc, prv: (2.0 * prv.astype(F32) * acc,), extras=[(s["pr"], "tile", 0)])
        dw2 = gmat["w_mlp2", l] = _mm(f"mlp2_dw_{l}", "tn", s["u"], dxb, FF, D, T, [BF16])
        dh2 = reducing(functools.partial(_mm, f"mlp1_dx_{l}", "nt", dp, mats["w_mlp1", l], T, D, FF, [F32]),
                       swaps=["w_mlp2"])
        add(["w_mlp2"])
        dw1 = gmat["w_mlp1", l] = reducing(functools.partial(_mm, f"mlp1_dw_{l}", "tn", s["h2"], dp, D, FF, T, [BF16]),
                                           scatters=["w_mlp2"])
        dx1, dx1b, dg2 = _rms_bwd(f"rms2_bwd_{l}", s["x1"], g2[l][None], dh2, dx)

        def unmerge(acc, ga, gb, oav, obv):
            sa, sb = _sigmoid(ga), _sigmoid(gb)
            return acc * sa, acc * sb, acc * oav * sa * (1.0 - sa), acc * obv * sb * (1.0 - sb)

        doa, dob, dga, dgb = reducing(
            functools.partial(_mm, f"out_proj_dx_{l}", "nt", dx1b, mats["w_o", l], T, D, D, [BF16] * 4),
            swaps=["w_mlp1"], tiles=MM_TILES_FUSED, epilogue=unmerge,
            extras=[(s["z"], "tile", 5 * D), (s["z"], "tile", 6 * D), (s["oa"], "tile", 0), (s["ob"], "tile", 0)])
        add(["w_mlp1"])
        dwo = gmat["w_o", l] = _mm(f"out_proj_dw_{l}", "tn", s["mg"], dx1b, D, D, T, [BF16])
        dpa = _mm(f"proj_a_dx_{l}", "nt", doa, mats["w_pa", l], T, D, D, [F32])
        dwpa = gmat["w_pa", l] = _mm(f"proj_a_dw_{l}", "tn", s["pa"], doa, D, D, T, [BF16])
        dpb = _mm(f"proj_b_dx_{l}", "nt", dob, mats["w_pb", l], T, D, D, [F32])
        dwpb = gmat["w_pb", l] = _mm(f"proj_b_dw_{l}", "tn", s["pb"], dob, D, D, T, [BF16])
        dz, dwr, dwi, sm = reducing(
            functools.partial(_mixer_bwd, f"mixer_bwd_{l}", l, s["z"], s["xc"], s["hl"], s["vb"], dpa, dpb, dga, dgb,
                              W["pch"], W["wr"], W["wi"]),
            swaps=["w_o", "w_pa", "w_pb"], scatters=["w_mlp1"], mixer=True)
        add(["w_o", "w_pa", "w_pb"])
        dwin = gmat["w_in", l] = reducing(
            functools.partial(_mm, f"in_proj_dw_{l}", "tn", s["h"], dz, D, 7 * D, T, [BF16]),
            scatters=["w_o", "w_pa", "w_pb"])
        dbin = reducing(functools.partial(_colsum, f"bias_grad_{l}", dz), swaps=["w_in"])
        add(["w_in"])
        dh = reducing(functools.partial(_mm, f"in_proj_dx_{l}", "nt", dz, mats["w_in", l], T, D, 7 * D, [F32]),
                      scatters=["w_in"])
        dx, dxb, dg1 = _rms_bwd(f"rms1_bwd_{l}", s["x0"], g1[l][None], dh, dx1)
        grads[l] = dict(w_in=dwin, w_pa=dwpa, w_pb=dwpb, w_o=dwo, w_mlp1=dw1, w_mlp2=dw2, wr=dwr, wi=dwi,
                        sm=sm, b_in=dbin, g1=dg1, g2=dg2)
    return loss_blk[0, 0], dx, grads, dgf, sums, landed


ANY = pl.BlockSpec(memory_space=pl.ANY)


def _place():
    x, y, c = lax.axis_index("x"), lax.axis_index("y"), lax.axis_index("c")
    peers = [(1 - x, y, c), (x, 1 - y, c), (1 - x, 1 - y, c)]
    chips = [2 * (1 - x) + y, 2 * x + (1 - y), 2 * (1 - x) + (1 - y)]
    return (x, y, c), 2 * x + y, peers, chips


def _window(ref, dim, q, size):
    idx = [slice(None)] * len(ref.shape)
    idx[dim] = pl.ds(q * size, size)
    return ref.at[tuple(idx)]


def _gather_weights(shards, dims, small):
    n = len(shards)
    sizes = [s.shape[d] for s, d in zip(shards, dims)]
    full = [jax.ShapeDtypeStruct(s.shape[:d] + (s.shape[d] * N_CHIPS,) + s.shape[d + 1:], s.dtype)
            for s, d in zip(shards, dims)]
    full.append(jax.ShapeDtypeStruct((N_CHIPS,) + small.shape, small.dtype))

    def body(*refs):
        ins, outs = refs[:n + 1], refs[n + 1:2 * n + 2]
        send_sems, recv_sems, local_sems = refs[2 * n + 2:]
        _, k, peers, chips = _place()

        def dst(w, q):
            return outs[w].at[q] if w == n else _window(outs[w], dims[w], q, sizes[w])

        local = [pltpu.make_async_copy(ins[w], dst(w, k), local_sems.at[w]) for w in range(n + 1)]
        for cp in local:
            cp.start()
        sends = []
        for p, peer in enumerate(peers):
            for w in range(n + 1):
                s = p * (n + 1) + w
                sends.append(pltpu.make_async_remote_copy(
                    src_ref=ins[w], dst_ref=dst(w, k), send_sem=send_sems.at[s], recv_sem=recv_sems.at[s],
                    device_id=peer, device_id_type=MESH))
        for cp in sends:
            cp.start()
        for p, peer in enumerate(peers):
            for w in range(n + 1):
                s = p * (n + 1) + w
                pltpu.make_async_remote_copy(
                    src_ref=ins[w], dst_ref=dst(w, chips[p]), send_sem=send_sems.at[s], recv_sem=recv_sems.at[s],
                    device_id=peer, device_id_type=MESH).wait_recv()
        for cp in sends:
            cp.wait_send()
        for cp in local:
            cp.wait()

    return pl.pallas_call(
        body,
        in_specs=[ANY] * (n + 1),
        out_specs=[ANY] * (n + 1),
        out_shape=full,
        scratch_shapes=[pltpu.SemaphoreType.DMA((3 * (n + 1),)), pltpu.SemaphoreType.DMA((3 * (n + 1),)),
                        pltpu.SemaphoreType.DMA((n + 1,))],
        name="gather_weights",
    )(*shards, small)


def _scatter_grads(grads, dims):
    n, depth = len(grads), len(grads[0])
    sizes = [g[0].shape[d] // N_CHIPS for g, d in zip(grads, dims)]
    land = []
    for g, d, sz in zip(grads, dims, sizes):
        shp = g[0].shape
        land.append(jax.ShapeDtypeStruct((N_CHIPS, depth) + shp[:d] + (sz,) + shp[d + 1:], g[0].dtype))

    def body(*refs):
        ins, outs = refs[:n * depth], refs[n * depth:n * depth + n]
        send_sems, recv_sems, local_sems = refs[n * depth + n:]
        _, k, peers, chips = _place()

        def src(w, l, q):
            return _window(ins[w * depth + l], dims[w], q, sizes[w])

        local = [pltpu.make_async_copy(src(w, l, k), outs[w].at[3, l], local_sems.at[w * depth + l])
                 for w in range(n) for l in range(depth)]
        for cp in local:
            cp.start()
        sends = []
        for p, peer in enumerate(peers):
            for w in range(n):
                for l in range(depth):
                    s = (p * n + w) * depth + l
                    sends.append(pltpu.make_async_remote_copy(
                        src_ref=src(w, l, chips[p]), dst_ref=outs[w].at[p, l], send_sem=send_sems.at[s],
                        recv_sem=recv_sems.at[s], device_id=peer, device_id_type=MESH))
        for cp in sends:
            cp.start()
        for cp in sends:
            cp.wait_recv()
        for cp in sends:
            cp.wait_send()
        for cp in local:
            cp.wait()

    flat = [g for gl in grads for g in gl]
    return pl.pallas_call(
        body,
        in_specs=[ANY] * (n * depth),
        out_specs=[ANY] * n,
        out_shape=land,
        scratch_shapes=[pltpu.SemaphoreType.DMA((3 * n * depth,)), pltpu.SemaphoreType.DMA((3 * n * depth,)),
                        pltpu.SemaphoreType.DMA((n * depth,))],
        name="scatter_grads",
    )(*flat)


def _sum_slots(name, land):
    _, R, C = land.shape
    tr, tc = _div_tile(R, 512, 8), _div_tile(C, 1024)

    def body(a_ref, b_ref, c_ref, d_ref, o_ref):
        o_ref[...] = ((d_ref[...].astype(F32) + a_ref[...].astype(F32)) + b_ref[...].astype(F32)) \
            + c_ref[...].astype(F32)

    def slot(q):
        return pl.BlockSpec((None, tr, tc), lambda i, j: (q, i, j))

    return pl.pallas_call(
        body,
        grid=(R // tr, C // tc),
        in_specs=[slot(0), slot(1), slot(2), slot(3)],
        out_specs=pl.BlockSpec((tr, tc), lambda i, j: (i, j)),
        out_shape=jax.ShapeDtypeStruct((R, C), F32),
        compiler_params=_cparams(("parallel", "parallel")),
        name=name,
    )(land, land, land, land)


def _swap_with_sibling(parts):
    n = len(parts)

    def body(*refs):
        ins, outs = refs[:n], refs[n:2 * n]
        send_sems, recv_sems = refs[2 * n:]
        (x, y, c), _, _, _ = _place()
        copies = [pltpu.make_async_remote_copy(
            src_ref=ins[w], dst_ref=outs[w], send_sem=send_sems.at[w], recv_sem=recv_sems.at[w],
            device_id=(x, y, 1 - c), device_id_type=MESH) for w in range(n)]
        for cp in copies:
            cp.start()
        for cp in copies:
            cp.wait()

    return pl.pallas_call(
        body,
        in_specs=[ANY] * n,
        out_specs=[ANY] * n,
        out_shape=[jax.ShapeDtypeStruct(p.shape, p.dtype) for p in parts],
        scratch_shapes=[pltpu.SemaphoreType.DMA((n,)), pltpu.SemaphoreType.DMA((n,))],
        name="swap_with_sibling",
    )(*parts)


class _Carry:
    def __init__(self, ins, out_shapes, rounds, counts, aliases=None, marks=(0.6, 0.92)):
        self.ins, self.out_shapes, self.rounds, self.counts = list(ins), list(out_shapes), list(rounds), list(counts)
        self.aliases, self.marks = dict(aliases or {}), marks

    def scratch(self):
        return [pltpu.SemaphoreType.DMA((n,)) for n in self.counts for _ in range(2)]

    def _copies(self, r, in_refs, out_refs, sems, landing):
        remote = self.rounds[r](in_refs, out_refs)
        assert len(remote) == self.counts[r], (r, len(remote), self.counts[r])
        return [pltpu.make_async_remote_copy(src_ref=s, dst_ref=(land if landing else d), send_sem=sems[2 * r].at[i],
                                             recv_sem=sems[2 * r + 1].at[i], device_id=peer, device_id_type=MESH)
                for i, (s, d, peer, land) in enumerate(remote)]

    def begin(self, r, in_refs, out_refs, sems):
        if r > 0:
            for cp in self._copies(r - 1, in_refs, out_refs, sems, True):
                cp.wait_recv()
        for cp in self._copies(r, in_refs, out_refs, sems, False):
            cp.start()

    def end(self, in_refs, out_refs, sems):
        last = len(self.rounds) - 1
        for cp in self._copies(last, in_refs, out_refs, sems, True):
            cp.wait_recv()
        for r in range(last + 1):
            for cp in self._copies(r, in_refs, out_refs, sems, False):
                cp.wait_send()

    def ride(self, step, total, in_refs, out_refs, sems, first):
        if first:
            @pl.when(step == 0)
            def _():
                self.begin(0, in_refs, out_refs, sems)
            return
        for r in range(1, len(self.rounds)):
            @pl.when(step == min(total - 1, int(total * self.marks[r - 1])))
            def _(r=r):
                self.begin(r, in_refs, out_refs, sems)

        @pl.when(step == total - 1)
        def _():
            self.end(in_refs, out_refs, sems)


def _run_carry(name, carry):
    n_in, n_out = len(carry.ins), len(carry.out_shapes)

    def body(*refs):
        in_refs, out_refs, sems = refs[:n_in], refs[n_in:n_in + n_out], refs[n_in + n_out:]
        for r in range(len(carry.rounds)):
            carry.begin(r, in_refs, out_refs, sems)
        carry.end(in_refs, out_refs, sems)

    return pl.pallas_call(
        body,
        in_specs=[ANY] * n_in,
        out_specs=[ANY] * n_out,
        out_shape=carry.out_shapes,
        scratch_shapes=carry.scratch(),
        input_output_aliases=carry.aliases,
        name=name,
    )(*carry.ins)


def _comm_call(name, ins, out_shapes, plan, n_local, n_remote, aliases=None):
    assert n_local == 0
    return _run_carry(name, _Carry(ins, out_shapes, [lambda i, o: plan(i, o)[1]], [n_remote], aliases=aliases))


def _gather_carry(arrays, items):
    shapes = [a.shape for a in arrays]

    def ring():
        x, y, c = lax.axis_index("x"), lax.axis_index("y"), lax.axis_index("c")
        first = (x + (1 - c) * (1 - 2 * x), y + c * (1 - 2 * y), c)
        second = (x + c * (1 - 2 * x), y + (1 - c) * (1 - 2 * y), c)
        return c, 2 * x + y, first, second

    def chip(pos):
        return 2 * pos[0] + pos[1]

    def round0(ins, outs):
        c, k, first, second = ring()
        remote = []
        for item in items:
            win = window(outs[item[0]], item)
            remote.append((win(2 * k + c), win(2 * k + c), first, win(2 * chip(first) + c)))
            remote.append((win(2 * k + c), win(2 * k + c), second, win(2 * chip(second) + c)))
        return remote

    def round1(ins, outs):
        c, k, first, second = ring()
        remote = []
        for item in items:
            win = window(outs[item[0]], item)
            relayed = win(2 * chip(first) + c)
            remote.append((relayed, relayed, second, win(2 * (3 - k) + c)))
        return remote

    def window(ref, item):
        idx, d, part, nparts = item
        h = shapes[idx][d] // (2 * N_CHIPS)
        rows = shapes[idx][1 - d] // nparts

        def win(j):
            sl = [None, None]
            sl[d] = pl.ds(j * h, h)
            sl[1 - d] = pl.ds(part * rows, rows)
            return ref.at[tuple(sl)]

        return win

    def round2(ins, outs):
        (x, y, c), _, _, chips = _place()
        remote = []
        for item in items:
            win = window(outs[item[0]], item)
            for p in range(3):
                remote.append((win(2 * chips[p] + c), win(2 * chips[p] + c), (x, y, 1 - c),
                               win(2 * chips[p] + 1 - c)))
        return remote

    n = len(items)
    return _Carry(arrays, [jax.ShapeDtypeStruct(a.shape, a.dtype) for a in arrays], [round0, round1, round2],
                  [2 * n, n, 3 * n], aliases={i: i for i in range(len(arrays))})


def _place_shard(name, w, layer, dim, chip):
    _, a, b = w.shape
    full = (a * N_CHIPS, b) if dim == 0 else (a, b * N_CHIPS)
    tr, tc = _div_tile(a, 512, 16), _div_tile(b, 2048)
    nr, nc = a // tr, b // tc

    def out_map(i, j, chip_ref):
        return (chip_ref[0] * nr + i, j) if dim == 0 else (i, chip_ref[0] * nc + j)

    def body(chip_ref, w_ref, o_ref):
        o_ref[...] = w_ref[...].astype(o_ref.dtype)

    return pl.pallas_call(
        body,
        grid_spec=pltpu.PrefetchScalarGridSpec(
            num_scalar_prefetch=1, grid=(nr, nc),
            in_specs=[pl.BlockSpec((None, tr, tc), lambda i, j, chip_ref: (layer, i, j))],
            out_specs=pl.BlockSpec((tr, tc), out_map)),
        out_shape=jax.ShapeDtypeStruct(full, BF16),
        compiler_params=_cparams(("parallel", "parallel")),
        name=name,
    )(chip, w)


def _half_shape(shape, dim):
    return shape[:dim] + (shape[dim] // (2 * N_CHIPS),) + shape[dim + 1:]


def _swap_carry(grads, dims):
    shapes = [jax.ShapeDtypeStruct((N_CHIPS,) + _half_shape(g.shape, d), g.dtype) for g, d in zip(grads, dims)]

    def plan(ins, outs):
        (x, y, c), _, _, _ = _place()
        remote = []
        for w, d in enumerate(dims):
            h = grads[w].shape[d] // (2 * N_CHIPS)
            for q in range(N_CHIPS):
                remote.append((_window(ins[w], d, 2 * q + 1 - c, h), outs[w].at[q], (x, y, 1 - c), outs[w].at[q]))
        return remote

    return _Carry(grads, shapes, [plan], [N_CHIPS * len(grads)])


def _scatter_carry(sums):
    def plan(ins, outs):
        _, _, peers, chips = _place()
        remote = []
        for w in range(len(sums)):
            for p in range(3):
                remote.append((ins[w].at[chips[p]], outs[w].at[p], peers[p], outs[w].at[p]))
        return remote

    return _Carry(sums, [jax.ShapeDtypeStruct((3,) + s.shape[1:], s.dtype) for s in sums], [plan], [3 * len(sums)])


def _merge_carries(carries):
    carries = [c for c in carries if c is not None]
    if len(carries) <= 1:
        return carries[0] if carries else None
    ins = [a for c in carries for a in c.ins]
    outs = [s for c in carries for s in c.out_shapes]

    def plan(in_refs, out_refs):
        remote, i0, o0 = [], 0, 0
        for c in carries:
            remote += c.rounds[0](in_refs[i0:i0 + len(c.ins)], out_refs[o0:o0 + len(c.out_shapes)])
            i0, o0 = i0 + len(c.ins), o0 + len(c.out_shapes)
        return remote

    assert all(len(c.rounds) == 1 and not c.aliases for c in carries)
    return _Carry(ins, outs, [plan], [sum(c.counts[0] for c in carries)])


def _add_halves(name, g, got, dim, core):
    R, C = g.shape
    if dim == 1:
        r, cc = R, C // (2 * N_CHIPS)
    else:
        r, cc = R // (2 * N_CHIPS), C
    tr, tc = _div_tile(r, 512, 16), _div_tile(cc, 1024)
    nr, nc = r // tr, cc // tc

    def g_map(q, i, j, core_ref):
        w = 2 * q + core_ref[0]
        return (i, w * nc + j) if dim == 1 else (w * nr + i, j)

    def body(core_ref, g_ref, got_ref, o_ref):
        o_ref[...] = (g_ref[...].astype(F32) + got_ref[...].astype(F32)).astype(o_ref.dtype)

    slab = pl.BlockSpec((None, tr, tc), lambda q, i, j, core_ref: (q, i, j))
    return pl.pallas_call(
        body,
        grid_spec=pltpu.PrefetchScalarGridSpec(
            num_scalar_prefetch=1, grid=(N_CHIPS, nr, nc),
            in_specs=[pl.BlockSpec((tr, tc), g_map), slab], out_specs=slab),
        out_shape=jax.ShapeDtypeStruct((N_CHIPS, r, cc), g.dtype),
        compiler_params=_cparams(("parallel", "parallel", "parallel")),
        name=name,
    )(core, g, got)


def _reduce_into(name, sums, land, acc, layer, dim, shape, where):
    _, r, cc = sums.shape
    tr, tc = _div_tile(r, 512, 16), _div_tile(cc, 1024)
    nr, nc = r // tr, cc // tc

    def out_map(i, j, s):
        return (layer, s[1] * nr + i, j) if dim == 1 else (layer, i, s[1] * nc + j)

    def body(*refs):
        own, a_ref, b_ref, c_ref, o_ref = refs[1], refs[2], refs[3], refs[4], refs[-1]
        o_ref[...] = ((own[...].astype(F32) + a_ref[...].astype(F32)) + b_ref[...].astype(F32)) \
            + c_ref[...].astype(F32)

    def slot(p):
        return pl.BlockSpec((None, tr, tc), lambda i, j, s: (p, i, j))

    in_specs = [pl.BlockSpec((None, tr, tc), lambda i, j, s: (s[0], i, j)), slot(0), slot(1), slot(2)]
    args = [where, sums, land, land, land]
    if acc is not None:
        in_specs.append(ANY)
        args.append(acc)
    return pl.pallas_call(
        body,
        grid_spec=pltpu.PrefetchScalarGridSpec(
            num_scalar_prefetch=1, grid=(nr, nc), in_specs=in_specs,
            out_specs=pl.BlockSpec((None, tr, tc), out_map)),
        out_shape=jax.ShapeDtypeStruct(shape, F32),
        input_output_aliases={5: 0} if acc is not None else {},
        compiler_params=_cparams(("parallel", "parallel")),
        name=name,
    )(*args)


def _join_halves(name, grads, dims):
    n = len(grads)

    def plan(ins, outs):
        (x, y, c), _, _, _ = _place()
        remote = []
        for w in range(n):
            d, h = dims[w], grads[w].shape[dims[w]] // 2
            mine = _window(outs[w], d, c, h)
            remote.append((mine, mine, (x, y, 1 - c), _window(outs[w], d, 1 - c, h)))
        return [], remote

    return _comm_call(name, grads, [jax.ShapeDtypeStruct(g.shape, g.dtype) for g in grads], plan, 0, n,
                      aliases={w: w for w in range(n)})


def _allreduce_small(pack):
    R, C = pack.shape

    def gather_body(in_ref, slots_ref, send_sems, recv_sems, local_sem):
        x, y, c = lax.axis_index("x"), lax.axis_index("y"), lax.axis_index("c")
        me = 4 * x + 2 * y + c
        flips = [(dx, dy, dc) for dx in (0, 1) for dy in (0, 1) for dc in (0, 1)][1:]

        def flip(v, d):
            return 1 - v if d else v

        local = pltpu.make_async_copy(in_ref, slots_ref.at[me], local_sem)
        local.start()
        sends = []
        for j, (dx, dy, dc) in enumerate(flips):
            px, py, pc = flip(x, dx), flip(y, dy), flip(c, dc)
            sends.append((pltpu.make_async_remote_copy(
                src_ref=in_ref, dst_ref=slots_ref.at[me], send_sem=send_sems.at[j], recv_sem=recv_sems.at[j],
                device_id=(px, py, pc), device_id_type=MESH), 4 * px + 2 * py + pc, j))
        for cp, _, _ in sends:
            cp.start()
        for cp, peer_id, j in sends:
            pltpu.make_async_remote_copy(
                src_ref=in_ref, dst_ref=slots_ref.at[peer_id], send_sem=send_sems.at[j], recv_sem=recv_sems.at[j],
                device_id=(x, y, c), device_id_type=MESH).wait_recv()
        for cp, _, _ in sends:
            cp.wait_send()
        local.wait()

    slots = pl.pallas_call(
        gather_body,
        in_specs=[ANY],
        out_specs=ANY,
        out_shape=jax.ShapeDtypeStruct((N_DEV, R, C), pack.dtype),
        scratch_shapes=[pltpu.SemaphoreType.DMA((N_DEV - 1,)), pltpu.SemaphoreType.DMA((N_DEV - 1,)),
                        pltpu.SemaphoreType.DMA],
        name="allgather_small",
    )(pack)

    def sum_body(s_ref, o_ref):
        acc = s_ref[0]
        for d in range(1, N_DEV):
            acc = acc + s_ref[d]
        o_ref[...] = acc

    return pl.pallas_call(
        sum_body,
        out_shape=jax.ShapeDtypeStruct((R, C), pack.dtype),
        name="sum_small",
    )(slots)


def _adamw_math(w, g, m, v):
    m2 = ADAM_B1 * m + (1.0 - ADAM_B1) * g
    v2 = ADAM_B2 * v + (1.0 - ADAM_B2) * (g * g)
    m_hat = m2 / (1.0 - ADAM_B1 ** ADAM_STEP)
    v_hat = v2 / (1.0 - ADAM_B2 ** ADAM_STEP)
    delta = -ADAM_LR * (m_hat / (jnp.sqrt(v_hat) + ADAM_EPS) + ADAM_WD * w)
    return delta, m2, v2


def _adamw_big(name, w, m, v, g_parts):
    shape = w.shape
    C = shape[-1]
    R = w.size // C
    tr, tc = _div_tile(R, 256, 8), _div_tile(C, 1024)
    n_g = len(g_parts)

    def body(*refs):
        w_ref, m_ref, v_ref = refs[:3]
        g_ref, d_ref, nm_ref, nv_ref = refs[3 + n_g:]
        g = refs[3][...]
        for extra in refs[4:3 + n_g]:
            g = g + extra[...]
        delta, m2, v2 = _adamw_math(w_ref[...], g, m_ref[...], v_ref[...])
        g_ref[...], d_ref[...], nm_ref[...], nv_ref[...] = g, delta, m2, v2

    blk = pl.BlockSpec((tr, tc), lambda i, j: (i, j))
    outs = pl.pallas_call(
        body,
        grid=(R // tr, C // tc),
        in_specs=[blk] * (3 + n_g),
        out_specs=[blk] * 4,
        out_shape=[jax.ShapeDtypeStruct((R, C), F32)] * 4,
        compiler_params=_cparams(("parallel", "parallel")),
        name=name,
    )(w.reshape(R, C), m.reshape(R, C), v.reshape(R, C), *[g.reshape(R, C) for g in g_parts])
    return [o.reshape(shape) for o in outs]


def _adamw_small(name, w, g, m, v):
    shape = w.shape
    two_d = (w.size // shape[-1], shape[-1])

    def body(w_ref, g_ref, m_ref, v_ref, d_ref, nm_ref, nv_ref):
        d_ref[...], nm_ref[...], nv_ref[...] = _adamw_math(w_ref[...], g_ref[...], m_ref[...], v_ref[...])

    outs = pl.pallas_call(
        body,
        out_shape=[jax.ShapeDtypeStruct(two_d, F32)] * 3,
        name=name,
    )(w.reshape(two_d), g.reshape(two_d), m.reshape(two_d), v.reshape(two_d))
    return [o.reshape(shape) for o in outs]


SMALL_ROWS = 40
S_BIN, S_G1, S_G2 = 16, 24, 32
MATS = ("w_in", "w_pa", "w_pb", "w_o", "w_mlp1", "w_mlp2")
LRU = ("lru_wr", "lru_wi")
BIG_DIM = dict(w_in=2, w_pa=1, w_pb=1, w_o=1, w_mlp1=2, w_mlp2=1, lru_wr=2, lru_wi=2)
WEIGHTS = ("norm1_g", "w_in", "b_in", "conv_a_w", "conv_a_b", "lru_wr", "lru_br", "lru_wi", "lru_bi", "lru_lam",
           "conv_b_w", "w_pa", "w_pb", "w_o", "norm2_g", "w_mlp1", "w_mlp2", "final_g")


def _rows_at(a, r0, total):
    pad = [(0, 0)] * a.ndim
    pad[-2] = (r0, total - r0 - a.shape[-2])
    return jnp.pad(a, pad)


def kernel(x, norm1_g, w_in, b_in, conv_a_w, conv_a_b, lru_wr, lru_br, lru_wi, lru_bi, lru_lam, conv_b_w, w_pa, w_pb, w_o, norm2_g, w_mlp1, w_mlp2, final_g, loss_target, m_norm1_g, m_w_in, m_b_in, m_conv_a_w, m_conv_a_b, m_lru_wr, m_lru_br, m_lru_wi, m_lru_bi, m_lru_lam, m_conv_b_w, m_w_pa, m_w_pb, m_w_o, m_norm2_g, m_w_mlp1, m_w_mlp2, m_final_g, v_norm1_g, v_w_in, v_b_in, v_conv_a_w, v_conv_a_b, v_lru_wr, v_lru_br, v_lru_wi, v_lru_bi, v_lru_lam, v_conv_b_w, v_w_pa, v_w_pb, v_w_o, v_norm2_g, v_w_mlp1, v_w_mlp2, v_final_g):
    wts = dict(norm1_g=norm1_g, w_in=w_in, b_in=b_in, conv_a_w=conv_a_w, conv_a_b=conv_a_b, lru_wr=lru_wr,
               lru_br=lru_br, lru_wi=lru_wi, lru_bi=lru_bi, lru_lam=lru_lam, conv_b_w=conv_b_w, w_pa=w_pa,
               w_pb=w_pb, w_o=w_o, norm2_g=norm2_g, w_mlp1=w_mlp1, w_mlp2=w_mlp2, final_g=final_g)
    mom = dict(norm1_g=m_norm1_g, w_in=m_w_in, b_in=m_b_in, conv_a_w=m_conv_a_w, conv_a_b=m_conv_a_b,
               lru_wr=m_lru_wr, lru_br=m_lru_br, lru_wi=m_lru_wi, lru_bi=m_lru_bi, lru_lam=m_lru_lam,
               conv_b_w=m_conv_b_w, w_pa=m_w_pa, w_pb=m_w_pb, w_o=m_w_o, norm2_g=m_norm2_g, w_mlp1=m_w_mlp1,
               w_mlp2=m_w_mlp2, final_g=m_final_g)
    vel = dict(norm1_g=v_norm1_g, w_in=v_w_in, b_in=v_b_in, conv_a_w=v_conv_a_w, conv_a_b=v_conv_a_b,
               lru_wr=v_lru_wr, lru_br=v_lru_br, lru_wi=v_lru_wi, lru_bi=v_lru_bi, lru_lam=v_lru_lam,
               conv_b_w=v_conv_b_w, w_pa=v_w_pa, w_pb=v_w_pb, w_o=v_w_o, norm2_g=v_norm2_g, w_mlp1=v_w_mlp1,
               w_mlp2=v_w_mlp2, final_g=v_final_g)
    depth, D = norm1_g.shape
    nb, bw = lru_wr.shape[1], lru_wr.shape[3]
    chip = 2 * lax.axis_index("x") + lax.axis_index("y")

    small_parts = [conv_a_w.reshape(-1), conv_b_w.reshape(-1), lru_br.reshape(-1), lru_bi.reshape(-1)]
    small_len = sum(p.shape[0] for p in small_parts)
    small_rows = -(-small_len // 1024) * 8
    small = jnp.concatenate(small_parts + [jnp.zeros((small_rows * 128 - small_len,), F32)]).reshape(small_rows, 128)
    gathered = _gather_weights([wts[n].astype(BF16) for n in LRU], [BIG_DIM[n] for n in LRU], small)
    full = dict(zip(LRU, gathered[:-1]))
    items = [(n, l) for l in range(depth) for n in MATS]
    mat_dims = [BIG_DIM[n] - 1 for n, _ in items]
    where = jnp.stack([chip, lax.axis_index("c")]).astype(jnp.int32)
    placed = {(n, l): _place_shard(f"place_{n}_{l}", wts[n], l, BIG_DIM[n] - 1, where) for n, l in items}
    flat = gathered[-1].reshape(N_CHIPS, small_rows * 128)
    off = 0
    small_full = []
    for part, shard in zip(small_parts, (conv_a_w, conv_b_w, lru_br, lru_bi)):
        piece = flat[:, off:off + part.shape[0]].reshape((N_CHIPS,) + shard.shape)
        small_full.append(jnp.moveaxis(piece, 0, -2).reshape(shard.shape[:-1] + (N_CHIPS * shard.shape[-1],)))
        off += part.shape[0]
    caw_f, cbw_f, br_f, bi_f = small_full
    pch = (_rows_at(conv_a_b[:, None, :], R_CAB, R_ROWS) + _rows_at(br_f.reshape(depth, 1, D), R_BR, R_ROWS)
           + _rows_at(bi_f.reshape(depth, 1, D), R_BI, R_ROWS) + _rows_at(lru_lam[:, None, :], R_LAM, R_ROWS)
           + _rows_at(caw_f, R_CAW, R_ROWS) + _rows_at(cbw_f, R_CBW, R_ROWS))
    W = dict(b_in=b_in, pch=pch, wr=full["lru_wr"], wi=full["lru_wi"], g1=norm1_g, g2=norm2_g, gf=final_g,
             core=lax.axis_index("c").astype(jnp.int32).reshape(1))

    loss_local, dx, grads, dgf, sums, landed = _local_fwd_bwd(x[0], loss_target[0], W, placed)
    loss = lax.psum(loss_local, ("x", "y", "c"))

    key = dict(w_in="w_in", w_pa="w_pa", w_pb="w_pb", w_o="w_o", w_mlp1="w1", w_mlp2="w2", lru_wr="wr", lru_wi="wi")
    out_g, out_d, out_m, out_v = {}, {}, {}, {}
    per_layer = [[grads[l][key[n]].astype(BF16) for l in range(depth)] for n in LRU]
    land = _scatter_grads(per_layer, [BIG_DIM[n] - 1 for n in LRU])
    chip_sums = [_sum_slots(f"sum_slots_{n}", ld.reshape(N_CHIPS, -1, ld.shape[-1])) for n, ld in zip(LRU, land)]
    sib_sums = _swap_with_sibling(chip_sums)
    for n, mine, sib in zip(LRU, chip_sums, sib_sums):
        out_g[n], out_d[n], out_m[n], out_v[n] = _adamw_big(f"adamw_{n}", wts[n], mom[n], vel[n], [mine, sib])
    acc = {n: None for n in MATS}
    for n, l in reversed(items):
        acc[n] = _reduce_into(f"reduce_{n}_{l}", sums[n, l], landed[n, l], acc[n], l, BIG_DIM[n], wts[n].shape, where)
    joined = _join_halves("join_halves", [acc[n] for n in MATS], [BIG_DIM[n] for n in MATS])
    for n, g in zip(MATS, joined):
        out_g[n], out_d[n], out_m[n], out_v[n] = _adamw_big(f"adamw_{n}", wts[n], mom[n], vel[n], [g])

    rows = []
    for l in range(depth):
        g = grads[l]
        rows.append(_rows_at(g["sm"], 0, SMALL_ROWS) + _rows_at(g["b_in"].reshape(7, D), S_BIN, SMALL_ROWS)
                    + _rows_at(g["g1"], S_G1, SMALL_ROWS) + _rows_at(g["g2"], S_G2, SMALL_ROWS))
    rows.append(_rows_at(dgf, 0, 8))
    tot = _allreduce_small(jnp.concatenate(rows, axis=0))
    per = tot[:depth * SMALL_ROWS].reshape(depth, SMALL_ROWS, D)

    def cols_of_chip(a, axis):
        size = a.shape[axis] // N_CHIPS
        return lax.dynamic_slice_in_dim(a, chip * size, size, axis=axis)

    small_g = dict(
        norm1_g=per[:, S_G1], b_in=per[:, S_BIN:S_BIN + 7].reshape(depth, 7 * D),
        conv_a_w=cols_of_chip(per[:, R_CAW:R_CAW + 4], 2), conv_a_b=per[:, R_CAB],
        lru_br=cols_of_chip(per[:, R_BR].reshape(depth, nb, bw), 2),
        lru_bi=cols_of_chip(per[:, R_BI].reshape(depth, nb, bw), 2), lru_lam=per[:, R_LAM],
        conv_b_w=cols_of_chip(per[:, R_CBW:R_CBW + 3], 2), norm2_g=per[:, S_G2],
        final_g=tot[depth * SMALL_ROWS])
    for n, g in small_g.items():
        out_g[n] = g
        out_d[n], out_m[n], out_v[n] = _adamw_small(f"adamw_{n}", wts[n], g, mom[n], vel[n])

    return (loss, dx[None], *[out_g[n] for n in WEIGHTS], *[out_d[n] for n in WEIGHTS],
            *[out_m[n] for n in WEIGHTS], *[out_v[n] for n in WEIGHTS])
```

```python
import functools

import jax
import jax.numpy as jnp
from jax import lax
from jax.experimental import pallas as pl
from jax.experimental.pallas import tpu as pltpu

F32 = jnp.float32
BF16 = jnp.bfloat16
MESH = pl.DeviceIdType.MESH

EPS = 1e-6
LRU_C = 8.0
ADAM_LR = 0.001
ADAM_B1 = 0.9
ADAM_B2 = 0.999
ADAM_EPS = 1e-08
ADAM_WD = 0.01
ADAM_STEP = 10

N_CHIPS = 4
N_DEV = 8
HALO = 8
VMEM_LIMIT = 56 * 1024 * 1024
MM_TILES = (1024, 1024, 2048)
MM_TILES_FUSED = (512, 1024, 2048)
SEQ_CHUNK = 256
MIXER_ROW_BLOCK = 32
ROW_TILE = 256

R_CAB, R_BR, R_BI, R_LAM, R_CAW, R_CBW, R_ROWS = 0, 1, 2, 3, 4, 8, 16


def _cparams(sem):
    return pltpu.CompilerParams(dimension_semantics=sem, vmem_limit_bytes=VMEM_LIMIT)


def _div_tile(n, pref, unit=128):
    if n <= pref:
        return n
    t = (pref // unit) * unit
    while n % t:
        t -= unit
    return t


def _sigmoid(v):
    return 1.0 / (1.0 + jnp.exp(-v))


def _gelu_and_grad(y):
    k = 0.7978845608028654
    c = 0.044715
    y2 = y * y
    t = jnp.tanh(k * (y + c * y2 * y))
    g = 0.5 * y * (1.0 + t)
    gp = 0.5 * (1.0 + t) + 0.5 * y * (1.0 - t * t) * (k * (1.0 + 3.0 * c * y2))
    return g, gp


def _softplus_neg(lam):
    e = jnp.exp(-jnp.abs(lam))
    w = 1.0 + e
    l1p = jnp.where(w == 1.0, e, jnp.log(w) * e / jnp.where(w == 1.0, 1.0, w - 1.0))
    return jnp.maximum(-lam, 0.0) + l1p


def _mm(name, mode, a, b, M, N, K, out_dtypes, epilogue=None, extras=(), la=None, lb=None, tiles=None,
        carry=None):
    tiles = MM_TILES if tiles is None else tiles
    tm, tn, tk = _div_tile(M, tiles[0]), _div_tile(N, tiles[1]), _div_tile(K, tiles[2])
    assert M % tm == 0 and N % tn == 0 and K % tk == 0, (name, M, N, K)
    nk = K // tk

    def spec(lead, shape, imap):
        if lead is None:
            return pl.BlockSpec(shape, imap)
        return pl.BlockSpec((None,) + shape, lambda i, j, k: (lead,) + imap(i, j, k))

    if mode == "nn":
        a_spec = spec(la, (tm, tk), lambda i, j, k: (i, k))
        b_spec = spec(lb, (tk, tn), lambda i, j, k: (k, j))
        dn = (((1,), (0,)), ((), ()))
    elif mode == "nt":
        a_spec = spec(la, (tm, tk), lambda i, j, k: (i, k))
        b_spec = spec(lb, (tn, tk), lambda i, j, k: (j, k))
        dn = (((1,), (1,)), ((), ()))
    else:
        a_spec = spec(la, (tk, tm), lambda i, j, k: (k, i))
        b_spec = spec(lb, (tk, tn), lambda i, j, k: (k, j))
        dn = (((0,), (0,)), ((), ()))

    ex_arrays, ex_specs = [], []
    for arr, kind, off in extras:
        ex_arrays.append(arr)
        if kind == "bias":
            ex_specs.append(pl.BlockSpec((1, tn), lambda i, j, k: (0, j)))
        else:
            assert off % tn == 0
            ex_specs.append(pl.BlockSpec((tm, tn), lambda i, j, k, o=off // tn: (i, j + o)))
    n_ex, n_out = len(ex_arrays), len(out_dtypes)
    n_cin = len(carry.ins) if carry else 0
    n_cout = len(carry.out_shapes) if carry else 0
    n_in = 2 + n_ex + n_cin
    gi, gj = M // tm, N // tn

    def body(*refs):
        a_ref, b_ref = refs[0], refs[1]
        ex = refs[2:2 + n_ex]
        outs = refs[n_in:n_in + n_out]
        acc = refs[n_in + n_out + n_cout]
        i, j, k = pl.program_id(0), pl.program_id(1), pl.program_id(2)
        if carry:
            c_in, c_out = refs[2 + n_ex:n_in], refs[n_in + n_out:n_in + n_out + n_cout]
            sems = refs[n_in + n_out + n_cout + 1:]
            step = (i * gj + j) * nk + k
            carry.ride(step, gi * gj * nk, c_in, c_out, sems, True)

        def product():
            return lax.dot_general(a_ref[...], b_ref[...], dn, preferred_element_type=F32)

        def finish(r):
            vals = (r,) if epilogue is None else epilogue(r, *[e[...] for e in ex])
            for o, v in zip(outs, vals):
                o[...] = v.astype(o.dtype)

        if nk == 1:
            finish(product())
        else:
            @pl.when(k == 0)
            def _():
                acc[...] = product()

            @pl.when((k > 0) & (k < nk - 1))
            def _():
                acc[...] += product()

            @pl.when(k == nk - 1)
            def _():
                finish(acc[...] + product())

        if carry:
            carry.ride(step, gi * gj * nk, c_in, c_out, sems, False)

    res = pl.pallas_call(
        body,
        grid=(gi, gj, nk),
        in_specs=[a_spec, b_spec, *ex_specs] + [ANY] * n_cin,
        out_specs=[pl.BlockSpec((tm, tn), lambda i, j, k: (i, j)) for _ in range(n_out)] + [ANY] * n_cout,
        out_shape=[jax.ShapeDtypeStruct((M, N), d) for d in out_dtypes] + (carry.out_shapes if carry else []),
        scratch_shapes=[pltpu.VMEM((tm, tn) if nk > 1 else (8, 128), F32)] + (carry.scratch() if carry else []),
        input_output_aliases={2 + n_ex + ci: n_out + co for ci, co in carry.aliases.items()} if carry else {},
        compiler_params=_cparams(("arbitrary",) * 3 if carry else ("parallel", "parallel", "arbitrary")),
        name=name,
    )(a, b, *ex_arrays, *(carry.ins if carry else []))
    main = res[0] if n_out == 1 else res[:n_out]
    return (main, res[n_out:]) if carry else main


def _rms_fwd(name, x, g_row):
    T, D = x.shape
    tm = min(ROW_TILE, T)

    def body(x_ref, g_ref, h_ref):
        xv = x_ref[...]
        r = lax.rsqrt(jnp.mean(xv * xv, axis=-1, keepdims=True) + EPS)
        h_ref[...] = (xv * r * g_ref[...]).astype(BF16)

    return pl.pallas_call(
        body,
        grid=(T // tm,),
        in_specs=[pl.BlockSpec((tm, D), lambda i: (i, 0)), pl.BlockSpec((1, D), lambda i: (0, 0))],
        out_specs=pl.BlockSpec((tm, D), lambda i: (i, 0)),
        out_shape=jax.ShapeDtypeStruct((T, D), BF16),
        compiler_params=_cparams(("parallel",)),
        name=name,
    )(x, g_row)


def _rms_bwd(name, x, g_row, dh, dres):
    T, D = x.shape
    tm = min(ROW_TILE, T)

    def body(x_ref, g_ref, dh_ref, dres_ref, dx_ref, dxb_ref, dg_ref):
        xv, dhv = x_ref[...], dh_ref[...]
        r = lax.rsqrt(jnp.mean(xv * xv, axis=-1, keepdims=True) + EPS)
        gd = g_ref[...] * dhv
        c = jnp.mean(xv * gd, axis=-1, keepdims=True)
        dx = r * gd - xv * (r * r * r) * c + dres_ref[...]
        dx_ref[...] = dx
        dxb_ref[...] = dx.astype(BF16)

        @pl.when(pl.program_id(0) == 0)
        def _():
            dg_ref[...] = jnp.zeros_like(dg_ref)

        dg_ref[...] += jnp.sum(dhv * xv * r, axis=0, keepdims=True)

    row = pl.BlockSpec((tm, D), lambda i: (i, 0))
    vec = pl.BlockSpec((1, D), lambda i: (0, 0))
    return pl.pallas_call(
        body,
        grid=(T // tm,),
        in_specs=[row, vec, row, row],
        out_specs=[row, row, vec],
        out_shape=[jax.ShapeDtypeStruct((T, D), F32), jax.ShapeDtypeStruct((T, D), BF16),
                   jax.ShapeDtypeStruct((1, D), F32)],
        compiler_params=_cparams(("arbitrary",)),
        name=name,
    )(x, g_row, dh, dres)


def _loss_head(name, x, g_row, tgt):
    T, D = x.shape
    tm = min(ROW_TILE, T)

    def body(x_ref, g_ref, t_ref, dx_ref, dxb_ref, dg_ref, loss_ref):
        xv, g = x_ref[...], g_ref[...]
        r = lax.rsqrt(jnp.mean(xv * xv, axis=-1, keepdims=True) + EPS)
        xh = xv * r
        e = xh * g - t_ref[...]
        lpart = 0.5 * jnp.sum(jnp.mean(e * e, axis=-1, keepdims=True))
        dy = e * (1.0 / D)
        gd = g * dy
        c = jnp.mean(xv * gd, axis=-1, keepdims=True)
        dx = r * gd - xv * (r * r * r) * c
        dx_ref[...] = dx
        dxb_ref[...] = dx.astype(BF16)

        @pl.when(pl.program_id(0) == 0)
        def _():
            dg_ref[...] = jnp.zeros_like(dg_ref)
            loss_ref[...] = jnp.zeros_like(loss_ref)

        dg_ref[...] += jnp.sum(dy * xh, axis=0, keepdims=True)
        loss_ref[...] += jnp.full(loss_ref.shape, lpart, F32)

    row = pl.BlockSpec((tm, D), lambda i: (i, 0))
    vec = pl.BlockSpec((1, D), lambda i: (0, 0))
    return pl.pallas_call(
        body,
        grid=(T // tm,),
        in_specs=[row, vec, row],
        out_specs=[row, row, vec, pl.BlockSpec((8, 128), lambda i: (0, 0))],
        out_shape=[jax.ShapeDtypeStruct((T, D), F32), jax.ShapeDtypeStruct((T, D), BF16),
                   jax.ShapeDtypeStruct((1, D), F32), jax.ShapeDtypeStruct((8, 128), F32)],
        compiler_params=_cparams(("arbitrary",)),
        name=name,
    )(x, g_row, tgt)


def _colsum(name, a, carry=None):
    T, N = a.shape
    tm, tn = min(512, T), _div_tile(N, 2048)
    gj, gi = N // tn, T // tm
    n_cin = len(carry.ins) if carry else 0
    n_cout = len(carry.out_shapes) if carry else 0

    def body(*refs):
        a_ref, o_ref = refs[0], refs[1 + n_cin]
        j, i = pl.program_id(0), pl.program_id(1)
        if carry:
            c_in, c_out, sems = refs[1:1 + n_cin], refs[2 + n_cin:2 + n_cin + n_cout], refs[2 + n_cin + n_cout:]
            carry.ride(j * gi + i, gj * gi, c_in, c_out, sems, True)

        @pl.when(i == 0)
        def _():
            o_ref[...] = jnp.zeros_like(o_ref)

        o_ref[...] += jnp.sum(a_ref[...].astype(F32), axis=0, keepdims=True)

        if carry:
            carry.ride(j * gi + i, gj * gi, c_in, c_out, sems, False)

    res = pl.pallas_call(
        body,
        grid=(gj, gi),
        in_specs=[pl.BlockSpec((tm, tn), lambda j, i: (i, j))] + [ANY] * n_cin,
        out_specs=[pl.BlockSpec((1, tn), lambda j, i: (0, j))] + [ANY] * n_cout,
        out_shape=[jax.ShapeDtypeStruct((1, N), F32)] + (carry.out_shapes if carry else []),
        scratch_shapes=carry.scratch() if carry else [],
        compiler_params=_cparams(("arbitrary", "arbitrary") if carry else ("parallel", "arbitrary")),
        name=name,
    )(a, *(carry.ins if carry else []))
    return (res[0], res[1:]) if carry else res[0]


def _tile_scan(a, b, row, reverse):
    for s in (1, 2, 4):
        if reverse:
            a_s, b_s, m = pltpu.roll(a, 8 - s, 0), pltpu.roll(b, 8 - s, 0), row < 8 - s
        else:
            a_s, b_s, m = pltpu.roll(a, s, 0), pltpu.roll(b, s, 0), row >= s
        b = jnp.where(m, a * b_s + b, b)
        a = jnp.where(m, a * a_s, a)
    return a, b


def _chunk_scan(a_s, b_s, out_ref, carry, n_tiles, width, reverse):
    row = lax.broadcasted_iota(jnp.int32, (8, width), 0)
    edge = 0 if reverse else 7

    group = 4 if n_tiles % 4 == 0 else 1

    def step(j, c):
        jj = (n_tiles // group - 1 - j) if reverse else j
        base = pl.multiple_of(jj * (8 * group), 8 * group)
        order = range(group - 1, -1, -1) if reverse else range(group)
        parts = {t: _tile_scan(a_s[pl.ds(base + 8 * t, 8), :], b_s[pl.ds(base + 8 * t, 8), :], row, reverse)
                 for t in order}
        for t in order:
            h = parts[t][0] * c + parts[t][1]
            out_ref[pl.ds(base + 8 * t, 8), :] = h
            c = jnp.broadcast_to(h[edge:edge + 1, :], (8, width))
        return c

    carry[...] = lax.fori_loop(0, n_tiles // group, step, carry[...])


def _mixer_specs(Tc, bw, nb, layer):
    def seg(s):
        return pl.BlockSpec((Tc, bw), lambda n, i: (i, s * nb + n))

    p_spec = pl.BlockSpec((None, R_ROWS, bw), lambda n, i: (layer, 0, n))
    w_spec = pl.BlockSpec((None, None, bw, bw), lambda n, i: (layer, n, 0, 0))
    return seg, p_spec, w_spec


def _mixer_fwd(name, layer, z, pch, wr, wi, comm=None):
    T, D = z.shape[0], z.shape[1] // 7
    bw, nb = wr.shape[-1], wr.shape[1]
    Tc = min(SEQ_CHUNK, T)
    nT = T // Tc
    n_cin = len(comm.ins) if comm else 0
    n_cout = len(comm.out_shapes) if comm else 0

    def body(*refs):
        xa_ref, ya_ref, cb_ref, cc_ref, cx_ref, p_ref, wr_ref, wi_ref = refs[:8]
        pa_ref, pb_ref, xc_ref, hl_ref, vb_ref = refs[8 + n_cin:13 + n_cin]
        xa_buf, u_buf, a_s, b_s, carry = refs[13 + n_cin + n_cout:18 + n_cin + n_cout]
        if comm:
            c_in, c_out, sems = refs[8:8 + n_cin], refs[13 + n_cin:13 + n_cin + n_cout], refs[18 + n_cin + n_cout:]
            step = pl.program_id(0) * nT + pl.program_id(1)
            comm.ride(step, nb * nT, c_in, c_out, sems, True)

        @pl.when(pl.program_id(1) == 0)
        def _():
            xa_buf[0:HALO, :] = jnp.zeros((HALO, bw), F32)
            u_buf[0:HALO, :] = jnp.zeros((HALO, bw), F32)
            carry[...] = jnp.zeros_like(carry)

        rb = min(MIXER_ROW_BLOCK, Tc)
        blocks = [(g * rb, slice(g * rb, (g + 1) * rb)) for g in range(Tc // rb)]

        def prow(k):
            return p_ref[k:k + 1, :]

        xa_buf[HALO:HALO + Tc, :] = xa_ref[...]
        u_buf[HALO:HALO + Tc, :] = cc_ref[...] * cx_ref[...]
        for r0, rs in blocks:
            xc = prow(R_CAB)
            for k in range(4):
                xc = xc + prow(R_CAW + k) * xa_buf[HALO - 3 + k + r0:HALO - 3 + k + r0 + rb, :]
            xc_ref[rs, :] = xc
        xcb = xc_ref[...].astype(BF16)
        a_s[...] = jnp.dot(xcb, wr_ref[...], preferred_element_type=F32)
        b_s[...] = jnp.dot(xcb, wi_ref[...], preferred_element_type=F32)
        sp = _softplus_neg(prow(R_LAM))
        for _, rs in blocks:
            r = _sigmoid(a_s[rs, :] + prow(R_BR))
            ig = _sigmoid(b_s[rs, :] + prow(R_BI))
            log_a = (-LRU_C) * r * sp
            t = jnp.tanh(log_a)
            a_s[rs, :] = jnp.exp(log_a)
            b_s[rs, :] = jnp.sqrt(-2.0 * t / (1.0 - t)) * (ig * xc_ref[rs, :])
        _chunk_scan(a_s, b_s, hl_ref, carry, Tc // 8, bw, False)
        for r0, rs in blocks:
            g, _ = _gelu_and_grad(ya_ref[rs, :])
            pa_ref[rs, :] = (hl_ref[rs, :] * g).astype(BF16)
            vb = jnp.zeros((rb, bw), F32)
            for k in range(3):
                vb = vb + prow(R_CBW + k) * u_buf[HALO - 2 + k + r0:HALO - 2 + k + r0 + rb, :]
            vb_ref[rs, :] = vb
            pb_ref[rs, :] = (cb_ref[rs, :] * vb).astype(BF16)
        xa_buf[0:HALO, :] = xa_buf[Tc:Tc + HALO, :]
        u_buf[0:HALO, :] = u_buf[Tc:Tc + HALO, :]

        if comm:
            comm.ride(step, nb * nT, c_in, c_out, sems, False)

    seg, p_spec, w_spec = _mixer_specs(Tc, bw, nb, layer)
    out = pl.BlockSpec((Tc, bw), lambda n, i: (i, n))
    res = pl.pallas_call(
        body,
        grid=(nb, nT),
        in_specs=[seg(0), seg(1), seg(2), seg(3), seg(4), p_spec, w_spec, w_spec] + [ANY] * n_cin,
        out_specs=[out] * 5 + [ANY] * n_cout,
        out_shape=[jax.ShapeDtypeStruct((T, D), BF16), jax.ShapeDtypeStruct((T, D), BF16),
                   jax.ShapeDtypeStruct((T, D), F32), jax.ShapeDtypeStruct((T, D), F32),
                   jax.ShapeDtypeStruct((T, D), F32)] + (comm.out_shapes if comm else []),
        scratch_shapes=[pltpu.VMEM((Tc + HALO, bw), F32), pltpu.VMEM((Tc + HALO, bw), F32),
                        pltpu.VMEM((Tc, bw), F32), pltpu.VMEM((Tc, bw), F32), pltpu.VMEM((8, bw), F32)]
        + (comm.scratch() if comm else []),
        input_output_aliases={8 + ci: 5 + co for ci, co in comm.aliases.items()} if comm else {},
        compiler_params=_cparams(("arbitrary", "arbitrary") if comm else ("parallel", "arbitrary")),
        name=name,
    )(z, z, z, z, z, pch, wr, wi, *(comm.ins if comm else []))
    return (res[:5], res[5:]) if comm else res


def _mixer_bwd(name, layer, z, xc, hl, vb, dpa, dpb, dga, dgb, pch, wr, wi, comm=None):
    T, D = z.shape[0], z.shape[1] // 7
    bw, nb = wr.shape[-1], wr.shape[1]
    Tc = min(SEQ_CHUNK, T)
    nT = T // Tc
    tpc = Tc // 8
    rb = min(MIXER_ROW_BLOCK, Tc)
    n_cin = len(comm.ins) if comm else 0
    n_cout = len(comm.out_shapes) if comm else 0
    total = nb * nT

    def body(*refs):
        (xa_ref, ya_ref, cb_ref, cc_ref, cx_ref, xc_ref, hl_ref, hp_ref, vb_ref, dpa_ref, dpb_ref, dga_ref, dgb_ref,
         p_ref, wr_ref, wi_ref) = refs[:16]
        dz_ref, dwr_ref, dwi_ref, sm_ref = refs[16 + n_cin:20 + n_cin]
        (h_buf, a_buf, dxc_buf, dvb_buf, a_s, d_s, lam_s, r_s, i_s, m_s, dpr_s, dpi_s, sm8,
         carry, stage, out_sems) = refs[20 + n_cin + n_cout:36 + n_cin + n_cout]
        i = pl.program_id(1)
        step = pl.program_id(0) * nT + i
        if comm:
            c_in, c_out = refs[16:16 + n_cin], refs[20 + n_cin:20 + n_cin + n_cout]
            sems = refs[36 + n_cin + n_cout:]
            comm.ride(step, total, c_in, c_out, sems, True)

        slot = step % 2

        def out_copies(sl):
            rows = pl.ds(pl.multiple_of((nT - 1 - i) * Tc, Tc), Tc)
            return [pltpu.make_async_copy(
                stage.at[sl, s], dz_ref.at[rows, pl.ds(pl.multiple_of((s * nb + pl.program_id(0)) * bw, bw), bw)],
                out_sems.at[sl, s]) for s in range(7)]

        @pl.when(step >= 2)
        def _():
            for cp in out_copies(slot):
                cp.wait()

        dxa_ref, dya_ref, dcb_ref, dcc_ref, dcx_ref = [stage.at[slot, s] for s in range(5)]
        stage[slot, 5, :, :] = dga_ref[...]
        stage[slot, 6, :, :] = dgb_ref[...]

        @pl.when(i == 0)
        def _():
            a_buf[Tc:Tc + HALO, :] = jnp.zeros((HALO, bw), F32)
            dxc_buf[Tc:Tc + HALO, :] = jnp.zeros((HALO, bw), F32)
            dvb_buf[Tc:Tc + HALO, :] = jnp.zeros((HALO, bw), F32)
            carry[...] = jnp.zeros_like(carry)
            dwr_ref[...] = jnp.zeros_like(dwr_ref)
            dwi_ref[...] = jnp.zeros_like(dwi_ref)
            sm8[...] = jnp.zeros_like(sm8)

        blocks = [(g * rb, slice(g * rb, (g + 1) * rb)) for g in range(Tc // rb)]

        def prow(k):
            return p_ref[k:k + 1, :]

        def part8(v):
            return jnp.sum(v.reshape(rb // 8, 8, bw), axis=0)

        sums8 = {}

        def tally(k, v):
            sums8[k] = sums8[k] + part8(v) if k in sums8 else part8(v)

        sp = _softplus_neg(prow(R_LAM))
        xcb = xc_ref[...].astype(BF16)
        r_s[...] = jnp.dot(xcb, wr_ref[...], preferred_element_type=F32)
        i_s[...] = jnp.dot(xcb, wi_ref[...], preferred_element_type=F32)
        for _, rs in blocks:
            r = _sigmoid(r_s[rs, :] + prow(R_BR))
            ig = _sigmoid(i_s[rs, :] + prow(R_BI))
            log_a = (-LRU_C) * r * sp
            t = jnp.tanh(log_a)
            r_s[rs, :] = r
            i_s[rs, :] = ig
            m_s[rs, :] = jnp.sqrt(-2.0 * t / (1.0 - t))
            a_buf[rs, :] = jnp.exp(log_a)
            g, gp = _gelu_and_grad(ya_ref[rs, :])
            dpav = dpa_ref[rs, :]
            d_s[rs, :] = dpav * g
            dya_ref[rs, :] = (dpav * hl_ref[rs, :] * gp).astype(BF16)
            dpbv = dpb_ref[rs, :]
            dcb_ref[rs, :] = (dpbv * vb_ref[rs, :]).astype(BF16)
            dvb_buf[rs, :] = dpbv * cb_ref[rs, :]

        a_s[...] = a_buf[1:Tc + 1, :]
        _chunk_scan(a_s, d_s, lam_s, carry, tpc, bw, True)
        h_buf[HALO:HALO + Tc, :] = hl_ref[...]
        h_buf[0:HALO, :] = jnp.where(i == nT - 1, 0.0, hp_ref[...])

        for r0, rs in blocks:
            lamv, xcv, r, ig, a = lam_s[rs, :], xc_ref[rs, :], r_s[rs, :], i_s[rs, :], a_buf[rs, :]
            mult = m_s[rs, :]
            da = lamv * h_buf[HALO - 1 + r0:HALO - 1 + r0 + rb, :]
            dmult = lamv * (ig * xcv)
            dbx = lamv * mult
            dig = dbx * xcv
            dxc_buf[rs, :] = dbx * ig
            dlog_a = da * a - dmult * (a * a) / mult
            dpr = (dlog_a * ((-LRU_C) * sp)) * r * (1.0 - r)
            dpi = dig * ig * (1.0 - ig)
            dpr_s[rs, :] = dpr.astype(BF16)
            dpi_s[rs, :] = dpi.astype(BF16)
            tally(R_BR, dpr)
            tally(R_BI, dpi)
            tally(R_LAM, dlog_a * ((-LRU_C) * r))

        dprb, dpib = dpr_s[...], dpi_s[...]
        nt = (((1,), (1,)), ((), ()))
        tn = (((0,), (0,)), ((), ()))
        dxc_buf[0:Tc, :] += (lax.dot_general(dprb, wr_ref[...], nt, preferred_element_type=F32)
                             + lax.dot_general(dpib, wi_ref[...], nt, preferred_element_type=F32))
        dwr_ref[...] += lax.dot_general(xcb, dprb, tn, preferred_element_type=F32)
        dwi_ref[...] += lax.dot_general(xcb, dpib, tn, preferred_element_type=F32)

        for r0, rs in blocks:
            xav = xa_ref[rs, :]
            tally(R_CAB, dxc_buf[rs, :])
            dxa = jnp.zeros((rb, bw), F32)
            for k in range(4):
                sh = dxc_buf[3 - k + r0:3 - k + r0 + rb, :]
                dxa = dxa + prow(R_CAW + k) * sh
                tally(R_CAW + k, xav * sh)
            dxa_ref[rs, :] = dxa.astype(BF16)
            ccv, cxv = cc_ref[rs, :], cx_ref[rs, :]
            u = ccv * cxv
            du = jnp.zeros((rb, bw), F32)
            for k in range(3):
                sh = dvb_buf[2 - k + r0:2 - k + r0 + rb, :]
                du = du + prow(R_CBW + k) * sh
                tally(R_CBW + k, u * sh)
            dcc_ref[rs, :] = (du * cxv).astype(BF16)
            dcx_ref[rs, :] = (du * ccv).astype(BF16)

        for k, v in sums8.items():
            sm8[8 * k:8 * k + 8, :] += v
        a_buf[Tc:Tc + HALO, :] = a_buf[0:HALO, :]
        dxc_buf[Tc:Tc + HALO, :] = dxc_buf[0:HALO, :]
        dvb_buf[Tc:Tc + HALO, :] = dvb_buf[0:HALO, :]

        @pl.when(i == nT - 1)
        def _():
            sm_ref[...] = jnp.sum(sm8[...].reshape(R_ROWS, 8, bw), axis=1)
            sm_ref[R_LAM:R_LAM + 1, :] = sm_ref[R_LAM:R_LAM + 1, :] * (-_sigmoid(-prow(R_LAM)))

        for cp in out_copies(slot):
            cp.start()

        @pl.when(step == total - 1)
        def _():
            if total > 1:
                for cp in out_copies(1 - slot):
                    cp.wait()
            for cp in out_copies(slot):
                cp.wait()

        if comm:
            comm.ride(step, total, c_in, c_out, sems, False)

    def seg(s):
        return pl.BlockSpec((Tc, bw), lambda n, i: (nT - 1 - i, s * nb + n))

    blk = pl.BlockSpec((Tc, bw), lambda n, i: (nT - 1 - i, n))
    halo = pl.BlockSpec((8, bw), lambda n, i: (jnp.maximum((nT - 1 - i) * tpc - 1, 0), n))
    p_spec = pl.BlockSpec((None, R_ROWS, bw), lambda n, i: (layer, 0, n))
    w_spec = pl.BlockSpec((None, None, bw, bw), lambda n, i: (layer, n, 0, 0))
    dw_spec = pl.BlockSpec((None, bw, bw), lambda n, i: (n, 0, 0))
    res = pl.pallas_call(
        body,
        grid=(nb, nT),
        in_specs=[seg(0), seg(1), seg(2), seg(3), seg(4), blk, blk, halo, blk, blk, blk, blk, blk,
                  p_spec, w_spec, w_spec] + [ANY] * n_cin,
        out_specs=[ANY, dw_spec, dw_spec, pl.BlockSpec((R_ROWS, bw), lambda n, i: (0, n))] + [ANY] * n_cout,
        out_shape=[jax.ShapeDtypeStruct((T, 7 * D), BF16), jax.ShapeDtypeStruct((nb, bw, bw), F32),
                   jax.ShapeDtypeStruct((nb, bw, bw), F32), jax.ShapeDtypeStruct((R_ROWS, D), F32)]
        + (comm.out_shapes if comm else []),
        scratch_shapes=[pltpu.VMEM((Tc + HALO, bw), F32)] * 4 + [pltpu.VMEM((Tc, bw), F32)] * 6
        + [pltpu.VMEM((Tc, bw), BF16)] * 2 + [pltpu.VMEM((8 * R_ROWS, bw), F32), pltpu.VMEM((8, bw), F32),
                                              pltpu.VMEM((2, 7, Tc, bw), BF16), pltpu.SemaphoreType.DMA((2, 7))]
        + (comm.scratch() if comm else []),
        input_output_aliases={16 + ci: 4 + co for ci, co in comm.aliases.items()} if comm else {},
        compiler_params=_cparams(("arbitrary", "arbitrary")),
        name=name,
    )(z, z, z, z, z, xc, hl, hl, vb, dpa, dpb, dga, dgb, pch, wr, wi, *(comm.ins if comm else []))
    return (res[:4], res[4:]) if comm else res


def _local_fwd_bwd(x, tgt, W, placed=None):
    T, D = x.shape
    g1, g2 = W["g1"], W["g2"]
    depth = g1.shape[0]
    FF = 4 * D
    if placed is None:
        mats = {(n, l): W[n][l] for n in MATS for l in range(depth)}
    else:
        mats = {}
        mats["w_in", 0], = _run_carry("gather_first", _gather_carry([placed["w_in", 0]], [(0, 1, 0, 1)]))

    def gathering(specs):
        if placed is None or not specs:
            return None, []
        keys = [(n, l) for n, l, _, _, _ in specs]
        arrays = [mats.get(k, placed[k]) for k in keys]
        return _gather_carry(arrays, [(i, d, part, nparts) for i, (_, _, d, part, nparts) in enumerate(specs)]), keys

    def hosted(call, specs, **kw):
        carry, keys = gathering(specs)
        if carry is None:
            return call(**kw)
        res, got = call(**kw, **{("comm" if call.func is _mixer_fwd else "carry"): carry})
        mats.update(zip(keys, got))
        return res

    saved = []
    xs = x
    for l in range(depth):
        h = _rms_fwd(f"rms1_fwd_{l}", xs, g1[l][None])
        nxt = l + 1 < depth
        projs = [("w_pa", l, 0, 0, 1), ("w_pb", l, 0, 0, 1), ("w_o", l, 0, 0, 1)]
        z = hosted(functools.partial(_mm, f"in_proj_{l}", "nn", h, mats["w_in", l], T, 7 * D, D, [F32]),
                   projs + ([("w_mlp1", l, 1, 0, 1)] if nxt else []),
                   epilogue=lambda acc, b: (acc + b,), extras=[(W["b_in"][l][None], "bias", 0)])
        pa, pb, xc, hl, vb = hosted(
            functools.partial(_mixer_fwd, f"mixer_fwd_{l}", l, z, W["pch"], W["wr"], W["wi"]),
            [("w_mlp2", l, 0, 0, 1)] if nxt else [("w_mlp1", l, 1, 0, 1)])
        oa = _mm(f"proj_a_{l}", "nn", pa, mats["w_pa", l], T, D, D, [F32])

        def merge(acc, oav, ga, gb):
            return acc, _sigmoid(ga) * oav + _sigmoid(gb) * acc

        ob, mg = _mm(f"proj_b_merge_{l}", "nn", pb, mats["w_pb", l], T, D, D, [F32, BF16], epilogue=merge,
                     tiles=MM_TILES_FUSED,
                     extras=[(oa, "tile", 0), (z, "tile", 5 * D), (z, "tile", 6 * D)])
        x1 = _mm(f"out_proj_{l}", "nn", mg, mats["w_o", l], T, D, D, [F32],
                 epilogue=lambda acc, res: (res + acc,), extras=[(xs, "tile", 0)])
        h2 = _rms_fwd(f"rms2_fwd_{l}", x1, g2[l][None])

        def relu2(acc):
            pr = jnp.maximum(acc, 0.0)
            return pr * pr, pr

        u, pr = hosted(functools.partial(_mm, f"mlp1_{l}", "nn", h2, mats["w_mlp1", l], T, FF, D, [BF16, BF16]),
                       [("w_in", l + 1, 1, 0, 2)] if nxt else [("w_mlp2", l, 0, 0, 1)], epilogue=relu2)
        x2 = hosted(functools.partial(_mm, f"mlp2_{l}", "nn", u, mats["w_mlp2", l], T, D, FF, [F32]),
                    [("w_in", l + 1, 1, 1, 2)] if nxt else [],
                    epilogue=lambda acc, res: (res + acc,), extras=[(x1, "tile", 0)])
        saved.append(dict(x0=xs, h=h, z=z, pa=pa, pb=pb, xc=xc, hl=hl, vb=vb, oa=oa, ob=ob, mg=mg, x1=x1,
                          h2=h2, u=u, pr=pr))
        xs = x2

    dx, dxb, dgf, loss_blk = _loss_head("loss_head", xs, W["gf"][None], tgt)

    gmat, got, sums, landed = {}, {}, {}, {}

    def reducing(call, swaps=(), scatters=(), part=None, mixer=False, **kw):
        swaps, scatters = [(n, l) for n in swaps], [(n, l) for n in scatters]
        if placed is None:
            return call(**kw)
        if part is not None:
            key = part[:2]
            carry = _scatter_carry([sums[key]], part[2], part[3], [landed[key]] if key in landed else None)
            res, moved = call(**kw, carry=carry)
            landed[key] = moved[0]
            return res
        cs = _swap_carry([gmat[k] for k in swaps], [BIG_DIM[k[0]] - 1 for k in swaps]) if swaps else None
        cc = _scatter_carry([sums[k] for k in scatters]) if scatters else None
        if cs is None and cc is None:
            return call(**kw)
        res, moved = call(**kw, **{("comm" if mixer else "carry"): _merge_carries([cs, cc])})
        got.update(zip(swaps, moved[:len(swaps)]))
        landed.update(zip(scatters, moved[len(swaps):]))
        return res

    def add(names):
        if placed is not None:
            for n in names:
                sums[n, l] = _add_halves(f"add_halves_{n}_{l}", gmat[n, l], got[n, l], BIG_DIM[n] - 1, W["core"])

    grads = [None] * depth
    for l in reversed(range(depth)):
        s = saved[l]
        dp = _mm(f"mlp2_dx_{l}", "nt", dxb, mats["w_mlp2", l], T, FF, D, [BF16],
                 epilogue=lambda acc, prv: (2.0 * prv.astype(F32) * acc,), extras=[(s["pr"], "tile", 0)])
        dw2 = gmat["w_mlp2", l] = reducing(functools.partial(_mm, f"mlp2_dw_{l}", "tn", s["u"], dxb, FF, D, T, [BF16]),
                                           part=("w_in", l + 1, 1, 2) if l + 1 < depth else None)
        dh2 = reducing(functools.partial(_mm, f"mlp1_dx_{l}", "nt", dp, mats["w_mlp1", l], T, D, FF, [F32]),
                       swaps=["w_mlp2"])
        add(["w_mlp2"])
        dw1 = gmat["w_mlp1", l] = reducing(functools.partial(_mm, f"mlp1_dw_{l}", "tn", s["h2"], dp, D, FF, T, [BF16]),
                                           scatters=["w_mlp2"])
        dx1, dx1b, dg2 = _rms_bwd(f"rms2_bwd_{l}", s["x1"], g2[l][None], dh2, dx)

        def unmerge(acc, ga, gb, oav, obv):
            sa, sb = _sigmoid(ga), _sigmoid(gb)
            return acc * sa, acc * sb, acc * oav * sa * (1.0 - sa), acc * obv * sb * (1.0 - sb)

        doa, dob, dga, dgb = reducing(
            functools.partial(_mm, f"out_proj_dx_{l}", "nt", dx1b, mats["w_o", l], T, D, D, [BF16] * 4),
            swaps=["w_mlp1"], tiles=MM_TILES_FUSED, epilogue=unmerge,
            extras=[(s["z"], "tile", 5 * D), (s["z"], "tile", 6 * D), (s["oa"], "tile", 0), (s["ob"], "tile", 0)])
        add(["w_mlp1"])
        dwo = gmat["w_o", l] = _mm(f"out_proj_dw_{l}", "tn", s["mg"], dx1b, D, D, T, [BF16])
        dpa = _mm(f"proj_a_dx_{l}", "nt", doa, mats["w_pa", l], T, D, D, [F32])
        dwpa = gmat["w_pa", l] = _mm(f"proj_a_dw_{l}", "tn", s["pa"], doa, D, D, T, [BF16])
        dpb = _mm(f"proj_b_dx_{l}", "nt", dob, mats["w_pb", l], T, D, D, [F32])
        dwpb = gmat["w_pb", l] = _mm(f"proj_b_dw_{l}", "tn", s["pb"], dob, D, D, T, [BF16])
        dz, dwr, dwi, sm = reducing(
            functools.partial(_mixer_bwd, f"mixer_bwd_{l}", l, s["z"], s["xc"], s["hl"], s["vb"], dpa, dpb, dga, dgb,
                              W["pch"], W["wr"], W["wi"]),
            swaps=["w_o", "w_pa", "w_pb"], scatters=["w_mlp1"], mixer=True)
        add(["w_o", "w_pa", "w_pb"])
        dwin = gmat["w_in", l] = reducing(
            functools.partial(_mm, f"in_proj_dw_{l}", "tn", s["h"], dz, D, 7 * D, T, [BF16]),
            scatters=["w_o", "w_pa", "w_pb"])
        dbin = reducing(functools.partial(_colsum, f"bias_grad_{l}", dz), swaps=["w_in"])
        add(["w_in"])
        dh = reducing(functools.partial(_mm, f"in_proj_dx_{l}", "nt", dz, mats["w_in", l], T, D, 7 * D, [F32]),
                      scatters=["w_in"] if l == 0 else [], part=("w_in", l, 0, 2) if l > 0 else None)
        dx, dxb, dg1 = _rms_bwd(f"rms1_bwd_{l}", s["x0"], g1[l][None], dh, dx1)
        grads[l] = dict(w_in=dwin, w_pa=dwpa, w_pb=dwpb, w_o=dwo, w_mlp1=dw1, w_mlp2=dw2, wr=dwr, wi=dwi,
                        sm=sm, b_in=dbin, g1=dg1, g2=dg2)
    return loss_blk[0, 0], dx, grads, dgf, sums, landed


ANY = pl.BlockSpec(memory_space=pl.ANY)


def _place():
    x, y, c = lax.axis_index("x"), lax.axis_index("y"), lax.axis_index("c")
    peers = [(1 - x, y, c), (x, 1 - y, c), (1 - x, 1 - y, c)]
    chips = [2 * (1 - x) + y, 2 * x + (1 - y), 2 * (1 - x) + (1 - y)]
    return (x, y, c), 2 * x + y, peers, chips


def _window(ref, dim, q, size):
    idx = [slice(None)] * len(ref.shape)
    idx[dim] = pl.ds(q * size, size)
    return ref.at[tuple(idx)]


def _gather_weights(shards, dims, small):
    n = len(shards)
    sizes = [s.shape[d] for s, d in zip(shards, dims)]
    full = [jax.ShapeDtypeStruct(s.shape[:d] + (s.shape[d] * N_CHIPS,) + s.shape[d + 1:], s.dtype)
            for s, d in zip(shards, dims)]
    full.append(jax.ShapeDtypeStruct((N_CHIPS,) + small.shape, small.dtype))

    def body(*refs):
        ins, outs = refs[:n + 1], refs[n + 1:2 * n + 2]
        send_sems, recv_sems, local_sems = refs[2 * n + 2:]
        _, k, peers, chips = _place()

        def dst(w, q):
            return outs[w].at[q] if w == n else _window(outs[w], dims[w], q, sizes[w])

        local = [pltpu.make_async_copy(ins[w], dst(w, k), local_sems.at[w]) for w in range(n + 1)]
        for cp in local:
            cp.start()
        sends = []
        for p, peer in enumerate(peers):
            for w in range(n + 1):
                s = p * (n + 1) + w
                sends.append(pltpu.make_async_remote_copy(
                    src_ref=ins[w], dst_ref=dst(w, k), send_sem=send_sems.at[s], recv_sem=recv_sems.at[s],
                    device_id=peer, device_id_type=MESH))
        for cp in sends:
            cp.start()
        for p, peer in enumerate(peers):
            for w in range(n + 1):
                s = p * (n + 1) + w
                pltpu.make_async_remote_copy(
                    src_ref=ins[w], dst_ref=dst(w, chips[p]), send_sem=send_sems.at[s], recv_sem=recv_sems.at[s],
                    device_id=peer, device_id_type=MESH).wait_recv()
        for cp in sends:
            cp.wait_send()
        for cp in local:
            cp.wait()

    return pl.pallas_call(
        body,
        in_specs=[ANY] * (n + 1),
        out_specs=[ANY] * (n + 1),
        out_shape=full,
        scratch_shapes=[pltpu.SemaphoreType.DMA((3 * (n + 1),)), pltpu.SemaphoreType.DMA((3 * (n + 1),)),
                        pltpu.SemaphoreType.DMA((n + 1,))],
        name="gather_weights",
    )(*shards, small)


def _scatter_grads(grads, dims):
    n, depth = len(grads), len(grads[0])
    sizes = [g[0].shape[d] // N_CHIPS for g, d in zip(grads, dims)]
    land = []
    for g, d, sz in zip(grads, dims, sizes):
        shp = g[0].shape
        land.append(jax.ShapeDtypeStruct((N_CHIPS, depth) + shp[:d] + (sz,) + shp[d + 1:], g[0].dtype))

    def body(*refs):
        ins, outs = refs[:n * depth], refs[n * depth:n * depth + n]
        send_sems, recv_sems, local_sems = refs[n * depth + n:]
        _, k, peers, chips = _place()

        def src(w, l, q):
            return _window(ins[w * depth + l], dims[w], q, sizes[w])

        local = [pltpu.make_async_copy(src(w, l, k), outs[w].at[3, l], local_sems.at[w * depth + l])
                 for w in range(n) for l in range(depth)]
        for cp in local:
            cp.start()
        sends = []
        for p, peer in enumerate(peers):
            for w in range(n):
                for l in range(depth):
                    s = (p * n + w) * depth + l
                    sends.append(pltpu.make_async_remote_copy(
                        src_ref=src(w, l, chips[p]), dst_ref=outs[w].at[p, l], send_sem=send_sems.at[s],
                        recv_sem=recv_sems.at[s], device_id=peer, device_id_type=MESH))
        for cp in sends:
            cp.start()
        for cp in sends:
            cp.wait_recv()
        for cp in sends:
            cp.wait_send()
        for cp in local:
            cp.wait()

    flat = [g for gl in grads for g in gl]
    return pl.pallas_call(
        body,
        in_specs=[ANY] * (n * depth),
        out_specs=[ANY] * n,
        out_shape=land,
        scratch_shapes=[pltpu.SemaphoreType.DMA((3 * n * depth,)), pltpu.SemaphoreType.DMA((3 * n * depth,)),
                        pltpu.SemaphoreType.DMA((n * depth,))],
        name="scatter_grads",
    )(*flat)


def _sum_slots(name, land):
    _, R, C = land.shape
    tr, tc = _div_tile(R, 512, 8), _div_tile(C, 1024)

    def body(a_ref, b_ref, c_ref, d_ref, o_ref):
        o_ref[...] = ((d_ref[...].astype(F32) + a_ref[...].astype(F32)) + b_ref[...].astype(F32)) \
            + c_ref[...].astype(F32)

    def slot(q):
        return pl.BlockSpec((None, tr, tc), lambda i, j: (q, i, j))

    return pl.pallas_call(
        body,
        grid=(R // tr, C // tc),
        in_specs=[slot(0), slot(1), slot(2), slot(3)],
        out_specs=pl.BlockSpec((tr, tc), lambda i, j: (i, j)),
        out_shape=jax.ShapeDtypeStruct((R, C), F32),
        compiler_params=_cparams(("parallel", "parallel")),
        name=name,
    )(land, land, land, land)


def _swap_with_sibling(parts):
    n = len(parts)

    def body(*refs):
        ins, outs = refs[:n], refs[n:2 * n]
        send_sems, recv_sems = refs[2 * n:]
        (x, y, c), _, _, _ = _place()
        copies = [pltpu.make_async_remote_copy(
            src_ref=ins[w], dst_ref=outs[w], send_sem=send_sems.at[w], recv_sem=recv_sems.at[w],
            device_id=(x, y, 1 - c), device_id_type=MESH) for w in range(n)]
        for cp in copies:
            cp.start()
        for cp in copies:
            cp.wait()

    return pl.pallas_call(
        body,
        in_specs=[ANY] * n,
        out_specs=[ANY] * n,
        out_shape=[jax.ShapeDtypeStruct(p.shape, p.dtype) for p in parts],
        scratch_shapes=[pltpu.SemaphoreType.DMA((n,)), pltpu.SemaphoreType.DMA((n,))],
        name="swap_with_sibling",
    )(*parts)


class _Carry:
    def __init__(self, ins, out_shapes, rounds, counts, aliases=None, marks=(0.6, 0.92)):
        self.ins, self.out_shapes, self.rounds, self.counts = list(ins), list(out_shapes), list(rounds), list(counts)
        self.aliases, self.marks = dict(aliases or {}), marks

    def scratch(self):
        return [pltpu.SemaphoreType.DMA((n,)) for n in self.counts for _ in range(2)]

    def _copies(self, r, in_refs, out_refs, sems, landing):
        remote = self.rounds[r](in_refs, out_refs)
        assert len(remote) == self.counts[r], (r, len(remote), self.counts[r])
        return [pltpu.make_async_remote_copy(src_ref=s, dst_ref=(land if landing else d), send_sem=sems[2 * r].at[i],
                                             recv_sem=sems[2 * r + 1].at[i], device_id=peer, device_id_type=MESH)
                for i, (s, d, peer, land) in enumerate(remote)]

    def begin(self, r, in_refs, out_refs, sems):
        if r > 0:
            for cp in self._copies(r - 1, in_refs, out_refs, sems, True):
                cp.wait_recv()
        for cp in self._copies(r, in_refs, out_refs, sems, False):
            cp.start()

    def end(self, in_refs, out_refs, sems):
        last = len(self.rounds) - 1
        for cp in self._copies(last, in_refs, out_refs, sems, True):
            cp.wait_recv()
        for r in range(last + 1):
            for cp in self._copies(r, in_refs, out_refs, sems, False):
                cp.wait_send()

    def ride(self, step, total, in_refs, out_refs, sems, first):
        if first:
            @pl.when(step == 0)
            def _():
                self.begin(0, in_refs, out_refs, sems)
            return
        for r in range(1, len(self.rounds)):
            @pl.when(step == min(total - 1, int(total * self.marks[r - 1])))
            def _(r=r):
                self.begin(r, in_refs, out_refs, sems)

        @pl.when(step == total - 1)
        def _():
            self.end(in_refs, out_refs, sems)


def _run_carry(name, carry):
    n_in, n_out = len(carry.ins), len(carry.out_shapes)

    def body(*refs):
        in_refs, out_refs, sems = refs[:n_in], refs[n_in:n_in + n_out], refs[n_in + n_out:]
        for r in range(len(carry.rounds)):
            carry.begin(r, in_refs, out_refs, sems)
        carry.end(in_refs, out_refs, sems)

    return pl.pallas_call(
        body,
        in_specs=[ANY] * n_in,
        out_specs=[ANY] * n_out,
        out_shape=carry.out_shapes,
        scratch_shapes=carry.scratch(),
        input_output_aliases=carry.aliases,
        name=name,
    )(*carry.ins)


def _comm_call(name, ins, out_shapes, plan, n_local, n_remote, aliases=None):
    assert n_local == 0
    return _run_carry(name, _Carry(ins, out_shapes, [lambda i, o: plan(i, o)[1]], [n_remote], aliases=aliases))


def _gather_carry(arrays, items):
    shapes = [a.shape for a in arrays]

    def ring():
        x, y, c = lax.axis_index("x"), lax.axis_index("y"), lax.axis_index("c")
        first = (x + (1 - c) * (1 - 2 * x), y + c * (1 - 2 * y), c)
        second = (x + c * (1 - 2 * x), y + (1 - c) * (1 - 2 * y), c)
        return c, 2 * x + y, first, second

    def chip(pos):
        return 2 * pos[0] + pos[1]

    def round0(ins, outs):
        c, k, first, second = ring()
        remote = []
        for item in items:
            win = window(outs[item[0]], item)
            remote.append((win(2 * k + c), win(2 * k + c), first, win(2 * chip(first) + c)))
            remote.append((win(2 * k + c), win(2 * k + c), second, win(2 * chip(second) + c)))
        return remote

    def round1(ins, outs):
        c, k, first, second = ring()
        remote = []
        for item in items:
            win = window(outs[item[0]], item)
            relayed = win(2 * chip(first) + c)
            remote.append((relayed, relayed, second, win(2 * (3 - k) + c)))
        return remote

    def window(ref, item):
        idx, d, part, nparts = item
        h = shapes[idx][d] // (2 * N_CHIPS)
        rows = shapes[idx][1 - d] // nparts

        def win(j):
            sl = [None, None]
            sl[d] = pl.ds(j * h, h)
            sl[1 - d] = pl.ds(part * rows, rows)
            return ref.at[tuple(sl)]

        return win

    def round2(ins, outs):
        (x, y, c), _, _, chips = _place()
        remote = []
        for item in items:
            win = window(outs[item[0]], item)
            for p in range(3):
                remote.append((win(2 * chips[p] + c), win(2 * chips[p] + c), (x, y, 1 - c),
                               win(2 * chips[p] + 1 - c)))
        return remote

    n = len(items)
    return _Carry(arrays, [jax.ShapeDtypeStruct(a.shape, a.dtype) for a in arrays], [round0, round1, round2],
                  [2 * n, n, 3 * n], aliases={i: i for i in range(len(arrays))})


def _place_shard(name, w, layer, dim, chip):
    _, a, b = w.shape
    full = (a * N_CHIPS, b) if dim == 0 else (a, b * N_CHIPS)
    tr, tc = _div_tile(a, 512, 16), _div_tile(b, 2048)
    nr, nc = a // tr, b // tc

    def out_map(i, j, chip_ref):
        return (chip_ref[0] * nr + i, j) if dim == 0 else (i, chip_ref[0] * nc + j)

    def body(chip_ref, w_ref, o_ref):
        o_ref[...] = w_ref[...].astype(o_ref.dtype)

    return pl.pallas_call(
        body,
        grid_spec=pltpu.PrefetchScalarGridSpec(
            num_scalar_prefetch=1, grid=(nr, nc),
            in_specs=[pl.BlockSpec((None, tr, tc), lambda i, j, chip_ref: (layer, i, j))],
            out_specs=pl.BlockSpec((tr, tc), out_map)),
        out_shape=jax.ShapeDtypeStruct(full, BF16),
        compiler_params=_cparams(("parallel", "parallel")),
        name=name,
    )(chip, w)


def _half_shape(shape, dim):
    return shape[:dim] + (shape[dim] // (2 * N_CHIPS),) + shape[dim + 1:]


def _swap_carry(grads, dims):
    shapes = [jax.ShapeDtypeStruct((N_CHIPS,) + _half_shape(g.shape, d), g.dtype) for g, d in zip(grads, dims)]

    def plan(ins, outs):
        (x, y, c), _, _, _ = _place()
        remote = []
        for w, d in enumerate(dims):
            h = grads[w].shape[d] // (2 * N_CHIPS)
            for q in range(N_CHIPS):
                remote.append((_window(ins[w], d, 2 * q + 1 - c, h), outs[w].at[q], (x, y, 1 - c), outs[w].at[q]))
        return remote

    return _Carry(grads, shapes, [plan], [N_CHIPS * len(grads)])


def _scatter_carry(sums, part=0, nparts=1, land=None):
    n = len(sums)

    def plan(ins, outs):
        _, _, peers, chips = _place()
        remote = []
        for w in range(n):
            r = sums[w].shape[1] // nparts
            rows = pl.ds(part * r, r)
            for p in range(3):
                remote.append((ins[w].at[chips[p], rows], outs[w].at[p, rows], peers[p], outs[w].at[p, rows]))
        return remote

    shapes = [jax.ShapeDtypeStruct((3,) + s.shape[1:], s.dtype) for s in sums]
    if land is None:
        return _Carry(sums, shapes, [plan], [3 * n])
    return _Carry(list(sums) + list(land), shapes, [plan], [3 * n], aliases={n + w: w for w in range(n)})


def _merge_carries(carries):
    carries = [c for c in carries if c is not None]
    if len(carries) <= 1:
        return carries[0] if carries else None
    ins = [a for c in carries for a in c.ins]
    outs = [s for c in carries for s in c.out_shapes]

    def plan(in_refs, out_refs):
        remote, i0, o0 = [], 0, 0
        for c in carries:
            remote += c.rounds[0](in_refs[i0:i0 + len(c.ins)], out_refs[o0:o0 + len(c.out_shapes)])
            i0, o0 = i0 + len(c.ins), o0 + len(c.out_shapes)
        return remote

    assert all(len(c.rounds) == 1 and not c.aliases for c in carries)
    return _Carry(ins, outs, [plan], [sum(c.counts[0] for c in carries)])


def _add_halves(name, g, got, dim, core):
    R, C = g.shape
    if dim == 1:
        r, cc = R, C // (2 * N_CHIPS)
    else:
        r, cc = R // (2 * N_CHIPS), C
    tr, tc = _div_tile(r, 512, 16), _div_tile(cc, 1024)
    nr, nc = r // tr, cc // tc

    def g_map(q, i, j, core_ref):
        w = 2 * q + core_ref[0]
        return (i, w * nc + j) if dim == 1 else (w * nr + i, j)

    def body(core_ref, g_ref, got_ref, o_ref):
        o_ref[...] = (g_ref[...].astype(F32) + got_ref[...].astype(F32)).astype(o_ref.dtype)

    slab = pl.BlockSpec((None, tr, tc), lambda q, i, j, core_ref: (q, i, j))
    return pl.pallas_call(
        body,
        grid_spec=pltpu.PrefetchScalarGridSpec(
            num_scalar_prefetch=1, grid=(N_CHIPS, nr, nc),
            in_specs=[pl.BlockSpec((tr, tc), g_map), slab], out_specs=slab),
        out_shape=jax.ShapeDtypeStruct((N_CHIPS, r, cc), g.dtype),
        compiler_params=_cparams(("parallel", "parallel", "parallel")),
        name=name,
    )(core, g, got)


def _reduce_into(name, sums, land, acc, layer, dim, shape, where):
    _, r, cc = sums.shape
    tr, tc = _div_tile(r, 512, 16), _div_tile(cc, 1024)
    nr, nc = r // tr, cc // tc

    def out_map(i, j, s):
        return (layer, s[1] * nr + i, j) if dim == 1 else (layer, i, s[1] * nc + j)

    def body(*refs):
        own, a_ref, b_ref, c_ref, o_ref = refs[1], refs[2], refs[3], refs[4], refs[-1]
        o_ref[...] = ((own[...].astype(F32) + a_ref[...].astype(F32)) + b_ref[...].astype(F32)) \
            + c_ref[...].astype(F32)

    def slot(p):
        return pl.BlockSpec((None, tr, tc), lambda i, j, s: (p, i, j))

    in_specs = [pl.BlockSpec((None, tr, tc), lambda i, j, s: (s[0], i, j)), slot(0), slot(1), slot(2)]
    args = [where, sums, land, land, land]
    if acc is not None:
        in_specs.append(ANY)
        args.append(acc)
    return pl.pallas_call(
        body,
        grid_spec=pltpu.PrefetchScalarGridSpec(
            num_scalar_prefetch=1, grid=(nr, nc), in_specs=in_specs,
            out_specs=pl.BlockSpec((None, tr, tc), out_map)),
        out_shape=jax.ShapeDtypeStruct(shape, F32),
        input_output_aliases={5: 0} if acc is not None else {},
        compiler_params=_cparams(("parallel", "parallel")),
        name=name,
    )(*args)


def _join_halves(name, grads, dims):
    n = len(grads)

    def plan(ins, outs):
        (x, y, c), _, _, _ = _place()
        remote = []
        for w in range(n):
            d, h = dims[w], grads[w].shape[dims[w]] // 2
            mine = _window(outs[w], d, c, h)
            remote.append((mine, mine, (x, y, 1 - c), _window(outs[w], d, 1 - c, h)))
        return [], remote

    return _comm_call(name, grads, [jax.ShapeDtypeStruct(g.shape, g.dtype) for g in grads], plan, 0, n,
                      aliases={w: w for w in range(n)})


def _allreduce_small(pack):
    R, C = pack.shape

    def gather_body(in_ref, slots_ref, send_sems, recv_sems, local_sem):
        x, y, c = lax.axis_index("x"), lax.axis_index("y"), lax.axis_index("c")
        me = 4 * x + 2 * y + c
        flips = [(dx, dy, dc) for dx in (0, 1) for dy in (0, 1) for dc in (0, 1)][1:]

        def flip(v, d):
            return 1 - v if d else v

        local = pltpu.make_async_copy(in_ref, slots_ref.at[me], local_sem)
        local.start()
        sends = []
        for j, (dx, dy, dc) in enumerate(flips):
            px, py, pc = flip(x, dx), flip(y, dy), flip(c, dc)
            sends.append((pltpu.make_async_remote_copy(
                src_ref=in_ref, dst_ref=slots_ref.at[me], send_sem=send_sems.at[j], recv_sem=recv_sems.at[j],
                device_id=(px, py, pc), device_id_type=MESH), 4 * px + 2 * py + pc, j))
        for cp, _, _ in sends:
            cp.start()
        for cp, peer_id, j in sends:
            pltpu.make_async_remote_copy(
                src_ref=in_ref, dst_ref=slots_ref.at[peer_id], send_sem=send_sems.at[j], recv_sem=recv_sems.at[j],
                device_id=(x, y, c), device_id_type=MESH).wait_recv()
        for cp, _, _ in sends:
            cp.wait_send()
        local.wait()

    slots = pl.pallas_call(
        gather_body,
        in_specs=[ANY],
        out_specs=ANY,
        out_shape=jax.ShapeDtypeStruct((N_DEV, R, C), pack.dtype),
        scratch_shapes=[pltpu.SemaphoreType.DMA((N_DEV - 1,)), pltpu.SemaphoreType.DMA((N_DEV - 1,)),
                        pltpu.SemaphoreType.DMA],
        name="allgather_small",
    )(pack)

    def sum_body(s_ref, o_ref):
        acc = s_ref[0]
        for d in range(1, N_DEV):
            acc = acc + s_ref[d]
        o_ref[...] = acc

    return pl.pallas_call(
        sum_body,
        out_shape=jax.ShapeDtypeStruct((R, C), pack.dtype),
        name="sum_small",
    )(slots)


def _adamw_math(w, g, m, v):
    m2 = ADAM_B1 * m + (1.0 - ADAM_B1) * g
    v2 = ADAM_B2 * v + (1.0 - ADAM_B2) * (g * g)
    m_hat = m2 / (1.0 - ADAM_B1 ** ADAM_STEP)
    v_hat = v2 / (1.0 - ADAM_B2 ** ADAM_STEP)
    delta = -ADAM_LR * (m_hat / (jnp.sqrt(v_hat) + ADAM_EPS) + ADAM_WD * w)
    return delta, m2, v2


def _adamw_big(name, w, m, v, g_parts):
    shape = w.shape
    C = shape[-1]
    R = w.size // C
    tr, tc = _div_tile(R, 256, 8), _div_tile(C, 1024)
    n_g = len(g_parts)

    def body(*refs):
        w_ref, m_ref, v_ref = refs[:3]
        g_ref, d_ref, nm_ref, nv_ref = refs[3 + n_g:]
        g = refs[3][...]
        for extra in refs[4:3 + n_g]:
            g = g + extra[...]
        delta, m2, v2 = _adamw_math(w_ref[...], g, m_ref[...], v_ref[...])
        g_ref[...], d_ref[...], nm_ref[...], nv_ref[...] = g, delta, m2, v2

    blk = pl.BlockSpec((tr, tc), lambda i, j: (i, j))
    outs = pl.pallas_call(
        body,
        grid=(R // tr, C // tc),
        in_specs=[blk] * (3 + n_g),
        out_specs=[blk] * 4,
        out_shape=[jax.ShapeDtypeStruct((R, C), F32)] * 4,
        compiler_params=_cparams(("parallel", "parallel")),
        name=name,
    )(w.reshape(R, C), m.reshape(R, C), v.reshape(R, C), *[g.reshape(R, C) for g in g_parts])
    return [o.reshape(shape) for o in outs]


def _adamw_small(name, w, g, m, v):
    shape = w.shape
    two_d = (w.size // shape[-1], shape[-1])

    def body(w_ref, g_ref, m_ref, v_ref, d_ref, nm_ref, nv_ref):
        d_ref[...], nm_ref[...], nv_ref[...] = _adamw_math(w_ref[...], g_ref[...], m_ref[...], v_ref[...])

    outs = pl.pallas_call(
        body,
        out_shape=[jax.ShapeDtypeStruct(two_d, F32)] * 3,
        name=name,
    )(w.reshape(two_d), g.reshape(two_d), m.reshape(two_d), v.reshape(two_d))
    return [o.reshape(shape) for o in outs]


SMALL_ROWS = 40
S_BIN, S_G1, S_G2 = 16, 24, 32
MATS = ("w_in", "w_pa", "w_pb", "w_o", "w_mlp1", "w_mlp2")
LRU = ("lru_wr", "lru_wi")
BIG_DIM = dict(w_in=2, w_pa=1, w_pb=1, w_o=1, w_mlp1=2, w_mlp2=1, lru_wr=2, lru_wi=2)
WEIGHTS = ("norm1_g", "w_in", "b_in", "conv_a_w", "conv_a_b", "lru_wr", "lru_br", "lru_wi", "lru_bi", "lru_lam",
           "conv_b_w", "w_pa", "w_pb", "w_o", "norm2_g", "w_mlp1", "w_mlp2", "final_g")


def _rows_at(a, r0, total):
    pad = [(0, 0)] * a.ndim
    pad[-2] = (r0, total - r0 - a.shape[-2])
    return jnp.pad(a, pad)


def kernel(x, norm1_g, w_in, b_in, conv_a_w, conv_a_b, lru_wr, lru_br, lru_wi, lru_bi, lru_lam, conv_b_w, w_pa, w_pb, w_o, norm2_g, w_mlp1, w_mlp2, final_g, loss_target, m_norm1_g, m_w_in, m_b_in, m_conv_a_w, m_conv_a_b, m_lru_wr, m_lru_br, m_lru_wi, m_lru_bi, m_lru_lam, m_conv_b_w, m_w_pa, m_w_pb, m_w_o, m_norm2_g, m_w_mlp1, m_w_mlp2, m_final_g, v_norm1_g, v_w_in, v_b_in, v_conv_a_w, v_conv_a_b, v_lru_wr, v_lru_br, v_lru_wi, v_lru_bi, v_lru_lam, v_conv_b_w, v_w_pa, v_w_pb, v_w_o, v_norm2_g, v_w_mlp1, v_w_mlp2, v_final_g):
    wts = dict(norm1_g=norm1_g, w_in=w_in, b_in=b_in, conv_a_w=conv_a_w, conv_a_b=conv_a_b, lru_wr=lru_wr,
               lru_br=lru_br, lru_wi=lru_wi, lru_bi=lru_bi, lru_lam=lru_lam, conv_b_w=conv_b_w, w_pa=w_pa,
               w_pb=w_pb, w_o=w_o, norm2_g=norm2_g, w_mlp1=w_mlp1, w_mlp2=w_mlp2, final_g=final_g)
    mom = dict(norm1_g=m_norm1_g, w_in=m_w_in, b_in=m_b_in, conv_a_w=m_conv_a_w, conv_a_b=m_conv_a_b,
               lru_wr=m_lru_wr, lru_br=m_lru_br, lru_wi=m_lru_wi, lru_bi=m_lru_bi, lru_lam=m_lru_lam,
               conv_b_w=m_conv_b_w, w_pa=m_w_pa, w_pb=m_w_pb, w_o=m_w_o, norm2_g=m_norm2_g, w_mlp1=m_w_mlp1,
               w_mlp2=m_w_mlp2, final_g=m_final_g)
    vel = dict(norm1_g=v_norm1_g, w_in=v_w_in, b_in=v_b_in, conv_a_w=v_conv_a_w, conv_a_b=v_conv_a_b,
               lru_wr=v_lru_wr, lru_br=v_lru_br, lru_wi=v_lru_wi, lru_bi=v_lru_bi, lru_lam=v_lru_lam,
               conv_b_w=v_conv_b_w, w_pa=v_w_pa, w_pb=v_w_pb, w_o=v_w_o, norm2_g=v_norm2_g, w_mlp1=v_w_mlp1,
               w_mlp2=v_w_mlp2, final_g=v_final_g)
    depth, D = norm1_g.shape
    nb, bw = lru_wr.shape[1], lru_wr.shape[3]
    chip = 2 * lax.axis_index("x") + lax.axis_index("y")

    small_parts = [conv_a_w.reshape(-1), conv_b_w.reshape(-1), lru_br.reshape(-1), lru_bi.reshape(-1)]
    small_len = sum(p.shape[0] for p in small_parts)
    small_rows = -(-small_len // 1024) * 8
    small = jnp.concatenate(small_parts + [jnp.zeros((small_rows * 128 - small_len,), F32)]).reshape(small_rows, 128)
    gathered = _gather_weights([wts[n].astype(BF16) for n in LRU], [BIG_DIM[n] for n in LRU], small)
    full = dict(zip(LRU, gathered[:-1]))
    items = [(n, l) for l in range(depth) for n in MATS]
    mat_dims = [BIG_DIM[n] - 1 for n, _ in items]
    where = jnp.stack([chip, lax.axis_index("c")]).astype(jnp.int32)
    placed = {(n, l): _place_shard(f"place_{n}_{l}", wts[n], l, BIG_DIM[n] - 1, where) for n, l in items}
    flat = gathered[-1].reshape(N_CHIPS, small_rows * 128)
    off = 0
    small_full = []
    for part, shard in zip(small_parts, (conv_a_w, conv_b_w, lru_br, lru_bi)):
        piece = flat[:, off:off + part.shape[0]].reshape((N_CHIPS,) + shard.shape)
        small_full.append(jnp.moveaxis(piece, 0, -2).reshape(shard.shape[:-1] + (N_CHIPS * shard.shape[-1],)))
        off += part.shape[0]
    caw_f, cbw_f, br_f, bi_f = small_full
    pch = (_rows_at(conv_a_b[:, None, :], R_CAB, R_ROWS) + _rows_at(br_f.reshape(depth, 1, D), R_BR, R_ROWS)
           + _rows_at(bi_f.reshape(depth, 1, D), R_BI, R_ROWS) + _rows_at(lru_lam[:, None, :], R_LAM, R_ROWS)
           + _rows_at(caw_f, R_CAW, R_ROWS) + _rows_at(cbw_f, R_CBW, R_ROWS))
    W = dict(b_in=b_in, pch=pch, wr=full["lru_wr"], wi=full["lru_wi"], g1=norm1_g, g2=norm2_g, gf=final_g,
             core=lax.axis_index("c").astype(jnp.int32).reshape(1))

    loss_local, dx, grads, dgf, sums, landed = _local_fwd_bwd(x[0], loss_target[0], W, placed)
    loss = lax.psum(loss_local, ("x", "y", "c"))

    key = dict(w_in="w_in", w_pa="w_pa", w_pb="w_pb", w_o="w_o", w_mlp1="w1", w_mlp2="w2", lru_wr="wr", lru_wi="wi")
    out_g, out_d, out_m, out_v = {}, {}, {}, {}
    per_layer = [[grads[l][key[n]].astype(BF16) for l in range(depth)] for n in LRU]
    land = _scatter_grads(per_layer, [BIG_DIM[n] - 1 for n in LRU])
    chip_sums = [_sum_slots(f"sum_slots_{n}", ld.reshape(N_CHIPS, -1, ld.shape[-1])) for n, ld in zip(LRU, land)]
    sib_sums = _swap_with_sibling(chip_sums)
    for n, mine, sib in zip(LRU, chip_sums, sib_sums):
        out_g[n], out_d[n], out_m[n], out_v[n] = _adamw_big(f"adamw_{n}", wts[n], mom[n], vel[n], [mine, sib])
    acc = {n: None for n in MATS}
    for n, l in reversed(items):
        acc[n] = _reduce_into(f"reduce_{n}_{l}", sums[n, l], landed[n, l], acc[n], l, BIG_DIM[n], wts[n].shape, where)
    joined = _join_halves("join_halves", [acc[n] for n in MATS], [BIG_DIM[n] for n in MATS])
    for n, g in zip(MATS, joined):
        out_g[n], out_d[n], out_m[n], out_v[n] = _adamw_big(f"adamw_{n}", wts[n], mom[n], vel[n], [g])

    rows = []
    for l in range(depth):
        g = grads[l]
        rows.append(_rows_at(g["sm"], 0, SMALL_ROWS) + _rows_at(g["b_in"].reshape(7, D), S_BIN, SMALL_ROWS)
                    + _rows_at(g["g1"], S_G1, SMALL_ROWS) + _rows_at(g["g2"], S_G2, SMALL_ROWS))
    rows.append(_rows_at(dgf, 0, 8))
    tot = _allreduce_small(jnp.concatenate(rows, axis=0))
    per = tot[:depth * SMALL_ROWS].reshape(depth, SMALL_ROWS, D)

    def cols_of_chip(a, axis):
        size = a.shape[axis] // N_CHIPS
        return lax.dynamic_slice_in_dim(a, chip * size, size, axis=axis)

    small_g = dict(
        norm1_g=per[:, S_G1], b_in=per[:, S_BIN:S_BIN + 7].reshape(depth, 7 * D),
        conv_a_w=cols_of_chip(per[:, R_CAW:R_CAW + 4], 2), conv_a_b=per[:, R_CAB],
        lru_br=cols_of_chip(per[:, R_BR].reshape(depth, nb, bw), 2),
        lru_bi=cols_of_chip(per[:, R_BI].reshape(depth, nb, bw), 2), lru_lam=per[:, R_LAM],
        conv_b_w=cols_of_chip(per[:, R_CBW:R_CBW + 3], 2), norm2_g=per[:, S_G2],
        final_g=tot[depth * SMALL_ROWS])
    for n, g in small_g.items():
        out_g[n] = g
        out_d[n], out_m[n], out_v[n] = _adamw_small(f"adamw_{n}", wts[n], g, mom[n], vel[n])

    return (loss, dx[None], *[out_g[n] for n in WEIGHTS], *[out_d[n] for n in WEIGHTS],
            *[out_m[n] for n in WEIGHTS], *[out_v[n] for n in WEIGHTS])
```

```python
import functools

import jax
import jax.numpy as jnp
from jax import lax
from jax.experimental import pallas as pl
from jax.experimental.pallas import tpu as pltpu

F32 = jnp.float32
BF16 = jnp.bfloat16
MESH = pl.DeviceIdType.MESH

EPS = 1e-6
LRU_C = 8.0
ADAM_LR = 0.001
ADAM_B1 = 0.9
ADAM_B2 = 0.999
ADAM_EPS = 1e-08
ADAM_WD = 0.01
ADAM_STEP = 10

N_CHIPS = 4
N_DEV = 8
HALO = 8
VMEM_LIMIT = 56 * 1024 * 1024
MM_TILES = (1024, 1024, 2048)
MM_TILES_FUSED = (512, 1024, 2048)
SEQ_CHUNK = 256
MIXER_ROW_BLOCK = 32
ROW_TILE = 256

R_CAB, R_BR, R_BI, R_LAM, R_CAW, R_CBW, R_ROWS = 0, 1, 2, 3, 4, 8, 16


def _cparams(sem):
    return pltpu.CompilerParams(dimension_semantics=sem, vmem_limit_bytes=VMEM_LIMIT)


def _div_tile(n, pref, unit=128):
    if n <= pref:
        return n
    t = (pref // unit) * unit
    while n % t:
        t -= unit
    return t


def _sigmoid(v):
    return 1.0 / (1.0 + jnp.exp(-v))


def _gelu_and_grad(y):
    k = 0.7978845608028654
    c = 0.044715
    y2 = y * y
    t = jnp.tanh(k * (y + c * y2 * y))
    g = 0.5 * y * (1.0 + t)
    gp = 0.5 * (1.0 + t) + 0.5 * y * (1.0 - t * t) * (k * (1.0 + 3.0 * c * y2))
    return g, gp


def _softplus_neg(lam):
    e = jnp.exp(-jnp.abs(lam))
    w = 1.0 + e
    l1p = jnp.where(w == 1.0, e, jnp.log(w) * e / jnp.where(w == 1.0, 1.0, w - 1.0))
    return jnp.maximum(-lam, 0.0) + l1p


def _mm(name, mode, a, b, M, N, K, out_dtypes, epilogue=None, extras=(), la=None, lb=None, tiles=None,
        carry=None):
    tiles = MM_TILES if tiles is None else tiles
    tm, tn, tk = _div_tile(M, tiles[0]), _div_tile(N, tiles[1]), _div_tile(K, tiles[2])
    assert M % tm == 0 and N % tn == 0 and K % tk == 0, (name, M, N, K)
    nk = K // tk

    def spec(lead, shape, imap):
        if lead is None:
            return pl.BlockSpec(shape, imap)
        return pl.BlockSpec((None,) + shape, lambda i, j, k: (lead,) + imap(i, j, k))

    if mode == "nn":
        a_spec = spec(la, (tm, tk), lambda i, j, k: (i, k))
        b_spec = spec(lb, (tk, tn), lambda i, j, k: (k, j))
        dn = (((1,), (0,)), ((), ()))
    elif mode == "nt":
        a_spec = spec(la, (tm, tk), lambda i, j, k: (i, k))
        b_spec = spec(lb, (tn, tk), lambda i, j, k: (j, k))
        dn = (((1,), (1,)), ((), ()))
    else:
        a_spec = spec(la, (tk, tm), lambda i, j, k: (k, i))
        b_spec = spec(lb, (tk, tn), lambda i, j, k: (k, j))
        dn = (((0,), (0,)), ((), ()))

    ex_arrays, ex_specs = [], []
    for arr, kind, off in extras:
        ex_arrays.append(arr)
        if kind == "bias":
            ex_specs.append(pl.BlockSpec((1, tn), lambda i, j, k: (0, j)))
        else:
            assert off % tn == 0
            ex_specs.append(pl.BlockSpec((tm, tn), lambda i, j, k, o=off // tn: (i, j + o)))
    n_ex, n_out = len(ex_arrays), len(out_dtypes)
    n_cin = len(carry.ins) if carry else 0
    n_cout = len(carry.out_shapes) if carry else 0
    n_in = 2 + n_ex + n_cin
    gi, gj = M // tm, N // tn

    def body(*refs):
        a_ref, b_ref = refs[0], refs[1]
        ex = refs[2:2 + n_ex]
        outs = refs[n_in:n_in + n_out]
        acc = refs[n_in + n_out + n_cout]
        i, j, k = pl.program_id(0), pl.program_id(1), pl.program_id(2)
        if carry:
            c_in, c_out = refs[2 + n_ex:n_in], refs[n_in + n_out:n_in + n_out + n_cout]
            sems = refs[n_in + n_out + n_cout + 1:]
            step = (i * gj + j) * nk + k
            carry.ride(step, gi * gj * nk, c_in, c_out, sems, True)

        def product():
            return lax.dot_general(a_ref[...], b_ref[...], dn, preferred_element_type=F32)

        def finish(r):
            vals = (r,) if epilogue is None else epilogue(r, *[e[...] for e in ex])
            for o, v in zip(outs, vals):
                o[...] = v.astype(o.dtype)

        if nk == 1:
            finish(product())
        else:
            @pl.when(k == 0)
            def _():
                acc[...] = product()

            @pl.when((k > 0) & (k < nk - 1))
            def _():
                acc[...] += product()

            @pl.when(k == nk - 1)
            def _():
                finish(acc[...] + product())

        if carry:
            carry.ride(step, gi * gj * nk, c_in, c_out, sems, False)

    res = pl.pallas_call(
        body,
        grid=(gi, gj, nk),
        in_specs=[a_spec, b_spec, *ex_specs] + [ANY] * n_cin,
        out_specs=[pl.BlockSpec((tm, tn), lambda i, j, k: (i, j)) for _ in range(n_out)] + [ANY] * n_cout,
        out_shape=[jax.ShapeDtypeStruct((M, N), d) for d in out_dtypes] + (carry.out_shapes if carry else []),
        scratch_shapes=[pltpu.VMEM((tm, tn) if nk > 1 else (8, 128), F32)] + (carry.scratch() if carry else []),
        input_output_aliases={2 + n_ex + ci: n_out + co for ci, co in carry.aliases.items()} if carry else {},
        compiler_params=_cparams(("arbitrary",) * 3 if carry else ("parallel", "parallel", "arbitrary")),
        name=name,
    )(a, b, *ex_arrays, *(carry.ins if carry else []))
    main = res[0] if n_out == 1 else res[:n_out]
    return (main, res[n_out:]) if carry else main


def _rms_fwd(name, x, g_row):
    T, D = x.shape
    tm = min(ROW_TILE, T)

    def body(x_ref, g_ref, h_ref):
        xv = x_ref[...]
        r = lax.rsqrt(jnp.mean(xv * xv, axis=-1, keepdims=True) + EPS)
        h_ref[...] = (xv * r * g_ref[...]).astype(BF16)

    return pl.pallas_call(
        body,
        grid=(T // tm,),
        in_specs=[pl.BlockSpec((tm, D), lambda i: (i, 0)), pl.BlockSpec((1, D), lambda i: (0, 0))],
        out_specs=pl.BlockSpec((tm, D), lambda i: (i, 0)),
        out_shape=jax.ShapeDtypeStruct((T, D), BF16),
        compiler_params=_cparams(("parallel",)),
        name=name,
    )(x, g_row)


def _rms_bwd(name, x, g_row, dh, dres):
    T, D = x.shape
    tm = min(ROW_TILE, T)

    def body(x_ref, g_ref, dh_ref, dres_ref, dx_ref, dxb_ref, dg_ref):
        xv, dhv = x_ref[...], dh_ref[...]
        r = lax.rsqrt(jnp.mean(xv * xv, axis=-1, keepdims=True) + EPS)
        gd = g_ref[...] * dhv
        c = jnp.mean(xv * gd, axis=-1, keepdims=True)
        dx = r * gd - xv * (r * r * r) * c + dres_ref[...]
        dx_ref[...] = dx
        dxb_ref[...] = dx.astype(BF16)

        @pl.when(pl.program_id(0) == 0)
        def _():
            dg_ref[...] = jnp.zeros_like(dg_ref)

        dg_ref[...] += jnp.sum(dhv * xv * r, axis=0, keepdims=True)

    row = pl.BlockSpec((tm, D), lambda i: (i, 0))
    vec = pl.BlockSpec((1, D), lambda i: (0, 0))
    return pl.pallas_call(
        body,
        grid=(T // tm,),
        in_specs=[row, vec, row, row],
        out_specs=[row, row, vec],
        out_shape=[jax.ShapeDtypeStruct((T, D), F32), jax.ShapeDtypeStruct((T, D), BF16),
                   jax.ShapeDtypeStruct((1, D), F32)],
        compiler_params=_cparams(("arbitrary",)),
        name=name,
    )(x, g_row, dh, dres)


def _loss_head(name, x, g_row, tgt):
    T, D = x.shape
    tm = min(ROW_TILE, T)

    def body(x_ref, g_ref, t_ref, dx_ref, dxb_ref, dg_ref, loss_ref):
        xv, g = x_ref[...], g_ref[...]
        r = lax.rsqrt(jnp.mean(xv * xv, axis=-1, keepdims=True) + EPS)
        xh = xv * r
        e = xh * g - t_ref[...]
        lpart = 0.5 * jnp.sum(jnp.mean(e * e, axis=-1, keepdims=True))
        dy = e * (1.0 / D)
        gd = g * dy
        c = jnp.mean(xv * gd, axis=-1, keepdims=True)
        dx = r * gd - xv * (r * r * r) * c
        dx_ref[...] = dx
        dxb_ref[...] = dx.astype(BF16)

        @pl.when(pl.program_id(0) == 0)
        def _():
            dg_ref[...] = jnp.zeros_like(dg_ref)
            loss_ref[...] = jnp.zeros_like(loss_ref)

        dg_ref[...] += jnp.sum(dy * xh, axis=0, keepdims=True)
        loss_ref[...] += jnp.full(loss_ref.shape, lpart, F32)

    row = pl.BlockSpec((tm, D), lambda i: (i, 0))
    vec = pl.BlockSpec((1, D), lambda i: (0, 0))
    return pl.pallas_call(
        body,
        grid=(T // tm,),
        in_specs=[row, vec, row],
        out_specs=[row, row, vec, pl.BlockSpec((8, 128), lambda i: (0, 0))],
        out_shape=[jax.ShapeDtypeStruct((T, D), F32), jax.ShapeDtypeStruct((T, D), BF16),
                   jax.ShapeDtypeStruct((1, D), F32), jax.ShapeDtypeStruct((8, 128), F32)],
        compiler_params=_cparams(("arbitrary",)),
        name=name,
    )(x, g_row, tgt)


def _colsum(name, a, carry=None):
    T, N = a.shape
    tm, tn = min(512, T), _div_tile(N, 2048)
    gj, gi = N // tn, T // tm
    n_cin = len(carry.ins) if carry else 0
    n_cout = len(carry.out_shapes) if carry else 0

    def body(*refs):
        a_ref, o_ref = refs[0], refs[1 + n_cin]
        j, i = pl.program_id(0), pl.program_id(1)
        if carry:
            c_in, c_out, sems = refs[1:1 + n_cin], refs[2 + n_cin:2 + n_cin + n_cout], refs[2 + n_cin + n_cout:]
            carry.ride(j * gi + i, gj * gi, c_in, c_out, sems, True)

        @pl.when(i == 0)
        def _():
            o_ref[...] = jnp.zeros_like(o_ref)

        o_ref[...] += jnp.sum(a_ref[...].astype(F32), axis=0, keepdims=True)

        if carry:
            carry.ride(j * gi + i, gj * gi, c_in, c_out, sems, False)

    res = pl.pallas_call(
        body,
        grid=(gj, gi),
        in_specs=[pl.BlockSpec((tm, tn), lambda j, i: (i, j))] + [ANY] * n_cin,
        out_specs=[pl.BlockSpec((1, tn), lambda j, i: (0, j))] + [ANY] * n_cout,
        out_shape=[jax.ShapeDtypeStruct((1, N), F32)] + (carry.out_shapes if carry else []),
        scratch_shapes=carry.scratch() if carry else [],
        compiler_params=_cparams(("arbitrary", "arbitrary") if carry else ("parallel", "arbitrary")),
        name=name,
    )(a, *(carry.ins if carry else []))
    return (res[0], res[1:]) if carry else res[0]


def _tile_scan(a, b, row, reverse):
    for s in (1, 2, 4):
        if reverse:
            a_s, b_s, m = pltpu.roll(a, 8 - s, 0), pltpu.roll(b, 8 - s, 0), row < 8 - s
        else:
            a_s, b_s, m = pltpu.roll(a, s, 0), pltpu.roll(b, s, 0), row >= s
        b = jnp.where(m, a * b_s + b, b)
        a = jnp.where(m, a * a_s, a)
    return a, b


def _chunk_scan(a_s, b_s, out_ref, carry, n_tiles, width, reverse):
    row = lax.broadcasted_iota(jnp.int32, (8, width), 0)
    edge = 0 if reverse else 7

    group = 4 if n_tiles % 4 == 0 else 1

    def step(j, c):
        jj = (n_tiles // group - 1 - j) if reverse else j
        base = pl.multiple_of(jj * (8 * group), 8 * group)
        order = range(group - 1, -1, -1) if reverse else range(group)
        parts = {t: _tile_scan(a_s[pl.ds(base + 8 * t, 8), :], b_s[pl.ds(base + 8 * t, 8), :], row, reverse)
                 for t in order}
        for t in order:
            h = parts[t][0] * c + parts[t][1]
            out_ref[pl.ds(base + 8 * t, 8), :] = h
            c = jnp.broadcast_to(h[edge:edge + 1, :], (8, width))
        return c

    carry[...] = lax.fori_loop(0, n_tiles // group, step, carry[...])


def _mixer_specs(Tc, bw, nb, layer):
    def seg(s):
        return pl.BlockSpec((Tc, bw), lambda n, i: (i, s * nb + n))

    p_spec = pl.BlockSpec((None, R_ROWS, bw), lambda n, i: (layer, 0, n))
    w_spec = pl.BlockSpec((None, None, bw, bw), lambda n, i: (layer, n, 0, 0))
    return seg, p_spec, w_spec


def _mixer_fwd(name, layer, z, pch, wr, wi, comm=None):
    T, D = z.shape[0], z.shape[1] // 7
    bw, nb = wr.shape[-1], wr.shape[1]
    Tc = min(SEQ_CHUNK, T)
    nT = T // Tc
    n_cin = len(comm.ins) if comm else 0
    n_cout = len(comm.out_shapes) if comm else 0

    def body(*refs):
        xa_ref, ya_ref, cb_ref, cc_ref, cx_ref, p_ref, wr_ref, wi_ref = refs[:8]
        pa_ref, pb_ref, xc_ref, hl_ref, vb_ref = refs[8 + n_cin:13 + n_cin]
        xa_buf, u_buf, a_s, b_s, carry = refs[13 + n_cin + n_cout:18 + n_cin + n_cout]
        if comm:
            c_in, c_out, sems = refs[8:8 + n_cin], refs[13 + n_cin:13 + n_cin + n_cout], refs[18 + n_cin + n_cout:]
            step = pl.program_id(0) * nT + pl.program_id(1)
            comm.ride(step, nb * nT, c_in, c_out, sems, True)

        @pl.when(pl.program_id(1) == 0)
        def _():
            xa_buf[0:HALO, :] = jnp.zeros((HALO, bw), F32)
            u_buf[0:HALO, :] = jnp.zeros((HALO, bw), F32)
            carry[...] = jnp.zeros_like(carry)

        rb = min(MIXER_ROW_BLOCK, Tc)
        blocks = [(g * rb, slice(g * rb, (g + 1) * rb)) for g in range(Tc // rb)]

        def prow(k):
            return p_ref[k:k + 1, :]

        xa_buf[HALO:HALO + Tc, :] = xa_ref[...]
        u_buf[HALO:HALO + Tc, :] = cc_ref[...] * cx_ref[...]
        for r0, rs in blocks:
            xc = prow(R_CAB)
            for k in range(4):
                xc = xc + prow(R_CAW + k) * xa_buf[HALO - 3 + k + r0:HALO - 3 + k + r0 + rb, :]
            xc_ref[rs, :] = xc
        xcb = xc_ref[...].astype(BF16)
        a_s[...] = jnp.dot(xcb, wr_ref[...], preferred_element_type=F32)
        b_s[...] = jnp.dot(xcb, wi_ref[...], preferred_element_type=F32)
        sp = _softplus_neg(prow(R_LAM))
        for _, rs in blocks:
            r = _sigmoid(a_s[rs, :] + prow(R_BR))
            ig = _sigmoid(b_s[rs, :] + prow(R_BI))
            log_a = (-LRU_C) * r * sp
            t = jnp.tanh(log_a)
            a_s[rs, :] = jnp.exp(log_a)
            b_s[rs, :] = jnp.sqrt(-2.0 * t / (1.0 - t)) * (ig * xc_ref[rs, :])
        _chunk_scan(a_s, b_s, hl_ref, carry, Tc // 8, bw, False)
        for r0, rs in blocks:
            g, _ = _gelu_and_grad(ya_ref[rs, :])
            pa_ref[rs, :] = (hl_ref[rs, :] * g).astype(BF16)
            vb = jnp.zeros((rb, bw), F32)
            for k in range(3):
                vb = vb + prow(R_CBW + k) * u_buf[HALO - 2 + k + r0:HALO - 2 + k + r0 + rb, :]
            vb_ref[rs, :] = vb
            pb_ref[rs, :] = (cb_ref[rs, :] * vb).astype(BF16)
        xa_buf[0:HALO, :] = xa_buf[Tc:Tc + HALO, :]
        u_buf[0:HALO, :] = u_buf[Tc:Tc + HALO, :]

        if comm:
            comm.ride(step, nb * nT, c_in, c_out, sems, False)

    seg, p_spec, w_spec = _mixer_specs(Tc, bw, nb, layer)
    out = pl.BlockSpec((Tc, bw), lambda n, i: (i, n))
    res = pl.pallas_call(
        body,
        grid=(nb, nT),
        in_specs=[seg(0), seg(1), seg(2), seg(3), seg(4), p_spec, w_spec, w_spec] + [ANY] * n_cin,
        out_specs=[out] * 5 + [ANY] * n_cout,
        out_shape=[jax.ShapeDtypeStruct((T, D), BF16), jax.ShapeDtypeStruct((T, D), BF16),
                   jax.ShapeDtypeStruct((T, D), F32), jax.ShapeDtypeStruct((T, D), F32),
                   jax.ShapeDtypeStruct((T, D), F32)] + (comm.out_shapes if comm else []),
        scratch_shapes=[pltpu.VMEM((Tc + HALO, bw), F32), pltpu.VMEM((Tc + HALO, bw), F32),
                        pltpu.VMEM((Tc, bw), F32), pltpu.VMEM((Tc, bw), F32), pltpu.VMEM((8, bw), F32)]
        + (comm.scratch() if comm else []),
        input_output_aliases={8 + ci: 5 + co for ci, co in comm.aliases.items()} if comm else {},
        compiler_params=_cparams(("arbitrary", "arbitrary") if comm else ("parallel", "arbitrary")),
        name=name,
    )(z, z, z, z, z, pch, wr, wi, *(comm.ins if comm else []))
    return (res[:5], res[5:]) if comm else res


def _mixer_bwd(name, layer, z, xc, hl, vb, dpa, dpb, dga, dgb, pch, wr, wi, comm=None):
    T, D = z.shape[0], z.shape[1] // 7
    bw, nb = wr.shape[-1], wr.shape[1]
    Tc = min(SEQ_CHUNK, T)
    nT = T // Tc
    tpc = Tc // 8
    rb = min(MIXER_ROW_BLOCK, Tc)
    n_cin = len(comm.ins) if comm else 0
    n_cout = len(comm.out_shapes) if comm else 0
    total = nb * nT

    def body(*refs):
        (xa_ref, ya_ref, cb_ref, cc_ref, cx_ref, xc_ref, hl_ref, hp_ref, vb_ref, dpa_ref, dpb_ref, dga_ref, dgb_ref,
         p_ref, wr_ref, wi_ref) = refs[:16]
        dz_ref, dwr_ref, dwi_ref, sm_ref = refs[16 + n_cin:20 + n_cin]
        (h_buf, a_buf, dxc_buf, dvb_buf, a_s, d_s, lam_s, r_s, i_s, m_s, dpr_s, dpi_s, sm8,
         carry, stage, out_sems) = refs[20 + n_cin + n_cout:36 + n_cin + n_cout]
        i = pl.program_id(1)
        step = pl.program_id(0) * nT + i
        if comm:
            c_in, c_out = refs[16:16 + n_cin], refs[20 + n_cin:20 + n_cin + n_cout]
            sems = refs[36 + n_cin + n_cout:]
            comm.ride(step, total, c_in, c_out, sems, True)

        slot = step % 2

        def out_copies(sl):
            rows = pl.ds(pl.multiple_of((nT - 1 - i) * Tc, Tc), Tc)
            return [pltpu.make_async_copy(
                stage.at[sl, s], dz_ref.at[rows, pl.ds(pl.multiple_of((s * nb + pl.program_id(0)) * bw, bw), bw)],
                out_sems.at[sl, s]) for s in range(7)]

        @pl.when(step >= 2)
        def _():
            for cp in out_copies(slot):
                cp.wait()

        dxa_ref, dya_ref, dcb_ref, dcc_ref, dcx_ref = [stage.at[slot, s] for s in range(5)]
        stage[slot, 5, :, :] = dga_ref[...]
        stage[slot, 6, :, :] = dgb_ref[...]

        @pl.when(i == 0)
        def _():
            a_buf[Tc:Tc + HALO, :] = jnp.zeros((HALO, bw), F32)
            dxc_buf[Tc:Tc + HALO, :] = jnp.zeros((HALO, bw), F32)
            dvb_buf[Tc:Tc + HALO, :] = jnp.zeros((HALO, bw), F32)
            carry[...] = jnp.zeros_like(carry)
            dwr_ref[...] = jnp.zeros_like(dwr_ref)
            dwi_ref[...] = jnp.zeros_like(dwi_ref)
            sm8[...] = jnp.zeros_like(sm8)

        blocks = [(g * rb, slice(g * rb, (g + 1) * rb)) for g in range(Tc // rb)]

        def prow(k):
            return p_ref[k:k + 1, :]

        def part8(v):
            return jnp.sum(v.reshape(rb // 8, 8, bw), axis=0)

        sums8 = {}

        def tally(k, v):
            sums8[k] = sums8[k] + part8(v) if k in sums8 else part8(v)

        sp = _softplus_neg(prow(R_LAM))
        xcb = xc_ref[...].astype(BF16)
        r_s[...] = jnp.dot(xcb, wr_ref[...], preferred_element_type=F32)
        i_s[...] = jnp.dot(xcb, wi_ref[...], preferred_element_type=F32)
        for _, rs in blocks:
            r = _sigmoid(r_s[rs, :] + prow(R_BR))
            ig = _sigmoid(i_s[rs, :] + prow(R_BI))
            log_a = (-LRU_C) * r * sp
            t = jnp.tanh(log_a)
            r_s[rs, :] = r
            i_s[rs, :] = ig
            m_s[rs, :] = jnp.sqrt(-2.0 * t / (1.0 - t))
            a_buf[rs, :] = jnp.exp(log_a)
            g, gp = _gelu_and_grad(ya_ref[rs, :])
            dpav = dpa_ref[rs, :]
            d_s[rs, :] = dpav * g
            dya_ref[rs, :] = (dpav * hl_ref[rs, :] * gp).astype(BF16)
            dpbv = dpb_ref[rs, :]
            dcb_ref[rs, :] = (dpbv * vb_ref[rs, :]).astype(BF16)
            dvb_buf[rs, :] = dpbv * cb_ref[rs, :]

        a_s[...] = a_buf[1:Tc + 1, :]
        _chunk_scan(a_s, d_s, lam_s, carry, tpc, bw, True)
        h_buf[HALO:HALO + Tc, :] = hl_ref[...]
        h_buf[0:HALO, :] = jnp.where(i == nT - 1, 0.0, hp_ref[...])

        for r0, rs in blocks:
            lamv, xcv, r, ig, a = lam_s[rs, :], xc_ref[rs, :], r_s[rs, :], i_s[rs, :], a_buf[rs, :]
            mult = m_s[rs, :]
            da = lamv * h_buf[HALO - 1 + r0:HALO - 1 + r0 + rb, :]
            dmult = lamv * (ig * xcv)
            dbx = lamv * mult
            dig = dbx * xcv
            dxc_buf[rs, :] = dbx * ig
            dlog_a = da * a - dmult * (a * a) / mult
            dpr = (dlog_a * ((-LRU_C) * sp)) * r * (1.0 - r)
            dpi = dig * ig * (1.0 - ig)
            dpr_s[rs, :] = dpr.astype(BF16)
            dpi_s[rs, :] = dpi.astype(BF16)
            tally(R_BR, dpr)
            tally(R_BI, dpi)
            tally(R_LAM, dlog_a * ((-LRU_C) * r))

        dprb, dpib = dpr_s[...], dpi_s[...]
        nt = (((1,), (1,)), ((), ()))
        tn = (((0,), (0,)), ((), ()))
        dxc_buf[0:Tc, :] += (lax.dot_general(dprb, wr_ref[...], nt, preferred_element_type=F32)
                             + lax.dot_general(dpib, wi_ref[...], nt, preferred_element_type=F32))
        dwr_ref[...] += lax.dot_general(xcb, dprb, tn, preferred_element_type=F32)
        dwi_ref[...] += lax.dot_general(xcb, dpib, tn, preferred_element_type=F32)

        for r0, rs in blocks:
            xav = xa_ref[rs, :]
            tally(R_CAB, dxc_buf[rs, :])
            dxa = jnp.zeros((rb, bw), F32)
            for k in range(4):
                sh = dxc_buf[3 - k + r0:3 - k + r0 + rb, :]
                dxa = dxa + prow(R_CAW + k) * sh
                tally(R_CAW + k, xav * sh)
            dxa_ref[rs, :] = dxa.astype(BF16)
            ccv, cxv = cc_ref[rs, :], cx_ref[rs, :]
            u = ccv * cxv
            du = jnp.zeros((rb, bw), F32)
            for k in range(3):
                sh = dvb_buf[2 - k + r0:2 - k + r0 + rb, :]
                du = du + prow(R_CBW + k) * sh
                tally(R_CBW + k, u * sh)
            dcc_ref[rs, :] = (du * cxv).astype(BF16)
            dcx_ref[rs, :] = (du * ccv).astype(BF16)

        for k, v in sums8.items():
            sm8[8 * k:8 * k + 8, :] += v
        a_buf[Tc:Tc + HALO, :] = a_buf[0:HALO, :]
        dxc_buf[Tc:Tc + HALO, :] = dxc_buf[0:HALO, :]
        dvb_buf[Tc:Tc + HALO, :] = dvb_buf[0:HALO, :]

        @pl.when(i == nT - 1)
        def _():
            sm_ref[...] = jnp.sum(sm8[...].reshape(R_ROWS, 8, bw), axis=1)
            sm_ref[R_LAM:R_LAM + 1, :] = sm_ref[R_LAM:R_LAM + 1, :] * (-_sigmoid(-prow(R_LAM)))

        for cp in out_copies(slot):
            cp.start()

        @pl.when(step == total - 1)
        def _():
            if total > 1:
                for cp in out_copies(1 - slot):
                    cp.wait()
            for cp in out_copies(slot):
                cp.wait()

        if comm:
            comm.ride(step, total, c_in, c_out, sems, False)

    def seg(s):
        return pl.BlockSpec((Tc, bw), lambda n, i: (nT - 1 - i, s * nb + n))

    blk = pl.BlockSpec((Tc, bw), lambda n, i: (nT - 1 - i, n))
    halo = pl.BlockSpec((8, bw), lambda n, i: (jnp.maximum((nT - 1 - i) * tpc - 1, 0), n))
    p_spec = pl.BlockSpec((None, R_ROWS, bw), lambda n, i: (layer, 0, n))
    w_spec = pl.BlockSpec((None, None, bw, bw), lambda n, i: (layer, n, 0, 0))
    dw_spec = pl.BlockSpec((None, bw, bw), lambda n, i: (n, 0, 0))
    res = pl.pallas_call(
        body,
        grid=(nb, nT),
        in_specs=[seg(0), seg(1), seg(2), seg(3), seg(4), blk, blk, halo, blk, blk, blk, blk, blk,
                  p_spec, w_spec, w_spec] + [ANY] * n_cin,
        out_specs=[ANY, dw_spec, dw_spec, pl.BlockSpec((R_ROWS, bw), lambda n, i: (0, n))] + [ANY] * n_cout,
        out_shape=[jax.ShapeDtypeStruct((T, 7 * D), BF16), jax.ShapeDtypeStruct((nb, bw, bw), F32),
                   jax.ShapeDtypeStruct((nb, bw, bw), F32), jax.ShapeDtypeStruct((R_ROWS, D), F32)]
        + (comm.out_shapes if comm else []),
        scratch_shapes=[pltpu.VMEM((Tc + HALO, bw), F32)] * 4 + [pltpu.VMEM((Tc, bw), F32)] * 6
        + [pltpu.VMEM((Tc, bw), BF16)] * 2 + [pltpu.VMEM((8 * R_ROWS, bw), F32), pltpu.VMEM((8, bw), F32),
                                              pltpu.VMEM((2, 7, Tc, bw), BF16), pltpu.SemaphoreType.DMA((2, 7))]
        + (comm.scratch() if comm else []),
        input_output_aliases={16 + ci: 4 + co for ci, co in comm.aliases.items()} if comm else {},
        compiler_params=_cparams(("arbitrary", "arbitrary")),
        name=name,
    )(z, z, z, z, z, xc, hl, hl, vb, dpa, dpb, dga, dgb, pch, wr, wi, *(comm.ins if comm else []))
    return (res[:4], res[4:]) if comm else res


def _local_fwd_bwd(x, tgt, W, placed=None):
    T, D = x.shape
    g1, g2 = W["g1"], W["g2"]
    depth = g1.shape[0]
    FF = 4 * D
    if placed is None:
        mats = {(n, l): W[n][l] for n in MATS for l in range(depth)}
    else:
        mats = {}
        mats["w_in", 0], = _run_carry("gather_first", _gather_carry([placed["w_in", 0]], [(0, 1, 0, 1)]))

    def gathering(specs):
        if placed is None or not specs:
            return None, []
        keys = [(n, l) for n, l, _, _, _ in specs]
        arrays = [mats.get(k, placed[k]) for k in keys]
        return _gather_carry(arrays, [(i, d, part, nparts) for i, (_, _, d, part, nparts) in enumerate(specs)]), keys

    def hosted(call, specs, **kw):
        carry, keys = gathering(specs)
        if carry is None:
            return call(**kw)
        res, got = call(**kw, **{("comm" if call.func is _mixer_fwd else "carry"): carry})
        mats.update(zip(keys, got))
        return res

    saved = []
    xs = x
    for l in range(depth):
        h = _rms_fwd(f"rms1_fwd_{l}", xs, g1[l][None])
        nxt = l + 1 < depth
        projs = [("w_pa", l, 0, 0, 1), ("w_pb", l, 0, 0, 1), ("w_o", l, 0, 0, 1)]
        z = hosted(functools.partial(_mm, f"in_proj_{l}", "nn", h, mats["w_in", l], T, 7 * D, D, [F32]),
                   projs + ([("w_mlp1", l, 1, 0, 1)] if nxt else []),
                   epilogue=lambda acc, b: (acc + b,), extras=[(W["b_in"][l][None], "bias", 0)])
        pa, pb, xc, hl, vb = hosted(
            functools.partial(_mixer_fwd, f"mixer_fwd_{l}", l, z, W["pch"], W["wr"], W["wi"]),
            [("w_mlp2", l, 0, 0, 1)] if nxt else [("w_mlp1", l, 1, 0, 1)])
        oa = _mm(f"proj_a_{l}", "nn", pa, mats["w_pa", l], T, D, D, [F32])

        def merge(acc, oav, ga, gb):
            return acc, _sigmoid(ga) * oav + _sigmoid(gb) * acc

        ob, mg = _mm(f"proj_b_merge_{l}", "nn", pb, mats["w_pb", l], T, D, D, [F32, BF16], epilogue=merge,
                     tiles=MM_TILES_FUSED,
                     extras=[(oa, "tile", 0), (z, "tile", 5 * D), (z, "tile", 6 * D)])
        x1 = _mm(f"out_proj_{l}", "nn", mg, mats["w_o", l], T, D, D, [F32],
                 epilogue=lambda acc, res: (res + acc,), extras=[(xs, "tile", 0)])
        h2 = _rms_fwd(f"rms2_fwd_{l}", x1, g2[l][None])

        def relu2(acc):
            pr = jnp.maximum(acc, 0.0)
            return pr * pr, pr

        u, pr = hosted(functools.partial(_mm, f"mlp1_{l}", "nn", h2, mats["w_mlp1", l], T, FF, D, [BF16, BF16]),
                       [("w_in", l + 1, 1, 0, 2)] if nxt else [("w_mlp2", l, 0, 0, 1)], epilogue=relu2)
        x2 = hosted(functools.partial(_mm, f"mlp2_{l}", "nn", u, mats["w_mlp2", l], T, D, FF, [F32]),
                    [("w_in", l + 1, 1, 1, 2)] if nxt else [],
                    epilogue=lambda acc, res: (res + acc,), extras=[(x1, "tile", 0)])
        saved.append(dict(x0=xs, h=h, z=z, pa=pa, pb=pb, xc=xc, hl=hl, vb=vb, oa=oa, ob=ob, mg=mg, x1=x1,
                          h2=h2, u=u, pr=pr))
        xs = x2

    dx, dxb, dgf, loss_blk = _loss_head("loss_head", xs, W["gf"][None], tgt)

    gmat, got, sums, landed, acc = {}, {}, {}, {}, {}

    def reducing(call, swaps=(), scatters=(), part=None, join=None, mixer=False, **kw):
        swaps, scatters = [(n, l) for n in swaps], [(n, l) for n in scatters]
        carries, sinks = [], []
        if placed is not None and swaps:
            carries.append(_swap_carry([gmat[k] for k in swaps], [BIG_DIM[k[0]] - 1 for k in swaps]))
            sinks.append((got, swaps))
        if placed is not None and scatters:
            carries.append(_scatter_carry([sums[k] for k in scatters]))
            sinks.append((landed, scatters))
        if placed is not None and part is not None:
            key = part[:2]
            carries.append(_scatter_carry([sums[key]], part[2], part[3], [landed[key]] if key in landed else None))
            sinks.append((landed, [key]))
        if placed is not None and join is not None:
            carries.append(_join_carry([acc[n] for n in MATS], [BIG_DIM[n] for n in MATS], join))
            sinks.append((acc, list(MATS)))
        if not carries:
            return call(**kw)
        res, moved = call(**kw, **{("comm" if mixer else "carry"): _merge_carries(carries)})
        moved = list(moved)
        for target, keys in sinks:
            for k in keys:
                target[k] = moved.pop(0)
        return res

    def add(names):
        if placed is not None:
            for n in names:
                sums[n, l] = _add_halves(f"add_halves_{n}_{l}", gmat[n, l], got[n, l], BIG_DIM[n] - 1, W["core"])

    grads = [None] * depth
    for l in reversed(range(depth)):
        s = saved[l]
        dp = _mm(f"mlp2_dx_{l}", "nt", dxb, mats["w_mlp2", l], T, FF, D, [BF16],
                 epilogue=lambda acc, prv: (2.0 * prv.astype(F32) * acc,), extras=[(s["pr"], "tile", 0)])
        dw2 = gmat["w_mlp2", l] = reducing(functools.partial(_mm, f"mlp2_dw_{l}", "tn", s["u"], dxb, FF, D, T, [BF16]),
                                           part=("w_in", l + 1, 1, 2) if l + 1 < depth else None)
        above = l + 1 if placed is not None and l + 1 < depth else None
        if above is not None:
            for n in MATS:
                acc[n] = _reduce_into(f"reduce_{n}_{above}", sums[n, above], landed[n, above], acc.get(n), above,
                                      BIG_DIM[n], W["shard_shapes"][n], W["where"])
        dh2 = reducing(functools.partial(_mm, f"mlp1_dx_{l}", "nt", dp, mats["w_mlp1", l], T, D, FF, [F32]),
                       swaps=["w_mlp2"])
        add(["w_mlp2"])
        dw1 = gmat["w_mlp1", l] = reducing(functools.partial(_mm, f"mlp1_dw_{l}", "tn", s["h2"], dp, D, FF, T, [BF16]),
                                           part=("w_mlp2", l, 0, 2))
        dx1, dx1b, dg2 = _rms_bwd(f"rms2_bwd_{l}", s["x1"], g2[l][None], dh2, dx)

        def unmerge(acc, ga, gb, oav, obv):
            sa, sb = _sigmoid(ga), _sigmoid(gb)
            return acc * sa, acc * sb, acc * oav * sa * (1.0 - sa), acc * obv * sb * (1.0 - sb)

        doa, dob, dga, dgb = reducing(
            functools.partial(_mm, f"out_proj_dx_{l}", "nt", dx1b, mats["w_o", l], T, D, D, [BF16] * 4),
            swaps=["w_mlp1"], part=("w_mlp2", l, 1, 2), join=above, tiles=MM_TILES_FUSED, epilogue=unmerge,
            extras=[(s["z"], "tile", 5 * D), (s["z"], "tile", 6 * D), (s["oa"], "tile", 0), (s["ob"], "tile", 0)])
        add(["w_mlp1"])
        dwo = gmat["w_o", l] = _mm(f"out_proj_dw_{l}", "tn", s["mg"], dx1b, D, D, T, [BF16])
        dpa = _mm(f"proj_a_dx_{l}", "nt", doa, mats["w_pa", l], T, D, D, [F32])
        dwpa = gmat["w_pa", l] = _mm(f"proj_a_dw_{l}", "tn", s["pa"], doa, D, D, T, [BF16])
        dpb = _mm(f"proj_b_dx_{l}", "nt", dob, mats["w_pb", l], T, D, D, [F32])
        dwpb = gmat["w_pb", l] = _mm(f"proj_b_dw_{l}", "tn", s["pb"], dob, D, D, T, [BF16])
        dz, dwr, dwi, sm = reducing(
            functools.partial(_mixer_bwd, f"mixer_bwd_{l}", l, s["z"], s["xc"], s["hl"], s["vb"], dpa, dpb, dga, dgb,
                              W["pch"], W["wr"], W["wi"]),
            swaps=["w_o", "w_pa", "w_pb"], scatters=["w_mlp1"], mixer=True)
        add(["w_o", "w_pa", "w_pb"])
        dwin = gmat["w_in", l] = reducing(
            functools.partial(_mm, f"in_proj_dw_{l}", "tn", s["h"], dz, D, 7 * D, T, [BF16]),
            scatters=["w_o", "w_pa", "w_pb"])
        dbin = reducing(functools.partial(_colsum, f"bias_grad_{l}", dz), swaps=["w_in"])
        add(["w_in"])
        dh = reducing(functools.partial(_mm, f"in_proj_dx_{l}", "nt", dz, mats["w_in", l], T, D, 7 * D, [F32]),
                      scatters=["w_in"] if l == 0 else [], part=("w_in", l, 0, 2) if l > 0 else None)
        dx, dxb, dg1 = _rms_bwd(f"rms1_bwd_{l}", s["x0"], g1[l][None], dh, dx1)
        grads[l] = dict(w_in=dwin, w_pa=dwpa, w_pb=dwpb, w_o=dwo, w_mlp1=dw1, w_mlp2=dw2, wr=dwr, wi=dwi,
                        sm=sm, b_in=dbin, g1=dg1, g2=dg2)
    return loss_blk[0, 0], dx, grads, dgf, sums, landed, acc


ANY = pl.BlockSpec(memory_space=pl.ANY)


def _place():
    x, y, c = lax.axis_index("x"), lax.axis_index("y"), lax.axis_index("c")
    peers = [(1 - x, y, c), (x, 1 - y, c), (1 - x, 1 - y, c)]
    chips = [2 * (1 - x) + y, 2 * x + (1 - y), 2 * (1 - x) + (1 - y)]
    return (x, y, c), 2 * x + y, peers, chips


def _window(ref, dim, q, size):
    idx = [slice(None)] * len(ref.shape)
    idx[dim] = pl.ds(q * size, size)
    return ref.at[tuple(idx)]


def _gather_weights(shards, dims, small):
    n = len(shards)
    sizes = [s.shape[d] for s, d in zip(shards, dims)]
    full = [jax.ShapeDtypeStruct(s.shape[:d] + (s.shape[d] * N_CHIPS,) + s.shape[d + 1:], s.dtype)
            for s, d in zip(shards, dims)]
    full.append(jax.ShapeDtypeStruct((N_CHIPS,) + small.shape, small.dtype))

    def body(*refs):
        ins, outs = refs[:n + 1], refs[n + 1:2 * n + 2]
        send_sems, recv_sems, local_sems = refs[2 * n + 2:]
        _, k, peers, chips = _place()

        def dst(w, q):
            return outs[w].at[q] if w == n else _window(outs[w], dims[w], q, sizes[w])

        local = [pltpu.make_async_copy(ins[w], dst(w, k), local_sems.at[w]) for w in range(n + 1)]
        for cp in local:
            cp.start()
        sends = []
        for p, peer in enumerate(peers):
            for w in range(n + 1):
                s = p * (n + 1) + w
                sends.append(pltpu.make_async_remote_copy(
                    src_ref=ins[w], dst_ref=dst(w, k), send_sem=send_sems.at[s], recv_sem=recv_sems.at[s],
                    device_id=peer, device_id_type=MESH))
        for cp in sends:
            cp.start()
        for p, peer in enumerate(peers):
            for w in range(n + 1):
                s = p * (n + 1) + w
                pltpu.make_async_remote_copy(
                    src_ref=ins[w], dst_ref=dst(w, chips[p]), send_sem=send_sems.at[s], recv_sem=recv_sems.at[s],
                    device_id=peer, device_id_type=MESH).wait_recv()
        for cp in sends:
            cp.wait_send()
        for cp in local:
            cp.wait()

    return pl.pallas_call(
        body,
        in_specs=[ANY] * (n + 1),
        out_specs=[ANY] * (n + 1),
        out_shape=full,
        scratch_shapes=[pltpu.SemaphoreType.DMA((3 * (n + 1),)), pltpu.SemaphoreType.DMA((3 * (n + 1),)),
                        pltpu.SemaphoreType.DMA((n + 1,))],
        name="gather_weights",
    )(*shards, small)


def _scatter_grads(grads, dims):
    n, depth = len(grads), len(grads[0])
    sizes = [g[0].shape[d] // N_CHIPS for g, d in zip(grads, dims)]
    land = []
    for g, d, sz in zip(grads, dims, sizes):
        shp = g[0].shape
        land.append(jax.ShapeDtypeStruct((N_CHIPS, depth) + shp[:d] + (sz,) + shp[d + 1:], g[0].dtype))

    def body(*refs):
        ins, outs = refs[:n * depth], refs[n * depth:n * depth + n]
        send_sems, recv_sems, local_sems = refs[n * depth + n:]
        _, k, peers, chips = _place()

        def src(w, l, q):
            return _window(ins[w * depth + l], dims[w], q, sizes[w])

        local = [pltpu.make_async_copy(src(w, l, k), outs[w].at[3, l], local_sems.at[w * depth + l])
                 for w in range(n) for l in range(depth)]
        for cp in local:
            cp.start()
        sends = []
        for p, peer in enumerate(peers):
            for w in range(n):
                for l in range(depth):
                    s = (p * n + w) * depth + l
                    sends.append(pltpu.make_async_remote_copy(
                        src_ref=src(w, l, chips[p]), dst_ref=outs[w].at[p, l], send_sem=send_sems.at[s],
                        recv_sem=recv_sems.at[s], device_id=peer, device_id_type=MESH))
        for cp in sends:
            cp.start()
        for cp in sends:
            cp.wait_recv()
        for cp in sends:
            cp.wait_send()
        for cp in local:
            cp.wait()

    flat = [g for gl in grads for g in gl]
    return pl.pallas_call(
        body,
        in_specs=[ANY] * (n * depth),
        out_specs=[ANY] * n,
        out_shape=land,
        scratch_shapes=[pltpu.SemaphoreType.DMA((3 * n * depth,)), pltpu.SemaphoreType.DMA((3 * n * depth,)),
                        pltpu.SemaphoreType.DMA((n * depth,))],
        name="scatter_grads",
    )(*flat)


def _sum_slots(name, land):
    _, R, C = land.shape
    tr, tc = _div_tile(R, 512, 8), _div_tile(C, 1024)

    def body(a_ref, b_ref, c_ref, d_ref, o_ref):
        o_ref[...] = ((d_ref[...].astype(F32) + a_ref[...].astype(F32)) + b_ref[...].astype(F32)) \
            + c_ref[...].astype(F32)

    def slot(q):
        return pl.BlockSpec((None, tr, tc), lambda i, j: (q, i, j))

    return pl.pallas_call(
        body,
        grid=(R // tr, C // tc),
        in_specs=[slot(0), slot(1), slot(2), slot(3)],
        out_specs=pl.BlockSpec((tr, tc), lambda i, j: (i, j)),
        out_shape=jax.ShapeDtypeStruct((R, C), F32),
        compiler_params=_cparams(("parallel", "parallel")),
        name=name,
    )(land, land, land, land)


def _swap_with_sibling(parts):
    n = len(parts)

    def body(*refs):
        ins, outs = refs[:n], refs[n:2 * n]
        send_sems, recv_sems = refs[2 * n:]
        (x, y, c), _, _, _ = _place()
        copies = [pltpu.make_async_remote_copy(
            src_ref=ins[w], dst_ref=outs[w], send_sem=send_sems.at[w], recv_sem=recv_sems.at[w],
            device_id=(x, y, 1 - c), device_id_type=MESH) for w in range(n)]
        for cp in copies:
            cp.start()
        for cp in copies:
            cp.wait()

    return pl.pallas_call(
        body,
        in_specs=[ANY] * n,
        out_specs=[ANY] * n,
        out_shape=[jax.ShapeDtypeStruct(p.shape, p.dtype) for p in parts],
        scratch_shapes=[pltpu.SemaphoreType.DMA((n,)), pltpu.SemaphoreType.DMA((n,))],
        name="swap_with_sibling",
    )(*parts)


class _Carry:
    def __init__(self, ins, out_shapes, rounds, counts, aliases=None, marks=(0.6, 0.92)):
        self.ins, self.out_shapes, self.rounds, self.counts = list(ins), list(out_shapes), list(rounds), list(counts)
        self.aliases, self.marks = dict(aliases or {}), marks

    def scratch(self):
        return [pltpu.SemaphoreType.DMA((n,)) for n in self.counts for _ in range(2)]

    def _copies(self, r, in_refs, out_refs, sems, landing):
        remote = self.rounds[r](in_refs, out_refs)
        assert len(remote) == self.counts[r], (r, len(remote), self.counts[r])
        return [pltpu.make_async_remote_copy(src_ref=s, dst_ref=(land if landing else d), send_sem=sems[2 * r].at[i],
                                             recv_sem=sems[2 * r + 1].at[i], device_id=peer, device_id_type=MESH)
                for i, (s, d, peer, land) in enumerate(remote)]

    def begin(self, r, in_refs, out_refs, sems):
        if r > 0:
            for cp in self._copies(r - 1, in_refs, out_refs, sems, True):
                cp.wait_recv()
        for cp in self._copies(r, in_refs, out_refs, sems, False):
            cp.start()

    def end(self, in_refs, out_refs, sems):
        last = len(self.rounds) - 1
        for cp in self._copies(last, in_refs, out_refs, sems, True):
            cp.wait_recv()
        for r in range(last + 1):
            for cp in self._copies(r, in_refs, out_refs, sems, False):
                cp.wait_send()

    def ride(self, step, total, in_refs, out_refs, sems, first):
        if first:
            @pl.when(step == 0)
            def _():
                self.begin(0, in_refs, out_refs, sems)
            return
        for r in range(1, len(self.rounds)):
            @pl.when(step == min(total - 1, int(total * self.marks[r - 1])))
            def _(r=r):
                self.begin(r, in_refs, out_refs, sems)

        @pl.when(step == total - 1)
        def _():
            self.end(in_refs, out_refs, sems)


def _run_carry(name, carry):
    n_in, n_out = len(carry.ins), len(carry.out_shapes)

    def body(*refs):
        in_refs, out_refs, sems = refs[:n_in], refs[n_in:n_in + n_out], refs[n_in + n_out:]
        for r in range(len(carry.rounds)):
            carry.begin(r, in_refs, out_refs, sems)
        carry.end(in_refs, out_refs, sems)

    return pl.pallas_call(
        body,
        in_specs=[ANY] * n_in,
        out_specs=[ANY] * n_out,
        out_shape=carry.out_shapes,
        scratch_shapes=carry.scratch(),
        input_output_aliases=carry.aliases,
        name=name,
    )(*carry.ins)


def _gather_carry(arrays, items):
    shapes = [a.shape for a in arrays]

    def ring():
        x, y, c = lax.axis_index("x"), lax.axis_index("y"), lax.axis_index("c")
        first = (x + (1 - c) * (1 - 2 * x), y + c * (1 - 2 * y), c)
        second = (x + c * (1 - 2 * x), y + (1 - c) * (1 - 2 * y), c)
        return c, 2 * x + y, first, second

    def chip(pos):
        return 2 * pos[0] + pos[1]

    def round0(ins, outs):
        c, k, first, second = ring()
        remote = []
        for item in items:
            win = window(outs[item[0]], item)
            remote.append((win(2 * k + c), win(2 * k + c), first, win(2 * chip(first) + c)))
            remote.append((win(2 * k + c), win(2 * k + c), second, win(2 * chip(second) + c)))
        return remote

    def round1(ins, outs):
        c, k, first, second = ring()
        remote = []
        for item in items:
            win = window(outs[item[0]], item)
            relayed = win(2 * chip(first) + c)
            remote.append((relayed, relayed, second, win(2 * (3 - k) + c)))
        return remote

    def window(ref, item):
        idx, d, part, nparts = item
        h = shapes[idx][d] // (2 * N_CHIPS)
        rows = shapes[idx][1 - d] // nparts

        def win(j):
            sl = [None, None]
            sl[d] = pl.ds(j * h, h)
            sl[1 - d] = pl.ds(part * rows, rows)
            return ref.at[tuple(sl)]

        return win

    def round2(ins, outs):
        (x, y, c), _, _, chips = _place()
        remote = []
        for item in items:
            win = window(outs[item[0]], item)
            for p in range(3):
                remote.append((win(2 * chips[p] + c), win(2 * chips[p] + c), (x, y, 1 - c),
                               win(2 * chips[p] + 1 - c)))
        return remote

    n = len(items)
    return _Carry(arrays, [jax.ShapeDtypeStruct(a.shape, a.dtype) for a in arrays], [round0, round1, round2],
                  [2 * n, n, 3 * n], aliases={i: i for i in range(len(arrays))})


def _place_shard(name, w, layer, dim, chip):
    _, a, b = w.shape
    full = (a * N_CHIPS, b) if dim == 0 else (a, b * N_CHIPS)
    tr, tc = _div_tile(a, 512, 16), _div_tile(b, 2048)
    nr, nc = a // tr, b // tc

    def out_map(i, j, chip_ref):
        return (chip_ref[0] * nr + i, j) if dim == 0 else (i, chip_ref[0] * nc + j)

    def body(chip_ref, w_ref, o_ref):
        o_ref[...] = w_ref[...].astype(o_ref.dtype)

    return pl.pallas_call(
        body,
        grid_spec=pltpu.PrefetchScalarGridSpec(
            num_scalar_prefetch=1, grid=(nr, nc),
            in_specs=[pl.BlockSpec((None, tr, tc), lambda i, j, chip_ref: (layer, i, j))],
            out_specs=pl.BlockSpec((tr, tc), out_map)),
        out_shape=jax.ShapeDtypeStruct(full, BF16),
        compiler_params=_cparams(("parallel", "parallel")),
        name=name,
    )(chip, w)


def _half_shape(shape, dim):
    return shape[:dim] + (shape[dim] // (2 * N_CHIPS),) + shape[dim + 1:]


def _swap_carry(grads, dims):
    shapes = [jax.ShapeDtypeStruct((N_CHIPS,) + _half_shape(g.shape, d), g.dtype) for g, d in zip(grads, dims)]

    def plan(ins, outs):
        (x, y, c), _, _, _ = _place()
        remote = []
        for w, d in enumerate(dims):
            h = grads[w].shape[d] // (2 * N_CHIPS)
            for q in range(N_CHIPS):
                remote.append((_window(ins[w], d, 2 * q + 1 - c, h), outs[w].at[q], (x, y, 1 - c), outs[w].at[q]))
        return remote

    return _Carry(grads, shapes, [plan], [N_CHIPS * len(grads)])


def _scatter_carry(sums, part=0, nparts=1, land=None):
    n = len(sums)

    def plan(ins, outs):
        _, _, peers, chips = _place()
        remote = []
        for w in range(n):
            r = sums[w].shape[1] // nparts
            rows = pl.ds(part * r, r)
            for p in range(3):
                remote.append((ins[w].at[chips[p], rows], outs[w].at[p, rows], peers[p], outs[w].at[p, rows]))
        return remote

    shapes = [jax.ShapeDtypeStruct((3,) + s.shape[1:], s.dtype) for s in sums]
    if land is None:
        return _Carry(sums, shapes, [plan], [3 * n])
    return _Carry(list(sums) + list(land), shapes, [plan], [3 * n], aliases={n + w: w for w in range(n)})


def _merge_carries(carries):
    carries = [c for c in carries if c is not None]
    if len(carries) <= 1:
        return carries[0] if carries else None
    ins = [a for c in carries for a in c.ins]
    outs = [s for c in carries for s in c.out_shapes]

    def plan(in_refs, out_refs):
        remote, i0, o0 = [], 0, 0
        for c in carries:
            remote += c.rounds[0](in_refs[i0:i0 + len(c.ins)], out_refs[o0:o0 + len(c.out_shapes)])
            i0, o0 = i0 + len(c.ins), o0 + len(c.out_shapes)
        return remote

    assert all(len(c.rounds) == 1 for c in carries)
    aliases, i0, o0 = {}, 0, 0
    for c in carries:
        aliases.update({i0 + ci: o0 + co for ci, co in c.aliases.items()})
        i0, o0 = i0 + len(c.ins), o0 + len(c.out_shapes)
    return _Carry(ins, outs, [plan], [sum(c.counts[0] for c in carries)], aliases=aliases)


def _add_halves(name, g, got, dim, core):
    R, C = g.shape
    if dim == 1:
        r, cc = R, C // (2 * N_CHIPS)
    else:
        r, cc = R // (2 * N_CHIPS), C
    tr, tc = _div_tile(r, 512, 16), _div_tile(cc, 1024)
    nr, nc = r // tr, cc // tc

    def g_map(q, i, j, core_ref):
        w = 2 * q + core_ref[0]
        return (i, w * nc + j) if dim == 1 else (w * nr + i, j)

    def body(core_ref, g_ref, got_ref, o_ref):
        o_ref[...] = (g_ref[...].astype(F32) + got_ref[...].astype(F32)).astype(o_ref.dtype)

    slab = pl.BlockSpec((None, tr, tc), lambda q, i, j, core_ref: (q, i, j))
    return pl.pallas_call(
        body,
        grid_spec=pltpu.PrefetchScalarGridSpec(
            num_scalar_prefetch=1, grid=(N_CHIPS, nr, nc),
            in_specs=[pl.BlockSpec((tr, tc), g_map), slab], out_specs=slab),
        out_shape=jax.ShapeDtypeStruct((N_CHIPS, r, cc), g.dtype),
        compiler_params=_cparams(("parallel", "parallel", "parallel")),
        name=name,
    )(core, g, got)


def _reduce_into(name, sums, land, acc, layer, dim, shape, where):
    _, r, cc = sums.shape
    tr, tc = _div_tile(r, 512, 16), _div_tile(cc, 1024)
    nr, nc = r // tr, cc // tc

    def out_map(i, j, s):
        return (layer, s[1] * nr + i, j) if dim == 1 else (layer, i, s[1] * nc + j)

    def body(*refs):
        own, a_ref, b_ref, c_ref, o_ref = refs[1], refs[2], refs[3], refs[4], refs[-1]
        o_ref[...] = ((own[...].astype(F32) + a_ref[...].astype(F32)) + b_ref[...].astype(F32)) \
            + c_ref[...].astype(F32)

    def slot(p):
        return pl.BlockSpec((None, tr, tc), lambda i, j, s: (p, i, j))

    in_specs = [pl.BlockSpec((None, tr, tc), lambda i, j, s: (s[0], i, j)), slot(0), slot(1), slot(2)]
    args = [where, sums, land, land, land]
    if acc is not None:
        in_specs.append(ANY)
        args.append(acc)
    return pl.pallas_call(
        body,
        grid_spec=pltpu.PrefetchScalarGridSpec(
            num_scalar_prefetch=1, grid=(nr, nc), in_specs=in_specs,
            out_specs=pl.BlockSpec((None, tr, tc), out_map)),
        out_shape=jax.ShapeDtypeStruct(shape, F32),
        input_output_aliases={5: 0} if acc is not None else {},
        compiler_params=_cparams(("parallel", "parallel")),
        name=name,
    )(*args)


def _join_carry(grads, dims, layer):
    n = len(grads)

    def plan(ins, outs):
        (x, y, c), _, _, _ = _place()
        remote = []
        for w in range(n):
            d, h = dims[w], grads[w].shape[dims[w]] // 2

            def win(half, w=w, d=d, h=h):
                idx = [slice(None)] * len(grads[w].shape)
                idx[0], idx[d] = layer, pl.ds(half * h, h)
                return outs[w].at[tuple(idx)]

            remote.append((win(c), win(c), (x, y, 1 - c), win(1 - c)))
        return remote

    return _Carry(grads, [jax.ShapeDtypeStruct(g.shape, g.dtype) for g in grads], [plan], [n],
                  aliases={w: w for w in range(n)})


def _allreduce_small(pack):
    R, C = pack.shape

    def gather_body(in_ref, slots_ref, send_sems, recv_sems, local_sem):
        x, y, c = lax.axis_index("x"), lax.axis_index("y"), lax.axis_index("c")
        me = 4 * x + 2 * y + c
        flips = [(dx, dy, dc) for dx in (0, 1) for dy in (0, 1) for dc in (0, 1)][1:]

        def flip(v, d):
            return 1 - v if d else v

        local = pltpu.make_async_copy(in_ref, slots_ref.at[me], local_sem)
        local.start()
        sends = []
        for j, (dx, dy, dc) in enumerate(flips):
            px, py, pc = flip(x, dx), flip(y, dy), flip(c, dc)
            sends.append((pltpu.make_async_remote_copy(
                src_ref=in_ref, dst_ref=slots_ref.at[me], send_sem=send_sems.at[j], recv_sem=recv_sems.at[j],
                device_id=(px, py, pc), device_id_type=MESH), 4 * px + 2 * py + pc, j))
        for cp, _, _ in sends:
            cp.start()
        for cp, peer_id, j in sends:
            pltpu.make_async_remote_copy(
                src_ref=in_ref, dst_ref=slots_ref.at[peer_id], send_sem=send_sems.at[j], recv_sem=recv_sems.at[j],
                device_id=(x, y, c), device_id_type=MESH).wait_recv()
        for cp, _, _ in sends:
            cp.wait_send()
        local.wait()

    slots = pl.pallas_call(
        gather_body,
        in_specs=[ANY],
        out_specs=ANY,
        out_shape=jax.ShapeDtypeStruct((N_DEV, R, C), pack.dtype),
        scratch_shapes=[pltpu.SemaphoreType.DMA((N_DEV - 1,)), pltpu.SemaphoreType.DMA((N_DEV - 1,)),
                        pltpu.SemaphoreType.DMA],
        name="allgather_small",
    )(pack)

    def sum_body(s_ref, o_ref):
        acc = s_ref[0]
        for d in range(1, N_DEV):
            acc = acc + s_ref[d]
        o_ref[...] = acc

    return pl.pallas_call(
        sum_body,
        out_shape=jax.ShapeDtypeStruct((R, C), pack.dtype),
        name="sum_small",
    )(slots)


def _adamw_math(w, g, m, v):
    m2 = ADAM_B1 * m + (1.0 - ADAM_B1) * g
    v2 = ADAM_B2 * v + (1.0 - ADAM_B2) * (g * g)
    m_hat = m2 / (1.0 - ADAM_B1 ** ADAM_STEP)
    v_hat = v2 / (1.0 - ADAM_B2 ** ADAM_STEP)
    delta = -ADAM_LR * (m_hat / (jnp.sqrt(v_hat) + ADAM_EPS) + ADAM_WD * w)
    return delta, m2, v2


def _adamw_big(name, w, m, v, g_parts):
    shape = w.shape
    C = shape[-1]
    R = w.size // C
    tr, tc = _div_tile(R, 256, 8), _div_tile(C, 1024)
    n_g = len(g_parts)

    def body(*refs):
        w_ref, m_ref, v_ref = refs[:3]
        g_ref, d_ref, nm_ref, nv_ref = refs[3 + n_g:]
        g = refs[3][...]
        for extra in refs[4:3 + n_g]:
            g = g + extra[...]
        delta, m2, v2 = _adamw_math(w_ref[...], g, m_ref[...], v_ref[...])
        g_ref[...], d_ref[...], nm_ref[...], nv_ref[...] = g, delta, m2, v2

    blk = pl.BlockSpec((tr, tc), lambda i, j: (i, j))
    outs = pl.pallas_call(
        body,
        grid=(R // tr, C // tc),
        in_specs=[blk] * (3 + n_g),
        out_specs=[blk] * 4,
        out_shape=[jax.ShapeDtypeStruct((R, C), F32)] * 4,
        compiler_params=_cparams(("parallel", "parallel")),
        name=name,
    )(w.reshape(R, C), m.reshape(R, C), v.reshape(R, C), *[g.reshape(R, C) for g in g_parts])
    return [o.reshape(shape) for o in outs]


def _adamw_small(name, w, g, m, v):
    shape = w.shape
    two_d = (w.size // shape[-1], shape[-1])

    def body(w_ref, g_ref, m_ref, v_ref, d_ref, nm_ref, nv_ref):
        d_ref[...], nm_ref[...], nv_ref[...] = _adamw_math(w_ref[...], g_ref[...], m_ref[...], v_ref[...])

    outs = pl.pallas_call(
        body,
        out_shape=[jax.ShapeDtypeStruct(two_d, F32)] * 3,
        name=name,
    )(w.reshape(two_d), g.reshape(two_d), m.reshape(two_d), v.reshape(two_d))
    return [o.reshape(shape) for o in outs]


SMALL_ROWS = 40
S_BIN, S_G1, S_G2 = 16, 24, 32
MATS = ("w_in", "w_pa", "w_pb", "w_o", "w_mlp1", "w_mlp2")
LRU = ("lru_wr", "lru_wi")
BIG_DIM = dict(w_in=2, w_pa=1, w_pb=1, w_o=1, w_mlp1=2, w_mlp2=1, lru_wr=2, lru_wi=2)
WEIGHTS = ("norm1_g", "w_in", "b_in", "conv_a_w", "conv_a_b", "lru_wr", "lru_br", "lru_wi", "lru_bi", "lru_lam",
           "conv_b_w", "w_pa", "w_pb", "w_o", "norm2_g", "w_mlp1", "w_mlp2", "final_g")


def _rows_at(a, r0, total):
    pad = [(0, 0)] * a.ndim
    pad[-2] = (r0, total - r0 - a.shape[-2])
    return jnp.pad(a, pad)


def kernel(x, norm1_g, w_in, b_in, conv_a_w, conv_a_b, lru_wr, lru_br, lru_wi, lru_bi, lru_lam, conv_b_w, w_pa, w_pb, w_o, norm2_g, w_mlp1, w_mlp2, final_g, loss_target, m_norm1_g, m_w_in, m_b_in, m_conv_a_w, m_conv_a_b, m_lru_wr, m_lru_br, m_lru_wi, m_lru_bi, m_lru_lam, m_conv_b_w, m_w_pa, m_w_pb, m_w_o, m_norm2_g, m_w_mlp1, m_w_mlp2, m_final_g, v_norm1_g, v_w_in, v_b_in, v_conv_a_w, v_conv_a_b, v_lru_wr, v_lru_br, v_lru_wi, v_lru_bi, v_lru_lam, v_conv_b_w, v_w_pa, v_w_pb, v_w_o, v_norm2_g, v_w_mlp1, v_w_mlp2, v_final_g):
    wts = dict(norm1_g=norm1_g, w_in=w_in, b_in=b_in, conv_a_w=conv_a_w, conv_a_b=conv_a_b, lru_wr=lru_wr,
               lru_br=lru_br, lru_wi=lru_wi, lru_bi=lru_bi, lru_lam=lru_lam, conv_b_w=conv_b_w, w_pa=w_pa,
               w_pb=w_pb, w_o=w_o, norm2_g=norm2_g, w_mlp1=w_mlp1, w_mlp2=w_mlp2, final_g=final_g)
    mom = dict(norm1_g=m_norm1_g, w_in=m_w_in, b_in=m_b_in, conv_a_w=m_conv_a_w, conv_a_b=m_conv_a_b,
               lru_wr=m_lru_wr, lru_br=m_lru_br, lru_wi=m_lru_wi, lru_bi=m_lru_bi, lru_lam=m_lru_lam,
               conv_b_w=m_conv_b_w, w_pa=m_w_pa, w_pb=m_w_pb, w_o=m_w_o, norm2_g=m_norm2_g, w_mlp1=m_w_mlp1,
               w_mlp2=m_w_mlp2, final_g=m_final_g)
    vel = dict(norm1_g=v_norm1_g, w_in=v_w_in, b_in=v_b_in, conv_a_w=v_conv_a_w, conv_a_b=v_conv_a_b,
               lru_wr=v_lru_wr, lru_br=v_lru_br, lru_wi=v_lru_wi, lru_bi=v_lru_bi, lru_lam=v_lru_lam,
               conv_b_w=v_conv_b_w, w_pa=v_w_pa, w_pb=v_w_pb, w_o=v_w_o, norm2_g=v_norm2_g, w_mlp1=v_w_mlp1,
               w_mlp2=v_w_mlp2, final_g=v_final_g)
    depth, D = norm1_g.shape
    nb, bw = lru_wr.shape[1], lru_wr.shape[3]
    chip = 2 * lax.axis_index("x") + lax.axis_index("y")

    small_parts = [conv_a_w.reshape(-1), conv_b_w.reshape(-1), lru_br.reshape(-1), lru_bi.reshape(-1)]
    small_len = sum(p.shape[0] for p in small_parts)
    small_rows = -(-small_len // 1024) * 8
    small = jnp.concatenate(small_parts + [jnp.zeros((small_rows * 128 - small_len,), F32)]).reshape(small_rows, 128)
    gathered = _gather_weights([wts[n].astype(BF16) for n in LRU], [BIG_DIM[n] for n in LRU], small)
    full = dict(zip(LRU, gathered[:-1]))
    items = [(n, l) for l in range(depth) for n in MATS]
    mat_dims = [BIG_DIM[n] - 1 for n, _ in items]
    where = jnp.stack([chip, lax.axis_index("c")]).astype(jnp.int32)
    placed = {(n, l): _place_shard(f"place_{n}_{l}", wts[n], l, BIG_DIM[n] - 1, where) for n, l in items}
    flat = gathered[-1].reshape(N_CHIPS, small_rows * 128)
    off = 0
    small_full = []
    for part, shard in zip(small_parts, (conv_a_w, conv_b_w, lru_br, lru_bi)):
        piece = flat[:, off:off + part.shape[0]].reshape((N_CHIPS,) + shard.shape)
        small_full.append(jnp.moveaxis(piece, 0, -2).reshape(shard.shape[:-1] + (N_CHIPS * shard.shape[-1],)))
        off += part.shape[0]
    caw_f, cbw_f, br_f, bi_f = small_full
    pch = (_rows_at(conv_a_b[:, None, :], R_CAB, R_ROWS) + _rows_at(br_f.reshape(depth, 1, D), R_BR, R_ROWS)
           + _rows_at(bi_f.reshape(depth, 1, D), R_BI, R_ROWS) + _rows_at(lru_lam[:, None, :], R_LAM, R_ROWS)
           + _rows_at(caw_f, R_CAW, R_ROWS) + _rows_at(cbw_f, R_CBW, R_ROWS))
    W = dict(b_in=b_in, pch=pch, wr=full["lru_wr"], wi=full["lru_wi"], g1=norm1_g, g2=norm2_g, gf=final_g,
             core=lax.axis_index("c").astype(jnp.int32).reshape(1), where=where,
             shard_shapes={n: wts[n].shape for n in MATS})

    loss_local, dx, grads, dgf, sums, landed, acc = _local_fwd_bwd(x[0], loss_target[0], W, placed)
    loss = lax.psum(loss_local, ("x", "y", "c"))

    key = dict(w_in="w_in", w_pa="w_pa", w_pb="w_pb", w_o="w_o", w_mlp1="w1", w_mlp2="w2", lru_wr="wr", lru_wi="wi")
    out_g, out_d, out_m, out_v = {}, {}, {}, {}
    per_layer = [[grads[l][key[n]].astype(BF16) for l in range(depth)] for n in LRU]
    land = _scatter_grads(per_layer, [BIG_DIM[n] - 1 for n in LRU])
    chip_sums = [_sum_slots(f"sum_slots_{n}", ld.reshape(N_CHIPS, -1, ld.shape[-1])) for n, ld in zip(LRU, land)]
    sib_sums = _swap_with_sibling(chip_sums)
    for n, mine, sib in zip(LRU, chip_sums, sib_sums):
        out_g[n], out_d[n], out_m[n], out_v[n] = _adamw_big(f"adamw_{n}", wts[n], mom[n], vel[n], [mine, sib])
    for n in MATS:
        acc[n] = _reduce_into(f"reduce_{n}_0", sums[n, 0], landed[n, 0], acc.get(n), 0, BIG_DIM[n], wts[n].shape, where)
    joined = _run_carry("join_halves", _join_carry([acc[n] for n in MATS], [BIG_DIM[n] for n in MATS], 0))
    for n, g in zip(MATS, joined):
        out_g[n], out_d[n], out_m[n], out_v[n] = _adamw_big(f"adamw_{n}", wts[n], mom[n], vel[n], [g])

    rows = []
    for l in range(depth):
        g = grads[l]
        rows.append(_rows_at(g["sm"], 0, SMALL_ROWS) + _rows_at(g["b_in"].reshape(7, D), S_BIN, SMALL_ROWS)
                    + _rows_at(g["g1"], S_G1, SMALL_ROWS) + _rows_at(g["g2"], S_G2, SMALL_ROWS))
    rows.append(_rows_at(dgf, 0, 8))
    tot = _allreduce_small(jnp.concatenate(rows, axis=0))
    per = tot[:depth * SMALL_ROWS].reshape(depth, SMALL_ROWS, D)

    def cols_of_chip(a, axis):
        size = a.shape[axis] // N_CHIPS
        return lax.dynamic_slice_in_dim(a, chip * size, size, axis=axis)

    small_g = dict(
        norm1_g=per[:, S_G1], b_in=per[:, S_BIN:S_BIN + 7].reshape(depth, 7 * D),
        conv_a_w=cols_of_chip(per[:, R_CAW:R_CAW + 4], 2), conv_a_b=per[:, R_CAB],
        lru_br=cols_of_chip(per[:, R_BR].reshape(depth, nb, bw), 2),
        lru_bi=cols_of_chip(per[:, R_BI].reshape(depth, nb, bw), 2), lru_lam=per[:, R_LAM],
        conv_b_w=cols_of_chip(per[:, R_CBW:R_CBW + 3], 2), norm2_g=per[:, S_G2],
        final_g=tot[depth * SMALL_ROWS])
    for n, g in small_g.items():
        out_g[n] = g
        out_d[n], out_m[n], out_v[n] = _adamw_small(f"adamw_{n}", wts[n], g, mom[n], vel[n])

    return (loss, dx[None], *[out_g[n] for n in WEIGHTS], *[out_d[n] for n in WEIGHTS],
            *[out_m[n] for n in WEIGHTS], *[out_v[n] for n in WEIGHTS])
```

```python
import functools

import jax
import jax.numpy as jnp
from jax import lax
from jax.experimental import pallas as pl
from jax.experimental.pallas import tpu as pltpu

F32 = jnp.float32
BF16 = jnp.bfloat16
MESH = pl.DeviceIdType.MESH

EPS = 1e-6
LRU_C = 8.0
ADAM_LR = 0.001
ADAM_B1 = 0.9
ADAM_B2 = 0.999
ADAM_EPS = 1e-08
ADAM_WD = 0.01
ADAM_STEP = 10

N_CHIPS = 4
N_DEV = 8
HALO = 8
VMEM_LIMIT = 56 * 1024 * 1024
MM_TILES = (1024, 1024, 2048)
MM_TILES_FUSED = (512, 1024, 2048)
MM_TILES_LONG_K = (1024, 1024, 4096)
SEQ_CHUNK = 256
MIXER_ROW_BLOCK = 32
ROW_TILE = 256

R_CAB, R_BR, R_BI, R_LAM, R_CAW, R_CBW, R_ROWS = 0, 1, 2, 3, 4, 8, 16


def _cparams(sem):
    return pltpu.CompilerParams(dimension_semantics=sem, vmem_limit_bytes=VMEM_LIMIT)


def _div_tile(n, pref, unit=128):
    if n <= pref:
        return n
    t = (pref // unit) * unit
    while n % t:
        t -= unit
    return t


def _sigmoid(v):
    return 1.0 / (1.0 + jnp.exp(-v))


def _gelu_and_grad(y):
    k = 0.7978845608028654
    c = 0.044715
    y2 = y * y
    t = jnp.tanh(k * (y + c * y2 * y))
    g = 0.5 * y * (1.0 + t)
    gp = 0.5 * (1.0 + t) + 0.5 * y * (1.0 - t * t) * (k * (1.0 + 3.0 * c * y2))
    return g, gp


def _softplus_neg(lam):
    e = jnp.exp(-jnp.abs(lam))
    w = 1.0 + e
    l1p = jnp.where(w == 1.0, e, jnp.log(w) * e / jnp.where(w == 1.0, 1.0, w - 1.0))
    return jnp.maximum(-lam, 0.0) + l1p


def _mm(name, mode, a, b, M, N, K, out_dtypes, epilogue=None, extras=(), la=None, lb=None, tiles=None,
        carry=None):
    tiles = MM_TILES if tiles is None else tiles
    tm, tn, tk = _div_tile(M, tiles[0]), _div_tile(N, tiles[1]), _div_tile(K, tiles[2])
    assert M % tm == 0 and N % tn == 0 and K % tk == 0, (name, M, N, K)
    nk = K // tk

    def spec(lead, shape, imap):
        if lead is None:
            return pl.BlockSpec(shape, imap)
        return pl.BlockSpec((None,) + shape, lambda i, j, k: (lead,) + imap(i, j, k))

    if mode == "nn":
        a_spec = spec(la, (tm, tk), lambda i, j, k: (i, k))
        b_spec = spec(lb, (tk, tn), lambda i, j, k: (k, j))
        dn = (((1,), (0,)), ((), ()))
    elif mode == "nt":
        a_spec = spec(la, (tm, tk), lambda i, j, k: (i, k))
        b_spec = spec(lb, (tn, tk), lambda i, j, k: (j, k))
        dn = (((1,), (1,)), ((), ()))
    else:
        a_spec = spec(la, (tk, tm), lambda i, j, k: (k, i))
        b_spec = spec(lb, (tk, tn), lambda i, j, k: (k, j))
        dn = (((0,), (0,)), ((), ()))

    ex_arrays, ex_specs = [], []
    for arr, kind, off in extras:
        ex_arrays.append(arr)
        if kind == "bias":
            ex_specs.append(pl.BlockSpec((1, tn), lambda i, j, k: (0, j)))
        else:
            assert off % tn == 0
            ex_specs.append(pl.BlockSpec((tm, tn), lambda i, j, k, o=off // tn: (i, j + o)))
    n_ex, n_out = len(ex_arrays), len(out_dtypes)
    n_cin = len(carry.ins) if carry else 0
    n_cout = len(carry.out_shapes) if carry else 0
    n_in = 2 + n_ex + n_cin
    gi, gj = M // tm, N // tn

    def body(*refs):
        a_ref, b_ref = refs[0], refs[1]
        ex = refs[2:2 + n_ex]
        outs = refs[n_in:n_in + n_out]
        acc = refs[n_in + n_out + n_cout]
        i, j, k = pl.program_id(0), pl.program_id(1), pl.program_id(2)
        if carry:
            c_in, c_out = refs[2 + n_ex:n_in], refs[n_in + n_out:n_in + n_out + n_cout]
            sems = refs[n_in + n_out + n_cout + 1:]
            step = (i * gj + j) * nk + k
            carry.ride(step, gi * gj * nk, c_in, c_out, sems, True)

        def product():
            return lax.dot_general(a_ref[...], b_ref[...], dn, preferred_element_type=F32)

        def finish(r):
            vals = (r,) if epilogue is None else epilogue(r, *[e[...] for e in ex])
            for o, v in zip(outs, vals):
                o[...] = v.astype(o.dtype)

        if nk == 1:
            finish(product())
        else:
            @pl.when(k == 0)
            def _():
                acc[...] = product()

            @pl.when((k > 0) & (k < nk - 1))
            def _():
                acc[...] += product()

            @pl.when(k == nk - 1)
            def _():
                finish(acc[...] + product())

        if carry:
            carry.ride(step, gi * gj * nk, c_in, c_out, sems, False)

    res = pl.pallas_call(
        body,
        grid=(gi, gj, nk),
        in_specs=[a_spec, b_spec, *ex_specs] + [ANY] * n_cin,
        out_specs=[pl.BlockSpec((tm, tn), lambda i, j, k: (i, j)) for _ in range(n_out)] + [ANY] * n_cout,
        out_shape=[jax.ShapeDtypeStruct((M, N), d) for d in out_dtypes] + (carry.out_shapes if carry else []),
        scratch_shapes=[pltpu.VMEM((tm, tn) if nk > 1 else (8, 128), F32)] + (carry.scratch() if carry else []),
        input_output_aliases={2 + n_ex + ci: n_out + co for ci, co in carry.aliases.items()} if carry else {},
        compiler_params=_cparams(("arbitrary",) * 3 if carry else ("parallel", "parallel", "arbitrary")),
        name=name,
    )(a, b, *ex_arrays, *(carry.ins if carry else []))
    main = res[0] if n_out == 1 else res[:n_out]
    return (main, res[n_out:]) if carry else main


def _rms_fwd(name, x, g_row):
    T, D = x.shape
    tm = min(ROW_TILE, T)

    def body(x_ref, g_ref, h_ref):
        xv = x_ref[...]
        r = lax.rsqrt(jnp.mean(xv * xv, axis=-1, keepdims=True) + EPS)
        h_ref[...] = (xv * r * g_ref[...]).astype(BF16)

    return pl.pallas_call(
        body,
        grid=(T // tm,),
        in_specs=[pl.BlockSpec((tm, D), lambda i: (i, 0)), pl.BlockSpec((1, D), lambda i: (0, 0))],
        out_specs=pl.BlockSpec((tm, D), lambda i: (i, 0)),
        out_shape=jax.ShapeDtypeStruct((T, D), BF16),
        compiler_params=_cparams(("parallel",)),
        name=name,
    )(x, g_row)


def _rms_bwd(name, x, g_row, dh, dres):
    T, D = x.shape
    tm = min(ROW_TILE, T)

    def body(x_ref, g_ref, dh_ref, dres_ref, dx_ref, dxb_ref, dg_ref):
        xv, dhv = x_ref[...], dh_ref[...]
        r = lax.rsqrt(jnp.mean(xv * xv, axis=-1, keepdims=True) + EPS)
        gd = g_ref[...] * dhv
        c = jnp.mean(xv * gd, axis=-1, keepdims=True)
        dx = r * gd - xv * (r * r * r) * c + dres_ref[...]
        dx_ref[...] = dx
        dxb_ref[...] = dx.astype(BF16)

        @pl.when(pl.program_id(0) == 0)
        def _():
            dg_ref[...] = jnp.zeros_like(dg_ref)

        dg_ref[...] += jnp.sum(dhv * xv * r, axis=0, keepdims=True)

    row = pl.BlockSpec((tm, D), lambda i: (i, 0))
    vec = pl.BlockSpec((1, D), lambda i: (0, 0))
    return pl.pallas_call(
        body,
        grid=(T // tm,),
        in_specs=[row, vec, row, row],
        out_specs=[row, row, vec],
        out_shape=[jax.ShapeDtypeStruct((T, D), F32), jax.ShapeDtypeStruct((T, D), BF16),
                   jax.ShapeDtypeStruct((1, D), F32)],
        compiler_params=_cparams(("arbitrary",)),
        name=name,
    )(x, g_row, dh, dres)


def _loss_head(name, x, g_row, tgt):
    T, D = x.shape
    tm = min(ROW_TILE, T)

    def body(x_ref, g_ref, t_ref, dx_ref, dxb_ref, dg_ref, loss_ref):
        xv, g = x_ref[...], g_ref[...]
        r = lax.rsqrt(jnp.mean(xv * xv, axis=-1, keepdims=True) + EPS)
        xh = xv * r
        e = xh * g - t_ref[...]
        lpart = 0.5 * jnp.sum(jnp.mean(e * e, axis=-1, keepdims=True))
        dy = e * (1.0 / D)
        gd = g * dy
        c = jnp.mean(xv * gd, axis=-1, keepdims=True)
        dx = r * gd - xv * (r * r * r) * c
        dx_ref[...] = dx
        dxb_ref[...] = dx.astype(BF16)

        @pl.when(pl.program_id(0) == 0)
        def _():
            dg_ref[...] = jnp.zeros_like(dg_ref)
            loss_ref[...] = jnp.zeros_like(loss_ref)

        dg_ref[...] += jnp.sum(dy * xh, axis=0, keepdims=True)
        loss_ref[...] += jnp.full(loss_ref.shape, lpart, F32)

    row = pl.BlockSpec((tm, D), lambda i: (i, 0))
    vec = pl.BlockSpec((1, D), lambda i: (0, 0))
    return pl.pallas_call(
        body,
        grid=(T // tm,),
        in_specs=[row, vec, row],
        out_specs=[row, row, vec, pl.BlockSpec((8, 128), lambda i: (0, 0))],
        out_shape=[jax.ShapeDtypeStruct((T, D), F32), jax.ShapeDtypeStruct((T, D), BF16),
                   jax.ShapeDtypeStruct((1, D), F32), jax.ShapeDtypeStruct((8, 128), F32)],
        compiler_params=_cparams(("arbitrary",)),
        name=name,
    )(x, g_row, tgt)


def _colsum(name, a, carry=None):
    T, N = a.shape
    tm, tn = min(512, T), _div_tile(N, 2048)
    gj, gi = N // tn, T // tm
    n_cin = len(carry.ins) if carry else 0
    n_cout = len(carry.out_shapes) if carry else 0

    def body(*refs):
        a_ref, o_ref = refs[0], refs[1 + n_cin]
        j, i = pl.program_id(0), pl.program_id(1)
        if carry:
            c_in, c_out, sems = refs[1:1 + n_cin], refs[2 + n_cin:2 + n_cin + n_cout], refs[2 + n_cin + n_cout:]
            carry.ride(j * gi + i, gj * gi, c_in, c_out, sems, True)

        @pl.when(i == 0)
        def _():
            o_ref[...] = jnp.zeros_like(o_ref)

        o_ref[...] += jnp.sum(a_ref[...].astype(F32), axis=0, keepdims=True)

        if carry:
            carry.ride(j * gi + i, gj * gi, c_in, c_out, sems, False)

    res = pl.pallas_call(
        body,
        grid=(gj, gi),
        in_specs=[pl.BlockSpec((tm, tn), lambda j, i: (i, j))] + [ANY] * n_cin,
        out_specs=[pl.BlockSpec((1, tn), lambda j, i: (0, j))] + [ANY] * n_cout,
        out_shape=[jax.ShapeDtypeStruct((1, N), F32)] + (carry.out_shapes if carry else []),
        scratch_shapes=carry.scratch() if carry else [],
        compiler_params=_cparams(("arbitrary", "arbitrary") if carry else ("parallel", "arbitrary")),
        name=name,
    )(a, *(carry.ins if carry else []))
    return (res[0], res[1:]) if carry else res[0]


def _tile_scan(a, b, row, reverse):
    for s in (1, 2, 4):
        if reverse:
            a_s, b_s, m = pltpu.roll(a, 8 - s, 0), pltpu.roll(b, 8 - s, 0), row < 8 - s
        else:
            a_s, b_s, m = pltpu.roll(a, s, 0), pltpu.roll(b, s, 0), row >= s
        b = jnp.where(m, a * b_s + b, b)
        a = jnp.where(m, a * a_s, a)
    return a, b


def _chunk_scan(a_s, b_s, out_ref, carry, n_tiles, width, reverse):
    row = lax.broadcasted_iota(jnp.int32, (8, width), 0)
    edge = 0 if reverse else 7

    group = 4 if n_tiles % 4 == 0 else 1

    def step(j, c):
        jj = (n_tiles // group - 1 - j) if reverse else j
        base = pl.multiple_of(jj * (8 * group), 8 * group)
        order = range(group - 1, -1, -1) if reverse else range(group)
        parts = {t: _tile_scan(a_s[pl.ds(base + 8 * t, 8), :], b_s[pl.ds(base + 8 * t, 8), :], row, reverse)
                 for t in order}
        for t in order:
            h = parts[t][0] * c + parts[t][1]
            out_ref[pl.ds(base + 8 * t, 8), :] = h
            c = jnp.broadcast_to(h[edge:edge + 1, :], (8, width))
        return c

    carry[...] = lax.fori_loop(0, n_tiles // group, step, carry[...])


def _mixer_specs(Tc, bw, nb, layer):
    def seg(s):
        return pl.BlockSpec((Tc, bw), lambda n, i: (i, s * nb + n))

    p_spec = pl.BlockSpec((None, R_ROWS, bw), lambda n, i: (layer, 0, n))
    w_spec = pl.BlockSpec((None, None, bw, bw), lambda n, i: (layer, n, 0, 0))
    return seg, p_spec, w_spec


def _mixer_fwd(name, layer, z, pch, wr, wi, comm=None):
    T, D = z.shape[0], z.shape[1] // 7
    bw, nb = wr.shape[-1], wr.shape[1]
    Tc = min(SEQ_CHUNK, T)
    nT = T // Tc
    n_cin = len(comm.ins) if comm else 0
    n_cout = len(comm.out_shapes) if comm else 0

    def body(*refs):
        xa_ref, ya_ref, cb_ref, cc_ref, cx_ref, p_ref, wr_ref, wi_ref = refs[:8]
        pa_ref, pb_ref, xc_ref, hl_ref, vb_ref = refs[8 + n_cin:13 + n_cin]
        xa_buf, u_buf, a_s, b_s, carry = refs[13 + n_cin + n_cout:18 + n_cin + n_cout]
        if comm:
            c_in, c_out, sems = refs[8:8 + n_cin], refs[13 + n_cin:13 + n_cin + n_cout], refs[18 + n_cin + n_cout:]
            step = pl.program_id(0) * nT + pl.program_id(1)
            comm.ride(step, nb * nT, c_in, c_out, sems, True)

        @pl.when(pl.program_id(1) == 0)
        def _():
            xa_buf[0:HALO, :] = jnp.zeros((HALO, bw), F32)
            u_buf[0:HALO, :] = jnp.zeros((HALO, bw), F32)
            carry[...] = jnp.zeros_like(carry)

        rb = min(MIXER_ROW_BLOCK, Tc)
        blocks = [(g * rb, slice(g * rb, (g + 1) * rb)) for g in range(Tc // rb)]

        def prow(k):
            return p_ref[k:k + 1, :]

        xa_buf[HALO:HALO + Tc, :] = xa_ref[...]
        u_buf[HALO:HALO + Tc, :] = cc_ref[...] * cx_ref[...]
        for r0, rs in blocks:
            xc = prow(R_CAB)
            for k in range(4):
                xc = xc + prow(R_CAW + k) * xa_buf[HALO - 3 + k + r0:HALO - 3 + k + r0 + rb, :]
            xc_ref[rs, :] = xc
        xcb = xc_ref[...].astype(BF16)
        a_s[...] = jnp.dot(xcb, wr_ref[...], preferred_element_type=F32)
        b_s[...] = jnp.dot(xcb, wi_ref[...], preferred_element_type=F32)
        sp = _softplus_neg(prow(R_LAM))
        for _, rs in blocks:
            r = _sigmoid(a_s[rs, :] + prow(R_BR))
            ig = _sigmoid(b_s[rs, :] + prow(R_BI))
            log_a = (-LRU_C) * r * sp
            t = jnp.tanh(log_a)
            a_s[rs, :] = jnp.exp(log_a)
            b_s[rs, :] = jnp.sqrt(-2.0 * t / (1.0 - t)) * (ig * xc_ref[rs, :])
        _chunk_scan(a_s, b_s, hl_ref, carry, Tc // 8, bw, False)
        for r0, rs in blocks:
            g, _ = _gelu_and_grad(ya_ref[rs, :])
            pa_ref[rs, :] = (hl_ref[rs, :] * g).astype(BF16)
            vb = jnp.zeros((rb, bw), F32)
            for k in range(3):
                vb = vb + prow(R_CBW + k) * u_buf[HALO - 2 + k + r0:HALO - 2 + k + r0 + rb, :]
            vb_ref[rs, :] = vb
            pb_ref[rs, :] = (cb_ref[rs, :] * vb).astype(BF16)
        xa_buf[0:HALO, :] = xa_buf[Tc:Tc + HALO, :]
        u_buf[0:HALO, :] = u_buf[Tc:Tc + HALO, :]

        if comm:
            comm.ride(step, nb * nT, c_in, c_out, sems, False)

    seg, p_spec, w_spec = _mixer_specs(Tc, bw, nb, layer)
    out = pl.BlockSpec((Tc, bw), lambda n, i: (i, n))
    res = pl.pallas_call(
        body,
        grid=(nb, nT),
        in_specs=[seg(0), seg(1), seg(2), seg(3), seg(4), p_spec, w_spec, w_spec] + [ANY] * n_cin,
        out_specs=[out] * 5 + [ANY] * n_cout,
        out_shape=[jax.ShapeDtypeStruct((T, D), BF16), jax.ShapeDtypeStruct((T, D), BF16),
                   jax.ShapeDtypeStruct((T, D), F32), jax.ShapeDtypeStruct((T, D), F32),
                   jax.ShapeDtypeStruct((T, D), F32)] + (comm.out_shapes if comm else []),
        scratch_shapes=[pltpu.VMEM((Tc + HALO, bw), F32), pltpu.VMEM((Tc + HALO, bw), F32),
                        pltpu.VMEM((Tc, bw), F32), pltpu.VMEM((Tc, bw), F32), pltpu.VMEM((8, bw), F32)]
        + (comm.scratch() if comm else []),
        input_output_aliases={8 + ci: 5 + co for ci, co in comm.aliases.items()} if comm else {},
        compiler_params=_cparams(("arbitrary", "arbitrary") if comm else ("parallel", "arbitrary")),
        name=name,
    )(z, z, z, z, z, pch, wr, wi, *(comm.ins if comm else []))
    return (res[:5], res[5:]) if comm else res


def _mixer_bwd(name, layer, z, xc, hl, vb, dpa, dpb, dga, dgb, pch, wr, wi, comm=None):
    T, D = z.shape[0], z.shape[1] // 7
    bw, nb = wr.shape[-1], wr.shape[1]
    Tc = min(SEQ_CHUNK, T)
    nT = T // Tc
    tpc = Tc // 8
    rb = min(MIXER_ROW_BLOCK, Tc)
    n_cin = len(comm.ins) if comm else 0
    n_cout = len(comm.out_shapes) if comm else 0
    total = nb * nT

    def body(*refs):
        (xa_ref, ya_ref, cb_ref, cc_ref, cx_ref, xc_ref, hl_ref, hp_ref, vb_ref, dpa_ref, dpb_ref, dga_ref, dgb_ref,
         p_ref, wr_ref, wi_ref) = refs[:16]
        dz_ref, dwr_ref, dwi_ref, sm_ref = refs[16 + n_cin:20 + n_cin]
        (h_buf, a_buf, dxc_buf, dvb_buf, a_s, d_s, lam_s, r_s, i_s, m_s, dpr_s, dpi_s, sm8,
         carry, stage, out_sems) = refs[20 + n_cin + n_cout:36 + n_cin + n_cout]
        i = pl.program_id(1)
        step = pl.program_id(0) * nT + i
        if comm:
            c_in, c_out = refs[16:16 + n_cin], refs[20 + n_cin:20 + n_cin + n_cout]
            sems = refs[36 + n_cin + n_cout:]
            comm.ride(step, total, c_in, c_out, sems, True)

        slot = step % 2

        def out_copies(sl):
            rows = pl.ds(pl.multiple_of((nT - 1 - i) * Tc, Tc), Tc)
            return [pltpu.make_async_copy(
                stage.at[sl, s], dz_ref.at[rows, pl.ds(pl.multiple_of((s * nb + pl.program_id(0)) * bw, bw), bw)],
                out_sems.at[sl, s]) for s in range(7)]

        @pl.when(step >= 2)
        def _():
            for cp in out_copies(slot):
                cp.wait()

        dxa_ref, dya_ref, dcb_ref, dcc_ref, dcx_ref = [stage.at[slot, s] for s in range(5)]
        stage[slot, 5, :, :] = dga_ref[...]
        stage[slot, 6, :, :] = dgb_ref[...]

        @pl.when(i == 0)
        def _():
            a_buf[Tc:Tc + HALO, :] = jnp.zeros((HALO, bw), F32)
            dxc_buf[Tc:Tc + HALO, :] = jnp.zeros((HALO, bw), F32)
            dvb_buf[Tc:Tc + HALO, :] = jnp.zeros((HALO, bw), F32)
            carry[...] = jnp.zeros_like(carry)
            dwr_ref[...] = jnp.zeros_like(dwr_ref)
            dwi_ref[...] = jnp.zeros_like(dwi_ref)
            sm8[...] = jnp.zeros_like(sm8)

        blocks = [(g * rb, slice(g * rb, (g + 1) * rb)) for g in range(Tc // rb)]

        def prow(k):
            return p_ref[k:k + 1, :]

        def part8(v):
            return jnp.sum(v.reshape(rb // 8, 8, bw), axis=0)

        sums8 = {}

        def tally(k, v):
            sums8[k] = sums8[k] + part8(v) if k in sums8 else part8(v)

        sp = _softplus_neg(prow(R_LAM))
        xcb = xc_ref[...].astype(BF16)
        r_s[...] = jnp.dot(xcb, wr_ref[...], preferred_element_type=F32)
        i_s[...] = jnp.dot(xcb, wi_ref[...], preferred_element_type=F32)
        for _, rs in blocks:
            r = _sigmoid(r_s[rs, :] + prow(R_BR))
            ig = _sigmoid(i_s[rs, :] + prow(R_BI))
            log_a = (-LRU_C) * r * sp
            t = jnp.tanh(log_a)
            r_s[rs, :] = r
            i_s[rs, :] = ig
            m_s[rs, :] = jnp.sqrt(-2.0 * t / (1.0 - t))
            a_buf[rs, :] = jnp.exp(log_a)
            g, gp = _gelu_and_grad(ya_ref[rs, :])
            dpav = dpa_ref[rs, :]
            d_s[rs, :] = dpav * g
            dya_ref[rs, :] = (dpav * hl_ref[rs, :] * gp).astype(BF16)
            dpbv = dpb_ref[rs, :]
            dcb_ref[rs, :] = (dpbv * vb_ref[rs, :]).astype(BF16)
            dvb_buf[rs, :] = dpbv * cb_ref[rs, :]

        a_s[...] = a_buf[1:Tc + 1, :]
        _chunk_scan(a_s, d_s, lam_s, carry, tpc, bw, True)
        h_buf[HALO:HALO + Tc, :] = hl_ref[...]
        h_buf[0:HALO, :] = jnp.where(i == nT - 1, 0.0, hp_ref[...])

        for r0, rs in blocks:
            lamv, xcv, r, ig, a = lam_s[rs, :], xc_ref[rs, :], r_s[rs, :], i_s[rs, :], a_buf[rs, :]
            mult = m_s[rs, :]
            da = lamv * h_buf[HALO - 1 + r0:HALO - 1 + r0 + rb, :]
            dmult = lamv * (ig * xcv)
            dbx = lamv * mult
            dig = dbx * xcv
            dxc_buf[rs, :] = dbx * ig
            dlog_a = da * a - dmult * (a * a) / mult
            dpr = (dlog_a * ((-LRU_C) * sp)) * r * (1.0 - r)
            dpi = dig * ig * (1.0 - ig)
            dpr_s[rs, :] = dpr.astype(BF16)
            dpi_s[rs, :] = dpi.astype(BF16)
            tally(R_BR, dpr)
            tally(R_BI, dpi)
            tally(R_LAM, dlog_a * ((-LRU_C) * r))

        dprb, dpib = dpr_s[...], dpi_s[...]
        nt = (((1,), (1,)), ((), ()))
        tn = (((0,), (0,)), ((), ()))
        dxc_buf[0:Tc, :] += (lax.dot_general(dprb, wr_ref[...], nt, preferred_element_type=F32)
                             + lax.dot_general(dpib, wi_ref[...], nt, preferred_element_type=F32))
        dwr_ref[...] += lax.dot_general(xcb, dprb, tn, preferred_element_type=F32)
        dwi_ref[...] += lax.dot_general(xcb, dpib, tn, preferred_element_type=F32)

        for r0, rs in blocks:
            xav = xa_ref[rs, :]
            tally(R_CAB, dxc_buf[rs, :])
            dxa = jnp.zeros((rb, bw), F32)
            for k in range(4):
                sh = dxc_buf[3 - k + r0:3 - k + r0 + rb, :]
                dxa = dxa + prow(R_CAW + k) * sh
                tally(R_CAW + k, xav * sh)
            dxa_ref[rs, :] = dxa.astype(BF16)
            ccv, cxv = cc_ref[rs, :], cx_ref[rs, :]
            u = ccv * cxv
            du = jnp.zeros((rb, bw), F32)
            for k in range(3):
                sh = dvb_buf[2 - k + r0:2 - k + r0 + rb, :]
                du = du + prow(R_CBW + k) * sh
                tally(R_CBW + k, u * sh)
            dcc_ref[rs, :] = (du * cxv).astype(BF16)
            dcx_ref[rs, :] = (du * ccv).astype(BF16)

        for k, v in sums8.items():
            sm8[8 * k:8 * k + 8, :] += v
        a_buf[Tc:Tc + HALO, :] = a_buf[0:HALO, :]
        dxc_buf[Tc:Tc + HALO, :] = dxc_buf[0:HALO, :]
        dvb_buf[Tc:Tc + HALO, :] = dvb_buf[0:HALO, :]

        @pl.when(i == nT - 1)
        def _():
            sm_ref[...] = jnp.sum(sm8[...].reshape(R_ROWS, 8, bw), axis=1)
            sm_ref[R_LAM:R_LAM + 1, :] = sm_ref[R_LAM:R_LAM + 1, :] * (-_sigmoid(-prow(R_LAM)))

        for cp in out_copies(slot):
            cp.start()

        @pl.when(step == total - 1)
        def _():
            if total > 1:
                for cp in out_copies(1 - slot):
                    cp.wait()
            for cp in out_copies(slot):
                cp.wait()

        if comm:
            comm.ride(step, total, c_in, c_out, sems, False)

    def seg(s):
        return pl.BlockSpec((Tc, bw), lambda n, i: (nT - 1 - i, s * nb + n))

    blk = pl.BlockSpec((Tc, bw), lambda n, i: (nT - 1 - i, n))
    halo = pl.BlockSpec((8, bw), lambda n, i: (jnp.maximum((nT - 1 - i) * tpc - 1, 0), n))
    p_spec = pl.BlockSpec((None, R_ROWS, bw), lambda n, i: (layer, 0, n))
    w_spec = pl.BlockSpec((None, None, bw, bw), lambda n, i: (layer, n, 0, 0))
    dw_spec = pl.BlockSpec((None, bw, bw), lambda n, i: (n, 0, 0))
    res = pl.pallas_call(
        body,
        grid=(nb, nT),
        in_specs=[seg(0), seg(1), seg(2), seg(3), seg(4), blk, blk, halo, blk, blk, blk, blk, blk,
                  p_spec, w_spec, w_spec] + [ANY] * n_cin,
        out_specs=[ANY, dw_spec, dw_spec, pl.BlockSpec((R_ROWS, bw), lambda n, i: (0, n))] + [ANY] * n_cout,
        out_shape=[jax.ShapeDtypeStruct((T, 7 * D), BF16), jax.ShapeDtypeStruct((nb, bw, bw), F32),
                   jax.ShapeDtypeStruct((nb, bw, bw), F32), jax.ShapeDtypeStruct((R_ROWS, D), F32)]
        + (comm.out_shapes if comm else []),
        scratch_shapes=[pltpu.VMEM((Tc + HALO, bw), F32)] * 4 + [pltpu.VMEM((Tc, bw), F32)] * 6
        + [pltpu.VMEM((Tc, bw), BF16)] * 2 + [pltpu.VMEM((8 * R_ROWS, bw), F32), pltpu.VMEM((8, bw), F32),
                                              pltpu.VMEM((2, 7, Tc, bw), BF16), pltpu.SemaphoreType.DMA((2, 7))]
        + (comm.scratch() if comm else []),
        input_output_aliases={16 + ci: 4 + co for ci, co in comm.aliases.items()} if comm else {},
        compiler_params=_cparams(("arbitrary", "arbitrary")),
        name=name,
    )(z, z, z, z, z, xc, hl, hl, vb, dpa, dpb, dga, dgb, pch, wr, wi, *(comm.ins if comm else []))
    return (res[:4], res[4:]) if comm else res


def _local_fwd_bwd(x, tgt, W, placed=None):
    T, D = x.shape
    g1, g2 = W["g1"], W["g2"]
    depth = g1.shape[0]
    FF = 4 * D
    if placed is None:
        mats = {(n, l): W[n][l] for n in MATS for l in range(depth)}
    else:
        mats = {}
        mats["w_in", 0], = _run_carry("gather_first", _gather_carry([placed["w_in", 0]], [(0, 1, 0, 1)]))

    def gathering(specs):
        if placed is None or not specs:
            return None, []
        keys = [(n, l) for n, l, _, _, _ in specs]
        arrays = [mats.get(k, placed[k]) for k in keys]
        return _gather_carry(arrays, [(i, d, part, nparts) for i, (_, _, d, part, nparts) in enumerate(specs)]), keys

    def hosted(call, specs, **kw):
        carry, keys = gathering(specs)
        if carry is None:
            return call(**kw)
        res, got = call(**kw, **{("comm" if call.func is _mixer_fwd else "carry"): carry})
        mats.update(zip(keys, got))
        return res

    saved = []
    xs = x
    for l in range(depth):
        h = _rms_fwd(f"rms1_fwd_{l}", xs, g1[l][None])
        nxt = l + 1 < depth
        projs = [("w_pa", l, 0, 0, 1), ("w_pb", l, 0, 0, 1), ("w_o", l, 0, 0, 1)]
        z = hosted(functools.partial(_mm, f"in_proj_{l}", "nn", h, mats["w_in", l], T, 7 * D, D, [F32]),
                   projs + ([("w_mlp1", l, 1, 0, 1)] if nxt else []),
                   epilogue=lambda acc, b: (acc + b,), extras=[(W["b_in"][l][None], "bias", 0)])
        pa, pb, xc, hl, vb = hosted(
            functools.partial(_mixer_fwd, f"mixer_fwd_{l}", l, z, W["pch"], W["wr"], W["wi"]),
            [("w_mlp2", l, 0, 0, 1)] if nxt else [("w_mlp1", l, 1, 0, 1)])
        oa = _mm(f"proj_a_{l}", "nn", pa, mats["w_pa", l], T, D, D, [F32])

        def merge(acc, oav, ga, gb):
            return acc, _sigmoid(ga) * oav + _sigmoid(gb) * acc

        ob, mg = _mm(f"proj_b_merge_{l}", "nn", pb, mats["w_pb", l], T, D, D, [F32, BF16], epilogue=merge,
                     tiles=MM_TILES_FUSED,
                     extras=[(oa, "tile", 0), (z, "tile", 5 * D), (z, "tile", 6 * D)])
        x1 = _mm(f"out_proj_{l}", "nn", mg, mats["w_o", l], T, D, D, [F32],
                 epilogue=lambda acc, res: (res + acc,), extras=[(xs, "tile", 0)])
        h2 = _rms_fwd(f"rms2_fwd_{l}", x1, g2[l][None])

        def relu2(acc):
            pr = jnp.maximum(acc, 0.0)
            return pr * pr, pr

        u, pr = hosted(functools.partial(_mm, f"mlp1_{l}", "nn", h2, mats["w_mlp1", l], T, FF, D, [BF16, BF16]),
                       [("w_in", l + 1, 1, 0, 2)] if nxt else [("w_mlp2", l, 0, 0, 1)], epilogue=relu2)
        x2 = hosted(functools.partial(_mm, f"mlp2_{l}", "nn", u, mats["w_mlp2", l], T, D, FF, [F32]),
                    [("w_in", l + 1, 1, 1, 2)] if nxt else [],
                    epilogue=lambda acc, res: (res + acc,), extras=[(x1, "tile", 0)])
        saved.append(dict(x0=xs, h=h, z=z, pa=pa, pb=pb, xc=xc, hl=hl, vb=vb, oa=oa, ob=ob, mg=mg, x1=x1,
                          h2=h2, u=u, pr=pr))
        xs = x2

    dx, dxb, dgf, loss_blk = _loss_head("loss_head", xs, W["gf"][None], tgt)

    gmat, got, sums, landed, acc = {}, {}, {}, {}, {}

    def reducing(call, swaps=(), scatters=(), part=None, join=None, mixer=False, **kw):
        swaps, scatters = [(n, l) for n in swaps], [(n, l) for n in scatters]
        carries, sinks = [], []
        if placed is not None and swaps:
            carries.append(_swap_carry([gmat[k] for k in swaps], [BIG_DIM[k[0]] - 1 for k in swaps]))
            sinks.append((got, swaps))
        if placed is not None and scatters:
            carries.append(_scatter_carry([sums[k] for k in scatters]))
            sinks.append((landed, scatters))
        if placed is not None and part is not None:
            key = part[:2]
            carries.append(_scatter_carry([sums[key]], part[2], part[3], [landed[key]] if key in landed else None))
            sinks.append((landed, [key]))
        if placed is not None and join is not None:
            carries.append(_join_carry([acc[n] for n in MATS], [BIG_DIM[n] for n in MATS], join))
            sinks.append((acc, list(MATS)))
        if not carries:
            return call(**kw)
        res, moved = call(**kw, **{("comm" if mixer else "carry"): _merge_carries(carries)})
        moved = list(moved)
        for target, keys in sinks:
            for k in keys:
                target[k] = moved.pop(0)
        return res

    def add(names):
        if placed is not None:
            for n in names:
                sums[n, l] = _add_halves(f"add_halves_{n}_{l}", gmat[n, l], got[n, l], BIG_DIM[n] - 1, W["core"])

    grads = [None] * depth
    for l in reversed(range(depth)):
        s = saved[l]
        dp = _mm(f"mlp2_dx_{l}", "nt", dxb, mats["w_mlp2", l], T, FF, D, [BF16],
                 epilogue=lambda acc, prv: (2.0 * prv.astype(F32) * acc,), extras=[(s["pr"], "tile", 0)])
        dw2 = gmat["w_mlp2", l] = reducing(functools.partial(_mm, f"mlp2_dw_{l}", "tn", s["u"], dxb, FF, D, T, [BF16]),
                                           part=("w_in", l + 1, 1, 2) if l + 1 < depth else None)
        above = l + 1 if placed is not None and l + 1 < depth else None
        if above is not None:
            for n in MATS:
                acc[n] = _reduce_into(f"reduce_{n}_{above}", sums[n, above], landed[n, above], acc.get(n), above,
                                      BIG_DIM[n], W["shard_shapes"][n], W["where"])
        dh2 = reducing(functools.partial(_mm, f"mlp1_dx_{l}", "nt", dp, mats["w_mlp1", l], T, D, FF, [F32]),
                       swaps=["w_mlp2"], tiles=MM_TILES_LONG_K)
        add(["w_mlp2"])
        dw1 = gmat["w_mlp1", l] = reducing(functools.partial(_mm, f"mlp1_dw_{l}", "tn", s["h2"], dp, D, FF, T, [BF16]),
                                           part=("w_mlp2", l, 0, 2))
        dx1, dx1b, dg2 = _rms_bwd(f"rms2_bwd_{l}", s["x1"], g2[l][None], dh2, dx)

        def unmerge(acc, ga, gb, oav, obv):
            sa, sb = _sigmoid(ga), _sigmoid(gb)
            return acc * sa, acc * sb, acc * oav * sa * (1.0 - sa), acc * obv * sb * (1.0 - sb)

        doa, dob, dga, dgb = reducing(
            functools.partial(_mm, f"out_proj_dx_{l}", "nt", dx1b, mats["w_o", l], T, D, D, [BF16] * 4),
            swaps=["w_mlp1"], part=("w_mlp2", l, 1, 2), join=above, tiles=MM_TILES_FUSED, epilogue=unmerge,
            extras=[(s["z"], "tile", 5 * D), (s["z"], "tile", 6 * D), (s["oa"], "tile", 0), (s["ob"], "tile", 0)])
        add(["w_mlp1"])
        dwo = gmat["w_o", l] = _mm(f"out_proj_dw_{l}", "tn", s["mg"], dx1b, D, D, T, [BF16])
        dpa = _mm(f"proj_a_dx_{l}", "nt", doa, mats["w_pa", l], T, D, D, [F32])
        dwpa = gmat["w_pa", l] = _mm(f"proj_a_dw_{l}", "tn", s["pa"], doa, D, D, T, [BF16])
        dpb = _mm(f"proj_b_dx_{l}", "nt", dob, mats["w_pb", l], T, D, D, [F32])
        dwpb = gmat["w_pb", l] = _mm(f"proj_b_dw_{l}", "tn", s["pb"], dob, D, D, T, [BF16])
        dz, dwr, dwi, sm = reducing(
            functools.partial(_mixer_bwd, f"mixer_bwd_{l}", l, s["z"], s["xc"], s["hl"], s["vb"], dpa, dpb, dga, dgb,
                              W["pch"], W["wr"], W["wi"]),
            swaps=["w_o", "w_pa", "w_pb"], scatters=["w_mlp1"], mixer=True)
        add(["w_o", "w_pa", "w_pb"])
        dwin = gmat["w_in", l] = reducing(
            functools.partial(_mm, f"in_proj_dw_{l}", "tn", s["h"], dz, D, 7 * D, T, [BF16]),
            scatters=["w_o", "w_pa", "w_pb"])
        dbin = reducing(functools.partial(_colsum, f"bias_grad_{l}", dz), swaps=["w_in"])
        add(["w_in"])
        dh = reducing(functools.partial(_mm, f"in_proj_dx_{l}", "nt", dz, mats["w_in", l], T, D, 7 * D, [F32]),
                      scatters=["w_in"] if l == 0 else [], part=("w_in", l, 0, 2) if l > 0 else None,
                      tiles=MM_TILES_LONG_K)
        dx, dxb, dg1 = _rms_bwd(f"rms1_bwd_{l}", s["x0"], g1[l][None], dh, dx1)
        grads[l] = dict(w_in=dwin, w_pa=dwpa, w_pb=dwpb, w_o=dwo, w_mlp1=dw1, w_mlp2=dw2, wr=dwr, wi=dwi,
                        sm=sm, b_in=dbin, g1=dg1, g2=dg2)
    return loss_blk[0, 0], dx, grads, dgf, sums, landed, acc


ANY = pl.BlockSpec(memory_space=pl.ANY)


def _place():
    x, y, c = lax.axis_index("x"), lax.axis_index("y"), lax.axis_index("c")
    peers = [(1 - x, y, c), (x, 1 - y, c), (1 - x, 1 - y, c)]
    chips = [2 * (1 - x) + y, 2 * x + (1 - y), 2 * (1 - x) + (1 - y)]
    return (x, y, c), 2 * x + y, peers, chips


def _window(ref, dim, q, size):
    idx = [slice(None)] * len(ref.shape)
    idx[dim] = pl.ds(q * size, size)
    return ref.at[tuple(idx)]


def _gather_weights(shards, dims, small):
    n = len(shards)
    sizes = [s.shape[d] for s, d in zip(shards, dims)]
    full = [jax.ShapeDtypeStruct(s.shape[:d] + (s.shape[d] * N_CHIPS,) + s.shape[d + 1:], s.dtype)
            for s, d in zip(shards, dims)]
    full.append(jax.ShapeDtypeStruct((N_CHIPS,) + small.shape, small.dtype))

    def body(*refs):
        ins, outs = refs[:n + 1], refs[n + 1:2 * n + 2]
        send_sems, recv_sems, local_sems = refs[2 * n + 2:]
        _, k, peers, chips = _place()

        def dst(w, q):
            return outs[w].at[q] if w == n else _window(outs[w], dims[w], q, sizes[w])

        local = [pltpu.make_async_copy(ins[w], dst(w, k), local_sems.at[w]) for w in range(n + 1)]
        for cp in local:
            cp.start()
        sends = []
        for p, peer in enumerate(peers):
            for w in range(n + 1):
                s = p * (n + 1) + w
                sends.append(pltpu.make_async_remote_copy(
                    src_ref=ins[w], dst_ref=dst(w, k), send_sem=send_sems.at[s], recv_sem=recv_sems.at[s],
                    device_id=peer, device_id_type=MESH))
        for cp in sends:
            cp.start()
        for p, peer in enumerate(peers):
            for w in range(n + 1):
                s = p * (n + 1) + w
                pltpu.make_async_remote_copy(
                    src_ref=ins[w], dst_ref=dst(w, chips[p]), send_sem=send_sems.at[s], recv_sem=recv_sems.at[s],
                    device_id=peer, device_id_type=MESH).wait_recv()
        for cp in sends:
            cp.wait_send()
        for cp in local:
            cp.wait()

    return pl.pallas_call(
        body,
        in_specs=[ANY] * (n + 1),
        out_specs=[ANY] * (n + 1),
        out_shape=full,
        scratch_shapes=[pltpu.SemaphoreType.DMA((3 * (n + 1),)), pltpu.SemaphoreType.DMA((3 * (n + 1),)),
                        pltpu.SemaphoreType.DMA((n + 1,))],
        name="gather_weights",
    )(*shards, small)


def _scatter_grads(grads, dims):
    n, depth = len(grads), len(grads[0])
    sizes = [g[0].shape[d] // N_CHIPS for g, d in zip(grads, dims)]
    land = []
    for g, d, sz in zip(grads, dims, sizes):
        shp = g[0].shape
        land.append(jax.ShapeDtypeStruct((N_CHIPS, depth) + shp[:d] + (sz,) + shp[d + 1:], g[0].dtype))

    def body(*refs):
        ins, outs = refs[:n * depth], refs[n * depth:n * depth + n]
        send_sems, recv_sems, local_sems = refs[n * depth + n:]
        _, k, peers, chips = _place()

        def src(w, l, q):
            return _window(ins[w * depth + l], dims[w], q, sizes[w])

        local = [pltpu.make_async_copy(src(w, l, k), outs[w].at[3, l], local_sems.at[w * depth + l])
                 for w in range(n) for l in range(depth)]
        for cp in local:
            cp.start()
        sends = []
        for p, peer in enumerate(peers):
            for w in range(n):
                for l in range(depth):
                    s = (p * n + w) * depth + l
                    sends.append(pltpu.make_async_remote_copy(
                        src_ref=src(w, l, chips[p]), dst_ref=outs[w].at[p, l], send_sem=send_sems.at[s],
                        recv_sem=recv_sems.at[s], device_id=peer, device_id_type=MESH))
        for cp in sends:
            cp.start()
        for cp in sends:
            cp.wait_recv()
        for cp in sends:
            cp.wait_send()
        for cp in local:
            cp.wait()

    flat = [g for gl in grads for g in gl]
    return pl.pallas_call(
        body,
        in_specs=[ANY] * (n * depth),
        out_specs=[ANY] * n,
        out_shape=land,
        scratch_shapes=[pltpu.SemaphoreType.DMA((3 * n * depth,)), pltpu.SemaphoreType.DMA((3 * n * depth,)),
                        pltpu.SemaphoreType.DMA((n * depth,))],
        name="scatter_grads",
    )(*flat)


def _sum_slots(name, land):
    _, R, C = land.shape
    tr, tc = _div_tile(R, 512, 8), _div_tile(C, 1024)

    def body(a_ref, b_ref, c_ref, d_ref, o_ref):
        o_ref[...] = ((d_ref[...].astype(F32) + a_ref[...].astype(F32)) + b_ref[...].astype(F32)) \
            + c_ref[...].astype(F32)

    def slot(q):
        return pl.BlockSpec((None, tr, tc), lambda i, j: (q, i, j))

    return pl.pallas_call(
        body,
        grid=(R // tr, C // tc),
        in_specs=[slot(0), slot(1), slot(2), slot(3)],
        out_specs=pl.BlockSpec((tr, tc), lambda i, j: (i, j)),
        out_shape=jax.ShapeDtypeStruct((R, C), F32),
        compiler_params=_cparams(("parallel", "parallel")),
        name=name,
    )(land, land, land, land)


def _swap_with_sibling(parts):
    n = len(parts)

    def body(*refs):
        ins, outs = refs[:n], refs[n:2 * n]
        send_sems, recv_sems = refs[2 * n:]
        (x, y, c), _, _, _ = _place()
        copies = [pltpu.make_async_remote_copy(
            src_ref=ins[w], dst_ref=outs[w], send_sem=send_sems.at[w], recv_sem=recv_sems.at[w],
            device_id=(x, y, 1 - c), device_id_type=MESH) for w in range(n)]
        for cp in copies:
            cp.start()
        for cp in copies:
            cp.wait()

    return pl.pallas_call(
        body,
        in_specs=[ANY] * n,
        out_specs=[ANY] * n,
        out_shape=[jax.ShapeDtypeStruct(p.shape, p.dtype) for p in parts],
        scratch_shapes=[pltpu.SemaphoreType.DMA((n,)), pltpu.SemaphoreType.DMA((n,))],
        name="swap_with_sibling",
    )(*parts)


class _Carry:
    def __init__(self, ins, out_shapes, rounds, counts, aliases=None, marks=(0.6, 0.92)):
        self.ins, self.out_shapes, self.rounds, self.counts = list(ins), list(out_shapes), list(rounds), list(counts)
        self.aliases, self.marks = dict(aliases or {}), marks

    def scratch(self):
        return [pltpu.SemaphoreType.DMA((n,)) for n in self.counts for _ in range(2)]

    def _copies(self, r, in_refs, out_refs, sems, landing):
        remote = self.rounds[r](in_refs, out_refs)
        assert len(remote) == self.counts[r], (r, len(remote), self.counts[r])
        return [pltpu.make_async_remote_copy(src_ref=s, dst_ref=(land if landing else d), send_sem=sems[2 * r].at[i],
                                             recv_sem=sems[2 * r + 1].at[i], device_id=peer, device_id_type=MESH)
                for i, (s, d, peer, land) in enumerate(remote)]

    def begin(self, r, in_refs, out_refs, sems):
        if r > 0:
            for cp in self._copies(r - 1, in_refs, out_refs, sems, True):
                cp.wait_recv()
        for cp in self._copies(r, in_refs, out_refs, sems, False):
            cp.start()

    def end(self, in_refs, out_refs, sems):
        last = len(self.rounds) - 1
        for cp in self._copies(last, in_refs, out_refs, sems, True):
            cp.wait_recv()
        for r in range(last + 1):
            for cp in self._copies(r, in_refs, out_refs, sems, False):
                cp.wait_send()

    def ride(self, step, total, in_refs, out_refs, sems, first):
        if first:
            @pl.when(step == 0)
            def _():
                self.begin(0, in_refs, out_refs, sems)
            return
        for r in range(1, len(self.rounds)):
            @pl.when(step == min(total - 1, int(total * self.marks[r - 1])))
            def _(r=r):
                self.begin(r, in_refs, out_refs, sems)

        @pl.when(step == total - 1)
        def _():
            self.end(in_refs, out_refs, sems)


def _run_carry(name, carry):
    n_in, n_out = len(carry.ins), len(carry.out_shapes)

    def body(*refs):
        in_refs, out_refs, sems = refs[:n_in], refs[n_in:n_in + n_out], refs[n_in + n_out:]
        for r in range(len(carry.rounds)):
            carry.begin(r, in_refs, out_refs, sems)
        carry.end(in_refs, out_refs, sems)

    return pl.pallas_call(
        body,
        in_specs=[ANY] * n_in,
        out_specs=[ANY] * n_out,
        out_shape=carry.out_shapes,
        scratch_shapes=carry.scratch(),
        input_output_aliases=carry.aliases,
        name=name,
    )(*carry.ins)


def _gather_carry(arrays, items):
    shapes = [a.shape for a in arrays]

    def ring():
        x, y, c = lax.axis_index("x"), lax.axis_index("y"), lax.axis_index("c")
        first = (x + (1 - c) * (1 - 2 * x), y + c * (1 - 2 * y), c)
        second = (x + c * (1 - 2 * x), y + (1 - c) * (1 - 2 * y), c)
        return c, 2 * x + y, first, second

    def chip(pos):
        return 2 * pos[0] + pos[1]

    def round0(ins, outs):
        c, k, first, second = ring()
        remote = []
        for item in items:
            win = window(outs[item[0]], item)
            remote.append((win(2 * k + c), win(2 * k + c), first, win(2 * chip(first) + c)))
            remote.append((win(2 * k + c), win(2 * k + c), second, win(2 * chip(second) + c)))
        return remote

    def round1(ins, outs):
        c, k, first, second = ring()
        remote = []
        for item in items:
            win = window(outs[item[0]], item)
            relayed = win(2 * chip(first) + c)
            remote.append((relayed, relayed, second, win(2 * (3 - k) + c)))
        return remote

    def window(ref, item):
        idx, d, part, nparts = item
        h = shapes[idx][d] // (2 * N_CHIPS)
        rows = shapes[idx][1 - d] // nparts

        def win(j):
            sl = [None, None]
            sl[d] = pl.ds(j * h, h)
            sl[1 - d] = pl.ds(part * rows, rows)
            return ref.at[tuple(sl)]

        return win

    def round2(ins, outs):
        (x, y, c), _, _, chips = _place()
        remote = []
        for item in items:
            win = window(outs[item[0]], item)
            for p in range(3):
                remote.append((win(2 * chips[p] + c), win(2 * chips[p] + c), (x, y, 1 - c),
                               win(2 * chips[p] + 1 - c)))
        return remote

    n = len(items)
    return _Carry(arrays, [jax.ShapeDtypeStruct(a.shape, a.dtype) for a in arrays], [round0, round1, round2],
                  [2 * n, n, 3 * n], aliases={i: i for i in range(len(arrays))})


def _place_shard(name, w, layer, dim, chip):
    _, a, b = w.shape
    full = (a * N_CHIPS, b) if dim == 0 else (a, b * N_CHIPS)
    tr, tc = _div_tile(a, 512, 16), _div_tile(b, 2048)
    nr, nc = a // tr, b // tc

    def out_map(i, j, chip_ref):
        return (chip_ref[0] * nr + i, j) if dim == 0 else (i, chip_ref[0] * nc + j)

    def body(chip_ref, w_ref, o_ref):
        o_ref[...] = w_ref[...].astype(o_ref.dtype)

    return pl.pallas_call(
        body,
        grid_spec=pltpu.PrefetchScalarGridSpec(
            num_scalar_prefetch=1, grid=(nr, nc),
            in_specs=[pl.BlockSpec((None, tr, tc), lambda i, j, chip_ref: (layer, i, j))],
            out_specs=pl.BlockSpec((tr, tc), out_map)),
        out_shape=jax.ShapeDtypeStruct(full, BF16),
        compiler_params=_cparams(("parallel", "parallel")),
        name=name,
    )(chip, w)


def _half_shape(shape, dim):
    return shape[:dim] + (shape[dim] // (2 * N_CHIPS),) + shape[dim + 1:]


def _swap_carry(grads, dims):
    shapes = [jax.ShapeDtypeStruct((N_CHIPS,) + _half_shape(g.shape, d), g.dtype) for g, d in zip(grads, dims)]

    def plan(ins, outs):
        (x, y, c), _, _, _ = _place()
        remote = []
        for w, d in enumerate(dims):
            h = grads[w].shape[d] // (2 * N_CHIPS)
            for q in range(N_CHIPS):
                remote.append((_window(ins[w], d, 2 * q + 1 - c, h), outs[w].at[q], (x, y, 1 - c), outs[w].at[q]))
        return remote

    return _Carry(grads, shapes, [plan], [N_CHIPS * len(grads)])


def _scatter_carry(sums, part=0, nparts=1, land=None):
    n = len(sums)

    def plan(ins, outs):
        _, _, peers, chips = _place()
        remote = []
        for w in range(n):
            r = sums[w].shape[1] // nparts
            rows = pl.ds(part * r, r)
            for p in range(3):
                remote.append((ins[w].at[chips[p], rows], outs[w].at[p, rows], peers[p], outs[w].at[p, rows]))
        return remote

    shapes = [jax.ShapeDtypeStruct((3,) + s.shape[1:], s.dtype) for s in sums]
    if land is None:
        return _Carry(sums, shapes, [plan], [3 * n])
    return _Carry(list(sums) + list(land), shapes, [plan], [3 * n], aliases={n + w: w for w in range(n)})


def _merge_carries(carries):
    carries = [c for c in carries if c is not None]
    if len(carries) <= 1:
        return carries[0] if carries else None
    ins = [a for c in carries for a in c.ins]
    outs = [s for c in carries for s in c.out_shapes]

    def plan(in_refs, out_refs):
        remote, i0, o0 = [], 0, 0
        for c in carries:
            remote += c.rounds[0](in_refs[i0:i0 + len(c.ins)], out_refs[o0:o0 + len(c.out_shapes)])
            i0, o0 = i0 + len(c.ins), o0 + len(c.out_shapes)
        return remote

    assert all(len(c.rounds) == 1 for c in carries)
    aliases, i0, o0 = {}, 0, 0
    for c in carries:
        aliases.update({i0 + ci: o0 + co for ci, co in c.aliases.items()})
        i0, o0 = i0 + len(c.ins), o0 + len(c.out_shapes)
    return _Carry(ins, outs, [plan], [sum(c.counts[0] for c in carries)], aliases=aliases)


def _add_halves(name, g, got, dim, core):
    R, C = g.shape
    if dim == 1:
        r, cc = R, C // (2 * N_CHIPS)
    else:
        r, cc = R // (2 * N_CHIPS), C
    tr, tc = _div_tile(r, 512, 16), _div_tile(cc, 1024)
    nr, nc = r // tr, cc // tc

    def g_map(q, i, j, core_ref):
        w = 2 * q + core_ref[0]
        return (i, w * nc + j) if dim == 1 else (w * nr + i, j)

    def body(core_ref, g_ref, got_ref, o_ref):
        o_ref[...] = (g_ref[...].astype(F32) + got_ref[...].astype(F32)).astype(o_ref.dtype)

    slab = pl.BlockSpec((None, tr, tc), lambda q, i, j, core_ref: (q, i, j))
    return pl.pallas_call(
        body,
        grid_spec=pltpu.PrefetchScalarGridSpec(
            num_scalar_prefetch=1, grid=(N_CHIPS, nr, nc),
            in_specs=[pl.BlockSpec((tr, tc), g_map), slab], out_specs=slab),
        out_shape=jax.ShapeDtypeStruct((N_CHIPS, r, cc), g.dtype),
        compiler_params=_cparams(("parallel", "parallel", "parallel")),
        name=name,
    )(core, g, got)


def _reduce_into(name, sums, land, acc, layer, dim, shape, where):
    _, r, cc = sums.shape
    tr, tc = _div_tile(r, 512, 16), _div_tile(cc, 1024)
    nr, nc = r // tr, cc // tc

    def out_map(i, j, s):
        return (layer, s[1] * nr + i, j) if dim == 1 else (layer, i, s[1] * nc + j)

    def body(*refs):
        own, a_ref, b_ref, c_ref, o_ref = refs[1], refs[2], refs[3], refs[4], refs[-1]
        o_ref[...] = ((own[...].astype(F32) + a_ref[...].astype(F32)) + b_ref[...].astype(F32)) \
            + c_ref[...].astype(F32)

    def slot(p):
        return pl.BlockSpec((None, tr, tc), lambda i, j, s: (p, i, j))

    in_specs = [pl.BlockSpec((None, tr, tc), lambda i, j, s: (s[0], i, j)), slot(0), slot(1), slot(2)]
    args = [where, sums, land, land, land]
    if acc is not None:
        in_specs.append(ANY)
        args.append(acc)
    return pl.pallas_call(
        body,
        grid_spec=pltpu.PrefetchScalarGridSpec(
            num_scalar_prefetch=1, grid=(nr, nc), in_specs=in_specs,
            out_specs=pl.BlockSpec((None, tr, tc), out_map)),
        out_shape=jax.ShapeDtypeStruct(shape, F32),
        input_output_aliases={5: 0} if acc is not None else {},
        compiler_params=_cparams(("parallel", "parallel")),
        name=name,
    )(*args)


def _join_carry(grads, dims, layer):
    n = len(grads)

    def plan(ins, outs):
        (x, y, c), _, _, _ = _place()
        remote = []
        for w in range(n):
            d, h = dims[w], grads[w].shape[dims[w]] // 2

            def win(half, w=w, d=d, h=h):
                idx = [slice(None)] * len(grads[w].shape)
                idx[0], idx[d] = layer, pl.ds(half * h, h)
                return outs[w].at[tuple(idx)]

            remote.append((win(c), win(c), (x, y, 1 - c), win(1 - c)))
        return remote

    return _Carry(grads, [jax.ShapeDtypeStruct(g.shape, g.dtype) for g in grads], [plan], [n],
                  aliases={w: w for w in range(n)})


def _allreduce_small(pack):
    R, C = pack.shape

    def gather_body(in_ref, slots_ref, send_sems, recv_sems, local_sem):
        x, y, c = lax.axis_index("x"), lax.axis_index("y"), lax.axis_index("c")
        me = 4 * x + 2 * y + c
        flips = [(dx, dy, dc) for dx in (0, 1) for dy in (0, 1) for dc in (0, 1)][1:]

        def flip(v, d):
            return 1 - v if d else v

        local = pltpu.make_async_copy(in_ref, slots_ref.at[me], local_sem)
        local.start()
        sends = []
        for j, (dx, dy, dc) in enumerate(flips):
            px, py, pc = flip(x, dx), flip(y, dy), flip(c, dc)
            sends.append((pltpu.make_async_remote_copy(
                src_ref=in_ref, dst_ref=slots_ref.at[me], send_sem=send_sems.at[j], recv_sem=recv_sems.at[j],
                device_id=(px, py, pc), device_id_type=MESH), 4 * px + 2 * py + pc, j))
        for cp, _, _ in sends:
            cp.start()
        for cp, peer_id, j in sends:
            pltpu.make_async_remote_copy(
                src_ref=in_ref, dst_ref=slots_ref.at[peer_id], send_sem=send_sems.at[j], recv_sem=recv_sems.at[j],
                device_id=(x, y, c), device_id_type=MESH).wait_recv()
        for cp, _, _ in sends:
            cp.wait_send()
        local.wait()

    slots = pl.pallas_call(
        gather_body,
        in_specs=[ANY],
        out_specs=ANY,
        out_shape=jax.ShapeDtypeStruct((N_DEV, R, C), pack.dtype),
        scratch_shapes=[pltpu.SemaphoreType.DMA((N_DEV - 1,)), pltpu.SemaphoreType.DMA((N_DEV - 1,)),
                        pltpu.SemaphoreType.DMA],
        name="allgather_small",
    )(pack)

    def sum_body(s_ref, o_ref):
        acc = s_ref[0]
        for d in range(1, N_DEV):
            acc = acc + s_ref[d]
        o_ref[...] = acc

    return pl.pallas_call(
        sum_body,
        out_shape=jax.ShapeDtypeStruct((R, C), pack.dtype),
        name="sum_small",
    )(slots)


def _adamw_math(w, g, m, v):
    m2 = ADAM_B1 * m + (1.0 - ADAM_B1) * g
    v2 = ADAM_B2 * v + (1.0 - ADAM_B2) * (g * g)
    m_hat = m2 / (1.0 - ADAM_B1 ** ADAM_STEP)
    v_hat = v2 / (1.0 - ADAM_B2 ** ADAM_STEP)
    delta = -ADAM_LR * (m_hat / (jnp.sqrt(v_hat) + ADAM_EPS) + ADAM_WD * w)
    return delta, m2, v2


def _adamw_big(name, w, m, v, g_parts):
    shape = w.shape
    C = shape[-1]
    R = w.size // C
    tr, tc = _div_tile(R, 256, 8), _div_tile(C, 1024)
    n_g = len(g_parts)

    def body(*refs):
        w_ref, m_ref, v_ref = refs[:3]
        g_ref, d_ref, nm_ref, nv_ref = refs[3 + n_g:]
        g = refs[3][...]
        for extra in refs[4:3 + n_g]:
            g = g + extra[...]
        delta, m2, v2 = _adamw_math(w_ref[...], g, m_ref[...], v_ref[...])
        g_ref[...], d_ref[...], nm_ref[...], nv_ref[...] = g, delta, m2, v2

    blk = pl.BlockSpec((tr, tc), lambda i, j: (i, j))
    outs = pl.pallas_call(
        body,
        grid=(R // tr, C // tc),
        in_specs=[blk] * (3 + n_g),
        out_specs=[blk] * 4,
        out_shape=[jax.ShapeDtypeStruct((R, C), F32)] * 4,
        compiler_params=_cparams(("parallel", "parallel")),
        name=name,
    )(w.reshape(R, C), m.reshape(R, C), v.reshape(R, C), *[g.reshape(R, C) for g in g_parts])
    return [o.reshape(shape) for o in outs]


def _adamw_small(name, w, g, m, v):
    shape = w.shape
    two_d = (w.size // shape[-1], shape[-1])

    def body(w_ref, g_ref, m_ref, v_ref, d_ref, nm_ref, nv_ref):
        d_ref[...], nm_ref[...], nv_ref[...] = _adamw_math(w_ref[...], g_ref[...], m_ref[...], v_ref[...])

    outs = pl.pallas_call(
        body,
        out_shape=[jax.ShapeDtypeStruct(two_d, F32)] * 3,
        name=name,
    )(w.reshape(two_d), g.reshape(two_d), m.reshape(two_d), v.reshape(two_d))
    return [o.reshape(shape) for o in outs]


SMALL_ROWS = 40
S_BIN, S_G1, S_G2 = 16, 24, 32
MATS = ("w_in", "w_pa", "w_pb", "w_o", "w_mlp1", "w_mlp2")
LRU = ("lru_wr", "lru_wi")
BIG_DIM = dict(w_in=2, w_pa=1, w_pb=1, w_o=1, w_mlp1=2, w_mlp2=1, lru_wr=2, lru_wi=2)
WEIGHTS = ("norm1_g", "w_in", "b_in", "conv_a_w", "conv_a_b", "lru_wr", "lru_br", "lru_wi", "lru_bi", "lru_lam",
           "conv_b_w", "w_pa", "w_pb", "w_o", "norm2_g", "w_mlp1", "w_mlp2", "final_g")


def _rows_at(a, r0, total):
    pad = [(0, 0)] * a.ndim
    pad[-2] = (r0, total - r0 - a.shape[-2])
    return jnp.pad(a, pad)


def kernel(x, norm1_g, w_in, b_in, conv_a_w, conv_a_b, lru_wr, lru_br, lru_wi, lru_bi, lru_lam, conv_b_w, w_pa, w_pb, w_o, norm2_g, w_mlp1, w_mlp2, final_g, loss_target, m_norm1_g, m_w_in, m_b_in, m_conv_a_w, m_conv_a_b, m_lru_wr, m_lru_br, m_lru_wi, m_lru_bi, m_lru_lam, m_conv_b_w, m_w_pa, m_w_pb, m_w_o, m_norm2_g, m_w_mlp1, m_w_mlp2, m_final_g, v_norm1_g, v_w_in, v_b_in, v_conv_a_w, v_conv_a_b, v_lru_wr, v_lru_br, v_lru_wi, v_lru_bi, v_lru_lam, v_conv_b_w, v_w_pa, v_w_pb, v_w_o, v_norm2_g, v_w_mlp1, v_w_mlp2, v_final_g):
    wts = dict(norm1_g=norm1_g, w_in=w_in, b_in=b_in, conv_a_w=conv_a_w, conv_a_b=conv_a_b, lru_wr=lru_wr,
               lru_br=lru_br, lru_wi=lru_wi, lru_bi=lru_bi, lru_lam=lru_lam, conv_b_w=conv_b_w, w_pa=w_pa,
               w_pb=w_pb, w_o=w_o, norm2_g=norm2_g, w_mlp1=w_mlp1, w_mlp2=w_mlp2, final_g=final_g)
    mom = dict(norm1_g=m_norm1_g, w_in=m_w_in, b_in=m_b_in, conv_a_w=m_conv_a_w, conv_a_b=m_conv_a_b,
               lru_wr=m_lru_wr, lru_br=m_lru_br, lru_wi=m_lru_wi, lru_bi=m_lru_bi, lru_lam=m_lru_lam,
               conv_b_w=m_conv_b_w, w_pa=m_w_pa, w_pb=m_w_pb, w_o=m_w_o, norm2_g=m_norm2_g, w_mlp1=m_w_mlp1,
               w_mlp2=m_w_mlp2, final_g=m_final_g)
    vel = dict(norm1_g=v_norm1_g, w_in=v_w_in, b_in=v_b_in, conv_a_w=v_conv_a_w, conv_a_b=v_conv_a_b,
               lru_wr=v_lru_wr, lru_br=v_lru_br, lru_wi=v_lru_wi, lru_bi=v_lru_bi, lru_lam=v_lru_lam,
               conv_b_w=v_conv_b_w, w_pa=v_w_pa, w_pb=v_w_pb, w_o=v_w_o, norm2_g=v_norm2_g, w_mlp1=v_w_mlp1,
               w_mlp2=v_w_mlp2, final_g=v_final_g)
    depth, D = norm1_g.shape
    nb, bw = lru_wr.shape[1], lru_wr.shape[3]
    chip = 2 * lax.axis_index("x") + lax.axis_index("y")

    small_parts = [conv_a_w.reshape(-1), conv_b_w.reshape(-1), lru_br.reshape(-1), lru_bi.reshape(-1)]
    small_len = sum(p.shape[0] for p in small_parts)
    small_rows = -(-small_len // 1024) * 8
    small = jnp.concatenate(small_parts + [jnp.zeros((small_rows * 128 - small_len,), F32)]).reshape(small_rows, 128)
    gathered = _gather_weights([wts[n].astype(BF16) for n in LRU], [BIG_DIM[n] for n in LRU], small)
    full = dict(zip(LRU, gathered[:-1]))
    items = [(n, l) for l in range(depth) for n in MATS]
    mat_dims = [BIG_DIM[n] - 1 for n, _ in items]
    where = jnp.stack([chip, lax.axis_index("c")]).astype(jnp.int32)
    placed = {(n, l): _place_shard(f"place_{n}_{l}", wts[n], l, BIG_DIM[n] - 1, where) for n, l in items}
    flat = gathered[-1].reshape(N_CHIPS, small_rows * 128)
    off = 0
    small_full = []
    for part, shard in zip(small_parts, (conv_a_w, conv_b_w, lru_br, lru_bi)):
        piece = flat[:, off:off + part.shape[0]].reshape((N_CHIPS,) + shard.shape)
        small_full.append(jnp.moveaxis(piece, 0, -2).reshape(shard.shape[:-1] + (N_CHIPS * shard.shape[-1],)))
        off += part.shape[0]
    caw_f, cbw_f, br_f, bi_f = small_full
    pch = (_rows_at(conv_a_b[:, None, :], R_CAB, R_ROWS) + _rows_at(br_f.reshape(depth, 1, D), R_BR, R_ROWS)
           + _rows_at(bi_f.reshape(depth, 1, D), R_BI, R_ROWS) + _rows_at(lru_lam[:, None, :], R_LAM, R_ROWS)
           + _rows_at(caw_f, R_CAW, R_ROWS) + _rows_at(cbw_f, R_CBW, R_ROWS))
    W = dict(b_in=b_in, pch=pch, wr=full["lru_wr"], wi=full["lru_wi"], g1=norm1_g, g2=norm2_g, gf=final_g,
             core=lax.axis_index("c").astype(jnp.int32).reshape(1), where=where,
             shard_shapes={n: wts[n].shape for n in MATS})

    loss_local, dx, grads, dgf, sums, landed, acc = _local_fwd_bwd(x[0], loss_target[0], W, placed)
    loss = lax.psum(loss_local, ("x", "y", "c"))

    key = dict(w_in="w_in", w_pa="w_pa", w_pb="w_pb", w_o="w_o", w_mlp1="w1", w_mlp2="w2", lru_wr="wr", lru_wi="wi")
    out_g, out_d, out_m, out_v = {}, {}, {}, {}
    per_layer = [[grads[l][key[n]].astype(BF16) for l in range(depth)] for n in LRU]
    land = _scatter_grads(per_layer, [BIG_DIM[n] - 1 for n in LRU])
    chip_sums = [_sum_slots(f"sum_slots_{n}", ld.reshape(N_CHIPS, -1, ld.shape[-1])) for n, ld in zip(LRU, land)]
    sib_sums = _swap_with_sibling(chip_sums)
    for n, mine, sib in zip(LRU, chip_sums, sib_sums):
        out_g[n], out_d[n], out_m[n], out_v[n] = _adamw_big(f"adamw_{n}", wts[n], mom[n], vel[n], [mine, sib])
    for n in MATS:
        acc[n] = _reduce_into(f"reduce_{n}_0", sums[n, 0], landed[n, 0], acc.get(n), 0, BIG_DIM[n], wts[n].shape, where)
    joined = _run_carry("join_halves", _join_carry([acc[n] for n in MATS], [BIG_DIM[n] for n in MATS], 0))
    for n, g in zip(MATS, joined):
        out_g[n], out_d[n], out_m[n], out_v[n] = _adamw_big(f"adamw_{n}", wts[n], mom[n], vel[n], [g])

    rows = []
    for l in range(depth):
        g = grads[l]
        rows.append(_rows_at(g["sm"], 0, SMALL_ROWS) + _rows_at(g["b_in"].reshape(7, D), S_BIN, SMALL_ROWS)
                    + _rows_at(g["g1"], S_G1, SMALL_ROWS) + _rows_at(g["g2"], S_G2, SMALL_ROWS))
    rows.append(_rows_at(dgf, 0, 8))
    tot = _allreduce_small(jnp.concatenate(rows, axis=0))
    per = tot[:depth * SMALL_ROWS].reshape(depth, SMALL_ROWS, D)

    def cols_of_chip(a, axis):
        size = a.shape[axis] // N_CHIPS
        return lax.dynamic_slice_in_dim(a, chip * size, size, axis=axis)

    small_g = dict(
        norm1_g=per[:, S_G1], b_in=per[:, S_BIN:S_BIN + 7].reshape(depth, 7 * D),
        conv_a_w=cols_of_chip(per[:, R_CAW:R_CAW + 4], 2), conv_a_b=per[:, R_CAB],
        lru_br=cols_of_chip(per[:, R_BR].reshape(depth, nb, bw), 2),
        lru_bi=cols_of_chip(per[:, R_BI].reshape(depth, nb, bw), 2), lru_lam=per[:, R_LAM],
        conv_b_w=cols_of_chip(per[:, R_CBW:R_CBW + 3], 2), norm2_g=per[:, S_G2],
        final_g=tot[depth * SMALL_ROWS])
    for n, g in small_g.items():
        out_g[n] = g
        out_d[n], out_m[n], out_v[n] = _adamw_small(f"adamw_{n}", wts[n], g, mom[n], vel[n])

    return (loss, dx[None], *[out_g[n] for n in WEIGHTS], *[out_d[n] for n in WEIGHTS],
            *[out_m[n] for n in WEIGHTS], *[out_v[n] for n in WEIGHTS])
```

```python
import functools

import jax
import jax.numpy as jnp
from jax import lax
from jax.experimental import pallas as pl
from jax.experimental.pallas import tpu as pltpu

F32 = jnp.float32
BF16 = jnp.bfloat16
MESH = pl.DeviceIdType.MESH

EPS = 1e-6
LRU_C = 8.0
ADAM_LR = 0.001
ADAM_B1 = 0.9
ADAM_B2 = 0.999
ADAM_EPS = 1e-08
ADAM_WD = 0.01
ADAM_STEP = 10

N_CHIPS = 4
N_DEV = 8
HALO = 8
VMEM_LIMIT = 56 * 1024 * 1024
MM_TILES = (1024, 1024, 2048)
MM_TILES_FUSED = (512, 1024, 2048)
MM_TILES_LONG_K = (1024, 1024, 4096)
SEQ_CHUNK = 512
MIXER_ROW_BLOCK = 32
ROW_TILE = 256

R_CAB, R_BR, R_BI, R_LAM, R_CAW, R_CBW, R_ROWS = 0, 1, 2, 3, 4, 8, 16


def _cparams(sem):
    return pltpu.CompilerParams(dimension_semantics=sem, vmem_limit_bytes=VMEM_LIMIT)


def _div_tile(n, pref, unit=128):
    if n <= pref:
        return n
    t = (pref // unit) * unit
    while n % t:
        t -= unit
    return t


def _sigmoid(v):
    return 1.0 / (1.0 + jnp.exp(-v))


def _gelu_and_grad(y):
    k = 0.7978845608028654
    c = 0.044715
    y2 = y * y
    t = jnp.tanh(k * (y + c * y2 * y))
    g = 0.5 * y * (1.0 + t)
    gp = 0.5 * (1.0 + t) + 0.5 * y * (1.0 - t * t) * (k * (1.0 + 3.0 * c * y2))
    return g, gp


def _softplus_neg(lam):
    e = jnp.exp(-jnp.abs(lam))
    w = 1.0 + e
    l1p = jnp.where(w == 1.0, e, jnp.log(w) * e / jnp.where(w == 1.0, 1.0, w - 1.0))
    return jnp.maximum(-lam, 0.0) + l1p


def _mm(name, mode, a, b, M, N, K, out_dtypes, epilogue=None, extras=(), la=None, lb=None, tiles=None,
        carry=None):
    tiles = MM_TILES if tiles is None else tiles
    tm, tn, tk = _div_tile(M, tiles[0]), _div_tile(N, tiles[1]), _div_tile(K, tiles[2])
    assert M % tm == 0 and N % tn == 0 and K % tk == 0, (name, M, N, K)
    nk = K // tk

    def spec(lead, shape, imap):
        if lead is None:
            return pl.BlockSpec(shape, imap)
        return pl.BlockSpec((None,) + shape, lambda i, j, k: (lead,) + imap(i, j, k))

    if mode == "nn":
        a_spec = spec(la, (tm, tk), lambda i, j, k: (i, k))
        b_spec = spec(lb, (tk, tn), lambda i, j, k: (k, j))
        dn = (((1,), (0,)), ((), ()))
    elif mode == "nt":
        a_spec = spec(la, (tm, tk), lambda i, j, k: (i, k))
        b_spec = spec(lb, (tn, tk), lambda i, j, k: (j, k))
        dn = (((1,), (1,)), ((), ()))
    else:
        a_spec = spec(la, (tk, tm), lambda i, j, k: (k, i))
        b_spec = spec(lb, (tk, tn), lambda i, j, k: (k, j))
        dn = (((0,), (0,)), ((), ()))

    ex_arrays, ex_specs = [], []
    for arr, kind, off in extras:
        ex_arrays.append(arr)
        if kind == "bias":
            ex_specs.append(pl.BlockSpec((1, tn), lambda i, j, k: (0, j)))
        else:
            assert off % tn == 0
            ex_specs.append(pl.BlockSpec((tm, tn), lambda i, j, k, o=off // tn: (i, j + o)))
    n_ex, n_out = len(ex_arrays), len(out_dtypes)
    n_cin = len(carry.ins) if carry else 0
    n_cout = len(carry.out_shapes) if carry else 0
    n_in = 2 + n_ex + n_cin
    gi, gj = M // tm, N // tn

    def body(*refs):
        a_ref, b_ref = refs[0], refs[1]
        ex = refs[2:2 + n_ex]
        outs = refs[n_in:n_in + n_out]
        acc = refs[n_in + n_out + n_cout]
        i, j, k = pl.program_id(0), pl.program_id(1), pl.program_id(2)
        if carry:
            c_in, c_out = refs[2 + n_ex:n_in], refs[n_in + n_out:n_in + n_out + n_cout]
            sems = refs[n_in + n_out + n_cout + 1:]
            step = (i * gj + j) * nk + k
            carry.ride(step, gi * gj * nk, c_in, c_out, sems, True)

        def product():
            return lax.dot_general(a_ref[...], b_ref[...], dn, preferred_element_type=F32)

        def finish(r):
            vals = (r,) if epilogue is None else epilogue(r, *[e[...] for e in ex])
            for o, v in zip(outs, vals):
                o[...] = v.astype(o.dtype)

        if nk == 1:
            finish(product())
        else:
            @pl.when(k == 0)
            def _():
                acc[...] = product()

            @pl.when((k > 0) & (k < nk - 1))
            def _():
                acc[...] += product()

            @pl.when(k == nk - 1)
            def _():
                finish(acc[...] + product())

        if carry:
            carry.ride(step, gi * gj * nk, c_in, c_out, sems, False)

    res = pl.pallas_call(
        body,
        grid=(gi, gj, nk),
        in_specs=[a_spec, b_spec, *ex_specs] + [ANY] * n_cin,
        out_specs=[pl.BlockSpec((tm, tn), lambda i, j, k: (i, j)) for _ in range(n_out)] + [ANY] * n_cout,
        out_shape=[jax.ShapeDtypeStruct((M, N), d) for d in out_dtypes] + (carry.out_shapes if carry else []),
        scratch_shapes=[pltpu.VMEM((tm, tn) if nk > 1 else (8, 128), F32)] + (carry.scratch() if carry else []),
        input_output_aliases={2 + n_ex + ci: n_out + co for ci, co in carry.aliases.items()} if carry else {},
        compiler_params=_cparams(("arbitrary",) * 3 if carry else ("parallel", "parallel", "arbitrary")),
        name=name,
    )(a, b, *ex_arrays, *(carry.ins if carry else []))
    main = res[0] if n_out == 1 else res[:n_out]
    return (main, res[n_out:]) if carry else main


def _rms_fwd(name, x, g_row):
    T, D = x.shape
    tm = min(ROW_TILE, T)

    def body(x_ref, g_ref, h_ref):
        xv = x_ref[...]
        r = lax.rsqrt(jnp.mean(xv * xv, axis=-1, keepdims=True) + EPS)
        h_ref[...] = (xv * r * g_ref[...]).astype(BF16)

    return pl.pallas_call(
        body,
        grid=(T // tm,),
        in_specs=[pl.BlockSpec((tm, D), lambda i: (i, 0)), pl.BlockSpec((1, D), lambda i: (0, 0))],
        out_specs=pl.BlockSpec((tm, D), lambda i: (i, 0)),
        out_shape=jax.ShapeDtypeStruct((T, D), BF16),
        compiler_params=_cparams(("parallel",)),
        name=name,
    )(x, g_row)


def _rms_bwd(name, x, g_row, dh, dres):
    T, D = x.shape
    tm = min(ROW_TILE, T)

    def body(x_ref, g_ref, dh_ref, dres_ref, dx_ref, dxb_ref, dg_ref):
        xv, dhv = x_ref[...], dh_ref[...]
        r = lax.rsqrt(jnp.mean(xv * xv, axis=-1, keepdims=True) + EPS)
        gd = g_ref[...] * dhv
        c = jnp.mean(xv * gd, axis=-1, keepdims=True)
        dx = r * gd - xv * (r * r * r) * c + dres_ref[...]
        dx_ref[...] = dx
        dxb_ref[...] = dx.astype(BF16)

        @pl.when(pl.program_id(0) == 0)
        def _():
            dg_ref[...] = jnp.zeros_like(dg_ref)

        dg_ref[...] += jnp.sum(dhv * xv * r, axis=0, keepdims=True)

    row = pl.BlockSpec((tm, D), lambda i: (i, 0))
    vec = pl.BlockSpec((1, D), lambda i: (0, 0))
    return pl.pallas_call(
        body,
        grid=(T // tm,),
        in_specs=[row, vec, row, row],
        out_specs=[row, row, vec],
        out_shape=[jax.ShapeDtypeStruct((T, D), F32), jax.ShapeDtypeStruct((T, D), BF16),
                   jax.ShapeDtypeStruct((1, D), F32)],
        compiler_params=_cparams(("arbitrary",)),
        name=name,
    )(x, g_row, dh, dres)


def _loss_head(name, x, g_row, tgt):
    T, D = x.shape
    tm = min(ROW_TILE, T)

    def body(x_ref, g_ref, t_ref, dx_ref, dxb_ref, dg_ref, loss_ref):
        xv, g = x_ref[...], g_ref[...]
        r = lax.rsqrt(jnp.mean(xv * xv, axis=-1, keepdims=True) + EPS)
        xh = xv * r
        e = xh * g - t_ref[...]
        lpart = 0.5 * jnp.sum(jnp.mean(e * e, axis=-1, keepdims=True))
        dy = e * (1.0 / D)
        gd = g * dy
        c = jnp.mean(xv * gd, axis=-1, keepdims=True)
        dx = r * gd - xv * (r * r * r) * c
        dx_ref[...] = dx
        dxb_ref[...] = dx.astype(BF16)

        @pl.when(pl.program_id(0) == 0)
        def _():
            dg_ref[...] = jnp.zeros_like(dg_ref)
            loss_ref[...] = jnp.zeros_like(loss_ref)

        dg_ref[...] += jnp.sum(dy * xh, axis=0, keepdims=True)
        loss_ref[...] += jnp.full(loss_ref.shape, lpart, F32)

    row = pl.BlockSpec((tm, D), lambda i: (i, 0))
    vec = pl.BlockSpec((1, D), lambda i: (0, 0))
    return pl.pallas_call(
        body,
        grid=(T // tm,),
        in_specs=[row, vec, row],
        out_specs=[row, row, vec, pl.BlockSpec((8, 128), lambda i: (0, 0))],
        out_shape=[jax.ShapeDtypeStruct((T, D), F32), jax.ShapeDtypeStruct((T, D), BF16),
                   jax.ShapeDtypeStruct((1, D), F32), jax.ShapeDtypeStruct((8, 128), F32)],
        compiler_params=_cparams(("arbitrary",)),
        name=name,
    )(x, g_row, tgt)


def _colsum(name, a, carry=None):
    T, N = a.shape
    tm, tn = min(512, T), _div_tile(N, 2048)
    gj, gi = N // tn, T // tm
    n_cin = len(carry.ins) if carry else 0
    n_cout = len(carry.out_shapes) if carry else 0

    def body(*refs):
        a_ref, o_ref = refs[0], refs[1 + n_cin]
        j, i = pl.program_id(0), pl.program_id(1)
        if carry:
            c_in, c_out, sems = refs[1:1 + n_cin], refs[2 + n_cin:2 + n_cin + n_cout], refs[2 + n_cin + n_cout:]
            carry.ride(j * gi + i, gj * gi, c_in, c_out, sems, True)

        @pl.when(i == 0)
        def _():
            o_ref[...] = jnp.zeros_like(o_ref)

        o_ref[...] += jnp.sum(a_ref[...].astype(F32), axis=0, keepdims=True)

        if carry:
            carry.ride(j * gi + i, gj * gi, c_in, c_out, sems, False)

    res = pl.pallas_call(
        body,
        grid=(gj, gi),
        in_specs=[pl.BlockSpec((tm, tn), lambda j, i: (i, j))] + [ANY] * n_cin,
        out_specs=[pl.BlockSpec((1, tn), lambda j, i: (0, j))] + [ANY] * n_cout,
        out_shape=[jax.ShapeDtypeStruct((1, N), F32)] + (carry.out_shapes if carry else []),
        scratch_shapes=carry.scratch() if carry else [],
        compiler_params=_cparams(("arbitrary", "arbitrary") if carry else ("parallel", "arbitrary")),
        name=name,
    )(a, *(carry.ins if carry else []))
    return (res[0], res[1:]) if carry else res[0]


def _tile_scan(a, b, row, reverse):
    for s in (1, 2, 4):
        if reverse:
            a_s, b_s, m = pltpu.roll(a, 8 - s, 0), pltpu.roll(b, 8 - s, 0), row < 8 - s
        else:
            a_s, b_s, m = pltpu.roll(a, s, 0), pltpu.roll(b, s, 0), row >= s
        b = jnp.where(m, a * b_s + b, b)
        a = jnp.where(m, a * a_s, a)
    return a, b


def _chunk_scan(a_s, b_s, out_ref, carry, n_tiles, width, reverse):
    row = lax.broadcasted_iota(jnp.int32, (8, width), 0)
    edge = 0 if reverse else 7

    group = 4 if n_tiles % 4 == 0 else 1

    def step(j, c):
        jj = (n_tiles // group - 1 - j) if reverse else j
        base = pl.multiple_of(jj * (8 * group), 8 * group)
        order = range(group - 1, -1, -1) if reverse else range(group)
        parts = {t: _tile_scan(a_s[pl.ds(base + 8 * t, 8), :], b_s[pl.ds(base + 8 * t, 8), :], row, reverse)
                 for t in order}
        for t in order:
            h = parts[t][0] * c + parts[t][1]
            out_ref[pl.ds(base + 8 * t, 8), :] = h
            c = jnp.broadcast_to(h[edge:edge + 1, :], (8, width))
        return c

    carry[...] = lax.fori_loop(0, n_tiles // group, step, carry[...])


def _mixer_specs(Tc, bw, nb, layer):
    def seg(s):
        return pl.BlockSpec((Tc, bw), lambda n, i: (i, s * nb + n))

    p_spec = pl.BlockSpec((None, R_ROWS, bw), lambda n, i: (layer, 0, n))
    w_spec = pl.BlockSpec((None, None, bw, bw), lambda n, i: (layer, n, 0, 0))
    return seg, p_spec, w_spec


def _mixer_fwd(name, layer, z, pch, wr, wi, comm=None):
    T, D = z.shape[0], z.shape[1] // 7
    bw, nb = wr.shape[-1], wr.shape[1]
    Tc = min(SEQ_CHUNK, T)
    nT = T // Tc
    n_cin = len(comm.ins) if comm else 0
    n_cout = len(comm.out_shapes) if comm else 0

    def body(*refs):
        xa_ref, ya_ref, cb_ref, cc_ref, cx_ref, p_ref, wr_ref, wi_ref = refs[:8]
        pa_ref, pb_ref, xc_ref, hl_ref, vb_ref = refs[8 + n_cin:13 + n_cin]
        xa_buf, u_buf, a_s, b_s, carry = refs[13 + n_cin + n_cout:18 + n_cin + n_cout]
        if comm:
            c_in, c_out, sems = refs[8:8 + n_cin], refs[13 + n_cin:13 + n_cin + n_cout], refs[18 + n_cin + n_cout:]
            step = pl.program_id(0) * nT + pl.program_id(1)
            comm.ride(step, nb * nT, c_in, c_out, sems, True)

        @pl.when(pl.program_id(1) == 0)
        def _():
            xa_buf[0:HALO, :] = jnp.zeros((HALO, bw), F32)
            u_buf[0:HALO, :] = jnp.zeros((HALO, bw), F32)
            carry[...] = jnp.zeros_like(carry)

        rb = min(MIXER_ROW_BLOCK, Tc)
        blocks = [(g * rb, slice(g * rb, (g + 1) * rb)) for g in range(Tc // rb)]

        def prow(k):
            return p_ref[k:k + 1, :]

        xa_buf[HALO:HALO + Tc, :] = xa_ref[...]
        u_buf[HALO:HALO + Tc, :] = cc_ref[...] * cx_ref[...]
        for r0, rs in blocks:
            xc = prow(R_CAB)
            for k in range(4):
                xc = xc + prow(R_CAW + k) * xa_buf[HALO - 3 + k + r0:HALO - 3 + k + r0 + rb, :]
            xc_ref[rs, :] = xc
        xcb = xc_ref[...].astype(BF16)
        a_s[...] = jnp.dot(xcb, wr_ref[...], preferred_element_type=F32)
        b_s[...] = jnp.dot(xcb, wi_ref[...], preferred_element_type=F32)
        sp = _softplus_neg(prow(R_LAM))
        for _, rs in blocks:
            r = _sigmoid(a_s[rs, :] + prow(R_BR))
            ig = _sigmoid(b_s[rs, :] + prow(R_BI))
            log_a = (-LRU_C) * r * sp
            t = jnp.tanh(log_a)
            a_s[rs, :] = jnp.exp(log_a)
            b_s[rs, :] = jnp.sqrt(-2.0 * t / (1.0 - t)) * (ig * xc_ref[rs, :])
        _chunk_scan(a_s, b_s, hl_ref, carry, Tc // 8, bw, False)
        for r0, rs in blocks:
            g, _ = _gelu_and_grad(ya_ref[rs, :])
            pa_ref[rs, :] = (hl_ref[rs, :] * g).astype(BF16)
            vb = jnp.zeros((rb, bw), F32)
            for k in range(3):
                vb = vb + prow(R_CBW + k) * u_buf[HALO - 2 + k + r0:HALO - 2 + k + r0 + rb, :]
            vb_ref[rs, :] = vb
            pb_ref[rs, :] = (cb_ref[rs, :] * vb).astype(BF16)
        xa_buf[0:HALO, :] = xa_buf[Tc:Tc + HALO, :]
        u_buf[0:HALO, :] = u_buf[Tc:Tc + HALO, :]

        if comm:
            comm.ride(step, nb * nT, c_in, c_out, sems, False)

    seg, p_spec, w_spec = _mixer_specs(Tc, bw, nb, layer)
    out = pl.BlockSpec((Tc, bw), lambda n, i: (i, n))
    res = pl.pallas_call(
        body,
        grid=(nb, nT),
        in_specs=[seg(0), seg(1), seg(2), seg(3), seg(4), p_spec, w_spec, w_spec] + [ANY] * n_cin,
        out_specs=[out] * 5 + [ANY] * n_cout,
        out_shape=[jax.ShapeDtypeStruct((T, D), BF16), jax.ShapeDtypeStruct((T, D), BF16),
                   jax.ShapeDtypeStruct((T, D), F32), jax.ShapeDtypeStruct((T, D), F32),
                   jax.ShapeDtypeStruct((T, D), F32)] + (comm.out_shapes if comm else []),
        scratch_shapes=[pltpu.VMEM((Tc + HALO, bw), F32), pltpu.VMEM((Tc + HALO, bw), F32),
                        pltpu.VMEM((Tc, bw), F32), pltpu.VMEM((Tc, bw), F32), pltpu.VMEM((8, bw), F32)]
        + (comm.scratch() if comm else []),
        input_output_aliases={8 + ci: 5 + co for ci, co in comm.aliases.items()} if comm else {},
        compiler_params=_cparams(("arbitrary", "arbitrary") if comm else ("parallel", "arbitrary")),
        name=name,
    )(z, z, z, z, z, pch, wr, wi, *(comm.ins if comm else []))
    return (res[:5], res[5:]) if comm else res


def _mixer_bwd(name, layer, z, xc, hl, vb, dpa, dpb, dga, dgb, pch, wr, wi, comm=None):
    T, D = z.shape[0], z.shape[1] // 7
    bw, nb = wr.shape[-1], wr.shape[1]
    Tc = min(SEQ_CHUNK, T)
    nT = T // Tc
    tpc = Tc // 8
    rb = min(MIXER_ROW_BLOCK, Tc)
    n_cin = len(comm.ins) if comm else 0
    n_cout = len(comm.out_shapes) if comm else 0
    total = nb * nT

    def body(*refs):
        (xa_ref, ya_ref, cb_ref, cc_ref, cx_ref, xc_ref, hl_ref, hp_ref, vb_ref, dpa_ref, dpb_ref, dga_ref, dgb_ref,
         p_ref, wr_ref, wi_ref) = refs[:16]
        dz_ref, dwr_ref, dwi_ref, sm_ref = refs[16 + n_cin:20 + n_cin]
        (h_buf, a_buf, dxc_buf, dvb_buf, a_s, d_s, lam_s, r_s, i_s, m_s, dpr_s, dpi_s, sm8,
         carry, stage, out_sems) = refs[20 + n_cin + n_cout:36 + n_cin + n_cout]
        i = pl.program_id(1)
        step = pl.program_id(0) * nT + i
        if comm:
            c_in, c_out = refs[16:16 + n_cin], refs[20 + n_cin:20 + n_cin + n_cout]
            sems = refs[36 + n_cin + n_cout:]
            comm.ride(step, total, c_in, c_out, sems, True)

        slot = step % 2

        def out_copies(sl):
            rows = pl.ds(pl.multiple_of((nT - 1 - i) * Tc, Tc), Tc)
            return [pltpu.make_async_copy(
                stage.at[sl, s], dz_ref.at[rows, pl.ds(pl.multiple_of((s * nb + pl.program_id(0)) * bw, bw), bw)],
                out_sems.at[sl, s]) for s in range(7)]

        @pl.when(step >= 2)
        def _():
            for cp in out_copies(slot):
                cp.wait()

        dxa_ref, dya_ref, dcb_ref, dcc_ref, dcx_ref = [stage.at[slot, s] for s in range(5)]
        stage[slot, 5, :, :] = dga_ref[...]
        stage[slot, 6, :, :] = dgb_ref[...]

        @pl.when(i == 0)
        def _():
            a_buf[Tc:Tc + HALO, :] = jnp.zeros((HALO, bw), F32)
            dxc_buf[Tc:Tc + HALO, :] = jnp.zeros((HALO, bw), F32)
            dvb_buf[Tc:Tc + HALO, :] = jnp.zeros((HALO, bw), F32)
            carry[...] = jnp.zeros_like(carry)
            dwr_ref[...] = jnp.zeros_like(dwr_ref)
            dwi_ref[...] = jnp.zeros_like(dwi_ref)
            sm8[...] = jnp.zeros_like(sm8)

        blocks = [(g * rb, slice(g * rb, (g + 1) * rb)) for g in range(Tc // rb)]

        def prow(k):
            return p_ref[k:k + 1, :]

        def part8(v):
            return jnp.sum(v.reshape(rb // 8, 8, bw), axis=0)

        sums8 = {}

        def tally(k, v):
            sums8[k] = sums8[k] + part8(v) if k in sums8 else part8(v)

        sp = _softplus_neg(prow(R_LAM))
        xcb = xc_ref[...].astype(BF16)
        r_s[...] = jnp.dot(xcb, wr_ref[...], preferred_element_type=F32)
        i_s[...] = jnp.dot(xcb, wi_ref[...], preferred_element_type=F32)
        for _, rs in blocks:
            r = _sigmoid(r_s[rs, :] + prow(R_BR))
            ig = _sigmoid(i_s[rs, :] + prow(R_BI))
            log_a = (-LRU_C) * r * sp
            t = jnp.tanh(log_a)
            r_s[rs, :] = r
            i_s[rs, :] = ig
            m_s[rs, :] = jnp.sqrt(-2.0 * t / (1.0 - t))
            a_buf[rs, :] = jnp.exp(log_a)
            g, gp = _gelu_and_grad(ya_ref[rs, :])
            dpav = dpa_ref[rs, :]
            d_s[rs, :] = dpav * g
            dya_ref[rs, :] = (dpav * hl_ref[rs, :] * gp).astype(BF16)
            dpbv = dpb_ref[rs, :]
            dcb_ref[rs, :] = (dpbv * vb_ref[rs, :]).astype(BF16)
            dvb_buf[rs, :] = dpbv * cb_ref[rs, :]

        a_s[...] = a_buf[1:Tc + 1, :]
        _chunk_scan(a_s, d_s, lam_s, carry, tpc, bw, True)
        h_buf[HALO:HALO + Tc, :] = hl_ref[...]
        h_buf[0:HALO, :] = jnp.where(i == nT - 1, 0.0, hp_ref[...])

        for r0, rs in blocks:
            lamv, xcv, r, ig, a = lam_s[rs, :], xc_ref[rs, :], r_s[rs, :], i_s[rs, :], a_buf[rs, :]
            mult = m_s[rs, :]
            da = lamv * h_buf[HALO - 1 + r0:HALO - 1 + r0 + rb, :]
            dmult = lamv * (ig * xcv)
            dbx = lamv * mult
            dig = dbx * xcv
            dxc_buf[rs, :] = dbx * ig
            dlog_a = da * a - dmult * (a * a) / mult
            dpr = (dlog_a * ((-LRU_C) * sp)) * r * (1.0 - r)
            dpi = dig * ig * (1.0 - ig)
            dpr_s[rs, :] = dpr.astype(BF16)
            dpi_s[rs, :] = dpi.astype(BF16)
            tally(R_BR, dpr)
            tally(R_BI, dpi)
            tally(R_LAM, dlog_a * ((-LRU_C) * r))

        dprb, dpib = dpr_s[...], dpi_s[...]
        nt = (((1,), (1,)), ((), ()))
        tn = (((0,), (0,)), ((), ()))
        dxc_buf[0:Tc, :] += (lax.dot_general(dprb, wr_ref[...], nt, preferred_element_type=F32)
                             + lax.dot_general(dpib, wi_ref[...], nt, preferred_element_type=F32))
        dwr_ref[...] += lax.dot_general(xcb, dprb, tn, preferred_element_type=F32)
        dwi_ref[...] += lax.dot_general(xcb, dpib, tn, preferred_element_type=F32)

        for r0, rs in blocks:
            xav = xa_ref[rs, :]
            tally(R_CAB, dxc_buf[rs, :])
            dxa = jnp.zeros((rb, bw), F32)
            for k in range(4):
                sh = dxc_buf[3 - k + r0:3 - k + r0 + rb, :]
                dxa = dxa + prow(R_CAW + k) * sh
                tally(R_CAW + k, xav * sh)
            dxa_ref[rs, :] = dxa.astype(BF16)
            ccv, cxv = cc_ref[rs, :], cx_ref[rs, :]
            u = ccv * cxv
            du = jnp.zeros((rb, bw), F32)
            for k in range(3):
                sh = dvb_buf[2 - k + r0:2 - k + r0 + rb, :]
                du = du + prow(R_CBW + k) * sh
                tally(R_CBW + k, u * sh)
            dcc_ref[rs, :] = (du * cxv).astype(BF16)
            dcx_ref[rs, :] = (du * ccv).astype(BF16)

        for k, v in sums8.items():
            sm8[8 * k:8 * k + 8, :] += v
        a_buf[Tc:Tc + HALO, :] = a_buf[0:HALO, :]
        dxc_buf[Tc:Tc + HALO, :] = dxc_buf[0:HALO, :]
        dvb_buf[Tc:Tc + HALO, :] = dvb_buf[0:HALO, :]

        @pl.when(i == nT - 1)
        def _():
            sm_ref[...] = jnp.sum(sm8[...].reshape(R_ROWS, 8, bw), axis=1)
            sm_ref[R_LAM:R_LAM + 1, :] = sm_ref[R_LAM:R_LAM + 1, :] * (-_sigmoid(-prow(R_LAM)))

        for cp in out_copies(slot):
            cp.start()

        @pl.when(step == total - 1)
        def _():
            if total > 1:
                for cp in out_copies(1 - slot):
                    cp.wait()
            for cp in out_copies(slot):
                cp.wait()

        if comm:
            comm.ride(step, total, c_in, c_out, sems, False)

    def seg(s):
        return pl.BlockSpec((Tc, bw), lambda n, i: (nT - 1 - i, s * nb + n))

    blk = pl.BlockSpec((Tc, bw), lambda n, i: (nT - 1 - i, n))
    halo = pl.BlockSpec((8, bw), lambda n, i: (jnp.maximum((nT - 1 - i) * tpc - 1, 0), n))
    p_spec = pl.BlockSpec((None, R_ROWS, bw), lambda n, i: (layer, 0, n))
    w_spec = pl.BlockSpec((None, None, bw, bw), lambda n, i: (layer, n, 0, 0))
    dw_spec = pl.BlockSpec((None, bw, bw), lambda n, i: (n, 0, 0))
    res = pl.pallas_call(
        body,
        grid=(nb, nT),
        in_specs=[seg(0), seg(1), seg(2), seg(3), seg(4), blk, blk, halo, blk, blk, blk, blk, blk,
                  p_spec, w_spec, w_spec] + [ANY] * n_cin,
        out_specs=[ANY, dw_spec, dw_spec, pl.BlockSpec((R_ROWS, bw), lambda n, i: (0, n))] + [ANY] * n_cout,
        out_shape=[jax.ShapeDtypeStruct((T, 7 * D), BF16), jax.ShapeDtypeStruct((nb, bw, bw), F32),
                   jax.ShapeDtypeStruct((nb, bw, bw), F32), jax.ShapeDtypeStruct((R_ROWS, D), F32)]
        + (comm.out_shapes if comm else []),
        scratch_shapes=[pltpu.VMEM((Tc + HALO, bw), F32)] * 4 + [pltpu.VMEM((Tc, bw), F32)] * 6
        + [pltpu.VMEM((Tc, bw), BF16)] * 2 + [pltpu.VMEM((8 * R_ROWS, bw), F32), pltpu.VMEM((8, bw), F32),
                                              pltpu.VMEM((2, 7, Tc, bw), BF16), pltpu.SemaphoreType.DMA((2, 7))]
        + (comm.scratch() if comm else []),
        input_output_aliases={16 + ci: 4 + co for ci, co in comm.aliases.items()} if comm else {},
        compiler_params=_cparams(("arbitrary", "arbitrary")),
        name=name,
    )(z, z, z, z, z, xc, hl, hl, vb, dpa, dpb, dga, dgb, pch, wr, wi, *(comm.ins if comm else []))
    return (res[:4], res[4:]) if comm else res


def _local_fwd_bwd(x, tgt, W, placed=None):
    T, D = x.shape
    g1, g2 = W["g1"], W["g2"]
    depth = g1.shape[0]
    FF = 4 * D
    if placed is None:
        mats = {(n, l): W[n][l] for n in MATS for l in range(depth)}
    else:
        mats = {}
        mats["w_in", 0], = _run_carry("gather_first", _gather_carry([placed["w_in", 0]], [(0, 1, 0, 1)]))

    def gathering(specs):
        if placed is None or not specs:
            return None, []
        keys = [(n, l) for n, l, _, _, _ in specs]
        arrays = [mats.get(k, placed[k]) for k in keys]
        return _gather_carry(arrays, [(i, d, part, nparts) for i, (_, _, d, part, nparts) in enumerate(specs)]), keys

    def hosted(call, specs, **kw):
        carry, keys = gathering(specs)
        if carry is None:
            return call(**kw)
        res, got = call(**kw, **{("comm" if call.func is _mixer_fwd else "carry"): carry})
        mats.update(zip(keys, got))
        return res

    saved = []
    xs = x
    for l in range(depth):
        h = _rms_fwd(f"rms1_fwd_{l}", xs, g1[l][None])
        nxt = l + 1 < depth
        projs = [("w_pa", l, 0, 0, 1), ("w_pb", l, 0, 0, 1), ("w_o", l, 0, 0, 1)]
        z = hosted(functools.partial(_mm, f"in_proj_{l}", "nn", h, mats["w_in", l], T, 7 * D, D, [F32]),
                   projs + ([("w_mlp1", l, 1, 0, 1)] if nxt else []),
                   epilogue=lambda acc, b: (acc + b,), extras=[(W["b_in"][l][None], "bias", 0)])
        pa, pb, xc, hl, vb = hosted(
            functools.partial(_mixer_fwd, f"mixer_fwd_{l}", l, z, W["pch"], W["wr"], W["wi"]),
            [("w_mlp2", l, 0, 0, 1)] if nxt else [("w_mlp1", l, 1, 0, 1)])
        oa = _mm(f"proj_a_{l}", "nn", pa, mats["w_pa", l], T, D, D, [F32])

        def merge(acc, oav, ga, gb):
            return acc, _sigmoid(ga) * oav + _sigmoid(gb) * acc

        ob, mg = _mm(f"proj_b_merge_{l}", "nn", pb, mats["w_pb", l], T, D, D, [F32, BF16], epilogue=merge,
                     tiles=MM_TILES_FUSED,
                     extras=[(oa, "tile", 0), (z, "tile", 5 * D), (z, "tile", 6 * D)])
        x1 = _mm(f"out_proj_{l}", "nn", mg, mats["w_o", l], T, D, D, [F32],
                 epilogue=lambda acc, res: (res + acc,), extras=[(xs, "tile", 0)])
        h2 = _rms_fwd(f"rms2_fwd_{l}", x1, g2[l][None])

        def relu2(acc):
            pr = jnp.maximum(acc, 0.0)
            return pr * pr, pr

        u, pr = hosted(functools.partial(_mm, f"mlp1_{l}", "nn", h2, mats["w_mlp1", l], T, FF, D, [BF16, BF16]),
                       [("w_in", l + 1, 1, 0, 2)] if nxt else [("w_mlp2", l, 0, 0, 1)], epilogue=relu2)
        x2 = hosted(functools.partial(_mm, f"mlp2_{l}", "nn", u, mats["w_mlp2", l], T, D, FF, [F32]),
                    [("w_in", l + 1, 1, 1, 2)] if nxt else [],
                    epilogue=lambda acc, res: (res + acc,), extras=[(x1, "tile", 0)])
        saved.append(dict(x0=xs, h=h, z=z, pa=pa, pb=pb, xc=xc, hl=hl, vb=vb, oa=oa, ob=ob, mg=mg, x1=x1,
                          h2=h2, u=u, pr=pr))
        xs = x2

    dx, dxb, dgf, loss_blk = _loss_head("loss_head", xs, W["gf"][None], tgt)

    gmat, got, sums, landed, acc = {}, {}, {}, {}, {}

    def reducing(call, swaps=(), scatters=(), part=None, join=None, mixer=False, **kw):
        swaps, scatters = [(n, l) for n in swaps], [(n, l) for n in scatters]
        carries, sinks = [], []
        if placed is not None and swaps:
            carries.append(_swap_carry([gmat[k] for k in swaps], [BIG_DIM[k[0]] - 1 for k in swaps]))
            sinks.append((got, swaps))
        if placed is not None and scatters:
            carries.append(_scatter_carry([sums[k] for k in scatters]))
            sinks.append((landed, scatters))
        if placed is not None and part is not None:
            key = part[:2]
            carries.append(_scatter_carry([sums[key]], part[2], part[3], [landed[key]] if key in landed else None))
            sinks.append((landed, [key]))
        if placed is not None and join is not None:
            carries.append(_join_carry([acc[n] for n in MATS], [BIG_DIM[n] for n in MATS], join))
            sinks.append((acc, list(MATS)))
        if not carries:
            return call(**kw)
        res, moved = call(**kw, **{("comm" if mixer else "carry"): _merge_carries(carries)})
        moved = list(moved)
        for target, keys in sinks:
            for k in keys:
                target[k] = moved.pop(0)
        return res

    def add(names):
        if placed is not None:
            for n in names:
                sums[n, l] = _add_halves(f"add_halves_{n}_{l}", gmat[n, l], got[n, l], BIG_DIM[n] - 1, W["core"])

    grads = [None] * depth
    for l in reversed(range(depth)):
        s = saved[l]
        dp = _mm(f"mlp2_dx_{l}", "nt", dxb, mats["w_mlp2", l], T, FF, D, [BF16],
                 epilogue=lambda acc, prv: (2.0 * prv.astype(F32) * acc,), extras=[(s["pr"], "tile", 0)])
        dw2 = gmat["w_mlp2", l] = reducing(functools.partial(_mm, f"mlp2_dw_{l}", "tn", s["u"], dxb, FF, D, T, [BF16]),
                                           part=("w_in", l + 1, 1, 2) if l + 1 < depth else None)
        above = l + 1 if placed is not None and l + 1 < depth else None
        if above is not None:
            for n in MATS:
                acc[n] = _reduce_into(f"reduce_{n}_{above}", sums[n, above], landed[n, above], acc.get(n), above,
                                      BIG_DIM[n], W["shard_shapes"][n], W["where"])
        dh2 = reducing(functools.partial(_mm, f"mlp1_dx_{l}", "nt", dp, mats["w_mlp1", l], T, D, FF, [F32]),
                       swaps=["w_mlp2"], tiles=MM_TILES_LONG_K)
        add(["w_mlp2"])
        dw1 = gmat["w_mlp1", l] = reducing(functools.partial(_mm, f"mlp1_dw_{l}", "tn", s["h2"], dp, D, FF, T, [BF16]),
                                           part=("w_mlp2", l, 0, 2))
        dx1, dx1b, dg2 = _rms_bwd(f"rms2_bwd_{l}", s["x1"], g2[l][None], dh2, dx)

        def unmerge(acc, ga, gb, oav, obv):
            sa, sb = _sigmoid(ga), _sigmoid(gb)
            return acc * sa, acc * sb, acc * oav * sa * (1.0 - sa), acc * obv * sb * (1.0 - sb)

        doa, dob, dga, dgb = reducing(
            functools.partial(_mm, f"out_proj_dx_{l}", "nt", dx1b, mats["w_o", l], T, D, D, [BF16] * 4),
            swaps=["w_mlp1"], part=("w_mlp2", l, 1, 2), join=above, tiles=MM_TILES_FUSED, epilogue=unmerge,
            extras=[(s["z"], "tile", 5 * D), (s["z"], "tile", 6 * D), (s["oa"], "tile", 0), (s["ob"], "tile", 0)])
        add(["w_mlp1"])
        dwo = gmat["w_o", l] = _mm(f"out_proj_dw_{l}", "tn", s["mg"], dx1b, D, D, T, [BF16])
        dpa = _mm(f"proj_a_dx_{l}", "nt", doa, mats["w_pa", l], T, D, D, [F32])
        dwpa = gmat["w_pa", l] = _mm(f"proj_a_dw_{l}", "tn", s["pa"], doa, D, D, T, [BF16])
        dpb = _mm(f"proj_b_dx_{l}", "nt", dob, mats["w_pb", l], T, D, D, [F32])
        dwpb = gmat["w_pb", l] = _mm(f"proj_b_dw_{l}", "tn", s["pb"], dob, D, D, T, [BF16])
        dz, dwr, dwi, sm = reducing(
            functools.partial(_mixer_bwd, f"mixer_bwd_{l}", l, s["z"], s["xc"], s["hl"], s["vb"], dpa, dpb, dga, dgb,
                              W["pch"], W["wr"], W["wi"]),
            swaps=["w_o", "w_pa", "w_pb"], scatters=["w_mlp1"], mixer=True)
        add(["w_o", "w_pa", "w_pb"])
        dwin = gmat["w_in", l] = reducing(
            functools.partial(_mm, f"in_proj_dw_{l}", "tn", s["h"], dz, D, 7 * D, T, [BF16]),
            scatters=["w_o", "w_pa", "w_pb"])
        dbin = reducing(functools.partial(_colsum, f"bias_grad_{l}", dz), swaps=["w_in"])
        add(["w_in"])
        dh = reducing(functools.partial(_mm, f"in_proj_dx_{l}", "nt", dz, mats["w_in", l], T, D, 7 * D, [F32]),
                      scatters=["w_in"] if l == 0 else [], part=("w_in", l, 0, 2) if l > 0 else None,
                      tiles=MM_TILES_LONG_K)
        dx, dxb, dg1 = _rms_bwd(f"rms1_bwd_{l}", s["x0"], g1[l][None], dh, dx1)
        grads[l] = dict(w_in=dwin, w_pa=dwpa, w_pb=dwpb, w_o=dwo, w_mlp1=dw1, w_mlp2=dw2, wr=dwr, wi=dwi,
                        sm=sm, b_in=dbin, g1=dg1, g2=dg2)
    return loss_blk[0, 0], dx, grads, dgf, sums, landed, acc


ANY = pl.BlockSpec(memory_space=pl.ANY)


def _place():
    x, y, c = lax.axis_index("x"), lax.axis_index("y"), lax.axis_index("c")
    peers = [(1 - x, y, c), (x, 1 - y, c), (1 - x, 1 - y, c)]
    chips = [2 * (1 - x) + y, 2 * x + (1 - y), 2 * (1 - x) + (1 - y)]
    return (x, y, c), 2 * x + y, peers, chips


def _window(ref, dim, q, size):
    idx = [slice(None)] * len(ref.shape)
    idx[dim] = pl.ds(q * size, size)
    return ref.at[tuple(idx)]


def _gather_weights(shards, dims, small):
    n = len(shards)
    sizes = [s.shape[d] for s, d in zip(shards, dims)]
    full = [jax.ShapeDtypeStruct(s.shape[:d] + (s.shape[d] * N_CHIPS,) + s.shape[d + 1:], s.dtype)
            for s, d in zip(shards, dims)]
    full.append(jax.ShapeDtypeStruct((N_CHIPS,) + small.shape, small.dtype))

    def body(*refs):
        ins, outs = refs[:n + 1], refs[n + 1:2 * n + 2]
        send_sems, recv_sems, local_sems = refs[2 * n + 2:]
        _, k, peers, chips = _place()

        def dst(w, q):
            return outs[w].at[q] if w == n else _window(outs[w], dims[w], q, sizes[w])

        local = [pltpu.make_async_copy(ins[w], dst(w, k), local_sems.at[w]) for w in range(n + 1)]
        for cp in local:
            cp.start()
        sends = []
        for p, peer in enumerate(peers):
            for w in range(n + 1):
                s = p * (n + 1) + w
                sends.append(pltpu.make_async_remote_copy(
                    src_ref=ins[w], dst_ref=dst(w, k), send_sem=send_sems.at[s], recv_sem=recv_sems.at[s],
                    device_id=peer, device_id_type=MESH))
        for cp in sends:
            cp.start()
        for p, peer in enumerate(peers):
            for w in range(n + 1):
                s = p * (n + 1) + w
                pltpu.make_async_remote_copy(
                    src_ref=ins[w], dst_ref=dst(w, chips[p]), send_sem=send_sems.at[s], recv_sem=recv_sems.at[s],
                    device_id=peer, device_id_type=MESH).wait_recv()
        for cp in sends:
            cp.wait_send()
        for cp in local:
            cp.wait()

    return pl.pallas_call(
        body,
        in_specs=[ANY] * (n + 1),
        out_specs=[ANY] * (n + 1),
        out_shape=full,
        scratch_shapes=[pltpu.SemaphoreType.DMA((3 * (n + 1),)), pltpu.SemaphoreType.DMA((3 * (n + 1),)),
                        pltpu.SemaphoreType.DMA((n + 1,))],
        name="gather_weights",
    )(*shards, small)


def _scatter_grads(grads, dims):
    n, depth = len(grads), len(grads[0])
    sizes = [g[0].shape[d] // N_CHIPS for g, d in zip(grads, dims)]
    land = []
    for g, d, sz in zip(grads, dims, sizes):
        shp = g[0].shape
        land.append(jax.ShapeDtypeStruct((N_CHIPS, depth) + shp[:d] + (sz,) + shp[d + 1:], g[0].dtype))

    def body(*refs):
        ins, outs = refs[:n * depth], refs[n * depth:n * depth + n]
        send_sems, recv_sems, local_sems = refs[n * depth + n:]
        _, k, peers, chips = _place()

        def src(w, l, q):
            return _window(ins[w * depth + l], dims[w], q, sizes[w])

        local = [pltpu.make_async_copy(src(w, l, k), outs[w].at[3, l], local_sems.at[w * depth + l])
                 for w in range(n) for l in range(depth)]
        for cp in local:
            cp.start()
        sends = []
        for p, peer in enumerate(peers):
            for w in range(n):
                for l in range(depth):
                    s = (p * n + w) * depth + l
                    sends.append(pltpu.make_async_remote_copy(
                        src_ref=src(w, l, chips[p]), dst_ref=outs[w].at[p, l], send_sem=send_sems.at[s],
                        recv_sem=recv_sems.at[s], device_id=peer, device_id_type=MESH))
        for cp in sends:
            cp.start()
        for cp in sends:
            cp.wait_recv()
        for cp in sends:
            cp.wait_send()
        for cp in local:
            cp.wait()

    flat = [g for gl in grads for g in gl]
    return pl.pallas_call(
        body,
        in_specs=[ANY] * (n * depth),
        out_specs=[ANY] * n,
        out_shape=land,
        scratch_shapes=[pltpu.SemaphoreType.DMA((3 * n * depth,)), pltpu.SemaphoreType.DMA((3 * n * depth,)),
                        pltpu.SemaphoreType.DMA((n * depth,))],
        name="scatter_grads",
    )(*flat)


def _sum_slots(name, land):
    _, R, C = land.shape
    tr, tc = _div_tile(R, 512, 8), _div_tile(C, 1024)

    def body(a_ref, b_ref, c_ref, d_ref, o_ref):
        o_ref[...] = ((d_ref[...].astype(F32) + a_ref[...].astype(F32)) + b_ref[...].astype(F32)) \
            + c_ref[...].astype(F32)

    def slot(q):
        return pl.BlockSpec((None, tr, tc), lambda i, j: (q, i, j))

    return pl.pallas_call(
        body,
        grid=(R // tr, C // tc),
        in_specs=[slot(0), slot(1), slot(2), slot(3)],
        out_specs=pl.BlockSpec((tr, tc), lambda i, j: (i, j)),
        out_shape=jax.ShapeDtypeStruct((R, C), F32),
        compiler_params=_cparams(("parallel", "parallel")),
        name=name,
    )(land, land, land, land)


def _swap_with_sibling(parts):
    n = len(parts)

    def body(*refs):
        ins, outs = refs[:n], refs[n:2 * n]
        send_sems, recv_sems = refs[2 * n:]
        (x, y, c), _, _, _ = _place()
        copies = [pltpu.make_async_remote_copy(
            src_ref=ins[w], dst_ref=outs[w], send_sem=send_sems.at[w], recv_sem=recv_sems.at[w],
            device_id=(x, y, 1 - c), device_id_type=MESH) for w in range(n)]
        for cp in copies:
            cp.start()
        for cp in copies:
            cp.wait()

    return pl.pallas_call(
        body,
        in_specs=[ANY] * n,
        out_specs=[ANY] * n,
        out_shape=[jax.ShapeDtypeStruct(p.shape, p.dtype) for p in parts],
        scratch_shapes=[pltpu.SemaphoreType.DMA((n,)), pltpu.SemaphoreType.DMA((n,))],
        name="swap_with_sibling",
    )(*parts)


CARRY_MARKS = (0.6, 0.92)


class _Carry:
    def __init__(self, ins, out_shapes, rounds, counts, aliases=None, marks=CARRY_MARKS):
        self.ins, self.out_shapes, self.rounds, self.counts = list(ins), list(out_shapes), list(rounds), list(counts)
        self.aliases, self.marks = dict(aliases or {}), marks

    def scratch(self):
        return [pltpu.SemaphoreType.DMA((n,)) for n in self.counts for _ in range(2)]

    def _copies(self, r, in_refs, out_refs, sems, landing):
        remote = self.rounds[r](in_refs, out_refs)
        assert len(remote) == self.counts[r], (r, len(remote), self.counts[r])
        return [pltpu.make_async_remote_copy(src_ref=s, dst_ref=(land if landing else d), send_sem=sems[2 * r].at[i],
                                             recv_sem=sems[2 * r + 1].at[i], device_id=peer, device_id_type=MESH)
                for i, (s, d, peer, land) in enumerate(remote)]

    def begin(self, r, in_refs, out_refs, sems):
        if r > 0:
            for cp in self._copies(r - 1, in_refs, out_refs, sems, True):
                cp.wait_recv()
        for cp in self._copies(r, in_refs, out_refs, sems, False):
            cp.start()

    def end(self, in_refs, out_refs, sems):
        last = len(self.rounds) - 1
        for cp in self._copies(last, in_refs, out_refs, sems, True):
            cp.wait_recv()
        for r in range(last + 1):
            for cp in self._copies(r, in_refs, out_refs, sems, False):
                cp.wait_send()

    def ride(self, step, total, in_refs, out_refs, sems, first):
        if first:
            @pl.when(step == 0)
            def _():
                self.begin(0, in_refs, out_refs, sems)
            return
        for r in range(1, len(self.rounds)):
            @pl.when(step == min(total - 1, int(total * self.marks[r - 1])))
            def _(r=r):
                self.begin(r, in_refs, out_refs, sems)

        @pl.when(step == total - 1)
        def _():
            self.end(in_refs, out_refs, sems)


def _run_carry(name, carry):
    n_in, n_out = len(carry.ins), len(carry.out_shapes)

    def body(*refs):
        in_refs, out_refs, sems = refs[:n_in], refs[n_in:n_in + n_out], refs[n_in + n_out:]
        for r in range(len(carry.rounds)):
            carry.begin(r, in_refs, out_refs, sems)
        carry.end(in_refs, out_refs, sems)

    return pl.pallas_call(
        body,
        in_specs=[ANY] * n_in,
        out_specs=[ANY] * n_out,
        out_shape=carry.out_shapes,
        scratch_shapes=carry.scratch(),
        input_output_aliases=carry.aliases,
        name=name,
    )(*carry.ins)


def _gather_carry(arrays, items):
    shapes = [a.shape for a in arrays]

    def ring():
        x, y, c = lax.axis_index("x"), lax.axis_index("y"), lax.axis_index("c")
        first = (x + (1 - c) * (1 - 2 * x), y + c * (1 - 2 * y), c)
        second = (x + c * (1 - 2 * x), y + (1 - c) * (1 - 2 * y), c)
        return c, 2 * x + y, first, second

    def chip(pos):
        return 2 * pos[0] + pos[1]

    def round0(ins, outs):
        c, k, first, second = ring()
        remote = []
        for item in items:
            win = window(outs[item[0]], item)
            remote.append((win(2 * k + c), win(2 * k + c), first, win(2 * chip(first) + c)))
            remote.append((win(2 * k + c), win(2 * k + c), second, win(2 * chip(second) + c)))
        return remote

    def round1(ins, outs):
        c, k, first, second = ring()
        remote = []
        for item in items:
            win = window(outs[item[0]], item)
            relayed = win(2 * chip(first) + c)
            remote.append((relayed, relayed, second, win(2 * (3 - k) + c)))
        return remote

    def window(ref, item):
        idx, d, part, nparts = item
        h = shapes[idx][d] // (2 * N_CHIPS)
        rows = shapes[idx][1 - d] // nparts

        def win(j):
            sl = [None, None]
            sl[d] = pl.ds(j * h, h)
            sl[1 - d] = pl.ds(part * rows, rows)
            return ref.at[tuple(sl)]

        return win

    def round2(ins, outs):
        (x, y, c), _, _, chips = _place()
        remote = []
        for item in items:
            win = window(outs[item[0]], item)
            for p in range(3):
                remote.append((win(2 * chips[p] + c), win(2 * chips[p] + c), (x, y, 1 - c),
                               win(2 * chips[p] + 1 - c)))
        return remote

    n = len(items)
    return _Carry(arrays, [jax.ShapeDtypeStruct(a.shape, a.dtype) for a in arrays], [round0, round1, round2],
                  [2 * n, n, 3 * n], aliases={i: i for i in range(len(arrays))})


def _place_shard(name, w, layer, dim, chip):
    _, a, b = w.shape
    full = (a * N_CHIPS, b) if dim == 0 else (a, b * N_CHIPS)
    tr, tc = _div_tile(a, 512, 16), _div_tile(b, 2048)
    nr, nc = a // tr, b // tc

    def out_map(i, j, chip_ref):
        return (chip_ref[0] * nr + i, j) if dim == 0 else (i, chip_ref[0] * nc + j)

    def body(chip_ref, w_ref, o_ref):
        o_ref[...] = w_ref[...].astype(o_ref.dtype)

    return pl.pallas_call(
        body,
        grid_spec=pltpu.PrefetchScalarGridSpec(
            num_scalar_prefetch=1, grid=(nr, nc),
            in_specs=[pl.BlockSpec((None, tr, tc), lambda i, j, chip_ref: (layer, i, j))],
            out_specs=pl.BlockSpec((tr, tc), out_map)),
        out_shape=jax.ShapeDtypeStruct(full, BF16),
        compiler_params=_cparams(("parallel", "parallel")),
        name=name,
    )(chip, w)


def _half_shape(shape, dim):
    return shape[:dim] + (shape[dim] // (2 * N_CHIPS),) + shape[dim + 1:]


def _swap_carry(grads, dims):
    shapes = [jax.ShapeDtypeStruct((N_CHIPS,) + _half_shape(g.shape, d), g.dtype) for g, d in zip(grads, dims)]

    def plan(ins, outs):
        (x, y, c), _, _, _ = _place()
        remote = []
        for w, d in enumerate(dims):
            h = grads[w].shape[d] // (2 * N_CHIPS)
            for q in range(N_CHIPS):
                remote.append((_window(ins[w], d, 2 * q + 1 - c, h), outs[w].at[q], (x, y, 1 - c), outs[w].at[q]))
        return remote

    return _Carry(grads, shapes, [plan], [N_CHIPS * len(grads)])


def _scatter_carry(sums, part=0, nparts=1, land=None):
    n = len(sums)

    def plan(ins, outs):
        _, _, peers, chips = _place()
        remote = []
        for w in range(n):
            r = sums[w].shape[1] // nparts
            rows = pl.ds(part * r, r)
            for p in range(3):
                remote.append((ins[w].at[chips[p], rows], outs[w].at[p, rows], peers[p], outs[w].at[p, rows]))
        return remote

    shapes = [jax.ShapeDtypeStruct((3,) + s.shape[1:], s.dtype) for s in sums]
    if land is None:
        return _Carry(sums, shapes, [plan], [3 * n])
    return _Carry(list(sums) + list(land), shapes, [plan], [3 * n], aliases={n + w: w for w in range(n)})


def _merge_carries(carries):
    carries = [c for c in carries if c is not None]
    if len(carries) <= 1:
        return carries[0] if carries else None
    ins = [a for c in carries for a in c.ins]
    outs = [s for c in carries for s in c.out_shapes]

    def plan(in_refs, out_refs):
        remote, i0, o0 = [], 0, 0
        for c in carries:
            remote += c.rounds[0](in_refs[i0:i0 + len(c.ins)], out_refs[o0:o0 + len(c.out_shapes)])
            i0, o0 = i0 + len(c.ins), o0 + len(c.out_shapes)
        return remote

    assert all(len(c.rounds) == 1 for c in carries)
    aliases, i0, o0 = {}, 0, 0
    for c in carries:
        aliases.update({i0 + ci: o0 + co for ci, co in c.aliases.items()})
        i0, o0 = i0 + len(c.ins), o0 + len(c.out_shapes)
    return _Carry(ins, outs, [plan], [sum(c.counts[0] for c in carries)], aliases=aliases)


def _add_halves(name, g, got, dim, core):
    R, C = g.shape
    if dim == 1:
        r, cc = R, C // (2 * N_CHIPS)
    else:
        r, cc = R // (2 * N_CHIPS), C
    tr, tc = _div_tile(r, 512, 16), _div_tile(cc, 1024)
    nr, nc = r // tr, cc // tc

    def g_map(q, i, j, core_ref):
        w = 2 * q + core_ref[0]
        return (i, w * nc + j) if dim == 1 else (w * nr + i, j)

    def body(core_ref, g_ref, got_ref, o_ref):
        o_ref[...] = (g_ref[...].astype(F32) + got_ref[...].astype(F32)).astype(o_ref.dtype)

    slab = pl.BlockSpec((None, tr, tc), lambda q, i, j, core_ref: (q, i, j))
    return pl.pallas_call(
        body,
        grid_spec=pltpu.PrefetchScalarGridSpec(
            num_scalar_prefetch=1, grid=(N_CHIPS, nr, nc),
            in_specs=[pl.BlockSpec((tr, tc), g_map), slab], out_specs=slab),
        out_shape=jax.ShapeDtypeStruct((N_CHIPS, r, cc), g.dtype),
        compiler_params=_cparams(("parallel", "parallel", "parallel")),
        name=name,
    )(core, g, got)


def _reduce_into(name, sums, land, acc, layer, dim, shape, where):
    _, r, cc = sums.shape
    tr, tc = _div_tile(r, 512, 16), _div_tile(cc, 1024)
    nr, nc = r // tr, cc // tc

    def out_map(i, j, s):
        return (layer, s[1] * nr + i, j) if dim == 1 else (layer, i, s[1] * nc + j)

    def body(*refs):
        own, a_ref, b_ref, c_ref, o_ref = refs[1], refs[2], refs[3], refs[4], refs[-1]
        o_ref[...] = ((own[...].astype(F32) + a_ref[...].astype(F32)) + b_ref[...].astype(F32)) \
            + c_ref[...].astype(F32)

    def slot(p):
        return pl.BlockSpec((None, tr, tc), lambda i, j, s: (p, i, j))

    in_specs = [pl.BlockSpec((None, tr, tc), lambda i, j, s: (s[0], i, j)), slot(0), slot(1), slot(2)]
    args = [where, sums, land, land, land]
    if acc is not None:
        in_specs.append(ANY)
        args.append(acc)
    return pl.pallas_call(
        body,
        grid_spec=pltpu.PrefetchScalarGridSpec(
            num_scalar_prefetch=1, grid=(nr, nc), in_specs=in_specs,
            out_specs=pl.BlockSpec((None, tr, tc), out_map)),
        out_shape=jax.ShapeDtypeStruct(shape, F32),
        input_output_aliases={5: 0} if acc is not None else {},
        compiler_params=_cparams(("parallel", "parallel")),
        name=name,
    )(*args)


def _join_carry(grads, dims, layer):
    n = len(grads)

    def plan(ins, outs):
        (x, y, c), _, _, _ = _place()
        remote = []
        for w in range(n):
            d, h = dims[w], grads[w].shape[dims[w]] // 2

            def win(half, w=w, d=d, h=h):
                idx = [slice(None)] * len(grads[w].shape)
                idx[0], idx[d] = layer, pl.ds(half * h, h)
                return outs[w].at[tuple(idx)]

            remote.append((win(c), win(c), (x, y, 1 - c), win(1 - c)))
        return remote

    return _Carry(grads, [jax.ShapeDtypeStruct(g.shape, g.dtype) for g in grads], [plan], [n],
                  aliases={w: w for w in range(n)})


def _allreduce_small(pack):
    R, C = pack.shape

    def gather_body(in_ref, slots_ref, send_sems, recv_sems, local_sem):
        x, y, c = lax.axis_index("x"), lax.axis_index("y"), lax.axis_index("c")
        me = 4 * x + 2 * y + c
        flips = [(dx, dy, dc) for dx in (0, 1) for dy in (0, 1) for dc in (0, 1)][1:]

        def flip(v, d):
            return 1 - v if d else v

        local = pltpu.make_async_copy(in_ref, slots_ref.at[me], local_sem)
        local.start()
        sends = []
        for j, (dx, dy, dc) in enumerate(flips):
            px, py, pc = flip(x, dx), flip(y, dy), flip(c, dc)
            sends.append((pltpu.make_async_remote_copy(
                src_ref=in_ref, dst_ref=slots_ref.at[me], send_sem=send_sems.at[j], recv_sem=recv_sems.at[j],
                device_id=(px, py, pc), device_id_type=MESH), 4 * px + 2 * py + pc, j))
        for cp, _, _ in sends:
            cp.start()
        for cp, peer_id, j in sends:
            pltpu.make_async_remote_copy(
                src_ref=in_ref, dst_ref=slots_ref.at[peer_id], send_sem=send_sems.at[j], recv_sem=recv_sems.at[j],
                device_id=(x, y, c), device_id_type=MESH).wait_recv()
        for cp, _, _ in sends:
            cp.wait_send()
        local.wait()

    slots = pl.pallas_call(
        gather_body,
        in_specs=[ANY],
        out_specs=ANY,
        out_shape=jax.ShapeDtypeStruct((N_DEV, R, C), pack.dtype),
        scratch_shapes=[pltpu.SemaphoreType.DMA((N_DEV - 1,)), pltpu.SemaphoreType.DMA((N_DEV - 1,)),
                        pltpu.SemaphoreType.DMA],
        name="allgather_small",
    )(pack)

    def sum_body(s_ref, o_ref):
        acc = s_ref[0]
        for d in range(1, N_DEV):
            acc = acc + s_ref[d]
        o_ref[...] = acc

    return pl.pallas_call(
        sum_body,
        out_shape=jax.ShapeDtypeStruct((R, C), pack.dtype),
        name="sum_small",
    )(slots)


def _adamw_math(w, g, m, v):
    m2 = ADAM_B1 * m + (1.0 - ADAM_B1) * g
    v2 = ADAM_B2 * v + (1.0 - ADAM_B2) * (g * g)
    m_hat = m2 / (1.0 - ADAM_B1 ** ADAM_STEP)
    v_hat = v2 / (1.0 - ADAM_B2 ** ADAM_STEP)
    delta = -ADAM_LR * (m_hat / (jnp.sqrt(v_hat) + ADAM_EPS) + ADAM_WD * w)
    return delta, m2, v2


def _adamw_big(name, w, m, v, g_parts):
    shape = w.shape
    C = shape[-1]
    R = w.size // C
    tr, tc = _div_tile(R, 256, 8), _div_tile(C, 1024)
    n_g = len(g_parts)

    def body(*refs):
        w_ref, m_ref, v_ref = refs[:3]
        g_ref, d_ref, nm_ref, nv_ref = refs[3 + n_g:]
        g = refs[3][...]
        for extra in refs[4:3 + n_g]:
            g = g + extra[...]
        delta, m2, v2 = _adamw_math(w_ref[...], g, m_ref[...], v_ref[...])
        g_ref[...], d_ref[...], nm_ref[...], nv_ref[...] = g, delta, m2, v2

    blk = pl.BlockSpec((tr, tc), lambda i, j: (i, j))
    outs = pl.pallas_call(
        body,
        grid=(R // tr, C // tc),
        in_specs=[blk] * (3 + n_g),
        out_specs=[blk] * 4,
        out_shape=[jax.ShapeDtypeStruct((R, C), F32)] * 4,
        compiler_params=_cparams(("parallel", "parallel")),
        name=name,
    )(w.reshape(R, C), m.reshape(R, C), v.reshape(R, C), *[g.reshape(R, C) for g in g_parts])
    return [o.reshape(shape) for o in outs]


def _adamw_small(name, w, g, m, v):
    shape = w.shape
    two_d = (w.size // shape[-1], shape[-1])

    def body(w_ref, g_ref, m_ref, v_ref, d_ref, nm_ref, nv_ref):
        d_ref[...], nm_ref[...], nv_ref[...] = _adamw_math(w_ref[...], g_ref[...], m_ref[...], v_ref[...])

    outs = pl.pallas_call(
        body,
        out_shape=[jax.ShapeDtypeStruct(two_d, F32)] * 3,
        name=name,
    )(w.reshape(two_d), g.reshape(two_d), m.reshape(two_d), v.reshape(two_d))
    return [o.reshape(shape) for o in outs]


SMALL_ROWS = 40
S_BIN, S_G1, S_G2 = 16, 24, 32
MATS = ("w_in", "w_pa", "w_pb", "w_o", "w_mlp1", "w_mlp2")
LRU = ("lru_wr", "lru_wi")
BIG_DIM = dict(w_in=2, w_pa=1, w_pb=1, w_o=1, w_mlp1=2, w_mlp2=1, lru_wr=2, lru_wi=2)
WEIGHTS = ("norm1_g", "w_in", "b_in", "conv_a_w", "conv_a_b", "lru_wr", "lru_br", "lru_wi", "lru_bi", "lru_lam",
           "conv_b_w", "w_pa", "w_pb", "w_o", "norm2_g", "w_mlp1", "w_mlp2", "final_g")


def _rows_at(a, r0, total):
    pad = [(0, 0)] * a.ndim
    pad[-2] = (r0, total - r0 - a.shape[-2])
    return jnp.pad(a, pad)


def kernel(x, norm1_g, w_in, b_in, conv_a_w, conv_a_b, lru_wr, lru_br, lru_wi, lru_bi, lru_lam, conv_b_w, w_pa, w_pb, w_o, norm2_g, w_mlp1, w_mlp2, final_g, loss_target, m_norm1_g, m_w_in, m_b_in, m_conv_a_w, m_conv_a_b, m_lru_wr, m_lru_br, m_lru_wi, m_lru_bi, m_lru_lam, m_conv_b_w, m_w_pa, m_w_pb, m_w_o, m_norm2_g, m_w_mlp1, m_w_mlp2, m_final_g, v_norm1_g, v_w_in, v_b_in, v_conv_a_w, v_conv_a_b, v_lru_wr, v_lru_br, v_lru_wi, v_lru_bi, v_lru_lam, v_conv_b_w, v_w_pa, v_w_pb, v_w_o, v_norm2_g, v_w_mlp1, v_w_mlp2, v_final_g):
    wts = dict(norm1_g=norm1_g, w_in=w_in, b_in=b_in, conv_a_w=conv_a_w, conv_a_b=conv_a_b, lru_wr=lru_wr,
               lru_br=lru_br, lru_wi=lru_wi, lru_bi=lru_bi, lru_lam=lru_lam, conv_b_w=conv_b_w, w_pa=w_pa,
               w_pb=w_pb, w_o=w_o, norm2_g=norm2_g, w_mlp1=w_mlp1, w_mlp2=w_mlp2, final_g=final_g)
    mom = dict(norm1_g=m_norm1_g, w_in=m_w_in, b_in=m_b_in, conv_a_w=m_conv_a_w, conv_a_b=m_conv_a_b,
               lru_wr=m_lru_wr, lru_br=m_lru_br, lru_wi=m_lru_wi, lru_bi=m_lru_bi, lru_lam=m_lru_lam,
               conv_b_w=m_conv_b_w, w_pa=m_w_pa, w_pb=m_w_pb, w_o=m_w_o, norm2_g=m_norm2_g, w_mlp1=m_w_mlp1,
               w_mlp2=m_w_mlp2, final_g=m_final_g)
    vel = dict(norm1_g=v_norm1_g, w_in=v_w_in, b_in=v_b_in, conv_a_w=v_conv_a_w, conv_a_b=v_conv_a_b,
               lru_wr=v_lru_wr, lru_br=v_lru_br, lru_wi=v_lru_wi, lru_bi=v_lru_bi, lru_lam=v_lru_lam,
               conv_b_w=v_conv_b_w, w_pa=v_w_pa, w_pb=v_w_pb, w_o=v_w_o, norm2_g=v_norm2_g, w_mlp1=v_w_mlp1,
               w_mlp2=v_w_mlp2, final_g=v_final_g)
    depth, D = norm1_g.shape
    nb, bw = lru_wr.shape[1], lru_wr.shape[3]
    chip = 2 * lax.axis_index("x") + lax.axis_index("y")

    small_parts = [conv_a_w.reshape(-1), conv_b_w.reshape(-1), lru_br.reshape(-1), lru_bi.reshape(-1)]
    small_len = sum(p.shape[0] for p in small_parts)
    small_rows = -(-small_len // 1024) * 8
    small = jnp.concatenate(small_parts + [jnp.zeros((small_rows * 128 - small_len,), F32)]).reshape(small_rows, 128)
    gathered = _gather_weights([wts[n].astype(BF16) for n in LRU], [BIG_DIM[n] for n in LRU], small)
    full = dict(zip(LRU, gathered[:-1]))
    items = [(n, l) for l in range(depth) for n in MATS]
    mat_dims = [BIG_DIM[n] - 1 for n, _ in items]
    where = jnp.stack([chip, lax.axis_index("c")]).astype(jnp.int32)
    placed = {(n, l): _place_shard(f"place_{n}_{l}", wts[n], l, BIG_DIM[n] - 1, where) for n, l in items}
    flat = gathered[-1].reshape(N_CHIPS, small_rows * 128)
    off = 0
    small_full = []
    for part, shard in zip(small_parts, (conv_a_w, conv_b_w, lru_br, lru_bi)):
        piece = flat[:, off:off + part.shape[0]].reshape((N_CHIPS,) + shard.shape)
        small_full.append(jnp.moveaxis(piece, 0, -2).reshape(shard.shape[:-1] + (N_CHIPS * shard.shape[-1],)))
        off += part.shape[0]
    caw_f, cbw_f, br_f, bi_f = small_full
    pch = (_rows_at(conv_a_b[:, None, :], R_CAB, R_ROWS) + _rows_at(br_f.reshape(depth, 1, D), R_BR, R_ROWS)
           + _rows_at(bi_f.reshape(depth, 1, D), R_BI, R_ROWS) + _rows_at(lru_lam[:, None, :], R_LAM, R_ROWS)
           + _rows_at(caw_f, R_CAW, R_ROWS) + _rows_at(cbw_f, R_CBW, R_ROWS))
    W = dict(b_in=b_in, pch=pch, wr=full["lru_wr"], wi=full["lru_wi"], g1=norm1_g, g2=norm2_g, gf=final_g,
             core=lax.axis_index("c").astype(jnp.int32).reshape(1), where=where,
             shard_shapes={n: wts[n].shape for n in MATS})

    loss_local, dx, grads, dgf, sums, landed, acc = _local_fwd_bwd(x[0], loss_target[0], W, placed)
    loss = lax.psum(loss_local, ("x", "y", "c"))

    key = dict(w_in="w_in", w_pa="w_pa", w_pb="w_pb", w_o="w_o", w_mlp1="w1", w_mlp2="w2", lru_wr="wr", lru_wi="wi")
    out_g, out_d, out_m, out_v = {}, {}, {}, {}
    per_layer = [[grads[l][key[n]].astype(BF16) for l in range(depth)] for n in LRU]
    land = _scatter_grads(per_layer, [BIG_DIM[n] - 1 for n in LRU])
    chip_sums = [_sum_slots(f"sum_slots_{n}", ld.reshape(N_CHIPS, -1, ld.shape[-1])) for n, ld in zip(LRU, land)]
    sib_sums = _swap_with_sibling(chip_sums)
    for n, mine, sib in zip(LRU, chip_sums, sib_sums):
        out_g[n], out_d[n], out_m[n], out_v[n] = _adamw_big(f"adamw_{n}", wts[n], mom[n], vel[n], [mine, sib])
    for n in MATS:
        acc[n] = _reduce_into(f"reduce_{n}_0", sums[n, 0], landed[n, 0], acc.get(n), 0, BIG_DIM[n], wts[n].shape, where)
    joined = _run_carry("join_halves", _join_carry([acc[n] for n in MATS], [BIG_DIM[n] for n in MATS], 0))
    for n, g in zip(MATS, joined):
        out_g[n], out_d[n], out_m[n], out_v[n] = _adamw_big(f"adamw_{n}", wts[n], mom[n], vel[n], [g])

    rows = []
    for l in range(depth):
        g = grads[l]
        rows.append(_rows_at(g["sm"], 0, SMALL_ROWS) + _rows_at(g["b_in"].reshape(7, D), S_BIN, SMALL_ROWS)
                    + _rows_at(g["g1"], S_G1, SMALL_ROWS) + _rows_at(g["g2"], S_G2, SMALL_ROWS))
    rows.append(_rows_at(dgf, 0, 8))
    tot = _allreduce_small(jnp.concatenate(rows, axis=0))
    per = tot[:depth * SMALL_ROWS].reshape(depth, SMALL_ROWS, D)

    def cols_of_chip(a, axis):
        size = a.shape[axis] // N_CHIPS
        return lax.dynamic_slice_in_dim(a, chip * size, size, axis=axis)

    small_g = dict(
        norm1_g=per[:, S_G1], b_in=per[:, S_BIN:S_BIN + 7].reshape(depth, 7 * D),
        conv_a_w=cols_of_chip(per[:, R_CAW:R_CAW + 4], 2), conv_a_b=per[:, R_CAB],
        lru_br=cols_of_chip(per[:, R_BR].reshape(depth, nb, bw), 2),
        lru_bi=cols_of_chip(per[:, R_BI].reshape(depth, nb, bw), 2), lru_lam=per[:, R_LAM],
        conv_b_w=cols_of_chip(per[:, R_CBW:R_CBW + 3], 2), norm2_g=per[:, S_G2],
        final_g=tot[depth * SMALL_ROWS])
    for n, g in small_g.items():
        out_g[n] = g
        out_d[n], out_m[n], out_v[n] = _adamw_small(f"adamw_{n}", wts[n], g, mom[n], vel[n])

    return (loss, dx[None], *[out_g[n] for n in WEIGHTS], *[out_d[n] for n in WEIGHTS],
            *[out_m[n] for n in WEIGHTS], *[out_v[n] for n in WEIGHTS])
```

```python
import functools

import jax
import jax.numpy as jnp
from jax import lax
from jax.experimental import pallas as pl
from jax.experimental.pallas import tpu as pltpu

F32 = jnp.float32
BF16 = jnp.bfloat16
MESH = pl.DeviceIdType.MESH

EPS = 1e-6
LRU_C = 8.0
ADAM_LR = 0.001
ADAM_B1 = 0.9
ADAM_B2 = 0.999
ADAM_EPS = 1e-08
ADAM_WD = 0.01
ADAM_STEP = 10

N_CHIPS = 4
N_DEV = 8
HALO = 8
VMEM_LIMIT = 56 * 1024 * 1024
MM_TILES = (1024, 1024, 2048)
MM_TILES_FUSED = (512, 1024, 2048)
MM_TILES_LONG_K = (1024, 1024, 4096)
SEQ_CHUNK = 512
MIXER_ROW_BLOCK = 32
ROW_TILE = 256

R_CAB, R_BR, R_BI, R_LAM, R_CAW, R_CBW, R_ROWS = 0, 1, 2, 3, 4, 8, 16


def _cparams(sem):
    return pltpu.CompilerParams(dimension_semantics=sem, vmem_limit_bytes=VMEM_LIMIT)


def _div_tile(n, pref, unit=128):
    if n <= pref:
        return n
    t = (pref // unit) * unit
    while n % t:
        t -= unit
    return t


def _sigmoid(v):
    return 1.0 / (1.0 + jnp.exp(-v))


def _gelu_and_grad(y):
    k = 0.7978845608028654
    c = 0.044715
    y2 = y * y
    t = jnp.tanh(k * (y + c * y2 * y))
    g = 0.5 * y * (1.0 + t)
    gp = 0.5 * (1.0 + t) + 0.5 * y * (1.0 - t * t) * (k * (1.0 + 3.0 * c * y2))
    return g, gp


def _softplus_neg(lam):
    e = jnp.exp(-jnp.abs(lam))
    w = 1.0 + e
    l1p = jnp.where(w == 1.0, e, jnp.log(w) * e / jnp.where(w == 1.0, 1.0, w - 1.0))
    return jnp.maximum(-lam, 0.0) + l1p


def _mm(name, mode, a, b, M, N, K, out_dtypes, epilogue=None, extras=(), la=None, lb=None, tiles=None,
        carry=None):
    tiles = MM_TILES if tiles is None else tiles
    tm, tn, tk = _div_tile(M, tiles[0]), _div_tile(N, tiles[1]), _div_tile(K, tiles[2])
    assert M % tm == 0 and N % tn == 0 and K % tk == 0, (name, M, N, K)
    nk = K // tk

    def spec(lead, shape, imap):
        if lead is None:
            return pl.BlockSpec(shape, imap)
        return pl.BlockSpec((None,) + shape, lambda i, j, k: (lead,) + imap(i, j, k))

    if mode == "nn":
        a_spec = spec(la, (tm, tk), lambda i, j, k: (i, k))
        b_spec = spec(lb, (tk, tn), lambda i, j, k: (k, j))
        dn = (((1,), (0,)), ((), ()))
    elif mode == "nt":
        a_spec = spec(la, (tm, tk), lambda i, j, k: (i, k))
        b_spec = spec(lb, (tn, tk), lambda i, j, k: (j, k))
        dn = (((1,), (1,)), ((), ()))
    else:
        a_spec = spec(la, (tk, tm), lambda i, j, k: (k, i))
        b_spec = spec(lb, (tk, tn), lambda i, j, k: (k, j))
        dn = (((0,), (0,)), ((), ()))

    ex_arrays, ex_specs = [], []
    for arr, kind, off in extras:
        ex_arrays.append(arr)
        if kind == "bias":
            ex_specs.append(pl.BlockSpec((1, tn), lambda i, j, k: (0, j)))
        else:
            assert off % tn == 0
            ex_specs.append(pl.BlockSpec((tm, tn), lambda i, j, k, o=off // tn: (i, j + o)))
    n_ex, n_out = len(ex_arrays), len(out_dtypes)
    n_cin = len(carry.ins) if carry else 0
    n_cout = len(carry.out_shapes) if carry else 0
    n_in = 2 + n_ex + n_cin
    gi, gj = M // tm, N // tn

    def body(*refs):
        a_ref, b_ref = refs[0], refs[1]
        ex = refs[2:2 + n_ex]
        outs = refs[n_in:n_in + n_out]
        acc = refs[n_in + n_out + n_cout]
        i, j, k = pl.program_id(0), pl.program_id(1), pl.program_id(2)
        if carry:
            c_in, c_out = refs[2 + n_ex:n_in], refs[n_in + n_out:n_in + n_out + n_cout]
            sems = refs[n_in + n_out + n_cout + 1:]
            step = (i * gj + j) * nk + k
            carry.ride(step, gi * gj * nk, c_in, c_out, sems, True)

        def product():
            return lax.dot_general(a_ref[...], b_ref[...], dn, preferred_element_type=F32)

        def finish(r):
            vals = (r,) if epilogue is None else epilogue(r, *[e[...] for e in ex])
            for o, v in zip(outs, vals):
                o[...] = v.astype(o.dtype)

        if nk == 1:
            finish(product())
        else:
            @pl.when(k == 0)
            def _():
                acc[...] = product()

            @pl.when((k > 0) & (k < nk - 1))
            def _():
                acc[...] += product()

            @pl.when(k == nk - 1)
            def _():
                finish(acc[...] + product())

        if carry:
            carry.ride(step, gi * gj * nk, c_in, c_out, sems, False)

    res = pl.pallas_call(
        body,
        grid=(gi, gj, nk),
        in_specs=[a_spec, b_spec, *ex_specs] + [ANY] * n_cin,
        out_specs=[pl.BlockSpec((tm, tn), lambda i, j, k: (i, j)) for _ in range(n_out)] + [ANY] * n_cout,
        out_shape=[jax.ShapeDtypeStruct((M, N), d) for d in out_dtypes] + (carry.out_shapes if carry else []),
        scratch_shapes=[pltpu.VMEM((tm, tn) if nk > 1 else (8, 128), F32)] + (carry.scratch() if carry else []),
        input_output_aliases={2 + n_ex + ci: n_out + co for ci, co in carry.aliases.items()} if carry else {},
        compiler_params=_cparams(("arbitrary",) * 3 if carry else ("parallel", "parallel", "arbitrary")),
        name=name,
    )(a, b, *ex_arrays, *(carry.ins if carry else []))
    main = res[0] if n_out == 1 else res[:n_out]
    return (main, res[n_out:]) if carry else main


def _rms_fwd(name, x, g_row):
    T, D = x.shape
    tm = min(ROW_TILE, T)

    def body(x_ref, g_ref, h_ref):
        xv = x_ref[...]
        r = lax.rsqrt(jnp.mean(xv * xv, axis=-1, keepdims=True) + EPS)
        h_ref[...] = (xv * r * g_ref[...]).astype(BF16)

    return pl.pallas_call(
        body,
        grid=(T // tm,),
        in_specs=[pl.BlockSpec((tm, D), lambda i: (i, 0)), pl.BlockSpec((1, D), lambda i: (0, 0))],
        out_specs=pl.BlockSpec((tm, D), lambda i: (i, 0)),
        out_shape=jax.ShapeDtypeStruct((T, D), BF16),
        compiler_params=_cparams(("parallel",)),
        name=name,
    )(x, g_row)


def _rms_bwd(name, x, g_row, dh, dres):
    T, D = x.shape
    tm = min(ROW_TILE, T)

    def body(x_ref, g_ref, dh_ref, dres_ref, dx_ref, dxb_ref, dg_ref):
        xv, dhv = x_ref[...], dh_ref[...]
        r = lax.rsqrt(jnp.mean(xv * xv, axis=-1, keepdims=True) + EPS)
        gd = g_ref[...] * dhv
        c = jnp.mean(xv * gd, axis=-1, keepdims=True)
        dx = r * gd - xv * (r * r * r) * c + dres_ref[...]
        dx_ref[...] = dx
        dxb_ref[...] = dx.astype(BF16)

        @pl.when(pl.program_id(0) == 0)
        def _():
            dg_ref[...] = jnp.zeros_like(dg_ref)

        dg_ref[...] += jnp.sum(dhv * xv * r, axis=0, keepdims=True)

    row = pl.BlockSpec((tm, D), lambda i: (i, 0))
    vec = pl.BlockSpec((1, D), lambda i: (0, 0))
    return pl.pallas_call(
        body,
        grid=(T // tm,),
        in_specs=[row, vec, row, row],
        out_specs=[row, row, vec],
        out_shape=[jax.ShapeDtypeStruct((T, D), F32), jax.ShapeDtypeStruct((T, D), BF16),
                   jax.ShapeDtypeStruct((1, D), F32)],
        compiler_params=_cparams(("arbitrary",)),
        name=name,
    )(x, g_row, dh, dres)


def _loss_head(name, x, g_row, tgt):
    T, D = x.shape
    tm = min(ROW_TILE, T)

    def body(x_ref, g_ref, t_ref, dx_ref, dxb_ref, dg_ref, loss_ref):
        xv, g = x_ref[...], g_ref[...]
        r = lax.rsqrt(jnp.mean(xv * xv, axis=-1, keepdims=True) + EPS)
        xh = xv * r
        e = xh * g - t_ref[...]
        lpart = 0.5 * jnp.sum(jnp.mean(e * e, axis=-1, keepdims=True))
        dy = e * (1.0 / D)
        gd = g * dy
        c = jnp.mean(xv * gd, axis=-1, keepdims=True)
        dx = r * gd - xv * (r * r * r) * c
        dx_ref[...] = dx
        dxb_ref[...] = dx.astype(BF16)

        @pl.when(pl.program_id(0) == 0)
        def _():
            dg_ref[...] = jnp.zeros_like(dg_ref)
            loss_ref[...] = jnp.zeros_like(loss_ref)

        dg_ref[...] += jnp.sum(dy * xh, axis=0, keepdims=True)
        loss_ref[...] += jnp.full(loss_ref.shape, lpart, F32)

    row = pl.BlockSpec((tm, D), lambda i: (i, 0))
    vec = pl.BlockSpec((1, D), lambda i: (0, 0))
    return pl.pallas_call(
        body,
        grid=(T // tm,),
        in_specs=[row, vec, row],
        out_specs=[row, row, vec, pl.BlockSpec((8, 128), lambda i: (0, 0))],
        out_shape=[jax.ShapeDtypeStruct((T, D), F32), jax.ShapeDtypeStruct((T, D), BF16),
                   jax.ShapeDtypeStruct((1, D), F32), jax.ShapeDtypeStruct((8, 128), F32)],
        compiler_params=_cparams(("arbitrary",)),
        name=name,
    )(x, g_row, tgt)


def _colsum(name, a, carry=None):
    T, N = a.shape
    tm, tn = min(512, T), _div_tile(N, 2048)
    gj, gi = N // tn, T // tm
    n_cin = len(carry.ins) if carry else 0
    n_cout = len(carry.out_shapes) if carry else 0

    def body(*refs):
        a_ref, o_ref = refs[0], refs[1 + n_cin]
        j, i = pl.program_id(0), pl.program_id(1)
        if carry:
            c_in, c_out, sems = refs[1:1 + n_cin], refs[2 + n_cin:2 + n_cin + n_cout], refs[2 + n_cin + n_cout:]
            carry.ride(j * gi + i, gj * gi, c_in, c_out, sems, True)

        @pl.when(i == 0)
        def _():
            o_ref[...] = jnp.zeros_like(o_ref)

        o_ref[...] += jnp.sum(a_ref[...].astype(F32), axis=0, keepdims=True)

        if carry:
            carry.ride(j * gi + i, gj * gi, c_in, c_out, sems, False)

    res = pl.pallas_call(
        body,
        grid=(gj, gi),
        in_specs=[pl.BlockSpec((tm, tn), lambda j, i: (i, j))] + [ANY] * n_cin,
        out_specs=[pl.BlockSpec((1, tn), lambda j, i: (0, j))] + [ANY] * n_cout,
        out_shape=[jax.ShapeDtypeStruct((1, N), F32)] + (carry.out_shapes if carry else []),
        scratch_shapes=carry.scratch() if carry else [],
        compiler_params=_cparams(("arbitrary", "arbitrary") if carry else ("parallel", "arbitrary")),
        name=name,
    )(a, *(carry.ins if carry else []))
    return (res[0], res[1:]) if carry else res[0]


def _tile_scan(a, b, row, reverse):
    for s in (1, 2, 4):
        if reverse:
            a_s, b_s, m = pltpu.roll(a, 8 - s, 0), pltpu.roll(b, 8 - s, 0), row < 8 - s
        else:
            a_s, b_s, m = pltpu.roll(a, s, 0), pltpu.roll(b, s, 0), row >= s
        b = jnp.where(m, a * b_s + b, b)
        a = jnp.where(m, a * a_s, a)
    return a, b


def _chunk_scan(a_s, b_s, out_ref, carry, n_tiles, width, reverse):
    row = lax.broadcasted_iota(jnp.int32, (8, width), 0)
    edge = 0 if reverse else 7

    group = 4 if n_tiles % 4 == 0 else 1

    def step(j, c):
        jj = (n_tiles // group - 1 - j) if reverse else j
        base = pl.multiple_of(jj * (8 * group), 8 * group)
        order = range(group - 1, -1, -1) if reverse else range(group)
        parts = {t: _tile_scan(a_s[pl.ds(base + 8 * t, 8), :], b_s[pl.ds(base + 8 * t, 8), :], row, reverse)
                 for t in order}
        for t in order:
            h = parts[t][0] * c + parts[t][1]
            out_ref[pl.ds(base + 8 * t, 8), :] = h
            c = jnp.broadcast_to(h[edge:edge + 1, :], (8, width))
        return c

    carry[...] = lax.fori_loop(0, n_tiles // group, step, carry[...])


def _mixer_specs(Tc, bw, nb, layer):
    def seg(s):
        return pl.BlockSpec((Tc, bw), lambda n, i: (i, s * nb + n))

    p_spec = pl.BlockSpec((None, R_ROWS, bw), lambda n, i: (layer, 0, n))
    w_spec = pl.BlockSpec((None, None, bw, bw), lambda n, i: (layer, n, 0, 0))
    return seg, p_spec, w_spec


def _mixer_fwd(name, layer, z, pch, wr, wi, comm=None):
    T, D = z.shape[0], z.shape[1] // 7
    bw, nb = wr.shape[-1], wr.shape[1]
    Tc = min(SEQ_CHUNK, T)
    nT = T // Tc
    n_cin = len(comm.ins) if comm else 0
    n_cout = len(comm.out_shapes) if comm else 0

    def body(*refs):
        xa_ref, ya_ref, cb_ref, cc_ref, cx_ref, p_ref, wr_ref, wi_ref = refs[:8]
        pa_ref, pb_ref, xc_ref, hl_ref, vb_ref = refs[8 + n_cin:13 + n_cin]
        xa_buf, u_buf, a_s, b_s, carry = refs[13 + n_cin + n_cout:18 + n_cin + n_cout]
        if comm:
            c_in, c_out, sems = refs[8:8 + n_cin], refs[13 + n_cin:13 + n_cin + n_cout], refs[18 + n_cin + n_cout:]
            step = pl.program_id(0) * nT + pl.program_id(1)
            comm.ride(step, nb * nT, c_in, c_out, sems, True)

        @pl.when(pl.program_id(1) == 0)
        def _():
            xa_buf[0:HALO, :] = jnp.zeros((HALO, bw), F32)
            u_buf[0:HALO, :] = jnp.zeros((HALO, bw), F32)
            carry[...] = jnp.zeros_like(carry)

        rb = min(MIXER_ROW_BLOCK, Tc)
        blocks = [(g * rb, slice(g * rb, (g + 1) * rb)) for g in range(Tc // rb)]

        def prow(k):
            return p_ref[k:k + 1, :]

        xa_buf[HALO:HALO + Tc, :] = xa_ref[...]
        u_buf[HALO:HALO + Tc, :] = cc_ref[...] * cx_ref[...]
        for r0, rs in blocks:
            xc = prow(R_CAB)
            for k in range(4):
                xc = xc + prow(R_CAW + k) * xa_buf[HALO - 3 + k + r0:HALO - 3 + k + r0 + rb, :]
            xc_ref[rs, :] = xc
        xcb = xc_ref[...].astype(BF16)
        a_s[...] = jnp.dot(xcb, wr_ref[...], preferred_element_type=F32)
        b_s[...] = jnp.dot(xcb, wi_ref[...], preferred_element_type=F32)
        sp = _softplus_neg(prow(R_LAM))
        for _, rs in blocks:
            r = _sigmoid(a_s[rs, :] + prow(R_BR))
            ig = _sigmoid(b_s[rs, :] + prow(R_BI))
            log_a = (-LRU_C) * r * sp
            t = jnp.tanh(log_a)
            a_s[rs, :] = jnp.exp(log_a)
            b_s[rs, :] = jnp.sqrt(-2.0 * t / (1.0 - t)) * (ig * xc_ref[rs, :])
        _chunk_scan(a_s, b_s, hl_ref, carry, Tc // 8, bw, False)
        for r0, rs in blocks:
            g, _ = _gelu_and_grad(ya_ref[rs, :])
            pa_ref[rs, :] = (hl_ref[rs, :] * g).astype(BF16)
            vb = jnp.zeros((rb, bw), F32)
            for k in range(3):
                vb = vb + prow(R_CBW + k) * u_buf[HALO - 2 + k + r0:HALO - 2 + k + r0 + rb, :]
            vb_ref[rs, :] = vb
            pb_ref[rs, :] = (cb_ref[rs, :] * vb).astype(BF16)
        xa_buf[0:HALO, :] = xa_buf[Tc:Tc + HALO, :]
        u_buf[0:HALO, :] = u_buf[Tc:Tc + HALO, :]

        if comm:
            comm.ride(step, nb * nT, c_in, c_out, sems, False)

    seg, p_spec, w_spec = _mixer_specs(Tc, bw, nb, layer)
    out = pl.BlockSpec((Tc, bw), lambda n, i: (i, n))
    res = pl.pallas_call(
        body,
        grid=(nb, nT),
        in_specs=[seg(0), seg(1), seg(2), seg(3), seg(4), p_spec, w_spec, w_spec] + [ANY] * n_cin,
        out_specs=[out] * 5 + [ANY] * n_cout,
        out_shape=[jax.ShapeDtypeStruct((T, D), BF16), jax.ShapeDtypeStruct((T, D), BF16),
                   jax.ShapeDtypeStruct((T, D), F32), jax.ShapeDtypeStruct((T, D), F32),
                   jax.ShapeDtypeStruct((T, D), F32)] + (comm.out_shapes if comm else []),
        scratch_shapes=[pltpu.VMEM((Tc + HALO, bw), F32), pltpu.VMEM((Tc + HALO, bw), F32),
                        pltpu.VMEM((Tc, bw), F32), pltpu.VMEM((Tc, bw), F32), pltpu.VMEM((8, bw), F32)]
        + (comm.scratch() if comm else []),
        input_output_aliases={8 + ci: 5 + co for ci, co in comm.aliases.items()} if comm else {},
        compiler_params=_cparams(("arbitrary", "arbitrary") if comm else ("parallel", "arbitrary")),
        name=name,
    )(z, z, z, z, z, pch, wr, wi, *(comm.ins if comm else []))
    return (res[:5], res[5:]) if comm else res


def _mixer_bwd(name, layer, z, xc, hl, vb, dpa, dpb, dga, dgb, pch, wr, wi, comm=None):
    T, D = z.shape[0], z.shape[1] // 7
    bw, nb = wr.shape[-1], wr.shape[1]
    Tc = min(SEQ_CHUNK, T)
    nT = T // Tc
    tpc = Tc // 8
    rb = min(MIXER_ROW_BLOCK, Tc)
    n_cin = len(comm.ins) if comm else 0
    n_cout = len(comm.out_shapes) if comm else 0
    total = nb * nT

    def body(*refs):
        (xa_ref, ya_ref, cb_ref, cc_ref, cx_ref, xc_ref, hl_ref, hp_ref, vb_ref, dpa_ref, dpb_ref, dga_ref, dgb_ref,
         p_ref, wr_ref, wi_ref) = refs[:16]
        dz_ref, dwr_ref, dwi_ref, sm_ref = refs[16 + n_cin:20 + n_cin]
        (h_buf, a_buf, dxc_buf, dvb_buf, a_s, d_s, lam_s, r_s, i_s, m_s, dpr_s, dpi_s, sm8,
         carry, stage, out_sems) = refs[20 + n_cin + n_cout:36 + n_cin + n_cout]
        i = pl.program_id(1)
        step = pl.program_id(0) * nT + i
        if comm:
            c_in, c_out = refs[16:16 + n_cin], refs[20 + n_cin:20 + n_cin + n_cout]
            sems = refs[36 + n_cin + n_cout:]
            comm.ride(step, total, c_in, c_out, sems, True)

        slot = step % 2

        def out_copies(sl):
            rows = pl.ds(pl.multiple_of((nT - 1 - i) * Tc, Tc), Tc)
            return [pltpu.make_async_copy(
                stage.at[sl, s], dz_ref.at[rows, pl.ds(pl.multiple_of((s * nb + pl.program_id(0)) * bw, bw), bw)],
                out_sems.at[sl, s]) for s in range(7)]

        @pl.when(step >= 2)
        def _():
            for cp in out_copies(slot):
                cp.wait()

        dxa_ref, dya_ref, dcb_ref, dcc_ref, dcx_ref = [stage.at[slot, s] for s in range(5)]
        stage[slot, 5, :, :] = dga_ref[...]
        stage[slot, 6, :, :] = dgb_ref[...]

        @pl.when(i == 0)
        def _():
            a_buf[Tc:Tc + HALO, :] = jnp.zeros((HALO, bw), F32)
            dxc_buf[Tc:Tc + HALO, :] = jnp.zeros((HALO, bw), F32)
            dvb_buf[Tc:Tc + HALO, :] = jnp.zeros((HALO, bw), F32)
            carry[...] = jnp.zeros_like(carry)
            dwr_ref[...] = jnp.zeros_like(dwr_ref)
            dwi_ref[...] = jnp.zeros_like(dwi_ref)
            sm8[...] = jnp.zeros_like(sm8)

        blocks = [(g * rb, slice(g * rb, (g + 1) * rb)) for g in range(Tc // rb)]

        def prow(k):
            return p_ref[k:k + 1, :]

        def part8(v):
            return jnp.sum(v.reshape(rb // 8, 8, bw), axis=0)

        sums8 = {}

        def tally(k, v):
            sums8[k] = sums8[k] + part8(v) if k in sums8 else part8(v)

        sp = _softplus_neg(prow(R_LAM))
        xcb = xc_ref[...].astype(BF16)
        r_s[...] = jnp.dot(xcb, wr_ref[...], preferred_element_type=F32)
        i_s[...] = jnp.dot(xcb, wi_ref[...], preferred_element_type=F32)
        for _, rs in blocks:
            r = _sigmoid(r_s[rs, :] + prow(R_BR))
            ig = _sigmoid(i_s[rs, :] + prow(R_BI))
            log_a = (-LRU_C) * r * sp
            t = jnp.tanh(log_a)
            r_s[rs, :] = r
            i_s[rs, :] = ig
            m_s[rs, :] = jnp.sqrt(-2.0 * t / (1.0 - t))
            a_buf[rs, :] = jnp.exp(log_a)
            g, gp = _gelu_and_grad(ya_ref[rs, :])
            dpav = dpa_ref[rs, :]
            d_s[rs, :] = dpav * g
            dya_ref[rs, :] = (dpav * hl_ref[rs, :] * gp).astype(BF16)
            dpbv = dpb_ref[rs, :]
            dcb_ref[rs, :] = (dpbv * vb_ref[rs, :]).astype(BF16)
            dvb_buf[rs, :] = dpbv * cb_ref[rs, :]

        a_s[...] = a_buf[1:Tc + 1, :]
        _chunk_scan(a_s, d_s, lam_s, carry, tpc, bw, True)
        h_buf[HALO:HALO + Tc, :] = hl_ref[...]
        h_buf[0:HALO, :] = jnp.where(i == nT - 1, 0.0, hp_ref[...])

        for r0, rs in blocks:
            lamv, xcv, r, ig, a = lam_s[rs, :], xc_ref[rs, :], r_s[rs, :], i_s[rs, :], a_buf[rs, :]
            mult = m_s[rs, :]
            da = lamv * h_buf[HALO - 1 + r0:HALO - 1 + r0 + rb, :]
            dmult = lamv * (ig * xcv)
            dbx = lamv * mult
            dig = dbx * xcv
            dxc_buf[rs, :] = dbx * ig
            dlog_a = da * a - dmult * (a * a) / mult
            dpr = (dlog_a * ((-LRU_C) * sp)) * r * (1.0 - r)
            dpi = dig * ig * (1.0 - ig)
            dpr_s[rs, :] = dpr.astype(BF16)
            dpi_s[rs, :] = dpi.astype(BF16)
            tally(R_BR, dpr)
            tally(R_BI, dpi)
            tally(R_LAM, dlog_a * ((-LRU_C) * r))

        dprb, dpib = dpr_s[...], dpi_s[...]
        nt = (((1,), (1,)), ((), ()))
        tn = (((0,), (0,)), ((), ()))
        dxc_buf[0:Tc, :] += (lax.dot_general(dprb, wr_ref[...], nt, preferred_element_type=F32)
                             + lax.dot_general(dpib, wi_ref[...], nt, preferred_element_type=F32))
        dwr_ref[...] += lax.dot_general(xcb, dprb, tn, preferred_element_type=F32)
        dwi_ref[...] += lax.dot_general(xcb, dpib, tn, preferred_element_type=F32)

        for r0, rs in blocks:
            xav = xa_ref[rs, :]
            tally(R_CAB, dxc_buf[rs, :])
            dxa = jnp.zeros((rb, bw), F32)
            for k in range(4):
                sh = dxc_buf[3 - k + r0:3 - k + r0 + rb, :]
                dxa = dxa + prow(R_CAW + k) * sh
                tally(R_CAW + k, xav * sh)
            dxa_ref[rs, :] = dxa.astype(BF16)
            ccv, cxv = cc_ref[rs, :], cx_ref[rs, :]
            u = ccv * cxv
            du = jnp.zeros((rb, bw), F32)
            for k in range(3):
                sh = dvb_buf[2 - k + r0:2 - k + r0 + rb, :]
                du = du + prow(R_CBW + k) * sh
                tally(R_CBW + k, u * sh)
            dcc_ref[rs, :] = (du * cxv).astype(BF16)
            dcx_ref[rs, :] = (du * ccv).astype(BF16)

        for k, v in sums8.items():
            sm8[8 * k:8 * k + 8, :] += v
        a_buf[Tc:Tc + HALO, :] = a_buf[0:HALO, :]
        dxc_buf[Tc:Tc + HALO, :] = dxc_buf[0:HALO, :]
        dvb_buf[Tc:Tc + HALO, :] = dvb_buf[0:HALO, :]

        @pl.when(i == nT - 1)
        def _():
            sm_ref[...] = jnp.sum(sm8[...].reshape(R_ROWS, 8, bw), axis=1)
            sm_ref[R_LAM:R_LAM + 1, :] = sm_ref[R_LAM:R_LAM + 1, :] * (-_sigmoid(-prow(R_LAM)))

        for cp in out_copies(slot):
            cp.start()

        @pl.when(step == total - 1)
        def _():
            if total > 1:
                for cp in out_copies(1 - slot):
                    cp.wait()
            for cp in out_copies(slot):
                cp.wait()

        if comm:
            comm.ride(step, total, c_in, c_out, sems, False)

    def seg(s):
        return pl.BlockSpec((Tc, bw), lambda n, i: (nT - 1 - i, s * nb + n))

    blk = pl.BlockSpec((Tc, bw), lambda n, i: (nT - 1 - i, n))
    halo = pl.BlockSpec((8, bw), lambda n, i: (jnp.maximum((nT - 1 - i) * tpc - 1, 0), n))
    p_spec = pl.BlockSpec((None, R_ROWS, bw), lambda n, i: (layer, 0, n))
    w_spec = pl.BlockSpec((None, None, bw, bw), lambda n, i: (layer, n, 0, 0))
    dw_spec = pl.BlockSpec((None, bw, bw), lambda n, i: (n, 0, 0))
    res = pl.pallas_call(
        body,
        grid=(nb, nT),
        in_specs=[seg(0), seg(1), seg(2), seg(3), seg(4), blk, blk, halo, blk, blk, blk, blk, blk,
                  p_spec, w_spec, w_spec] + [ANY] * n_cin,
        out_specs=[ANY, dw_spec, dw_spec, pl.BlockSpec((R_ROWS, bw), lambda n, i: (0, n))] + [ANY] * n_cout,
        out_shape=[jax.ShapeDtypeStruct((T, 7 * D), BF16), jax.ShapeDtypeStruct((nb, bw, bw), F32),
                   jax.ShapeDtypeStruct((nb, bw, bw), F32), jax.ShapeDtypeStruct((R_ROWS, D), F32)]
        + (comm.out_shapes if comm else []),
        scratch_shapes=[pltpu.VMEM((Tc + HALO, bw), F32)] * 4 + [pltpu.VMEM((Tc, bw), F32)] * 6
        + [pltpu.VMEM((Tc, bw), BF16)] * 2 + [pltpu.VMEM((8 * R_ROWS, bw), F32), pltpu.VMEM((8, bw), F32),
                                              pltpu.VMEM((2, 7, Tc, bw), BF16), pltpu.SemaphoreType.DMA((2, 7))]
        + (comm.scratch() if comm else []),
        input_output_aliases={16 + ci: 4 + co for ci, co in comm.aliases.items()} if comm else {},
        compiler_params=_cparams(("arbitrary", "arbitrary")),
        name=name,
    )(z, z, z, z, z, xc, hl, hl, vb, dpa, dpb, dga, dgb, pch, wr, wi, *(comm.ins if comm else []))
    return (res[:4], res[4:]) if comm else res


def _local_fwd_bwd(x, tgt, W, placed=None):
    T, D = x.shape
    g1, g2 = W["g1"], W["g2"]
    depth = g1.shape[0]
    FF = 4 * D
    if placed is None:
        mats = {(n, l): W[n][l] for n in MATS for l in range(depth)}
    else:
        mats = {}
        mats["w_in", 0], = _run_carry("gather_first", _gather_carry([placed["w_in", 0]], [(0, 1, 0, 1)]))

    def gathering(specs):
        if placed is None or not specs:
            return None, []
        keys = [(n, l) for n, l, _, _, _ in specs]
        arrays = [mats.get(k, placed[k]) for k in keys]
        return _gather_carry(arrays, [(i, d, part, nparts) for i, (_, _, d, part, nparts) in enumerate(specs)]), keys

    def hosted(call, specs, **kw):
        carry, keys = gathering(specs)
        if carry is None:
            return call(**kw)
        res, got = call(**kw, **{("comm" if call.func is _mixer_fwd else "carry"): carry})
        mats.update(zip(keys, got))
        return res

    saved = []
    xs = x
    for l in range(depth):
        h = _rms_fwd(f"rms1_fwd_{l}", xs, g1[l][None])
        nxt = l + 1 < depth
        projs = [("w_pa", l, 0, 0, 1), ("w_pb", l, 0, 0, 1), ("w_o", l, 0, 0, 1)]
        z = hosted(functools.partial(_mm, f"in_proj_{l}", "nn", h, mats["w_in", l], T, 7 * D, D, [F32]),
                   projs + ([("w_mlp1", l, 1, 0, 1)] if nxt else []),
                   epilogue=lambda acc, b: (acc + b,), extras=[(W["b_in"][l][None], "bias", 0)])
        pa, pb, xc, hl, vb = hosted(
            functools.partial(_mixer_fwd, f"mixer_fwd_{l}", l, z, W["pch"], W["wr"], W["wi"]),
            [("w_mlp2", l, 0, 0, 1)] if nxt else [("w_mlp1", l, 1, 0, 1)])
        oa = _mm(f"proj_a_{l}", "nn", pa, mats["w_pa", l], T, D, D, [F32])

        def merge(acc, oav, ga, gb):
            return acc, _sigmoid(ga) * oav + _sigmoid(gb) * acc

        ob, mg = _mm(f"proj_b_merge_{l}", "nn", pb, mats["w_pb", l], T, D, D, [F32, BF16], epilogue=merge,
                     tiles=MM_TILES_FUSED,
                     extras=[(oa, "tile", 0), (z, "tile", 5 * D), (z, "tile", 6 * D)])
        x1 = _mm(f"out_proj_{l}", "nn", mg, mats["w_o", l], T, D, D, [F32],
                 epilogue=lambda acc, res: (res + acc,), extras=[(xs, "tile", 0)])
        h2 = _rms_fwd(f"rms2_fwd_{l}", x1, g2[l][None])

        def relu2(acc):
            pr = jnp.maximum(acc, 0.0)
            return pr * pr, pr

        u, pr = hosted(functools.partial(_mm, f"mlp1_{l}", "nn", h2, mats["w_mlp1", l], T, FF, D, [BF16, BF16]),
                       [("w_in", l + 1, 1, 0, 2)] if nxt else [("w_mlp2", l, 0, 0, 1)], epilogue=relu2)
        x2 = hosted(functools.partial(_mm, f"mlp2_{l}", "nn", u, mats["w_mlp2", l], T, D, FF, [F32]),
                    [("w_in", l + 1, 1, 1, 2)] if nxt else [],
                    epilogue=lambda acc, res: (res + acc,), extras=[(x1, "tile", 0)])
        saved.append(dict(x0=xs, h=h, z=z, pa=pa, pb=pb, xc=xc, hl=hl, vb=vb, oa=oa, ob=ob, mg=mg, x1=x1,
                          h2=h2, u=u, pr=pr))
        xs = x2

    dx, dxb, dgf, loss_blk = _loss_head("loss_head", xs, W["gf"][None], tgt)

    gmat, got, sums, landed, acc = {}, {}, {}, {}, {}

    def reducing(call, swaps=(), scatters=(), part=None, join=None, mixer=False, **kw):
        swaps, scatters = [(n, l) for n in swaps], [(n, l) for n in scatters]
        carries, sinks = [], []
        if placed is not None and swaps:
            carries.append(_swap_carry([gmat[k] for k in swaps], [BIG_DIM[k[0]] - 1 for k in swaps]))
            sinks.append((got, swaps))
        if placed is not None and scatters:
            carries.append(_scatter_carry([sums[k] for k in scatters]))
            sinks.append((landed, scatters))
        if placed is not None and part is not None:
            key = part[:2]
            carries.append(_scatter_carry([sums[key]], part[2], part[3], [landed[key]] if key in landed else None))
            sinks.append((landed, [key]))
        if placed is not None and join is not None:
            carries.append(_join_carry([acc[n] for n in MATS], [BIG_DIM[n] for n in MATS], join))
            sinks.append((acc, list(MATS)))
        if not carries:
            return call(**kw)
        res, moved = call(**kw, **{("comm" if mixer else "carry"): _merge_carries(carries)})
        moved = list(moved)
        for target, keys in sinks:
            for k in keys:
                target[k] = moved.pop(0)
        return res

    def add(names):
        if placed is not None:
            for n in names:
                sums[n, l] = _add_halves(f"add_halves_{n}_{l}", gmat[n, l], got[n, l], BIG_DIM[n] - 1, W["core"])

    grads = [None] * depth
    for l in reversed(range(depth)):
        s = saved[l]
        dp = _mm(f"mlp2_dx_{l}", "nt", dxb, mats["w_mlp2", l], T, FF, D, [BF16],
                 epilogue=lambda acc, prv: (2.0 * prv.astype(F32) * acc,), extras=[(s["pr"], "tile", 0)])
        dw2 = gmat["w_mlp2", l] = reducing(functools.partial(_mm, f"mlp2_dw_{l}", "tn", s["u"], dxb, FF, D, T, [BF16]),
                                           part=("w_in", l + 1, 1, 2) if l + 1 < depth else None)
        above = l + 1 if placed is not None and l + 1 < depth else None
        if above is not None:
            for n in MATS:
                acc[n] = _reduce_into(f"reduce_{n}_{above}", sums[n, above], landed[n, above], acc.get(n), above,
                                      BIG_DIM[n], W["shard_shapes"][n], W["where"])
        dh2 = reducing(functools.partial(_mm, f"mlp1_dx_{l}", "nt", dp, mats["w_mlp1", l], T, D, FF, [F32]),
                       swaps=["w_mlp2"], tiles=MM_TILES_LONG_K)
        add(["w_mlp2"])
        dw1 = gmat["w_mlp1", l] = reducing(functools.partial(_mm, f"mlp1_dw_{l}", "tn", s["h2"], dp, D, FF, T, [BF16]),
                                           part=("w_mlp2", l, 0, 2))
        dx1, dx1b, dg2 = _rms_bwd(f"rms2_bwd_{l}", s["x1"], g2[l][None], dh2, dx)

        def unmerge(acc, ga, gb, oav, obv):
            sa, sb = _sigmoid(ga), _sigmoid(gb)
            return acc * sa, acc * sb, acc * oav * sa * (1.0 - sa), acc * obv * sb * (1.0 - sb)

        doa, dob, dga, dgb = reducing(
            functools.partial(_mm, f"out_proj_dx_{l}", "nt", dx1b, mats["w_o", l], T, D, D, [BF16] * 4),
            swaps=["w_mlp1"], part=("w_mlp2", l, 1, 2), join=above, tiles=MM_TILES_FUSED, epilogue=unmerge,
            extras=[(s["z"], "tile", 5 * D), (s["z"], "tile", 6 * D), (s["oa"], "tile", 0), (s["ob"], "tile", 0)])
        add(["w_mlp1"])
        dwo = gmat["w_o", l] = _mm(f"out_proj_dw_{l}", "tn", s["mg"], dx1b, D, D, T, [BF16])
        dpa = _mm(f"proj_a_dx_{l}", "nt", doa, mats["w_pa", l], T, D, D, [F32])
        dwpa = gmat["w_pa", l] = _mm(f"proj_a_dw_{l}", "tn", s["pa"], doa, D, D, T, [BF16])
        dpb = _mm(f"proj_b_dx_{l}", "nt", dob, mats["w_pb", l], T, D, D, [F32])
        dwpb = gmat["w_pb", l] = _mm(f"proj_b_dw_{l}", "tn", s["pb"], dob, D, D, T, [BF16])
        dz, dwr, dwi, sm = reducing(
            functools.partial(_mixer_bwd, f"mixer_bwd_{l}", l, s["z"], s["xc"], s["hl"], s["vb"], dpa, dpb, dga, dgb,
                              W["pch"], W["wr"], W["wi"]),
            swaps=["w_o", "w_pa", "w_pb"], scatters=["w_mlp1"], mixer=True)
        add(["w_o", "w_pa", "w_pb"])
        dwin = gmat["w_in", l] = reducing(
            functools.partial(_mm, f"in_proj_dw_{l}", "tn", s["h"], dz, D, 7 * D, T, [BF16]),
            scatters=["w_o", "w_pa", "w_pb"])
        dbin = reducing(functools.partial(_colsum, f"bias_grad_{l}", dz), swaps=["w_in"])
        add(["w_in"])
        dh = reducing(functools.partial(_mm, f"in_proj_dx_{l}", "nt", dz, mats["w_in", l], T, D, 7 * D, [F32]),
                      scatters=["w_in"] if l == 0 else [], part=("w_in", l, 0, 2) if l > 0 else None,
                      tiles=MM_TILES_LONG_K)
        dx, dxb, dg1 = _rms_bwd(f"rms1_bwd_{l}", s["x0"], g1[l][None], dh, dx1)
        grads[l] = dict(w_in=dwin, w_pa=dwpa, w_pb=dwpb, w_o=dwo, w_mlp1=dw1, w_mlp2=dw2, wr=dwr, wi=dwi,
                        sm=sm, b_in=dbin, g1=dg1, g2=dg2)
    return loss_blk[0, 0], dx, grads, dgf, sums, landed, acc


ANY = pl.BlockSpec(memory_space=pl.ANY)


def _place():
    x, y, c = lax.axis_index("x"), lax.axis_index("y"), lax.axis_index("c")
    peers = [(1 - x, y, c), (x, 1 - y, c), (1 - x, 1 - y, c)]
    chips = [2 * (1 - x) + y, 2 * x + (1 - y), 2 * (1 - x) + (1 - y)]
    return (x, y, c), 2 * x + y, peers, chips


def _window(ref, dim, q, size):
    idx = [slice(None)] * len(ref.shape)
    idx[dim] = pl.ds(q * size, size)
    return ref.at[tuple(idx)]


def _gather_weights(shards, dims, small):
    n = len(shards)
    sizes = [s.shape[d] for s, d in zip(shards, dims)]
    full = [jax.ShapeDtypeStruct(s.shape[:d] + (s.shape[d] * N_CHIPS,) + s.shape[d + 1:], s.dtype)
            for s, d in zip(shards, dims)]
    full.append(jax.ShapeDtypeStruct((N_CHIPS,) + small.shape, small.dtype))

    def body(*refs):
        ins, outs = refs[:n + 1], refs[n + 1:2 * n + 2]
        send_sems, recv_sems, local_sems = refs[2 * n + 2:]
        _, k, peers, chips = _place()

        def dst(w, q):
            return outs[w].at[q] if w == n else _window(outs[w], dims[w], q, sizes[w])

        local = [pltpu.make_async_copy(ins[w], dst(w, k), local_sems.at[w]) for w in range(n + 1)]
        for cp in local:
            cp.start()
        sends = []
        for p, peer in enumerate(peers):
            for w in range(n + 1):
                s = p * (n + 1) + w
                sends.append(pltpu.make_async_remote_copy(
                    src_ref=ins[w], dst_ref=dst(w, k), send_sem=send_sems.at[s], recv_sem=recv_sems.at[s],
                    device_id=peer, device_id_type=MESH))
        for cp in sends:
            cp.start()
        for p, peer in enumerate(peers):
            for w in range(n + 1):
                s = p * (n + 1) + w
                pltpu.make_async_remote_copy(
                    src_ref=ins[w], dst_ref=dst(w, chips[p]), send_sem=send_sems.at[s], recv_sem=recv_sems.at[s],
                    device_id=peer, device_id_type=MESH).wait_recv()
        for cp in sends:
            cp.wait_send()
        for cp in local:
            cp.wait()

    return pl.pallas_call(
        body,
        in_specs=[ANY] * (n + 1),
        out_specs=[ANY] * (n + 1),
        out_shape=full,
        scratch_shapes=[pltpu.SemaphoreType.DMA((3 * (n + 1),)), pltpu.SemaphoreType.DMA((3 * (n + 1),)),
                        pltpu.SemaphoreType.DMA((n + 1,))],
        name="gather_weights",
    )(*shards, small)


def _scatter_grads(grads, dims):
    n, depth = len(grads), len(grads[0])
    sizes = [g[0].shape[d] // N_CHIPS for g, d in zip(grads, dims)]
    land = []
    for g, d, sz in zip(grads, dims, sizes):
        shp = g[0].shape
        land.append(jax.ShapeDtypeStruct((N_CHIPS, depth) + shp[:d] + (sz,) + shp[d + 1:], g[0].dtype))

    def body(*refs):
        ins, outs = refs[:n * depth], refs[n * depth:n * depth + n]
        send_sems, recv_sems, local_sems = refs[n * depth + n:]
        _, k, peers, chips = _place()

        def src(w, l, q):
            return _window(ins[w * depth + l], dims[w], q, sizes[w])

        local = [pltpu.make_async_copy(src(w, l, k), outs[w].at[3, l], local_sems.at[w * depth + l])
                 for w in range(n) for l in range(depth)]
        for cp in local:
            cp.start()
        sends = []
        for p, peer in enumerate(peers):
            for w in range(n):
                for l in range(depth):
                    s = (p * n + w) * depth + l
                    sends.append(pltpu.make_async_remote_copy(
                        src_ref=src(w, l, chips[p]), dst_ref=outs[w].at[p, l], send_sem=send_sems.at[s],
                        recv_sem=recv_sems.at[s], device_id=peer, device_id_type=MESH))
        for cp in sends:
            cp.start()
        for cp in sends:
            cp.wait_recv()
        for cp in sends:
            cp.wait_send()
        for cp in local:
            cp.wait()

    flat = [g for gl in grads for g in gl]
    return pl.pallas_call(
        body,
        in_specs=[ANY] * (n * depth),
        out_specs=[ANY] * n,
        out_shape=land,
        scratch_shapes=[pltpu.SemaphoreType.DMA((3 * n * depth,)), pltpu.SemaphoreType.DMA((3 * n * depth,)),
                        pltpu.SemaphoreType.DMA((n * depth,))],
        name="scatter_grads",
    )(*flat)


def _sum_slots(name, land):
    _, R, C = land.shape
    tr, tc = _div_tile(R, 512, 8), _div_tile(C, 1024)

    def body(a_ref, b_ref, c_ref, d_ref, o_ref):
        o_ref[...] = ((d_ref[...].astype(F32) + a_ref[...].astype(F32)) + b_ref[...].astype(F32)) \
            + c_ref[...].astype(F32)

    def slot(q):
        return pl.BlockSpec((None, tr, tc), lambda i, j: (q, i, j))

    return pl.pallas_call(
        body,
        grid=(R // tr, C // tc),
        in_specs=[slot(0), slot(1), slot(2), slot(3)],
        out_specs=pl.BlockSpec((tr, tc), lambda i, j: (i, j)),
        out_shape=jax.ShapeDtypeStruct((R, C), F32),
        compiler_params=_cparams(("parallel", "parallel")),
        name=name,
    )(land, land, land, land)


def _swap_with_sibling(parts):
    n = len(parts)

    def body(*refs):
        ins, outs = refs[:n], refs[n:2 * n]
        send_sems, recv_sems = refs[2 * n:]
        (x, y, c), _, _, _ = _place()
        copies = [pltpu.make_async_remote_copy(
            src_ref=ins[w], dst_ref=outs[w], send_sem=send_sems.at[w], recv_sem=recv_sems.at[w],
            device_id=(x, y, 1 - c), device_id_type=MESH) for w in range(n)]
        for cp in copies:
            cp.start()
        for cp in copies:
            cp.wait()

    return pl.pallas_call(
        body,
        in_specs=[ANY] * n,
        out_specs=[ANY] * n,
        out_shape=[jax.ShapeDtypeStruct(p.shape, p.dtype) for p in parts],
        scratch_shapes=[pltpu.SemaphoreType.DMA((n,)), pltpu.SemaphoreType.DMA((n,))],
        name="swap_with_sibling",
    )(*parts)


CARRY_MARKS = (0.6, 0.92)


class _Carry:
    def __init__(self, ins, out_shapes, rounds, counts, aliases=None, marks=CARRY_MARKS):
        self.ins, self.out_shapes, self.rounds, self.counts = list(ins), list(out_shapes), list(rounds), list(counts)
        self.aliases, self.marks = dict(aliases or {}), marks

    def scratch(self):
        return [pltpu.SemaphoreType.DMA((n,)) for n in self.counts for _ in range(2)]

    def _copies(self, r, in_refs, out_refs, sems, landing):
        remote = self.rounds[r](in_refs, out_refs)
        assert len(remote) == self.counts[r], (r, len(remote), self.counts[r])
        return [pltpu.make_async_remote_copy(src_ref=s, dst_ref=(land if landing else d), send_sem=sems[2 * r].at[i],
                                             recv_sem=sems[2 * r + 1].at[i], device_id=peer, device_id_type=MESH)
                for i, (s, d, peer, land) in enumerate(remote)]

    def begin(self, r, in_refs, out_refs, sems):
        if r > 0:
            for cp in self._copies(r - 1, in_refs, out_refs, sems, True):
                cp.wait_recv()
        for cp in self._copies(r, in_refs, out_refs, sems, False):
            cp.start()

    def end(self, in_refs, out_refs, sems):
        last = len(self.rounds) - 1
        for cp in self._copies(last, in_refs, out_refs, sems, True):
            cp.wait_recv()
        for r in range(last + 1):
            for cp in self._copies(r, in_refs, out_refs, sems, False):
                cp.wait_send()

    def ride(self, step, total, in_refs, out_refs, sems, first):
        if first:
            @pl.when(step == 0)
            def _():
                self.begin(0, in_refs, out_refs, sems)
            return
        for r in range(1, len(self.rounds)):
            @pl.when(step == min(total - 1, int(total * self.marks[r - 1])))
            def _(r=r):
                self.begin(r, in_refs, out_refs, sems)

        @pl.when(step == total - 1)
        def _():
            self.end(in_refs, out_refs, sems)


def _run_carry(name, carry):
    n_in, n_out = len(carry.ins), len(carry.out_shapes)

    def body(*refs):
        in_refs, out_refs, sems = refs[:n_in], refs[n_in:n_in + n_out], refs[n_in + n_out:]
        for r in range(len(carry.rounds)):
            carry.begin(r, in_refs, out_refs, sems)
        carry.end(in_refs, out_refs, sems)

    return pl.pallas_call(
        body,
        in_specs=[ANY] * n_in,
        out_specs=[ANY] * n_out,
        out_shape=carry.out_shapes,
        scratch_shapes=carry.scratch(),
        input_output_aliases=carry.aliases,
        name=name,
    )(*carry.ins)


def _gather_carry(arrays, items):
    shapes = [a.shape for a in arrays]

    def ring():
        x, y, c = lax.axis_index("x"), lax.axis_index("y"), lax.axis_index("c")
        first = (x + (1 - c) * (1 - 2 * x), y + c * (1 - 2 * y), c)
        second = (x + c * (1 - 2 * x), y + (1 - c) * (1 - 2 * y), c)
        return c, 2 * x + y, first, second

    def chip(pos):
        return 2 * pos[0] + pos[1]

    def round0(ins, outs):
        c, k, first, second = ring()
        remote = []
        for item in items:
            win = window(outs[item[0]], item)
            remote.append((win(2 * k + c), win(2 * k + c), first, win(2 * chip(first) + c)))
            remote.append((win(2 * k + c), win(2 * k + c), second, win(2 * chip(second) + c)))
        return remote

    def round1(ins, outs):
        c, k, first, second = ring()
        remote = []
        for item in items:
            win = window(outs[item[0]], item)
            relayed = win(2 * chip(first) + c)
            remote.append((relayed, relayed, second, win(2 * (3 - k) + c)))
        return remote

    def window(ref, item):
        idx, d, part, nparts = item
        h = shapes[idx][d] // (2 * N_CHIPS)
        rows = shapes[idx][1 - d] // nparts

        def win(j):
            sl = [None, None]
            sl[d] = pl.ds(j * h, h)
            sl[1 - d] = pl.ds(part * rows, rows)
            return ref.at[tuple(sl)]

        return win

    def round2(ins, outs):
        (x, y, c), _, _, chips = _place()
        remote = []
        for item in items:
            win = window(outs[item[0]], item)
            for p in range(3):
                remote.append((win(2 * chips[p] + c), win(2 * chips[p] + c), (x, y, 1 - c),
                               win(2 * chips[p] + 1 - c)))
        return remote

    n = len(items)
    return _Carry(arrays, [jax.ShapeDtypeStruct(a.shape, a.dtype) for a in arrays], [round0, round1, round2],
                  [2 * n, n, 3 * n], aliases={i: i for i in range(len(arrays))})


def _place_shard(name, w, layer, dim, chip):
    _, a, b = w.shape
    full = (a * N_CHIPS, b) if dim == 0 else (a, b * N_CHIPS)
    tr, tc = _div_tile(a, 512, 16), _div_tile(b, 2048)
    nr, nc = a // tr, b // tc

    def out_map(i, j, chip_ref):
        return (chip_ref[0] * nr + i, j) if dim == 0 else (i, chip_ref[0] * nc + j)

    def body(chip_ref, w_ref, o_ref):
        o_ref[...] = w_ref[...].astype(o_ref.dtype)

    return pl.pallas_call(
        body,
        grid_spec=pltpu.PrefetchScalarGridSpec(
            num_scalar_prefetch=1, grid=(nr, nc),
            in_specs=[pl.BlockSpec((None, tr, tc), lambda i, j, chip_ref: (layer, i, j))],
            out_specs=pl.BlockSpec((tr, tc), out_map)),
        out_shape=jax.ShapeDtypeStruct(full, BF16),
        compiler_params=_cparams(("parallel", "parallel")),
        name=name,
    )(chip, w)


def _half_shape(shape, dim):
    return shape[:dim] + (shape[dim] // (2 * N_CHIPS),) + shape[dim + 1:]


def _swap_carry(grads, dims):
    shapes = [jax.ShapeDtypeStruct((N_CHIPS,) + _half_shape(g.shape, d), g.dtype) for g, d in zip(grads, dims)]

    def plan(ins, outs):
        (x, y, c), _, _, _ = _place()
        remote = []
        for w, d in enumerate(dims):
            h = grads[w].shape[d] // (2 * N_CHIPS)
            for q in range(N_CHIPS):
                remote.append((_window(ins[w], d, 2 * q + 1 - c, h), outs[w].at[q], (x, y, 1 - c), outs[w].at[q]))
        return remote

    return _Carry(grads, shapes, [plan], [N_CHIPS * len(grads)])


def _scatter_carry(sums, part=0, nparts=1, land=None):
    n = len(sums)

    def plan(ins, outs):
        _, _, peers, chips = _place()
        remote = []
        for w in range(n):
            r = sums[w].shape[1] // nparts
            rows = pl.ds(part * r, r)
            for p in range(3):
                remote.append((ins[w].at[chips[p], rows], outs[w].at[p, rows], peers[p], outs[w].at[p, rows]))
        return remote

    shapes = [jax.ShapeDtypeStruct((3,) + s.shape[1:], s.dtype) for s in sums]
    if land is None:
        return _Carry(sums, shapes, [plan], [3 * n])
    return _Carry(list(sums) + list(land), shapes, [plan], [3 * n], aliases={n + w: w for w in range(n)})


def _merge_carries(carries):
    carries = [c for c in carries if c is not None]
    if len(carries) <= 1:
        return carries[0] if carries else None
    ins = [a for c in carries for a in c.ins]
    outs = [s for c in carries for s in c.out_shapes]

    def plan(in_refs, out_refs):
        remote, i0, o0 = [], 0, 0
        for c in carries:
            remote += c.rounds[0](in_refs[i0:i0 + len(c.ins)], out_refs[o0:o0 + len(c.out_shapes)])
            i0, o0 = i0 + len(c.ins), o0 + len(c.out_shapes)
        return remote

    assert all(len(c.rounds) == 1 for c in carries)
    aliases, i0, o0 = {}, 0, 0
    for c in carries:
        aliases.update({i0 + ci: o0 + co for ci, co in c.aliases.items()})
        i0, o0 = i0 + len(c.ins), o0 + len(c.out_shapes)
    return _Carry(ins, outs, [plan], [sum(c.counts[0] for c in carries)], aliases=aliases)


def _add_halves(name, g, got, dim, core):
    R, C = g.shape
    if dim == 1:
        r, cc = R, C // (2 * N_CHIPS)
    else:
        r, cc = R // (2 * N_CHIPS), C
    tr, tc = _div_tile(r, 1024, 16), _div_tile(cc, 2048)
    nr, nc = r // tr, cc // tc

    def g_map(q, i, j, core_ref):
        w = 2 * q + core_ref[0]
        return (i, w * nc + j) if dim == 1 else (w * nr + i, j)

    def body(core_ref, g_ref, got_ref, o_ref):
        o_ref[...] = (g_ref[...].astype(F32) + got_ref[...].astype(F32)).astype(o_ref.dtype)

    slab = pl.BlockSpec((None, tr, tc), lambda q, i, j, core_ref: (q, i, j))
    return pl.pallas_call(
        body,
        grid_spec=pltpu.PrefetchScalarGridSpec(
            num_scalar_prefetch=1, grid=(N_CHIPS, nr, nc),
            in_specs=[pl.BlockSpec((tr, tc), g_map), slab], out_specs=slab),
        out_shape=jax.ShapeDtypeStruct((N_CHIPS, r, cc), g.dtype),
        compiler_params=_cparams(("parallel", "parallel", "parallel")),
        name=name,
    )(core, g, got)


def _reduce_into(name, sums, land, acc, layer, dim, shape, where):
    _, r, cc = sums.shape
    tr, tc = _div_tile(r, 512, 16), _div_tile(cc, 1024)
    nr, nc = r // tr, cc // tc

    def out_map(i, j, s):
        return (layer, s[1] * nr + i, j) if dim == 1 else (layer, i, s[1] * nc + j)

    def body(*refs):
        own, a_ref, b_ref, c_ref, o_ref = refs[1], refs[2], refs[3], refs[4], refs[-1]
        o_ref[...] = ((own[...].astype(F32) + a_ref[...].astype(F32)) + b_ref[...].astype(F32)) \
            + c_ref[...].astype(F32)

    def slot(p):
        return pl.BlockSpec((None, tr, tc), lambda i, j, s: (p, i, j))

    in_specs = [pl.BlockSpec((None, tr, tc), lambda i, j, s: (s[0], i, j)), slot(0), slot(1), slot(2)]
    args = [where, sums, land, land, land]
    if acc is not None:
        in_specs.append(ANY)
        args.append(acc)
    return pl.pallas_call(
        body,
        grid_spec=pltpu.PrefetchScalarGridSpec(
            num_scalar_prefetch=1, grid=(nr, nc), in_specs=in_specs,
            out_specs=pl.BlockSpec((None, tr, tc), out_map)),
        out_shape=jax.ShapeDtypeStruct(shape, F32),
        input_output_aliases={5: 0} if acc is not None else {},
        compiler_params=_cparams(("parallel", "parallel")),
        name=name,
    )(*args)


def _join_carry(grads, dims, layer):
    n = len(grads)

    def plan(ins, outs):
        (x, y, c), _, _, _ = _place()
        remote = []
        for w in range(n):
            d, h = dims[w], grads[w].shape[dims[w]] // 2

            def win(half, w=w, d=d, h=h):
                idx = [slice(None)] * len(grads[w].shape)
                idx[0], idx[d] = layer, pl.ds(half * h, h)
                return outs[w].at[tuple(idx)]

            remote.append((win(c), win(c), (x, y, 1 - c), win(1 - c)))
        return remote

    return _Carry(grads, [jax.ShapeDtypeStruct(g.shape, g.dtype) for g in grads], [plan], [n],
                  aliases={w: w for w in range(n)})


def _allreduce_small(pack):
    R, C = pack.shape

    def gather_body(in_ref, slots_ref, send_sems, recv_sems, local_sem):
        x, y, c = lax.axis_index("x"), lax.axis_index("y"), lax.axis_index("c")
        me = 4 * x + 2 * y + c
        flips = [(dx, dy, dc) for dx in (0, 1) for dy in (0, 1) for dc in (0, 1)][1:]

        def flip(v, d):
            return 1 - v if d else v

        local = pltpu.make_async_copy(in_ref, slots_ref.at[me], local_sem)
        local.start()
        sends = []
        for j, (dx, dy, dc) in enumerate(flips):
            px, py, pc = flip(x, dx), flip(y, dy), flip(c, dc)
            sends.append((pltpu.make_async_remote_copy(
                src_ref=in_ref, dst_ref=slots_ref.at[me], send_sem=send_sems.at[j], recv_sem=recv_sems.at[j],
                device_id=(px, py, pc), device_id_type=MESH), 4 * px + 2 * py + pc, j))
        for cp, _, _ in sends:
            cp.start()
        for cp, peer_id, j in sends:
            pltpu.make_async_remote_copy(
                src_ref=in_ref, dst_ref=slots_ref.at[peer_id], send_sem=send_sems.at[j], recv_sem=recv_sems.at[j],
                device_id=(x, y, c), device_id_type=MESH).wait_recv()
        for cp, _, _ in sends:
            cp.wait_send()
        local.wait()

    slots = pl.pallas_call(
        gather_body,
        in_specs=[ANY],
        out_specs=ANY,
        out_shape=jax.ShapeDtypeStruct((N_DEV, R, C), pack.dtype),
        scratch_shapes=[pltpu.SemaphoreType.DMA((N_DEV - 1,)), pltpu.SemaphoreType.DMA((N_DEV - 1,)),
                        pltpu.SemaphoreType.DMA],
        name="allgather_small",
    )(pack)

    def sum_body(s_ref, o_ref):
        acc = s_ref[0]
        for d in range(1, N_DEV):
            acc = acc + s_ref[d]
        o_ref[...] = acc

    return pl.pallas_call(
        sum_body,
        out_shape=jax.ShapeDtypeStruct((R, C), pack.dtype),
        name="sum_small",
    )(slots)


def _adamw_math(w, g, m, v):
    m2 = ADAM_B1 * m + (1.0 - ADAM_B1) * g
    v2 = ADAM_B2 * v + (1.0 - ADAM_B2) * (g * g)
    m_hat = m2 / (1.0 - ADAM_B1 ** ADAM_STEP)
    v_hat = v2 / (1.0 - ADAM_B2 ** ADAM_STEP)
    delta = -ADAM_LR * (m_hat / (jnp.sqrt(v_hat) + ADAM_EPS) + ADAM_WD * w)
    return delta, m2, v2


def _adamw_big(name, w, m, v, g_parts):
    shape = w.shape
    C = shape[-1]
    R = w.size // C
    tr, tc = _div_tile(R, 256, 8), _div_tile(C, 1024)
    n_g = len(g_parts)

    def body(*refs):
        w_ref, m_ref, v_ref = refs[:3]
        g_ref, d_ref, nm_ref, nv_ref = refs[3 + n_g:]
        g = refs[3][...]
        for extra in refs[4:3 + n_g]:
            g = g + extra[...]
        delta, m2, v2 = _adamw_math(w_ref[...], g, m_ref[...], v_ref[...])
        g_ref[...], d_ref[...], nm_ref[...], nv_ref[...] = g, delta, m2, v2

    blk = pl.BlockSpec((tr, tc), lambda i, j: (i, j))
    outs = pl.pallas_call(
        body,
        grid=(R // tr, C // tc),
        in_specs=[blk] * (3 + n_g),
        out_specs=[blk] * 4,
        out_shape=[jax.ShapeDtypeStruct((R, C), F32)] * 4,
        compiler_params=_cparams(("parallel", "parallel")),
        name=name,
    )(w.reshape(R, C), m.reshape(R, C), v.reshape(R, C), *[g.reshape(R, C) for g in g_parts])
    return [o.reshape(shape) for o in outs]


def _adamw_small(name, w, g, m, v):
    shape = w.shape
    two_d = (w.size // shape[-1], shape[-1])

    def body(w_ref, g_ref, m_ref, v_ref, d_ref, nm_ref, nv_ref):
        d_ref[...], nm_ref[...], nv_ref[...] = _adamw_math(w_ref[...], g_ref[...], m_ref[...], v_ref[...])

    outs = pl.pallas_call(
        body,
        out_shape=[jax.ShapeDtypeStruct(two_d, F32)] * 3,
        name=name,
    )(w.reshape(two_d), g.reshape(two_d), m.reshape(two_d), v.reshape(two_d))
    return [o.reshape(shape) for o in outs]


SMALL_ROWS = 40
S_BIN, S_G1, S_G2 = 16, 24, 32
MATS = ("w_in", "w_pa", "w_pb", "w_o", "w_mlp1", "w_mlp2")
LRU = ("lru_wr", "lru_wi")
BIG_DIM = dict(w_in=2, w_pa=1, w_pb=1, w_o=1, w_mlp1=2, w_mlp2=1, lru_wr=2, lru_wi=2)
WEIGHTS = ("norm1_g", "w_in", "b_in", "conv_a_w", "conv_a_b", "lru_wr", "lru_br", "lru_wi", "lru_bi", "lru_lam",
           "conv_b_w", "w_pa", "w_pb", "w_o", "norm2_g", "w_mlp1", "w_mlp2", "final_g")


def _rows_at(a, r0, total):
    pad = [(0, 0)] * a.ndim
    pad[-2] = (r0, total - r0 - a.shape[-2])
    return jnp.pad(a, pad)


def kernel(x, norm1_g, w_in, b_in, conv_a_w, conv_a_b, lru_wr, lru_br, lru_wi, lru_bi, lru_lam, conv_b_w, w_pa, w_pb, w_o, norm2_g, w_mlp1, w_mlp2, final_g, loss_target, m_norm1_g, m_w_in, m_b_in, m_conv_a_w, m_conv_a_b, m_lru_wr, m_lru_br, m_lru_wi, m_lru_bi, m_lru_lam, m_conv_b_w, m_w_pa, m_w_pb, m_w_o, m_norm2_g, m_w_mlp1, m_w_mlp2, m_final_g, v_norm1_g, v_w_in, v_b_in, v_conv_a_w, v_conv_a_b, v_lru_wr, v_lru_br, v_lru_wi, v_lru_bi, v_lru_lam, v_conv_b_w, v_w_pa, v_w_pb, v_w_o, v_norm2_g, v_w_mlp1, v_w_mlp2, v_final_g):
    wts = dict(norm1_g=norm1_g, w_in=w_in, b_in=b_in, conv_a_w=conv_a_w, conv_a_b=conv_a_b, lru_wr=lru_wr,
               lru_br=lru_br, lru_wi=lru_wi, lru_bi=lru_bi, lru_lam=lru_lam, conv_b_w=conv_b_w, w_pa=w_pa,
               w_pb=w_pb, w_o=w_o, norm2_g=norm2_g, w_mlp1=w_mlp1, w_mlp2=w_mlp2, final_g=final_g)
    mom = dict(norm1_g=m_norm1_g, w_in=m_w_in, b_in=m_b_in, conv_a_w=m_conv_a_w, conv_a_b=m_conv_a_b,
               lru_wr=m_lru_wr, lru_br=m_lru_br, lru_wi=m_lru_wi, lru_bi=m_lru_bi, lru_lam=m_lru_lam,
               conv_b_w=m_conv_b_w, w_pa=m_w_pa, w_pb=m_w_pb, w_o=m_w_o, norm2_g=m_norm2_g, w_mlp1=m_w_mlp1,
               w_mlp2=m_w_mlp2, final_g=m_final_g)
    vel = dict(norm1_g=v_norm1_g, w_in=v_w_in, b_in=v_b_in, conv_a_w=v_conv_a_w, conv_a_b=v_conv_a_b,
               lru_wr=v_lru_wr, lru_br=v_lru_br, lru_wi=v_lru_wi, lru_bi=v_lru_bi, lru_lam=v_lru_lam,
               conv_b_w=v_conv_b_w, w_pa=v_w_pa, w_pb=v_w_pb, w_o=v_w_o, norm2_g=v_norm2_g, w_mlp1=v_w_mlp1,
               w_mlp2=v_w_mlp2, final_g=v_final_g)
    depth, D = norm1_g.shape
    nb, bw = lru_wr.shape[1], lru_wr.shape[3]
    chip = 2 * lax.axis_index("x") + lax.axis_index("y")

    small_parts = [conv_a_w.reshape(-1), conv_b_w.reshape(-1), lru_br.reshape(-1), lru_bi.reshape(-1)]
    small_len = sum(p.shape[0] for p in small_parts)
    small_rows = -(-small_len // 1024) * 8
    small = jnp.concatenate(small_parts + [jnp.zeros((small_rows * 128 - small_len,), F32)]).reshape(small_rows, 128)
    gathered = _gather_weights([wts[n].astype(BF16) for n in LRU], [BIG_DIM[n] for n in LRU], small)
    full = dict(zip(LRU, gathered[:-1]))
    items = [(n, l) for l in range(depth) for n in MATS]
    mat_dims = [BIG_DIM[n] - 1 for n, _ in items]
    where = jnp.stack([chip, lax.axis_index("c")]).astype(jnp.int32)
    placed = {(n, l): _place_shard(f"place_{n}_{l}", wts[n], l, BIG_DIM[n] - 1, where) for n, l in items}
    flat = gathered[-1].reshape(N_CHIPS, small_rows * 128)
    off = 0
    small_full = []
    for part, shard in zip(small_parts, (conv_a_w, conv_b_w, lru_br, lru_bi)):
        piece = flat[:, off:off + part.shape[0]].reshape((N_CHIPS,) + shard.shape)
        small_full.append(jnp.moveaxis(piece, 0, -2).reshape(shard.shape[:-1] + (N_CHIPS * shard.shape[-1],)))
        off += part.shape[0]
    caw_f, cbw_f, br_f, bi_f = small_full
    pch = (_rows_at(conv_a_b[:, None, :], R_CAB, R_ROWS) + _rows_at(br_f.reshape(depth, 1, D), R_BR, R_ROWS)
           + _rows_at(bi_f.reshape(depth, 1, D), R_BI, R_ROWS) + _rows_at(lru_lam[:, None, :], R_LAM, R_ROWS)
           + _rows_at(caw_f, R_CAW, R_ROWS) + _rows_at(cbw_f, R_CBW, R_ROWS))
    W = dict(b_in=b_in, pch=pch, wr=full["lru_wr"], wi=full["lru_wi"], g1=norm1_g, g2=norm2_g, gf=final_g,
             core=lax.axis_index("c").astype(jnp.int32).reshape(1), where=where,
             shard_shapes={n: wts[n].shape for n in MATS})

    loss_local, dx, grads, dgf, sums, landed, acc = _local_fwd_bwd(x[0], loss_target[0], W, placed)
    loss = lax.psum(loss_local, ("x", "y", "c"))

    key = dict(w_in="w_in", w_pa="w_pa", w_pb="w_pb", w_o="w_o", w_mlp1="w1", w_mlp2="w2", lru_wr="wr", lru_wi="wi")
    out_g, out_d, out_m, out_v = {}, {}, {}, {}
    per_layer = [[grads[l][key[n]].astype(BF16) for l in range(depth)] for n in LRU]
    land = _scatter_grads(per_layer, [BIG_DIM[n] - 1 for n in LRU])
    chip_sums = [_sum_slots(f"sum_slots_{n}", ld.reshape(N_CHIPS, -1, ld.shape[-1])) for n, ld in zip(LRU, land)]
    sib_sums = _swap_with_sibling(chip_sums)
    for n, mine, sib in zip(LRU, chip_sums, sib_sums):
        out_g[n], out_d[n], out_m[n], out_v[n] = _adamw_big(f"adamw_{n}", wts[n], mom[n], vel[n], [mine, sib])
    for n in MATS:
        acc[n] = _reduce_into(f"reduce_{n}_0", sums[n, 0], landed[n, 0], acc.get(n), 0, BIG_DIM[n], wts[n].shape, where)
    joined = _run_carry("join_halves", _join_carry([acc[n] for n in MATS], [BIG_DIM[n] for n in MATS], 0))
    for n, g in zip(MATS, joined):
        out_g[n], out_d[n], out_m[n], out_v[n] = _adamw_big(f"adamw_{n}", wts[n], mom[n], vel[n], [g])

    rows = []
    for l in range(depth):
        g = grads[l]
        rows.append(_rows_at(g["sm"], 0, SMALL_ROWS) + _rows_at(g["b_in"].reshape(7, D), S_BIN, SMALL_ROWS)
                    + _rows_at(g["g1"], S_G1, SMALL_ROWS) + _rows_at(g["g2"], S_G2, SMALL_ROWS))
    rows.append(_rows_at(dgf, 0, 8))
    tot = _allreduce_small(jnp.concatenate(rows, axis=0))
    per = tot[:depth * SMALL_ROWS].reshape(depth, SMALL_ROWS, D)

    def cols_of_chip(a, axis):
        size = a.shape[axis] // N_CHIPS
        return lax.dynamic_slice_in_dim(a, chip * size, size, axis=axis)

    small_g = dict(
        norm1_g=per[:, S_G1], b_in=per[:, S_BIN:S_BIN + 7].reshape(depth, 7 * D),
        conv_a_w=cols_of_chip(per[:, R_CAW:R_CAW + 4], 2), conv_a_b=per[:, R_CAB],
        lru_br=cols_of_chip(per[:, R_BR].reshape(depth, nb, bw), 2),
        lru_bi=cols_of_chip(per[:, R_BI].reshape(depth, nb, bw), 2), lru_lam=per[:, R_LAM],
        conv_b_w=cols_of_chip(per[:, R_CBW:R_CBW + 3], 2), norm2_g=per[:, S_G2],
        final_g=tot[depth * SMALL_ROWS])
    for n, g in small_g.items():
        out_g[n] = g
        out_d[n], out_m[n], out_v[n] = _adamw_small(f"adamw_{n}", wts[n], g, mom[n], vel[n])

    return (loss, dx[None], *[out_g[n] for n in WEIGHTS], *[out_d[n] for n in WEIGHTS],
            *[out_m[n] for n in WEIGHTS], *[out_v[n] for n in WEIGHTS])
```

```python
import functools

import jax
import jax.numpy as jnp
from jax import lax
from jax.experimental import pallas as pl
from jax.experimental.pallas import tpu as pltpu

F32 = jnp.float32
BF16 = jnp.bfloat16
MESH = pl.DeviceIdType.MESH

EPS = 1e-6
LRU_C = 8.0
ADAM_LR = 0.001
ADAM_B1 = 0.9
ADAM_B2 = 0.999
ADAM_EPS = 1e-08
ADAM_WD = 0.01
ADAM_STEP = 10

N_CHIPS = 4
N_DEV = 8
HALO = 8
VMEM_LIMIT = 56 * 1024 * 1024
MM_TILES = (1024, 1024, 2048)
MM_TILES_FUSED = (512, 1024, 2048)
MM_TILES_LONG_K = (1024, 1024, 4096)
SEQ_CHUNK = 512
MIXER_ROW_BLOCK = 32
ROW_TILE = 256

R_CAB, R_BR, R_BI, R_LAM, R_CAW, R_CBW, R_ROWS = 0, 1, 2, 3, 4, 8, 16


def _cparams(sem):
    return pltpu.CompilerParams(dimension_semantics=sem, vmem_limit_bytes=VMEM_LIMIT)


def _div_tile(n, pref, unit=128):
    if n <= pref:
        return n
    t = (pref // unit) * unit
    while n % t:
        t -= unit
    return t


def _sigmoid(v):
    return 0.5 * jnp.tanh(0.5 * v) + 0.5


def _gelu_and_grad(y):
    k = 0.7978845608028654
    c = 0.044715
    y2 = y * y
    t = jnp.tanh(k * (y + c * y2 * y))
    g = 0.5 * y * (1.0 + t)
    gp = 0.5 * (1.0 + t) + 0.5 * y * (1.0 - t * t) * (k * (1.0 + 3.0 * c * y2))
    return g, gp


def _softplus_neg(lam):
    e = jnp.exp(-jnp.abs(lam))
    w = 1.0 + e
    l1p = jnp.where(w == 1.0, e, jnp.log(w) * e / jnp.where(w == 1.0, 1.0, w - 1.0))
    return jnp.maximum(-lam, 0.0) + l1p


def _mm(name, mode, a, b, M, N, K, out_dtypes, epilogue=None, extras=(), la=None, lb=None, tiles=None,
        carry=None):
    tiles = MM_TILES if tiles is None else tiles
    tm, tn, tk = _div_tile(M, tiles[0]), _div_tile(N, tiles[1]), _div_tile(K, tiles[2])
    assert M % tm == 0 and N % tn == 0 and K % tk == 0, (name, M, N, K)
    nk = K // tk

    def spec(lead, shape, imap):
        if lead is None:
            return pl.BlockSpec(shape, imap)
        return pl.BlockSpec((None,) + shape, lambda i, j, k: (lead,) + imap(i, j, k))

    if mode == "nn":
        a_spec = spec(la, (tm, tk), lambda i, j, k: (i, k))
        b_spec = spec(lb, (tk, tn), lambda i, j, k: (k, j))
        dn = (((1,), (0,)), ((), ()))
    elif mode == "nt":
        a_spec = spec(la, (tm, tk), lambda i, j, k: (i, k))
        b_spec = spec(lb, (tn, tk), lambda i, j, k: (j, k))
        dn = (((1,), (1,)), ((), ()))
    else:
        a_spec = spec(la, (tk, tm), lambda i, j, k: (k, i))
        b_spec = spec(lb, (tk, tn), lambda i, j, k: (k, j))
        dn = (((0,), (0,)), ((), ()))

    ex_arrays, ex_specs = [], []
    for arr, kind, off in extras:
        ex_arrays.append(arr)
        if kind == "bias":
            ex_specs.append(pl.BlockSpec((1, tn), lambda i, j, k: (0, j)))
        else:
            assert off % tn == 0
            ex_specs.append(pl.BlockSpec((tm, tn), lambda i, j, k, o=off // tn: (i, j + o)))
    n_ex, n_out = len(ex_arrays), len(out_dtypes)
    n_cin = len(carry.ins) if carry else 0
    n_cout = len(carry.out_shapes) if carry else 0
    n_in = 2 + n_ex + n_cin
    gi, gj = M // tm, N // tn

    def body(*refs):
        a_ref, b_ref = refs[0], refs[1]
        ex = refs[2:2 + n_ex]
        outs = refs[n_in:n_in + n_out]
        acc = refs[n_in + n_out + n_cout]
        i, j, k = pl.program_id(0), pl.program_id(1), pl.program_id(2)
        if carry:
            c_in, c_out = refs[2 + n_ex:n_in], refs[n_in + n_out:n_in + n_out + n_cout]
            sems = refs[n_in + n_out + n_cout + 1:]
            step = (i * gj + j) * nk + k
            carry.ride(step, gi * gj * nk, c_in, c_out, sems, True)

        def product():
            return lax.dot_general(a_ref[...], b_ref[...], dn, preferred_element_type=F32)

        def finish(r):
            vals = (r,) if epilogue is None else epilogue(r, *[e[...] for e in ex])
            for o, v in zip(outs, vals):
                o[...] = v.astype(o.dtype)

        if nk == 1:
            finish(product())
        else:
            @pl.when(k == 0)
            def _():
                acc[...] = product()

            @pl.when((k > 0) & (k < nk - 1))
            def _():
                acc[...] += product()

            @pl.when(k == nk - 1)
            def _():
                finish(acc[...] + product())

        if carry:
            carry.ride(step, gi * gj * nk, c_in, c_out, sems, False)

    res = pl.pallas_call(
        body,
        grid=(gi, gj, nk),
        in_specs=[a_spec, b_spec, *ex_specs] + [ANY] * n_cin,
        out_specs=[pl.BlockSpec((tm, tn), lambda i, j, k: (i, j)) for _ in range(n_out)] + [ANY] * n_cout,
        out_shape=[jax.ShapeDtypeStruct((M, N), d) for d in out_dtypes] + (carry.out_shapes if carry else []),
        scratch_shapes=[pltpu.VMEM((tm, tn) if nk > 1 else (8, 128), F32)] + (carry.scratch() if carry else []),
        input_output_aliases={2 + n_ex + ci: n_out + co for ci, co in carry.aliases.items()} if carry else {},
        compiler_params=_cparams(("arbitrary",) * 3 if carry else ("parallel", "parallel", "arbitrary")),
        name=name,
    )(a, b, *ex_arrays, *(carry.ins if carry else []))
    main = res[0] if n_out == 1 else res[:n_out]
    return (main, res[n_out:]) if carry else main


def _rms_fwd(name, x, g_row):
    T, D = x.shape
    tm = min(ROW_TILE, T)

    def body(x_ref, g_ref, h_ref):
        xv = x_ref[...]
        r = lax.rsqrt(jnp.mean(xv * xv, axis=-1, keepdims=True) + EPS)
        h_ref[...] = (xv * r * g_ref[...]).astype(BF16)

    return pl.pallas_call(
        body,
        grid=(T // tm,),
        in_specs=[pl.BlockSpec((tm, D), lambda i: (i, 0)), pl.BlockSpec((1, D), lambda i: (0, 0))],
        out_specs=pl.BlockSpec((tm, D), lambda i: (i, 0)),
        out_shape=jax.ShapeDtypeStruct((T, D), BF16),
        compiler_params=_cparams(("parallel",)),
        name=name,
    )(x, g_row)


def _rms_bwd(name, x, g_row, dh, dres):
    T, D = x.shape
    tm = min(ROW_TILE, T)

    def body(x_ref, g_ref, dh_ref, dres_ref, dx_ref, dxb_ref, dg_ref):
        xv, dhv = x_ref[...], dh_ref[...]
        r = lax.rsqrt(jnp.mean(xv * xv, axis=-1, keepdims=True) + EPS)
        gd = g_ref[...] * dhv
        c = jnp.mean(xv * gd, axis=-1, keepdims=True)
        dx = r * gd - xv * (r * r * r) * c + dres_ref[...]
        dx_ref[...] = dx
        dxb_ref[...] = dx.astype(BF16)

        @pl.when(pl.program_id(0) == 0)
        def _():
            dg_ref[...] = jnp.zeros_like(dg_ref)

        dg_ref[...] += jnp.sum(dhv * xv * r, axis=0, keepdims=True)

    row = pl.BlockSpec((tm, D), lambda i: (i, 0))
    vec = pl.BlockSpec((1, D), lambda i: (0, 0))
    return pl.pallas_call(
        body,
        grid=(T // tm,),
        in_specs=[row, vec, row, row],
        out_specs=[row, row, vec],
        out_shape=[jax.ShapeDtypeStruct((T, D), F32), jax.ShapeDtypeStruct((T, D), BF16),
                   jax.ShapeDtypeStruct((1, D), F32)],
        compiler_params=_cparams(("arbitrary",)),
        name=name,
    )(x, g_row, dh, dres)


def _loss_head(name, x, g_row, tgt):
    T, D = x.shape
    tm = min(ROW_TILE, T)

    def body(x_ref, g_ref, t_ref, dx_ref, dxb_ref, dg_ref, loss_ref):
        xv, g = x_ref[...], g_ref[...]
        r = lax.rsqrt(jnp.mean(xv * xv, axis=-1, keepdims=True) + EPS)
        xh = xv * r
        e = xh * g - t_ref[...]
        lpart = 0.5 * jnp.sum(jnp.mean(e * e, axis=-1, keepdims=True))
        dy = e * (1.0 / D)
        gd = g * dy
        c = jnp.mean(xv * gd, axis=-1, keepdims=True)
        dx = r * gd - xv * (r * r * r) * c
        dx_ref[...] = dx
        dxb_ref[...] = dx.astype(BF16)

        @pl.when(pl.program_id(0) == 0)
        def _():
            dg_ref[...] = jnp.zeros_like(dg_ref)
            loss_ref[...] = jnp.zeros_like(loss_ref)

        dg_ref[...] += jnp.sum(dy * xh, axis=0, keepdims=True)
        loss_ref[...] += jnp.full(loss_ref.shape, lpart, F32)

    row = pl.BlockSpec((tm, D), lambda i: (i, 0))
    vec = pl.BlockSpec((1, D), lambda i: (0, 0))
    return pl.pallas_call(
        body,
        grid=(T // tm,),
        in_specs=[row, vec, row],
        out_specs=[row, row, vec, pl.BlockSpec((8, 128), lambda i: (0, 0))],
        out_shape=[jax.ShapeDtypeStruct((T, D), F32), jax.ShapeDtypeStruct((T, D), BF16),
                   jax.ShapeDtypeStruct((1, D), F32), jax.ShapeDtypeStruct((8, 128), F32)],
        compiler_params=_cparams(("arbitrary",)),
        name=name,
    )(x, g_row, tgt)


def _colsum(name, a, carry=None):
    T, N = a.shape
    tm, tn = min(512, T), _div_tile(N, 2048)
    gj, gi = N // tn, T // tm
    n_cin = len(carry.ins) if carry else 0
    n_cout = len(carry.out_shapes) if carry else 0

    def body(*refs):
        a_ref, o_ref = refs[0], refs[1 + n_cin]
        j, i = pl.program_id(0), pl.program_id(1)
        if carry:
            c_in, c_out, sems = refs[1:1 + n_cin], refs[2 + n_cin:2 + n_cin + n_cout], refs[2 + n_cin + n_cout:]
            carry.ride(j * gi + i, gj * gi, c_in, c_out, sems, True)

        @pl.when(i == 0)
        def _():
            o_ref[...] = jnp.zeros_like(o_ref)

        o_ref[...] += jnp.sum(a_ref[...].astype(F32), axis=0, keepdims=True)

        if carry:
            carry.ride(j * gi + i, gj * gi, c_in, c_out, sems, False)

    res = pl.pallas_call(
        body,
        grid=(gj, gi),
        in_specs=[pl.BlockSpec((tm, tn), lambda j, i: (i, j))] + [ANY] * n_cin,
        out_specs=[pl.BlockSpec((1, tn), lambda j, i: (0, j))] + [ANY] * n_cout,
        out_shape=[jax.ShapeDtypeStruct((1, N), F32)] + (carry.out_shapes if carry else []),
        scratch_shapes=carry.scratch() if carry else [],
        compiler_params=_cparams(("arbitrary", "arbitrary") if carry else ("parallel", "arbitrary")),
        name=name,
    )(a, *(carry.ins if carry else []))
    return (res[0], res[1:]) if carry else res[0]


def _tile_scan(a, b, row, reverse):
    for s in (1, 2, 4):
        if reverse:
            a_s, b_s, m = pltpu.roll(a, 8 - s, 0), pltpu.roll(b, 8 - s, 0), row < 8 - s
        else:
            a_s, b_s, m = pltpu.roll(a, s, 0), pltpu.roll(b, s, 0), row >= s
        b = jnp.where(m, a * b_s + b, b)
        a = jnp.where(m, a * a_s, a)
    return a, b


def _chunk_scan(a_s, b_s, out_ref, carry, n_tiles, width, reverse):
    row = lax.broadcasted_iota(jnp.int32, (8, width), 0)
    edge = 0 if reverse else 7

    group = 4 if n_tiles % 4 == 0 else 1

    def step(j, c):
        jj = (n_tiles // group - 1 - j) if reverse else j
        base = pl.multiple_of(jj * (8 * group), 8 * group)
        order = range(group - 1, -1, -1) if reverse else range(group)
        parts = {t: _tile_scan(a_s[pl.ds(base + 8 * t, 8), :], b_s[pl.ds(base + 8 * t, 8), :], row, reverse)
                 for t in order}
        for t in order:
            h = parts[t][0] * c + parts[t][1]
            out_ref[pl.ds(base + 8 * t, 8), :] = h
            c = jnp.broadcast_to(h[edge:edge + 1, :], (8, width))
        return c

    carry[...] = lax.fori_loop(0, n_tiles // group, step, carry[...])


def _mixer_specs(Tc, bw, nb, layer):
    def seg(s):
        return pl.BlockSpec((Tc, bw), lambda n, i: (i, s * nb + n))

    p_spec = pl.BlockSpec((None, R_ROWS, bw), lambda n, i: (layer, 0, n))
    w_spec = pl.BlockSpec((None, None, bw, bw), lambda n, i: (layer, n, 0, 0))
    return seg, p_spec, w_spec


def _mixer_fwd(name, layer, z, pch, wr, wi, comm=None):
    T, D = z.shape[0], z.shape[1] // 7
    bw, nb = wr.shape[-1], wr.shape[1]
    Tc = min(SEQ_CHUNK, T)
    nT = T // Tc
    n_cin = len(comm.ins) if comm else 0
    n_cout = len(comm.out_shapes) if comm else 0

    def body(*refs):
        xa_ref, ya_ref, cb_ref, cc_ref, cx_ref, p_ref, wr_ref, wi_ref = refs[:8]
        pa_ref, pb_ref, xc_ref, hl_ref, vb_ref = refs[8 + n_cin:13 + n_cin]
        xa_buf, u_buf, a_s, b_s, carry = refs[13 + n_cin + n_cout:18 + n_cin + n_cout]
        if comm:
            c_in, c_out, sems = refs[8:8 + n_cin], refs[13 + n_cin:13 + n_cin + n_cout], refs[18 + n_cin + n_cout:]
            step = pl.program_id(0) * nT + pl.program_id(1)
            comm.ride(step, nb * nT, c_in, c_out, sems, True)

        @pl.when(pl.program_id(1) == 0)
        def _():
            xa_buf[0:HALO, :] = jnp.zeros((HALO, bw), F32)
            u_buf[0:HALO, :] = jnp.zeros((HALO, bw), F32)
            carry[...] = jnp.zeros_like(carry)

        rb = min(MIXER_ROW_BLOCK, Tc)
        blocks = [(g * rb, slice(g * rb, (g + 1) * rb)) for g in range(Tc // rb)]

        def prow(k):
            return p_ref[k:k + 1, :]

        xa_buf[HALO:HALO + Tc, :] = xa_ref[...]
        u_buf[HALO:HALO + Tc, :] = cc_ref[...] * cx_ref[...]
        for r0, rs in blocks:
            xc = prow(R_CAB)
            for k in range(4):
                xc = xc + prow(R_CAW + k) * xa_buf[HALO - 3 + k + r0:HALO - 3 + k + r0 + rb, :]
            xc_ref[rs, :] = xc
        xcb = xc_ref[...].astype(BF16)
        a_s[...] = jnp.dot(xcb, wr_ref[...], preferred_element_type=F32)
        b_s[...] = jnp.dot(xcb, wi_ref[...], preferred_element_type=F32)
        sp = _softplus_neg(prow(R_LAM))
        for _, rs in blocks:
            r = _sigmoid(a_s[rs, :] + prow(R_BR))
            ig = _sigmoid(b_s[rs, :] + prow(R_BI))
            log_a = (-LRU_C) * r * sp
            t = jnp.tanh(log_a)
            a_s[rs, :] = jnp.exp(log_a)
            b_s[rs, :] = jnp.sqrt(-2.0 * t / (1.0 - t)) * (ig * xc_ref[rs, :])
        _chunk_scan(a_s, b_s, hl_ref, carry, Tc // 8, bw, False)
        for r0, rs in blocks:
            g, _ = _gelu_and_grad(ya_ref[rs, :])
            pa_ref[rs, :] = (hl_ref[rs, :] * g).astype(BF16)
            vb = jnp.zeros((rb, bw), F32)
            for k in range(3):
                vb = vb + prow(R_CBW + k) * u_buf[HALO - 2 + k + r0:HALO - 2 + k + r0 + rb, :]
            vb_ref[rs, :] = vb
            pb_ref[rs, :] = (cb_ref[rs, :] * vb).astype(BF16)
        xa_buf[0:HALO, :] = xa_buf[Tc:Tc + HALO, :]
        u_buf[0:HALO, :] = u_buf[Tc:Tc + HALO, :]

        if comm:
            comm.ride(step, nb * nT, c_in, c_out, sems, False)

    seg, p_spec, w_spec = _mixer_specs(Tc, bw, nb, layer)
    out = pl.BlockSpec((Tc, bw), lambda n, i: (i, n))
    res = pl.pallas_call(
        body,
        grid=(nb, nT),
        in_specs=[seg(0), seg(1), seg(2), seg(3), seg(4), p_spec, w_spec, w_spec] + [ANY] * n_cin,
        out_specs=[out] * 5 + [ANY] * n_cout,
        out_shape=[jax.ShapeDtypeStruct((T, D), BF16), jax.ShapeDtypeStruct((T, D), BF16),
                   jax.ShapeDtypeStruct((T, D), F32), jax.ShapeDtypeStruct((T, D), F32),
                   jax.ShapeDtypeStruct((T, D), F32)] + (comm.out_shapes if comm else []),
        scratch_shapes=[pltpu.VMEM((Tc + HALO, bw), F32), pltpu.VMEM((Tc + HALO, bw), F32),
                        pltpu.VMEM((Tc, bw), F32), pltpu.VMEM((Tc, bw), F32), pltpu.VMEM((8, bw), F32)]
        + (comm.scratch() if comm else []),
        input_output_aliases={8 + ci: 5 + co for ci, co in comm.aliases.items()} if comm else {},
        compiler_params=_cparams(("arbitrary", "arbitrary") if comm else ("parallel", "arbitrary")),
        name=name,
    )(z, z, z, z, z, pch, wr, wi, *(comm.ins if comm else []))
    return (res[:5], res[5:]) if comm else res


def _mixer_bwd(name, layer, z, xc, hl, vb, dpa, dpb, dga, dgb, pch, wr, wi, comm=None):
    T, D = z.shape[0], z.shape[1] // 7
    bw, nb = wr.shape[-1], wr.shape[1]
    Tc = min(SEQ_CHUNK, T)
    nT = T // Tc
    tpc = Tc // 8
    rb = min(MIXER_ROW_BLOCK, Tc)
    n_cin = len(comm.ins) if comm else 0
    n_cout = len(comm.out_shapes) if comm else 0
    total = nb * nT

    def body(*refs):
        (xa_ref, ya_ref, cb_ref, cc_ref, cx_ref, xc_ref, hl_ref, hp_ref, vb_ref, dpa_ref, dpb_ref, dga_ref, dgb_ref,
         p_ref, wr_ref, wi_ref) = refs[:16]
        dz_ref, dwr_ref, dwi_ref, sm_ref = refs[16 + n_cin:20 + n_cin]
        (h_buf, a_buf, dxc_buf, dvb_buf, a_s, d_s, lam_s, r_s, i_s, m_s, dpr_s, dpi_s, sm8,
         carry, stage, out_sems) = refs[20 + n_cin + n_cout:36 + n_cin + n_cout]
        i = pl.program_id(1)
        step = pl.program_id(0) * nT + i
        if comm:
            c_in, c_out = refs[16:16 + n_cin], refs[20 + n_cin:20 + n_cin + n_cout]
            sems = refs[36 + n_cin + n_cout:]
            comm.ride(step, total, c_in, c_out, sems, True)

        slot = step % 2

        def out_copies(sl):
            rows = pl.ds(pl.multiple_of((nT - 1 - i) * Tc, Tc), Tc)
            return [pltpu.make_async_copy(
                stage.at[sl, s], dz_ref.at[rows, pl.ds(pl.multiple_of((s * nb + pl.program_id(0)) * bw, bw), bw)],
                out_sems.at[sl, s]) for s in range(7)]

        @pl.when(step >= 2)
        def _():
            for cp in out_copies(slot):
                cp.wait()

        dxa_ref, dya_ref, dcb_ref, dcc_ref, dcx_ref = [stage.at[slot, s] for s in range(5)]
        stage[slot, 5, :, :] = dga_ref[...]
        stage[slot, 6, :, :] = dgb_ref[...]

        @pl.when(i == 0)
        def _():
            a_buf[Tc:Tc + HALO, :] = jnp.zeros((HALO, bw), F32)
            dxc_buf[Tc:Tc + HALO, :] = jnp.zeros((HALO, bw), F32)
            dvb_buf[Tc:Tc + HALO, :] = jnp.zeros((HALO, bw), F32)
            carry[...] = jnp.zeros_like(carry)
            dwr_ref[...] = jnp.zeros_like(dwr_ref)
            dwi_ref[...] = jnp.zeros_like(dwi_ref)
            sm8[...] = jnp.zeros_like(sm8)

        blocks = [(g * rb, slice(g * rb, (g + 1) * rb)) for g in range(Tc // rb)]

        def prow(k):
            return p_ref[k:k + 1, :]

        def part8(v):
            return jnp.sum(v.reshape(rb // 8, 8, bw), axis=0)

        sums8 = {}

        def tally(k, v):
            sums8[k] = sums8[k] + part8(v) if k in sums8 else part8(v)

        sp = _softplus_neg(prow(R_LAM))
        xcb = xc_ref[...].astype(BF16)
        r_s[...] = jnp.dot(xcb, wr_ref[...], preferred_element_type=F32)
        i_s[...] = jnp.dot(xcb, wi_ref[...], preferred_element_type=F32)
        for _, rs in blocks:
            r = _sigmoid(r_s[rs, :] + prow(R_BR))
            ig = _sigmoid(i_s[rs, :] + prow(R_BI))
            log_a = (-LRU_C) * r * sp
            t = jnp.tanh(log_a)
            r_s[rs, :] = r
            i_s[rs, :] = ig
            m_s[rs, :] = jnp.sqrt(-2.0 * t / (1.0 - t))
            a_buf[rs, :] = jnp.exp(log_a)
            g, gp = _gelu_and_grad(ya_ref[rs, :])
            dpav = dpa_ref[rs, :]
            d_s[rs, :] = dpav * g
            dya_ref[rs, :] = (dpav * hl_ref[rs, :] * gp).astype(BF16)
            dpbv = dpb_ref[rs, :]
            dcb_ref[rs, :] = (dpbv * vb_ref[rs, :]).astype(BF16)
            dvb_buf[rs, :] = dpbv * cb_ref[rs, :]

        a_s[...] = a_buf[1:Tc + 1, :]
        _chunk_scan(a_s, d_s, lam_s, carry, tpc, bw, True)
        h_buf[HALO:HALO + Tc, :] = hl_ref[...]
        h_buf[0:HALO, :] = jnp.where(i == nT - 1, 0.0, hp_ref[...])

        for r0, rs in blocks:
            lamv, xcv, r, ig, a = lam_s[rs, :], xc_ref[rs, :], r_s[rs, :], i_s[rs, :], a_buf[rs, :]
            mult = m_s[rs, :]
            da = lamv * h_buf[HALO - 1 + r0:HALO - 1 + r0 + rb, :]
            dmult = lamv * (ig * xcv)
            dbx = lamv * mult
            dig = dbx * xcv
            dxc_buf[rs, :] = dbx * ig
            dlog_a = da * a - dmult * (a * a) / mult
            dpr = (dlog_a * ((-LRU_C) * sp)) * r * (1.0 - r)
            dpi = dig * ig * (1.0 - ig)
            dpr_s[rs, :] = dpr.astype(BF16)
            dpi_s[rs, :] = dpi.astype(BF16)
            tally(R_BR, dpr)
            tally(R_BI, dpi)
            tally(R_LAM, dlog_a * ((-LRU_C) * r))

        dprb, dpib = dpr_s[...], dpi_s[...]
        nt = (((1,), (1,)), ((), ()))
        tn = (((0,), (0,)), ((), ()))
        dxc_buf[0:Tc, :] += (lax.dot_general(dprb, wr_ref[...], nt, preferred_element_type=F32)
                             + lax.dot_general(dpib, wi_ref[...], nt, preferred_element_type=F32))
        dwr_ref[...] += lax.dot_general(xcb, dprb, tn, preferred_element_type=F32)
        dwi_ref[...] += lax.dot_general(xcb, dpib, tn, preferred_element_type=F32)

        for r0, rs in blocks:
            xav = xa_ref[rs, :]
            tally(R_CAB, dxc_buf[rs, :])
            dxa = jnp.zeros((rb, bw), F32)
            for k in range(4):
                sh = dxc_buf[3 - k + r0:3 - k + r0 + rb, :]
                dxa = dxa + prow(R_CAW + k) * sh
                tally(R_CAW + k, xav * sh)
            dxa_ref[rs, :] = dxa.astype(BF16)
            ccv, cxv = cc_ref[rs, :], cx_ref[rs, :]
            u = ccv * cxv
            du = jnp.zeros((rb, bw), F32)
            for k in range(3):
                sh = dvb_buf[2 - k + r0:2 - k + r0 + rb, :]
                du = du + prow(R_CBW + k) * sh
                tally(R_CBW + k, u * sh)
            dcc_ref[rs, :] = (du * cxv).astype(BF16)
            dcx_ref[rs, :] = (du * ccv).astype(BF16)

        for k, v in sums8.items():
            sm8[8 * k:8 * k + 8, :] += v
        a_buf[Tc:Tc + HALO, :] = a_buf[0:HALO, :]
        dxc_buf[Tc:Tc + HALO, :] = dxc_buf[0:HALO, :]
        dvb_buf[Tc:Tc + HALO, :] = dvb_buf[0:HALO, :]

        @pl.when(i == nT - 1)
        def _():
            sm_ref[...] = jnp.sum(sm8[...].reshape(R_ROWS, 8, bw), axis=1)
            sm_ref[R_LAM:R_LAM + 1, :] = sm_ref[R_LAM:R_LAM + 1, :] * (-_sigmoid(-prow(R_LAM)))

        for cp in out_copies(slot):
            cp.start()

        @pl.when(step == total - 1)
        def _():
            if total > 1:
                for cp in out_copies(1 - slot):
                    cp.wait()
            for cp in out_copies(slot):
                cp.wait()

        if comm:
            comm.ride(step, total, c_in, c_out, sems, False)

    def seg(s):
        return pl.BlockSpec((Tc, bw), lambda n, i: (nT - 1 - i, s * nb + n))

    blk = pl.BlockSpec((Tc, bw), lambda n, i: (nT - 1 - i, n))
    halo = pl.BlockSpec((8, bw), lambda n, i: (jnp.maximum((nT - 1 - i) * tpc - 1, 0), n))
    p_spec = pl.BlockSpec((None, R_ROWS, bw), lambda n, i: (layer, 0, n))
    w_spec = pl.BlockSpec((None, None, bw, bw), lambda n, i: (layer, n, 0, 0))
    dw_spec = pl.BlockSpec((None, bw, bw), lambda n, i: (n, 0, 0))
    res = pl.pallas_call(
        body,
        grid=(nb, nT),
        in_specs=[seg(0), seg(1), seg(2), seg(3), seg(4), blk, blk, halo, blk, blk, blk, blk, blk,
                  p_spec, w_spec, w_spec] + [ANY] * n_cin,
        out_specs=[ANY, dw_spec, dw_spec, pl.BlockSpec((R_ROWS, bw), lambda n, i: (0, n))] + [ANY] * n_cout,
        out_shape=[jax.ShapeDtypeStruct((T, 7 * D), BF16), jax.ShapeDtypeStruct((nb, bw, bw), F32),
                   jax.ShapeDtypeStruct((nb, bw, bw), F32), jax.ShapeDtypeStruct((R_ROWS, D), F32)]
        + (comm.out_shapes if comm else []),
        scratch_shapes=[pltpu.VMEM((Tc + HALO, bw), F32)] * 4 + [pltpu.VMEM((Tc, bw), F32)] * 6
        + [pltpu.VMEM((Tc, bw), BF16)] * 2 + [pltpu.VMEM((8 * R_ROWS, bw), F32), pltpu.VMEM((8, bw), F32),
                                              pltpu.VMEM((2, 7, Tc, bw), BF16), pltpu.SemaphoreType.DMA((2, 7))]
        + (comm.scratch() if comm else []),
        input_output_aliases={16 + ci: 4 + co for ci, co in comm.aliases.items()} if comm else {},
        compiler_params=_cparams(("arbitrary", "arbitrary")),
        name=name,
    )(z, z, z, z, z, xc, hl, hl, vb, dpa, dpb, dga, dgb, pch, wr, wi, *(comm.ins if comm else []))
    return (res[:4], res[4:]) if comm else res


def _local_fwd_bwd(x, tgt, W, placed=None):
    T, D = x.shape
    g1, g2 = W["g1"], W["g2"]
    depth = g1.shape[0]
    FF = 4 * D
    if placed is None:
        mats = {(n, l): W[n][l] for n in MATS for l in range(depth)}
    else:
        mats = {}
        mats["w_in", 0], = _run_carry("gather_first", _gather_carry([placed["w_in", 0]], [(0, 1, 0, 1)]))

    def gathering(specs):
        if placed is None or not specs:
            return None, []
        keys = [(n, l) for n, l, _, _, _ in specs]
        arrays = [mats.get(k, placed[k]) for k in keys]
        return _gather_carry(arrays, [(i, d, part, nparts) for i, (_, _, d, part, nparts) in enumerate(specs)]), keys

    def hosted(call, specs, **kw):
        carry, keys = gathering(specs)
        if carry is None:
            return call(**kw)
        res, got = call(**kw, **{("comm" if call.func is _mixer_fwd else "carry"): carry})
        mats.update(zip(keys, got))
        return res

    saved = []
    xs = x
    for l in range(depth):
        h = _rms_fwd(f"rms1_fwd_{l}", xs, g1[l][None])
        nxt = l + 1 < depth
        projs = [("w_pa", l, 0, 0, 1), ("w_pb", l, 0, 0, 1), ("w_o", l, 0, 0, 1)]
        z = hosted(functools.partial(_mm, f"in_proj_{l}", "nn", h, mats["w_in", l], T, 7 * D, D, [F32]),
                   projs + ([("w_mlp1", l, 1, 0, 1)] if nxt else []),
                   epilogue=lambda acc, b: (acc + b,), extras=[(W["b_in"][l][None], "bias", 0)])
        pa, pb, xc, hl, vb = hosted(
            functools.partial(_mixer_fwd, f"mixer_fwd_{l}", l, z, W["pch"], W["wr"], W["wi"]),
            [("w_mlp2", l, 0, 0, 1)] if nxt else [("w_mlp1", l, 1, 0, 1)])
        oa = _mm(f"proj_a_{l}", "nn", pa, mats["w_pa", l], T, D, D, [F32])

        def merge(acc, oav, ga, gb):
            return acc, _sigmoid(ga) * oav + _sigmoid(gb) * acc

        ob, mg = _mm(f"proj_b_merge_{l}", "nn", pb, mats["w_pb", l], T, D, D, [F32, BF16], epilogue=merge,
                     tiles=MM_TILES_FUSED,
                     extras=[(oa, "tile", 0), (z, "tile", 5 * D), (z, "tile", 6 * D)])
        x1 = _mm(f"out_proj_{l}", "nn", mg, mats["w_o", l], T, D, D, [F32],
                 epilogue=lambda acc, res: (res + acc,), extras=[(xs, "tile", 0)])
        h2 = _rms_fwd(f"rms2_fwd_{l}", x1, g2[l][None])

        def relu2(acc):
            pr = jnp.maximum(acc, 0.0)
            return pr * pr, pr

        u, pr = hosted(functools.partial(_mm, f"mlp1_{l}", "nn", h2, mats["w_mlp1", l], T, FF, D, [BF16, BF16]),
                       [("w_in", l + 1, 1, 0, 2)] if nxt else [("w_mlp2", l, 0, 0, 1)], epilogue=relu2)
        x2 = hosted(functools.partial(_mm, f"mlp2_{l}", "nn", u, mats["w_mlp2", l], T, D, FF, [F32]),
                    [("w_in", l + 1, 1, 1, 2)] if nxt else [],
                    epilogue=lambda acc, res: (res + acc,), extras=[(x1, "tile", 0)])
        saved.append(dict(x0=xs, h=h, z=z, pa=pa, pb=pb, xc=xc, hl=hl, vb=vb, oa=oa, ob=ob, mg=mg, x1=x1,
                          h2=h2, u=u, pr=pr))
        xs = x2

    dx, dxb, dgf, loss_blk = _loss_head("loss_head", xs, W["gf"][None], tgt)

    gmat, got, sums, landed, acc = {}, {}, {}, {}, {}

    def reducing(call, swaps=(), scatters=(), part=None, join=None, mixer=False, **kw):
        swaps, scatters = [(n, l) for n in swaps], [(n, l) for n in scatters]
        carries, sinks = [], []
        if placed is not None and swaps:
            carries.append(_swap_carry([gmat[k] for k in swaps], [BIG_DIM[k[0]] - 1 for k in swaps]))
            sinks.append((got, swaps))
        if placed is not None and scatters:
            carries.append(_scatter_carry([sums[k] for k in scatters]))
            sinks.append((landed, scatters))
        if placed is not None and part is not None:
            key = part[:2]
            carries.append(_scatter_carry([sums[key]], part[2], part[3], [landed[key]] if key in landed else None))
            sinks.append((landed, [key]))
        if placed is not None and join is not None:
            carries.append(_join_carry([acc[n] for n in MATS], [BIG_DIM[n] for n in MATS], join))
            sinks.append((acc, list(MATS)))
        if not carries:
            return call(**kw)
        res, moved = call(**kw, **{("comm" if mixer else "carry"): _merge_carries(carries)})
        moved = list(moved)
        for target, keys in sinks:
            for k in keys:
                target[k] = moved.pop(0)
        return res

    def add(names):
        if placed is not None:
            for n in names:
                sums[n, l] = _add_halves(f"add_halves_{n}_{l}", gmat[n, l], got[n, l], BIG_DIM[n] - 1, W["core"])

    grads = [None] * depth
    for l in reversed(range(depth)):
        s = saved[l]
        dp = _mm(f"mlp2_dx_{l}", "nt", dxb, mats["w_mlp2", l], T, FF, D, [BF16],
                 epilogue=lambda acc, prv: (2.0 * prv.astype(F32) * acc,), extras=[(s["pr"], "tile", 0)])
        dw2 = gmat["w_mlp2", l] = reducing(functools.partial(_mm, f"mlp2_dw_{l}", "tn", s["u"], dxb, FF, D, T, [BF16]),
                                           part=("w_in", l + 1, 1, 2) if l + 1 < depth else None)
        above = l + 1 if placed is not None and l + 1 < depth else None
        if above is not None:
            for n in MATS:
                acc[n] = _reduce_into(f"reduce_{n}_{above}", sums[n, above], landed[n, above], acc.get(n), above,
                                      BIG_DIM[n], W["shard_shapes"][n], W["where"])
        dh2 = reducing(functools.partial(_mm, f"mlp1_dx_{l}", "nt", dp, mats["w_mlp1", l], T, D, FF, [F32]),
                       swaps=["w_mlp2"], tiles=MM_TILES_LONG_K)
        add(["w_mlp2"])
        dw1 = gmat["w_mlp1", l] = reducing(functools.partial(_mm, f"mlp1_dw_{l}", "tn", s["h2"], dp, D, FF, T, [BF16]),
                                           part=("w_mlp2", l, 0, 2))
        dx1, dx1b, dg2 = _rms_bwd(f"rms2_bwd_{l}", s["x1"], g2[l][None], dh2, dx)

        def unmerge(acc, ga, gb, oav, obv):
            sa, sb = _sigmoid(ga), _sigmoid(gb)
            return acc * sa, acc * sb, acc * oav * sa * (1.0 - sa), acc * obv * sb * (1.0 - sb)

        doa, dob, dga, dgb = reducing(
            functools.partial(_mm, f"out_proj_dx_{l}", "nt", dx1b, mats["w_o", l], T, D, D, [BF16] * 4),
            swaps=["w_mlp1"], part=("w_mlp2", l, 1, 2), join=above, tiles=MM_TILES_FUSED, epilogue=unmerge,
            extras=[(s["z"], "tile", 5 * D), (s["z"], "tile", 6 * D), (s["oa"], "tile", 0), (s["ob"], "tile", 0)])
        add(["w_mlp1"])
        dwo = gmat["w_o", l] = _mm(f"out_proj_dw_{l}", "tn", s["mg"], dx1b, D, D, T, [BF16])
        dpa = _mm(f"proj_a_dx_{l}", "nt", doa, mats["w_pa", l], T, D, D, [F32])
        dwpa = gmat["w_pa", l] = _mm(f"proj_a_dw_{l}", "tn", s["pa"], doa, D, D, T, [BF16])
        dpb = _mm(f"proj_b_dx_{l}", "nt", dob, mats["w_pb", l], T, D, D, [F32])
        dwpb = gmat["w_pb", l] = _mm(f"proj_b_dw_{l}", "tn", s["pb"], dob, D, D, T, [BF16])
        dz, dwr, dwi, sm = reducing(
            functools.partial(_mixer_bwd, f"mixer_bwd_{l}", l, s["z"], s["xc"], s["hl"], s["vb"], dpa, dpb, dga, dgb,
                              W["pch"], W["wr"], W["wi"]),
            swaps=["w_o", "w_pa", "w_pb"], scatters=["w_mlp1"], mixer=True)
        add(["w_o", "w_pa", "w_pb"])
        dwin = gmat["w_in", l] = reducing(
            functools.partial(_mm, f"in_proj_dw_{l}", "tn", s["h"], dz, D, 7 * D, T, [BF16]),
            scatters=["w_o", "w_pa", "w_pb"])
        dbin = reducing(functools.partial(_colsum, f"bias_grad_{l}", dz), swaps=["w_in"])
        add(["w_in"])
        dh = reducing(functools.partial(_mm, f"in_proj_dx_{l}", "nt", dz, mats["w_in", l], T, D, 7 * D, [F32]),
                      scatters=["w_in"] if l == 0 else [], part=("w_in", l, 0, 2) if l > 0 else None,
                      tiles=MM_TILES_LONG_K)
        dx, dxb, dg1 = _rms_bwd(f"rms1_bwd_{l}", s["x0"], g1[l][None], dh, dx1)
        grads[l] = dict(w_in=dwin, w_pa=dwpa, w_pb=dwpb, w_o=dwo, w_mlp1=dw1, w_mlp2=dw2, wr=dwr, wi=dwi,
                        sm=sm, b_in=dbin, g1=dg1, g2=dg2)
    return loss_blk[0, 0], dx, grads, dgf, sums, landed, acc


ANY = pl.BlockSpec(memory_space=pl.ANY)


def _place():
    x, y, c = lax.axis_index("x"), lax.axis_index("y"), lax.axis_index("c")
    peers = [(1 - x, y, c), (x, 1 - y, c), (1 - x, 1 - y, c)]
    chips = [2 * (1 - x) + y, 2 * x + (1 - y), 2 * (1 - x) + (1 - y)]
    return (x, y, c), 2 * x + y, peers, chips


def _window(ref, dim, q, size):
    idx = [slice(None)] * len(ref.shape)
    idx[dim] = pl.ds(q * size, size)
    return ref.at[tuple(idx)]


def _gather_weights(shards, dims, small):
    n = len(shards)
    sizes = [s.shape[d] for s, d in zip(shards, dims)]
    full = [jax.ShapeDtypeStruct(s.shape[:d] + (s.shape[d] * N_CHIPS,) + s.shape[d + 1:], s.dtype)
            for s, d in zip(shards, dims)]
    full.append(jax.ShapeDtypeStruct((N_CHIPS,) + small.shape, small.dtype))

    def body(*refs):
        ins, outs = refs[:n + 1], refs[n + 1:2 * n + 2]
        send_sems, recv_sems, local_sems = refs[2 * n + 2:]
        _, k, peers, chips = _place()

        def dst(w, q):
            return outs[w].at[q] if w == n else _window(outs[w], dims[w], q, sizes[w])

        local = [pltpu.make_async_copy(ins[w], dst(w, k), local_sems.at[w]) for w in range(n + 1)]
        for cp in local:
            cp.start()
        sends = []
        for p, peer in enumerate(peers):
            for w in range(n + 1):
                s = p * (n + 1) + w
                sends.append(pltpu.make_async_remote_copy(
                    src_ref=ins[w], dst_ref=dst(w, k), send_sem=send_sems.at[s], recv_sem=recv_sems.at[s],
                    device_id=peer, device_id_type=MESH))
        for cp in sends:
            cp.start()
        for p, peer in enumerate(peers):
            for w in range(n + 1):
                s = p * (n + 1) + w
                pltpu.make_async_remote_copy(
                    src_ref=ins[w], dst_ref=dst(w, chips[p]), send_sem=send_sems.at[s], recv_sem=recv_sems.at[s],
                    device_id=peer, device_id_type=MESH).wait_recv()
        for cp in sends:
            cp.wait_send()
        for cp in local:
            cp.wait()

    return pl.pallas_call(
        body,
        in_specs=[ANY] * (n + 1),
        out_specs=[ANY] * (n + 1),
        out_shape=full,
        scratch_shapes=[pltpu.SemaphoreType.DMA((3 * (n + 1),)), pltpu.SemaphoreType.DMA((3 * (n + 1),)),
                        pltpu.SemaphoreType.DMA((n + 1,))],
        name="gather_weights",
    )(*shards, small)


def _scatter_grads(grads, dims):
    n, depth = len(grads), len(grads[0])
    sizes = [g[0].shape[d] // N_CHIPS for g, d in zip(grads, dims)]
    land = []
    for g, d, sz in zip(grads, dims, sizes):
        shp = g[0].shape
        land.append(jax.ShapeDtypeStruct((N_CHIPS, depth) + shp[:d] + (sz,) + shp[d + 1:], g[0].dtype))

    def body(*refs):
        ins, outs = refs[:n * depth], refs[n * depth:n * depth + n]
        send_sems, recv_sems, local_sems = refs[n * depth + n:]
        _, k, peers, chips = _place()

        def src(w, l, q):
            return _window(ins[w * depth + l], dims[w], q, sizes[w])

        local = [pltpu.make_async_copy(src(w, l, k), outs[w].at[3, l], local_sems.at[w * depth + l])
                 for w in range(n) for l in range(depth)]
        for cp in local:
            cp.start()
        sends = []
        for p, peer in enumerate(peers):
            for w in range(n):
                for l in range(depth):
                    s = (p * n + w) * depth + l
                    sends.append(pltpu.make_async_remote_copy(
                        src_ref=src(w, l, chips[p]), dst_ref=outs[w].at[p, l], send_sem=send_sems.at[s],
                        recv_sem=recv_sems.at[s], device_id=peer, device_id_type=MESH))
        for cp in sends:
            cp.start()
        for cp in sends:
            cp.wait_recv()
        for cp in sends:
            cp.wait_send()
        for cp in local:
            cp.wait()

    flat = [g for gl in grads for g in gl]
    return pl.pallas_call(
        body,
        in_specs=[ANY] * (n * depth),
        out_specs=[ANY] * n,
        out_shape=land,
        scratch_shapes=[pltpu.SemaphoreType.DMA((3 * n * depth,)), pltpu.SemaphoreType.DMA((3 * n * depth,)),
                        pltpu.SemaphoreType.DMA((n * depth,))],
        name="scatter_grads",
    )(*flat)


def _sum_slots(name, land):
    _, R, C = land.shape
    tr, tc = _div_tile(R, 512, 8), _div_tile(C, 1024)

    def body(a_ref, b_ref, c_ref, d_ref, o_ref):
        o_ref[...] = ((d_ref[...].astype(F32) + a_ref[...].astype(F32)) + b_ref[...].astype(F32)) \
            + c_ref[...].astype(F32)

    def slot(q):
        return pl.BlockSpec((None, tr, tc), lambda i, j: (q, i, j))

    return pl.pallas_call(
        body,
        grid=(R // tr, C // tc),
        in_specs=[slot(0), slot(1), slot(2), slot(3)],
        out_specs=pl.BlockSpec((tr, tc), lambda i, j: (i, j)),
        out_shape=jax.ShapeDtypeStruct((R, C), F32),
        compiler_params=_cparams(("parallel", "parallel")),
        name=name,
    )(land, land, land, land)


def _swap_with_sibling(parts):
    n = len(parts)

    def body(*refs):
        ins, outs = refs[:n], refs[n:2 * n]
        send_sems, recv_sems = refs[2 * n:]
        (x, y, c), _, _, _ = _place()
        copies = [pltpu.make_async_remote_copy(
            src_ref=ins[w], dst_ref=outs[w], send_sem=send_sems.at[w], recv_sem=recv_sems.at[w],
            device_id=(x, y, 1 - c), device_id_type=MESH) for w in range(n)]
        for cp in copies:
            cp.start()
        for cp in copies:
            cp.wait()

    return pl.pallas_call(
        body,
        in_specs=[ANY] * n,
        out_specs=[ANY] * n,
        out_shape=[jax.ShapeDtypeStruct(p.shape, p.dtype) for p in parts],
        scratch_shapes=[pltpu.SemaphoreType.DMA((n,)), pltpu.SemaphoreType.DMA((n,))],
        name="swap_with_sibling",
    )(*parts)


CARRY_MARKS = (0.6, 0.92)


class _Carry:
    def __init__(self, ins, out_shapes, rounds, counts, aliases=None, marks=CARRY_MARKS):
        self.ins, self.out_shapes, self.rounds, self.counts = list(ins), list(out_shapes), list(rounds), list(counts)
        self.aliases, self.marks = dict(aliases or {}), marks

    def scratch(self):
        return [pltpu.SemaphoreType.DMA((n,)) for n in self.counts for _ in range(2)]

    def _copies(self, r, in_refs, out_refs, sems, landing):
        remote = self.rounds[r](in_refs, out_refs)
        assert len(remote) == self.counts[r], (r, len(remote), self.counts[r])
        return [pltpu.make_async_remote_copy(src_ref=s, dst_ref=(land if landing else d), send_sem=sems[2 * r].at[i],
                                             recv_sem=sems[2 * r + 1].at[i], device_id=peer, device_id_type=MESH)
                for i, (s, d, peer, land) in enumerate(remote)]

    def begin(self, r, in_refs, out_refs, sems):
        if r > 0:
            for cp in self._copies(r - 1, in_refs, out_refs, sems, True):
                cp.wait_recv()
        for cp in self._copies(r, in_refs, out_refs, sems, False):
            cp.start()

    def end(self, in_refs, out_refs, sems):
        last = len(self.rounds) - 1
        for cp in self._copies(last, in_refs, out_refs, sems, True):
            cp.wait_recv()
        for r in range(last + 1):
            for cp in self._copies(r, in_refs, out_refs, sems, False):
                cp.wait_send()

    def ride(self, step, total, in_refs, out_refs, sems, first):
        if first:
            @pl.when(step == 0)
            def _():
                self.begin(0, in_refs, out_refs, sems)
            return
        for r in range(1, len(self.rounds)):
            @pl.when(step == min(total - 1, int(total * self.marks[r - 1])))
            def _(r=r):
                self.begin(r, in_refs, out_refs, sems)

        @pl.when(step == total - 1)
        def _():
            self.end(in_refs, out_refs, sems)


def _run_carry(name, carry):
    n_in, n_out = len(carry.ins), len(carry.out_shapes)

    def body(*refs):
        in_refs, out_refs, sems = refs[:n_in], refs[n_in:n_in + n_out], refs[n_in + n_out:]
        for r in range(len(carry.rounds)):
            carry.begin(r, in_refs, out_refs, sems)
        carry.end(in_refs, out_refs, sems)

    return pl.pallas_call(
        body,
        in_specs=[ANY] * n_in,
        out_specs=[ANY] * n_out,
        out_shape=carry.out_shapes,
        scratch_shapes=carry.scratch(),
        input_output_aliases=carry.aliases,
        name=name,
    )(*carry.ins)


def _gather_carry(arrays, items):
    shapes = [a.shape for a in arrays]

    def ring():
        x, y, c = lax.axis_index("x"), lax.axis_index("y"), lax.axis_index("c")
        first = (x + (1 - c) * (1 - 2 * x), y + c * (1 - 2 * y), c)
        second = (x + c * (1 - 2 * x), y + (1 - c) * (1 - 2 * y), c)
        return c, 2 * x + y, first, second

    def chip(pos):
        return 2 * pos[0] + pos[1]

    def round0(ins, outs):
        c, k, first, second = ring()
        remote = []
        for item in items:
            win = window(outs[item[0]], item)
            remote.append((win(2 * k + c), win(2 * k + c), first, win(2 * chip(first) + c)))
            remote.append((win(2 * k + c), win(2 * k + c), second, win(2 * chip(second) + c)))
        return remote

    def round1(ins, outs):
        c, k, first, second = ring()
        remote = []
        for item in items:
            win = window(outs[item[0]], item)
            relayed = win(2 * chip(first) + c)
            remote.append((relayed, relayed, second, win(2 * (3 - k) + c)))
        return remote

    def window(ref, item):
        idx, d, part, nparts = item
        h = shapes[idx][d] // (2 * N_CHIPS)
        rows = shapes[idx][1 - d] // nparts

        def win(j):
            sl = [None, None]
            sl[d] = pl.ds(j * h, h)
            sl[1 - d] = pl.ds(part * rows, rows)
            return ref.at[tuple(sl)]

        return win

    def round2(ins, outs):
        (x, y, c), _, _, chips = _place()
        remote = []
        for item in items:
            win = window(outs[item[0]], item)
            for p in range(3):
                remote.append((win(2 * chips[p] + c), win(2 * chips[p] + c), (x, y, 1 - c),
                               win(2 * chips[p] + 1 - c)))
        return remote

    n = len(items)
    return _Carry(arrays, [jax.ShapeDtypeStruct(a.shape, a.dtype) for a in arrays], [round0, round1, round2],
                  [2 * n, n, 3 * n], aliases={i: i for i in range(len(arrays))})


def _place_shard(name, w, layer, dim, chip):
    _, a, b = w.shape
    full = (a * N_CHIPS, b) if dim == 0 else (a, b * N_CHIPS)
    tr, tc = _div_tile(a, 512, 16), _div_tile(b, 2048)
    nr, nc = a // tr, b // tc

    def out_map(i, j, chip_ref):
        return (chip_ref[0] * nr + i, j) if dim == 0 else (i, chip_ref[0] * nc + j)

    def body(chip_ref, w_ref, o_ref):
        o_ref[...] = w_ref[...].astype(o_ref.dtype)

    return pl.pallas_call(
        body,
        grid_spec=pltpu.PrefetchScalarGridSpec(
            num_scalar_prefetch=1, grid=(nr, nc),
            in_specs=[pl.BlockSpec((None, tr, tc), lambda i, j, chip_ref: (layer, i, j))],
            out_specs=pl.BlockSpec((tr, tc), out_map)),
        out_shape=jax.ShapeDtypeStruct(full, BF16),
        compiler_params=_cparams(("parallel", "parallel")),
        name=name,
    )(chip, w)


def _half_shape(shape, dim):
    return shape[:dim] + (shape[dim] // (2 * N_CHIPS),) + shape[dim + 1:]


def _swap_carry(grads, dims):
    shapes = [jax.ShapeDtypeStruct((N_CHIPS,) + _half_shape(g.shape, d), g.dtype) for g, d in zip(grads, dims)]

    def plan(ins, outs):
        (x, y, c), _, _, _ = _place()
        remote = []
        for w, d in enumerate(dims):
            h = grads[w].shape[d] // (2 * N_CHIPS)
            for q in range(N_CHIPS):
                remote.append((_window(ins[w], d, 2 * q + 1 - c, h), outs[w].at[q], (x, y, 1 - c), outs[w].at[q]))
        return remote

    return _Carry(grads, shapes, [plan], [N_CHIPS * len(grads)])


def _scatter_carry(sums, part=0, nparts=1, land=None):
    n = len(sums)

    def plan(ins, outs):
        _, _, peers, chips = _place()
        remote = []
        for w in range(n):
            r = sums[w].shape[1] // nparts
            rows = pl.ds(part * r, r)
            for p in range(3):
                remote.append((ins[w].at[chips[p], rows], outs[w].at[p, rows], peers[p], outs[w].at[p, rows]))
        return remote

    shapes = [jax.ShapeDtypeStruct((3,) + s.shape[1:], s.dtype) for s in sums]
    if land is None:
        return _Carry(sums, shapes, [plan], [3 * n])
    return _Carry(list(sums) + list(land), shapes, [plan], [3 * n], aliases={n + w: w for w in range(n)})


def _merge_carries(carries):
    carries = [c for c in carries if c is not None]
    if len(carries) <= 1:
        return carries[0] if carries else None
    ins = [a for c in carries for a in c.ins]
    outs = [s for c in carries for s in c.out_shapes]

    def plan(in_refs, out_refs):
        remote, i0, o0 = [], 0, 0
        for c in carries:
            remote += c.rounds[0](in_refs[i0:i0 + len(c.ins)], out_refs[o0:o0 + len(c.out_shapes)])
            i0, o0 = i0 + len(c.ins), o0 + len(c.out_shapes)
        return remote

    assert all(len(c.rounds) == 1 for c in carries)
    aliases, i0, o0 = {}, 0, 0
    for c in carries:
        aliases.update({i0 + ci: o0 + co for ci, co in c.aliases.items()})
        i0, o0 = i0 + len(c.ins), o0 + len(c.out_shapes)
    return _Carry(ins, outs, [plan], [sum(c.counts[0] for c in carries)], aliases=aliases)


def _add_halves(name, g, got, dim, core):
    R, C = g.shape
    if dim == 1:
        r, cc = R, C // (2 * N_CHIPS)
    else:
        r, cc = R // (2 * N_CHIPS), C
    tr, tc = _div_tile(r, 1024, 16), _div_tile(cc, 2048)
    nr, nc = r // tr, cc // tc

    def g_map(q, i, j, core_ref):
        w = 2 * q + core_ref[0]
        return (i, w * nc + j) if dim == 1 else (w * nr + i, j)

    def body(core_ref, g_ref, got_ref, o_ref):
        o_ref[...] = (g_ref[...].astype(F32) + got_ref[...].astype(F32)).astype(o_ref.dtype)

    slab = pl.BlockSpec((None, tr, tc), lambda q, i, j, core_ref: (q, i, j))
    return pl.pallas_call(
        body,
        grid_spec=pltpu.PrefetchScalarGridSpec(
            num_scalar_prefetch=1, grid=(N_CHIPS, nr, nc),
            in_specs=[pl.BlockSpec((tr, tc), g_map), slab], out_specs=slab),
        out_shape=jax.ShapeDtypeStruct((N_CHIPS, r, cc), g.dtype),
        compiler_params=_cparams(("parallel", "parallel", "parallel")),
        name=name,
    )(core, g, got)


def _reduce_into(name, sums, land, acc, layer, dim, shape, where):
    _, r, cc = sums.shape
    tr, tc = _div_tile(r, 512, 16), _div_tile(cc, 1024)
    nr, nc = r // tr, cc // tc

    def out_map(i, j, s):
        return (layer, s[1] * nr + i, j) if dim == 1 else (layer, i, s[1] * nc + j)

    def body(*refs):
        own, a_ref, b_ref, c_ref, o_ref = refs[1], refs[2], refs[3], refs[4], refs[-1]
        o_ref[...] = ((own[...].astype(F32) + a_ref[...].astype(F32)) + b_ref[...].astype(F32)) \
            + c_ref[...].astype(F32)

    def slot(p):
        return pl.BlockSpec((None, tr, tc), lambda i, j, s: (p, i, j))

    in_specs = [pl.BlockSpec((None, tr, tc), lambda i, j, s: (s[0], i, j)), slot(0), slot(1), slot(2)]
    args = [where, sums, land, land, land]
    if acc is not None:
        in_specs.append(ANY)
        args.append(acc)
    return pl.pallas_call(
        body,
        grid_spec=pltpu.PrefetchScalarGridSpec(
            num_scalar_prefetch=1, grid=(nr, nc), in_specs=in_specs,
            out_specs=pl.BlockSpec((None, tr, tc), out_map)),
        out_shape=jax.ShapeDtypeStruct(shape, F32),
        input_output_aliases={5: 0} if acc is not None else {},
        compiler_params=_cparams(("parallel", "parallel")),
        name=name,
    )(*args)


def _join_carry(grads, dims, layer):
    n = len(grads)

    def plan(ins, outs):
        (x, y, c), _, _, _ = _place()
        remote = []
        for w in range(n):
            d, h = dims[w], grads[w].shape[dims[w]] // 2

            def win(half, w=w, d=d, h=h):
                idx = [slice(None)] * len(grads[w].shape)
                idx[0], idx[d] = layer, pl.ds(half * h, h)
                return outs[w].at[tuple(idx)]

            remote.append((win(c), win(c), (x, y, 1 - c), win(1 - c)))
        return remote

    return _Carry(grads, [jax.ShapeDtypeStruct(g.shape, g.dtype) for g in grads], [plan], [n],
                  aliases={w: w for w in range(n)})


def _allreduce_small(pack):
    R, C = pack.shape

    def gather_body(in_ref, slots_ref, send_sems, recv_sems, local_sem):
        x, y, c = lax.axis_index("x"), lax.axis_index("y"), lax.axis_index("c")
        me = 4 * x + 2 * y + c
        flips = [(dx, dy, dc) for dx in (0, 1) for dy in (0, 1) for dc in (0, 1)][1:]

        def flip(v, d):
            return 1 - v if d else v

        local = pltpu.make_async_copy(in_ref, slots_ref.at[me], local_sem)
        local.start()
        sends = []
        for j, (dx, dy, dc) in enumerate(flips):
            px, py, pc = flip(x, dx), flip(y, dy), flip(c, dc)
            sends.append((pltpu.make_async_remote_copy(
                src_ref=in_ref, dst_ref=slots_ref.at[me], send_sem=send_sems.at[j], recv_sem=recv_sems.at[j],
                device_id=(px, py, pc), device_id_type=MESH), 4 * px + 2 * py + pc, j))
        for cp, _, _ in sends:
            cp.start()
        for cp, peer_id, j in sends:
            pltpu.make_async_remote_copy(
                src_ref=in_ref, dst_ref=slots_ref.at[peer_id], send_sem=send_sems.at[j], recv_sem=recv_sems.at[j],
                device_id=(x, y, c), device_id_type=MESH).wait_recv()
        for cp, _, _ in sends:
            cp.wait_send()
        local.wait()

    slots = pl.pallas_call(
        gather_body,
        in_specs=[ANY],
        out_specs=ANY,
        out_shape=jax.ShapeDtypeStruct((N_DEV, R, C), pack.dtype),
        scratch_shapes=[pltpu.SemaphoreType.DMA((N_DEV - 1,)), pltpu.SemaphoreType.DMA((N_DEV - 1,)),
                        pltpu.SemaphoreType.DMA],
        name="allgather_small",
    )(pack)

    def sum_body(s_ref, o_ref):
        acc = s_ref[0]
        for d in range(1, N_DEV):
            acc = acc + s_ref[d]
        o_ref[...] = acc

    return pl.pallas_call(
        sum_body,
        out_shape=jax.ShapeDtypeStruct((R, C), pack.dtype),
        name="sum_small",
    )(slots)


def _adamw_math(w, g, m, v):
    m2 = ADAM_B1 * m + (1.0 - ADAM_B1) * g
    v2 = ADAM_B2 * v + (1.0 - ADAM_B2) * (g * g)
    m_hat = m2 / (1.0 - ADAM_B1 ** ADAM_STEP)
    v_hat = v2 / (1.0 - ADAM_B2 ** ADAM_STEP)
    delta = -ADAM_LR * (m_hat / (jnp.sqrt(v_hat) + ADAM_EPS) + ADAM_WD * w)
    return delta, m2, v2


def _adamw_big(name, w, m, v, g_parts):
    shape = w.shape
    C = shape[-1]
    R = w.size // C
    tr, tc = _div_tile(R, 256, 8), _div_tile(C, 1024)
    n_g = len(g_parts)

    def body(*refs):
        w_ref, m_ref, v_ref = refs[:3]
        g_ref, d_ref, nm_ref, nv_ref = refs[3 + n_g:]
        g = refs[3][...]
        for extra in refs[4:3 + n_g]:
            g = g + extra[...]
        delta, m2, v2 = _adamw_math(w_ref[...], g, m_ref[...], v_ref[...])
        g_ref[...], d_ref[...], nm_ref[...], nv_ref[...] = g, delta, m2, v2

    blk = pl.BlockSpec((tr, tc), lambda i, j: (i, j))
    outs = pl.pallas_call(
        body,
        grid=(R // tr, C // tc),
        in_specs=[blk] * (3 + n_g),
        out_specs=[blk] * 4,
        out_shape=[jax.ShapeDtypeStruct((R, C), F32)] * 4,
        compiler_params=_cparams(("parallel", "parallel")),
        name=name,
    )(w.reshape(R, C), m.reshape(R, C), v.reshape(R, C), *[g.reshape(R, C) for g in g_parts])
    return [o.reshape(shape) for o in outs]


def _adamw_small(name, w, g, m, v):
    shape = w.shape
    two_d = (w.size // shape[-1], shape[-1])

    def body(w_ref, g_ref, m_ref, v_ref, d_ref, nm_ref, nv_ref):
        d_ref[...], nm_ref[...], nv_ref[...] = _adamw_math(w_ref[...], g_ref[...], m_ref[...], v_ref[...])

    outs = pl.pallas_call(
        body,
        out_shape=[jax.ShapeDtypeStruct(two_d, F32)] * 3,
        name=name,
    )(w.reshape(two_d), g.reshape(two_d), m.reshape(two_d), v.reshape(two_d))
    return [o.reshape(shape) for o in outs]


SMALL_ROWS = 40
S_BIN, S_G1, S_G2 = 16, 24, 32
MATS = ("w_in", "w_pa", "w_pb", "w_o", "w_mlp1", "w_mlp2")
LRU = ("lru_wr", "lru_wi")
BIG_DIM = dict(w_in=2, w_pa=1, w_pb=1, w_o=1, w_mlp1=2, w_mlp2=1, lru_wr=2, lru_wi=2)
WEIGHTS = ("norm1_g", "w_in", "b_in", "conv_a_w", "conv_a_b", "lru_wr", "lru_br", "lru_wi", "lru_bi", "lru_lam",
           "conv_b_w", "w_pa", "w_pb", "w_o", "norm2_g", "w_mlp1", "w_mlp2", "final_g")


def _rows_at(a, r0, total):
    pad = [(0, 0)] * a.ndim
    pad[-2] = (r0, total - r0 - a.shape[-2])
    return jnp.pad(a, pad)


def kernel(x, norm1_g, w_in, b_in, conv_a_w, conv_a_b, lru_wr, lru_br, lru_wi, lru_bi, lru_lam, conv_b_w, w_pa, w_pb, w_o, norm2_g, w_mlp1, w_mlp2, final_g, loss_target, m_norm1_g, m_w_in, m_b_in, m_conv_a_w, m_conv_a_b, m_lru_wr, m_lru_br, m_lru_wi, m_lru_bi, m_lru_lam, m_conv_b_w, m_w_pa, m_w_pb, m_w_o, m_norm2_g, m_w_mlp1, m_w_mlp2, m_final_g, v_norm1_g, v_w_in, v_b_in, v_conv_a_w, v_conv_a_b, v_lru_wr, v_lru_br, v_lru_wi, v_lru_bi, v_lru_lam, v_conv_b_w, v_w_pa, v_w_pb, v_w_o, v_norm2_g, v_w_mlp1, v_w_mlp2, v_final_g):
    wts = dict(norm1_g=norm1_g, w_in=w_in, b_in=b_in, conv_a_w=conv_a_w, conv_a_b=conv_a_b, lru_wr=lru_wr,
               lru_br=lru_br, lru_wi=lru_wi, lru_bi=lru_bi, lru_lam=lru_lam, conv_b_w=conv_b_w, w_pa=w_pa,
               w_pb=w_pb, w_o=w_o, norm2_g=norm2_g, w_mlp1=w_mlp1, w_mlp2=w_mlp2, final_g=final_g)
    mom = dict(norm1_g=m_norm1_g, w_in=m_w_in, b_in=m_b_in, conv_a_w=m_conv_a_w, conv_a_b=m_conv_a_b,
               lru_wr=m_lru_wr, lru_br=m_lru_br, lru_wi=m_lru_wi, lru_bi=m_lru_bi, lru_lam=m_lru_lam,
               conv_b_w=m_conv_b_w, w_pa=m_w_pa, w_pb=m_w_pb, w_o=m_w_o, norm2_g=m_norm2_g, w_mlp1=m_w_mlp1,
               w_mlp2=m_w_mlp2, final_g=m_final_g)
    vel = dict(norm1_g=v_norm1_g, w_in=v_w_in, b_in=v_b_in, conv_a_w=v_conv_a_w, conv_a_b=v_conv_a_b,
               lru_wr=v_lru_wr, lru_br=v_lru_br, lru_wi=v_lru_wi, lru_bi=v_lru_bi, lru_lam=v_lru_lam,
               conv_b_w=v_conv_b_w, w_pa=v_w_pa, w_pb=v_w_pb, w_o=v_w_o, norm2_g=v_norm2_g, w_mlp1=v_w_mlp1,
               w_mlp2=v_w_mlp2, final_g=v_final_g)
    depth, D = norm1_g.shape
    nb, bw = lru_wr.shape[1], lru_wr.shape[3]
    chip = 2 * lax.axis_index("x") + lax.axis_index("y")

    small_parts = [conv_a_w.reshape(-1), conv_b_w.reshape(-1), lru_br.reshape(-1), lru_bi.reshape(-1)]
    small_len = sum(p.shape[0] for p in small_parts)
    small_rows = -(-small_len // 1024) * 8
    small = jnp.concatenate(small_parts + [jnp.zeros((small_rows * 128 - small_len,), F32)]).reshape(small_rows, 128)
    gathered = _gather_weights([wts[n].astype(BF16) for n in LRU], [BIG_DIM[n] for n in LRU], small)
    full = dict(zip(LRU, gathered[:-1]))
    items = [(n, l) for l in range(depth) for n in MATS]
    mat_dims = [BIG_DIM[n] - 1 for n, _ in items]
    where = jnp.stack([chip, lax.axis_index("c")]).astype(jnp.int32)
    placed = {(n, l): _place_shard(f"place_{n}_{l}", wts[n], l, BIG_DIM[n] - 1, where) for n, l in items}
    flat = gathered[-1].reshape(N_CHIPS, small_rows * 128)
    off = 0
    small_full = []
    for part, shard in zip(small_parts, (conv_a_w, conv_b_w, lru_br, lru_bi)):
        piece = flat[:, off:off + part.shape[0]].reshape((N_CHIPS,) + shard.shape)
        small_full.append(jnp.moveaxis(piece, 0, -2).reshape(shard.shape[:-1] + (N_CHIPS * shard.shape[-1],)))
        off += part.shape[0]
    caw_f, cbw_f, br_f, bi_f = small_full
    pch = (_rows_at(conv_a_b[:, None, :], R_CAB, R_ROWS) + _rows_at(br_f.reshape(depth, 1, D), R_BR, R_ROWS)
           + _rows_at(bi_f.reshape(depth, 1, D), R_BI, R_ROWS) + _rows_at(lru_lam[:, None, :], R_LAM, R_ROWS)
           + _rows_at(caw_f, R_CAW, R_ROWS) + _rows_at(cbw_f, R_CBW, R_ROWS))
    W = dict(b_in=b_in, pch=pch, wr=full["lru_wr"], wi=full["lru_wi"], g1=norm1_g, g2=norm2_g, gf=final_g,
             core=lax.axis_index("c").astype(jnp.int32).reshape(1), where=where,
             shard_shapes={n: wts[n].shape for n in MATS})

    loss_local, dx, grads, dgf, sums, landed, acc = _local_fwd_bwd(x[0], loss_target[0], W, placed)
    loss = lax.psum(loss_local, ("x", "y", "c"))

    key = dict(w_in="w_in", w_pa="w_pa", w_pb="w_pb", w_o="w_o", w_mlp1="w1", w_mlp2="w2", lru_wr="wr", lru_wi="wi")
    out_g, out_d, out_m, out_v = {}, {}, {}, {}
    per_layer = [[grads[l][key[n]].astype(BF16) for l in range(depth)] for n in LRU]
    land = _scatter_grads(per_layer, [BIG_DIM[n] - 1 for n in LRU])
    chip_sums = [_sum_slots(f"sum_slots_{n}", ld.reshape(N_CHIPS, -1, ld.shape[-1])) for n, ld in zip(LRU, land)]
    sib_sums = _swap_with_sibling(chip_sums)
    for n, mine, sib in zip(LRU, chip_sums, sib_sums):
        out_g[n], out_d[n], out_m[n], out_v[n] = _adamw_big(f"adamw_{n}", wts[n], mom[n], vel[n], [mine, sib])
    for n in MATS:
        acc[n] = _reduce_into(f"reduce_{n}_0", sums[n, 0], landed[n, 0], acc.get(n), 0, BIG_DIM[n], wts[n].shape, where)
    joined = _run_carry("join_halves", _join_carry([acc[n] for n in MATS], [BIG_DIM[n] for n in MATS], 0))
    for n, g in zip(MATS, joined):
        out_g[n], out_d[n], out_m[n], out_v[n] = _adamw_big(f"adamw_{n}", wts[n], mom[n], vel[n], [g])

    rows = []
    for l in range(depth):
        g = grads[l]
        rows.append(_rows_at(g["sm"], 0, SMALL_ROWS) + _rows_at(g["b_in"].reshape(7, D), S_BIN, SMALL_ROWS)
                    + _rows_at(g["g1"], S_G1, SMALL_ROWS) + _rows_at(g["g2"], S_G2, SMALL_ROWS))
    rows.append(_rows_at(dgf, 0, 8))
    tot = _allreduce_small(jnp.concatenate(rows, axis=0))
    per = tot[:depth * SMALL_ROWS].reshape(depth, SMALL_ROWS, D)

    def cols_of_chip(a, axis):
        size = a.shape[axis] // N_CHIPS
        return lax.dynamic_slice_in_dim(a, chip * size, size, axis=axis)

    small_g = dict(
        norm1_g=per[:, S_G1], b_in=per[:, S_BIN:S_BIN + 7].reshape(depth, 7 * D),
        conv_a_w=cols_of_chip(per[:, R_CAW:R_CAW + 4], 2), conv_a_b=per[:, R_CAB],
        lru_br=cols_of_chip(per[:, R_BR].reshape(depth, nb, bw), 2),
        lru_bi=cols_of_chip(per[:, R_BI].reshape(depth, nb, bw), 2), lru_lam=per[:, R_LAM],
        conv_b_w=cols_of_chip(per[:, R_CBW:R_CBW + 3], 2), norm2_g=per[:, S_G2],
        final_g=tot[depth * SMALL_ROWS])
    for n, g in small_g.items():
        out_g[n] = g
        out_d[n], out_m[n], out_v[n] = _adamw_small(f"adamw_{n}", wts[n], g, mom[n], vel[n])

    return (loss, dx[None], *[out_g[n] for n in WEIGHTS], *[out_d[n] for n in WEIGHTS],
            *[out_m[n] for n in WEIGHTS], *[out_v[n] for n in WEIGHTS])
```
